```python
import jax, jax.numpy as jnp
from jax import lax
import numpy as np

D_MODEL = 1024
BATCH = 8
SEQ = 8192
DEPTH = 2

N_A_LAYERS = DEPTH // 2
N_B_LAYERS = DEPTH - N_A_LAYERS
FOX_HEADS = 16
FOX_HEAD_DIM = D_MODEL // FOX_HEADS
FOX_WIDTH = FOX_HEADS * FOX_HEAD_DIM
FOX_IN_COLS = 3 * FOX_WIDTH + FOX_HEADS
MLA_HEADS = 8
QK_NOPE_DIM = 128
QK_ROPE_DIM = 64
V_HEAD_DIM = 128
Q_LORA_RANK = 384
KV_LORA_RANK = 256
ROPE_BASE = 10000.0
D_FF = 4 * D_MODEL
Q_BLOCK = 128
EPS = 1e-6

kernel_name = "yoco_fox_mla_hybrid"


def rms_norm(x, g):
    xf = x.astype(jnp.float32)
    y = xf * lax.rsqrt(jnp.mean(xf * xf, axis=-1, keepdims=True) + EPS)
    return (y * g.astype(jnp.float32)).astype(x.dtype)


def sq_relu_mlp(h, w_up, w_down):
    return jnp.square(jax.nn.relu(h @ w_up)) @ w_down


def rope_tables(seq_len, dim):
    inv = 1.0 / (ROPE_BASE ** (jnp.arange(0, dim, 2, dtype=jnp.float32) / dim))
    ang = jnp.arange(seq_len, dtype=jnp.float32)[:, None] * inv[None, :]
    return jnp.cos(ang), jnp.sin(ang)


def apply_rope(t, cos, sin):
    cos = cos.astype(t.dtype)
    sin = sin.astype(t.dtype)
    half = t.shape[-1] // 2
    t1, t2 = t[..., :half], t[..., half:]
    return jnp.concatenate([t1 * cos - t2 * sin, t2 * cos + t1 * sin], axis=-1)


def causal_block_attention(logits_fn, q_parts, v):
    B, S = v.shape[0], v.shape[1]
    nb = S // Q_BLOCK
    blocks = tuple(jnp.moveaxis(t.reshape((B, nb, Q_BLOCK) + t.shape[2:]), 1, 0) for t in q_parts)
    kpos = jnp.arange(S)

    def one_block(args):
        i, qb = args
        logits = logits_fn(qb)
        qpos = i * Q_BLOCK + jnp.arange(Q_BLOCK)
        logits = jnp.where(kpos[None, :] <= qpos[:, None], logits, -jnp.inf)
        p = jax.nn.softmax(logits, axis=-1).astype(v.dtype)
        return jnp.einsum('bhqk,bkhd->bqhd', p, v)

    out = lax.map(one_block, (jnp.arange(nb), blocks))
    return jnp.moveaxis(out, 0, 1).reshape((B, S) + out.shape[3:])


def fox_mixer(h, w_in, b_f, w_out):
    B, S, _ = h.shape
    proj = h @ w_in
    q = proj[..., :FOX_WIDTH].reshape(B, S, FOX_HEADS, FOX_HEAD_DIM)
    k = proj[..., FOX_WIDTH:2 * FOX_WIDTH].reshape(B, S, FOX_HEADS, FOX_HEAD_DIM)
    v = proj[..., 2 * FOX_WIDTH:3 * FOX_WIDTH].reshape(B, S, FOX_HEADS, FOX_HEAD_DIM)
    f_logit = proj[..., 3 * FOX_WIDTH:].astype(jnp.float32) + b_f.astype(jnp.float32)
    cum = jnp.cumsum(jax.nn.log_sigmoid(f_logit), axis=1)
    c_keys = jnp.transpose(cum, (0, 2, 1))
    scale = FOX_HEAD_DIM ** -0.5

    def logits_fn(qb):
        q_blk, c_blk = qb
        s = jnp.einsum('bqhd,bkhd->bhqk', q_blk, k, preferred_element_type=jnp.float32) * scale
        return s + jnp.transpose(c_blk, (0, 2, 1))[..., None] - c_keys[:, :, None, :]

    ctx = causal_block_attention(logits_fn, (q, cum), v)
    return ctx.reshape(B, S, FOX_WIDTH) @ w_out


def mla_shared_kv(stream, kv_norm_g, w_kv_a, kv_a_norm_g, w_kv_b, cos, sin):
    B, S, _ = stream.shape
    src = rms_norm(stream, kv_norm_g)
    kv_a = src @ w_kv_a
    c_kv = rms_norm(kv_a[..., :KV_LORA_RANK], kv_a_norm_g)
    k_rope = apply_rope(kv_a[..., KV_LORA_RANK:], cos, sin)
    kv_b = (c_kv @ w_kv_b).reshape(B, S, MLA_HEADS, QK_NOPE_DIM + V_HEAD_DIM)
    k_nope = kv_b[..., :QK_NOPE_DIM]
    v = kv_b[..., QK_NOPE_DIM:]
    return k_nope, k_rope, v


def mla_mixer(h, w_q_a, q_a_norm_g, w_q_b, w_out, k_nope, k_rope, v, cos, sin):
    B, S, _ = h.shape
    c_q = rms_norm(h @ w_q_a, q_a_norm_g)
    q = (c_q @ w_q_b).reshape(B, S, MLA_HEADS, QK_NOPE_DIM + QK_ROPE_DIM)
    q_nope = q[..., :QK_NOPE_DIM]
    q_rope = apply_rope(q[..., QK_NOPE_DIM:], cos[:, None, :], sin[:, None, :])
    scale = (QK_NOPE_DIM + QK_ROPE_DIM) ** -0.5

    def logits_fn(qb):
        qn, qr = qb
        s = jnp.einsum('bqhd,bkhd->bhqk', qn, k_nope, preferred_element_type=jnp.float32)
        s = s + jnp.einsum('bqhr,bkr->bhqk', qr, k_rope, preferred_element_type=jnp.float32)
        return s * scale

    ctx = causal_block_attention(logits_fn, (q_nope, q_rope), v)
    return ctx.reshape(B, S, MLA_HEADS * V_HEAD_DIM) @ w_out


def _fwd_setup_inputs(seed: int = 0) -> dict:
    key = jax.random.key(seed)
    ks = jax.random.split(key, 24)

    def w(k, shape, fan_in):
        return jax.random.normal(k, shape, jnp.float32) * (fan_in ** -0.5)

    def gain(k, shape):
        return 1.0 + 0.02 * jax.random.normal(k, shape, jnp.float32)

    return {
        "x": jax.random.normal(ks[0], (BATCH, SEQ, D_MODEL), jnp.float32),
        "norm_mix_g": gain(ks[1], (DEPTH, D_MODEL)),
        "norm_ffn_g": gain(ks[2], (DEPTH, D_MODEL)),
        "fox_w_in": w(ks[3], (N_A_LAYERS, D_MODEL, FOX_IN_COLS), D_MODEL),
        "fox_b_f": 1.0 + 0.1 * jax.random.normal(ks[4], (N_A_LAYERS, FOX_HEADS), jnp.float32),
        "fox_w_out": w(ks[5], (N_A_LAYERS, FOX_WIDTH, D_MODEL), FOX_WIDTH),
        "kv_norm_g": gain(ks[6], (D_MODEL,)),
        "mla_w_kv_a": w(ks[7], (D_MODEL, KV_LORA_RANK + QK_ROPE_DIM), D_MODEL),
        "mla_kv_a_norm_g": gain(ks[8], (KV_LORA_RANK,)),
        "mla_w_kv_b": w(ks[9], (KV_LORA_RANK, MLA_HEADS * (QK_NOPE_DIM + V_HEAD_DIM)), KV_LORA_RANK),
        "mla_w_q_a": w(ks[10], (N_B_LAYERS, D_MODEL, Q_LORA_RANK), D_MODEL),
        "mla_q_a_norm_g": gain(ks[11], (N_B_LAYERS, Q_LORA_RANK)),
        "mla_w_q_b": w(ks[12], (N_B_LAYERS, Q_LORA_RANK, MLA_HEADS * (QK_NOPE_DIM + QK_ROPE_DIM)), Q_LORA_RANK),
        "mla_w_out": w(ks[13], (N_B_LAYERS, MLA_HEADS * V_HEAD_DIM, D_MODEL), MLA_HEADS * V_HEAD_DIM),
        "ffn_w_up": w(ks[14], (DEPTH, D_MODEL, D_FF), D_MODEL),
        "ffn_w_down": w(ks[15], (DEPTH, D_FF, D_MODEL), D_FF),
        "final_norm_g": gain(ks[16], (D_MODEL,)),
    }


def _fwd_reference(x, norm_mix_g, norm_ffn_g, fox_w_in, fox_b_f, fox_w_out, kv_norm_g,
              mla_w_kv_a, mla_kv_a_norm_g, mla_w_kv_b, mla_w_q_a, mla_q_a_norm_g,
              mla_w_q_b, mla_w_out, ffn_w_up, ffn_w_down, final_norm_g):
    S = x.shape[1]
    cos, sin = rope_tables(S, QK_ROPE_DIM)
    k_nope = k_rope = v_shared = None
    for layer in range(DEPTH):
        h = rms_norm(x, norm_mix_g[layer])
        if layer < N_A_LAYERS:
            x = x + fox_mixer(h, fox_w_in[layer], fox_b_f[layer], fox_w_out[layer])
        else:
            b = layer - N_A_LAYERS
            x = x + mla_mixer(h, mla_w_q_a[b], mla_q_a_norm_g[b], mla_w_q_b[b], mla_w_out[b],
                              k_nope, k_rope, v_shared, cos, sin)
        x = x + sq_relu_mlp(rms_norm(x, norm_ffn_g[layer]), ffn_w_up[layer], ffn_w_down[layer])
        if layer == N_A_LAYERS - 1:
            k_nope, k_rope, v_shared = mla_shared_kv(x, kv_norm_g, mla_w_kv_a, mla_kv_a_norm_g,
                                                     mla_w_kv_b, cos, sin)
    return rms_norm(x, final_norm_g)


import jax as _jax
import jax.numpy as _jnp

TWIN_FORMAT = 'train_step'
FWD_PARAMS = ['x', 'norm_mix_g', 'norm_ffn_g', 'fox_w_in', 'fox_b_f', 'fox_w_out', 'kv_norm_g', 'mla_w_kv_a', 'mla_kv_a_norm_g', 'mla_w_kv_b', 'mla_w_q_a', 'mla_q_a_norm_g', 'mla_w_q_b', 'mla_w_out', 'ffn_w_up', 'ffn_w_down', 'final_norm_g']
TWIN_WEIGHTS = ['norm_mix_g', 'norm_ffn_g', 'fox_w_in', 'fox_b_f', 'fox_w_out', 'kv_norm_g', 'mla_w_kv_a', 'mla_kv_a_norm_g', 'mla_w_kv_b', 'mla_w_q_a', 'mla_q_a_norm_g', 'mla_w_q_b', 'mla_w_out', 'ffn_w_up', 'ffn_w_down', 'final_norm_g']
TWIN_DIFF_INPUT = 'x'
TWIN_INPUTS = ['x', 'norm_mix_g', 'norm_ffn_g', 'fox_w_in', 'fox_b_f', 'fox_w_out', 'kv_norm_g', 'mla_w_kv_a', 'mla_kv_a_norm_g', 'mla_w_kv_b', 'mla_w_q_a', 'mla_q_a_norm_g', 'mla_w_q_b', 'mla_w_out', 'ffn_w_up', 'ffn_w_down', 'final_norm_g', 'loss_target', 'm_norm_mix_g', 'm_norm_ffn_g', 'm_fox_w_in', 'm_fox_b_f', 'm_fox_w_out', 'm_kv_norm_g', 'm_mla_w_kv_a', 'm_mla_kv_a_norm_g', 'm_mla_w_kv_b', 'm_mla_w_q_a', 'm_mla_q_a_norm_g', 'm_mla_w_q_b', 'm_mla_w_out', 'm_ffn_w_up', 'm_ffn_w_down', 'm_final_norm_g', 'v_norm_mix_g', 'v_norm_ffn_g', 'v_fox_w_in', 'v_fox_b_f', 'v_fox_w_out', 'v_kv_norm_g', 'v_mla_w_kv_a', 'v_mla_kv_a_norm_g', 'v_mla_w_kv_b', 'v_mla_w_q_a', 'v_mla_q_a_norm_g', 'v_mla_w_q_b', 'v_mla_w_out', 'v_ffn_w_up', 'v_ffn_w_down', 'v_final_norm_g']
TWIN_OUTPUTS = ['loss', 'grad_x', 'grad_norm_mix_g', 'grad_norm_ffn_g', 'grad_fox_w_in', 'grad_fox_b_f', 'grad_fox_w_out', 'grad_kv_norm_g', 'grad_mla_w_kv_a', 'grad_mla_kv_a_norm_g', 'grad_mla_w_kv_b', 'grad_mla_w_q_a', 'grad_mla_q_a_norm_g', 'grad_mla_w_q_b', 'grad_mla_w_out', 'grad_ffn_w_up', 'grad_ffn_w_down', 'grad_final_norm_g', 'delta_norm_mix_g', 'delta_norm_ffn_g', 'delta_fox_w_in', 'delta_fox_b_f', 'delta_fox_w_out', 'delta_kv_norm_g', 'delta_mla_w_kv_a', 'delta_mla_kv_a_norm_g', 'delta_mla_w_kv_b', 'delta_mla_w_q_a', 'delta_mla_q_a_norm_g', 'delta_mla_w_q_b', 'delta_mla_w_out', 'delta_ffn_w_up', 'delta_ffn_w_down', 'delta_final_norm_g', 'new_m_norm_mix_g', 'new_m_norm_ffn_g', 'new_m_fox_w_in', 'new_m_fox_b_f', 'new_m_fox_w_out', 'new_m_kv_norm_g', 'new_m_mla_w_kv_a', 'new_m_mla_kv_a_norm_g', 'new_m_mla_w_kv_b', 'new_m_mla_w_q_a', 'new_m_mla_q_a_norm_g', 'new_m_mla_w_q_b', 'new_m_mla_w_out', 'new_m_ffn_w_up', 'new_m_ffn_w_down', 'new_m_final_norm_g', 'new_v_norm_mix_g', 'new_v_norm_ffn_g', 'new_v_fox_w_in', 'new_v_fox_b_f', 'new_v_fox_w_out', 'new_v_kv_norm_g', 'new_v_mla_w_kv_a', 'new_v_mla_kv_a_norm_g', 'new_v_mla_w_kv_b', 'new_v_mla_w_q_a', 'new_v_mla_q_a_norm_g', 'new_v_mla_w_q_b', 'new_v_mla_w_out', 'new_v_ffn_w_up', 'new_v_ffn_w_down', 'new_v_final_norm_g']
TWIN_LEAF_KINDS = {'loss': 'loss', 'grad_x': 'grad_x', 'grad_norm_mix_g': 'grad_w', 'grad_norm_ffn_g': 'grad_w', 'grad_fox_w_in': 'grad_w', 'grad_fox_b_f': 'grad_w', 'grad_fox_w_out': 'grad_w', 'grad_kv_norm_g': 'grad_w', 'grad_mla_w_kv_a': 'grad_w', 'grad_mla_kv_a_norm_g': 'grad_w', 'grad_mla_w_kv_b': 'grad_w', 'grad_mla_w_q_a': 'grad_w', 'grad_mla_q_a_norm_g': 'grad_w', 'grad_mla_w_q_b': 'grad_w', 'grad_mla_w_out': 'grad_w', 'grad_ffn_w_up': 'grad_w', 'grad_ffn_w_down': 'grad_w', 'grad_final_norm_g': 'grad_w', 'delta_norm_mix_g': 'delta_w', 'delta_norm_ffn_g': 'delta_w', 'delta_fox_w_in': 'delta_w', 'delta_fox_b_f': 'delta_w', 'delta_fox_w_out': 'delta_w', 'delta_kv_norm_g': 'delta_w', 'delta_mla_w_kv_a': 'delta_w', 'delta_mla_kv_a_norm_g': 'delta_w', 'delta_mla_w_kv_b': 'delta_w', 'delta_mla_w_q_a': 'delta_w', 'delta_mla_q_a_norm_g': 'delta_w', 'delta_mla_w_q_b': 'delta_w', 'delta_mla_w_out': 'delta_w', 'delta_ffn_w_up': 'delta_w', 'delta_ffn_w_down': 'delta_w', 'delta_final_norm_g': 'delta_w', 'new_m_norm_mix_g': 'new_m', 'new_m_norm_ffn_g': 'new_m', 'new_m_fox_w_in': 'new_m', 'new_m_fox_b_f': 'new_m', 'new_m_fox_w_out': 'new_m', 'new_m_kv_norm_g': 'new_m', 'new_m_mla_w_kv_a': 'new_m', 'new_m_mla_kv_a_norm_g': 'new_m', 'new_m_mla_w_kv_b': 'new_m', 'new_m_mla_w_q_a': 'new_m', 'new_m_mla_q_a_norm_g': 'new_m', 'new_m_mla_w_q_b': 'new_m', 'new_m_mla_w_out': 'new_m', 'new_m_ffn_w_up': 'new_m', 'new_m_ffn_w_down': 'new_m', 'new_m_final_norm_g': 'new_m', 'new_v_norm_mix_g': 'new_v', 'new_v_norm_ffn_g': 'new_v', 'new_v_fox_w_in': 'new_v', 'new_v_fox_b_f': 'new_v', 'new_v_fox_w_out': 'new_v', 'new_v_kv_norm_g': 'new_v', 'new_v_mla_w_kv_a': 'new_v', 'new_v_mla_kv_a_norm_g': 'new_v', 'new_v_mla_w_kv_b': 'new_v', 'new_v_mla_w_q_a': 'new_v', 'new_v_mla_q_a_norm_g': 'new_v', 'new_v_mla_w_q_b': 'new_v', 'new_v_mla_w_out': 'new_v', 'new_v_ffn_w_up': 'new_v', 'new_v_ffn_w_down': 'new_v', 'new_v_final_norm_g': 'new_v'}


def _forward(args):
    return _fwd_reference(*[args[k] for k in FWD_PARAMS])


def _output_shape():
    def fwd():
        inp = _fwd_setup_inputs(0)
        return _fwd_reference(*[inp[k] for k in FWD_PARAMS])
    out = _jax.eval_shape(fwd)
    return out.shape, out.dtype

N_MICROBATCH = 1
ADAM_LR = 0.001
ADAM_B1 = 0.9
ADAM_B2 = 0.999
ADAM_EPS = 1e-08
ADAM_WD = 0.01
ADAM_STEP = 10
PER_EXAMPLE_BATCH_AXIS = {'x': 0, 'loss_target': 0}
SHARED_INPUTS = []
_WEIGHT_DTYPES = {'norm_mix_g': _jnp.float32, 'norm_ffn_g': _jnp.float32, 'fox_w_in': _jnp.float32, 'fox_b_f': _jnp.float32, 'fox_w_out': _jnp.float32, 'kv_norm_g': _jnp.float32, 'mla_w_kv_a': _jnp.float32, 'mla_kv_a_norm_g': _jnp.float32, 'mla_w_kv_b': _jnp.float32, 'mla_w_q_a': _jnp.float32, 'mla_q_a_norm_g': _jnp.float32, 'mla_w_q_b': _jnp.float32, 'mla_w_out': _jnp.float32, 'ffn_w_up': _jnp.float32, 'ffn_w_down': _jnp.float32, 'final_norm_g': _jnp.float32}
MOMENT_SCALE = {'norm_mix_g': 1.332826e-01, 'norm_ffn_g': 2.102518e-01, 'fox_w_in': 1.090281e-01, 'fox_b_f': 6.320335e-01, 'fox_w_out': 1.370262e-01, 'kv_norm_g': 1.029102e-01, 'mla_w_kv_a': 1.654937e-01, 'mla_kv_a_norm_g': 1.818086e-01, 'mla_w_kv_b': 7.010298e-02, 'mla_w_q_a': 3.970884e-02, 'mla_q_a_norm_g': 3.673377e-02, 'mla_w_q_b': 1.996564e-02, 'mla_w_out': 1.027097e-01, 'ffn_w_up': 1.025460e-01, 'ffn_w_down': 2.439615e-01, 'final_norm_g': 6.562862e+01}


def _to_microbatches(a, axis):
    t = _jnp.moveaxis(a, axis, 0)
    t = t.reshape((N_MICROBATCH, t.shape[0] // N_MICROBATCH) + t.shape[1:])
    return _jnp.moveaxis(t, 1, axis + 1)


def setup_inputs(seed: int = 0) -> dict:
    inp = _fwd_setup_inputs(seed)
    key = _jax.random.fold_in(_jax.random.key(seed), 7919)
    shape, _ = _output_shape()
    out = dict(inp)
    out["loss_target"] = _jax.random.normal(_jax.random.fold_in(key, 0), shape, _jnp.float32)
    for i, name in enumerate(TWIN_WEIGHTS):
        w = inp[name].astype(_jnp.float32)
        if MOMENT_SCALE is None:
            s = _jnp.sqrt(_jnp.mean(_jnp.square(w)) + 1e-30)
        else:
            s = MOMENT_SCALE[name]
        km, kv = _jax.random.split(_jax.random.fold_in(key, i + 1))
        out[name] = w
        out["m_" + name] = s * _jax.random.normal(km, w.shape, _jnp.float32)
        out["v_" + name] = (s * s) * _jax.random.uniform(kv, w.shape, _jnp.float32, 0.5, 1.5)
    if N_MICROBATCH > 1:
        for name, axis in PER_EXAMPLE_BATCH_AXIS.items():
            out[name] = _to_microbatches(out[name], axis)
    return {'x': out['x'], 'norm_mix_g': out['norm_mix_g'], 'norm_ffn_g': out['norm_ffn_g'], 'fox_w_in': out['fox_w_in'], 'fox_b_f': out['fox_b_f'], 'fox_w_out': out['fox_w_out'], 'kv_norm_g': out['kv_norm_g'], 'mla_w_kv_a': out['mla_w_kv_a'], 'mla_kv_a_norm_g': out['mla_kv_a_norm_g'], 'mla_w_kv_b': out['mla_w_kv_b'], 'mla_w_q_a': out['mla_w_q_a'], 'mla_q_a_norm_g': out['mla_q_a_norm_g'], 'mla_w_q_b': out['mla_w_q_b'], 'mla_w_out': out['mla_w_out'], 'ffn_w_up': out['ffn_w_up'], 'ffn_w_down': out['ffn_w_down'], 'final_norm_g': out['final_norm_g'], 'loss_target': out['loss_target'], 'm_norm_mix_g': out['m_norm_mix_g'], 'm_norm_ffn_g': out['m_norm_ffn_g'], 'm_fox_w_in': out['m_fox_w_in'], 'm_fox_b_f': out['m_fox_b_f'], 'm_fox_w_out': out['m_fox_w_out'], 'm_kv_norm_g': out['m_kv_norm_g'], 'm_mla_w_kv_a': out['m_mla_w_kv_a'], 'm_mla_kv_a_norm_g': out['m_mla_kv_a_norm_g'], 'm_mla_w_kv_b': out['m_mla_w_kv_b'], 'm_mla_w_q_a': out['m_mla_w_q_a'], 'm_mla_q_a_norm_g': out['m_mla_q_a_norm_g'], 'm_mla_w_q_b': out['m_mla_w_q_b'], 'm_mla_w_out': out['m_mla_w_out'], 'm_ffn_w_up': out['m_ffn_w_up'], 'm_ffn_w_down': out['m_ffn_w_down'], 'm_final_norm_g': out['m_final_norm_g'], 'v_norm_mix_g': out['v_norm_mix_g'], 'v_norm_ffn_g': out['v_norm_ffn_g'], 'v_fox_w_in': out['v_fox_w_in'], 'v_fox_b_f': out['v_fox_b_f'], 'v_fox_w_out': out['v_fox_w_out'], 'v_kv_norm_g': out['v_kv_norm_g'], 'v_mla_w_kv_a': out['v_mla_w_kv_a'], 'v_mla_kv_a_norm_g': out['v_mla_kv_a_norm_g'], 'v_mla_w_kv_b': out['v_mla_w_kv_b'], 'v_mla_w_q_a': out['v_mla_w_q_a'], 'v_mla_q_a_norm_g': out['v_mla_q_a_norm_g'], 'v_mla_w_q_b': out['v_mla_w_q_b'], 'v_mla_w_out': out['v_mla_w_out'], 'v_ffn_w_up': out['v_ffn_w_up'], 'v_ffn_w_down': out['v_ffn_w_down'], 'v_final_norm_g': out['v_final_norm_g']}


def _loss(weights, diff, rest, loss_target):
    with _jax.named_scope("forward"):
        args = {**rest, TWIN_DIFF_INPUT: diff, **{k: w.astype(_WEIGHT_DTYPES[k]) for k, w in weights.items()}}
        y = _forward(args)
    with _jax.named_scope("loss_head"):
        err = _jnp.square(y.astype(_jnp.float32) - loss_target)
        return 0.5 * _jnp.sum(_jnp.mean(err, axis=-1)) if err.ndim else 0.5 * err


def _adamw(w, g, m, v):
    m = ADAM_B1 * m + (1.0 - ADAM_B1) * g
    v = ADAM_B2 * v + (1.0 - ADAM_B2) * _jnp.square(g)
    m_hat = m / (1.0 - ADAM_B1 ** ADAM_STEP)
    v_hat = v / (1.0 - ADAM_B2 ** ADAM_STEP)
    delta = -ADAM_LR * (m_hat / (_jnp.sqrt(v_hat) + ADAM_EPS) + ADAM_WD * w)
    return delta, m, v


def reference(x, norm_mix_g, norm_ffn_g, fox_w_in, fox_b_f, fox_w_out, kv_norm_g, mla_w_kv_a, mla_kv_a_norm_g, mla_w_kv_b, mla_w_q_a, mla_q_a_norm_g, mla_w_q_b, mla_w_out, ffn_w_up, ffn_w_down, final_norm_g, loss_target, m_norm_mix_g, m_norm_ffn_g, m_fox_w_in, m_fox_b_f, m_fox_w_out, m_kv_norm_g, m_mla_w_kv_a, m_mla_kv_a_norm_g, m_mla_w_kv_b, m_mla_w_q_a, m_mla_q_a_norm_g, m_mla_w_q_b, m_mla_w_out, m_ffn_w_up, m_ffn_w_down, m_final_norm_g, v_norm_mix_g, v_norm_ffn_g, v_fox_w_in, v_fox_b_f, v_fox_w_out, v_kv_norm_g, v_mla_w_kv_a, v_mla_kv_a_norm_g, v_mla_w_kv_b, v_mla_w_q_a, v_mla_q_a_norm_g, v_mla_w_q_b, v_mla_w_out, v_ffn_w_up, v_ffn_w_down, v_final_norm_g):
    given = dict(x=x, norm_mix_g=norm_mix_g, norm_ffn_g=norm_ffn_g, fox_w_in=fox_w_in, fox_b_f=fox_b_f, fox_w_out=fox_w_out, kv_norm_g=kv_norm_g, mla_w_kv_a=mla_w_kv_a, mla_kv_a_norm_g=mla_kv_a_norm_g, mla_w_kv_b=mla_w_kv_b, mla_w_q_a=mla_w_q_a, mla_q_a_norm_g=mla_q_a_norm_g, mla_w_q_b=mla_w_q_b, mla_w_out=mla_w_out, ffn_w_up=ffn_w_up, ffn_w_down=ffn_w_down, final_norm_g=final_norm_g, loss_target=loss_target, m_norm_mix_g=m_norm_mix_g, m_norm_ffn_g=m_norm_ffn_g, m_fox_w_in=m_fox_w_in, m_fox_b_f=m_fox_b_f, m_fox_w_out=m_fox_w_out, m_kv_norm_g=m_kv_norm_g, m_mla_w_kv_a=m_mla_w_kv_a, m_mla_kv_a_norm_g=m_mla_kv_a_norm_g, m_mla_w_kv_b=m_mla_w_kv_b, m_mla_w_q_a=m_mla_w_q_a, m_mla_q_a_norm_g=m_mla_q_a_norm_g, m_mla_w_q_b=m_mla_w_q_b, m_mla_w_out=m_mla_w_out, m_ffn_w_up=m_ffn_w_up, m_ffn_w_down=m_ffn_w_down, m_final_norm_g=m_final_norm_g, v_norm_mix_g=v_norm_mix_g, v_norm_ffn_g=v_norm_ffn_g, v_fox_w_in=v_fox_w_in, v_fox_b_f=v_fox_b_f, v_fox_w_out=v_fox_w_out, v_kv_norm_g=v_kv_norm_g, v_mla_w_kv_a=v_mla_w_kv_a, v_mla_kv_a_norm_g=v_mla_kv_a_norm_g, v_mla_w_kv_b=v_mla_w_kv_b, v_mla_w_q_a=v_mla_w_q_a, v_mla_q_a_norm_g=v_mla_q_a_norm_g, v_mla_w_q_b=v_mla_w_q_b, v_mla_w_out=v_mla_w_out, v_ffn_w_up=v_ffn_w_up, v_ffn_w_down=v_ffn_w_down, v_final_norm_g=v_final_norm_g)
    weights = {n: given[n] for n in TWIN_WEIGHTS}
    shared = {n: given[n] for n in SHARED_INPUTS}
    per_example = {n: given[n] for n in ['x']}
    grad_fn = _jax.value_and_grad(_loss, argnums=(0, 1))

    def one_microbatch(ex, loss_target):
        ex = dict(ex)
        diff = ex.pop(TWIN_DIFF_INPUT)
        return grad_fn(weights, diff, {**shared, **ex}, loss_target)

    if N_MICROBATCH == 1:
        loss, (grad_w, grad_x) = one_microbatch(per_example, given["loss_target"])
    else:
        def body(carry, xs):
            loss_sum, grad_sum = carry
            l_k, (gw_k, gx_k) = one_microbatch(xs[0], xs[1])
            with _jax.named_scope("update"):
                return (loss_sum + l_k, _jax.tree.map(_jnp.add, grad_sum, gw_k)), gx_k

        init = (_jnp.zeros((), _jnp.float32), _jax.tree.map(_jnp.zeros_like, weights))
        (loss, grad_w), grad_x = _jax.lax.scan(body, init, (per_example, given["loss_target"]))
    with _jax.named_scope("update"):
        delta_w, new_m, new_v = {}, {}, {}
        for n in TWIN_WEIGHTS:
            delta_w[n], new_m[n], new_v[n] = _adamw(weights[n], grad_w[n], given["m_" + n], given["v_" + n])
    return (loss, grad_x, *[grad_w[n] for n in TWIN_WEIGHTS], *[delta_w[n] for n in TWIN_WEIGHTS],
            *[new_m[n] for n in TWIN_WEIGHTS], *[new_v[n] for n in TWIN_WEIGHTS])
```

```python
import functools
import math

import jax
import jax.numpy as jnp
from jax import lax
from jax.experimental import pallas as pl
from jax.experimental.pallas import tpu as pltpu

F32 = jnp.float32
BF16 = jnp.bfloat16
MESH = pl.DeviceIdType.MESH

N_DEV = 8
D_MODEL = 1024
FOX_HEADS = 16
FOX_HEAD_DIM = 64
MLA_HEADS = 8
QK_NOPE = 128
QK_ROPE = 64
V_HEAD = 128
Q_LORA = 384
KV_LORA = 256
KV_A_PAD = 384
D_FF = 4096
ROPE_BASE = 10000.0
EPS = 1e-6
NEG = -1e30

ADAM_LR = 0.001
ADAM_B1 = 0.9
ADAM_B2 = 0.999
ADAM_EPS = 1e-08
ADAM_WD = 0.01
ADAM_STEP = 10

VMEM_LIMIT_BYTES = 48 * 1024 * 1024

NN = (((1,), (0,)), ((), ()))
NT = (((1,), (1,)), ((), ()))
TN = (((0,), (0,)), ((), ()))
_FORMS = {"nn": NN, "nt": NT}


def _cparams(sem=None):
    return pltpu.CompilerParams(dimension_semantics=sem, vmem_limit_bytes=VMEM_LIMIT_BYTES)


def _pick(n, cands):
    for c in cands:
        if c <= n and n % c == 0:
            return c
    return n


def _dot(a, b, dims):
    return lax.dot_general(a, b, dims, preferred_element_type=F32)


def _mm(name, a, b, form, out_dtypes, epi=None, extras=(), tm=1024, tn=512):
    M, K = a.shape
    N = b.shape[1] if form == "nn" else b.shape[0]
    tm = _pick(M, (tm, 512, 256, 128))
    tn = _pick(N, (tn, 384, 256, 128))
    n_ex = len(extras)

    def body(*refs):
        a_ref, b_ref = refs[0], refs[1]
        ex = refs[2:2 + n_ex]
        outs = refs[2 + n_ex:]
        acc = _dot(a_ref[...].astype(BF16), b_ref[...].astype(BF16), _FORMS[form])
        res = epi(acc, *[e[...] for e in ex]) if epi is not None else (acc,)
        for o_ref, r in zip(outs, res):
            o_ref[...] = r.astype(o_ref.dtype)

    if form == "nn":
        b_spec = pl.BlockSpec((K, tn), lambda i, j: (0, j))
    else:
        b_spec = pl.BlockSpec((tn, K), lambda i, j: (j, 0))
    tile = pl.BlockSpec((tm, tn), lambda i, j: (i, j))
    out = pl.pallas_call(
        body, name=name, grid=(M // tm, N // tn),
        in_specs=[pl.BlockSpec((tm, K), lambda i, j: (i, 0)), b_spec] + [tile] * n_ex,
        out_specs=[tile] * len(out_dtypes),
        out_shape=[jax.ShapeDtypeStruct((M, N), dt) for dt in out_dtypes],
        compiler_params=_cparams(("parallel", "arbitrary")),
    )(a, b, *extras)
    return out if len(out_dtypes) > 1 else out[0]


def _mm_tn(name, a, b):
    T, Ka = a.shape
    N = b.shape[1]
    tk = _pick(Ka, (1024, 512, 384, 256, 128))
    tn = _pick(N, (1024, 768, 512, 384, 256, 128))
    tt = _pick(T, (1024, 512, 256, 128))

    def body(a_ref, b_ref, o_ref):
        @pl.when(pl.program_id(2) == 0)
        def _():
            o_ref[...] = jnp.zeros_like(o_ref)

        o_ref[...] += _dot(a_ref[...].astype(BF16), b_ref[...].astype(BF16), TN)

    return pl.pallas_call(
        body, name=name, grid=(Ka // tk, N // tn, T // tt),
        in_specs=[pl.BlockSpec((tt, tk), lambda i, j, t: (t, i)),
                  pl.BlockSpec((tt, tn), lambda i, j, t: (t, j))],
        out_specs=pl.BlockSpec((tk, tn), lambda i, j, t: (i, j)),
        out_shape=jax.ShapeDtypeStruct((Ka, N), F32),
        compiler_params=_cparams(("parallel", "parallel", "arbitrary")),
    )(a, b)


def _rms(name, x, g, out_dtype):
    T, D = x.shape
    tm = _pick(T, (1024, 512, 256, 128))

    def body(x_ref, g_ref, o_ref):
        xf = x_ref[...]
        r = lax.rsqrt(jnp.mean(xf * xf, axis=-1, keepdims=True) + EPS)
        o_ref[...] = (xf * r * g_ref[...]).astype(o_ref.dtype)

    return pl.pallas_call(
        body, name=name, grid=(T // tm,),
        in_specs=[pl.BlockSpec((tm, D), lambda i: (i, 0)), pl.BlockSpec((1, D), lambda i: (0, 0))],
        out_specs=pl.BlockSpec((tm, D), lambda i: (i, 0)),
        out_shape=jax.ShapeDtypeStruct((T, D), out_dtype),
        compiler_params=_cparams(("parallel",)),
    )(x, g)


def _rms_bwd(name, x, g, dh, dres=None):
    T, D = x.shape
    tm = _pick(T, (512, 256, 128))
    has_res = dres is not None

    def body(*refs):
        if has_res:
            x_ref, g_ref, dh_ref, dres_ref, dx_ref, dg_ref = refs
        else:
            x_ref, g_ref, dh_ref, dx_ref, dg_ref = refs

        @pl.when(pl.program_id(0) == 0)
        def _():
            dg_ref[...] = jnp.zeros_like(dg_ref)

        xf = x_ref[...]
        r = lax.rsqrt(jnp.mean(xf * xf, axis=-1, keepdims=True) + EPS)
        xhat = xf * r
        dy = dh_ref[...].astype(F32)
        dxh = dy * g_ref[...]
        dx = r * (dxh - xhat * jnp.mean(dxh * xhat, axis=-1, keepdims=True))
        if has_res:
            dx = dx + dres_ref[...]
        dx_ref[...] = dx
        dg_ref[...] += jnp.sum(dy * xhat, axis=0, keepdims=True)

    row = pl.BlockSpec((tm, D), lambda i: (i, 0))
    vec = pl.BlockSpec((1, D), lambda i: (0, 0))
    ins = [x, g, dh] + ([dres] if has_res else [])
    return pl.pallas_call(
        body, name=name, grid=(T // tm,),
        in_specs=[row, vec, row] + ([row] if has_res else []),
        out_specs=[row, vec],
        out_shape=[jax.ShapeDtypeStruct((T, D), F32), jax.ShapeDtypeStruct((1, D), F32)],
        compiler_params=_cparams(("arbitrary",)),
    )(*ins)


def _loss_head(name, x, g, tgt):
    T, D = x.shape
    tm = _pick(T, (512, 256, 128))

    def body(x_ref, g_ref, t_ref, dx_ref, dg_ref, loss_ref):
        @pl.when(pl.program_id(0) == 0)
        def _():
            dg_ref[...] = jnp.zeros_like(dg_ref)
            loss_ref[...] = jnp.zeros_like(loss_ref)

        xf = x_ref[...]
        r = lax.rsqrt(jnp.mean(xf * xf, axis=-1, keepdims=True) + EPS)
        xhat = xf * r
        gv = g_ref[...]
        err = xhat * gv - t_ref[...]
        row_loss = jnp.mean(err * err, axis=-1, keepdims=True)
        loss_ref[...] += 0.5 * jnp.sum(row_loss, axis=0, keepdims=True)
        dy = err * (1.0 / D)
        dxh = dy * gv
        dx_ref[...] = r * (dxh - xhat * jnp.mean(dxh * xhat, axis=-1, keepdims=True))
        dg_ref[...] += jnp.sum(dy * xhat, axis=0, keepdims=True)

    row = pl.BlockSpec((tm, D), lambda i: (i, 0))
    vec = pl.BlockSpec((1, D), lambda i: (0, 0))
    return pl.pallas_call(
        body, name=name, grid=(T // tm,),
        in_specs=[row, vec, row],
        out_specs=[row, vec, pl.BlockSpec((1, 128), lambda i: (0, 0))],
        out_shape=[jax.ShapeDtypeStruct((T, D), F32), jax.ShapeDtypeStruct((1, D), F32),
                   jax.ShapeDtypeStruct((1, 128), F32)],
        compiler_params=_cparams(("arbitrary",)),
    )(x, g, tgt)


def _swap_halves(t):
    half = t.shape[-1] // 2
    return jnp.concatenate([t[:, half:], t[:, :half]], axis=-1)


def _rope(name, t, cos2, sgn_sin, out_dtype):
    H, T, R = t.shape
    tm = _pick(T, (1024, 512, 256, 128))

    def body(t_ref, c_ref, s_ref, o_ref):
        tf = t_ref[...].astype(F32)
        o_ref[...] = (tf * c_ref[...] + _swap_halves(tf) * s_ref[...]).astype(o_ref.dtype)

    slab = pl.BlockSpec((None, tm, R), lambda h, i: (h, i, 0))
    tab = pl.BlockSpec((tm, R), lambda h, i: (i, 0))
    return pl.pallas_call(
        body, name=name, grid=(H, T // tm),
        in_specs=[slab, tab, tab], out_specs=slab,
        out_shape=jax.ShapeDtypeStruct((H, T, R), out_dtype),
        compiler_params=_cparams(("parallel", "parallel")),
    )(t, cos2, sgn_sin)


def _rope_bwd(name, dy, cos2, sgn_sin, sum_heads):
    H, T, R = dy.shape
    tm = _pick(T, (1024, 512, 256, 128))

    def body(d_ref, c_ref, s_ref, o_ref):
        d = d_ref[...]
        if sum_heads:
            tot = d[0]
            for h in range(1, H):
                tot = tot + d[h]
            d = tot
        o_ref[...] = d * c_ref[...] + _swap_halves(d * s_ref[...])

    if sum_heads:
        grid = (T // tm,)
        in_slab = pl.BlockSpec((H, tm, R), lambda i: (0, i, 0))
        out_slab = pl.BlockSpec((tm, R), lambda i: (i, 0))
        tab = pl.BlockSpec((tm, R), lambda i: (i, 0))
        out_shape = jax.ShapeDtypeStruct((T, R), F32)
        sem = ("parallel",)
    else:
        grid = (H, T // tm)
        in_slab = pl.BlockSpec((None, tm, R), lambda h, i: (h, i, 0))
        out_slab = in_slab
        tab = pl.BlockSpec((tm, R), lambda h, i: (i, 0))
        out_shape = jax.ShapeDtypeStruct((H, T, R), F32)
        sem = ("parallel", "parallel")
    return pl.pallas_call(
        body, name=name, grid=grid, in_specs=[in_slab, tab, tab], out_specs=out_slab,
        out_shape=out_shape, compiler_params=_cparams(sem),
    )(dy, cos2, sgn_sin)


def _log_sigmoid(z):
    return jnp.minimum(z, 0.0) - jnp.log(1.0 + jnp.exp(-jnp.abs(z)))


def _gate_cumsum(name, fl, b, tb):
    H, T = fl.shape

    def body(f_ref, b_ref, c_ref, carry):
        @pl.when(pl.program_id(0) == 0)
        def _():
            carry[...] = jnp.zeros_like(carry)

        ls = _log_sigmoid(f_ref[...] + b_ref[...])
        src = lax.broadcasted_iota(jnp.int32, (tb, tb), 0)
        dst = lax.broadcasted_iota(jnp.int32, (tb, tb), 1)
        tri = (src <= dst).astype(F32)
        c = lax.dot_general(ls, tri, NN, precision=lax.Precision.HIGHEST,
                            preferred_element_type=F32) + carry[...]
        c_ref[...] = c
        carry[...] = carry[...] + jnp.sum(ls, axis=-1, keepdims=True)

    return pl.pallas_call(
        body, name=name, grid=(T // tb,),
        in_specs=[pl.BlockSpec((H, tb), lambda i: (0, i)), pl.BlockSpec((H, 1), lambda i: (0, 0))],
        out_specs=pl.BlockSpec((H, tb), lambda i: (0, i)),
        out_shape=jax.ShapeDtypeStruct((H, T), F32),
        scratch_shapes=[pltpu.VMEM((H, 1), F32)],
        compiler_params=_cparams(("arbitrary",)),
    )(fl, b)


def _gate_cumsum_bwd(name, d_query, d_key, fl, b, tb):
    H, T = fl.shape
    nb = T // tb

    def body(dq_ref, dk_ref, f_ref, b_ref, dfl_ref, db_ref, carry):
        @pl.when(pl.program_id(0) == 0)
        def _():
            carry[...] = jnp.zeros_like(carry)
            db_ref[...] = jnp.zeros_like(db_ref)

        d = dq_ref[...] - dk_ref[...]
        src = lax.broadcasted_iota(jnp.int32, (tb, tb), 0)
        dst = lax.broadcasted_iota(jnp.int32, (tb, tb), 1)
        tri = (src >= dst).astype(F32)
        dls = lax.dot_general(d, tri, NN, precision=lax.Precision.HIGHEST,
                              preferred_element_type=F32) + carry[...]
        z = f_ref[...] + b_ref[...]
        dfl = dls * (1.0 / (1.0 + jnp.exp(z)))
        dfl_ref[...] = dfl
        db_ref[...] += jnp.sum(dfl, axis=-1, keepdims=True)
        carry[...] = carry[...] + jnp.sum(d, axis=-1, keepdims=True)

    blk = pl.BlockSpec((H, tb), lambda i: (0, nb - 1 - i))
    vec = pl.BlockSpec((H, 1), lambda i: (0, 0))
    return pl.pallas_call(
        body, name=name, grid=(nb,),
        in_specs=[blk, blk, blk, vec], out_specs=[blk, vec],
        out_shape=[jax.ShapeDtypeStruct((H, T), F32), jax.ShapeDtypeStruct((H, 1), F32)],
        scratch_shapes=[pltpu.VMEM((H, 1), F32)],
        compiler_params=_cparams(("arbitrary",)),
    )(d_query, d_key, fl, b)


def _causal_mask(tq, rows_are_queries):
    r = lax.broadcasted_iota(jnp.int32, (tq, tq), 0)
    c = lax.broadcasted_iota(jnp.int32, (tq, tq), 1)
    return (c <= r) if rows_are_queries else (r <= c)


def _flash_fwd(name, q, k, v, bias_row, scale, tq):
    H, T, dqk = q.shape
    dv = v.shape[2]
    nq = T // tq
    has_bias = bias_row is not None

    def body(*refs):
        if has_bias:
            q_ref, k_ref, v_ref, b_ref, o_ref, lse_ref, m_sc, l_sc, acc_sc = refs
        else:
            q_ref, k_ref, v_ref, o_ref, lse_ref, m_sc, l_sc, acc_sc = refs
        qi = pl.program_id(1)
        ki = pl.program_id(2)

        @pl.when(ki == 0)
        def _():
            m_sc[...] = jnp.full_like(m_sc, NEG)
            l_sc[...] = jnp.zeros_like(l_sc)
            acc_sc[...] = jnp.zeros_like(acc_sc)

        def step(masked):
            s = _dot(q_ref[...], k_ref[...], NT) * scale
            if has_bias:
                s = s + b_ref[...]
            if masked:
                s = jnp.where(_causal_mask(tq, True), s, NEG)
            m_prev = m_sc[...]
            m_new = jnp.maximum(m_prev, jnp.max(s, axis=1, keepdims=True))
            alpha = jnp.exp(m_prev - m_new)
            p = jnp.exp(s - m_new)
            l_sc[...] = alpha * l_sc[...] + jnp.sum(p, axis=1, keepdims=True)
            acc_sc[...] = alpha * acc_sc[...] + _dot(p.astype(BF16), v_ref[...], NN)
            m_sc[...] = m_new

        @pl.when(ki < qi)
        def _():
            step(False)

        @pl.when(ki == qi)
        def _():
            step(True)
            l = l_sc[...]
            o_ref[...] = acc_sc[...] / l
            lse_ref[...] = m_sc[...] + jnp.log(l)

    q_spec = pl.BlockSpec((None, tq, dqk), lambda h, i, j: (h, i, 0))
    k_spec = pl.BlockSpec((None, tq, dqk), lambda h, i, j: (h, jnp.minimum(i, j), 0))
    v_spec = pl.BlockSpec((None, tq, dv), lambda h, i, j: (h, jnp.minimum(i, j), 0))
    in_specs = [q_spec, k_spec, v_spec]
    ins = [q, k, v]
    if has_bias:
        in_specs.append(pl.BlockSpec((None, 1, tq), lambda h, i, j: (h, 0, jnp.minimum(i, j))))
        ins.append(bias_row)
    return pl.pallas_call(
        body, name=name, grid=(H, nq, nq), in_specs=in_specs,
        out_specs=[pl.BlockSpec((None, tq, dv), lambda h, i, j: (h, i, 0)),
                   pl.BlockSpec((None, tq, 1), lambda h, i, j: (h, i, 0))],
        out_shape=[jax.ShapeDtypeStruct((H, T, dv), F32), jax.ShapeDtypeStruct((H, T, 1), F32)],
        scratch_shapes=[pltpu.VMEM((tq, 1), F32), pltpu.VMEM((tq, 1), F32), pltpu.VMEM((tq, dv), F32)],
        compiler_params=_cparams(("parallel", "parallel", "arbitrary")),
    )(*ins)


def _row_dot(name, a, b):
    H, T, d = a.shape
    tm = _pick(T, (1024, 512, 256, 128))

    def body(a_ref, b_ref, o_ref):
        o_ref[...] = jnp.sum(a_ref[...].astype(F32) * b_ref[...].astype(F32), axis=-1, keepdims=True)

    slab = pl.BlockSpec((None, tm, d), lambda h, i: (h, i, 0))
    return pl.pallas_call(
        body, name=name, grid=(H, T // tm), in_specs=[slab, slab],
        out_specs=pl.BlockSpec((None, tm, 1), lambda h, i: (h, i, 0)),
        out_shape=jax.ShapeDtypeStruct((H, T, 1), F32),
        compiler_params=_cparams(("parallel", "parallel")),
    )(a, b)


def _flash_dq(name, q, k, v, bias_row, do, lse, delta, scale, tq):
    H, T, dqk = q.shape
    dv = v.shape[2]
    nq = T // tq
    has_bias = bias_row is not None

    def body(*refs):
        if has_bias:
            q_ref, k_ref, v_ref, b_ref, do_ref, lse_ref, dl_ref, dq_ref, dr_ref, dq_sc, dr_sc = refs
        else:
            q_ref, k_ref, v_ref, do_ref, lse_ref, dl_ref, dq_ref, dq_sc = refs
        qi = pl.program_id(1)
        ki = pl.program_id(2)

        @pl.when(ki == 0)
        def _():
            dq_sc[...] = jnp.zeros_like(dq_sc)
            if has_bias:
                dr_sc[...] = jnp.zeros_like(dr_sc)

        def step(masked):
            kb = k_ref[...]
            s = _dot(q_ref[...], kb, NT) * scale
            if has_bias:
                s = s + b_ref[...]
            if masked:
                s = jnp.where(_causal_mask(tq, True), s, NEG)
            p = jnp.exp(s - lse_ref[...])
            dp = _dot(do_ref[...].astype(BF16), v_ref[...], NT)
            ds = p * (dp - dl_ref[...])
            dq_sc[...] += _dot(ds.astype(BF16), kb, NN)
            if has_bias:
                dr_sc[...] += jnp.sum(ds, axis=1, keepdims=True)

        @pl.when(ki < qi)
        def _():
            step(False)

        @pl.when(ki == qi)
        def _():
            step(True)
            dq_ref[...] = dq_sc[...] * scale
            if has_bias:
                dr_ref[...] = dr_sc[...]

    q_spec = pl.BlockSpec((None, tq, dqk), lambda h, i, j: (h, i, 0))
    k_spec = pl.BlockSpec((None, tq, dqk), lambda h, i, j: (h, jnp.minimum(i, j), 0))
    v_spec = pl.BlockSpec((None, tq, dv), lambda h, i, j: (h, jnp.minimum(i, j), 0))
    col = pl.BlockSpec((None, tq, 1), lambda h, i, j: (h, i, 0))
    in_specs = [q_spec, k_spec, v_spec]
    ins = [q, k, v]
    if has_bias:
        in_specs.append(pl.BlockSpec((None, 1, tq), lambda h, i, j: (h, 0, jnp.minimum(i, j))))
        ins.append(bias_row)
    in_specs += [pl.BlockSpec((None, tq, dv), lambda h, i, j: (h, i, 0)), col, col]
    ins += [do, lse, delta]
    out_specs = [q_spec]
    out_shape = [jax.ShapeDtypeStruct((H, T, dqk), F32)]
    scratch = [pltpu.VMEM((tq, dqk), F32)]
    if has_bias:
        out_specs.append(col)
        out_shape.append(jax.ShapeDtypeStruct((H, T, 1), F32))
        scratch.append(pltpu.VMEM((tq, 1), F32))
    out = pl.pallas_call(
        body, name=name, grid=(H, nq, nq), in_specs=in_specs, out_specs=out_specs,
        out_shape=out_shape, scratch_shapes=scratch,
        compiler_params=_cparams(("parallel", "parallel", "arbitrary")),
    )(*ins)
    return out if has_bias else out[0]


def _flash_dkv(name, q, k, v, bias_col, do, lse_row, delta_row, scale, tq):
    H, T, dqk = q.shape
    dv = v.shape[2]
    nq = T // tq
    has_bias = bias_col is not None

    def body(*refs):
        if has_bias:
            (q_ref, k_ref, v_ref, b_ref, do_ref, lse_ref, dl_ref,
             dk_ref, dv_ref, db_ref, dk_sc, dv_sc, db_sc) = refs
        else:
            q_ref, k_ref, v_ref, do_ref, lse_ref, dl_ref, dk_ref, dv_ref, dk_sc, dv_sc = refs
        ki = pl.program_id(1)
        qi = pl.program_id(2)

        @pl.when(qi == 0)
        def _():
            dk_sc[...] = jnp.zeros_like(dk_sc)
            dv_sc[...] = jnp.zeros_like(dv_sc)
            if has_bias:
                db_sc[...] = jnp.zeros_like(db_sc)

        def step(masked):
            qb = q_ref[...]
            dob = do_ref[...].astype(BF16)
            st = _dot(k_ref[...], qb, NT) * scale
            if has_bias:
                st = st + b_ref[...]
            if masked:
                st = jnp.where(_causal_mask(tq, False), st, NEG)
            pt = jnp.exp(st - lse_ref[...])
            dv_sc[...] += _dot(pt.astype(BF16), dob, NN)
            dpt = _dot(v_ref[...], dob, NT)
            dst = pt * (dpt - dl_ref[...])
            dk_sc[...] += _dot(dst.astype(BF16), qb, NN)
            if has_bias:
                db_sc[...] += jnp.sum(dst, axis=1, keepdims=True)

        @pl.when(qi > ki)
        def _():
            step(False)

        @pl.when(qi == ki)
        def _():
            step(True)

        @pl.when(qi == nq - 1)
        def _():
            dk_ref[...] = dk_sc[...] * scale
            dv_ref[...] = dv_sc[...]
            if has_bias:
                db_ref[...] = db_sc[...]

    q_spec = pl.BlockSpec((None, tq, dqk), lambda h, j, i: (h, jnp.maximum(i, j), 0))
    k_spec = pl.BlockSpec((None, tq, dqk), lambda h, j, i: (h, j, 0))
    v_spec = pl.BlockSpec((None, tq, dv), lambda h, j, i: (h, j, 0))
    do_spec = pl.BlockSpec((None, tq, dv), lambda h, j, i: (h, jnp.maximum(i, j), 0))
    row = pl.BlockSpec((None, 1, tq), lambda h, j, i: (h, 0, jnp.maximum(i, j)))
    kcol = pl.BlockSpec((None, tq, 1), lambda h, j, i: (h, j, 0))
    in_specs = [q_spec, k_spec, v_spec]
    ins = [q, k, v]
    if has_bias:
        in_specs.append(kcol)
        ins.append(bias_col)
    in_specs += [do_spec, row, row]
    ins += [do, lse_row, delta_row]
    out_specs = [k_spec, v_spec]
    out_shape = [jax.ShapeDtypeStruct((H, T, dqk), F32), jax.ShapeDtypeStruct((H, T, dv), F32)]
    scratch = [pltpu.VMEM((tq, dqk), F32), pltpu.VMEM((tq, dv), F32)]
    if has_bias:
        out_specs.append(kcol)
        out_shape.append(jax.ShapeDtypeStruct((H, T, 1), F32))
        scratch.append(pltpu.VMEM((tq, 1), F32))
    return pl.pallas_call(
        body, name=name, grid=(H, nq, nq), in_specs=in_specs, out_specs=out_specs,
        out_shape=out_shape, scratch_shapes=scratch,
        compiler_params=_cparams(("parallel", "parallel", "arbitrary")),
    )(*ins)


def _adamw_math(w, g, m, v):
    m = ADAM_B1 * m + (1.0 - ADAM_B1) * g
    v = ADAM_B2 * v + (1.0 - ADAM_B2) * (g * g)
    m_hat = m / (1.0 - ADAM_B1 ** ADAM_STEP)
    v_hat = v / (1.0 - ADAM_B2 ** ADAM_STEP)
    delta = -ADAM_LR * (m_hat / (jnp.sqrt(v_hat) + ADAM_EPS) + ADAM_WD * w)
    return delta, m, v


def _adamw(name, parts, w, m, v):
    P, R, C = parts.shape
    tr = _pick(R, (256, 128, 64, 32, 16, 8))

    def body(p_ref, w_ref, m_ref, v_ref, g_out, d_out, m_out, v_out):
        g = p_ref[0]
        for i in range(1, P):
            g = g + p_ref[i]
        delta, m_new, v_new = _adamw_math(w_ref[...], g, m_ref[...], v_ref[...])
        g_out[...] = g
        d_out[...] = delta
        m_out[...] = m_new
        v_out[...] = v_new

    blk = pl.BlockSpec((tr, C), lambda i: (i, 0))
    sds = jax.ShapeDtypeStruct((R, C), F32)
    return pl.pallas_call(
        body, name=name, grid=(R // tr,),
        in_specs=[pl.BlockSpec((P, tr, C), lambda i: (0, i, 0)), blk, blk, blk],
        out_specs=[blk] * 4, out_shape=[sds] * 4,
        compiler_params=_cparams(("parallel",)),
    )(parts, w, m, v)


def _my_position():
    return lax.axis_index("x"), lax.axis_index("y"), lax.axis_index("c")


def _slot(p):
    return 4 * p[0] + 2 * p[1] + p[2]


def _flip(p, k):
    return tuple((1 - p[i]) if (k >> (2 - i)) & 1 else p[i] for i in range(3))


def _allgather_weights(shards):
    n = len(shards)

    def body(*refs):
        ins = refs[:n]
        outs = refs[n:2 * n]
        send_sems, recv_sems, local_sems = refs[2 * n:]
        x, y, c = _my_position()
        me, sibling = (x, y, c), (x, y, 1 - c)
        chips = [(1 - x, y), (x, 1 - y), (1 - x, 1 - y)]

        def copy(a, k, block, to, src=None):
            dst = outs[a].at[_slot(block)]
            return pltpu.make_async_remote_copy(
                src_ref=dst if src is None else src, dst_ref=dst,
                send_sem=send_sems.at[7 * a + k], recv_sem=recv_sems.at[7 * a + k],
                device_id=to, device_id_type=MESH)

        started = []
        for a in range(n):
            mine = pltpu.make_async_copy(ins[a], outs[a].at[_slot(me)], local_sems.at[a])
            mine.start()
            started.append(mine)
        first = []
        for a in range(n):
            first.append(copy(a, 0, me, sibling, src=ins[a]))
            first += [copy(a, 1 + j, me, (*chip, c), src=ins[a]) for j, chip in enumerate(chips)]
        for cp in first:
            cp.start()
        passed = []
        for j, chip in enumerate(chips):
            for a in range(n):
                copy(a, 1 + j, (*chip, c), me).wait_recv()
                fwd = copy(a, 4 + j, (*chip, c), sibling)
                fwd.start()
                passed.append(fwd)
        for a in range(n):
            copy(a, 0, sibling, me).wait_recv()
            for j, chip in enumerate(chips):
                copy(a, 4 + j, (*chip, 1 - c), me).wait_recv()
        for cp in first + passed:
            cp.wait_send()
        for mine in started:
            mine.wait()

    hbm = pl.BlockSpec(memory_space=pl.ANY)
    return pl.pallas_call(
        body, name="allgather_weights",
        in_specs=[hbm] * n, out_specs=[hbm] * n,
        out_shape=[jax.ShapeDtypeStruct((N_DEV,) + s.shape, s.dtype) for s in shards],
        scratch_shapes=[pltpu.SemaphoreType.DMA((7 * n,)), pltpu.SemaphoreType.DMA((7 * n,)),
                        pltpu.SemaphoreType.DMA((n,))],
        compiler_params=pltpu.CompilerParams(has_side_effects=True),
    )(*shards)


def _alltoall_grads(grads):
    n = len(grads)

    def body(*refs):
        ins = refs[:n]
        outs = refs[n:2 * n]
        send_sems, recv_sems, local_sems = refs[2 * n:]
        me = _my_position()
        started = []
        for a in range(n):
            mine = pltpu.make_async_copy(ins[a].at[_slot(me)], outs[a].at[_slot(me)], local_sems.at[a])
            mine.start()
            started.append(mine)
        sends = []
        for k in range(1, N_DEV):
            peer = _flip(me, k)
            for a in range(n):
                cp = pltpu.make_async_remote_copy(
                    src_ref=ins[a].at[_slot(peer)], dst_ref=outs[a].at[_slot(me)],
                    send_sem=send_sems.at[7 * a + k - 1], recv_sem=recv_sems.at[7 * a + k - 1],
                    device_id=peer, device_id_type=MESH)
                cp.start()
                sends.append(cp)
        for k in range(1, N_DEV):
            peer = _flip(me, k)
            for a in range(n):
                pltpu.make_async_remote_copy(
                    src_ref=ins[a].at[_slot(peer)], dst_ref=outs[a].at[_slot(peer)],
                    send_sem=send_sems.at[7 * a + k - 1], recv_sem=recv_sems.at[7 * a + k - 1],
                    device_id=peer, device_id_type=MESH).wait_recv()
        for cp in sends:
            cp.wait_send()
        for mine in started:
            mine.wait()

    hbm = pl.BlockSpec(memory_space=pl.ANY)
    return pl.pallas_call(
        body, name="alltoall_grads",
        in_specs=[hbm] * n, out_specs=[hbm] * n,
        out_shape=[jax.ShapeDtypeStruct(g.shape, g.dtype) for g in grads],
        scratch_shapes=[pltpu.SemaphoreType.DMA((7 * n,)), pltpu.SemaphoreType.DMA((7 * n,)),
                        pltpu.SemaphoreType.DMA((n,))],
        compiler_params=pltpu.CompilerParams(has_side_effects=True),
    )(*grads)


def _allreduce_small(v):
    R, C = v.shape

    def body(v_ref, o_ref, buf, send_sems, recv_sems):
        me = _my_position()
        buf[_slot(me)] = v_ref[...]
        sends = []
        for k in range(1, N_DEV):
            peer = _flip(me, k)
            cp = pltpu.make_async_remote_copy(
                src_ref=v_ref, dst_ref=buf.at[_slot(me)],
                send_sem=send_sems.at[k - 1], recv_sem=recv_sems.at[k - 1],
                device_id=peer, device_id_type=MESH)
            cp.start()
            sends.append(cp)
        for k in range(1, N_DEV):
            peer = _flip(me, k)
            pltpu.make_async_remote_copy(
                src_ref=v_ref, dst_ref=buf.at[_slot(peer)],
                send_sem=send_sems.at[k - 1], recv_sem=recv_sems.at[k - 1],
                device_id=peer, device_id_type=MESH).wait_recv()
        for cp in sends:
            cp.wait_send()
        tot = buf[0]
        for s in range(1, N_DEV):
            tot = tot + buf[s]
        o_ref[...] = tot

    vm = pl.BlockSpec(memory_space=pltpu.VMEM)
    return pl.pallas_call(
        body, name="allreduce_small",
        in_specs=[vm], out_specs=vm, out_shape=jax.ShapeDtypeStruct((R, C), F32),
        scratch_shapes=[pltpu.VMEM((N_DEV, R, C), F32), pltpu.SemaphoreType.DMA((7,)),
                        pltpu.SemaphoreType.DMA((7,))],
        compiler_params=pltpu.CompilerParams(has_side_effects=True),
    )(v)


def _to_heads(t, heads):
    T = t.shape[0]
    return t.reshape(T, heads, t.shape[1] // heads).transpose(1, 0, 2)


def _from_heads(t):
    H, T, d = t.shape
    return t.transpose(1, 0, 2).reshape(T, H * d)


def _col_to_row(t):
    return t.reshape(t.shape[0], 1, t.shape[1])


def _pad_cols(t, n):
    return jnp.pad(t, ((0, 0), (0, n - t.shape[1])))


def _pack_small(mix, ffn, kv, fin, kva, qa, bf, last):
    row6 = jnp.concatenate([kva.reshape(-1), qa.reshape(-1), bf.reshape(-1),
                            jnp.zeros((D_MODEL - KV_LORA - Q_LORA - FOX_HEADS,), F32)])
    return jnp.stack([mix[0], mix[1], ffn[0], ffn[1], kv.reshape(-1), fin.reshape(-1), row6, last])


def _unpack_small(p):
    mix = p[0:2]
    ffn = p[2:4]
    kv = p[4]
    fin = p[5]
    kva = p[6, :KV_LORA]
    qa = p[6, KV_LORA:KV_LORA + Q_LORA].reshape(1, Q_LORA)
    bf = p[6, KV_LORA + Q_LORA:KV_LORA + Q_LORA + FOX_HEADS].reshape(1, FOX_HEADS)
    return mix, ffn, bf, kv, kva, qa, fin


def _mlp_fwd(tag, xin, g, w_up, w_down):
    h = _rms(f"{tag}_norm", xin, g, BF16)

    def act(acc):
        r = jnp.maximum(acc, 0.0)
        return acc, r * r

    u, a = _mm(f"{tag}_up", h, w_up, "nn", (BF16, BF16), epi=act)
    xout = _mm(f"{tag}_down", a, w_down, "nn", (F32,), epi=lambda acc, r: (acc + r,), extras=(xin,))
    return xout, (h, u, a)


def _mlp_bwd(tag, gout, xin, g, w_up, w_down, saved):
    h, u, a = saved
    dw_down = _mm_tn(f"{tag}_dwdown", a, gout)
    du = _mm(f"{tag}_du", gout, w_down, "nt", (BF16,),
             epi=lambda acc, uu: (acc * (2.0 * jnp.maximum(uu.astype(F32), 0.0)),), extras=(u,))
    dw_up = _mm_tn(f"{tag}_dwup", h, du)
    dh = _mm(f"{tag}_dh", du, w_up, "nt", (F32,))
    gin, dg = _rms_bwd(f"{tag}_norm_bwd", xin, g, dh, dres=gout)
    return gin, dg, dw_up, dw_down


def kernel(x, norm_mix_g, norm_ffn_g, fox_w_in, fox_b_f, fox_w_out, kv_norm_g, mla_w_kv_a, mla_kv_a_norm_g, mla_w_kv_b, mla_w_q_a, mla_q_a_norm_g, mla_w_q_b, mla_w_out, ffn_w_up, ffn_w_down, final_norm_g, loss_target, m_norm_mix_g, m_norm_ffn_g, m_fox_w_in, m_fox_b_f, m_fox_w_out, m_kv_norm_g, m_mla_w_kv_a, m_mla_kv_a_norm_g, m_mla_w_kv_b, m_mla_w_q_a, m_mla_q_a_norm_g, m_mla_w_q_b, m_mla_w_out, m_ffn_w_up, m_ffn_w_down, m_final_norm_g, v_norm_mix_g, v_norm_ffn_g, v_fox_w_in, v_fox_b_f, v_fox_w_out, v_kv_norm_g, v_mla_w_kv_a, v_mla_kv_a_norm_g, v_mla_w_kv_b, v_mla_w_q_a, v_mla_q_a_norm_g, v_mla_w_q_b, v_mla_w_out, v_ffn_w_up, v_ffn_w_down, v_final_norm_g):
    T = x.shape[1]
    D = D_MODEL
    tq = 512 if T >= 2048 else 128
    x0 = x[0]
    tgt = loss_target[0]

    shards = [fox_w_in[0], fox_w_out[0], mla_w_kv_a, mla_w_kv_b, mla_w_q_a[0], mla_w_q_b[0],
              mla_w_out[0], ffn_w_up, ffn_w_down]
    gat = _allgather_weights([s.astype(BF16) for s in shards])
    w_in = gat[0].transpose(1, 0, 2).reshape(D, 3 * D + FOX_HEADS)
    w_qkv = w_in[:, :3 * D]
    w_f = _pad_cols(w_in[:, 3 * D:], 128)
    w_fo = gat[1].reshape(D, D)
    w_kva = _pad_cols(gat[2].reshape(D, KV_LORA + QK_ROPE), KV_A_PAD)
    w_kvb = gat[3].transpose(1, 0, 2).reshape(KV_LORA, MLA_HEADS * (QK_NOPE + V_HEAD))
    w_qa = gat[4].reshape(D, Q_LORA)
    w_qb = gat[5].transpose(1, 0, 2).reshape(Q_LORA, MLA_HEADS * (QK_NOPE + QK_ROPE))
    w_mo = gat[6].reshape(D, D)
    w_up = gat[7].transpose(1, 2, 0, 3).reshape(2, D, D_FF)
    w_down = gat[8].transpose(1, 0, 2, 3).reshape(2, D_FF, D)

    g_mix0, g_mix1 = norm_mix_g[0:1], norm_mix_g[1:2]
    g_ffn0, g_ffn1 = norm_ffn_g[0:1], norm_ffn_g[1:2]
    g_kv = kv_norm_g.reshape(1, D)
    g_kva = mla_kv_a_norm_g.reshape(1, KV_LORA)
    g_qa = mla_q_a_norm_g.reshape(1, Q_LORA)
    g_fin = final_norm_g.reshape(1, D)

    inv = 1.0 / (ROPE_BASE ** (jnp.arange(0, QK_ROPE, 2, dtype=F32) / QK_ROPE))
    ang = jnp.arange(T, dtype=F32)[:, None] * inv[None, :]
    cos, sin = jnp.cos(ang), jnp.sin(ang)
    cos2 = jnp.concatenate([cos, cos], axis=-1)
    sgn_sin = jnp.concatenate([-sin, sin], axis=-1)

    h0 = _rms("l0_mix_norm", x0, g_mix0, BF16)
    qkv = _mm("fox_qkv", h0, w_qkv, "nn", (BF16,))
    fl_pad = _mm("fox_gate_logit", h0, w_f, "nn", (F32,))
    fq = _to_heads(qkv[:, :D], FOX_HEADS)
    fk = _to_heads(qkv[:, D:2 * D], FOX_HEADS)
    fv = _to_heads(qkv[:, 2 * D:], FOX_HEADS)
    fl = fl_pad[:, :FOX_HEADS].T
    b_f = fox_b_f.reshape(FOX_HEADS, 1)
    cgate = _gate_cumsum("fox_gate_scan", fl, b_f, tq)
    fox_scale = FOX_HEAD_DIM ** -0.5
    fbias_row = (-cgate).reshape(FOX_HEADS, 1, T)
    fbias_col = (-cgate).reshape(FOX_HEADS, T, 1)
    fo, flse = _flash_fwd("fox_attn", fq, fk, fv, fbias_row, fox_scale, tq)
    fctx = _from_heads(fo).astype(BF16)
    x1 = _mm("fox_out", fctx, w_fo, "nn", (F32,), epi=lambda acc, r: (acc + r,), extras=(x0,))
    x2, mlp0 = _mlp_fwd("l0_ffn", x1, g_ffn0, w_up[0], w_down[0])

    src = _rms("kv_norm", x2, g_kv, BF16)
    kva = _mm("kv_a", src, w_kva, "nn", (F32,))
    kva_lat = kva[:, :KV_LORA]
    c_kv = _rms("kv_a_norm", kva_lat, g_kva, BF16)
    k_rope = _rope("k_rope", kva[:, KV_LORA:KV_LORA + QK_ROPE][None], cos2, sgn_sin, BF16)
    kvb = _mm("kv_b", c_kv, w_kvb, "nn", (BF16,))
    kvb_h = _to_heads(kvb, MLA_HEADS)
    mk = jnp.concatenate([kvb_h[:, :, :QK_NOPE],
                          jnp.broadcast_to(k_rope, (MLA_HEADS, T, QK_ROPE))], axis=-1)
    mv = kvb_h[:, :, QK_NOPE:]

    h1 = _rms("l1_mix_norm", x2, g_mix1, BF16)
    qa = _mm("q_a", h1, w_qa, "nn", (F32,))
    c_q = _rms("q_a_norm", qa, g_qa, BF16)
    qf = _mm("q_b", c_q, w_qb, "nn", (F32,))
    qf_h = _to_heads(qf, MLA_HEADS)
    q_rope = _rope("q_rope", qf_h[:, :, QK_NOPE:], cos2, sgn_sin, BF16)
    mq = jnp.concatenate([qf_h[:, :, :QK_NOPE].astype(BF16), q_rope], axis=-1)
    mla_scale = (QK_NOPE + QK_ROPE) ** -0.5
    mo, mlse = _flash_fwd("mla_attn", mq, mk, mv, None, mla_scale, tq)
    mctx = _from_heads(mo).astype(BF16)
    x3 = _mm("mla_out", mctx, w_mo, "nn", (F32,), epi=lambda acc, r: (acc + r,), extras=(x2,))
    x4, mlp1 = _mlp_fwd("l1_ffn", x3, g_ffn1, w_up[1], w_down[1])

    g4, dg_fin, loss_vec = _loss_head("loss_head", x4, g_fin, tgt)

    g3, dg_ffn1, dw_up1, dw_down1 = _mlp_bwd("l1_ffn", g4, x3, g_ffn1, w_up[1], w_down[1], mlp1)

    dw_mo = _mm_tn("mla_out_dw", mctx, g3)
    dmo = _to_heads(_mm("mla_out_dx", g3, w_mo, "nt", (F32,)), MLA_HEADS)
    mdelta = _row_dot("mla_delta", mo, dmo)
    mdq = _flash_dq("mla_attn_dq", mq, mk, mv, None, dmo, mlse, mdelta, mla_scale, tq)
    mdk, mdv = _flash_dkv("mla_attn_dkv", mq, mk, mv, None, dmo, _col_to_row(mlse),
                          _col_to_row(mdelta), mla_scale, tq)
    dq_rope = _rope_bwd("q_rope_bwd", mdq[:, :, QK_NOPE:], cos2, sgn_sin, False)
    dqf = _from_heads(jnp.concatenate([mdq[:, :, :QK_NOPE], dq_rope], axis=-1))
    dw_qb = _mm_tn("q_b_dw", c_q, dqf)
    dc_q = _mm("q_b_dx", dqf, w_qb, "nt", (F32,))
    dqa, dg_qa = _rms_bwd("q_a_norm_bwd", qa, g_qa, dc_q)
    dw_qa = _mm_tn("q_a_dw", h1, dqa)
    dh1 = _mm("q_a_dx", dqa, w_qa, "nt", (F32,))
    g2a, dg_mix1 = _rms_bwd("l1_mix_norm_bwd", x2, g_mix1, dh1, dres=g3)

    dk_rope = _rope_bwd("k_rope_bwd", mdk[:, :, QK_NOPE:], cos2, sgn_sin, True)
    dkvb = _from_heads(jnp.concatenate([mdk[:, :, :QK_NOPE], mdv], axis=-1))
    dw_kvb = _mm_tn("kv_b_dw", c_kv, dkvb)
    dc_kv = _mm("kv_b_dx", dkvb, w_kvb, "nt", (F32,))
    dkva_lat, dg_kva = _rms_bwd("kv_a_norm_bwd", kva_lat, g_kva, dc_kv)
    dkva = _pad_cols(jnp.concatenate([dkva_lat, dk_rope], axis=-1), KV_A_PAD)
    dw_kva = _mm_tn("kv_a_dw", src, dkva)[:, :KV_LORA + QK_ROPE]
    dsrc = _mm("kv_a_dx", dkva, w_kva, "nt", (F32,))
    g2, dg_kv = _rms_bwd("kv_norm_bwd", x2, g_kv, dsrc, dres=g2a)

    g1, dg_ffn0, dw_up0, dw_down0 = _mlp_bwd("l0_ffn", g2, x1, g_ffn0, w_up[0], w_down[0], mlp0)

    dw_fo = _mm_tn("fox_out_dw", fctx, g1)
    dfo = _to_heads(_mm("fox_out_dx", g1, w_fo, "nt", (F32,)), FOX_HEADS)
    fdelta = _row_dot("fox_delta", fo, dfo)
    fdq, fdrow = _flash_dq("fox_attn_dq", fq, fk, fv, fbias_row, dfo, flse, fdelta, fox_scale, tq)
    fdk, fdv, fdb = _flash_dkv("fox_attn_dkv", fq, fk, fv, fbias_col, dfo, _col_to_row(flse),
                               _col_to_row(fdelta), fox_scale, tq)
    dfl, db_f = _gate_cumsum_bwd("fox_gate_scan_bwd", fdrow.reshape(FOX_HEADS, T),
                                 fdb.reshape(FOX_HEADS, T), fl, b_f, tq)
    dqkv = jnp.concatenate([_from_heads(fdq), _from_heads(fdk), _from_heads(fdv)], axis=-1).astype(BF16)
    dfl_pad = _pad_cols(dfl.T, 128)
    dw_qkv = _mm_tn("fox_qkv_dw", h0, dqkv)
    dw_f = _mm_tn("fox_gate_dw", h0, dfl_pad)[:, :FOX_HEADS]
    dw_in = jnp.concatenate([dw_qkv, dw_f], axis=-1)
    dh0a = _mm("fox_gate_dx", dfl_pad, w_f, "nt", (F32,))
    dh0 = _mm("fox_qkv_dx", dqkv, w_qkv, "nt", (F32,), epi=lambda acc, r: (acc + r,), extras=(dh0a,))
    grad_x, dg_mix0 = _rms_bwd("l0_mix_norm_bwd", x0, g_mix0, dh0, dres=g1)

    dw_up = jnp.stack([dw_up0, dw_up1])
    dw_down = jnp.stack([dw_down0, dw_down1])
    shard_major = [
        dw_in.reshape(D, N_DEV, -1).transpose(1, 0, 2),
        dw_fo.reshape(N_DEV, D // N_DEV, D),
        dw_kva.reshape(N_DEV, D // N_DEV, KV_LORA + QK_ROPE),
        dw_kvb.reshape(KV_LORA, N_DEV, -1).transpose(1, 0, 2),
        dw_qa.reshape(N_DEV, D // N_DEV, Q_LORA),
        dw_qb.reshape(Q_LORA, N_DEV, -1).transpose(1, 0, 2),
        dw_mo.reshape(N_DEV, D // N_DEV, D),
        dw_up.reshape(2, D, N_DEV, -1).transpose(2, 0, 1, 3),
        dw_down.reshape(2, N_DEV, D_FF // N_DEV, D).transpose(1, 0, 2, 3),
    ]
    parts = _alltoall_grads(shard_major)

    names = ["fox_w_in", "fox_w_out", "mla_w_kv_a", "mla_w_kv_b", "mla_w_q_a", "mla_w_q_b",
             "mla_w_out", "ffn_w_up", "ffn_w_down"]
    moms = [m_fox_w_in, m_fox_w_out, m_mla_w_kv_a, m_mla_w_kv_b, m_mla_w_q_a, m_mla_w_q_b,
            m_mla_w_out, m_ffn_w_up, m_ffn_w_down]
    vars_ = [v_fox_w_in, v_fox_w_out, v_mla_w_kv_a, v_mla_w_kv_b, v_mla_w_q_a, v_mla_w_q_b,
             v_mla_w_out, v_ffn_w_up, v_ffn_w_down]
    full = [fox_w_in, fox_w_out, mla_w_kv_a, mla_w_kv_b, mla_w_q_a, mla_w_q_b, mla_w_out,
            ffn_w_up, ffn_w_down]
    big = {}
    for nm, p, w, m, v in zip(names, parts, full, moms, vars_):
        C = w.shape[-1]
        res = _adamw(f"adamw_{nm}", p.reshape(N_DEV, -1, C), w.reshape(-1, C), m.reshape(-1, C),
                     v.reshape(-1, C))
        big[nm] = [r.reshape(w.shape) for r in res]

    zrow = jnp.zeros((D,), F32)
    g_small = _pack_small(jnp.concatenate([dg_mix0, dg_mix1]), jnp.concatenate([dg_ffn0, dg_ffn1]),
                          dg_kv, dg_fin, dg_kva, dg_qa, db_f, zrow.at[0].set(loss_vec[0, 0]))
    tot_small = _allreduce_small(g_small)
    w_small = _pack_small(norm_mix_g, norm_ffn_g, kv_norm_g, final_norm_g, mla_kv_a_norm_g,
                          mla_q_a_norm_g, fox_b_f, zrow)
    m_small = _pack_small(m_norm_mix_g, m_norm_ffn_g, m_kv_norm_g, m_final_norm_g, m_mla_kv_a_norm_g,
                          m_mla_q_a_norm_g, m_fox_b_f, zrow)
    v_small = _pack_small(v_norm_mix_g, v_norm_ffn_g, v_kv_norm_g, v_final_norm_g, v_mla_kv_a_norm_g,
                          v_mla_q_a_norm_g, v_fox_b_f, zrow)
    small = _adamw("adamw_small", tot_small[None], w_small, m_small, v_small)
    loss = tot_small[7, 0]
    small = [_unpack_small(s) for s in small]

    def ordered(i):
        mix, ffn, bf, kv, kva, qa, fin = small[i]
        return [mix, ffn, big["fox_w_in"][i], bf, big["fox_w_out"][i], kv, big["mla_w_kv_a"][i], kva,
                big["mla_w_kv_b"][i], big["mla_w_q_a"][i], qa, big["mla_w_q_b"][i],
                big["mla_w_out"][i], big["ffn_w_up"][i], big["ffn_w_down"][i], fin]

    return (loss, grad_x[None], *ordered(0), *ordered(1), *ordered(2), *ordered(3))
```

```python
import functools
import math

import jax
import jax.numpy as jnp
from jax import lax
from jax.experimental import pallas as pl
from jax.experimental.pallas import tpu as pltpu

F32 = jnp.float32
BF16 = jnp.bfloat16
MESH = pl.DeviceIdType.MESH

N_DEV = 8
D_MODEL = 1024
FOX_HEADS = 16
FOX_HEAD_DIM = 64
FOX_AUG = 128
MLA_AUG = 256
MLA_HEADS = 8
QK_NOPE = 128
QK_ROPE = 64
V_HEAD = 128
Q_LORA = 384
KV_LORA = 256
KV_A_PAD = 384
D_FF = 4096
ROPE_BASE = 10000.0
EPS = 1e-6
NEG = -1e30

ADAM_LR = 0.001
ADAM_B1 = 0.9
ADAM_B2 = 0.999
ADAM_EPS = 1e-08
ADAM_WD = 0.01
ADAM_STEP = 10

VMEM_LIMIT_BYTES = 48 * 1024 * 1024

NN = (((1,), (0,)), ((), ()))
NT = (((1,), (1,)), ((), ()))
TN = (((0,), (0,)), ((), ()))
_FORMS = {"nn": NN, "nt": NT}


def _cparams(sem=None):
    return pltpu.CompilerParams(dimension_semantics=sem, vmem_limit_bytes=VMEM_LIMIT_BYTES)


def _pick(n, cands):
    for c in cands:
        if c <= n and n % c == 0:
            return c
    return n


def _dot(a, b, dims):
    return lax.dot_general(a, b, dims, preferred_element_type=F32)


def _mm(name, a, b, form, out_dtypes, epi=None, extras=(), tm=1024, tn=512):
    M, K = a.shape
    N = b.shape[1] if form == "nn" else b.shape[0]
    tm = _pick(M, (tm, 512, 256, 128))
    tn = _pick(N, (tn, 384, 256, 128))
    n_ex = len(extras)

    def body(*refs):
        a_ref, b_ref = refs[0], refs[1]
        ex = refs[2:2 + n_ex]
        outs = refs[2 + n_ex:]
        acc = _dot(a_ref[...].astype(BF16), b_ref[...].astype(BF16), _FORMS[form])
        res = epi(acc, *[e[...] for e in ex]) if epi is not None else (acc,)
        for o_ref, r in zip(outs, res):
            o_ref[...] = r.astype(o_ref.dtype)

    if form == "nn":
        b_spec = pl.BlockSpec((K, tn), lambda i, j: (0, j))
    else:
        b_spec = pl.BlockSpec((tn, K), lambda i, j: (j, 0))
    tile = pl.BlockSpec((tm, tn), lambda i, j: (i, j))
    out = pl.pallas_call(
        body, name=name, grid=(M // tm, N // tn),
        in_specs=[pl.BlockSpec((tm, K), lambda i, j: (i, 0)), b_spec] + [tile] * n_ex,
        out_specs=[tile] * len(out_dtypes),
        out_shape=[jax.ShapeDtypeStruct((M, N), dt) for dt in out_dtypes],
        compiler_params=_cparams(("parallel", "arbitrary")),
    )(a, b, *extras)
    return out if len(out_dtypes) > 1 else out[0]


def _mm_tn(name, a, b):
    T, Ka = a.shape
    N = b.shape[1]
    tk = _pick(Ka, (1024, 512, 384, 256, 128))
    tn = _pick(N, (1024, 768, 512, 384, 256, 128))
    tt = _pick(T, (1024, 512, 256, 128))

    def body(a_ref, b_ref, o_ref):
        @pl.when(pl.program_id(2) == 0)
        def _():
            o_ref[...] = jnp.zeros_like(o_ref)

        o_ref[...] += _dot(a_ref[...].astype(BF16), b_ref[...].astype(BF16), TN)

    return pl.pallas_call(
        body, name=name, grid=(Ka // tk, N // tn, T // tt),
        in_specs=[pl.BlockSpec((tt, tk), lambda i, j, t: (t, i)),
                  pl.BlockSpec((tt, tn), lambda i, j, t: (t, j))],
        out_specs=pl.BlockSpec((tk, tn), lambda i, j, t: (i, j)),
        out_shape=jax.ShapeDtypeStruct((Ka, N), F32),
        compiler_params=_cparams(("parallel", "parallel", "arbitrary")),
    )(a, b)


def _rms(name, x, g, out_dtype):
    T, D = x.shape
    tm = _pick(T, (1024, 512, 256, 128))

    def body(x_ref, g_ref, o_ref):
        xf = x_ref[...]
        r = lax.rsqrt(jnp.mean(xf * xf, axis=-1, keepdims=True) + EPS)
        o_ref[...] = (xf * r * g_ref[...]).astype(o_ref.dtype)

    return pl.pallas_call(
        body, name=name, grid=(T // tm,),
        in_specs=[pl.BlockSpec((tm, D), lambda i: (i, 0)), pl.BlockSpec((1, D), lambda i: (0, 0))],
        out_specs=pl.BlockSpec((tm, D), lambda i: (i, 0)),
        out_shape=jax.ShapeDtypeStruct((T, D), out_dtype),
        compiler_params=_cparams(("parallel",)),
    )(x, g)


def _rms_bwd(name, x, g, dh, dres=None):
    T, D = x.shape
    tm = _pick(T, (512, 256, 128))
    has_res = dres is not None

    def body(*refs):
        if has_res:
            x_ref, g_ref, dh_ref, dres_ref, dx_ref, dg_ref = refs
        else:
            x_ref, g_ref, dh_ref, dx_ref, dg_ref = refs

        @pl.when(pl.program_id(0) == 0)
        def _():
            dg_ref[...] = jnp.zeros_like(dg_ref)

        xf = x_ref[...]
        r = lax.rsqrt(jnp.mean(xf * xf, axis=-1, keepdims=True) + EPS)
        xhat = xf * r
        dy = dh_ref[...].astype(F32)
        dxh = dy * g_ref[...]
        dx = r * (dxh - xhat * jnp.mean(dxh * xhat, axis=-1, keepdims=True))
        if has_res:
            dx = dx + dres_ref[...]
        dx_ref[...] = dx
        dg_ref[...] += jnp.sum(dy * xhat, axis=0, keepdims=True)

    row = pl.BlockSpec((tm, D), lambda i: (i, 0))
    vec = pl.BlockSpec((1, D), lambda i: (0, 0))
    ins = [x, g, dh] + ([dres] if has_res else [])
    return pl.pallas_call(
        body, name=name, grid=(T // tm,),
        in_specs=[row, vec, row] + ([row] if has_res else []),
        out_specs=[row, vec],
        out_shape=[jax.ShapeDtypeStruct((T, D), F32), jax.ShapeDtypeStruct((1, D), F32)],
        compiler_params=_cparams(("arbitrary",)),
    )(*ins)


def _loss_head(name, x, g, tgt):
    T, D = x.shape
    tm = _pick(T, (512, 256, 128))

    def body(x_ref, g_ref, t_ref, dx_ref, dg_ref, loss_ref):
        @pl.when(pl.program_id(0) == 0)
        def _():
            dg_ref[...] = jnp.zeros_like(dg_ref)
            loss_ref[...] = jnp.zeros_like(loss_ref)

        xf = x_ref[...]
        r = lax.rsqrt(jnp.mean(xf * xf, axis=-1, keepdims=True) + EPS)
        xhat = xf * r
        gv = g_ref[...]
        err = xhat * gv - t_ref[...]
        row_loss = jnp.mean(err * err, axis=-1, keepdims=True)
        loss_ref[...] += 0.5 * jnp.sum(row_loss, axis=0, keepdims=True)
        dy = err * (1.0 / D)
        dxh = dy * gv
        dx_ref[...] = r * (dxh - xhat * jnp.mean(dxh * xhat, axis=-1, keepdims=True))
        dg_ref[...] += jnp.sum(dy * xhat, axis=0, keepdims=True)

    row = pl.BlockSpec((tm, D), lambda i: (i, 0))
    vec = pl.BlockSpec((1, D), lambda i: (0, 0))
    return pl.pallas_call(
        body, name=name, grid=(T // tm,),
        in_specs=[row, vec, row],
        out_specs=[row, vec, pl.BlockSpec((1, 128), lambda i: (0, 0))],
        out_shape=[jax.ShapeDtypeStruct((T, D), F32), jax.ShapeDtypeStruct((1, D), F32),
                   jax.ShapeDtypeStruct((1, 128), F32)],
        compiler_params=_cparams(("arbitrary",)),
    )(x, g, tgt)


def _swap_halves(t):
    half = t.shape[-1] // 2
    return jnp.concatenate([t[:, half:], t[:, :half]], axis=-1)


def _rope(name, t, cos2, sgn_sin, out_dtype):
    H, T, R = t.shape
    tm = _pick(T, (1024, 512, 256, 128))

    def body(t_ref, c_ref, s_ref, o_ref):
        tf = t_ref[...].astype(F32)
        o_ref[...] = (tf * c_ref[...] + _swap_halves(tf) * s_ref[...]).astype(o_ref.dtype)

    slab = pl.BlockSpec((None, tm, R), lambda h, i: (h, i, 0))
    tab = pl.BlockSpec((tm, R), lambda h, i: (i, 0))
    return pl.pallas_call(
        body, name=name, grid=(H, T // tm),
        in_specs=[slab, tab, tab], out_specs=slab,
        out_shape=jax.ShapeDtypeStruct((H, T, R), out_dtype),
        compiler_params=_cparams(("parallel", "parallel")),
    )(t, cos2, sgn_sin)


def _mla_q_prep(name, qf, cos2, sgn_sin, scale):
    H, T, W = qf.shape
    R = cos2.shape[1]
    tm = _pick(T, (1024, 512, 256, 128))

    def body(t_ref, c_ref, s_ref, o_ref):
        tf = t_ref[...]
        r = tf[:, W - R:]
        roped = r * c_ref[...] + _swap_halves(r) * s_ref[...]
        o_ref[...] = (jnp.concatenate([tf[:, :W - R], roped], axis=-1) * scale).astype(o_ref.dtype)

    slab = pl.BlockSpec((None, tm, W), lambda h, i: (h, i, 0))
    tab = pl.BlockSpec((tm, R), lambda h, i: (i, 0))
    return pl.pallas_call(
        body, name=name, grid=(H, T // tm),
        in_specs=[slab, tab, tab], out_specs=slab,
        out_shape=jax.ShapeDtypeStruct((H, T, W), BF16),
        compiler_params=_cparams(("parallel", "parallel")),
    )(qf, cos2, sgn_sin)


def _rope_bwd(name, dy, cos2, sgn_sin, sum_heads):
    H, T, R = dy.shape
    tm = _pick(T, (1024, 512, 256, 128))

    def body(d_ref, c_ref, s_ref, o_ref):
        d = d_ref[...]
        if sum_heads:
            tot = d[0]
            for h in range(1, H):
                tot = tot + d[h]
            d = tot
        o_ref[...] = d * c_ref[...] + _swap_halves(d * s_ref[...])

    if sum_heads:
        grid = (T // tm,)
        in_slab = pl.BlockSpec((H, tm, R), lambda i: (0, i, 0))
        out_slab = pl.BlockSpec((tm, R), lambda i: (i, 0))
        tab = pl.BlockSpec((tm, R), lambda i: (i, 0))
        out_shape = jax.ShapeDtypeStruct((T, R), F32)
        sem = ("parallel",)
    else:
        grid = (H, T // tm)
        in_slab = pl.BlockSpec((None, tm, R), lambda h, i: (h, i, 0))
        out_slab = in_slab
        tab = pl.BlockSpec((tm, R), lambda h, i: (i, 0))
        out_shape = jax.ShapeDtypeStruct((H, T, R), F32)
        sem = ("parallel", "parallel")
    return pl.pallas_call(
        body, name=name, grid=grid, in_specs=[in_slab, tab, tab], out_specs=out_slab,
        out_shape=out_shape, compiler_params=_cparams(sem),
    )(dy, cos2, sgn_sin)


def _log_sigmoid(z):
    return jnp.minimum(z, 0.0) - jnp.log(1.0 + jnp.exp(-jnp.abs(z)))


def _gate_cumsum(name, fl, b, tb):
    H, T = fl.shape

    def body(f_ref, b_ref, c_ref, carry):
        @pl.when(pl.program_id(0) == 0)
        def _():
            carry[...] = jnp.zeros_like(carry)

        ls = _log_sigmoid(f_ref[...] + b_ref[...])
        src = lax.broadcasted_iota(jnp.int32, (tb, tb), 0)
        dst = lax.broadcasted_iota(jnp.int32, (tb, tb), 1)
        tri = (src <= dst).astype(F32)
        c = lax.dot_general(ls, tri, NN, precision=lax.Precision.HIGHEST,
                            preferred_element_type=F32) + carry[...]
        c_ref[...] = c
        carry[...] = carry[...] + jnp.sum(ls, axis=-1, keepdims=True)

    return pl.pallas_call(
        body, name=name, grid=(T // tb,),
        in_specs=[pl.BlockSpec((H, tb), lambda i: (0, i)), pl.BlockSpec((H, 1), lambda i: (0, 0))],
        out_specs=pl.BlockSpec((H, tb), lambda i: (0, i)),
        out_shape=jax.ShapeDtypeStruct((H, T), F32),
        scratch_shapes=[pltpu.VMEM((H, 1), F32)],
        compiler_params=_cparams(("arbitrary",)),
    )(fl, b)


def _gate_cumsum_bwd(name, d_query, d_key, fl, b, tb):
    H, T = fl.shape
    nb = T // tb

    def body(dq_ref, dk_ref, f_ref, b_ref, dfl_ref, db_ref, carry):
        @pl.when(pl.program_id(0) == 0)
        def _():
            carry[...] = jnp.zeros_like(carry)
            db_ref[...] = jnp.zeros_like(db_ref)

        d = dq_ref[...] - dk_ref[...]
        src = lax.broadcasted_iota(jnp.int32, (tb, tb), 0)
        dst = lax.broadcasted_iota(jnp.int32, (tb, tb), 1)
        tri = (src >= dst).astype(F32)
        dls = lax.dot_general(d, tri, NN, precision=lax.Precision.HIGHEST,
                              preferred_element_type=F32) + carry[...]
        z = f_ref[...] + b_ref[...]
        dfl = dls * (1.0 / (1.0 + jnp.exp(z)))
        dfl_ref[...] = dfl
        db_ref[...] += jnp.sum(dfl, axis=-1, keepdims=True)
        carry[...] = carry[...] + jnp.sum(d, axis=-1, keepdims=True)

    blk = pl.BlockSpec((H, tb), lambda i: (0, nb - 1 - i))
    vec = pl.BlockSpec((H, 1), lambda i: (0, 0))
    return pl.pallas_call(
        body, name=name, grid=(nb,),
        in_specs=[blk, blk, blk, vec], out_specs=[blk, vec],
        out_shape=[jax.ShapeDtypeStruct((H, T), F32), jax.ShapeDtypeStruct((H, 1), F32)],
        scratch_shapes=[pltpu.VMEM((H, 1), F32)],
        compiler_params=_cparams(("arbitrary",)),
    )(d_query, d_key, fl, b)


def _causal_mask(tq, rows_are_queries):
    r = lax.broadcasted_iota(jnp.int32, (tq, tq), 0)
    c = lax.broadcasted_iota(jnp.int32, (tq, tq), 1)
    return (c <= r) if rows_are_queries else (r <= c)


def _chunk_rows(j, tq):
    return pl.ds(pl.multiple_of(j * tq, tq), tq)


def _flash_fwd(name, q, k, v_aug, dv, tq):
    H, T, dqk = q.shape
    dva = v_aug.shape[2]
    nq = T // tq

    def body(q_ref, k_ref, v_ref, o_ref, lse_ref, m_sc, acc_sc):
        qi = pl.program_id(1)
        m_sc[...] = jnp.full_like(m_sc, NEG)
        acc_sc[...] = jnp.zeros_like(acc_sc)

        def chunk(j, masked):
            rows = _chunk_rows(j, tq)
            s = _dot(q_ref[...], k_ref[rows, :], NT)
            if masked:
                s = jnp.where(_causal_mask(tq, True), s, NEG)
            m_prev = m_sc[...]
            m_new = jnp.maximum(m_prev, jnp.max(s, axis=1, keepdims=True))
            p = jnp.exp(s - jnp.tile(m_new, (1, tq // 128)))
            alpha = jnp.tile(jnp.exp(m_prev - m_new), (1, dva // 128))
            acc_sc[...] = alpha * acc_sc[...] + _dot(p.astype(BF16), v_ref[rows, :], NN)
            m_sc[...] = m_new

        def off_diagonal(j, carry):
            chunk(j, False)
            return carry

        lax.fori_loop(0, qi, off_diagonal, 0)
        chunk(qi, True)
        acc = acc_sc[...]
        l = acc[:, dv:dv + 1]
        o_ref[...] = acc[:, :dv] / l
        lse_ref[...] = m_sc[:, :1] + jnp.log(l)

    return pl.pallas_call(
        body, name=name, grid=(H, nq),
        in_specs=[pl.BlockSpec((None, tq, dqk), lambda h, i: (h, i, 0)),
                  pl.BlockSpec((None, T, dqk), lambda h, i: (h, 0, 0)),
                  pl.BlockSpec((None, T, dva), lambda h, i: (h, 0, 0))],
        out_specs=[pl.BlockSpec((None, tq, dv), lambda h, i: (h, i, 0)),
                   pl.BlockSpec((None, tq, 1), lambda h, i: (h, i, 0))],
        out_shape=[jax.ShapeDtypeStruct((H, T, dv), F32), jax.ShapeDtypeStruct((H, T, 1), F32)],
        scratch_shapes=[pltpu.VMEM((tq, 128), F32), pltpu.VMEM((tq, dva), F32)],
        compiler_params=_cparams(("parallel", "arbitrary")),
    )(q, k, v_aug)


def _row_dot(name, a, b):
    H, T, d = a.shape
    tm = _pick(T, (1024, 512, 256, 128))

    def body(a_ref, b_ref, o_ref):
        o_ref[...] = jnp.sum(a_ref[...].astype(F32) * b_ref[...].astype(F32), axis=-1, keepdims=True)

    slab = pl.BlockSpec((None, tm, d), lambda h, i: (h, i, 0))
    return pl.pallas_call(
        body, name=name, grid=(H, T // tm), in_specs=[slab, slab],
        out_specs=pl.BlockSpec((None, tm, 1), lambda h, i: (h, i, 0)),
        out_shape=jax.ShapeDtypeStruct((H, T, 1), F32),
        compiler_params=_cparams(("parallel", "parallel")),
    )(a, b)


def _flash_dq(name, q, k, v, do, scale, tq):
    H, T, dqk = q.shape
    dva = v.shape[2]
    nq = T // tq

    def body(q_ref, k_ref, v_ref, do_ref, dq_ref, acc_sc):
        qi = pl.program_id(1)
        acc_sc[...] = jnp.zeros_like(acc_sc)

        def chunk(j, masked):
            rows = _chunk_rows(j, tq)
            kb = k_ref[rows, :]
            s = _dot(q_ref[...], kb, NT)
            if masked:
                s = jnp.where(_causal_mask(tq, True), s, NEG)
            ds = jnp.exp(s) * _dot(do_ref[...], v_ref[rows, :], NT)
            acc_sc[...] += _dot(ds.astype(BF16), kb, NN)

        def off_diagonal(j, carry):
            chunk(j, False)
            return carry

        lax.fori_loop(0, qi, off_diagonal, 0)
        chunk(qi, True)
        dq_ref[...] = acc_sc[...] * scale

    q_spec = pl.BlockSpec((None, tq, dqk), lambda h, i: (h, i, 0))
    return pl.pallas_call(
        body, name=name, grid=(H, nq),
        in_specs=[q_spec,
                  pl.BlockSpec((None, T, dqk), lambda h, i: (h, 0, 0)),
                  pl.BlockSpec((None, T, dva), lambda h, i: (h, 0, 0)),
                  pl.BlockSpec((None, tq, dva), lambda h, i: (h, i, 0))],
        out_specs=q_spec,
        out_shape=jax.ShapeDtypeStruct((H, T, dqk), F32),
        scratch_shapes=[pltpu.VMEM((tq, dqk), F32)],
        compiler_params=_cparams(("parallel", "arbitrary")),
    )(q, k, v, do)


def _flash_dkv(name, q, k, v, do, tq):
    H, T, dqk = q.shape
    dva = v.shape[2]
    nq = T // tq

    def body(q_ref, k_ref, v_ref, do_ref, dk_ref, dv_ref, dk_sc, dv_sc):
        ki = pl.program_id(1)
        dk_sc[...] = jnp.zeros_like(dk_sc)
        dv_sc[...] = jnp.zeros_like(dv_sc)

        def chunk(i, masked):
            rows = _chunk_rows(i, tq)
            qb = q_ref[rows, :]
            dob = do_ref[rows, :]
            st = _dot(k_ref[...], qb, NT)
            if masked:
                st = jnp.where(_causal_mask(tq, False), st, NEG)
            pt = jnp.exp(st)
            dv_sc[...] += _dot(pt.astype(BF16), dob, NN)
            dst = pt * _dot(v_ref[...], dob, NT)
            dk_sc[...] += _dot(dst.astype(BF16), qb, NN)

        def off_diagonal(i, carry):
            chunk(i, False)
            return carry

        chunk(ki, True)
        lax.fori_loop(ki + 1, nq, off_diagonal, 0)
        dk_ref[...] = dk_sc[...]
        dv_ref[...] = dv_sc[...]

    k_spec = pl.BlockSpec((None, tq, dqk), lambda h, j: (h, j, 0))
    v_spec = pl.BlockSpec((None, tq, dva), lambda h, j: (h, j, 0))
    return pl.pallas_call(
        body, name=name, grid=(H, nq),
        in_specs=[pl.BlockSpec((None, T, dqk), lambda h, j: (h, 0, 0)), k_spec, v_spec,
                  pl.BlockSpec((None, T, dva), lambda h, j: (h, 0, 0))],
        out_specs=[k_spec, v_spec],
        out_shape=[jax.ShapeDtypeStruct((H, T, dqk), F32), jax.ShapeDtypeStruct((H, T, dva), F32)],
        scratch_shapes=[pltpu.VMEM((tq, dqk), F32), pltpu.VMEM((tq, dva), F32)],
        compiler_params=_cparams(("parallel", "arbitrary")),
    )(q, k, v, do)


def _adamw_math(w, g, m, v):
    m = ADAM_B1 * m + (1.0 - ADAM_B1) * g
    v = ADAM_B2 * v + (1.0 - ADAM_B2) * (g * g)
    m_hat = m / (1.0 - ADAM_B1 ** ADAM_STEP)
    v_hat = v / (1.0 - ADAM_B2 ** ADAM_STEP)
    delta = -ADAM_LR * (m_hat / (jnp.sqrt(v_hat) + ADAM_EPS) + ADAM_WD * w)
    return delta, m, v


def _adamw(name, parts, w, m, v):
    P, R, C = parts.shape
    tr = _pick(R, (256, 128, 64, 32, 16, 8))

    def body(p_ref, w_ref, m_ref, v_ref, g_out, d_out, m_out, v_out):
        g = p_ref[0]
        for i in range(1, P):
            g = g + p_ref[i]
        delta, m_new, v_new = _adamw_math(w_ref[...], g, m_ref[...], v_ref[...])
        g_out[...] = g
        d_out[...] = delta
        m_out[...] = m_new
        v_out[...] = v_new

    blk = pl.BlockSpec((tr, C), lambda i: (i, 0))
    sds = jax.ShapeDtypeStruct((R, C), F32)
    return pl.pallas_call(
        body, name=name, grid=(R // tr,),
        in_specs=[pl.BlockSpec((P, tr, C), lambda i: (0, i, 0)), blk, blk, blk],
        out_specs=[blk] * 4, out_shape=[sds] * 4,
        compiler_params=_cparams(("parallel",)),
    )(parts, w, m, v)


def _my_position():
    return lax.axis_index("x"), lax.axis_index("y"), lax.axis_index("c")


def _slot(p):
    return 4 * p[0] + 2 * p[1] + p[2]


def _flip(p, k):
    return tuple((1 - p[i]) if (k >> (2 - i)) & 1 else p[i] for i in range(3))


def _allgather_weights(shards):
    n = len(shards)

    def body(*refs):
        ins = refs[:n]
        outs = refs[n:2 * n]
        send_sems, recv_sems, local_sems = refs[2 * n:]
        x, y, c = _my_position()
        me, sibling = (x, y, c), (x, y, 1 - c)
        chips = [(1 - x, y), (x, 1 - y), (1 - x, 1 - y)]

        def copy(a, k, block, to, src=None):
            dst = outs[a].at[_slot(block)]
            return pltpu.make_async_remote_copy(
                src_ref=dst if src is None else src, dst_ref=dst,
                send_sem=send_sems.at[7 * a + k], recv_sem=recv_sems.at[7 * a + k],
                device_id=to, device_id_type=MESH)

        started = []
        for a in range(n):
            mine = pltpu.make_async_copy(ins[a], outs[a].at[_slot(me)], local_sems.at[a])
            mine.start()
            started.append(mine)
        first = []
        for a in range(n):
            first.append(copy(a, 0, me, sibling, src=ins[a]))
            first += [copy(a, 1 + j, me, (*chip, c), src=ins[a]) for j, chip in enumerate(chips)]
        for cp in first:
            cp.start()
        passed = []
        for j, chip in enumerate(chips):
            for a in range(n):
                copy(a, 1 + j, (*chip, c), me).wait_recv()
                fwd = copy(a, 4 + j, (*chip, c), sibling)
                fwd.start()
                passed.append(fwd)
        for a in range(n):
            copy(a, 0, sibling, me).wait_recv()
            for j, chip in enumerate(chips):
                copy(a, 4 + j, (*chip, 1 - c), me).wait_recv()
        for cp in first + passed:
            cp.wait_send()
        for mine in started:
            mine.wait()

    hbm = pl.BlockSpec(memory_space=pl.ANY)
    return pl.pallas_call(
        body, name="allgather_weights",
        in_specs=[hbm] * n, out_specs=[hbm] * n,
        out_shape=[jax.ShapeDtypeStruct((N_DEV,) + s.shape, s.dtype) for s in shards],
        scratch_shapes=[pltpu.SemaphoreType.DMA((7 * n,)), pltpu.SemaphoreType.DMA((7 * n,)),
                        pltpu.SemaphoreType.DMA((n,))],
        compiler_params=pltpu.CompilerParams(has_side_effects=True),
    )(*shards)


def _alltoall_grads(grads):
    n = len(grads)

    def body(*refs):
        ins = refs[:n]
        outs = refs[n:2 * n]
        send_sems, recv_sems, local_sems = refs[2 * n:]
        me = _my_position()
        started = []
        for a in range(n):
            mine = pltpu.make_async_copy(ins[a].at[_slot(me)], outs[a].at[_slot(me)], local_sems.at[a])
            mine.start()
            started.append(mine)
        sends = []
        for k in range(1, N_DEV):
            peer = _flip(me, k)
            for a in range(n):
                cp = pltpu.make_async_remote_copy(
                    src_ref=ins[a].at[_slot(peer)], dst_ref=outs[a].at[_slot(me)],
                    send_sem=send_sems.at[7 * a + k - 1], recv_sem=recv_sems.at[7 * a + k - 1],
                    device_id=peer, device_id_type=MESH)
                cp.start()
                sends.append(cp)
        for k in range(1, N_DEV):
            peer = _flip(me, k)
            for a in range(n):
                pltpu.make_async_remote_copy(
                    src_ref=ins[a].at[_slot(peer)], dst_ref=outs[a].at[_slot(peer)],
                    send_sem=send_sems.at[7 * a + k - 1], recv_sem=recv_sems.at[7 * a + k - 1],
                    device_id=peer, device_id_type=MESH).wait_recv()
        for cp in sends:
            cp.wait_send()
        for mine in started:
            mine.wait()

    hbm = pl.BlockSpec(memory_space=pl.ANY)
    return pl.pallas_call(
        body, name="alltoall_grads",
        in_specs=[hbm] * n, out_specs=[hbm] * n,
        out_shape=[jax.ShapeDtypeStruct(g.shape, g.dtype) for g in grads],
        scratch_shapes=[pltpu.SemaphoreType.DMA((7 * n,)), pltpu.SemaphoreType.DMA((7 * n,)),
                        pltpu.SemaphoreType.DMA((n,))],
        compiler_params=pltpu.CompilerParams(has_side_effects=True),
    )(*grads)


def _allreduce_small(v):
    R, C = v.shape

    def body(v_ref, o_ref, buf, send_sems, recv_sems):
        me = _my_position()
        buf[_slot(me)] = v_ref[...]
        sends = []
        for k in range(1, N_DEV):
            peer = _flip(me, k)
            cp = pltpu.make_async_remote_copy(
                src_ref=v_ref, dst_ref=buf.at[_slot(me)],
                send_sem=send_sems.at[k - 1], recv_sem=recv_sems.at[k - 1],
                device_id=peer, device_id_type=MESH)
            cp.start()
            sends.append(cp)
        for k in range(1, N_DEV):
            peer = _flip(me, k)
            pltpu.make_async_remote_copy(
                src_ref=v_ref, dst_ref=buf.at[_slot(peer)],
                send_sem=send_sems.at[k - 1], recv_sem=recv_sems.at[k - 1],
                device_id=peer, device_id_type=MESH).wait_recv()
        for cp in sends:
            cp.wait_send()
        tot = buf[0]
        for s in range(1, N_DEV):
            tot = tot + buf[s]
        o_ref[...] = tot

    vm = pl.BlockSpec(memory_space=pltpu.VMEM)
    return pl.pallas_call(
        body, name="allreduce_small",
        in_specs=[vm], out_specs=vm, out_shape=jax.ShapeDtypeStruct((R, C), F32),
        scratch_shapes=[pltpu.VMEM((N_DEV, R, C), F32), pltpu.SemaphoreType.DMA((7,)),
                        pltpu.SemaphoreType.DMA((7,))],
        compiler_params=pltpu.CompilerParams(has_side_effects=True),
    )(v)


def _to_heads(t, heads):
    T = t.shape[0]
    return t.reshape(T, heads, t.shape[1] // heads).transpose(1, 0, 2)


def _from_heads(t):
    H, T, d = t.shape
    return t.transpose(1, 0, 2).reshape(T, H * d)


def _split3(t):
    hi = lax.reduce_precision(t, 8, 7)
    r = t - hi
    mid = lax.reduce_precision(r, 8, 7)
    lo = lax.reduce_precision(r - mid, 8, 7)
    return jnp.concatenate([hi, mid, lo], axis=-1).astype(BF16)


def _pad_cols(t, n):
    return jnp.pad(t, ((0, 0), (0, n - t.shape[1])))


def _pack_small(mix, ffn, kv, fin, kva, qa, bf, last):
    row6 = jnp.concatenate([kva.reshape(-1), qa.reshape(-1), bf.reshape(-1),
                            jnp.zeros((D_MODEL - KV_LORA - Q_LORA - FOX_HEADS,), F32)])
    return jnp.stack([mix[0], mix[1], ffn[0], ffn[1], kv.reshape(-1), fin.reshape(-1), row6, last])


def _unpack_small(p):
    mix = p[0:2]
    ffn = p[2:4]
    kv = p[4]
    fin = p[5]
    kva = p[6, :KV_LORA]
    qa = p[6, KV_LORA:KV_LORA + Q_LORA].reshape(1, Q_LORA)
    bf = p[6, KV_LORA + Q_LORA:KV_LORA + Q_LORA + FOX_HEADS].reshape(1, FOX_HEADS)
    return mix, ffn, bf, kv, kva, qa, fin


def _mlp_fwd(tag, xin, g, w_up, w_down):
    h = _rms(f"{tag}_norm", xin, g, BF16)

    def act(acc):
        r = jnp.maximum(acc, 0.0)
        return acc, r * r

    u, a = _mm(f"{tag}_up", h, w_up, "nn", (BF16, BF16), epi=act)
    xout = _mm(f"{tag}_down", a, w_down, "nn", (F32,), epi=lambda acc, r: (acc + r,), extras=(xin,))
    return xout, (h, u, a)


def _mlp_bwd(tag, gout, xin, g, w_up, w_down, saved):
    h, u, a = saved
    dw_down = _mm_tn(f"{tag}_dwdown", a, gout)
    du = _mm(f"{tag}_du", gout, w_down, "nt", (BF16,),
             epi=lambda acc, uu: (acc * (2.0 * jnp.maximum(uu.astype(F32), 0.0)),), extras=(u,))
    dw_up = _mm_tn(f"{tag}_dwup", h, du)
    dh = _mm(f"{tag}_dh", du, w_up, "nt", (F32,))
    gin, dg = _rms_bwd(f"{tag}_norm_bwd", xin, g, dh, dres=gout)
    return gin, dg, dw_up, dw_down


def kernel(x, norm_mix_g, norm_ffn_g, fox_w_in, fox_b_f, fox_w_out, kv_norm_g, mla_w_kv_a, mla_kv_a_norm_g, mla_w_kv_b, mla_w_q_a, mla_q_a_norm_g, mla_w_q_b, mla_w_out, ffn_w_up, ffn_w_down, final_norm_g, loss_target, m_norm_mix_g, m_norm_ffn_g, m_fox_w_in, m_fox_b_f, m_fox_w_out, m_kv_norm_g, m_mla_w_kv_a, m_mla_kv_a_norm_g, m_mla_w_kv_b, m_mla_w_q_a, m_mla_q_a_norm_g, m_mla_w_q_b, m_mla_w_out, m_ffn_w_up, m_ffn_w_down, m_final_norm_g, v_norm_mix_g, v_norm_ffn_g, v_fox_w_in, v_fox_b_f, v_fox_w_out, v_kv_norm_g, v_mla_w_kv_a, v_mla_kv_a_norm_g, v_mla_w_kv_b, v_mla_w_q_a, v_mla_q_a_norm_g, v_mla_w_q_b, v_mla_w_out, v_ffn_w_up, v_ffn_w_down, v_final_norm_g):
    T = x.shape[1]
    D = D_MODEL
    tq = 512 if T >= 2048 else 128
    x0 = x[0]
    tgt = loss_target[0]

    shards = [fox_w_in[0], fox_w_out[0], mla_w_kv_a, mla_w_kv_b, mla_w_q_a[0], mla_w_q_b[0],
              mla_w_out[0], ffn_w_up, ffn_w_down]
    gat = _allgather_weights([s.astype(BF16) for s in shards])
    w_in = gat[0].transpose(1, 0, 2).reshape(D, 3 * D + FOX_HEADS)
    w_qkv = w_in[:, :3 * D]
    w_f = _pad_cols(w_in[:, 3 * D:], 128)
    w_fo = gat[1].reshape(D, D)
    w_kva = _pad_cols(gat[2].reshape(D, KV_LORA + QK_ROPE), KV_A_PAD)
    w_kvb = gat[3].transpose(1, 0, 2).reshape(KV_LORA, MLA_HEADS * (QK_NOPE + V_HEAD))
    w_qa = gat[4].reshape(D, Q_LORA)
    w_qb = gat[5].transpose(1, 0, 2).reshape(Q_LORA, MLA_HEADS * (QK_NOPE + QK_ROPE))
    w_mo = gat[6].reshape(D, D)
    w_up = gat[7].transpose(1, 2, 0, 3).reshape(2, D, D_FF)
    w_down = gat[8].transpose(1, 0, 2, 3).reshape(2, D_FF, D)

    g_mix0, g_mix1 = norm_mix_g[0:1], norm_mix_g[1:2]
    g_ffn0, g_ffn1 = norm_ffn_g[0:1], norm_ffn_g[1:2]
    g_kv = kv_norm_g.reshape(1, D)
    g_kva = mla_kv_a_norm_g.reshape(1, KV_LORA)
    g_qa = mla_q_a_norm_g.reshape(1, Q_LORA)
    g_fin = final_norm_g.reshape(1, D)

    inv = 1.0 / (ROPE_BASE ** (jnp.arange(0, QK_ROPE, 2, dtype=F32) / QK_ROPE))
    ang = jnp.arange(T, dtype=F32)[:, None] * inv[None, :]
    cos, sin = jnp.cos(ang), jnp.sin(ang)
    cos2 = jnp.concatenate([cos, cos], axis=-1)
    sgn_sin = jnp.concatenate([-sin, sin], axis=-1)

    h0 = _rms("l0_mix_norm", x0, g_mix0, BF16)
    qkv = _mm("fox_qkv", h0, w_qkv, "nn", (BF16,))
    fl_pad = _mm("fox_gate_logit", h0, w_f, "nn", (F32,))
    fq = _to_heads(qkv[:, :D], FOX_HEADS)
    fk = _to_heads(qkv[:, D:2 * D], FOX_HEADS)
    fv = _to_heads(qkv[:, 2 * D:], FOX_HEADS)
    fl = fl_pad[:, :FOX_HEADS].T
    b_f = fox_b_f.reshape(FOX_HEADS, 1)
    cgate = _gate_cumsum("fox_gate_scan", fl, b_f, tq)
    fox_scale = FOX_HEAD_DIM ** -0.5
    one3 = jnp.ones((FOX_HEADS, T, 3), BF16)
    fpad = lambda n: jnp.zeros((FOX_HEADS, T, n), BF16)
    fqs = fq * fox_scale
    fq_aug = jnp.concatenate([fqs, one3, fpad(FOX_AUG - FOX_HEAD_DIM - 3)], axis=-1)
    fk_aug = jnp.concatenate([fk, _split3((-cgate)[..., None]), one3[..., :1] * (1.0 / fox_scale), one3,
                              fpad(FOX_AUG - FOX_HEAD_DIM - 7)], axis=-1)
    fv_aug = jnp.concatenate([fv, one3, fpad(FOX_AUG - FOX_HEAD_DIM - 3)], axis=-1)
    fo, flse = _flash_fwd("fox_attn", fq_aug, fk_aug, fv_aug, FOX_HEAD_DIM, tq)
    fctx = _from_heads(fo).astype(BF16)
    x1 = _mm("fox_out", fctx, w_fo, "nn", (F32,), epi=lambda acc, r: (acc + r,), extras=(x0,))
    x2, mlp0 = _mlp_fwd("l0_ffn", x1, g_ffn0, w_up[0], w_down[0])

    src = _rms("kv_norm", x2, g_kv, BF16)
    kva = _mm("kv_a", src, w_kva, "nn", (F32,))
    kva_lat = kva[:, :KV_LORA]
    c_kv = _rms("kv_a_norm", kva_lat, g_kva, BF16)
    k_rope = _rope("k_rope", kva[:, KV_LORA:KV_LORA + QK_ROPE][None], cos2, sgn_sin, BF16)
    kvb = _mm("kv_b", c_kv, w_kvb, "nn", (BF16,))
    kvb_h = _to_heads(kvb, MLA_HEADS)
    mk = jnp.concatenate([kvb_h[:, :, :QK_NOPE],
                          jnp.broadcast_to(k_rope, (MLA_HEADS, T, QK_ROPE))], axis=-1)
    mv = kvb_h[:, :, QK_NOPE:]

    h1 = _rms("l1_mix_norm", x2, g_mix1, BF16)
    qa = _mm("q_a", h1, w_qa, "nn", (F32,))
    c_q = _rms("q_a_norm", qa, g_qa, BF16)
    qf = _mm("q_b", c_q, w_qb, "nn", (F32,))
    qf_h = _to_heads(qf, MLA_HEADS)
    mla_scale = (QK_NOPE + QK_ROPE) ** -0.5
    mq = _mla_q_prep("q_prep", qf_h, cos2, sgn_sin, mla_scale)
    mone3 = jnp.ones((MLA_HEADS, T, 3), BF16)
    mpad = lambda n: jnp.zeros((MLA_HEADS, T, n), BF16)
    mv_aug = jnp.concatenate([mv, mone3, mpad(MLA_AUG - V_HEAD - 3)], axis=-1)
    mo, mlse = _flash_fwd("mla_attn", mq, mk, mv_aug, V_HEAD, tq)
    mctx = _from_heads(mo).astype(BF16)
    x3 = _mm("mla_out", mctx, w_mo, "nn", (F32,), epi=lambda acc, r: (acc + r,), extras=(x2,))
    x4, mlp1 = _mlp_fwd("l1_ffn", x3, g_ffn1, w_up[1], w_down[1])

    g4, dg_fin, loss_vec = _loss_head("loss_head", x4, g_fin, tgt)

    g3, dg_ffn1, dw_up1, dw_down1 = _mlp_bwd("l1_ffn", g4, x3, g_ffn1, w_up[1], w_down[1], mlp1)

    dw_mo = _mm_tn("mla_out_dw", mctx, g3)
    dmo = _to_heads(_mm("mla_out_dx", g3, w_mo, "nt", (BF16,)), MLA_HEADS)
    mdelta = _row_dot("mla_delta", mo, dmo)
    dqk = QK_NOPE + QK_ROPE
    mq_bwd = jnp.concatenate([mq, _split3(-mlse), mpad(MLA_AUG - dqk - 3)], axis=-1)
    mk_bwd = jnp.concatenate([mk, mone3, mpad(MLA_AUG - dqk - 3)], axis=-1)
    mdo_aug = jnp.concatenate([dmo, _split3(-mdelta), mpad(MLA_AUG - V_HEAD - 3)], axis=-1)
    mdq = _flash_dq("mla_attn_dq", mq_bwd, mk_bwd, mv_aug, mdo_aug, mla_scale, tq)[:, :, :dqk]
    mdk, mdv = _flash_dkv("mla_attn_dkv", mq_bwd, mk_bwd, mv_aug, mdo_aug, tq)
    mdk = mdk[:, :, :dqk]
    mdv = mdv[:, :, :V_HEAD]
    dq_rope = _rope_bwd("q_rope_bwd", mdq[:, :, QK_NOPE:], cos2, sgn_sin, False)
    dqf = _from_heads(jnp.concatenate([mdq[:, :, :QK_NOPE], dq_rope], axis=-1))
    dw_qb = _mm_tn("q_b_dw", c_q, dqf)
    dc_q = _mm("q_b_dx", dqf, w_qb, "nt", (F32,))
    dqa, dg_qa = _rms_bwd("q_a_norm_bwd", qa, g_qa, dc_q)
    dw_qa = _mm_tn("q_a_dw", h1, dqa)
    dh1 = _mm("q_a_dx", dqa, w_qa, "nt", (F32,))
    g2a, dg_mix1 = _rms_bwd("l1_mix_norm_bwd", x2, g_mix1, dh1, dres=g3)

    dk_rope = _rope_bwd("k_rope_bwd", mdk[:, :, QK_NOPE:], cos2, sgn_sin, True)
    dkvb = _from_heads(jnp.concatenate([mdk[:, :, :QK_NOPE], mdv], axis=-1))
    dw_kvb = _mm_tn("kv_b_dw", c_kv, dkvb)
    dc_kv = _mm("kv_b_dx", dkvb, w_kvb, "nt", (F32,))
    dkva_lat, dg_kva = _rms_bwd("kv_a_norm_bwd", kva_lat, g_kva, dc_kv)
    dkva = _pad_cols(jnp.concatenate([dkva_lat, dk_rope], axis=-1), KV_A_PAD)
    dw_kva = _mm_tn("kv_a_dw", src, dkva)[:, :KV_LORA + QK_ROPE]
    dsrc = _mm("kv_a_dx", dkva, w_kva, "nt", (F32,))
    g2, dg_kv = _rms_bwd("kv_norm_bwd", x2, g_kv, dsrc, dres=g2a)

    g1, dg_ffn0, dw_up0, dw_down0 = _mlp_bwd("l0_ffn", g2, x1, g_ffn0, w_up[0], w_down[0], mlp0)

    dw_fo = _mm_tn("fox_out_dw", fctx, g1)
    dfo = _to_heads(_mm("fox_out_dx", g1, w_fo, "nt", (BF16,)), FOX_HEADS)
    fdelta = _row_dot("fox_delta", fo, dfo)
    fq_bwd = jnp.concatenate([fqs, one3, fpad(1), _split3(-flse), fpad(FOX_AUG - FOX_HEAD_DIM - 7)], axis=-1)
    fdo_aug = jnp.concatenate([dfo, _split3(-fdelta), fpad(FOX_AUG - FOX_HEAD_DIM - 3)], axis=-1)
    fdq_aug = _flash_dq("fox_attn_dq", fq_bwd, fk_aug, fv_aug, fdo_aug, fox_scale, tq)
    fdk_aug, fdv_aug = _flash_dkv("fox_attn_dkv", fq_bwd, fk_aug, fv_aug, fdo_aug, tq)
    fdq = fdq_aug[:, :, :FOX_HEAD_DIM]
    fdk = fdk_aug[:, :, :FOX_HEAD_DIM]
    fdv = fdv_aug[:, :, :FOX_HEAD_DIM]
    dfl, db_f = _gate_cumsum_bwd("fox_gate_scan_bwd", fdq_aug[:, :, FOX_HEAD_DIM + 3],
                                 fdk_aug[:, :, FOX_HEAD_DIM], fl, b_f, tq)
    dqkv = jnp.concatenate([_from_heads(fdq), _from_heads(fdk), _from_heads(fdv)], axis=-1).astype(BF16)
    dfl_pad = _pad_cols(dfl.T, 128)
    dw_qkv = _mm_tn("fox_qkv_dw", h0, dqkv)
    dw_f = _mm_tn("fox_gate_dw", h0, dfl_pad)[:, :FOX_HEADS]
    dw_in = jnp.concatenate([dw_qkv, dw_f], axis=-1)
    dh0a = _mm("fox_gate_dx", dfl_pad, w_f, "nt", (F32,))
    dh0 = _mm("fox_qkv_dx", dqkv, w_qkv, "nt", (F32,), epi=lambda acc, r: (acc + r,), extras=(dh0a,))
    grad_x, dg_mix0 = _rms_bwd("l0_mix_norm_bwd", x0, g_mix0, dh0, dres=g1)

    dw_up = jnp.stack([dw_up0, dw_up1])
    dw_down = jnp.stack([dw_down0, dw_down1])
    shard_major = [
        dw_in.reshape(D, N_DEV, -1).transpose(1, 0, 2),
        dw_fo.reshape(N_DEV, D // N_DEV, D),
        dw_kva.reshape(N_DEV, D // N_DEV, KV_LORA + QK_ROPE),
        dw_kvb.reshape(KV_LORA, N_DEV, -1).transpose(1, 0, 2),
        dw_qa.reshape(N_DEV, D // N_DEV, Q_LORA),
        dw_qb.reshape(Q_LORA, N_DEV, -1).transpose(1, 0, 2),
        dw_mo.reshape(N_DEV, D // N_DEV, D),
        dw_up.reshape(2, D, N_DEV, -1).transpose(2, 0, 1, 3),
        dw_down.reshape(2, N_DEV, D_FF // N_DEV, D).transpose(1, 0, 2, 3),
    ]
    parts = _alltoall_grads(shard_major)

    names = ["fox_w_in", "fox_w_out", "mla_w_kv_a", "mla_w_kv_b", "mla_w_q_a", "mla_w_q_b",
             "mla_w_out", "ffn_w_up", "ffn_w_down"]
    moms = [m_fox_w_in, m_fox_w_out, m_mla_w_kv_a, m_mla_w_kv_b, m_mla_w_q_a, m_mla_w_q_b,
            m_mla_w_out, m_ffn_w_up, m_ffn_w_down]
    vars_ = [v_fox_w_in, v_fox_w_out, v_mla_w_kv_a, v_mla_w_kv_b, v_mla_w_q_a, v_mla_w_q_b,
             v_mla_w_out, v_ffn_w_up, v_ffn_w_down]
    full = [fox_w_in, fox_w_out, mla_w_kv_a, mla_w_kv_b, mla_w_q_a, mla_w_q_b, mla_w_out,
            ffn_w_up, ffn_w_down]
    big = {}
    for nm, p, w, m, v in zip(names, parts, full, moms, vars_):
        C = w.shape[-1]
        res = _adamw(f"adamw_{nm}", p.reshape(N_DEV, -1, C), w.reshape(-1, C), m.reshape(-1, C),
                     v.reshape(-1, C))
        big[nm] = [r.reshape(w.shape) for r in res]

    zrow = jnp.zeros((D,), F32)
    g_small = _pack_small(jnp.concatenate([dg_mix0, dg_mix1]), jnp.concatenate([dg_ffn0, dg_ffn1]),
                          dg_kv, dg_fin, dg_kva, dg_qa, db_f, zrow.at[0].set(loss_vec[0, 0]))
    tot_small = _allreduce_small(g_small)
    w_small = _pack_small(norm_mix_g, norm_ffn_g, kv_norm_g, final_norm_g, mla_kv_a_norm_g,
                          mla_q_a_norm_g, fox_b_f, zrow)
    m_small = _pack_small(m_norm_mix_g, m_norm_ffn_g, m_kv_norm_g, m_final_norm_g, m_mla_kv_a_norm_g,
                          m_mla_q_a_norm_g, m_fox_b_f, zrow)
    v_small = _pack_small(v_norm_mix_g, v_norm_ffn_g, v_kv_norm_g, v_final_norm_g, v_mla_kv_a_norm_g,
                          v_mla_q_a_norm_g, v_fox_b_f, zrow)
    small = _adamw("adamw_small", tot_small[None], w_small, m_small, v_small)
    loss = tot_small[7, 0]
    small = [_unpack_small(s) for s in small]

    def ordered(i):
        mix, ffn, bf, kv, kva, qa, fin = small[i]
        return [mix, ffn, big["fox_w_in"][i], bf, big["fox_w_out"][i], kv, big["mla_w_kv_a"][i], kva,
                big["mla_w_kv_b"][i], big["mla_w_q_a"][i], qa, big["mla_w_q_b"][i],
                big["mla_w_out"][i], big["ffn_w_up"][i], big["ffn_w_down"][i], fin]

    return (loss, grad_x[None], *ordered(0), *ordered(1), *ordered(2), *ordered(3))
```

```python
import functools
import math

import jax
import jax.numpy as jnp
from jax import lax
from jax.experimental import pallas as pl
from jax.experimental.pallas import tpu as pltpu

F32 = jnp.float32
BF16 = jnp.bfloat16
MESH = pl.DeviceIdType.MESH

N_DEV = 8
D_MODEL = 1024
FOX_HEADS = 16
FOX_HEAD_DIM = 64
FOX_AUG = 128
MLA_AUG = 256
MLA_HEADS = 8
QK_NOPE = 128
QK_ROPE = 64
V_HEAD = 128
Q_LORA = 384
KV_LORA = 256
KV_A_PAD = 384
D_FF = 4096
ROPE_BASE = 10000.0
EPS = 1e-6
NEG = -1e30

ADAM_LR = 0.001
ADAM_B1 = 0.9
ADAM_B2 = 0.999
ADAM_EPS = 1e-08
ADAM_WD = 0.01
ADAM_STEP = 10

VMEM_LIMIT_BYTES = 48 * 1024 * 1024

NN = (((1,), (0,)), ((), ()))
NT = (((1,), (1,)), ((), ()))
TN = (((0,), (0,)), ((), ()))
_FORMS = {"nn": NN, "nt": NT}


def _cparams(sem=None):
    return pltpu.CompilerParams(dimension_semantics=sem, vmem_limit_bytes=VMEM_LIMIT_BYTES)


def _pick(n, cands):
    for c in cands:
        if c <= n and n % c == 0:
            return c
    return n


def _dot(a, b, dims):
    return lax.dot_general(a, b, dims, preferred_element_type=F32)


def _mm(name, a, b, form, out_dtypes, epi=None, extras=(), tm=1024, tn=512):
    M, K = a.shape
    N = b.shape[1] if form == "nn" else b.shape[0]
    tm = _pick(M, (tm, 512, 256, 128))
    tn = _pick(N, (tn, 384, 256, 128))
    n_ex = len(extras)

    def body(*refs):
        a_ref, b_ref = refs[0], refs[1]
        ex = refs[2:2 + n_ex]
        outs = refs[2 + n_ex:]
        acc = _dot(a_ref[...].astype(BF16), b_ref[...].astype(BF16), _FORMS[form])
        res = epi(acc, *[e[...] for e in ex]) if epi is not None else (acc,)
        for o_ref, r in zip(outs, res):
            o_ref[...] = r.astype(o_ref.dtype)

    if form == "nn":
        b_spec = pl.BlockSpec((K, tn), lambda i, j: (0, j))
    else:
        b_spec = pl.BlockSpec((tn, K), lambda i, j: (j, 0))
    tile = pl.BlockSpec((tm, tn), lambda i, j: (i, j))
    out = pl.pallas_call(
        body, name=name, grid=(M // tm, N // tn),
        in_specs=[pl.BlockSpec((tm, K), lambda i, j: (i, 0)), b_spec] + [tile] * n_ex,
        out_specs=[tile] * len(out_dtypes),
        out_shape=[jax.ShapeDtypeStruct((M, N), dt) for dt in out_dtypes],
        compiler_params=_cparams(("parallel", "arbitrary")),
    )(a, b, *extras)
    return out if len(out_dtypes) > 1 else out[0]


def _mm_tn(name, a, b):
    T, Ka = a.shape
    N = b.shape[1]
    tk = _pick(Ka, (1024, 512, 384, 256, 128))
    tn = _pick(N, (1024, 768, 512, 384, 256, 128))
    tt = _pick(T, (1024, 512, 256, 128))

    def body(a_ref, b_ref, o_ref):
        @pl.when(pl.program_id(2) == 0)
        def _():
            o_ref[...] = jnp.zeros_like(o_ref)

        o_ref[...] += _dot(a_ref[...].astype(BF16), b_ref[...].astype(BF16), TN)

    return pl.pallas_call(
        body, name=name, grid=(Ka // tk, N // tn, T // tt),
        in_specs=[pl.BlockSpec((tt, tk), lambda i, j, t: (t, i)),
                  pl.BlockSpec((tt, tn), lambda i, j, t: (t, j))],
        out_specs=pl.BlockSpec((tk, tn), lambda i, j, t: (i, j)),
        out_shape=jax.ShapeDtypeStruct((Ka, N), F32),
        compiler_params=_cparams(("parallel", "parallel", "arbitrary")),
    )(a, b)


def _rms(name, x, g, out_dtype):
    T, D = x.shape
    tm = _pick(T, (1024, 512, 256, 128))

    def body(x_ref, g_ref, o_ref):
        xf = x_ref[...]
        r = lax.rsqrt(jnp.mean(xf * xf, axis=-1, keepdims=True) + EPS)
        o_ref[...] = (xf * r * g_ref[...]).astype(o_ref.dtype)

    return pl.pallas_call(
        body, name=name, grid=(T // tm,),
        in_specs=[pl.BlockSpec((tm, D), lambda i: (i, 0)), pl.BlockSpec((1, D), lambda i: (0, 0))],
        out_specs=pl.BlockSpec((tm, D), lambda i: (i, 0)),
        out_shape=jax.ShapeDtypeStruct((T, D), out_dtype),
        compiler_params=_cparams(("parallel",)),
    )(x, g)


def _rms_bwd(name, x, g, dh, dres=None):
    T, D = x.shape
    tm = _pick(T, (512, 256, 128))
    has_res = dres is not None

    def body(*refs):
        if has_res:
            x_ref, g_ref, dh_ref, dres_ref, dx_ref, dg_ref = refs
        else:
            x_ref, g_ref, dh_ref, dx_ref, dg_ref = refs

        @pl.when(pl.program_id(0) == 0)
        def _():
            dg_ref[...] = jnp.zeros_like(dg_ref)

        xf = x_ref[...]
        r = lax.rsqrt(jnp.mean(xf * xf, axis=-1, keepdims=True) + EPS)
        xhat = xf * r
        dy = dh_ref[...].astype(F32)
        dxh = dy * g_ref[...]
        dx = r * (dxh - xhat * jnp.mean(dxh * xhat, axis=-1, keepdims=True))
        if has_res:
            dx = dx + dres_ref[...]
        dx_ref[...] = dx
        dg_ref[...] += jnp.sum(dy * xhat, axis=0, keepdims=True)

    row = pl.BlockSpec((tm, D), lambda i: (i, 0))
    vec = pl.BlockSpec((1, D), lambda i: (0, 0))
    ins = [x, g, dh] + ([dres] if has_res else [])
    return pl.pallas_call(
        body, name=name, grid=(T // tm,),
        in_specs=[row, vec, row] + ([row] if has_res else []),
        out_specs=[row, vec],
        out_shape=[jax.ShapeDtypeStruct((T, D), F32), jax.ShapeDtypeStruct((1, D), F32)],
        compiler_params=_cparams(("arbitrary",)),
    )(*ins)


def _loss_head(name, x, g, tgt):
    T, D = x.shape
    tm = _pick(T, (512, 256, 128))

    def body(x_ref, g_ref, t_ref, dx_ref, dg_ref, loss_ref):
        @pl.when(pl.program_id(0) == 0)
        def _():
            dg_ref[...] = jnp.zeros_like(dg_ref)
            loss_ref[...] = jnp.zeros_like(loss_ref)

        xf = x_ref[...]
        r = lax.rsqrt(jnp.mean(xf * xf, axis=-1, keepdims=True) + EPS)
        xhat = xf * r
        gv = g_ref[...]
        err = xhat * gv - t_ref[...]
        row_loss = jnp.mean(err * err, axis=-1, keepdims=True)
        loss_ref[...] += 0.5 * jnp.sum(row_loss, axis=0, keepdims=True)
        dy = err * (1.0 / D)
        dxh = dy * gv
        dx_ref[...] = r * (dxh - xhat * jnp.mean(dxh * xhat, axis=-1, keepdims=True))
        dg_ref[...] += jnp.sum(dy * xhat, axis=0, keepdims=True)

    row = pl.BlockSpec((tm, D), lambda i: (i, 0))
    vec = pl.BlockSpec((1, D), lambda i: (0, 0))
    return pl.pallas_call(
        body, name=name, grid=(T // tm,),
        in_specs=[row, vec, row],
        out_specs=[row, vec, pl.BlockSpec((1, 128), lambda i: (0, 0))],
        out_shape=[jax.ShapeDtypeStruct((T, D), F32), jax.ShapeDtypeStruct((1, D), F32),
                   jax.ShapeDtypeStruct((1, 128), F32)],
        compiler_params=_cparams(("arbitrary",)),
    )(x, g, tgt)


def _swap_halves(t):
    half = t.shape[-1] // 2
    return jnp.concatenate([t[:, half:], t[:, :half]], axis=-1)


def _rope(name, t, cos2, sgn_sin, out_dtype):
    H, T, R = t.shape
    tm = _pick(T, (1024, 512, 256, 128))

    def body(t_ref, c_ref, s_ref, o_ref):
        tf = t_ref[...].astype(F32)
        o_ref[...] = (tf * c_ref[...] + _swap_halves(tf) * s_ref[...]).astype(o_ref.dtype)

    slab = pl.BlockSpec((None, tm, R), lambda h, i: (h, i, 0))
    tab = pl.BlockSpec((tm, R), lambda h, i: (i, 0))
    return pl.pallas_call(
        body, name=name, grid=(H, T // tm),
        in_specs=[slab, tab, tab], out_specs=slab,
        out_shape=jax.ShapeDtypeStruct((H, T, R), out_dtype),
        compiler_params=_cparams(("parallel", "parallel")),
    )(t, cos2, sgn_sin)


def _mla_q_prep(name, qf, cos2, sgn_sin, scale):
    H, T, W = qf.shape
    R = cos2.shape[1]
    tm = _pick(T, (1024, 512, 256, 128))

    def body(t_ref, c_ref, s_ref, o_ref):
        tf = t_ref[...]
        r = tf[:, W - R:]
        roped = r * c_ref[...] + _swap_halves(r) * s_ref[...]
        o_ref[...] = (jnp.concatenate([tf[:, :W - R], roped], axis=-1) * scale).astype(o_ref.dtype)

    slab = pl.BlockSpec((None, tm, W), lambda h, i: (h, i, 0))
    tab = pl.BlockSpec((tm, R), lambda h, i: (i, 0))
    return pl.pallas_call(
        body, name=name, grid=(H, T // tm),
        in_specs=[slab, tab, tab], out_specs=slab,
        out_shape=jax.ShapeDtypeStruct((H, T, W), BF16),
        compiler_params=_cparams(("parallel", "parallel")),
    )(qf, cos2, sgn_sin)


def _rope_bwd(name, dy, cos2, sgn_sin, sum_heads):
    H, T, R = dy.shape
    tm = _pick(T, (1024, 512, 256, 128))

    def body(d_ref, c_ref, s_ref, o_ref):
        d = d_ref[...]
        if sum_heads:
            tot = d[0]
            for h in range(1, H):
                tot = tot + d[h]
            d = tot
        o_ref[...] = d * c_ref[...] + _swap_halves(d * s_ref[...])

    if sum_heads:
        grid = (T // tm,)
        in_slab = pl.BlockSpec((H, tm, R), lambda i: (0, i, 0))
        out_slab = pl.BlockSpec((tm, R), lambda i: (i, 0))
        tab = pl.BlockSpec((tm, R), lambda i: (i, 0))
        out_shape = jax.ShapeDtypeStruct((T, R), F32)
        sem = ("parallel",)
    else:
        grid = (H, T // tm)
        in_slab = pl.BlockSpec((None, tm, R), lambda h, i: (h, i, 0))
        out_slab = in_slab
        tab = pl.BlockSpec((tm, R), lambda h, i: (i, 0))
        out_shape = jax.ShapeDtypeStruct((H, T, R), F32)
        sem = ("parallel", "parallel")
    return pl.pallas_call(
        body, name=name, grid=grid, in_specs=[in_slab, tab, tab], out_specs=out_slab,
        out_shape=out_shape, compiler_params=_cparams(sem),
    )(dy, cos2, sgn_sin)


def _log_sigmoid(z):
    return jnp.minimum(z, 0.0) - jnp.log(1.0 + jnp.exp(-jnp.abs(z)))


def _gate_cumsum(name, fl, b, tb):
    H, T = fl.shape

    def body(f_ref, b_ref, c_ref, carry):
        @pl.when(pl.program_id(0) == 0)
        def _():
            carry[...] = jnp.zeros_like(carry)

        ls = _log_sigmoid(f_ref[...] + b_ref[...])
        src = lax.broadcasted_iota(jnp.int32, (tb, tb), 0)
        dst = lax.broadcasted_iota(jnp.int32, (tb, tb), 1)
        tri = (src <= dst).astype(F32)
        c = lax.dot_general(ls, tri, NN, precision=lax.Precision.HIGHEST,
                            preferred_element_type=F32) + carry[...]
        c_ref[...] = c
        carry[...] = carry[...] + jnp.sum(ls, axis=-1, keepdims=True)

    return pl.pallas_call(
        body, name=name, grid=(T // tb,),
        in_specs=[pl.BlockSpec((H, tb), lambda i: (0, i)), pl.BlockSpec((H, 1), lambda i: (0, 0))],
        out_specs=pl.BlockSpec((H, tb), lambda i: (0, i)),
        out_shape=jax.ShapeDtypeStruct((H, T), F32),
        scratch_shapes=[pltpu.VMEM((H, 1), F32)],
        compiler_params=_cparams(("arbitrary",)),
    )(fl, b)


def _gate_cumsum_bwd(name, d_query, d_key, fl, b, tb):
    H, T = fl.shape
    nb = T // tb

    def body(dq_ref, dk_ref, f_ref, b_ref, dfl_ref, db_ref, carry):
        @pl.when(pl.program_id(0) == 0)
        def _():
            carry[...] = jnp.zeros_like(carry)
            db_ref[...] = jnp.zeros_like(db_ref)

        d = dq_ref[...] - dk_ref[...]
        src = lax.broadcasted_iota(jnp.int32, (tb, tb), 0)
        dst = lax.broadcasted_iota(jnp.int32, (tb, tb), 1)
        tri = (src >= dst).astype(F32)
        dls = lax.dot_general(d, tri, NN, precision=lax.Precision.HIGHEST,
                              preferred_element_type=F32) + carry[...]
        z = f_ref[...] + b_ref[...]
        dfl = dls * (1.0 / (1.0 + jnp.exp(z)))
        dfl_ref[...] = dfl
        db_ref[...] += jnp.sum(dfl, axis=-1, keepdims=True)
        carry[...] = carry[...] + jnp.sum(d, axis=-1, keepdims=True)

    blk = pl.BlockSpec((H, tb), lambda i: (0, nb - 1 - i))
    vec = pl.BlockSpec((H, 1), lambda i: (0, 0))
    return pl.pallas_call(
        body, name=name, grid=(nb,),
        in_specs=[blk, blk, blk, vec], out_specs=[blk, vec],
        out_shape=[jax.ShapeDtypeStruct((H, T), F32), jax.ShapeDtypeStruct((H, 1), F32)],
        scratch_shapes=[pltpu.VMEM((H, 1), F32)],
        compiler_params=_cparams(("arbitrary",)),
    )(d_query, d_key, fl, b)


def _causal_mask(tq, rows_are_queries):
    r = lax.broadcasted_iota(jnp.int32, (tq, tq), 0)
    c = lax.broadcasted_iota(jnp.int32, (tq, tq), 1)
    return (c <= r) if rows_are_queries else (r <= c)


def _chunk_rows(j, tq):
    return pl.ds(pl.multiple_of(j * tq, tq), tq)


def _flash_fwd(name, q, k, v_aug, dv, tq):
    H, T, dqk = q.shape
    dva = v_aug.shape[2]
    nq = T // tq

    def body(q_ref, k_ref, v_ref, o_ref, lse_ref, m_sc, acc_sc):
        qi = pl.program_id(1)
        m_sc[...] = jnp.full_like(m_sc, NEG)
        acc_sc[...] = jnp.zeros_like(acc_sc)

        def chunk(j, masked):
            rows = _chunk_rows(j, tq)
            s = _dot(q_ref[...], k_ref[rows, :], NT)
            if masked:
                s = jnp.where(_causal_mask(tq, True), s, NEG)
            m_prev = m_sc[...]
            m_new = jnp.maximum(m_prev, jnp.max(s, axis=1, keepdims=True))
            p = jnp.exp(s - jnp.tile(m_new, (1, tq // 128)))
            alpha = jnp.tile(jnp.exp(m_prev - m_new), (1, dva // 128))
            acc_sc[...] = alpha * acc_sc[...] + _dot(p.astype(BF16), v_ref[rows, :], NN)
            m_sc[...] = m_new

        def off_diagonal(j, carry):
            chunk(j, False)
            return carry

        lax.fori_loop(0, qi, off_diagonal, 0)
        chunk(qi, True)
        acc = acc_sc[...]
        l = acc[:, dv:dv + 1]
        o_ref[...] = acc[:, :dv] / l
        lse_ref[...] = m_sc[:, :1] + jnp.log(l)

    return pl.pallas_call(
        body, name=name, grid=(H, nq),
        in_specs=[pl.BlockSpec((None, tq, dqk), lambda h, i: (h, i, 0)),
                  pl.BlockSpec((None, T, dqk), lambda h, i: (h, 0, 0)),
                  pl.BlockSpec((None, T, dva), lambda h, i: (h, 0, 0))],
        out_specs=[pl.BlockSpec((None, tq, dv), lambda h, i: (h, i, 0)),
                   pl.BlockSpec((None, tq, 1), lambda h, i: (h, i, 0))],
        out_shape=[jax.ShapeDtypeStruct((H, T, dv), F32), jax.ShapeDtypeStruct((H, T, 1), F32)],
        scratch_shapes=[pltpu.VMEM((tq, 128), F32), pltpu.VMEM((tq, dva), F32)],
        compiler_params=_cparams(("parallel", "arbitrary")),
    )(q, k, v_aug)


def _row_dot(name, a, b):
    H, T, d = a.shape
    tm = _pick(T, (1024, 512, 256, 128))

    def body(a_ref, b_ref, o_ref):
        o_ref[...] = jnp.sum(a_ref[...].astype(F32) * b_ref[...].astype(F32), axis=-1, keepdims=True)

    slab = pl.BlockSpec((None, tm, d), lambda h, i: (h, i, 0))
    return pl.pallas_call(
        body, name=name, grid=(H, T // tm), in_specs=[slab, slab],
        out_specs=pl.BlockSpec((None, tm, 1), lambda h, i: (h, i, 0)),
        out_shape=jax.ShapeDtypeStruct((H, T, 1), F32),
        compiler_params=_cparams(("parallel", "parallel")),
    )(a, b)


def _flash_dq(name, q, k, v, do, scale, tq):
    H, T, dqk = q.shape
    dva = v.shape[2]
    nq = T // tq

    def body(q_ref, k_ref, v_ref, do_ref, dq_ref, acc_sc):
        qi = pl.program_id(1)
        acc_sc[...] = jnp.zeros_like(acc_sc)

        def chunk(j, masked):
            rows = _chunk_rows(j, tq)
            kb = k_ref[rows, :]
            s = _dot(q_ref[...], kb, NT)
            if masked:
                s = jnp.where(_causal_mask(tq, True), s, NEG)
            ds = jnp.exp(s) * _dot(do_ref[...], v_ref[rows, :], NT)
            acc_sc[...] += _dot(ds.astype(BF16), kb, NN)

        def off_diagonal(j, carry):
            chunk(j, False)
            return carry

        lax.fori_loop(0, qi, off_diagonal, 0)
        chunk(qi, True)
        dq_ref[...] = acc_sc[...] * scale

    q_spec = pl.BlockSpec((None, tq, dqk), lambda h, i: (h, i, 0))
    return pl.pallas_call(
        body, name=name, grid=(H, nq),
        in_specs=[q_spec,
                  pl.BlockSpec((None, T, dqk), lambda h, i: (h, 0, 0)),
                  pl.BlockSpec((None, T, dva), lambda h, i: (h, 0, 0)),
                  pl.BlockSpec((None, tq, dva), lambda h, i: (h, i, 0))],
        out_specs=q_spec,
        out_shape=jax.ShapeDtypeStruct((H, T, dqk), F32),
        scratch_shapes=[pltpu.VMEM((tq, dqk), F32)],
        compiler_params=_cparams(("parallel", "arbitrary")),
    )(q, k, v, do)


def _flash_dkv(name, q, k, v, do, tq):
    H, T, dqk = q.shape
    dva = v.shape[2]
    nq = T // tq

    def body(q_ref, k_ref, v_ref, do_ref, dk_ref, dv_ref, dk_sc, dv_sc):
        ki = pl.program_id(1)
        dk_sc[...] = jnp.zeros_like(dk_sc)
        dv_sc[...] = jnp.zeros_like(dv_sc)

        def chunk(i, masked):
            rows = _chunk_rows(i, tq)
            qb = q_ref[rows, :]
            dob = do_ref[rows, :]
            st = _dot(k_ref[...], qb, NT)
            if masked:
                st = jnp.where(_causal_mask(tq, False), st, NEG)
            pt = jnp.exp(st)
            dv_sc[...] += _dot(pt.astype(BF16), dob, NN)
            dst = pt * _dot(v_ref[...], dob, NT)
            dk_sc[...] += _dot(dst.astype(BF16), qb, NN)

        def off_diagonal(i, carry):
            chunk(i, False)
            return carry

        chunk(ki, True)
        lax.fori_loop(ki + 1, nq, off_diagonal, 0)
        dk_ref[...] = dk_sc[...]
        dv_ref[...] = dv_sc[...]

    k_spec = pl.BlockSpec((None, tq, dqk), lambda h, j: (h, j, 0))
    v_spec = pl.BlockSpec((None, tq, dva), lambda h, j: (h, j, 0))
    return pl.pallas_call(
        body, name=name, grid=(H, nq),
        in_specs=[pl.BlockSpec((None, T, dqk), lambda h, j: (h, 0, 0)), k_spec, v_spec,
                  pl.BlockSpec((None, T, dva), lambda h, j: (h, 0, 0))],
        out_specs=[k_spec, v_spec],
        out_shape=[jax.ShapeDtypeStruct((H, T, dqk), F32), jax.ShapeDtypeStruct((H, T, dva), F32)],
        scratch_shapes=[pltpu.VMEM((tq, dqk), F32), pltpu.VMEM((tq, dva), F32)],
        compiler_params=_cparams(("parallel", "arbitrary")),
    )(q, k, v, do)


def _adamw_math(w, g, m, v):
    m = ADAM_B1 * m + (1.0 - ADAM_B1) * g
    v = ADAM_B2 * v + (1.0 - ADAM_B2) * (g * g)
    m_hat = m / (1.0 - ADAM_B1 ** ADAM_STEP)
    v_hat = v / (1.0 - ADAM_B2 ** ADAM_STEP)
    delta = -ADAM_LR * (m_hat / (jnp.sqrt(v_hat) + ADAM_EPS) + ADAM_WD * w)
    return delta, m, v


def _adamw(name, parts, w, m, v):
    P, R, C = parts.shape
    tr = _pick(R, (256, 128, 64, 32, 16, 8))

    def body(p_ref, w_ref, m_ref, v_ref, g_out, d_out, m_out, v_out):
        g = p_ref[0].astype(F32)
        for i in range(1, P):
            g = g + p_ref[i].astype(F32)
        delta, m_new, v_new = _adamw_math(w_ref[...], g, m_ref[...], v_ref[...])
        g_out[...] = g
        d_out[...] = delta
        m_out[...] = m_new
        v_out[...] = v_new

    blk = pl.BlockSpec((tr, C), lambda i: (i, 0))
    sds = jax.ShapeDtypeStruct((R, C), F32)
    return pl.pallas_call(
        body, name=name, grid=(R // tr,),
        in_specs=[pl.BlockSpec((P, tr, C), lambda i: (0, i, 0)), blk, blk, blk],
        out_specs=[blk] * 4, out_shape=[sds] * 4,
        compiler_params=_cparams(("parallel",)),
    )(parts, w, m, v)


def _my_position():
    return lax.axis_index("x"), lax.axis_index("y"), lax.axis_index("c")


def _slot(p):
    return 4 * p[0] + 2 * p[1] + p[2]


def _flip(p, k):
    return tuple((1 - p[i]) if (k >> (2 - i)) & 1 else p[i] for i in range(3))


def _allgather_weights(shards):
    n = len(shards)

    def body(*refs):
        ins = refs[:n]
        outs = refs[n:2 * n]
        send_sems, recv_sems, local_sems = refs[2 * n:]
        x, y, c = _my_position()
        me, sibling = (x, y, c), (x, y, 1 - c)
        chips = [(1 - x, y), (x, 1 - y), (1 - x, 1 - y)]

        def copy(a, k, block, to, src=None):
            dst = outs[a].at[_slot(block)]
            return pltpu.make_async_remote_copy(
                src_ref=dst if src is None else src, dst_ref=dst,
                send_sem=send_sems.at[7 * a + k], recv_sem=recv_sems.at[7 * a + k],
                device_id=to, device_id_type=MESH)

        started = []
        for a in range(n):
            mine = pltpu.make_async_copy(ins[a], outs[a].at[_slot(me)], local_sems.at[a])
            mine.start()
            started.append(mine)
        first = []
        for a in range(n):
            first.append(copy(a, 0, me, sibling, src=ins[a]))
            first += [copy(a, 1 + j, me, (*chip, c), src=ins[a]) for j, chip in enumerate(chips)]
        for cp in first:
            cp.start()
        passed = []
        for j, chip in enumerate(chips):
            for a in range(n):
                copy(a, 1 + j, (*chip, c), me).wait_recv()
                fwd = copy(a, 4 + j, (*chip, c), sibling)
                fwd.start()
                passed.append(fwd)
        for a in range(n):
            copy(a, 0, sibling, me).wait_recv()
            for j, chip in enumerate(chips):
                copy(a, 4 + j, (*chip, 1 - c), me).wait_recv()
        for cp in first + passed:
            cp.wait_send()
        for mine in started:
            mine.wait()

    hbm = pl.BlockSpec(memory_space=pl.ANY)
    return pl.pallas_call(
        body, name="allgather_weights",
        in_specs=[hbm] * n, out_specs=[hbm] * n,
        out_shape=[jax.ShapeDtypeStruct((N_DEV,) + s.shape, s.dtype) for s in shards],
        scratch_shapes=[pltpu.SemaphoreType.DMA((7 * n,)), pltpu.SemaphoreType.DMA((7 * n,)),
                        pltpu.SemaphoreType.DMA((n,))],
        compiler_params=pltpu.CompilerParams(has_side_effects=True),
    )(*shards)


def _alltoall_grads(grads):
    n = len(grads)

    def body(*refs):
        ins = refs[:n]
        outs = refs[n:2 * n]
        send_sems, recv_sems, local_sems = refs[2 * n:]
        me = _my_position()
        started = []
        for a in range(n):
            mine = pltpu.make_async_copy(ins[a].at[_slot(me)], outs[a].at[_slot(me)], local_sems.at[a])
            mine.start()
            started.append(mine)
        sends = []
        for k in range(1, N_DEV):
            peer = _flip(me, k)
            for a in range(n):
                cp = pltpu.make_async_remote_copy(
                    src_ref=ins[a].at[_slot(peer)], dst_ref=outs[a].at[_slot(me)],
                    send_sem=send_sems.at[7 * a + k - 1], recv_sem=recv_sems.at[7 * a + k - 1],
                    device_id=peer, device_id_type=MESH)
                cp.start()
                sends.append(cp)
        for k in range(1, N_DEV):
            peer = _flip(me, k)
            for a in range(n):
                pltpu.make_async_remote_copy(
                    src_ref=ins[a].at[_slot(peer)], dst_ref=outs[a].at[_slot(peer)],
                    send_sem=send_sems.at[7 * a + k - 1], recv_sem=recv_sems.at[7 * a + k - 1],
                    device_id=peer, device_id_type=MESH).wait_recv()
        for cp in sends:
            cp.wait_send()
        for mine in started:
            mine.wait()

    hbm = pl.BlockSpec(memory_space=pl.ANY)
    return pl.pallas_call(
        body, name="alltoall_grads",
        in_specs=[hbm] * n, out_specs=[hbm] * n,
        out_shape=[jax.ShapeDtypeStruct(g.shape, g.dtype) for g in grads],
        scratch_shapes=[pltpu.SemaphoreType.DMA((7 * n,)), pltpu.SemaphoreType.DMA((7 * n,)),
                        pltpu.SemaphoreType.DMA((n,))],
        compiler_params=pltpu.CompilerParams(has_side_effects=True),
    )(*grads)


def _allreduce_small(v):
    R, C = v.shape

    def body(v_ref, o_ref, buf, send_sems, recv_sems):
        me = _my_position()
        buf[_slot(me)] = v_ref[...]
        sends = []
        for k in range(1, N_DEV):
            peer = _flip(me, k)
            cp = pltpu.make_async_remote_copy(
                src_ref=v_ref, dst_ref=buf.at[_slot(me)],
                send_sem=send_sems.at[k - 1], recv_sem=recv_sems.at[k - 1],
                device_id=peer, device_id_type=MESH)
            cp.start()
            sends.append(cp)
        for k in range(1, N_DEV):
            peer = _flip(me, k)
            pltpu.make_async_remote_copy(
                src_ref=v_ref, dst_ref=buf.at[_slot(peer)],
                send_sem=send_sems.at[k - 1], recv_sem=recv_sems.at[k - 1],
                device_id=peer, device_id_type=MESH).wait_recv()
        for cp in sends:
            cp.wait_send()
        tot = buf[0]
        for s in range(1, N_DEV):
            tot = tot + buf[s]
        o_ref[...] = tot

    vm = pl.BlockSpec(memory_space=pltpu.VMEM)
    return pl.pallas_call(
        body, name="allreduce_small",
        in_specs=[vm], out_specs=vm, out_shape=jax.ShapeDtypeStruct((R, C), F32),
        scratch_shapes=[pltpu.VMEM((N_DEV, R, C), F32), pltpu.SemaphoreType.DMA((7,)),
                        pltpu.SemaphoreType.DMA((7,))],
        compiler_params=pltpu.CompilerParams(has_side_effects=True),
    )(v)


def _to_heads(t, heads):
    T = t.shape[0]
    return t.reshape(T, heads, t.shape[1] // heads).transpose(1, 0, 2)


def _from_heads(t):
    H, T, d = t.shape
    return t.transpose(1, 0, 2).reshape(T, H * d)


def _widen(t, width, ones_at=None, pieces_at=None, pieces=None, const_at=None, const=None):
    out = jnp.pad(t, ((0, 0), (0, 0), (0, width - t.shape[-1])))
    lane = lax.broadcasted_iota(jnp.int32, (1, 1, width), 2)
    if ones_at is not None:
        out = jnp.where((lane >= ones_at) & (lane < ones_at + 3), jnp.ones((), BF16), out)
    if pieces_at is not None:
        for i in range(3):
            out = jnp.where(lane == pieces_at + i, pieces[..., i:i + 1], out)
    if const_at is not None:
        out = jnp.where(lane == const_at, jnp.asarray(const, BF16), out)
    return out


def _split3(t):
    hi = lax.reduce_precision(t, 8, 7)
    r = t - hi
    mid = lax.reduce_precision(r, 8, 7)
    lo = lax.reduce_precision(r - mid, 8, 7)
    return jnp.concatenate([hi, mid, lo], axis=-1).astype(BF16)


def _pad_cols(t, n):
    return jnp.pad(t, ((0, 0), (0, n - t.shape[1])))


def _pack_small(mix, ffn, kv, fin, kva, qa, bf, last):
    row6 = jnp.concatenate([kva.reshape(-1), qa.reshape(-1), bf.reshape(-1),
                            jnp.zeros((D_MODEL - KV_LORA - Q_LORA - FOX_HEADS,), F32)])
    return jnp.stack([mix[0], mix[1], ffn[0], ffn[1], kv.reshape(-1), fin.reshape(-1), row6, last])


def _unpack_small(p):
    mix = p[0:2]
    ffn = p[2:4]
    kv = p[4]
    fin = p[5]
    kva = p[6, :KV_LORA]
    qa = p[6, KV_LORA:KV_LORA + Q_LORA].reshape(1, Q_LORA)
    bf = p[6, KV_LORA + Q_LORA:KV_LORA + Q_LORA + FOX_HEADS].reshape(1, FOX_HEADS)
    return mix, ffn, bf, kv, kva, qa, fin


def _mlp_fwd(tag, xin, g, w_up, w_down):
    h = _rms(f"{tag}_norm", xin, g, BF16)

    def act(acc):
        r = jnp.maximum(acc, 0.0)
        return acc, r * r

    u, a = _mm(f"{tag}_up", h, w_up, "nn", (BF16, BF16), epi=act)
    xout = _mm(f"{tag}_down", a, w_down, "nn", (F32,), epi=lambda acc, r: (acc + r,), extras=(xin,))
    return xout, (h, u, a)


def _mlp_bwd(tag, gout, xin, g, w_up, w_down, saved):
    h, u, a = saved
    dw_down = _mm_tn(f"{tag}_dwdown", a, gout)
    du = _mm(f"{tag}_du", gout, w_down, "nt", (BF16,),
             epi=lambda acc, uu: (acc * (2.0 * jnp.maximum(uu.astype(F32), 0.0)),), extras=(u,))
    dw_up = _mm_tn(f"{tag}_dwup", h, du)
    dh = _mm(f"{tag}_dh", du, w_up, "nt", (F32,))
    gin, dg = _rms_bwd(f"{tag}_norm_bwd", xin, g, dh, dres=gout)
    return gin, dg, dw_up, dw_down


def kernel(x, norm_mix_g, norm_ffn_g, fox_w_in, fox_b_f, fox_w_out, kv_norm_g, mla_w_kv_a, mla_kv_a_norm_g, mla_w_kv_b, mla_w_q_a, mla_q_a_norm_g, mla_w_q_b, mla_w_out, ffn_w_up, ffn_w_down, final_norm_g, loss_target, m_norm_mix_g, m_norm_ffn_g, m_fox_w_in, m_fox_b_f, m_fox_w_out, m_kv_norm_g, m_mla_w_kv_a, m_mla_kv_a_norm_g, m_mla_w_kv_b, m_mla_w_q_a, m_mla_q_a_norm_g, m_mla_w_q_b, m_mla_w_out, m_ffn_w_up, m_ffn_w_down, m_final_norm_g, v_norm_mix_g, v_norm_ffn_g, v_fox_w_in, v_fox_b_f, v_fox_w_out, v_kv_norm_g, v_mla_w_kv_a, v_mla_kv_a_norm_g, v_mla_w_kv_b, v_mla_w_q_a, v_mla_q_a_norm_g, v_mla_w_q_b, v_mla_w_out, v_ffn_w_up, v_ffn_w_down, v_final_norm_g):
    T = x.shape[1]
    D = D_MODEL
    tq = 512 if T >= 2048 else 128
    x0 = x[0]
    tgt = loss_target[0]

    shards = [fox_w_in[0], fox_w_out[0], mla_w_kv_a, mla_w_kv_b, mla_w_q_a[0], mla_w_q_b[0],
              mla_w_out[0], ffn_w_up, ffn_w_down]
    gat = _allgather_weights([s.astype(BF16) for s in shards])
    w_in = gat[0].transpose(1, 0, 2).reshape(D, 3 * D + FOX_HEADS)
    w_qkv = w_in[:, :3 * D]
    w_f = _pad_cols(w_in[:, 3 * D:], 128)
    w_fo = gat[1].reshape(D, D)
    w_kva = _pad_cols(gat[2].reshape(D, KV_LORA + QK_ROPE), KV_A_PAD)
    w_kvb = gat[3].transpose(1, 0, 2).reshape(KV_LORA, MLA_HEADS * (QK_NOPE + V_HEAD))
    w_qa = gat[4].reshape(D, Q_LORA)
    w_qb = gat[5].transpose(1, 0, 2).reshape(Q_LORA, MLA_HEADS * (QK_NOPE + QK_ROPE))
    w_mo = gat[6].reshape(D, D)
    w_up = gat[7].transpose(1, 2, 0, 3).reshape(2, D, D_FF)
    w_down = gat[8].transpose(1, 0, 2, 3).reshape(2, D_FF, D)

    g_mix0, g_mix1 = norm_mix_g[0:1], norm_mix_g[1:2]
    g_ffn0, g_ffn1 = norm_ffn_g[0:1], norm_ffn_g[1:2]
    g_kv = kv_norm_g.reshape(1, D)
    g_kva = mla_kv_a_norm_g.reshape(1, KV_LORA)
    g_qa = mla_q_a_norm_g.reshape(1, Q_LORA)
    g_fin = final_norm_g.reshape(1, D)

    inv = 1.0 / (ROPE_BASE ** (jnp.arange(0, QK_ROPE, 2, dtype=F32) / QK_ROPE))
    ang = jnp.arange(T, dtype=F32)[:, None] * inv[None, :]
    cos, sin = jnp.cos(ang), jnp.sin(ang)
    cos2 = jnp.concatenate([cos, cos], axis=-1)
    sgn_sin = jnp.concatenate([-sin, sin], axis=-1)

    h0 = _rms("l0_mix_norm", x0, g_mix0, BF16)
    qkv = _mm("fox_qkv", h0, w_qkv, "nn", (BF16,))
    fl_pad = _mm("fox_gate_logit", h0, w_f, "nn", (F32,))
    fq = _to_heads(qkv[:, :D], FOX_HEADS)
    fk = _to_heads(qkv[:, D:2 * D], FOX_HEADS)
    fv = _to_heads(qkv[:, 2 * D:], FOX_HEADS)
    fl = fl_pad[:, :FOX_HEADS].T
    b_f = fox_b_f.reshape(FOX_HEADS, 1)
    cgate = _gate_cumsum("fox_gate_scan", fl, b_f, tq)
    fox_scale = FOX_HEAD_DIM ** -0.5
    fq_aug = _widen(fq * fox_scale, FOX_AUG, ones_at=FOX_HEAD_DIM)
    fk_aug = _widen(fk, FOX_AUG, ones_at=FOX_HEAD_DIM + 4, pieces_at=FOX_HEAD_DIM,
                    pieces=_split3((-cgate)[..., None]), const_at=FOX_HEAD_DIM + 3, const=1.0 / fox_scale)
    fv_aug = _widen(fv, FOX_AUG, ones_at=FOX_HEAD_DIM)
    fo, flse = _flash_fwd("fox_attn", fq_aug, fk_aug, fv_aug, FOX_HEAD_DIM, tq)
    fctx = _from_heads(fo).astype(BF16)
    x1 = _mm("fox_out", fctx, w_fo, "nn", (F32,), epi=lambda acc, r: (acc + r,), extras=(x0,))
    x2, mlp0 = _mlp_fwd("l0_ffn", x1, g_ffn0, w_up[0], w_down[0])

    src = _rms("kv_norm", x2, g_kv, BF16)
    kva = _mm("kv_a", src, w_kva, "nn", (F32,))
    kva_lat = kva[:, :KV_LORA]
    c_kv = _rms("kv_a_norm", kva_lat, g_kva, BF16)
    k_rope = _rope("k_rope", kva[:, KV_LORA:KV_LORA + QK_ROPE][None], cos2, sgn_sin, BF16)
    kvb = _mm("kv_b", c_kv, w_kvb, "nn", (BF16,))
    kvb_h = _to_heads(kvb, MLA_HEADS)
    mk = jnp.concatenate([kvb_h[:, :, :QK_NOPE],
                          jnp.broadcast_to(k_rope, (MLA_HEADS, T, QK_ROPE))], axis=-1)
    mv = kvb_h[:, :, QK_NOPE:]

    h1 = _rms("l1_mix_norm", x2, g_mix1, BF16)
    qa = _mm("q_a", h1, w_qa, "nn", (F32,))
    c_q = _rms("q_a_norm", qa, g_qa, BF16)
    qf = _mm("q_b", c_q, w_qb, "nn", (F32,))
    qf_h = _to_heads(qf, MLA_HEADS)
    mla_scale = (QK_NOPE + QK_ROPE) ** -0.5
    mq = _mla_q_prep("q_prep", qf_h, cos2, sgn_sin, mla_scale)
    mv_aug = _widen(mv, MLA_AUG, ones_at=V_HEAD)
    mo, mlse = _flash_fwd("mla_attn", mq, mk, mv_aug, V_HEAD, tq)
    mctx = _from_heads(mo).astype(BF16)
    x3 = _mm("mla_out", mctx, w_mo, "nn", (F32,), epi=lambda acc, r: (acc + r,), extras=(x2,))
    x4, mlp1 = _mlp_fwd("l1_ffn", x3, g_ffn1, w_up[1], w_down[1])

    g4, dg_fin, loss_vec = _loss_head("loss_head", x4, g_fin, tgt)

    g3, dg_ffn1, dw_up1, dw_down1 = _mlp_bwd("l1_ffn", g4, x3, g_ffn1, w_up[1], w_down[1], mlp1)

    dw_mo = _mm_tn("mla_out_dw", mctx, g3)
    dmo = _to_heads(_mm("mla_out_dx", g3, w_mo, "nt", (BF16,)), MLA_HEADS)
    mdelta = _row_dot("mla_delta", mo, dmo)
    dqk = QK_NOPE + QK_ROPE
    mq_bwd = _widen(mq, MLA_AUG, pieces_at=dqk, pieces=_split3(-mlse))
    mk_bwd = _widen(mk, MLA_AUG, ones_at=dqk)
    mdo_aug = _widen(dmo, MLA_AUG, pieces_at=V_HEAD, pieces=_split3(-mdelta))
    mdq = _flash_dq("mla_attn_dq", mq_bwd, mk_bwd, mv_aug, mdo_aug, mla_scale, tq)[:, :, :dqk]
    mdk, mdv = _flash_dkv("mla_attn_dkv", mq_bwd, mk_bwd, mv_aug, mdo_aug, tq)
    mdk = mdk[:, :, :dqk]
    mdv = mdv[:, :, :V_HEAD]
    dq_rope = _rope_bwd("q_rope_bwd", mdq[:, :, QK_NOPE:], cos2, sgn_sin, False)
    dqf = _from_heads(jnp.concatenate([mdq[:, :, :QK_NOPE], dq_rope], axis=-1))
    dw_qb = _mm_tn("q_b_dw", c_q, dqf)
    dc_q = _mm("q_b_dx", dqf, w_qb, "nt", (F32,))
    dqa, dg_qa = _rms_bwd("q_a_norm_bwd", qa, g_qa, dc_q)
    dw_qa = _mm_tn("q_a_dw", h1, dqa)
    dh1 = _mm("q_a_dx", dqa, w_qa, "nt", (F32,))
    g2a, dg_mix1 = _rms_bwd("l1_mix_norm_bwd", x2, g_mix1, dh1, dres=g3)

    dk_rope = _rope_bwd("k_rope_bwd", mdk[:, :, QK_NOPE:], cos2, sgn_sin, True)
    dkvb = _from_heads(jnp.concatenate([mdk[:, :, :QK_NOPE], mdv], axis=-1))
    dw_kvb = _mm_tn("kv_b_dw", c_kv, dkvb)
    dc_kv = _mm("kv_b_dx", dkvb, w_kvb, "nt", (F32,))
    dkva_lat, dg_kva = _rms_bwd("kv_a_norm_bwd", kva_lat, g_kva, dc_kv)
    dkva = _pad_cols(jnp.concatenate([dkva_lat, dk_rope], axis=-1), KV_A_PAD)
    dw_kva = _mm_tn("kv_a_dw", src, dkva)[:, :KV_LORA + QK_ROPE]
    dsrc = _mm("kv_a_dx", dkva, w_kva, "nt", (F32,))
    g2, dg_kv = _rms_bwd("kv_norm_bwd", x2, g_kv, dsrc, dres=g2a)

    g1, dg_ffn0, dw_up0, dw_down0 = _mlp_bwd("l0_ffn", g2, x1, g_ffn0, w_up[0], w_down[0], mlp0)

    dw_fo = _mm_tn("fox_out_dw", fctx, g1)
    dfo = _to_heads(_mm("fox_out_dx", g1, w_fo, "nt", (BF16,)), FOX_HEADS)
    fdelta = _row_dot("fox_delta", fo, dfo)
    fq_bwd = _widen(fq * fox_scale, FOX_AUG, ones_at=FOX_HEAD_DIM, pieces_at=FOX_HEAD_DIM + 4,
                    pieces=_split3(-flse))
    fdo_aug = _widen(dfo, FOX_AUG, pieces_at=FOX_HEAD_DIM, pieces=_split3(-fdelta))
    fdq_aug = _flash_dq("fox_attn_dq", fq_bwd, fk_aug, fv_aug, fdo_aug, fox_scale, tq)
    fdk_aug, fdv_aug = _flash_dkv("fox_attn_dkv", fq_bwd, fk_aug, fv_aug, fdo_aug, tq)
    fdq = fdq_aug[:, :, :FOX_HEAD_DIM]
    fdk = fdk_aug[:, :, :FOX_HEAD_DIM]
    fdv = fdv_aug[:, :, :FOX_HEAD_DIM]
    dfl, db_f = _gate_cumsum_bwd("fox_gate_scan_bwd", fdq_aug[:, :, FOX_HEAD_DIM + 3],
                                 fdk_aug[:, :, FOX_HEAD_DIM], fl, b_f, tq)
    dqkv = jnp.concatenate([_from_heads(fdq), _from_heads(fdk), _from_heads(fdv)], axis=-1).astype(BF16)
    dfl_pad = _pad_cols(dfl.T, 128)
    dw_qkv = _mm_tn("fox_qkv_dw", h0, dqkv)
    dw_f = _mm_tn("fox_gate_dw", h0, dfl_pad)[:, :FOX_HEADS]
    dw_in = jnp.concatenate([dw_qkv, dw_f], axis=-1)
    dh0a = _mm("fox_gate_dx", dfl_pad, w_f, "nt", (F32,))
    dh0 = _mm("fox_qkv_dx", dqkv, w_qkv, "nt", (F32,), epi=lambda acc, r: (acc + r,), extras=(dh0a,))
    grad_x, dg_mix0 = _rms_bwd("l0_mix_norm_bwd", x0, g_mix0, dh0, dres=g1)

    dw_up = jnp.stack([dw_up0, dw_up1])
    dw_down = jnp.stack([dw_down0, dw_down1])
    shard_major = [
        dw_in.reshape(D, N_DEV, -1).transpose(1, 0, 2),
        dw_fo.reshape(N_DEV, D // N_DEV, D),
        dw_kva.reshape(N_DEV, D // N_DEV, KV_LORA + QK_ROPE),
        dw_kvb.reshape(KV_LORA, N_DEV, -1).transpose(1, 0, 2),
        dw_qa.reshape(N_DEV, D // N_DEV, Q_LORA),
        dw_qb.reshape(Q_LORA, N_DEV, -1).transpose(1, 0, 2),
        dw_mo.reshape(N_DEV, D // N_DEV, D),
        dw_up.reshape(2, D, N_DEV, -1).transpose(2, 0, 1, 3),
        dw_down.reshape(2, N_DEV, D_FF // N_DEV, D).transpose(1, 0, 2, 3),
    ]
    parts = _alltoall_grads([g.astype(BF16) for g in shard_major])

    names = ["fox_w_in", "fox_w_out", "mla_w_kv_a", "mla_w_kv_b", "mla_w_q_a", "mla_w_q_b",
             "mla_w_out", "ffn_w_up", "ffn_w_down"]
    moms = [m_fox_w_in, m_fox_w_out, m_mla_w_kv_a, m_mla_w_kv_b, m_mla_w_q_a, m_mla_w_q_b,
            m_mla_w_out, m_ffn_w_up, m_ffn_w_down]
    vars_ = [v_fox_w_in, v_fox_w_out, v_mla_w_kv_a, v_mla_w_kv_b, v_mla_w_q_a, v_mla_w_q_b,
             v_mla_w_out, v_ffn_w_up, v_ffn_w_down]
    full = [fox_w_in, fox_w_out, mla_w_kv_a, mla_w_kv_b, mla_w_q_a, mla_w_q_b, mla_w_out,
            ffn_w_up, ffn_w_down]
    big = {}
    for nm, p, w, m, v in zip(names, parts, full, moms, vars_):
        C = w.shape[-1]
        res = _adamw(f"adamw_{nm}", p.reshape(N_DEV, -1, C), w.reshape(-1, C), m.reshape(-1, C),
                     v.reshape(-1, C))
        big[nm] = [r.reshape(w.shape) for r in res]

    zrow = jnp.zeros((D,), F32)
    g_small = _pack_small(jnp.concatenate([dg_mix0, dg_mix1]), jnp.concatenate([dg_ffn0, dg_ffn1]),
                          dg_kv, dg_fin, dg_kva, dg_qa, db_f, zrow.at[0].set(loss_vec[0, 0]))
    tot_small = _allreduce_small(g_small)
    w_small = _pack_small(norm_mix_g, norm_ffn_g, kv_norm_g, final_norm_g, mla_kv_a_norm_g,
                          mla_q_a_norm_g, fox_b_f, zrow)
    m_small = _pack_small(m_norm_mix_g, m_norm_ffn_g, m_kv_norm_g, m_final_norm_g, m_mla_kv_a_norm_g,
                          m_mla_q_a_norm_g, m_fox_b_f, zrow)
    v_small = _pack_small(v_norm_mix_g, v_norm_ffn_g, v_kv_norm_g, v_final_norm_g, v_mla_kv_a_norm_g,
                          v_mla_q_a_norm_g, v_fox_b_f, zrow)
    small = _adamw("adamw_small", tot_small[None], w_small, m_small, v_small)
    loss = tot_small[7, 0]
    small = [_unpack_small(s) for s in small]

    def ordered(i):
        mix, ffn, bf, kv, kva, qa, fin = small[i]
        return [mix, ffn, big["fox_w_in"][i], bf, big["fox_w_out"][i], kv, big["mla_w_kv_a"][i], kva,
                big["mla_w_kv_b"][i], big["mla_w_q_a"][i], qa, big["mla_w_q_b"][i],
                big["mla_w_out"][i], big["ffn_w_up"][i], big["ffn_w_down"][i], fin]

    return (loss, grad_x[None], *ordered(0), *ordered(1), *ordered(2), *ordered(3))
```

```python
import functools
import math

import jax
import jax.numpy as jnp
from jax import lax
from jax.experimental import pallas as pl
from jax.experimental.pallas import tpu as pltpu

F32 = jnp.float32
BF16 = jnp.bfloat16
MESH = pl.DeviceIdType.MESH

N_DEV = 8
D_MODEL = 1024
FOX_HEADS = 16
FOX_HEAD_DIM = 64
FOX_AUG = 128
MLA_AUG = 256
MLA_HEADS = 8
QK_NOPE = 128
QK_ROPE = 64
V_HEAD = 128
Q_LORA = 384
KV_LORA = 256
KV_A_PAD = 384
D_FF = 4096
ROPE_BASE = 10000.0
EPS = 1e-6
NEG = -1e30

ADAM_LR = 0.001
ADAM_B1 = 0.9
ADAM_B2 = 0.999
ADAM_EPS = 1e-08
ADAM_WD = 0.01
ADAM_STEP = 10

VMEM_LIMIT_BYTES = 48 * 1024 * 1024

NN = (((1,), (0,)), ((), ()))
NT = (((1,), (1,)), ((), ()))
TN = (((0,), (0,)), ((), ()))
_FORMS = {"nn": NN, "nt": NT}


def _cparams(sem=None):
    return pltpu.CompilerParams(dimension_semantics=sem, vmem_limit_bytes=VMEM_LIMIT_BYTES)


def _pick(n, cands):
    for c in cands:
        if c <= n and n % c == 0:
            return c
    return n


def _dot(a, b, dims):
    return lax.dot_general(a, b, dims, preferred_element_type=F32)


def _mm(name, a, b, form, out_dtypes, epi=None, extras=(), tm=1024, tn=512):
    M, K = a.shape
    N = b.shape[1] if form == "nn" else b.shape[0]
    tm = _pick(M, (tm, 512, 256, 128))
    tn = _pick(N, (tn, 384, 256, 128))
    n_ex = len(extras)

    def body(*refs):
        a_ref, b_ref = refs[0], refs[1]
        ex = refs[2:2 + n_ex]
        outs = refs[2 + n_ex:]
        acc = _dot(a_ref[...].astype(BF16), b_ref[...].astype(BF16), _FORMS[form])
        res = epi(acc, *[e[...] for e in ex]) if epi is not None else (acc,)
        for o_ref, r in zip(outs, res):
            o_ref[...] = r.astype(o_ref.dtype)

    if form == "nn":
        b_spec = pl.BlockSpec((K, tn), lambda i, j: (0, j))
    else:
        b_spec = pl.BlockSpec((tn, K), lambda i, j: (j, 0))
    tile = pl.BlockSpec((tm, tn), lambda i, j: (i, j))
    out = pl.pallas_call(
        body, name=name, grid=(M // tm, N // tn),
        in_specs=[pl.BlockSpec((tm, K), lambda i, j: (i, 0)), b_spec] + [tile] * n_ex,
        out_specs=[tile] * len(out_dtypes),
        out_shape=[jax.ShapeDtypeStruct((M, N), dt) for dt in out_dtypes],
        compiler_params=_cparams(("parallel", "arbitrary")),
    )(a, b, *extras)
    return out if len(out_dtypes) > 1 else out[0]


def _mm_tn(name, a, b):
    T, Ka = a.shape
    N = b.shape[1]
    tk = _pick(Ka, (1024, 512, 384, 256, 128))
    tn = _pick(N, (1024, 768, 512, 384, 256, 128))
    tt = _pick(T, (1024, 512, 256, 128))

    def body(a_ref, b_ref, o_ref):
        @pl.when(pl.program_id(2) == 0)
        def _():
            o_ref[...] = jnp.zeros_like(o_ref)

        o_ref[...] += _dot(a_ref[...].astype(BF16), b_ref[...].astype(BF16), TN)

    return pl.pallas_call(
        body, name=name, grid=(Ka // tk, N // tn, T // tt),
        in_specs=[pl.BlockSpec((tt, tk), lambda i, j, t: (t, i)),
                  pl.BlockSpec((tt, tn), lambda i, j, t: (t, j))],
        out_specs=pl.BlockSpec((tk, tn), lambda i, j, t: (i, j)),
        out_shape=jax.ShapeDtypeStruct((Ka, N), F32),
        compiler_params=_cparams(("parallel", "parallel", "arbitrary")),
    )(a, b)


def _rms(name, x, g, out_dtype):
    T, D = x.shape
    tm = _pick(T, (1024, 512, 256, 128))

    def body(x_ref, g_ref, o_ref):
        xf = x_ref[...]
        r = lax.rsqrt(jnp.mean(xf * xf, axis=-1, keepdims=True) + EPS)
        o_ref[...] = (xf * r * g_ref[...]).astype(o_ref.dtype)

    return pl.pallas_call(
        body, name=name, grid=(T // tm,),
        in_specs=[pl.BlockSpec((tm, D), lambda i: (i, 0)), pl.BlockSpec((1, D), lambda i: (0, 0))],
        out_specs=pl.BlockSpec((tm, D), lambda i: (i, 0)),
        out_shape=jax.ShapeDtypeStruct((T, D), out_dtype),
        compiler_params=_cparams(("parallel",)),
    )(x, g)


def _rms_bwd(name, x, g, dh, dres=None):
    T, D = x.shape
    tm = _pick(T, (512, 256, 128))
    has_res = dres is not None

    def body(*refs):
        if has_res:
            x_ref, g_ref, dh_ref, dres_ref, dx_ref, dg_ref = refs
        else:
            x_ref, g_ref, dh_ref, dx_ref, dg_ref = refs

        @pl.when(pl.program_id(0) == 0)
        def _():
            dg_ref[...] = jnp.zeros_like(dg_ref)

        xf = x_ref[...]
        r = lax.rsqrt(jnp.mean(xf * xf, axis=-1, keepdims=True) + EPS)
        xhat = xf * r
        dy = dh_ref[...].astype(F32)
        dxh = dy * g_ref[...]
        dx = r * (dxh - xhat * jnp.mean(dxh * xhat, axis=-1, keepdims=True))
        if has_res:
            dx = dx + dres_ref[...]
        dx_ref[...] = dx
        dg_ref[...] += jnp.sum(dy * xhat, axis=0, keepdims=True)

    row = pl.BlockSpec((tm, D), lambda i: (i, 0))
    vec = pl.BlockSpec((1, D), lambda i: (0, 0))
    ins = [x, g, dh] + ([dres] if has_res else [])
    return pl.pallas_call(
        body, name=name, grid=(T // tm,),
        in_specs=[row, vec, row] + ([row] if has_res else []),
        out_specs=[row, vec],
        out_shape=[jax.ShapeDtypeStruct((T, D), F32), jax.ShapeDtypeStruct((1, D), F32)],
        compiler_params=_cparams(("arbitrary",)),
    )(*ins)


def _loss_head(name, x, g, tgt):
    T, D = x.shape
    tm = _pick(T, (512, 256, 128))

    def body(x_ref, g_ref, t_ref, dx_ref, dg_ref, loss_ref):
        @pl.when(pl.program_id(0) == 0)
        def _():
            dg_ref[...] = jnp.zeros_like(dg_ref)
            loss_ref[...] = jnp.zeros_like(loss_ref)

        xf = x_ref[...]
        r = lax.rsqrt(jnp.mean(xf * xf, axis=-1, keepdims=True) + EPS)
        xhat = xf * r
        gv = g_ref[...]
        err = xhat * gv - t_ref[...]
        row_loss = jnp.mean(err * err, axis=-1, keepdims=True)
        loss_ref[...] += 0.5 * jnp.sum(row_loss, axis=0, keepdims=True)
        dy = err * (1.0 / D)
        dxh = dy * gv
        dx_ref[...] = r * (dxh - xhat * jnp.mean(dxh * xhat, axis=-1, keepdims=True))
        dg_ref[...] += jnp.sum(dy * xhat, axis=0, keepdims=True)

    row = pl.BlockSpec((tm, D), lambda i: (i, 0))
    vec = pl.BlockSpec((1, D), lambda i: (0, 0))
    return pl.pallas_call(
        body, name=name, grid=(T // tm,),
        in_specs=[row, vec, row],
        out_specs=[row, vec, pl.BlockSpec((1, 128), lambda i: (0, 0))],
        out_shape=[jax.ShapeDtypeStruct((T, D), F32), jax.ShapeDtypeStruct((1, D), F32),
                   jax.ShapeDtypeStruct((1, 128), F32)],
        compiler_params=_cparams(("arbitrary",)),
    )(x, g, tgt)


def _swap_halves(t):
    half = t.shape[-1] // 2
    return jnp.concatenate([t[:, half:], t[:, :half]], axis=-1)


def _rope(name, t, cos2, sgn_sin, out_dtype):
    H, T, R = t.shape
    tm = _pick(T, (1024, 512, 256, 128))

    def body(t_ref, c_ref, s_ref, o_ref):
        tf = t_ref[...].astype(F32)
        o_ref[...] = (tf * c_ref[...] + _swap_halves(tf) * s_ref[...]).astype(o_ref.dtype)

    slab = pl.BlockSpec((None, tm, R), lambda h, i: (h, i, 0))
    tab = pl.BlockSpec((tm, R), lambda h, i: (i, 0))
    return pl.pallas_call(
        body, name=name, grid=(H, T // tm),
        in_specs=[slab, tab, tab], out_specs=slab,
        out_shape=jax.ShapeDtypeStruct((H, T, R), out_dtype),
        compiler_params=_cparams(("parallel", "parallel")),
    )(t, cos2, sgn_sin)


def _mla_q_prep(name, qf, cos2, sgn_sin, scale):
    H, T, W = qf.shape
    R = cos2.shape[1]
    tm = _pick(T, (1024, 512, 256, 128))

    def body(t_ref, c_ref, s_ref, o_ref):
        tf = t_ref[...]
        r = tf[:, W - R:]
        roped = r * c_ref[...] + _swap_halves(r) * s_ref[...]
        o_ref[...] = (jnp.concatenate([tf[:, :W - R], roped], axis=-1) * scale).astype(o_ref.dtype)

    slab = pl.BlockSpec((None, tm, W), lambda h, i: (h, i, 0))
    tab = pl.BlockSpec((tm, R), lambda h, i: (i, 0))
    return pl.pallas_call(
        body, name=name, grid=(H, T // tm),
        in_specs=[slab, tab, tab], out_specs=slab,
        out_shape=jax.ShapeDtypeStruct((H, T, W), BF16),
        compiler_params=_cparams(("parallel", "parallel")),
    )(qf, cos2, sgn_sin)


def _rope_bwd(name, dy, cos2, sgn_sin, sum_heads):
    H, T, R = dy.shape
    tm = _pick(T, (1024, 512, 256, 128))

    def body(d_ref, c_ref, s_ref, o_ref):
        d = d_ref[...]
        if sum_heads:
            tot = d[0]
            for h in range(1, H):
                tot = tot + d[h]
            d = tot
        o_ref[...] = d * c_ref[...] + _swap_halves(d * s_ref[...])

    if sum_heads:
        grid = (T // tm,)
        in_slab = pl.BlockSpec((H, tm, R), lambda i: (0, i, 0))
        out_slab = pl.BlockSpec((tm, R), lambda i: (i, 0))
        tab = pl.BlockSpec((tm, R), lambda i: (i, 0))
        out_shape = jax.ShapeDtypeStruct((T, R), F32)
        sem = ("parallel",)
    else:
        grid = (H, T // tm)
        in_slab = pl.BlockSpec((None, tm, R), lambda h, i: (h, i, 0))
        out_slab = in_slab
        tab = pl.BlockSpec((tm, R), lambda h, i: (i, 0))
        out_shape = jax.ShapeDtypeStruct((H, T, R), F32)
        sem = ("parallel", "parallel")
    return pl.pallas_call(
        body, name=name, grid=grid, in_specs=[in_slab, tab, tab], out_specs=out_slab,
        out_shape=out_shape, compiler_params=_cparams(sem),
    )(dy, cos2, sgn_sin)


def _log_sigmoid(z):
    return jnp.minimum(z, 0.0) - jnp.log(1.0 + jnp.exp(-jnp.abs(z)))


def _gate_cumsum(name, fl, b, tb):
    H, T = fl.shape

    def body(f_ref, b_ref, c_ref, carry):
        @pl.when(pl.program_id(0) == 0)
        def _():
            carry[...] = jnp.zeros_like(carry)

        ls = _log_sigmoid(f_ref[...] + b_ref[...])
        src = lax.broadcasted_iota(jnp.int32, (tb, tb), 0)
        dst = lax.broadcasted_iota(jnp.int32, (tb, tb), 1)
        tri = (src <= dst).astype(F32)
        c = lax.dot_general(ls, tri, NN, precision=lax.Precision.HIGHEST,
                            preferred_element_type=F32) + carry[...]
        c_ref[...] = c
        carry[...] = carry[...] + jnp.sum(ls, axis=-1, keepdims=True)

    return pl.pallas_call(
        body, name=name, grid=(T // tb,),
        in_specs=[pl.BlockSpec((H, tb), lambda i: (0, i)), pl.BlockSpec((H, 1), lambda i: (0, 0))],
        out_specs=pl.BlockSpec((H, tb), lambda i: (0, i)),
        out_shape=jax.ShapeDtypeStruct((H, T), F32),
        scratch_shapes=[pltpu.VMEM((H, 1), F32)],
        compiler_params=_cparams(("arbitrary",)),
    )(fl, b)


def _gate_cumsum_bwd(name, d_query, d_key, fl, b, tb):
    H, T = fl.shape
    nb = T // tb

    def body(dq_ref, dk_ref, f_ref, b_ref, dfl_ref, db_ref, carry):
        @pl.when(pl.program_id(0) == 0)
        def _():
            carry[...] = jnp.zeros_like(carry)
            db_ref[...] = jnp.zeros_like(db_ref)

        d = dq_ref[...] - dk_ref[...]
        src = lax.broadcasted_iota(jnp.int32, (tb, tb), 0)
        dst = lax.broadcasted_iota(jnp.int32, (tb, tb), 1)
        tri = (src >= dst).astype(F32)
        dls = lax.dot_general(d, tri, NN, precision=lax.Precision.HIGHEST,
                              preferred_element_type=F32) + carry[...]
        z = f_ref[...] + b_ref[...]
        dfl = dls * (1.0 / (1.0 + jnp.exp(z)))
        dfl_ref[...] = dfl
        db_ref[...] += jnp.sum(dfl, axis=-1, keepdims=True)
        carry[...] = carry[...] + jnp.sum(d, axis=-1, keepdims=True)

    blk = pl.BlockSpec((H, tb), lambda i: (0, nb - 1 - i))
    vec = pl.BlockSpec((H, 1), lambda i: (0, 0))
    return pl.pallas_call(
        body, name=name, grid=(nb,),
        in_specs=[blk, blk, blk, vec], out_specs=[blk, vec],
        out_shape=[jax.ShapeDtypeStruct((H, T), F32), jax.ShapeDtypeStruct((H, 1), F32)],
        scratch_shapes=[pltpu.VMEM((H, 1), F32)],
        compiler_params=_cparams(("arbitrary",)),
    )(d_query, d_key, fl, b)


def _causal_mask(tq, rows_are_queries):
    r = lax.broadcasted_iota(jnp.int32, (tq, tq), 0)
    c = lax.broadcasted_iota(jnp.int32, (tq, tq), 1)
    return (c <= r) if rows_are_queries else (r <= c)


def _chunk_rows(j, tq):
    return pl.ds(pl.multiple_of(j * tq, tq), tq)


def _flash_fwd(name, q, k, v_aug, dv, tq, exchange=None):
    H, T, dqk = q.shape
    dva = v_aug.shape[2]
    nq = T // tq

    def body(q_ref, k_ref, v_ref, o_ref, lse_ref, m_sc, acc_sc):
        qi = pl.program_id(1)
        m_sc[...] = jnp.full_like(m_sc, NEG)
        acc_sc[...] = jnp.zeros_like(acc_sc)

        def chunk(j, masked):
            rows = _chunk_rows(j, tq)
            s = _dot(q_ref[...], k_ref[rows, :], NT)
            if masked:
                s = jnp.where(_causal_mask(tq, True), s, NEG)
            m_prev = m_sc[...]
            m_new = jnp.maximum(m_prev, jnp.max(s, axis=1, keepdims=True))
            p = jnp.exp(s - jnp.tile(m_new, (1, tq // 128)))
            alpha = jnp.tile(jnp.exp(m_prev - m_new), (1, dva // 128))
            acc_sc[...] = alpha * acc_sc[...] + _dot(p.astype(BF16), v_ref[rows, :], NN)
            m_sc[...] = m_new

        def off_diagonal(j, carry):
            chunk(j, False)
            return carry

        lax.fori_loop(0, qi, off_diagonal, 0)
        chunk(qi, True)
        acc = acc_sc[...]
        l = acc[:, dv:dv + 1]
        o_ref[...] = acc[:, :dv] / l
        lse_ref[...] = m_sc[:, :1] + jnp.log(l)

    return _call_carrying(
        body, name, (H, nq),
        in_specs=[pl.BlockSpec((None, tq, dqk), lambda h, i: (h, i, 0)),
                  pl.BlockSpec((None, T, dqk), lambda h, i: (h, 0, 0)),
                  pl.BlockSpec((None, T, dva), lambda h, i: (h, 0, 0))],
        out_specs=[pl.BlockSpec((None, tq, dv), lambda h, i: (h, i, 0)),
                   pl.BlockSpec((None, tq, 1), lambda h, i: (h, i, 0))],
        out_shape=[jax.ShapeDtypeStruct((H, T, dv), F32), jax.ShapeDtypeStruct((H, T, 1), F32)],
        scratch_shapes=[pltpu.VMEM((tq, 128), F32), pltpu.VMEM((tq, dva), F32)],
        operands=(q, k, v_aug), exchange=exchange)


def _row_dot(name, a, b):
    H, T, d = a.shape
    tm = _pick(T, (1024, 512, 256, 128))

    def body(a_ref, b_ref, o_ref):
        o_ref[...] = jnp.sum(a_ref[...].astype(F32) * b_ref[...].astype(F32), axis=-1, keepdims=True)

    slab = pl.BlockSpec((None, tm, d), lambda h, i: (h, i, 0))
    return pl.pallas_call(
        body, name=name, grid=(H, T // tm), in_specs=[slab, slab],
        out_specs=pl.BlockSpec((None, tm, 1), lambda h, i: (h, i, 0)),
        out_shape=jax.ShapeDtypeStruct((H, T, 1), F32),
        compiler_params=_cparams(("parallel", "parallel")),
    )(a, b)


def _flash_dq(name, q, k, v, do, scale, tq, exchange=None):
    H, T, dqk = q.shape
    dva = v.shape[2]
    nq = T // tq

    def body(q_ref, k_ref, v_ref, do_ref, dq_ref, acc_sc):
        qi = pl.program_id(1)
        acc_sc[...] = jnp.zeros_like(acc_sc)

        def chunk(j, masked):
            rows = _chunk_rows(j, tq)
            kb = k_ref[rows, :]
            s = _dot(q_ref[...], kb, NT)
            if masked:
                s = jnp.where(_causal_mask(tq, True), s, NEG)
            ds = jnp.exp(s) * _dot(do_ref[...], v_ref[rows, :], NT)
            acc_sc[...] += _dot(ds.astype(BF16), kb, NN)

        def off_diagonal(j, carry):
            chunk(j, False)
            return carry

        lax.fori_loop(0, qi, off_diagonal, 0)
        chunk(qi, True)
        dq_ref[...] = acc_sc[...] * scale

    q_spec = pl.BlockSpec((None, tq, dqk), lambda h, i: (h, i, 0))
    (dq,), exchanged = _call_carrying(
        body, name, (H, nq),
        in_specs=[q_spec,
                  pl.BlockSpec((None, T, dqk), lambda h, i: (h, 0, 0)),
                  pl.BlockSpec((None, T, dva), lambda h, i: (h, 0, 0)),
                  pl.BlockSpec((None, tq, dva), lambda h, i: (h, i, 0))],
        out_specs=[q_spec],
        out_shape=[jax.ShapeDtypeStruct((H, T, dqk), F32)],
        scratch_shapes=[pltpu.VMEM((tq, dqk), F32)],
        operands=(q, k, v, do), exchange=exchange)
    return dq, exchanged


def _flash_dkv(name, q, k, v, do, tq):
    H, T, dqk = q.shape
    dva = v.shape[2]
    nq = T // tq

    def body(q_ref, k_ref, v_ref, do_ref, dk_ref, dv_ref, dk_sc, dv_sc):
        ki = pl.program_id(1)
        dk_sc[...] = jnp.zeros_like(dk_sc)
        dv_sc[...] = jnp.zeros_like(dv_sc)

        def chunk(i, masked):
            rows = _chunk_rows(i, tq)
            qb = q_ref[rows, :]
            dob = do_ref[rows, :]
            st = _dot(k_ref[...], qb, NT)
            if masked:
                st = jnp.where(_causal_mask(tq, False), st, NEG)
            pt = jnp.exp(st)
            dv_sc[...] += _dot(pt.astype(BF16), dob, NN)
            dst = pt * _dot(v_ref[...], dob, NT)
            dk_sc[...] += _dot(dst.astype(BF16), qb, NN)

        def off_diagonal(i, carry):
            chunk(i, False)
            return carry

        chunk(ki, True)
        lax.fori_loop(ki + 1, nq, off_diagonal, 0)
        dk_ref[...] = dk_sc[...]
        dv_ref[...] = dv_sc[...]

    k_spec = pl.BlockSpec((None, tq, dqk), lambda h, j: (h, j, 0))
    v_spec = pl.BlockSpec((None, tq, dva), lambda h, j: (h, j, 0))
    return pl.pallas_call(
        body, name=name, grid=(H, nq),
        in_specs=[pl.BlockSpec((None, T, dqk), lambda h, j: (h, 0, 0)), k_spec, v_spec,
                  pl.BlockSpec((None, T, dva), lambda h, j: (h, 0, 0))],
        out_specs=[k_spec, v_spec],
        out_shape=[jax.ShapeDtypeStruct((H, T, dqk), F32), jax.ShapeDtypeStruct((H, T, dva), F32)],
        scratch_shapes=[pltpu.VMEM((tq, dqk), F32), pltpu.VMEM((tq, dva), F32)],
        compiler_params=_cparams(("parallel", "arbitrary")),
    )(q, k, v, do)


def _adamw_math(w, g, m, v):
    m = ADAM_B1 * m + (1.0 - ADAM_B1) * g
    v = ADAM_B2 * v + (1.0 - ADAM_B2) * (g * g)
    m_hat = m / (1.0 - ADAM_B1 ** ADAM_STEP)
    v_hat = v / (1.0 - ADAM_B2 ** ADAM_STEP)
    delta = -ADAM_LR * (m_hat / (jnp.sqrt(v_hat) + ADAM_EPS) + ADAM_WD * w)
    return delta, m, v


def _adamw(name, parts, w, m, v):
    P, R, C = parts.shape
    tr = _pick(R, (256, 128, 64, 32, 16, 8))

    def body(p_ref, w_ref, m_ref, v_ref, g_out, d_out, m_out, v_out):
        g = p_ref[0].astype(F32)
        for i in range(1, P):
            g = g + p_ref[i].astype(F32)
        delta, m_new, v_new = _adamw_math(w_ref[...], g, m_ref[...], v_ref[...])
        g_out[...] = g
        d_out[...] = delta
        m_out[...] = m_new
        v_out[...] = v_new

    blk = pl.BlockSpec((tr, C), lambda i: (i, 0))
    sds = jax.ShapeDtypeStruct((R, C), F32)
    return pl.pallas_call(
        body, name=name, grid=(R // tr,),
        in_specs=[pl.BlockSpec((P, tr, C), lambda i: (0, i, 0)), blk, blk, blk],
        out_specs=[blk] * 4, out_shape=[sds] * 4,
        compiler_params=_cparams(("parallel",)),
    )(parts, w, m, v)


def _my_position():
    return lax.axis_index("x"), lax.axis_index("y"), lax.axis_index("c")


def _slot(p):
    return 4 * p[0] + 2 * p[1] + p[2]


def _flip(p, k):
    return tuple((1 - p[i]) if (k >> (2 - i)) & 1 else p[i] for i in range(3))


def _allgather_weights(shards):
    n = len(shards)

    def body(*refs):
        ins = refs[:n]
        outs = refs[n:2 * n]
        send_sems, recv_sems, local_sems = refs[2 * n:]
        x, y, c = _my_position()
        me, sibling = (x, y, c), (x, y, 1 - c)
        chips = [(1 - x, y), (x, 1 - y), (1 - x, 1 - y)]

        def copy(a, k, block, to, src=None):
            dst = outs[a].at[_slot(block)]
            return pltpu.make_async_remote_copy(
                src_ref=dst if src is None else src, dst_ref=dst,
                send_sem=send_sems.at[7 * a + k], recv_sem=recv_sems.at[7 * a + k],
                device_id=to, device_id_type=MESH)

        started = []
        for a in range(n):
            mine = pltpu.make_async_copy(ins[a], outs[a].at[_slot(me)], local_sems.at[a])
            mine.start()
            started.append(mine)
        first = []
        for a in range(n):
            first.append(copy(a, 0, me, sibling, src=ins[a]))
            first += [copy(a, 1 + j, me, (*chip, c), src=ins[a]) for j, chip in enumerate(chips)]
        for cp in first:
            cp.start()
        passed = []
        for j, chip in enumerate(chips):
            for a in range(n):
                copy(a, 1 + j, (*chip, c), me).wait_recv()
                fwd = copy(a, 4 + j, (*chip, c), sibling)
                fwd.start()
                passed.append(fwd)
        for a in range(n):
            copy(a, 0, sibling, me).wait_recv()
            for j, chip in enumerate(chips):
                copy(a, 4 + j, (*chip, 1 - c), me).wait_recv()
        for cp in first + passed:
            cp.wait_send()
        for mine in started:
            mine.wait()

    hbm = pl.BlockSpec(memory_space=pl.ANY)
    return pl.pallas_call(
        body, name="allgather_weights",
        in_specs=[hbm] * n, out_specs=[hbm] * n,
        out_shape=[jax.ShapeDtypeStruct((N_DEV,) + s.shape, s.dtype) for s in shards],
        scratch_shapes=[pltpu.SemaphoreType.DMA((7 * n,)), pltpu.SemaphoreType.DMA((7 * n,)),
                        pltpu.SemaphoreType.DMA((n,))],
        compiler_params=pltpu.CompilerParams(has_side_effects=True),
    )(*shards)


def _exchange_copies(kind, x_in, x_out, send_sems, recv_sems, local_sems, receives=True):
    me = _my_position()
    mine = _slot(me)
    local, sends, recvs = [], [], []
    for a in range(len(x_in)):
        src = x_in[a] if kind == "gather" else x_in[a].at[mine]
        local.append(pltpu.make_async_copy(src, x_out[a].at[mine], local_sems.at[a]))
    for k in range(1, N_DEV):
        peer = _flip(me, k)
        theirs = _slot(peer)
        for a in range(len(x_in)):
            src = x_in[a] if kind == "gather" else x_in[a].at[theirs]
            ends = [(x_out[a].at[mine], sends)] + ([(x_out[a].at[theirs], recvs)] if receives else [])
            for dst, group in ends:
                group.append(pltpu.make_async_remote_copy(
                    src_ref=src, dst_ref=dst, send_sem=send_sems.at[7 * a + k - 1],
                    recv_sem=recv_sems.at[7 * a + k - 1], device_id=peer, device_id_type=MESH))
    return local, sends, recvs


def _exchange_out_shapes(kind, arrays):
    return [jax.ShapeDtypeStruct(((N_DEV,) + a.shape) if kind == "gather" else a.shape, a.dtype)
            for a in arrays]


def _exchange_sems(n):
    return [pltpu.SemaphoreType.DMA((7 * n,)), pltpu.SemaphoreType.DMA((7 * n,)),
            pltpu.SemaphoreType.DMA((n,))]


def _alltoall_grads(name, grads):
    n = len(grads)

    def body(*refs):
        local, sends, recvs = _exchange_copies("scatter", refs[:n], refs[n:2 * n], *refs[2 * n:])
        for cp in local + sends:
            cp.start()
        for cp in recvs:
            cp.wait_recv()
        for cp in sends:
            cp.wait_send()
        for cp in local:
            cp.wait()

    hbm = pl.BlockSpec(memory_space=pl.ANY)
    return pl.pallas_call(
        body, name=name,
        in_specs=[hbm] * n, out_specs=[hbm] * n,
        out_shape=_exchange_out_shapes("scatter", grads), scratch_shapes=_exchange_sems(n),
        compiler_params=pltpu.CompilerParams(has_side_effects=True),
    )(*grads)


def _call_carrying(body, name, grid, in_specs, out_specs, out_shape, scratch_shapes, operands, exchange):
    if exchange is None:
        out = pl.pallas_call(
            body, name=name, grid=grid, in_specs=in_specs, out_specs=out_specs, out_shape=out_shape,
            scratch_shapes=scratch_shapes, compiler_params=_cparams(("parallel",) + ("arbitrary",) * (len(grid) - 1)),
        )(*operands)
        return out, None
    kind, arrays = exchange
    n, n_in, n_out, n_sc = len(arrays), len(in_specs), len(out_specs), len(scratch_shapes)

    def full_body(*refs):
        ins, refs = refs[:n_in], refs[n_in:]
        x_in, refs = refs[:n], refs[n:]
        outs, refs = refs[:n_out], refs[n_out:]
        x_out, refs = refs[:n], refs[n:]
        scratch, sems = refs[:n_sc], refs[n_sc:]
        first = last = None
        for axis, size in enumerate(grid):
            at_start = pl.program_id(axis) == 0
            at_end = pl.program_id(axis) == size - 1
            first = at_start if first is None else jnp.logical_and(first, at_start)
            last = at_end if last is None else jnp.logical_and(last, at_end)

        @pl.when(first)
        def _():
            local, sends, _ = _exchange_copies(kind, x_in, x_out, *sems, receives=False)
            for cp in local + sends:
                cp.start()

        body(*ins, *outs, *scratch)

        @pl.when(last)
        def _():
            local, sends, recvs = _exchange_copies(kind, x_in, x_out, *sems)
            for cp in recvs:
                cp.wait_recv()
            for cp in sends:
                cp.wait_send()
            for cp in local:
                cp.wait()

    hbm = pl.BlockSpec(memory_space=pl.ANY)
    out = pl.pallas_call(
        full_body, name=name, grid=grid,
        in_specs=list(in_specs) + [hbm] * n, out_specs=list(out_specs) + [hbm] * n,
        out_shape=list(out_shape) + _exchange_out_shapes(kind, arrays),
        scratch_shapes=list(scratch_shapes) + _exchange_sems(n),
        compiler_params=pltpu.CompilerParams(dimension_semantics=("arbitrary",) * len(grid),
                                             vmem_limit_bytes=VMEM_LIMIT_BYTES, has_side_effects=True),
    )(*operands, *arrays)
    return out[:n_out], out[n_out:]


def _allreduce_small(v):
    R, C = v.shape

    def body(v_ref, o_ref, buf, send_sems, recv_sems):
        me = _my_position()
        buf[_slot(me)] = v_ref[...]
        sends = []
        for k in range(1, N_DEV):
            peer = _flip(me, k)
            cp = pltpu.make_async_remote_copy(
                src_ref=v_ref, dst_ref=buf.at[_slot(me)],
                send_sem=send_sems.at[k - 1], recv_sem=recv_sems.at[k - 1],
                device_id=peer, device_id_type=MESH)
            cp.start()
            sends.append(cp)
        for k in range(1, N_DEV):
            peer = _flip(me, k)
            pltpu.make_async_remote_copy(
                src_ref=v_ref, dst_ref=buf.at[_slot(peer)],
                send_sem=send_sems.at[k - 1], recv_sem=recv_sems.at[k - 1],
                device_id=peer, device_id_type=MESH).wait_recv()
        for cp in sends:
            cp.wait_send()
        tot = buf[0]
        for s in range(1, N_DEV):
            tot = tot + buf[s]
        o_ref[...] = tot

    vm = pl.BlockSpec(memory_space=pltpu.VMEM)
    return pl.pallas_call(
        body, name="allreduce_small",
        in_specs=[vm], out_specs=vm, out_shape=jax.ShapeDtypeStruct((R, C), F32),
        scratch_shapes=[pltpu.VMEM((N_DEV, R, C), F32), pltpu.SemaphoreType.DMA((7,)),
                        pltpu.SemaphoreType.DMA((7,))],
        compiler_params=pltpu.CompilerParams(has_side_effects=True),
    )(v)


def _to_heads(t, heads):
    T = t.shape[0]
    return t.reshape(T, heads, t.shape[1] // heads).transpose(1, 0, 2)


def _from_heads(t):
    H, T, d = t.shape
    return t.transpose(1, 0, 2).reshape(T, H * d)


def _widen(t, width, ones_at=None, pieces_at=None, pieces=None, const_at=None, const=None):
    out = jnp.pad(t, ((0, 0), (0, 0), (0, width - t.shape[-1])))
    lane = lax.broadcasted_iota(jnp.int32, (1, 1, width), 2)
    if ones_at is not None:
        out = jnp.where((lane >= ones_at) & (lane < ones_at + 3), jnp.ones((), BF16), out)
    if pieces_at is not None:
        for i in range(3):
            out = jnp.where(lane == pieces_at + i, pieces[..., i:i + 1], out)
    if const_at is not None:
        out = jnp.where(lane == const_at, jnp.asarray(const, BF16), out)
    return out


def _split3(t):
    hi = lax.reduce_precision(t, 8, 7)
    r = t - hi
    mid = lax.reduce_precision(r, 8, 7)
    lo = lax.reduce_precision(r - mid, 8, 7)
    return jnp.concatenate([hi, mid, lo], axis=-1).astype(BF16)


def _pad_cols(t, n):
    return jnp.pad(t, ((0, 0), (0, n - t.shape[1])))


def _pack_small(mix, ffn, kv, fin, kva, qa, bf, last):
    row6 = jnp.concatenate([kva.reshape(-1), qa.reshape(-1), bf.reshape(-1),
                            jnp.zeros((D_MODEL - KV_LORA - Q_LORA - FOX_HEADS,), F32)])
    return jnp.stack([mix[0], mix[1], ffn[0], ffn[1], kv.reshape(-1), fin.reshape(-1), row6, last])


def _unpack_small(p):
    mix = p[0:2]
    ffn = p[2:4]
    kv = p[4]
    fin = p[5]
    kva = p[6, :KV_LORA]
    qa = p[6, KV_LORA:KV_LORA + Q_LORA].reshape(1, Q_LORA)
    bf = p[6, KV_LORA + Q_LORA:KV_LORA + Q_LORA + FOX_HEADS].reshape(1, FOX_HEADS)
    return mix, ffn, bf, kv, kva, qa, fin


def _mlp_fwd(tag, xin, g, w_up, w_down):
    h = _rms(f"{tag}_norm", xin, g, BF16)

    def act(acc):
        r = jnp.maximum(acc, 0.0)
        return acc, r * r

    u, a = _mm(f"{tag}_up", h, w_up, "nn", (BF16, BF16), epi=act)
    xout = _mm(f"{tag}_down", a, w_down, "nn", (F32,), epi=lambda acc, r: (acc + r,), extras=(xin,))
    return xout, (h, u, a)


def _mlp_bwd(tag, gout, xin, g, w_up, w_down, saved):
    h, u, a = saved
    dw_down = _mm_tn(f"{tag}_dwdown", a, gout)
    du = _mm(f"{tag}_du", gout, w_down, "nt", (BF16,),
             epi=lambda acc, uu: (acc * (2.0 * jnp.maximum(uu.astype(F32), 0.0)),), extras=(u,))
    dw_up = _mm_tn(f"{tag}_dwup", h, du)
    dh = _mm(f"{tag}_dh", du, w_up, "nt", (F32,))
    gin, dg = _rms_bwd(f"{tag}_norm_bwd", xin, g, dh, dres=gout)
    return gin, dg, dw_up, dw_down


def kernel(x, norm_mix_g, norm_ffn_g, fox_w_in, fox_b_f, fox_w_out, kv_norm_g, mla_w_kv_a, mla_kv_a_norm_g, mla_w_kv_b, mla_w_q_a, mla_q_a_norm_g, mla_w_q_b, mla_w_out, ffn_w_up, ffn_w_down, final_norm_g, loss_target, m_norm_mix_g, m_norm_ffn_g, m_fox_w_in, m_fox_b_f, m_fox_w_out, m_kv_norm_g, m_mla_w_kv_a, m_mla_kv_a_norm_g, m_mla_w_kv_b, m_mla_w_q_a, m_mla_q_a_norm_g, m_mla_w_q_b, m_mla_w_out, m_ffn_w_up, m_ffn_w_down, m_final_norm_g, v_norm_mix_g, v_norm_ffn_g, v_fox_w_in, v_fox_b_f, v_fox_w_out, v_kv_norm_g, v_mla_w_kv_a, v_mla_kv_a_norm_g, v_mla_w_kv_b, v_mla_w_q_a, v_mla_q_a_norm_g, v_mla_w_q_b, v_mla_w_out, v_ffn_w_up, v_ffn_w_down, v_final_norm_g):
    T = x.shape[1]
    D = D_MODEL
    tq = 512 if T >= 2048 else 128
    x0 = x[0]
    tgt = loss_target[0]

    gat_fox = _allgather_weights([fox_w_in[0].astype(BF16), fox_w_out[0].astype(BF16)])
    later_shards = [s.astype(BF16) for s in (mla_w_kv_a, mla_w_kv_b, mla_w_q_a[0], mla_w_q_b[0],
                                             mla_w_out[0], ffn_w_up, ffn_w_down)]
    w_in = gat_fox[0].transpose(1, 0, 2).reshape(D, 3 * D + FOX_HEADS)
    w_qkv = w_in[:, :3 * D]
    w_f = _pad_cols(w_in[:, 3 * D:], 128)
    w_fo = gat_fox[1].reshape(D, D)
    g_mix0, g_mix1 = norm_mix_g[0:1], norm_mix_g[1:2]
    g_ffn0, g_ffn1 = norm_ffn_g[0:1], norm_ffn_g[1:2]
    g_kv = kv_norm_g.reshape(1, D)
    g_kva = mla_kv_a_norm_g.reshape(1, KV_LORA)
    g_qa = mla_q_a_norm_g.reshape(1, Q_LORA)
    g_fin = final_norm_g.reshape(1, D)

    inv = 1.0 / (ROPE_BASE ** (jnp.arange(0, QK_ROPE, 2, dtype=F32) / QK_ROPE))
    ang = jnp.arange(T, dtype=F32)[:, None] * inv[None, :]
    cos, sin = jnp.cos(ang), jnp.sin(ang)
    cos2 = jnp.concatenate([cos, cos], axis=-1)
    sgn_sin = jnp.concatenate([-sin, sin], axis=-1)

    h0 = _rms("l0_mix_norm", x0, g_mix0, BF16)
    qkv = _mm("fox_qkv", h0, w_qkv, "nn", (BF16,))
    fl_pad = _mm("fox_gate_logit", h0, w_f, "nn", (F32,))
    fq = _to_heads(qkv[:, :D], FOX_HEADS)
    fk = _to_heads(qkv[:, D:2 * D], FOX_HEADS)
    fv = _to_heads(qkv[:, 2 * D:], FOX_HEADS)
    fl = fl_pad[:, :FOX_HEADS].T
    b_f = fox_b_f.reshape(FOX_HEADS, 1)
    cgate = _gate_cumsum("fox_gate_scan", fl, b_f, tq)
    fox_scale = FOX_HEAD_DIM ** -0.5
    fq_aug = _widen(fq * fox_scale, FOX_AUG, ones_at=FOX_HEAD_DIM)
    fk_aug = _widen(fk, FOX_AUG, ones_at=FOX_HEAD_DIM + 4, pieces_at=FOX_HEAD_DIM,
                    pieces=_split3((-cgate)[..., None]), const_at=FOX_HEAD_DIM + 3, const=1.0 / fox_scale)
    fv_aug = _widen(fv, FOX_AUG, ones_at=FOX_HEAD_DIM)
    (fo, flse), gat = _flash_fwd("fox_attn", fq_aug, fk_aug, fv_aug, FOX_HEAD_DIM, tq,
                                 exchange=("gather", later_shards))
    w_kva = _pad_cols(gat[0].reshape(D, KV_LORA + QK_ROPE), KV_A_PAD)
    w_kvb = gat[1].transpose(1, 0, 2).reshape(KV_LORA, MLA_HEADS * (QK_NOPE + V_HEAD))
    w_qa = gat[2].reshape(D, Q_LORA)
    w_qb = gat[3].transpose(1, 0, 2).reshape(Q_LORA, MLA_HEADS * (QK_NOPE + QK_ROPE))
    w_mo = gat[4].reshape(D, D)
    w_up = gat[5].transpose(1, 2, 0, 3).reshape(2, D, D_FF)
    w_down = gat[6].transpose(1, 0, 2, 3).reshape(2, D_FF, D)
    fctx = _from_heads(fo).astype(BF16)
    x1 = _mm("fox_out", fctx, w_fo, "nn", (F32,), epi=lambda acc, r: (acc + r,), extras=(x0,))
    x2, mlp0 = _mlp_fwd("l0_ffn", x1, g_ffn0, w_up[0], w_down[0])

    src = _rms("kv_norm", x2, g_kv, BF16)
    kva = _mm("kv_a", src, w_kva, "nn", (F32,))
    kva_lat = kva[:, :KV_LORA]
    c_kv = _rms("kv_a_norm", kva_lat, g_kva, BF16)
    k_rope = _rope("k_rope", kva[:, KV_LORA:KV_LORA + QK_ROPE][None], cos2, sgn_sin, BF16)
    kvb = _mm("kv_b", c_kv, w_kvb, "nn", (BF16,))
    kvb_h = _to_heads(kvb, MLA_HEADS)
    mk = jnp.concatenate([kvb_h[:, :, :QK_NOPE],
                          jnp.broadcast_to(k_rope, (MLA_HEADS, T, QK_ROPE))], axis=-1)
    mv = kvb_h[:, :, QK_NOPE:]

    h1 = _rms("l1_mix_norm", x2, g_mix1, BF16)
    qa = _mm("q_a", h1, w_qa, "nn", (F32,))
    c_q = _rms("q_a_norm", qa, g_qa, BF16)
    qf = _mm("q_b", c_q, w_qb, "nn", (F32,))
    qf_h = _to_heads(qf, MLA_HEADS)
    mla_scale = (QK_NOPE + QK_ROPE) ** -0.5
    mq = _mla_q_prep("q_prep", qf_h, cos2, sgn_sin, mla_scale)
    mv_aug = _widen(mv, MLA_AUG, ones_at=V_HEAD)
    (mo, mlse), _ = _flash_fwd("mla_attn", mq, mk, mv_aug, V_HEAD, tq)
    mctx = _from_heads(mo).astype(BF16)
    x3 = _mm("mla_out", mctx, w_mo, "nn", (F32,), epi=lambda acc, r: (acc + r,), extras=(x2,))
    x4, mlp1 = _mlp_fwd("l1_ffn", x3, g_ffn1, w_up[1], w_down[1])

    g4, dg_fin, loss_vec = _loss_head("loss_head", x4, g_fin, tgt)

    g3, dg_ffn1, dw_up1, dw_down1 = _mlp_bwd("l1_ffn", g4, x3, g_ffn1, w_up[1], w_down[1], mlp1)

    dw_mo = _mm_tn("mla_out_dw", mctx, g3)
    dmo = _to_heads(_mm("mla_out_dx", g3, w_mo, "nt", (BF16,)), MLA_HEADS)
    mdelta = _row_dot("mla_delta", mo, dmo)
    dqk = QK_NOPE + QK_ROPE
    mq_bwd = _widen(mq, MLA_AUG, pieces_at=dqk, pieces=_split3(-mlse))
    mk_bwd = _widen(mk, MLA_AUG, ones_at=dqk)
    mdo_aug = _widen(dmo, MLA_AUG, pieces_at=V_HEAD, pieces=_split3(-mdelta))
    mdq = _flash_dq("mla_attn_dq", mq_bwd, mk_bwd, mv_aug, mdo_aug, mla_scale, tq)[0][:, :, :dqk]
    mdk, mdv = _flash_dkv("mla_attn_dkv", mq_bwd, mk_bwd, mv_aug, mdo_aug, tq)
    mdk = mdk[:, :, :dqk]
    mdv = mdv[:, :, :V_HEAD]
    dq_rope = _rope_bwd("q_rope_bwd", mdq[:, :, QK_NOPE:], cos2, sgn_sin, False)
    dqf = _from_heads(jnp.concatenate([mdq[:, :, :QK_NOPE], dq_rope], axis=-1))
    dw_qb = _mm_tn("q_b_dw", c_q, dqf)
    dc_q = _mm("q_b_dx", dqf, w_qb, "nt", (F32,))
    dqa, dg_qa = _rms_bwd("q_a_norm_bwd", qa, g_qa, dc_q)
    dw_qa = _mm_tn("q_a_dw", h1, dqa)
    dh1 = _mm("q_a_dx", dqa, w_qa, "nt", (F32,))
    g2a, dg_mix1 = _rms_bwd("l1_mix_norm_bwd", x2, g_mix1, dh1, dres=g3)

    dk_rope = _rope_bwd("k_rope_bwd", mdk[:, :, QK_NOPE:], cos2, sgn_sin, True)
    dkvb = _from_heads(jnp.concatenate([mdk[:, :, :QK_NOPE], mdv], axis=-1))
    dw_kvb = _mm_tn("kv_b_dw", c_kv, dkvb)
    dc_kv = _mm("kv_b_dx", dkvb, w_kvb, "nt", (F32,))
    dkva_lat, dg_kva = _rms_bwd("kv_a_norm_bwd", kva_lat, g_kva, dc_kv)
    dkva = _pad_cols(jnp.concatenate([dkva_lat, dk_rope], axis=-1), KV_A_PAD)
    dw_kva = _mm_tn("kv_a_dw", src, dkva)[:, :KV_LORA + QK_ROPE]
    dsrc = _mm("kv_a_dx", dkva, w_kva, "nt", (F32,))
    g2, dg_kv = _rms_bwd("kv_norm_bwd", x2, g_kv, dsrc, dres=g2a)

    g1, dg_ffn0, dw_up0, dw_down0 = _mlp_bwd("l0_ffn", g2, x1, g_ffn0, w_up[0], w_down[0], mlp0)

    dw_fo = _mm_tn("fox_out_dw", fctx, g1)
    dfo = _to_heads(_mm("fox_out_dx", g1, w_fo, "nt", (BF16,)), FOX_HEADS)
    fdelta = _row_dot("fox_delta", fo, dfo)
    fq_bwd = _widen(fq * fox_scale, FOX_AUG, ones_at=FOX_HEAD_DIM, pieces_at=FOX_HEAD_DIM + 4,
                    pieces=_split3(-flse))
    fdo_aug = _widen(dfo, FOX_AUG, pieces_at=FOX_HEAD_DIM, pieces=_split3(-fdelta))
    dw_up = jnp.stack([dw_up0, dw_up1])
    dw_down = jnp.stack([dw_down0, dw_down1])
    early = [
        dw_fo.reshape(N_DEV, D // N_DEV, D),
        dw_kva.reshape(N_DEV, D // N_DEV, KV_LORA + QK_ROPE),
        dw_kvb.reshape(KV_LORA, N_DEV, -1).transpose(1, 0, 2),
        dw_qa.reshape(N_DEV, D // N_DEV, Q_LORA),
        dw_qb.reshape(Q_LORA, N_DEV, -1).transpose(1, 0, 2),
        dw_mo.reshape(N_DEV, D // N_DEV, D),
        dw_up.reshape(2, D, N_DEV, -1).transpose(2, 0, 1, 3),
        dw_down.reshape(2, N_DEV, D_FF // N_DEV, D).transpose(1, 0, 2, 3),
    ]
    fdq_aug, early_parts = _flash_dq("fox_attn_dq", fq_bwd, fk_aug, fv_aug, fdo_aug, fox_scale, tq,
                                     exchange=("scatter", [g.astype(BF16) for g in early]))
    fdk_aug, fdv_aug = _flash_dkv("fox_attn_dkv", fq_bwd, fk_aug, fv_aug, fdo_aug, tq)
    fdq = fdq_aug[:, :, :FOX_HEAD_DIM]
    fdk = fdk_aug[:, :, :FOX_HEAD_DIM]
    fdv = fdv_aug[:, :, :FOX_HEAD_DIM]
    dfl, db_f = _gate_cumsum_bwd("fox_gate_scan_bwd", fdq_aug[:, :, FOX_HEAD_DIM + 3],
                                 fdk_aug[:, :, FOX_HEAD_DIM], fl, b_f, tq)
    dqkv = jnp.concatenate([_from_heads(fdq), _from_heads(fdk), _from_heads(fdv)], axis=-1).astype(BF16)
    dfl_pad = _pad_cols(dfl.T, 128)
    dw_qkv = _mm_tn("fox_qkv_dw", h0, dqkv)
    dw_f = _mm_tn("fox_gate_dw", h0, dfl_pad)[:, :FOX_HEADS]
    dw_in = jnp.concatenate([dw_qkv, dw_f], axis=-1)
    dh0a = _mm("fox_gate_dx", dfl_pad, w_f, "nt", (F32,))
    dh0 = _mm("fox_qkv_dx", dqkv, w_qkv, "nt", (F32,), epi=lambda acc, r: (acc + r,), extras=(dh0a,))
    grad_x, dg_mix0 = _rms_bwd("l0_mix_norm_bwd", x0, g_mix0, dh0, dres=g1)

    late = dw_in.reshape(D, N_DEV, -1).transpose(1, 0, 2).astype(BF16)
    parts = list(_alltoall_grads("alltoall_fox_w_in", [late])) + list(early_parts)

    names = ["fox_w_in", "fox_w_out", "mla_w_kv_a", "mla_w_kv_b", "mla_w_q_a", "mla_w_q_b",
             "mla_w_out", "ffn_w_up", "ffn_w_down"]
    moms = [m_fox_w_in, m_fox_w_out, m_mla_w_kv_a, m_mla_w_kv_b, m_mla_w_q_a, m_mla_w_q_b,
            m_mla_w_out, m_ffn_w_up, m_ffn_w_down]
    vars_ = [v_fox_w_in, v_fox_w_out, v_mla_w_kv_a, v_mla_w_kv_b, v_mla_w_q_a, v_mla_w_q_b,
             v_mla_w_out, v_ffn_w_up, v_ffn_w_down]
    full = [fox_w_in, fox_w_out, mla_w_kv_a, mla_w_kv_b, mla_w_q_a, mla_w_q_b, mla_w_out,
            ffn_w_up, ffn_w_down]
    big = {}
    for nm, p, w, m, v in zip(names, parts, full, moms, vars_):
        C = w.shape[-1]
        res = _adamw(f"adamw_{nm}", p.reshape(N_DEV, -1, C), w.reshape(-1, C), m.reshape(-1, C),
                     v.reshape(-1, C))
        big[nm] = [r.reshape(w.shape) for r in res]

    zrow = jnp.zeros((D,), F32)
    g_small = _pack_small(jnp.concatenate([dg_mix0, dg_mix1]), jnp.concatenate([dg_ffn0, dg_ffn1]),
                          dg_kv, dg_fin, dg_kva, dg_qa, db_f, zrow.at[0].set(loss_vec[0, 0]))
    tot_small = _allreduce_small(g_small)
    w_small = _pack_small(norm_mix_g, norm_ffn_g, kv_norm_g, final_norm_g, mla_kv_a_norm_g,
                          mla_q_a_norm_g, fox_b_f, zrow)
    m_small = _pack_small(m_norm_mix_g, m_norm_ffn_g, m_kv_norm_g, m_final_norm_g, m_mla_kv_a_norm_g,
                          m_mla_q_a_norm_g, m_fox_b_f, zrow)
    v_small = _pack_small(v_norm_mix_g, v_norm_ffn_g, v_kv_norm_g, v_final_norm_g, v_mla_kv_a_norm_g,
                          v_mla_q_a_norm_g, v_fox_b_f, zrow)
    small = _adamw("adamw_small", tot_small[None], w_small, m_small, v_small)
    loss = tot_small[7, 0]
    small = [_unpack_small(s) for s in small]

    def ordered(i):
        mix, ffn, bf, kv, kva, qa, fin = small[i]
        return [mix, ffn, big["fox_w_in"][i], bf, big["fox_w_out"][i], kv, big["mla_w_kv_a"][i], kva,
                big["mla_w_kv_b"][i], big["mla_w_q_a"][i], qa, big["mla_w_q_b"][i],
                big["mla_w_out"][i], big["ffn_w_up"][i], big["ffn_w_down"][i], fin]

    return (loss, grad_x[None], *ordered(0), *ordered(1), *ordered(2), *ordered(3))
```

```python
import functools
import math

import jax
import jax.numpy as jnp
from jax import lax
from jax.experimental import pallas as pl
from jax.experimental.pallas import tpu as pltpu

F32 = jnp.float32
BF16 = jnp.bfloat16
MESH = pl.DeviceIdType.MESH

N_DEV = 8
D_MODEL = 1024
FOX_HEADS = 16
FOX_HEAD_DIM = 64
FOX_AUG = 128
MLA_AUG = 256
MLA_HEADS = 8
QK_NOPE = 128
QK_ROPE = 64
V_HEAD = 128
Q_LORA = 384
KV_LORA = 256
KV_A_PAD = 384
D_FF = 4096
ROPE_BASE = 10000.0
EPS = 1e-6
NEG = -1e30

ADAM_LR = 0.001
ADAM_B1 = 0.9
ADAM_B2 = 0.999
ADAM_EPS = 1e-08
ADAM_WD = 0.01
ADAM_STEP = 10

VMEM_LIMIT_BYTES = 56 * 1024 * 1024

NN = (((1,), (0,)), ((), ()))
NT = (((1,), (1,)), ((), ()))
TN = (((0,), (0,)), ((), ()))
_FORMS = {"nn": NN, "nt": NT}


def _cparams(sem=None):
    return pltpu.CompilerParams(dimension_semantics=sem, vmem_limit_bytes=VMEM_LIMIT_BYTES)


def _pick(n, cands):
    for c in cands:
        if c <= n and n % c == 0:
            return c
    return n


def _dot(a, b, dims):
    return lax.dot_general(a, b, dims, preferred_element_type=F32)


def _mm(name, a, b, form, out_dtypes, epi=None, extras=(), tm=1024, tn=512):
    M, K = a.shape
    N = b.shape[1] if form == "nn" else b.shape[0]
    tm = _pick(M, (tm, 512, 256, 128))
    tn = _pick(N, (tn, 384, 256, 128))
    n_ex = len(extras)

    def body(*refs):
        a_ref, b_ref = refs[0], refs[1]
        ex = refs[2:2 + n_ex]
        outs = refs[2 + n_ex:]
        acc = _dot(a_ref[...].astype(BF16), b_ref[...].astype(BF16), _FORMS[form])
        res = epi(acc, *[e[...] for e in ex]) if epi is not None else (acc,)
        for o_ref, r in zip(outs, res):
            o_ref[...] = r.astype(o_ref.dtype)

    if form == "nn":
        b_spec = pl.BlockSpec((K, tn), lambda i, j: (0, j))
    else:
        b_spec = pl.BlockSpec((tn, K), lambda i, j: (j, 0))
    tile = pl.BlockSpec((tm, tn), lambda i, j: (i, j))
    out = pl.pallas_call(
        body, name=name, grid=(M // tm, N // tn),
        in_specs=[pl.BlockSpec((tm, K), lambda i, j: (i, 0)), b_spec] + [tile] * n_ex,
        out_specs=[tile] * len(out_dtypes),
        out_shape=[jax.ShapeDtypeStruct((M, N), dt) for dt in out_dtypes],
        compiler_params=_cparams(("parallel", "arbitrary")),
    )(a, b, *extras)
    return out if len(out_dtypes) > 1 else out[0]


def _mm_tn(name, a, b):
    T, Ka = a.shape
    N = b.shape[1]
    tk = _pick(Ka, (1024, 512, 384, 256, 128))
    tn = _pick(N, (1024, 768, 512, 384, 256, 128))
    tt = _pick(T, (1024, 512, 256, 128))

    def body(a_ref, b_ref, o_ref):
        @pl.when(pl.program_id(2) == 0)
        def _():
            o_ref[...] = jnp.zeros_like(o_ref)

        o_ref[...] += _dot(a_ref[...].astype(BF16), b_ref[...].astype(BF16), TN)

    return pl.pallas_call(
        body, name=name, grid=(Ka // tk, N // tn, T // tt),
        in_specs=[pl.BlockSpec((tt, tk), lambda i, j, t: (t, i)),
                  pl.BlockSpec((tt, tn), lambda i, j, t: (t, j))],
        out_specs=pl.BlockSpec((tk, tn), lambda i, j, t: (i, j)),
        out_shape=jax.ShapeDtypeStruct((Ka, N), F32),
        compiler_params=_cparams(("parallel", "parallel", "arbitrary")),
    )(a, b)


def _rms(name, x, g, out_dtype):
    T, D = x.shape
    tm = _pick(T, (1024, 512, 256, 128))

    def body(x_ref, g_ref, o_ref):
        xf = x_ref[...]
        r = lax.rsqrt(jnp.mean(xf * xf, axis=-1, keepdims=True) + EPS)
        o_ref[...] = (xf * r * g_ref[...]).astype(o_ref.dtype)

    return pl.pallas_call(
        body, name=name, grid=(T // tm,),
        in_specs=[pl.BlockSpec((tm, D), lambda i: (i, 0)), pl.BlockSpec((1, D), lambda i: (0, 0))],
        out_specs=pl.BlockSpec((tm, D), lambda i: (i, 0)),
        out_shape=jax.ShapeDtypeStruct((T, D), out_dtype),
        compiler_params=_cparams(("parallel",)),
    )(x, g)


def _rms_bwd(name, x, g, dh, dres=None):
    T, D = x.shape
    tm = _pick(T, (512, 256, 128))
    has_res = dres is not None

    def body(*refs):
        if has_res:
            x_ref, g_ref, dh_ref, dres_ref, dx_ref, dg_ref = refs
        else:
            x_ref, g_ref, dh_ref, dx_ref, dg_ref = refs

        @pl.when(pl.program_id(0) == 0)
        def _():
            dg_ref[...] = jnp.zeros_like(dg_ref)

        xf = x_ref[...]
        r = lax.rsqrt(jnp.mean(xf * xf, axis=-1, keepdims=True) + EPS)
        xhat = xf * r
        dy = dh_ref[...].astype(F32)
        dxh = dy * g_ref[...]
        dx = r * (dxh - xhat * jnp.mean(dxh * xhat, axis=-1, keepdims=True))
        if has_res:
            dx = dx + dres_ref[...]
        dx_ref[...] = dx
        dg_ref[...] += jnp.sum(dy * xhat, axis=0, keepdims=True)

    row = pl.BlockSpec((tm, D), lambda i: (i, 0))
    vec = pl.BlockSpec((1, D), lambda i: (0, 0))
    ins = [x, g, dh] + ([dres] if has_res else [])
    return pl.pallas_call(
        body, name=name, grid=(T // tm,),
        in_specs=[row, vec, row] + ([row] if has_res else []),
        out_specs=[row, vec],
        out_shape=[jax.ShapeDtypeStruct((T, D), F32), jax.ShapeDtypeStruct((1, D), F32)],
        compiler_params=_cparams(("arbitrary",)),
    )(*ins)


def _loss_head(name, x, g, tgt):
    T, D = x.shape
    tm = _pick(T, (512, 256, 128))

    def body(x_ref, g_ref, t_ref, dx_ref, dg_ref, loss_ref):
        @pl.when(pl.program_id(0) == 0)
        def _():
            dg_ref[...] = jnp.zeros_like(dg_ref)
            loss_ref[...] = jnp.zeros_like(loss_ref)

        xf = x_ref[...]
        r = lax.rsqrt(jnp.mean(xf * xf, axis=-1, keepdims=True) + EPS)
        xhat = xf * r
        gv = g_ref[...]
        err = xhat * gv - t_ref[...]
        row_loss = jnp.mean(err * err, axis=-1, keepdims=True)
        loss_ref[...] += 0.5 * jnp.sum(row_loss, axis=0, keepdims=True)
        dy = err * (1.0 / D)
        dxh = dy * gv
        dx_ref[...] = r * (dxh - xhat * jnp.mean(dxh * xhat, axis=-1, keepdims=True))
        dg_ref[...] += jnp.sum(dy * xhat, axis=0, keepdims=True)

    row = pl.BlockSpec((tm, D), lambda i: (i, 0))
    vec = pl.BlockSpec((1, D), lambda i: (0, 0))
    return pl.pallas_call(
        body, name=name, grid=(T // tm,),
        in_specs=[row, vec, row],
        out_specs=[row, vec, pl.BlockSpec((1, 128), lambda i: (0, 0))],
        out_shape=[jax.ShapeDtypeStruct((T, D), F32), jax.ShapeDtypeStruct((1, D), F32),
                   jax.ShapeDtypeStruct((1, 128), F32)],
        compiler_params=_cparams(("arbitrary",)),
    )(x, g, tgt)


def _swap_halves(t):
    half = t.shape[-1] // 2
    return jnp.concatenate([t[:, half:], t[:, :half]], axis=-1)


def _rope(name, t, cos2, sgn_sin, out_dtype):
    H, T, R = t.shape
    tm = _pick(T, (1024, 512, 256, 128))

    def body(t_ref, c_ref, s_ref, o_ref):
        tf = t_ref[...].astype(F32)
        o_ref[...] = (tf * c_ref[...] + _swap_halves(tf) * s_ref[...]).astype(o_ref.dtype)

    slab = pl.BlockSpec((None, tm, R), lambda h, i: (h, i, 0))
    tab = pl.BlockSpec((tm, R), lambda h, i: (i, 0))
    return pl.pallas_call(
        body, name=name, grid=(H, T // tm),
        in_specs=[slab, tab, tab], out_specs=slab,
        out_shape=jax.ShapeDtypeStruct((H, T, R), out_dtype),
        compiler_params=_cparams(("parallel", "parallel")),
    )(t, cos2, sgn_sin)


def _mla_q_prep(name, qf, cos2, sgn_sin, scale):
    H, T, W = qf.shape
    R = cos2.shape[1]
    tm = _pick(T, (1024, 512, 256, 128))

    def body(t_ref, c_ref, s_ref, o_ref):
        tf = t_ref[...]
        r = tf[:, W - R:]
        roped = r * c_ref[...] + _swap_halves(r) * s_ref[...]
        o_ref[...] = (jnp.concatenate([tf[:, :W - R], roped], axis=-1) * scale).astype(o_ref.dtype)

    slab = pl.BlockSpec((None, tm, W), lambda h, i: (h, i, 0))
    tab = pl.BlockSpec((tm, R), lambda h, i: (i, 0))
    return pl.pallas_call(
        body, name=name, grid=(H, T // tm),
        in_specs=[slab, tab, tab], out_specs=slab,
        out_shape=jax.ShapeDtypeStruct((H, T, W), BF16),
        compiler_params=_cparams(("parallel", "parallel")),
    )(qf, cos2, sgn_sin)


def _rope_bwd(name, dy, cos2, sgn_sin, sum_heads):
    H, T, R = dy.shape
    tm = _pick(T, (1024, 512, 256, 128))

    def body(d_ref, c_ref, s_ref, o_ref):
        d = d_ref[...]
        if sum_heads:
            tot = d[0]
            for h in range(1, H):
                tot = tot + d[h]
            d = tot
        o_ref[...] = d * c_ref[...] + _swap_halves(d * s_ref[...])

    if sum_heads:
        grid = (T // tm,)
        in_slab = pl.BlockSpec((H, tm, R), lambda i: (0, i, 0))
        out_slab = pl.BlockSpec((tm, R), lambda i: (i, 0))
        tab = pl.BlockSpec((tm, R), lambda i: (i, 0))
        out_shape = jax.ShapeDtypeStruct((T, R), F32)
        sem = ("parallel",)
    else:
        grid = (H, T // tm)
        in_slab = pl.BlockSpec((None, tm, R), lambda h, i: (h, i, 0))
        out_slab = in_slab
        tab = pl.BlockSpec((tm, R), lambda h, i: (i, 0))
        out_shape = jax.ShapeDtypeStruct((H, T, R), F32)
        sem = ("parallel", "parallel")
    return pl.pallas_call(
        body, name=name, grid=grid, in_specs=[in_slab, tab, tab], out_specs=out_slab,
        out_shape=out_shape, compiler_params=_cparams(sem),
    )(dy, cos2, sgn_sin)


def _log_sigmoid(z):
    return jnp.minimum(z, 0.0) - jnp.log(1.0 + jnp.exp(-jnp.abs(z)))


def _gate_cumsum(name, fl, b, tb):
    H, T = fl.shape

    def body(f_ref, b_ref, c_ref, carry):
        @pl.when(pl.program_id(0) == 0)
        def _():
            carry[...] = jnp.zeros_like(carry)

        ls = _log_sigmoid(f_ref[...] + b_ref[...])
        src = lax.broadcasted_iota(jnp.int32, (tb, tb), 0)
        dst = lax.broadcasted_iota(jnp.int32, (tb, tb), 1)
        tri = (src <= dst).astype(F32)
        c = lax.dot_general(ls, tri, NN, precision=lax.Precision.HIGHEST,
                            preferred_element_type=F32) + carry[...]
        c_ref[...] = c
        carry[...] = carry[...] + jnp.sum(ls, axis=-1, keepdims=True)

    return pl.pallas_call(
        body, name=name, grid=(T // tb,),
        in_specs=[pl.BlockSpec((H, tb), lambda i: (0, i)), pl.BlockSpec((H, 1), lambda i: (0, 0))],
        out_specs=pl.BlockSpec((H, tb), lambda i: (0, i)),
        out_shape=jax.ShapeDtypeStruct((H, T), F32),
        scratch_shapes=[pltpu.VMEM((H, 1), F32)],
        compiler_params=_cparams(("arbitrary",)),
    )(fl, b)


def _gate_cumsum_bwd(name, d_query, d_key, fl, b, tb):
    H, T = fl.shape
    nb = T // tb

    def body(dq_ref, dk_ref, f_ref, b_ref, dfl_ref, db_ref, carry):
        @pl.when(pl.program_id(0) == 0)
        def _():
            carry[...] = jnp.zeros_like(carry)
            db_ref[...] = jnp.zeros_like(db_ref)

        d = dq_ref[...] - dk_ref[...]
        src = lax.broadcasted_iota(jnp.int32, (tb, tb), 0)
        dst = lax.broadcasted_iota(jnp.int32, (tb, tb), 1)
        tri = (src >= dst).astype(F32)
        dls = lax.dot_general(d, tri, NN, precision=lax.Precision.HIGHEST,
                              preferred_element_type=F32) + carry[...]
        z = f_ref[...] + b_ref[...]
        dfl = dls * (1.0 / (1.0 + jnp.exp(z)))
        dfl_ref[...] = dfl
        db_ref[...] += jnp.sum(dfl, axis=-1, keepdims=True)
        carry[...] = carry[...] + jnp.sum(d, axis=-1, keepdims=True)

    blk = pl.BlockSpec((H, tb), lambda i: (0, nb - 1 - i))
    vec = pl.BlockSpec((H, 1), lambda i: (0, 0))
    return pl.pallas_call(
        body, name=name, grid=(nb,),
        in_specs=[blk, blk, blk, vec], out_specs=[blk, vec],
        out_shape=[jax.ShapeDtypeStruct((H, T), F32), jax.ShapeDtypeStruct((H, 1), F32)],
        scratch_shapes=[pltpu.VMEM((H, 1), F32)],
        compiler_params=_cparams(("arbitrary",)),
    )(d_query, d_key, fl, b)


def _causal_mask(tq, rows_are_queries):
    r = lax.broadcasted_iota(jnp.int32, (tq, tq), 0)
    c = lax.broadcasted_iota(jnp.int32, (tq, tq), 1)
    return (c <= r) if rows_are_queries else (r <= c)


def _chunk_rows(j, tq):
    return pl.ds(pl.multiple_of(j * tq, tq), tq)


def _flash_fwd(name, q, k, v_aug, dv, tq, exchange=None):
    H, T, dqk = q.shape
    dva = v_aug.shape[2]
    nq = T // tq

    def body(q_ref, k_ref, v_ref, o_ref, lse_ref, m_sc, acc_sc):
        qi = pl.program_id(1)
        m_sc[...] = jnp.full_like(m_sc, NEG)
        acc_sc[...] = jnp.zeros_like(acc_sc)

        def chunk(j, masked):
            rows = _chunk_rows(j, tq)
            s = _dot(q_ref[...], k_ref[rows, :], NT)
            if masked:
                s = jnp.where(_causal_mask(tq, True), s, NEG)
            m_prev = m_sc[...]
            m_new = jnp.maximum(m_prev, jnp.max(s, axis=1, keepdims=True))
            p = jnp.exp(s - jnp.tile(m_new, (1, tq // 128)))
            alpha = jnp.tile(jnp.exp(m_prev - m_new), (1, dva // 128))
            acc_sc[...] = alpha * acc_sc[...] + _dot(p.astype(BF16), v_ref[rows, :], NN)
            m_sc[...] = m_new

        def off_diagonal(j, carry):
            chunk(j, False)
            return carry

        lax.fori_loop(0, qi, off_diagonal, 0)
        chunk(qi, True)
        acc = acc_sc[...]
        l = acc[:, dv:dv + 1]
        o_ref[...] = acc[:, :dv] / l
        lse_ref[...] = m_sc[:, :1] + jnp.log(l)

    return _call_carrying(
        body, name, (H, nq),
        in_specs=[pl.BlockSpec((None, tq, dqk), lambda h, i: (h, i, 0)),
                  pl.BlockSpec((None, T, dqk), lambda h, i: (h, 0, 0)),
                  pl.BlockSpec((None, T, dva), lambda h, i: (h, 0, 0))],
        out_specs=[pl.BlockSpec((None, tq, dv), lambda h, i: (h, i, 0)),
                   pl.BlockSpec((None, tq, 1), lambda h, i: (h, i, 0))],
        out_shape=[jax.ShapeDtypeStruct((H, T, dv), F32), jax.ShapeDtypeStruct((H, T, 1), F32)],
        scratch_shapes=[pltpu.VMEM((tq, 128), F32), pltpu.VMEM((tq, dva), F32)],
        operands=(q, k, v_aug), exchange=exchange)


def _row_dot(name, a, b):
    H, T, d = a.shape
    tm = _pick(T, (1024, 512, 256, 128))

    def body(a_ref, b_ref, o_ref):
        o_ref[...] = jnp.sum(a_ref[...].astype(F32) * b_ref[...].astype(F32), axis=-1, keepdims=True)

    slab = pl.BlockSpec((None, tm, d), lambda h, i: (h, i, 0))
    return pl.pallas_call(
        body, name=name, grid=(H, T // tm), in_specs=[slab, slab],
        out_specs=pl.BlockSpec((None, tm, 1), lambda h, i: (h, i, 0)),
        out_shape=jax.ShapeDtypeStruct((H, T, 1), F32),
        compiler_params=_cparams(("parallel", "parallel")),
    )(a, b)


def _flash_bwd(name, q, k, v, do, scale, tq, exchange=None):
    H, T, dqk = q.shape
    dva = v.shape[2]
    nq = T // tq

    def body(q_ref, k_ref, v_ref, do_ref, dq_ref, dk_ref, dv_ref, dk_sc, dv_sc):
        ki = pl.program_id(1)
        dk_sc[...] = jnp.zeros_like(dk_sc)
        dv_sc[...] = jnp.zeros_like(dv_sc)

        @pl.when(ki == 0)
        def _():
            dq_ref[...] = jnp.zeros_like(dq_ref)

        def chunk(i, masked):
            rows = _chunk_rows(i, tq)
            qb = q_ref[rows, :]
            dob = do_ref[rows, :]
            kb = k_ref[...]
            st = _dot(kb, qb, NT)
            if masked:
                st = jnp.where(_causal_mask(tq, False), st, NEG)
            pt = jnp.exp(st)
            dv_sc[...] += _dot(pt.astype(BF16), dob, NN)
            dst = (pt * _dot(v_ref[...], dob, NT)).astype(BF16)
            dk_sc[...] += _dot(dst, qb, NN)
            dq_ref[rows, :] += _dot(dst, kb, TN)

        def off_diagonal(i, carry):
            chunk(i, False)
            return carry

        chunk(ki, True)
        lax.fori_loop(ki + 1, nq, off_diagonal, 0)
        dk_ref[...] = dk_sc[...]
        dv_ref[...] = dv_sc[...]

        @pl.when(ki == nq - 1)
        def _():
            dq_ref[...] = dq_ref[...] * scale

    whole_q = pl.BlockSpec((None, T, dqk), lambda h, j: (h, 0, 0))
    k_spec = pl.BlockSpec((None, tq, dqk), lambda h, j: (h, j, 0))
    v_spec = pl.BlockSpec((None, tq, dva), lambda h, j: (h, j, 0))
    return _call_carrying(
        body, name, (H, nq),
        in_specs=[whole_q, k_spec, v_spec, pl.BlockSpec((None, T, dva), lambda h, j: (h, 0, 0))],
        out_specs=[whole_q, k_spec, v_spec],
        out_shape=[jax.ShapeDtypeStruct((H, T, dqk), F32), jax.ShapeDtypeStruct((H, T, dqk), F32),
                   jax.ShapeDtypeStruct((H, T, dva), F32)],
        scratch_shapes=[pltpu.VMEM((tq, dqk), F32), pltpu.VMEM((tq, dva), F32)],
        operands=(q, k, v, do), exchange=exchange)


def _adamw_math(w, g, m, v):
    m = ADAM_B1 * m + (1.0 - ADAM_B1) * g
    v = ADAM_B2 * v + (1.0 - ADAM_B2) * (g * g)
    m_hat = m / (1.0 - ADAM_B1 ** ADAM_STEP)
    v_hat = v / (1.0 - ADAM_B2 ** ADAM_STEP)
    delta = -ADAM_LR * (m_hat / (jnp.sqrt(v_hat) + ADAM_EPS) + ADAM_WD * w)
    return delta, m, v


def _adamw(name, parts, w, m, v):
    P, R, C = parts.shape
    tr = _pick(R, (256, 128, 64, 32, 16, 8))

    def body(p_ref, w_ref, m_ref, v_ref, g_out, d_out, m_out, v_out):
        g = p_ref[0].astype(F32)
        for i in range(1, P):
            g = g + p_ref[i].astype(F32)
        delta, m_new, v_new = _adamw_math(w_ref[...], g, m_ref[...], v_ref[...])
        g_out[...] = g
        d_out[...] = delta
        m_out[...] = m_new
        v_out[...] = v_new

    blk = pl.BlockSpec((tr, C), lambda i: (i, 0))
    sds = jax.ShapeDtypeStruct((R, C), F32)
    return pl.pallas_call(
        body, name=name, grid=(R // tr,),
        in_specs=[pl.BlockSpec((P, tr, C), lambda i: (0, i, 0)), blk, blk, blk],
        out_specs=[blk] * 4, out_shape=[sds] * 4,
        compiler_params=_cparams(("parallel",)),
    )(parts, w, m, v)


def _my_position():
    return lax.axis_index("x"), lax.axis_index("y"), lax.axis_index("c")


def _slot(p):
    return 4 * p[0] + 2 * p[1] + p[2]


def _flip(p, k):
    return tuple((1 - p[i]) if (k >> (2 - i)) & 1 else p[i] for i in range(3))


def _allgather_weights(shards):
    n = len(shards)

    def body(*refs):
        ins = refs[:n]
        outs = refs[n:2 * n]
        send_sems, recv_sems, local_sems = refs[2 * n:]
        x, y, c = _my_position()
        me, sibling = (x, y, c), (x, y, 1 - c)
        chips = [(1 - x, y), (x, 1 - y), (1 - x, 1 - y)]

        def copy(a, k, block, to, src=None):
            dst = outs[a].at[_slot(block)]
            return pltpu.make_async_remote_copy(
                src_ref=dst if src is None else src, dst_ref=dst,
                send_sem=send_sems.at[7 * a + k], recv_sem=recv_sems.at[7 * a + k],
                device_id=to, device_id_type=MESH)

        started = []
        for a in range(n):
            mine = pltpu.make_async_copy(ins[a], outs[a].at[_slot(me)], local_sems.at[a])
            mine.start()
            started.append(mine)
        first = []
        for a in range(n):
            first.append(copy(a, 0, me, sibling, src=ins[a]))
            first += [copy(a, 1 + j, me, (*chip, c), src=ins[a]) for j, chip in enumerate(chips)]
        for cp in first:
            cp.start()
        passed = []
        for j, chip in enumerate(chips):
            for a in range(n):
                copy(a, 1 + j, (*chip, c), me).wait_recv()
                fwd = copy(a, 4 + j, (*chip, c), sibling)
                fwd.start()
                passed.append(fwd)
        for a in range(n):
            copy(a, 0, sibling, me).wait_recv()
            for j, chip in enumerate(chips):
                copy(a, 4 + j, (*chip, 1 - c), me).wait_recv()
        for cp in first + passed:
            cp.wait_send()
        for mine in started:
            mine.wait()

    hbm = pl.BlockSpec(memory_space=pl.ANY)
    return pl.pallas_call(
        body, name="allgather_weights",
        in_specs=[hbm] * n, out_specs=[hbm] * n,
        out_shape=[jax.ShapeDtypeStruct((N_DEV,) + s.shape, s.dtype) for s in shards],
        scratch_shapes=[pltpu.SemaphoreType.DMA((7 * n,)), pltpu.SemaphoreType.DMA((7 * n,)),
                        pltpu.SemaphoreType.DMA((n,))],
        compiler_params=pltpu.CompilerParams(has_side_effects=True),
    )(*shards)


def _exchange_copies(kind, x_in, x_out, send_sems, recv_sems, local_sems, receives=True):
    me = _my_position()
    mine = _slot(me)
    local, sends, recvs = [], [], []
    for a in range(len(x_in)):
        src = x_in[a] if kind == "gather" else x_in[a].at[mine]
        local.append(pltpu.make_async_copy(src, x_out[a].at[mine], local_sems.at[a]))
    for k in range(1, N_DEV):
        peer = _flip(me, k)
        theirs = _slot(peer)
        for a in range(len(x_in)):
            src = x_in[a] if kind == "gather" else x_in[a].at[theirs]
            ends = [(x_out[a].at[mine], sends)] + ([(x_out[a].at[theirs], recvs)] if receives else [])
            for dst, group in ends:
                group.append(pltpu.make_async_remote_copy(
                    src_ref=src, dst_ref=dst, send_sem=send_sems.at[7 * a + k - 1],
                    recv_sem=recv_sems.at[7 * a + k - 1], device_id=peer, device_id_type=MESH))
    return local, sends, recvs


def _exchange_out_shapes(kind, arrays):
    return [jax.ShapeDtypeStruct(((N_DEV,) + a.shape) if kind == "gather" else a.shape, a.dtype)
            for a in arrays]


def _exchange_sems(n):
    return [pltpu.SemaphoreType.DMA((7 * n,)), pltpu.SemaphoreType.DMA((7 * n,)),
            pltpu.SemaphoreType.DMA((n,))]


def _alltoall_grads(name, grads):
    n = len(grads)

    def body(*refs):
        local, sends, recvs = _exchange_copies("scatter", refs[:n], refs[n:2 * n], *refs[2 * n:])
        for cp in local + sends:
            cp.start()
        for cp in recvs:
            cp.wait_recv()
        for cp in sends:
            cp.wait_send()
        for cp in local:
            cp.wait()

    hbm = pl.BlockSpec(memory_space=pl.ANY)
    return pl.pallas_call(
        body, name=name,
        in_specs=[hbm] * n, out_specs=[hbm] * n,
        out_shape=_exchange_out_shapes("scatter", grads), scratch_shapes=_exchange_sems(n),
        compiler_params=pltpu.CompilerParams(has_side_effects=True),
    )(*grads)


def _call_carrying(body, name, grid, in_specs, out_specs, out_shape, scratch_shapes, operands, exchange):
    if exchange is None:
        out = pl.pallas_call(
            body, name=name, grid=grid, in_specs=in_specs, out_specs=out_specs, out_shape=out_shape,
            scratch_shapes=scratch_shapes, compiler_params=_cparams(("parallel",) + ("arbitrary",) * (len(grid) - 1)),
        )(*operands)
        return out, None
    kind, arrays = exchange
    n, n_in, n_out, n_sc = len(arrays), len(in_specs), len(out_specs), len(scratch_shapes)

    def full_body(*refs):
        ins, refs = refs[:n_in], refs[n_in:]
        x_in, refs = refs[:n], refs[n:]
        outs, refs = refs[:n_out], refs[n_out:]
        x_out, refs = refs[:n], refs[n:]
        scratch, sems = refs[:n_sc], refs[n_sc:]
        first = last = None
        for axis, size in enumerate(grid):
            at_start = pl.program_id(axis) == 0
            at_end = pl.program_id(axis) == size - 1
            first = at_start if first is None else jnp.logical_and(first, at_start)
            last = at_end if last is None else jnp.logical_and(last, at_end)

        @pl.when(first)
        def _():
            local, sends, _ = _exchange_copies(kind, x_in, x_out, *sems, receives=False)
            for cp in local + sends:
                cp.start()

        body(*ins, *outs, *scratch)

        @pl.when(last)
        def _():
            local, sends, recvs = _exchange_copies(kind, x_in, x_out, *sems)
            for cp in recvs:
                cp.wait_recv()
            for cp in sends:
                cp.wait_send()
            for cp in local:
                cp.wait()

    hbm = pl.BlockSpec(memory_space=pl.ANY)
    out = pl.pallas_call(
        full_body, name=name, grid=grid,
        in_specs=list(in_specs) + [hbm] * n, out_specs=list(out_specs) + [hbm] * n,
        out_shape=list(out_shape) + _exchange_out_shapes(kind, arrays),
        scratch_shapes=list(scratch_shapes) + _exchange_sems(n),
        compiler_params=pltpu.CompilerParams(dimension_semantics=("arbitrary",) * len(grid),
                                             vmem_limit_bytes=VMEM_LIMIT_BYTES, has_side_effects=True),
    )(*operands, *arrays)
    return out[:n_out], out[n_out:]


def _allreduce_small(v):
    R, C = v.shape

    def body(v_ref, o_ref, buf, send_sems, recv_sems):
        me = _my_position()
        buf[_slot(me)] = v_ref[...]
        sends = []
        for k in range(1, N_DEV):
            peer = _flip(me, k)
            cp = pltpu.make_async_remote_copy(
                src_ref=v_ref, dst_ref=buf.at[_slot(me)],
                send_sem=send_sems.at[k - 1], recv_sem=recv_sems.at[k - 1],
                device_id=peer, device_id_type=MESH)
            cp.start()
            sends.append(cp)
        for k in range(1, N_DEV):
            peer = _flip(me, k)
            pltpu.make_async_remote_copy(
                src_ref=v_ref, dst_ref=buf.at[_slot(peer)],
                send_sem=send_sems.at[k - 1], recv_sem=recv_sems.at[k - 1],
                device_id=peer, device_id_type=MESH).wait_recv()
        for cp in sends:
            cp.wait_send()
        tot = buf[0]
        for s in range(1, N_DEV):
            tot = tot + buf[s]
        o_ref[...] = tot

    vm = pl.BlockSpec(memory_space=pltpu.VMEM)
    return pl.pallas_call(
        body, name="allreduce_small",
        in_specs=[vm], out_specs=vm, out_shape=jax.ShapeDtypeStruct((R, C), F32),
        scratch_shapes=[pltpu.VMEM((N_DEV, R, C), F32), pltpu.SemaphoreType.DMA((7,)),
                        pltpu.SemaphoreType.DMA((7,))],
        compiler_params=pltpu.CompilerParams(has_side_effects=True),
    )(v)


def _to_heads(t, heads):
    T = t.shape[0]
    return t.reshape(T, heads, t.shape[1] // heads).transpose(1, 0, 2)


def _from_heads(t):
    H, T, d = t.shape
    return t.transpose(1, 0, 2).reshape(T, H * d)


def _widen(t, width, ones_at=None, pieces_at=None, pieces=None, const_at=None, const=None):
    out = jnp.pad(t, ((0, 0), (0, 0), (0, width - t.shape[-1])))
    lane = lax.broadcasted_iota(jnp.int32, (1, 1, width), 2)
    if ones_at is not None:
        out = jnp.where((lane >= ones_at) & (lane < ones_at + 3), jnp.ones((), BF16), out)
    if pieces_at is not None:
        for i in range(3):
            out = jnp.where(lane == pieces_at + i, pieces[..., i:i + 1], out)
    if const_at is not None:
        out = jnp.where(lane == const_at, jnp.asarray(const, BF16), out)
    return out


def _split3(t):
    hi = lax.reduce_precision(t, 8, 7)
    r = t - hi
    mid = lax.reduce_precision(r, 8, 7)
    lo = lax.reduce_precision(r - mid, 8, 7)
    return jnp.concatenate([hi, mid, lo], axis=-1).astype(BF16)


def _pad_cols(t, n):
    return jnp.pad(t, ((0, 0), (0, n - t.shape[1])))


def _pack_small(mix, ffn, kv, fin, kva, qa, bf, last):
    row6 = jnp.concatenate([kva.reshape(-1), qa.reshape(-1), bf.reshape(-1),
                            jnp.zeros((D_MODEL - KV_LORA - Q_LORA - FOX_HEADS,), F32)])
    return jnp.stack([mix[0], mix[1], ffn[0], ffn[1], kv.reshape(-1), fin.reshape(-1), row6, last])


def _unpack_small(p):
    mix = p[0:2]
    ffn = p[2:4]
    kv = p[4]
    fin = p[5]
    kva = p[6, :KV_LORA]
    qa = p[6, KV_LORA:KV_LORA + Q_LORA].reshape(1, Q_LORA)
    bf = p[6, KV_LORA + Q_LORA:KV_LORA + Q_LORA + FOX_HEADS].reshape(1, FOX_HEADS)
    return mix, ffn, bf, kv, kva, qa, fin


def _mlp_fwd(tag, xin, g, w_up, w_down):
    h = _rms(f"{tag}_norm", xin, g, BF16)

    def act(acc):
        r = jnp.maximum(acc, 0.0)
        return acc, r * r

    u, a = _mm(f"{tag}_up", h, w_up, "nn", (BF16, BF16), epi=act)
    xout = _mm(f"{tag}_down", a, w_down, "nn", (F32,), epi=lambda acc, r: (acc + r,), extras=(xin,))
    return xout, (h, u, a)


def _mlp_bwd(tag, gout, xin, g, w_up, w_down, saved):
    h, u, a = saved
    dw_down = _mm_tn(f"{tag}_dwdown", a, gout)
    du = _mm(f"{tag}_du", gout, w_down, "nt", (BF16,),
             epi=lambda acc, uu: (acc * (2.0 * jnp.maximum(uu.astype(F32), 0.0)),), extras=(u,))
    dw_up = _mm_tn(f"{tag}_dwup", h, du)
    dh = _mm(f"{tag}_dh", du, w_up, "nt", (F32,))
    gin, dg = _rms_bwd(f"{tag}_norm_bwd", xin, g, dh, dres=gout)
    return gin, dg, dw_up, dw_down


def kernel(x, norm_mix_g, norm_ffn_g, fox_w_in, fox_b_f, fox_w_out, kv_norm_g, mla_w_kv_a, mla_kv_a_norm_g, mla_w_kv_b, mla_w_q_a, mla_q_a_norm_g, mla_w_q_b, mla_w_out, ffn_w_up, ffn_w_down, final_norm_g, loss_target, m_norm_mix_g, m_norm_ffn_g, m_fox_w_in, m_fox_b_f, m_fox_w_out, m_kv_norm_g, m_mla_w_kv_a, m_mla_kv_a_norm_g, m_mla_w_kv_b, m_mla_w_q_a, m_mla_q_a_norm_g, m_mla_w_q_b, m_mla_w_out, m_ffn_w_up, m_ffn_w_down, m_final_norm_g, v_norm_mix_g, v_norm_ffn_g, v_fox_w_in, v_fox_b_f, v_fox_w_out, v_kv_norm_g, v_mla_w_kv_a, v_mla_kv_a_norm_g, v_mla_w_kv_b, v_mla_w_q_a, v_mla_q_a_norm_g, v_mla_w_q_b, v_mla_w_out, v_ffn_w_up, v_ffn_w_down, v_final_norm_g):
    T = x.shape[1]
    D = D_MODEL
    tq = 512 if T >= 2048 else 128
    x0 = x[0]
    tgt = loss_target[0]

    gat_fox = _allgather_weights([fox_w_in[0].astype(BF16), fox_w_out[0].astype(BF16)])
    later_shards = [s.astype(BF16) for s in (mla_w_kv_a, mla_w_kv_b, mla_w_q_a[0], mla_w_q_b[0],
                                             mla_w_out[0], ffn_w_up, ffn_w_down)]
    w_in = gat_fox[0].transpose(1, 0, 2).reshape(D, 3 * D + FOX_HEADS)
    w_qkv = w_in[:, :3 * D]
    w_f = _pad_cols(w_in[:, 3 * D:], 128)
    w_fo = gat_fox[1].reshape(D, D)
    g_mix0, g_mix1 = norm_mix_g[0:1], norm_mix_g[1:2]
    g_ffn0, g_ffn1 = norm_ffn_g[0:1], norm_ffn_g[1:2]
    g_kv = kv_norm_g.reshape(1, D)
    g_kva = mla_kv_a_norm_g.reshape(1, KV_LORA)
    g_qa = mla_q_a_norm_g.reshape(1, Q_LORA)
    g_fin = final_norm_g.reshape(1, D)

    inv = 1.0 / (ROPE_BASE ** (jnp.arange(0, QK_ROPE, 2, dtype=F32) / QK_ROPE))
    ang = jnp.arange(T, dtype=F32)[:, None] * inv[None, :]
    cos, sin = jnp.cos(ang), jnp.sin(ang)
    cos2 = jnp.concatenate([cos, cos], axis=-1)
    sgn_sin = jnp.concatenate([-sin, sin], axis=-1)

    h0 = _rms("l0_mix_norm", x0, g_mix0, BF16)
    qkv = _mm("fox_qkv", h0, w_qkv, "nn", (BF16,))
    fl_pad = _mm("fox_gate_logit", h0, w_f, "nn", (F32,))
    fq = _to_heads(qkv[:, :D], FOX_HEADS)
    fk = _to_heads(qkv[:, D:2 * D], FOX_HEADS)
    fv = _to_heads(qkv[:, 2 * D:], FOX_HEADS)
    fl = fl_pad[:, :FOX_HEADS].T
    b_f = fox_b_f.reshape(FOX_HEADS, 1)
    cgate = _gate_cumsum("fox_gate_scan", fl, b_f, tq)
    fox_scale = FOX_HEAD_DIM ** -0.5
    fq_aug = _widen(fq * fox_scale, FOX_AUG, ones_at=FOX_HEAD_DIM)
    fk_aug = _widen(fk, FOX_AUG, ones_at=FOX_HEAD_DIM + 4, pieces_at=FOX_HEAD_DIM,
                    pieces=_split3((-cgate)[..., None]), const_at=FOX_HEAD_DIM + 3, const=1.0 / fox_scale)
    fv_aug = _widen(fv, FOX_AUG, ones_at=FOX_HEAD_DIM)
    (fo, flse), gat = _flash_fwd("fox_attn", fq_aug, fk_aug, fv_aug, FOX_HEAD_DIM, tq,
                                 exchange=("gather", later_shards))
    w_kva = _pad_cols(gat[0].reshape(D, KV_LORA + QK_ROPE), KV_A_PAD)
    w_kvb = gat[1].transpose(1, 0, 2).reshape(KV_LORA, MLA_HEADS * (QK_NOPE + V_HEAD))
    w_qa = gat[2].reshape(D, Q_LORA)
    w_qb = gat[3].transpose(1, 0, 2).reshape(Q_LORA, MLA_HEADS * (QK_NOPE + QK_ROPE))
    w_mo = gat[4].reshape(D, D)
    w_up = gat[5].transpose(1, 2, 0, 3).reshape(2, D, D_FF)
    w_down = gat[6].transpose(1, 0, 2, 3).reshape(2, D_FF, D)
    fctx = _from_heads(fo).astype(BF16)
    x1 = _mm("fox_out", fctx, w_fo, "nn", (F32,), epi=lambda acc, r: (acc + r,), extras=(x0,))
    x2, mlp0 = _mlp_fwd("l0_ffn", x1, g_ffn0, w_up[0], w_down[0])

    src = _rms("kv_norm", x2, g_kv, BF16)
    kva = _mm("kv_a", src, w_kva, "nn", (F32,))
    kva_lat = kva[:, :KV_LORA]
    c_kv = _rms("kv_a_norm", kva_lat, g_kva, BF16)
    k_rope = _rope("k_rope", kva[:, KV_LORA:KV_LORA + QK_ROPE][None], cos2, sgn_sin, BF16)
    kvb = _mm("kv_b", c_kv, w_kvb, "nn", (BF16,))
    kvb_h = _to_heads(kvb, MLA_HEADS)
    mk = jnp.concatenate([kvb_h[:, :, :QK_NOPE],
                          jnp.broadcast_to(k_rope, (MLA_HEADS, T, QK_ROPE))], axis=-1)
    mv = kvb_h[:, :, QK_NOPE:]

    h1 = _rms("l1_mix_norm", x2, g_mix1, BF16)
    qa = _mm("q_a", h1, w_qa, "nn", (F32,))
    c_q = _rms("q_a_norm", qa, g_qa, BF16)
    qf = _mm("q_b", c_q, w_qb, "nn", (F32,))
    qf_h = _to_heads(qf, MLA_HEADS)
    mla_scale = (QK_NOPE + QK_ROPE) ** -0.5
    mq = _mla_q_prep("q_prep", qf_h, cos2, sgn_sin, mla_scale)
    mv_aug = _widen(mv, MLA_AUG, ones_at=V_HEAD)
    (mo, mlse), _ = _flash_fwd("mla_attn", mq, mk, mv_aug, V_HEAD, tq)
    mctx = _from_heads(mo).astype(BF16)
    x3 = _mm("mla_out", mctx, w_mo, "nn", (F32,), epi=lambda acc, r: (acc + r,), extras=(x2,))
    x4, mlp1 = _mlp_fwd("l1_ffn", x3, g_ffn1, w_up[1], w_down[1])

    g4, dg_fin, loss_vec = _loss_head("loss_head", x4, g_fin, tgt)

    g3, dg_ffn1, dw_up1, dw_down1 = _mlp_bwd("l1_ffn", g4, x3, g_ffn1, w_up[1], w_down[1], mlp1)

    dw_mo = _mm_tn("mla_out_dw", mctx, g3)
    dmo = _to_heads(_mm("mla_out_dx", g3, w_mo, "nt", (BF16,)), MLA_HEADS)
    mdelta = _row_dot("mla_delta", mo, dmo)
    dqk = QK_NOPE + QK_ROPE
    mq_bwd = _widen(mq, MLA_AUG, pieces_at=dqk, pieces=_split3(-mlse))
    mk_bwd = _widen(mk, MLA_AUG, ones_at=dqk)
    mdo_aug = _widen(dmo, MLA_AUG, pieces_at=V_HEAD, pieces=_split3(-mdelta))
    (mdq, mdk, mdv), _ = _flash_bwd("mla_attn_bwd", mq_bwd, mk_bwd, mv_aug, mdo_aug, mla_scale, tq)
    mdq = mdq[:, :, :dqk]
    mdk = mdk[:, :, :dqk]
    mdv = mdv[:, :, :V_HEAD]
    dq_rope = _rope_bwd("q_rope_bwd", mdq[:, :, QK_NOPE:], cos2, sgn_sin, False)
    dqf = _from_heads(jnp.concatenate([mdq[:, :, :QK_NOPE], dq_rope], axis=-1))
    dw_qb = _mm_tn("q_b_dw", c_q, dqf)
    dc_q = _mm("q_b_dx", dqf, w_qb, "nt", (F32,))
    dqa, dg_qa = _rms_bwd("q_a_norm_bwd", qa, g_qa, dc_q)
    dw_qa = _mm_tn("q_a_dw", h1, dqa)
    dh1 = _mm("q_a_dx", dqa, w_qa, "nt", (F32,))
    g2a, dg_mix1 = _rms_bwd("l1_mix_norm_bwd", x2, g_mix1, dh1, dres=g3)

    dk_rope = _rope_bwd("k_rope_bwd", mdk[:, :, QK_NOPE:], cos2, sgn_sin, True)
    dkvb = _from_heads(jnp.concatenate([mdk[:, :, :QK_NOPE], mdv], axis=-1))
    dw_kvb = _mm_tn("kv_b_dw", c_kv, dkvb)
    dc_kv = _mm("kv_b_dx", dkvb, w_kvb, "nt", (F32,))
    dkva_lat, dg_kva = _rms_bwd("kv_a_norm_bwd", kva_lat, g_kva, dc_kv)
    dkva = _pad_cols(jnp.concatenate([dkva_lat, dk_rope], axis=-1), KV_A_PAD)
    dw_kva = _mm_tn("kv_a_dw", src, dkva)[:, :KV_LORA + QK_ROPE]
    dsrc = _mm("kv_a_dx", dkva, w_kva, "nt", (F32,))
    g2, dg_kv = _rms_bwd("kv_norm_bwd", x2, g_kv, dsrc, dres=g2a)

    g1, dg_ffn0, dw_up0, dw_down0 = _mlp_bwd("l0_ffn", g2, x1, g_ffn0, w_up[0], w_down[0], mlp0)

    dw_fo = _mm_tn("fox_out_dw", fctx, g1)
    dfo = _to_heads(_mm("fox_out_dx", g1, w_fo, "nt", (BF16,)), FOX_HEADS)
    fdelta = _row_dot("fox_delta", fo, dfo)
    fq_bwd = _widen(fq * fox_scale, FOX_AUG, ones_at=FOX_HEAD_DIM, pieces_at=FOX_HEAD_DIM + 4,
                    pieces=_split3(-flse))
    fdo_aug = _widen(dfo, FOX_AUG, pieces_at=FOX_HEAD_DIM, pieces=_split3(-fdelta))
    dw_up = jnp.stack([dw_up0, dw_up1])
    dw_down = jnp.stack([dw_down0, dw_down1])
    early = [
        dw_fo.reshape(N_DEV, D // N_DEV, D),
        dw_kva.reshape(N_DEV, D // N_DEV, KV_LORA + QK_ROPE),
        dw_kvb.reshape(KV_LORA, N_DEV, -1).transpose(1, 0, 2),
        dw_qa.reshape(N_DEV, D // N_DEV, Q_LORA),
        dw_qb.reshape(Q_LORA, N_DEV, -1).transpose(1, 0, 2),
        dw_mo.reshape(N_DEV, D // N_DEV, D),
        dw_up.reshape(2, D, N_DEV, -1).transpose(2, 0, 1, 3),
        dw_down.reshape(2, N_DEV, D_FF // N_DEV, D).transpose(1, 0, 2, 3),
    ]
    (fdq_aug, fdk_aug, fdv_aug), early_parts = _flash_bwd(
        "fox_attn_bwd", fq_bwd, fk_aug, fv_aug, fdo_aug, fox_scale, tq,
        exchange=("scatter", [g.astype(BF16) for g in early]))
    fdq = fdq_aug[:, :, :FOX_HEAD_DIM]
    fdk = fdk_aug[:, :, :FOX_HEAD_DIM]
    fdv = fdv_aug[:, :, :FOX_HEAD_DIM]
    dfl, db_f = _gate_cumsum_bwd("fox_gate_scan_bwd", fdq_aug[:, :, FOX_HEAD_DIM + 3],
                                 fdk_aug[:, :, FOX_HEAD_DIM], fl, b_f, tq)
    dqkv = jnp.concatenate([_from_heads(fdq), _from_heads(fdk), _from_heads(fdv)], axis=-1).astype(BF16)
    dfl_pad = _pad_cols(dfl.T, 128)
    dw_qkv = _mm_tn("fox_qkv_dw", h0, dqkv)
    dw_f = _mm_tn("fox_gate_dw", h0, dfl_pad)[:, :FOX_HEADS]
    dw_in = jnp.concatenate([dw_qkv, dw_f], axis=-1)
    dh0a = _mm("fox_gate_dx", dfl_pad, w_f, "nt", (F32,))
    dh0 = _mm("fox_qkv_dx", dqkv, w_qkv, "nt", (F32,), epi=lambda acc, r: (acc + r,), extras=(dh0a,))
    grad_x, dg_mix0 = _rms_bwd("l0_mix_norm_bwd", x0, g_mix0, dh0, dres=g1)

    late = dw_in.reshape(D, N_DEV, -1).transpose(1, 0, 2).astype(BF16)
    parts = list(_alltoall_grads("alltoall_fox_w_in", [late])) + list(early_parts)

    names = ["fox_w_in", "fox_w_out", "mla_w_kv_a", "mla_w_kv_b", "mla_w_q_a", "mla_w_q_b",
             "mla_w_out", "ffn_w_up", "ffn_w_down"]
    moms = [m_fox_w_in, m_fox_w_out, m_mla_w_kv_a, m_mla_w_kv_b, m_mla_w_q_a, m_mla_w_q_b,
            m_mla_w_out, m_ffn_w_up, m_ffn_w_down]
    vars_ = [v_fox_w_in, v_fox_w_out, v_mla_w_kv_a, v_mla_w_kv_b, v_mla_w_q_a, v_mla_w_q_b,
             v_mla_w_out, v_ffn_w_up, v_ffn_w_down]
    full = [fox_w_in, fox_w_out, mla_w_kv_a, mla_w_kv_b, mla_w_q_a, mla_w_q_b, mla_w_out,
            ffn_w_up, ffn_w_down]
    big = {}
    for nm, p, w, m, v in zip(names, parts, full, moms, vars_):
        C = w.shape[-1]
        res = _adamw(f"adamw_{nm}", p.reshape(N_DEV, -1, C), w.reshape(-1, C), m.reshape(-1, C),
                     v.reshape(-1, C))
        big[nm] = [r.reshape(w.shape) for r in res]

    zrow = jnp.zeros((D,), F32)
    g_small = _pack_small(jnp.concatenate([dg_mix0, dg_mix1]), jnp.concatenate([dg_ffn0, dg_ffn1]),
                          dg_kv, dg_fin, dg_kva, dg_qa, db_f, zrow.at[0].set(loss_vec[0, 0]))
    tot_small = _allreduce_small(g_small)
    w_small = _pack_small(norm_mix_g, norm_ffn_g, kv_norm_g, final_norm_g, mla_kv_a_norm_g,
                          mla_q_a_norm_g, fox_b_f, zrow)
    m_small = _pack_small(m_norm_mix_g, m_norm_ffn_g, m_kv_norm_g, m_final_norm_g, m_mla_kv_a_norm_g,
                          m_mla_q_a_norm_g, m_fox_b_f, zrow)
    v_small = _pack_small(v_norm_mix_g, v_norm_ffn_g, v_kv_norm_g, v_final_norm_g, v_mla_kv_a_norm_g,
                          v_mla_q_a_norm_g, v_fox_b_f, zrow)
    small = _adamw("adamw_small", tot_small[None], w_small, m_small, v_small)
    loss = tot_small[7, 0]
    small = [_unpack_small(s) for s in small]

    def ordered(i):
        mix, ffn, bf, kv, kva, qa, fin = small[i]
        return [mix, ffn, big["fox_w_in"][i], bf, big["fox_w_out"][i], kv, big["mla_w_kv_a"][i], kva,
                big["mla_w_kv_b"][i], big["mla_w_q_a"][i], qa, big["mla_w_q_b"][i],
                big["mla_w_out"][i], big["ffn_w_up"][i], big["ffn_w_down"][i], fin]

    return (loss, grad_x[None], *ordered(0), *ordered(1), *ordered(2), *ordered(3))
```

```python
import functools
import math

import jax
import jax.numpy as jnp
from jax import lax
from jax.experimental import pallas as pl
from jax.experimental.pallas import tpu as pltpu

F32 = jnp.float32
BF16 = jnp.bfloat16
MESH = pl.DeviceIdType.MESH

N_DEV = 8
D_MODEL = 1024
FOX_HEADS = 16
FOX_HEAD_DIM = 64
FOX_AUG = 128
MLA_AUG = 256
MLA_HEADS = 8
QK_NOPE = 128
QK_ROPE = 64
V_HEAD = 128
Q_LORA = 384
KV_LORA = 256
KV_A_PAD = 384
D_FF = 4096
ROPE_BASE = 10000.0
EPS = 1e-6
NEG = -1e30

ADAM_LR = 0.001
ADAM_B1 = 0.9
ADAM_B2 = 0.999
ADAM_EPS = 1e-08
ADAM_WD = 0.01
ADAM_STEP = 10

VMEM_LIMIT_BYTES = 56 * 1024 * 1024

NN = (((1,), (0,)), ((), ()))
NT = (((1,), (1,)), ((), ()))
TN = (((0,), (0,)), ((), ()))
_FORMS = {"nn": NN, "nt": NT}


def _cparams(sem=None):
    return pltpu.CompilerParams(dimension_semantics=sem, vmem_limit_bytes=VMEM_LIMIT_BYTES)


def _pick(n, cands):
    for c in cands:
        if c <= n and n % c == 0:
            return c
    return n


def _dot(a, b, dims):
    return lax.dot_general(a, b, dims, preferred_element_type=F32)


def _mm(name, a, b, form, out_dtypes, epi=None, extras=(), tm=1024, tn=512):
    M, K = a.shape
    N = b.shape[1] if form == "nn" else b.shape[0]
    tm = _pick(M, (tm, 512, 256, 128))
    tn = _pick(N, (tn, 384, 256, 128))
    n_ex = len(extras)

    def body(*refs):
        a_ref, b_ref = refs[0], refs[1]
        ex = refs[2:2 + n_ex]
        outs = refs[2 + n_ex:]
        acc = _dot(a_ref[...].astype(BF16), b_ref[...].astype(BF16), _FORMS[form])
        res = epi(acc, *[e[...] for e in ex]) if epi is not None else (acc,)
        for o_ref, r in zip(outs, res):
            o_ref[...] = r.astype(o_ref.dtype)

    if form == "nn":
        b_spec = pl.BlockSpec((K, tn), lambda i, j: (0, j))
    else:
        b_spec = pl.BlockSpec((tn, K), lambda i, j: (j, 0))
    tile = pl.BlockSpec((tm, tn), lambda i, j: (i, j))
    out = pl.pallas_call(
        body, name=name, grid=(M // tm, N // tn),
        in_specs=[pl.BlockSpec((tm, K), lambda i, j: (i, 0)), b_spec] + [tile] * n_ex,
        out_specs=[tile] * len(out_dtypes),
        out_shape=[jax.ShapeDtypeStruct((M, N), dt) for dt in out_dtypes],
        compiler_params=_cparams(("parallel", "arbitrary")),
    )(a, b, *extras)
    return out if len(out_dtypes) > 1 else out[0]


def _mm_tn(name, a, b):
    T, Ka = a.shape
    N = b.shape[1]
    tk = _pick(Ka, (1024, 512, 384, 256, 128))
    tn = _pick(N, (1024, 768, 512, 384, 256, 128))
    tt = _pick(T, (1024, 512, 256, 128))

    def body(a_ref, b_ref, o_ref):
        @pl.when(pl.program_id(2) == 0)
        def _():
            o_ref[...] = jnp.zeros_like(o_ref)

        o_ref[...] += _dot(a_ref[...].astype(BF16), b_ref[...].astype(BF16), TN)

    return pl.pallas_call(
        body, name=name, grid=(Ka // tk, N // tn, T // tt),
        in_specs=[pl.BlockSpec((tt, tk), lambda i, j, t: (t, i)),
                  pl.BlockSpec((tt, tn), lambda i, j, t: (t, j))],
        out_specs=pl.BlockSpec((tk, tn), lambda i, j, t: (i, j)),
        out_shape=jax.ShapeDtypeStruct((Ka, N), F32),
        compiler_params=_cparams(("parallel", "parallel", "arbitrary")),
    )(a, b)


def _mm_heads(name, a, w, out_dtype, heads_per_step=1, row_add=None):
    T, K = a.shape
    H, _, N = w.shape
    hb = heads_per_step
    tm = _pick(T, (1024, 512, 256, 128))
    has_row = row_add is not None

    def body(*refs):
        a_ref, w_ref = refs[0], refs[1]
        o_ref = refs[-1]
        av = a_ref[...].astype(BF16)
        for s in range(hb):
            acc = _dot(av, w_ref[s].astype(BF16), NN)
            if has_row:
                acc = acc + refs[2][s]
            o_ref[s] = acc.astype(o_ref.dtype)

    in_specs = [pl.BlockSpec((tm, K), lambda i, h: (i, 0)), pl.BlockSpec((hb, K, N), lambda i, h: (h, 0, 0))]
    if has_row:
        in_specs.append(pl.BlockSpec((hb, 1, N), lambda i, h: (h, 0, 0)))
    return pl.pallas_call(
        body, name=name, grid=(T // tm, H // hb), in_specs=in_specs,
        out_specs=pl.BlockSpec((hb, tm, N), lambda i, h: (h, i, 0)),
        out_shape=jax.ShapeDtypeStruct((H, T, N), out_dtype),
        compiler_params=_cparams(("parallel", "arbitrary")),
    )(a, w, *([row_add] if has_row else []))


def _mm_heads_dw(name, a, g):
    T, K = a.shape
    H, _, N = g.shape
    tt = _pick(T, (1024, 512, 256, 128))

    def body(a_ref, g_ref, o_ref):
        @pl.when(pl.program_id(1) == 0)
        def _():
            o_ref[...] = jnp.zeros_like(o_ref)

        o_ref[...] += _dot(a_ref[...].astype(BF16), g_ref[...].astype(BF16), TN)

    return pl.pallas_call(
        body, name=name, grid=(H, T // tt),
        in_specs=[pl.BlockSpec((tt, K), lambda h, t: (t, 0)), pl.BlockSpec((None, tt, N), lambda h, t: (h, t, 0))],
        out_specs=pl.BlockSpec((None, K, N), lambda h, t: (h, 0, 0)),
        out_shape=jax.ShapeDtypeStruct((H, K, N), F32),
        compiler_params=_cparams(("parallel", "arbitrary")),
    )(a, g)


def _mm_heads_dx(name, g, w):
    H, T, N = g.shape
    K = w.shape[1]
    tm = _pick(T, (512, 256, 128))

    def body(g_ref, w_ref, o_ref):
        acc = _dot(g_ref[0].astype(BF16), w_ref[0].astype(BF16), NT)
        for s in range(1, H):
            acc = acc + _dot(g_ref[s].astype(BF16), w_ref[s].astype(BF16), NT)
        o_ref[...] = acc

    return pl.pallas_call(
        body, name=name, grid=(T // tm,),
        in_specs=[pl.BlockSpec((H, tm, N), lambda i: (0, i, 0)), pl.BlockSpec((H, K, N), lambda i: (0, 0, 0))],
        out_specs=pl.BlockSpec((tm, K), lambda i: (i, 0)),
        out_shape=jax.ShapeDtypeStruct((T, K), F32),
        compiler_params=_cparams(("parallel",)),
    )(g, w)


def _rms(name, x, g, out_dtype):
    T, D = x.shape
    tm = _pick(T, (1024, 512, 256, 128))

    def body(x_ref, g_ref, o_ref):
        xf = x_ref[...]
        r = lax.rsqrt(jnp.mean(xf * xf, axis=-1, keepdims=True) + EPS)
        o_ref[...] = (xf * r * g_ref[...]).astype(o_ref.dtype)

    return pl.pallas_call(
        body, name=name, grid=(T // tm,),
        in_specs=[pl.BlockSpec((tm, D), lambda i: (i, 0)), pl.BlockSpec((1, D), lambda i: (0, 0))],
        out_specs=pl.BlockSpec((tm, D), lambda i: (i, 0)),
        out_shape=jax.ShapeDtypeStruct((T, D), out_dtype),
        compiler_params=_cparams(("parallel",)),
    )(x, g)


def _rms_bwd(name, x, g, dh, dres=None):
    T, D = x.shape
    tm = _pick(T, (512, 256, 128))
    has_res = dres is not None

    def body(*refs):
        if has_res:
            x_ref, g_ref, dh_ref, dres_ref, dx_ref, dg_ref = refs
        else:
            x_ref, g_ref, dh_ref, dx_ref, dg_ref = refs

        @pl.when(pl.program_id(0) == 0)
        def _():
            dg_ref[...] = jnp.zeros_like(dg_ref)

        xf = x_ref[...]
        r = lax.rsqrt(jnp.mean(xf * xf, axis=-1, keepdims=True) + EPS)
        xhat = xf * r
        dy = dh_ref[...].astype(F32)
        dxh = dy * g_ref[...]
        dx = r * (dxh - xhat * jnp.mean(dxh * xhat, axis=-1, keepdims=True))
        if has_res:
            dx = dx + dres_ref[...]
        dx_ref[...] = dx
        dg_ref[...] += jnp.sum(dy * xhat, axis=0, keepdims=True)

    row = pl.BlockSpec((tm, D), lambda i: (i, 0))
    vec = pl.BlockSpec((1, D), lambda i: (0, 0))
    ins = [x, g, dh] + ([dres] if has_res else [])
    return pl.pallas_call(
        body, name=name, grid=(T // tm,),
        in_specs=[row, vec, row] + ([row] if has_res else []),
        out_specs=[row, vec],
        out_shape=[jax.ShapeDtypeStruct((T, D), F32), jax.ShapeDtypeStruct((1, D), F32)],
        compiler_params=_cparams(("arbitrary",)),
    )(*ins)


def _loss_head(name, x, g, tgt):
    T, D = x.shape
    tm = _pick(T, (512, 256, 128))

    def body(x_ref, g_ref, t_ref, dx_ref, dg_ref, loss_ref):
        @pl.when(pl.program_id(0) == 0)
        def _():
            dg_ref[...] = jnp.zeros_like(dg_ref)
            loss_ref[...] = jnp.zeros_like(loss_ref)

        xf = x_ref[...]
        r = lax.rsqrt(jnp.mean(xf * xf, axis=-1, keepdims=True) + EPS)
        xhat = xf * r
        gv = g_ref[...]
        err = xhat * gv - t_ref[...]
        row_loss = jnp.mean(err * err, axis=-1, keepdims=True)
        loss_ref[...] += 0.5 * jnp.sum(row_loss, axis=0, keepdims=True)
        dy = err * (1.0 / D)
        dxh = dy * gv
        dx_ref[...] = r * (dxh - xhat * jnp.mean(dxh * xhat, axis=-1, keepdims=True))
        dg_ref[...] += jnp.sum(dy * xhat, axis=0, keepdims=True)

    row = pl.BlockSpec((tm, D), lambda i: (i, 0))
    vec = pl.BlockSpec((1, D), lambda i: (0, 0))
    return pl.pallas_call(
        body, name=name, grid=(T // tm,),
        in_specs=[row, vec, row],
        out_specs=[row, vec, pl.BlockSpec((1, 128), lambda i: (0, 0))],
        out_shape=[jax.ShapeDtypeStruct((T, D), F32), jax.ShapeDtypeStruct((1, D), F32),
                   jax.ShapeDtypeStruct((1, 128), F32)],
        compiler_params=_cparams(("arbitrary",)),
    )(x, g, tgt)


def _swap_halves(t):
    half = t.shape[-1] // 2
    return jnp.concatenate([t[:, half:], t[:, :half]], axis=-1)


def _rope(name, t, cos2, sgn_sin, out_dtype):
    H, T, R = t.shape
    tm = _pick(T, (1024, 512, 256, 128))

    def body(t_ref, c_ref, s_ref, o_ref):
        tf = t_ref[...].astype(F32)
        o_ref[...] = (tf * c_ref[...] + _swap_halves(tf) * s_ref[...]).astype(o_ref.dtype)

    slab = pl.BlockSpec((None, tm, R), lambda h, i: (h, i, 0))
    tab = pl.BlockSpec((tm, R), lambda h, i: (i, 0))
    return pl.pallas_call(
        body, name=name, grid=(H, T // tm),
        in_specs=[slab, tab, tab], out_specs=slab,
        out_shape=jax.ShapeDtypeStruct((H, T, R), out_dtype),
        compiler_params=_cparams(("parallel", "parallel")),
    )(t, cos2, sgn_sin)


def _mla_q_prep(name, qf, cos2, sgn_sin, scale):
    H, T, W = qf.shape
    R = cos2.shape[1]
    tm = _pick(T, (1024, 512, 256, 128))

    def body(t_ref, c_ref, s_ref, o_ref):
        tf = t_ref[...]
        r = tf[:, W - R:]
        roped = r * c_ref[...] + _swap_halves(r) * s_ref[...]
        o_ref[...] = (jnp.concatenate([tf[:, :W - R], roped], axis=-1) * scale).astype(o_ref.dtype)

    slab = pl.BlockSpec((None, tm, W), lambda h, i: (h, i, 0))
    tab = pl.BlockSpec((tm, R), lambda h, i: (i, 0))
    return pl.pallas_call(
        body, name=name, grid=(H, T // tm),
        in_specs=[slab, tab, tab], out_specs=slab,
        out_shape=jax.ShapeDtypeStruct((H, T, W), BF16),
        compiler_params=_cparams(("parallel", "parallel")),
    )(qf, cos2, sgn_sin)


def _rope_bwd(name, dy, cos2, sgn_sin, sum_heads):
    H, T, R = dy.shape
    tm = _pick(T, (1024, 512, 256, 128))

    def body(d_ref, c_ref, s_ref, o_ref):
        d = d_ref[...]
        if sum_heads:
            tot = d[0]
            for h in range(1, H):
                tot = tot + d[h]
            d = tot
        o_ref[...] = d * c_ref[...] + _swap_halves(d * s_ref[...])

    if sum_heads:
        grid = (T // tm,)
        in_slab = pl.BlockSpec((H, tm, R), lambda i: (0, i, 0))
        out_slab = pl.BlockSpec((tm, R), lambda i: (i, 0))
        tab = pl.BlockSpec((tm, R), lambda i: (i, 0))
        out_shape = jax.ShapeDtypeStruct((T, R), F32)
        sem = ("parallel",)
    else:
        grid = (H, T // tm)
        in_slab = pl.BlockSpec((None, tm, R), lambda h, i: (h, i, 0))
        out_slab = in_slab
        tab = pl.BlockSpec((tm, R), lambda h, i: (i, 0))
        out_shape = jax.ShapeDtypeStruct((H, T, R), F32)
        sem = ("parallel", "parallel")
    return pl.pallas_call(
        body, name=name, grid=grid, in_specs=[in_slab, tab, tab], out_specs=out_slab,
        out_shape=out_shape, compiler_params=_cparams(sem),
    )(dy, cos2, sgn_sin)


def _log_sigmoid(z):
    return jnp.minimum(z, 0.0) - jnp.log(1.0 + jnp.exp(-jnp.abs(z)))


def _gate_cumsum(name, fl, b, tb):
    H, T = fl.shape

    def body(f_ref, b_ref, c_ref, carry):
        @pl.when(pl.program_id(0) == 0)
        def _():
            carry[...] = jnp.zeros_like(carry)

        ls = _log_sigmoid(f_ref[...] + b_ref[...])
        src = lax.broadcasted_iota(jnp.int32, (tb, tb), 0)
        dst = lax.broadcasted_iota(jnp.int32, (tb, tb), 1)
        tri = (src <= dst).astype(F32)
        c = lax.dot_general(ls, tri, NN, precision=lax.Precision.HIGHEST,
                            preferred_element_type=F32) + carry[...]
        c_ref[...] = c
        carry[...] = carry[...] + jnp.sum(ls, axis=-1, keepdims=True)

    return pl.pallas_call(
        body, name=name, grid=(T // tb,),
        in_specs=[pl.BlockSpec((H, tb), lambda i: (0, i)), pl.BlockSpec((H, 1), lambda i: (0, 0))],
        out_specs=pl.BlockSpec((H, tb), lambda i: (0, i)),
        out_shape=jax.ShapeDtypeStruct((H, T), F32),
        scratch_shapes=[pltpu.VMEM((H, 1), F32)],
        compiler_params=_cparams(("arbitrary",)),
    )(fl, b)


def _gate_cumsum_bwd(name, d_query, d_key, fl, b, tb):
    H, T = fl.shape
    nb = T // tb

    def body(dq_ref, dk_ref, f_ref, b_ref, dfl_ref, db_ref, carry):
        @pl.when(pl.program_id(0) == 0)
        def _():
            carry[...] = jnp.zeros_like(carry)
            db_ref[...] = jnp.zeros_like(db_ref)

        d = dq_ref[...] - dk_ref[...]
        src = lax.broadcasted_iota(jnp.int32, (tb, tb), 0)
        dst = lax.broadcasted_iota(jnp.int32, (tb, tb), 1)
        tri = (src >= dst).astype(F32)
        dls = lax.dot_general(d, tri, NN, precision=lax.Precision.HIGHEST,
                              preferred_element_type=F32) + carry[...]
        z = f_ref[...] + b_ref[...]
        dfl = dls * (1.0 / (1.0 + jnp.exp(z)))
        dfl_ref[...] = dfl
        db_ref[...] += jnp.sum(dfl, axis=-1, keepdims=True)
        carry[...] = carry[...] + jnp.sum(d, axis=-1, keepdims=True)

    blk = pl.BlockSpec((H, tb), lambda i: (0, nb - 1 - i))
    vec = pl.BlockSpec((H, 1), lambda i: (0, 0))
    return pl.pallas_call(
        body, name=name, grid=(nb,),
        in_specs=[blk, blk, blk, vec], out_specs=[blk, vec],
        out_shape=[jax.ShapeDtypeStruct((H, T), F32), jax.ShapeDtypeStruct((H, 1), F32)],
        scratch_shapes=[pltpu.VMEM((H, 1), F32)],
        compiler_params=_cparams(("arbitrary",)),
    )(d_query, d_key, fl, b)


def _causal_mask(tq, rows_are_queries):
    r = lax.broadcasted_iota(jnp.int32, (tq, tq), 0)
    c = lax.broadcasted_iota(jnp.int32, (tq, tq), 1)
    return (c <= r) if rows_are_queries else (r <= c)


def _chunk_rows(j, tq):
    return pl.ds(pl.multiple_of(j * tq, tq), tq)


def _flash_fwd(name, q, k, v_aug, dv, tq, exchange=None, q_head0=0, v_head0=0):
    H, T, dqk = k.shape
    dva = v_aug.shape[2]
    nq = T // tq

    def body(q_ref, k_ref, v_ref, o_ref, lse_ref, m_sc, acc_sc):
        qi = pl.program_id(1)
        m_sc[...] = jnp.full_like(m_sc, NEG)
        acc_sc[...] = jnp.zeros_like(acc_sc)

        def chunk(j, masked):
            rows = _chunk_rows(j, tq)
            s = _dot(q_ref[...], k_ref[rows, :], NT)
            if masked:
                s = jnp.where(_causal_mask(tq, True), s, NEG)
            m_prev = m_sc[...]
            m_new = jnp.maximum(m_prev, jnp.max(s, axis=1, keepdims=True))
            p = jnp.exp(s - jnp.tile(m_new, (1, tq // 128)))
            alpha = jnp.tile(jnp.exp(m_prev - m_new), (1, dva // 128))
            acc_sc[...] = alpha * acc_sc[...] + _dot(p.astype(BF16), v_ref[rows, :], NN)
            m_sc[...] = m_new

        def off_diagonal(j, carry):
            chunk(j, False)
            return carry

        lax.fori_loop(0, qi, off_diagonal, 0)
        chunk(qi, True)
        acc = acc_sc[...]
        l = acc[:, dv:dv + 1]
        o_ref[...] = acc[:, :dv] / l
        lse_ref[...] = m_sc[:, :1] + jnp.log(l)

    return _call_carrying(
        body, name, (H, nq),
        in_specs=[pl.BlockSpec((None, tq, dqk), lambda h, i: (h + q_head0, i, 0)),
                  pl.BlockSpec((None, T, dqk), lambda h, i: (h, 0, 0)),
                  pl.BlockSpec((None, T, dva), lambda h, i: (h + v_head0, 0, 0))],
        out_specs=[pl.BlockSpec((None, tq, dv), lambda h, i: (h, i, 0)),
                   pl.BlockSpec((None, tq, 1), lambda h, i: (h, i, 0))],
        out_shape=[jax.ShapeDtypeStruct((H, T, dv), F32), jax.ShapeDtypeStruct((H, T, 1), F32)],
        scratch_shapes=[pltpu.VMEM((tq, 128), F32), pltpu.VMEM((tq, dva), F32)],
        operands=(q, k, v_aug), exchange=exchange)


def _row_dot(name, a, b):
    H, T, d = a.shape
    tm = _pick(T, (1024, 512, 256, 128))

    def body(a_ref, b_ref, o_ref):
        o_ref[...] = jnp.sum(a_ref[...].astype(F32) * b_ref[...].astype(F32), axis=-1, keepdims=True)

    slab = pl.BlockSpec((None, tm, d), lambda h, i: (h, i, 0))
    return pl.pallas_call(
        body, name=name, grid=(H, T // tm), in_specs=[slab, slab],
        out_specs=pl.BlockSpec((None, tm, 1), lambda h, i: (h, i, 0)),
        out_shape=jax.ShapeDtypeStruct((H, T, 1), F32),
        compiler_params=_cparams(("parallel", "parallel")),
    )(a, b)


def _flash_bwd(name, q, k, v, do, scale, tq, exchange=None, v_head0=0):
    H, T, dqk = q.shape
    dva = v.shape[2]
    nq = T // tq

    def body(q_ref, k_ref, v_ref, do_ref, dq_ref, dk_ref, dv_ref, dk_sc, dv_sc):
        ki = pl.program_id(1)
        dk_sc[...] = jnp.zeros_like(dk_sc)
        dv_sc[...] = jnp.zeros_like(dv_sc)

        @pl.when(ki == 0)
        def _():
            dq_ref[...] = jnp.zeros_like(dq_ref)

        def chunk(i, masked):
            rows = _chunk_rows(i, tq)
            qb = q_ref[rows, :]
            dob = do_ref[rows, :]
            kb = k_ref[...]
            st = _dot(kb, qb, NT)
            if masked:
                st = jnp.where(_causal_mask(tq, False), st, NEG)
            pt = jnp.exp(st)
            dv_sc[...] += _dot(pt.astype(BF16), dob, NN)
            dst = (pt * _dot(v_ref[...], dob, NT)).astype(BF16)
            dk_sc[...] += _dot(dst, qb, NN)
            dq_ref[rows, :] += _dot(dst, kb, TN)

        def off_diagonal(i, carry):
            chunk(i, False)
            return carry

        chunk(ki, True)
        lax.fori_loop(ki + 1, nq, off_diagonal, 0)
        dk_ref[...] = dk_sc[...]
        dv_ref[...] = dv_sc[...]

        @pl.when(ki == nq - 1)
        def _():
            dq_ref[...] = dq_ref[...] * scale

    whole_q = pl.BlockSpec((None, T, dqk), lambda h, j: (h, 0, 0))
    k_spec = pl.BlockSpec((None, tq, dqk), lambda h, j: (h, j, 0))
    v_spec = pl.BlockSpec((None, tq, dva), lambda h, j: (h, j, 0))
    v_in_spec = pl.BlockSpec((None, tq, dva), lambda h, j: (h + v_head0, j, 0))
    return _call_carrying(
        body, name, (H, nq),
        in_specs=[whole_q, k_spec, v_in_spec, pl.BlockSpec((None, T, dva), lambda h, j: (h, 0, 0))],
        out_specs=[whole_q, k_spec, v_spec],
        out_shape=[jax.ShapeDtypeStruct((H, T, dqk), F32), jax.ShapeDtypeStruct((H, T, dqk), F32),
                   jax.ShapeDtypeStruct((H, T, dva), F32)],
        scratch_shapes=[pltpu.VMEM((tq, dqk), F32), pltpu.VMEM((tq, dva), F32)],
        operands=(q, k, v, do), exchange=exchange)


def _adamw_math(w, g, m, v):
    m = ADAM_B1 * m + (1.0 - ADAM_B1) * g
    v = ADAM_B2 * v + (1.0 - ADAM_B2) * (g * g)
    m_hat = m / (1.0 - ADAM_B1 ** ADAM_STEP)
    v_hat = v / (1.0 - ADAM_B2 ** ADAM_STEP)
    delta = -ADAM_LR * (m_hat / (jnp.sqrt(v_hat) + ADAM_EPS) + ADAM_WD * w)
    return delta, m, v


def _adamw(name, parts, w, m, v):
    P, R, C = parts.shape
    tr = _pick(R, (256, 128, 64, 32, 16, 8))

    def body(p_ref, w_ref, m_ref, v_ref, g_out, d_out, m_out, v_out):
        g = p_ref[0].astype(F32)
        for i in range(1, P):
            g = g + p_ref[i].astype(F32)
        delta, m_new, v_new = _adamw_math(w_ref[...], g, m_ref[...], v_ref[...])
        g_out[...] = g
        d_out[...] = delta
        m_out[...] = m_new
        v_out[...] = v_new

    blk = pl.BlockSpec((tr, C), lambda i: (i, 0))
    sds = jax.ShapeDtypeStruct((R, C), F32)
    return pl.pallas_call(
        body, name=name, grid=(R // tr,),
        in_specs=[pl.BlockSpec((P, tr, C), lambda i: (0, i, 0)), blk, blk, blk],
        out_specs=[blk] * 4, out_shape=[sds] * 4,
        compiler_params=_cparams(("parallel",)),
    )(parts, w, m, v)


def _my_position():
    return lax.axis_index("x"), lax.axis_index("y"), lax.axis_index("c")


def _slot(p):
    return 4 * p[0] + 2 * p[1] + p[2]


def _flip(p, k):
    return tuple((1 - p[i]) if (k >> (2 - i)) & 1 else p[i] for i in range(3))


def _allgather_weights(shards):
    n = len(shards)

    def body(*refs):
        ins = refs[:n]
        outs = refs[n:2 * n]
        send_sems, recv_sems, local_sems = refs[2 * n:]
        x, y, c = _my_position()
        me, sibling = (x, y, c), (x, y, 1 - c)
        chips = [(1 - x, y), (x, 1 - y), (1 - x, 1 - y)]

        def copy(a, k, block, to, src=None):
            dst = outs[a].at[_slot(block)]
            return pltpu.make_async_remote_copy(
                src_ref=dst if src is None else src, dst_ref=dst,
                send_sem=send_sems.at[7 * a + k], recv_sem=recv_sems.at[7 * a + k],
                device_id=to, device_id_type=MESH)

        started = []
        for a in range(n):
            mine = pltpu.make_async_copy(ins[a], outs[a].at[_slot(me)], local_sems.at[a])
            mine.start()
            started.append(mine)
        first = []
        for a in range(n):
            first.append(copy(a, 0, me, sibling, src=ins[a]))
            first += [copy(a, 1 + j, me, (*chip, c), src=ins[a]) for j, chip in enumerate(chips)]
        for cp in first:
            cp.start()
        passed = []
        for j, chip in enumerate(chips):
            for a in range(n):
                copy(a, 1 + j, (*chip, c), me).wait_recv()
                fwd = copy(a, 4 + j, (*chip, c), sibling)
                fwd.start()
                passed.append(fwd)
        for a in range(n):
            copy(a, 0, sibling, me).wait_recv()
            for j, chip in enumerate(chips):
                copy(a, 4 + j, (*chip, 1 - c), me).wait_recv()
        for cp in first + passed:
            cp.wait_send()
        for mine in started:
            mine.wait()

    hbm = pl.BlockSpec(memory_space=pl.ANY)
    return pl.pallas_call(
        body, name="allgather_weights",
        in_specs=[hbm] * n, out_specs=[hbm] * n,
        out_shape=[jax.ShapeDtypeStruct((N_DEV,) + s.shape, s.dtype) for s in shards],
        scratch_shapes=[pltpu.SemaphoreType.DMA((7 * n,)), pltpu.SemaphoreType.DMA((7 * n,)),
                        pltpu.SemaphoreType.DMA((n,))],
        compiler_params=pltpu.CompilerParams(has_side_effects=True),
    )(*shards)


def _exchange_copies(kind, x_in, x_out, send_sems, recv_sems, local_sems, receives=True):
    me = _my_position()
    mine = _slot(me)
    local, sends, recvs = [], [], []
    for a in range(len(x_in)):
        src = x_in[a] if kind == "gather" else x_in[a].at[mine]
        local.append(pltpu.make_async_copy(src, x_out[a].at[mine], local_sems.at[a]))
    for k in range(1, N_DEV):
        peer = _flip(me, k)
        theirs = _slot(peer)
        for a in range(len(x_in)):
            src = x_in[a] if kind == "gather" else x_in[a].at[theirs]
            ends = [(x_out[a].at[mine], sends)] + ([(x_out[a].at[theirs], recvs)] if receives else [])
            for dst, group in ends:
                group.append(pltpu.make_async_remote_copy(
                    src_ref=src, dst_ref=dst, send_sem=send_sems.at[7 * a + k - 1],
                    recv_sem=recv_sems.at[7 * a + k - 1], device_id=peer, device_id_type=MESH))
    return local, sends, recvs


def _exchange_out_shapes(kind, arrays):
    return [jax.ShapeDtypeStruct(((N_DEV,) + a.shape) if kind == "gather" else a.shape, a.dtype)
            for a in arrays]


def _exchange_sems(n):
    return [pltpu.SemaphoreType.DMA((7 * n,)), pltpu.SemaphoreType.DMA((7 * n,)),
            pltpu.SemaphoreType.DMA((n,))]


def _alltoall_grads(name, grads):
    n = len(grads)

    def body(*refs):
        local, sends, recvs = _exchange_copies("scatter", refs[:n], refs[n:2 * n], *refs[2 * n:])
        for cp in local + sends:
            cp.start()
        for cp in recvs:
            cp.wait_recv()
        for cp in sends:
            cp.wait_send()
        for cp in local:
            cp.wait()

    hbm = pl.BlockSpec(memory_space=pl.ANY)
    return pl.pallas_call(
        body, name=name,
        in_specs=[hbm] * n, out_specs=[hbm] * n,
        out_shape=_exchange_out_shapes("scatter", grads), scratch_shapes=_exchange_sems(n),
        compiler_params=pltpu.CompilerParams(has_side_effects=True),
    )(*grads)


def _call_carrying(body, name, grid, in_specs, out_specs, out_shape, scratch_shapes, operands, exchange):
    if exchange is None:
        out = pl.pallas_call(
            body, name=name, grid=grid, in_specs=in_specs, out_specs=out_specs, out_shape=out_shape,
            scratch_shapes=scratch_shapes, compiler_params=_cparams(("parallel",) + ("arbitrary",) * (len(grid) - 1)),
        )(*operands)
        return out, None
    kind, arrays = exchange
    n, n_in, n_out, n_sc = len(arrays), len(in_specs), len(out_specs), len(scratch_shapes)

    def full_body(*refs):
        ins, refs = refs[:n_in], refs[n_in:]
        x_in, refs = refs[:n], refs[n:]
        outs, refs = refs[:n_out], refs[n_out:]
        x_out, refs = refs[:n], refs[n:]
        scratch, sems = refs[:n_sc], refs[n_sc:]
        first = last = None
        for axis, size in enumerate(grid):
            at_start = pl.program_id(axis) == 0
            at_end = pl.program_id(axis) == size - 1
            first = at_start if first is None else jnp.logical_and(first, at_start)
            last = at_end if last is None else jnp.logical_and(last, at_end)

        @pl.when(first)
        def _():
            local, sends, _ = _exchange_copies(kind, x_in, x_out, *sems, receives=False)
            for cp in local + sends:
                cp.start()

        body(*ins, *outs, *scratch)

        @pl.when(last)
        def _():
            local, sends, recvs = _exchange_copies(kind, x_in, x_out, *sems)
            for cp in recvs:
                cp.wait_recv()
            for cp in sends:
                cp.wait_send()
            for cp in local:
                cp.wait()

    hbm = pl.BlockSpec(memory_space=pl.ANY)
    out = pl.pallas_call(
        full_body, name=name, grid=grid,
        in_specs=list(in_specs) + [hbm] * n, out_specs=list(out_specs) + [hbm] * n,
        out_shape=list(out_shape) + _exchange_out_shapes(kind, arrays),
        scratch_shapes=list(scratch_shapes) + _exchange_sems(n),
        compiler_params=pltpu.CompilerParams(dimension_semantics=("arbitrary",) * len(grid),
                                             vmem_limit_bytes=VMEM_LIMIT_BYTES, has_side_effects=True),
    )(*operands, *arrays)
    return out[:n_out], out[n_out:]


def _allreduce_small(v):
    R, C = v.shape

    def body(v_ref, o_ref, buf, send_sems, recv_sems):
        me = _my_position()
        buf[_slot(me)] = v_ref[...]
        sends = []
        for k in range(1, N_DEV):
            peer = _flip(me, k)
            cp = pltpu.make_async_remote_copy(
                src_ref=v_ref, dst_ref=buf.at[_slot(me)],
                send_sem=send_sems.at[k - 1], recv_sem=recv_sems.at[k - 1],
                device_id=peer, device_id_type=MESH)
            cp.start()
            sends.append(cp)
        for k in range(1, N_DEV):
            peer = _flip(me, k)
            pltpu.make_async_remote_copy(
                src_ref=v_ref, dst_ref=buf.at[_slot(peer)],
                send_sem=send_sems.at[k - 1], recv_sem=recv_sems.at[k - 1],
                device_id=peer, device_id_type=MESH).wait_recv()
        for cp in sends:
            cp.wait_send()
        tot = buf[0]
        for s in range(1, N_DEV):
            tot = tot + buf[s]
        o_ref[...] = tot

    vm = pl.BlockSpec(memory_space=pltpu.VMEM)
    return pl.pallas_call(
        body, name="allreduce_small",
        in_specs=[vm], out_specs=vm, out_shape=jax.ShapeDtypeStruct((R, C), F32),
        scratch_shapes=[pltpu.VMEM((N_DEV, R, C), F32), pltpu.SemaphoreType.DMA((7,)),
                        pltpu.SemaphoreType.DMA((7,))],
        compiler_params=pltpu.CompilerParams(has_side_effects=True),
    )(v)


def _to_heads(t, heads):
    T = t.shape[0]
    return t.reshape(T, heads, t.shape[1] // heads).transpose(1, 0, 2)


def _from_heads(t):
    H, T, d = t.shape
    return t.transpose(1, 0, 2).reshape(T, H * d)


def _widen(t, width, ones_at=None, pieces_at=None, pieces=None, const_at=None, const=None):
    out = jnp.pad(t, ((0, 0), (0, 0), (0, width - t.shape[-1])))
    lane = lax.broadcasted_iota(jnp.int32, (1, 1, width), 2)
    if ones_at is not None:
        out = jnp.where((lane >= ones_at) & (lane < ones_at + 3), jnp.ones((), BF16), out)
    if pieces_at is not None:
        for i in range(3):
            out = jnp.where(lane == pieces_at + i, pieces[..., i:i + 1], out)
    if const_at is not None:
        out = jnp.where(lane == const_at, jnp.asarray(const, BF16), out)
    return out


def _split3(t):
    hi = lax.reduce_precision(t, 8, 7)
    r = t - hi
    mid = lax.reduce_precision(r, 8, 7)
    lo = lax.reduce_precision(r - mid, 8, 7)
    return jnp.concatenate([hi, mid, lo], axis=-1).astype(BF16)


def _pad_cols(t, n):
    return jnp.pad(t, ((0, 0), (0, n - t.shape[1])))


def _pack_small(mix, ffn, kv, fin, kva, qa, bf, last):
    row6 = jnp.concatenate([kva.reshape(-1), qa.reshape(-1), bf.reshape(-1),
                            jnp.zeros((D_MODEL - KV_LORA - Q_LORA - FOX_HEADS,), F32)])
    return jnp.stack([mix[0], mix[1], ffn[0], ffn[1], kv.reshape(-1), fin.reshape(-1), row6, last])


def _unpack_small(p):
    mix = p[0:2]
    ffn = p[2:4]
    kv = p[4]
    fin = p[5]
    kva = p[6, :KV_LORA]
    qa = p[6, KV_LORA:KV_LORA + Q_LORA].reshape(1, Q_LORA)
    bf = p[6, KV_LORA + Q_LORA:KV_LORA + Q_LORA + FOX_HEADS].reshape(1, FOX_HEADS)
    return mix, ffn, bf, kv, kva, qa, fin


def _mlp_fwd(tag, xin, g, w_up, w_down):
    h = _rms(f"{tag}_norm", xin, g, BF16)

    def act(acc):
        r = jnp.maximum(acc, 0.0)
        return acc, r * r

    u, a = _mm(f"{tag}_up", h, w_up, "nn", (BF16, BF16), epi=act)
    xout = _mm(f"{tag}_down", a, w_down, "nn", (F32,), epi=lambda acc, r: (acc + r,), extras=(xin,))
    return xout, (h, u, a)


def _mlp_bwd(tag, gout, xin, g, w_up, w_down, saved):
    h, u, a = saved
    dw_down = _mm_tn(f"{tag}_dwdown", a, gout)
    du = _mm(f"{tag}_du", gout, w_down, "nt", (BF16,),
             epi=lambda acc, uu: (acc * (2.0 * jnp.maximum(uu.astype(F32), 0.0)),), extras=(u,))
    dw_up = _mm_tn(f"{tag}_dwup", h, du)
    dh = _mm(f"{tag}_dh", du, w_up, "nt", (F32,))
    gin, dg = _rms_bwd(f"{tag}_norm_bwd", xin, g, dh, dres=gout)
    return gin, dg, dw_up, dw_down


def kernel(x, norm_mix_g, norm_ffn_g, fox_w_in, fox_b_f, fox_w_out, kv_norm_g, mla_w_kv_a, mla_kv_a_norm_g, mla_w_kv_b, mla_w_q_a, mla_q_a_norm_g, mla_w_q_b, mla_w_out, ffn_w_up, ffn_w_down, final_norm_g, loss_target, m_norm_mix_g, m_norm_ffn_g, m_fox_w_in, m_fox_b_f, m_fox_w_out, m_kv_norm_g, m_mla_w_kv_a, m_mla_kv_a_norm_g, m_mla_w_kv_b, m_mla_w_q_a, m_mla_q_a_norm_g, m_mla_w_q_b, m_mla_w_out, m_ffn_w_up, m_ffn_w_down, m_final_norm_g, v_norm_mix_g, v_norm_ffn_g, v_fox_w_in, v_fox_b_f, v_fox_w_out, v_kv_norm_g, v_mla_w_kv_a, v_mla_kv_a_norm_g, v_mla_w_kv_b, v_mla_w_q_a, v_mla_q_a_norm_g, v_mla_w_q_b, v_mla_w_out, v_ffn_w_up, v_ffn_w_down, v_final_norm_g):
    T = x.shape[1]
    D = D_MODEL
    tq = 512 if T >= 2048 else 128
    x0 = x[0]
    tgt = loss_target[0]

    gat_fox = _allgather_weights([fox_w_in[0].astype(BF16), fox_w_out[0].astype(BF16)])
    later_shards = [s.astype(BF16) for s in (mla_w_kv_a, mla_w_kv_b, mla_w_q_a[0], mla_w_q_b[0],
                                             mla_w_out[0], ffn_w_up, ffn_w_down)]
    w_in = gat_fox[0].transpose(1, 0, 2).reshape(D, 3 * D + FOX_HEADS)
    w_qkv = w_in[:, :3 * D]
    w_f = _pad_cols(w_in[:, 3 * D:], 128)
    w_fo = gat_fox[1].reshape(D, D)
    g_mix0, g_mix1 = norm_mix_g[0:1], norm_mix_g[1:2]
    g_ffn0, g_ffn1 = norm_ffn_g[0:1], norm_ffn_g[1:2]
    g_kv = kv_norm_g.reshape(1, D)
    g_kva = mla_kv_a_norm_g.reshape(1, KV_LORA)
    g_qa = mla_q_a_norm_g.reshape(1, Q_LORA)
    g_fin = final_norm_g.reshape(1, D)

    inv = 1.0 / (ROPE_BASE ** (jnp.arange(0, QK_ROPE, 2, dtype=F32) / QK_ROPE))
    ang = jnp.arange(T, dtype=F32)[:, None] * inv[None, :]
    cos, sin = jnp.cos(ang), jnp.sin(ang)
    cos2 = jnp.concatenate([cos, cos], axis=-1)
    sgn_sin = jnp.concatenate([-sin, sin], axis=-1)

    h0 = _rms("l0_mix_norm", x0, g_mix0, BF16)
    fl_pad = _mm("fox_gate_logit", h0, w_f, "nn", (F32,))
    fl = fl_pad[:, :FOX_HEADS].T
    b_f = fox_b_f.reshape(FOX_HEADS, 1)
    cgate = _gate_cumsum("fox_gate_scan", fl, b_f, tq)
    fox_scale = FOX_HEAD_DIM ** -0.5
    w_heads = w_qkv.reshape(D, 3, FOX_HEADS, FOX_HEAD_DIM).transpose(1, 2, 0, 3)
    w_heads = w_heads * jnp.asarray([fox_scale, 1.0, 1.0], BF16).reshape(3, 1, 1, 1)
    w_heads = jnp.pad(w_heads.reshape(3 * FOX_HEADS, D, FOX_HEAD_DIM),
                      ((0, 0), (0, 0), (0, FOX_AUG - FOX_HEAD_DIM)))
    lane = jnp.arange(FOX_AUG)
    ones_q = ((lane >= FOX_HEAD_DIM) & (lane < FOX_HEAD_DIM + 3)).astype(F32)
    consts_k = (((lane >= FOX_HEAD_DIM + 4) & (lane < FOX_HEAD_DIM + 7)).astype(F32)
                + (lane == FOX_HEAD_DIM + 3).astype(F32) * (1.0 / fox_scale))
    const_rows = jnp.broadcast_to(jnp.stack([ones_q, consts_k, ones_q])[:, None, None, :],
                                  (3, FOX_HEADS, 1, FOX_AUG)).reshape(3 * FOX_HEADS, 1, FOX_AUG)
    qkv_h = _mm_heads("fox_qkv", h0, w_heads, BF16, heads_per_step=8, row_add=const_rows)
    fk_aug = _widen(qkv_h[FOX_HEADS:2 * FOX_HEADS], FOX_AUG, pieces_at=FOX_HEAD_DIM,
                    pieces=_split3((-cgate)[..., None]))
    (fo, flse), gat = _flash_fwd("fox_attn", qkv_h, fk_aug, qkv_h, FOX_HEAD_DIM, tq,
                                 exchange=("gather", later_shards), q_head0=0, v_head0=2 * FOX_HEADS)
    w_kva = _pad_cols(gat[0].reshape(D, KV_LORA + QK_ROPE), KV_A_PAD)
    w_kvb_h = gat[1]
    w_qa = gat[2].reshape(D, Q_LORA)
    w_qb_h = gat[3]
    w_mo = gat[4].reshape(D, D)
    w_up = gat[5].transpose(1, 2, 0, 3).reshape(2, D, D_FF)
    w_down = gat[6].transpose(1, 0, 2, 3).reshape(2, D_FF, D)
    fctx = _from_heads(fo).astype(BF16)
    x1 = _mm("fox_out", fctx, w_fo, "nn", (F32,), epi=lambda acc, r: (acc + r,), extras=(x0,))
    x2, mlp0 = _mlp_fwd("l0_ffn", x1, g_ffn0, w_up[0], w_down[0])

    src = _rms("kv_norm", x2, g_kv, BF16)
    kva = _mm("kv_a", src, w_kva, "nn", (F32,))
    kva_lat = kva[:, :KV_LORA]
    c_kv = _rms("kv_a_norm", kva_lat, g_kva, BF16)
    k_rope = _rope("k_rope", kva[:, KV_LORA:KV_LORA + QK_ROPE][None], cos2, sgn_sin, BF16)
    kvb_h = _mm_heads("kv_b", c_kv, w_kvb_h, BF16)
    mk = jnp.concatenate([kvb_h[:, :, :QK_NOPE],
                          jnp.broadcast_to(k_rope, (MLA_HEADS, T, QK_ROPE))], axis=-1)
    mv = kvb_h[:, :, QK_NOPE:]

    h1 = _rms("l1_mix_norm", x2, g_mix1, BF16)
    qa = _mm("q_a", h1, w_qa, "nn", (F32,))
    c_q = _rms("q_a_norm", qa, g_qa, BF16)
    qf_h = _mm_heads("q_b", c_q, w_qb_h, F32)
    mla_scale = (QK_NOPE + QK_ROPE) ** -0.5
    mq = _mla_q_prep("q_prep", qf_h, cos2, sgn_sin, mla_scale)
    mv_aug = _widen(mv, MLA_AUG, ones_at=V_HEAD)
    (mo, mlse), _ = _flash_fwd("mla_attn", mq, mk, mv_aug, V_HEAD, tq)
    mctx = _from_heads(mo).astype(BF16)
    x3 = _mm("mla_out", mctx, w_mo, "nn", (F32,), epi=lambda acc, r: (acc + r,), extras=(x2,))
    x4, mlp1 = _mlp_fwd("l1_ffn", x3, g_ffn1, w_up[1], w_down[1])

    g4, dg_fin, loss_vec = _loss_head("loss_head", x4, g_fin, tgt)

    g3, dg_ffn1, dw_up1, dw_down1 = _mlp_bwd("l1_ffn", g4, x3, g_ffn1, w_up[1], w_down[1], mlp1)

    dw_mo = _mm_tn("mla_out_dw", mctx, g3)
    dmo = _to_heads(_mm("mla_out_dx", g3, w_mo, "nt", (BF16,)), MLA_HEADS)
    mdelta = _row_dot("mla_delta", mo, dmo)
    dqk = QK_NOPE + QK_ROPE
    mq_bwd = _widen(mq, MLA_AUG, pieces_at=dqk, pieces=_split3(-mlse))
    mk_bwd = _widen(mk, MLA_AUG, ones_at=dqk)
    mdo_aug = _widen(dmo, MLA_AUG, pieces_at=V_HEAD, pieces=_split3(-mdelta))
    (mdq, mdk, mdv), _ = _flash_bwd("mla_attn_bwd", mq_bwd, mk_bwd, mv_aug, mdo_aug, mla_scale, tq)
    mdq = mdq[:, :, :dqk]
    mdk = mdk[:, :, :dqk]
    mdv = mdv[:, :, :V_HEAD]
    dq_rope = _rope_bwd("q_rope_bwd", mdq[:, :, QK_NOPE:], cos2, sgn_sin, False)
    dqf_h = jnp.concatenate([mdq[:, :, :QK_NOPE], dq_rope], axis=-1)
    dw_qb_h = _mm_heads_dw("q_b_dw", c_q, dqf_h)
    dc_q = _mm_heads_dx("q_b_dx", dqf_h, w_qb_h)
    dqa, dg_qa = _rms_bwd("q_a_norm_bwd", qa, g_qa, dc_q)
    dw_qa = _mm_tn("q_a_dw", h1, dqa)
    dh1 = _mm("q_a_dx", dqa, w_qa, "nt", (F32,))
    g2a, dg_mix1 = _rms_bwd("l1_mix_norm_bwd", x2, g_mix1, dh1, dres=g3)

    dk_rope = _rope_bwd("k_rope_bwd", mdk[:, :, QK_NOPE:], cos2, sgn_sin, True)
    dkvb_h = jnp.concatenate([mdk[:, :, :QK_NOPE], mdv], axis=-1)
    dw_kvb_h = _mm_heads_dw("kv_b_dw", c_kv, dkvb_h)
    dc_kv = _mm_heads_dx("kv_b_dx", dkvb_h, w_kvb_h)
    dkva_lat, dg_kva = _rms_bwd("kv_a_norm_bwd", kva_lat, g_kva, dc_kv)
    dkva = _pad_cols(jnp.concatenate([dkva_lat, dk_rope], axis=-1), KV_A_PAD)
    dw_kva = _mm_tn("kv_a_dw", src, dkva)[:, :KV_LORA + QK_ROPE]
    dsrc = _mm("kv_a_dx", dkva, w_kva, "nt", (F32,))
    g2, dg_kv = _rms_bwd("kv_norm_bwd", x2, g_kv, dsrc, dres=g2a)

    g1, dg_ffn0, dw_up0, dw_down0 = _mlp_bwd("l0_ffn", g2, x1, g_ffn0, w_up[0], w_down[0], mlp0)

    dw_fo = _mm_tn("fox_out_dw", fctx, g1)
    dfo = _to_heads(_mm("fox_out_dx", g1, w_fo, "nt", (BF16,)), FOX_HEADS)
    fdelta = _row_dot("fox_delta", fo, dfo)
    fq_bwd = _widen(qkv_h[:FOX_HEADS], FOX_AUG, pieces_at=FOX_HEAD_DIM + 4, pieces=_split3(-flse))
    fdo_aug = _widen(dfo, FOX_AUG, pieces_at=FOX_HEAD_DIM, pieces=_split3(-fdelta))
    dw_up = jnp.stack([dw_up0, dw_up1])
    dw_down = jnp.stack([dw_down0, dw_down1])
    early = [
        dw_fo.reshape(N_DEV, D // N_DEV, D),
        dw_kva.reshape(N_DEV, D // N_DEV, KV_LORA + QK_ROPE),
        dw_kvb_h,
        dw_qa.reshape(N_DEV, D // N_DEV, Q_LORA),
        dw_qb_h,
        dw_mo.reshape(N_DEV, D // N_DEV, D),
        dw_up.reshape(2, D, N_DEV, -1).transpose(2, 0, 1, 3),
        dw_down.reshape(2, N_DEV, D_FF // N_DEV, D).transpose(1, 0, 2, 3),
    ]
    (fdq_aug, fdk_aug, fdv_aug), early_parts = _flash_bwd(
        "fox_attn_bwd", fq_bwd, fk_aug, qkv_h, fdo_aug, fox_scale, tq,
        exchange=("scatter", [g.astype(BF16) for g in early]), v_head0=2 * FOX_HEADS)
    fdq = fdq_aug[:, :, :FOX_HEAD_DIM]
    fdk = fdk_aug[:, :, :FOX_HEAD_DIM]
    fdv = fdv_aug[:, :, :FOX_HEAD_DIM]
    dfl, db_f = _gate_cumsum_bwd("fox_gate_scan_bwd", fdq_aug[:, :, FOX_HEAD_DIM + 3],
                                 fdk_aug[:, :, FOX_HEAD_DIM], fl, b_f, tq)
    dqkv = jnp.concatenate([_from_heads(fdq), _from_heads(fdk), _from_heads(fdv)], axis=-1).astype(BF16)
    dfl_pad = _pad_cols(dfl.T, 128)
    dw_qkv = _mm_tn("fox_qkv_dw", h0, dqkv)
    dw_f = _mm_tn("fox_gate_dw", h0, dfl_pad)[:, :FOX_HEADS]
    dw_in = jnp.concatenate([dw_qkv, dw_f], axis=-1)
    dh0a = _mm("fox_gate_dx", dfl_pad, w_f, "nt", (F32,))
    dh0 = _mm("fox_qkv_dx", dqkv, w_qkv, "nt", (F32,), epi=lambda acc, r: (acc + r,), extras=(dh0a,))
    grad_x, dg_mix0 = _rms_bwd("l0_mix_norm_bwd", x0, g_mix0, dh0, dres=g1)

    late = dw_in.reshape(D, N_DEV, -1).transpose(1, 0, 2).astype(BF16)
    parts = list(_alltoall_grads("alltoall_fox_w_in", [late])) + list(early_parts)

    names = ["fox_w_in", "fox_w_out", "mla_w_kv_a", "mla_w_kv_b", "mla_w_q_a", "mla_w_q_b",
             "mla_w_out", "ffn_w_up", "ffn_w_down"]
    moms = [m_fox_w_in, m_fox_w_out, m_mla_w_kv_a, m_mla_w_kv_b, m_mla_w_q_a, m_mla_w_q_b,
            m_mla_w_out, m_ffn_w_up, m_ffn_w_down]
    vars_ = [v_fox_w_in, v_fox_w_out, v_mla_w_kv_a, v_mla_w_kv_b, v_mla_w_q_a, v_mla_w_q_b,
             v_mla_w_out, v_ffn_w_up, v_ffn_w_down]
    full = [fox_w_in, fox_w_out, mla_w_kv_a, mla_w_kv_b, mla_w_q_a, mla_w_q_b, mla_w_out,
            ffn_w_up, ffn_w_down]
    big = {}
    for nm, p, w, m, v in zip(names, parts, full, moms, vars_):
        C = w.shape[-1]
        res = _adamw(f"adamw_{nm}", p.reshape(N_DEV, -1, C), w.reshape(-1, C), m.reshape(-1, C),
                     v.reshape(-1, C))
        big[nm] = [r.reshape(w.shape) for r in res]

    zrow = jnp.zeros((D,), F32)
    g_small = _pack_small(jnp.concatenate([dg_mix0, dg_mix1]), jnp.concatenate([dg_ffn0, dg_ffn1]),
                          dg_kv, dg_fin, dg_kva, dg_qa, db_f, zrow.at[0].set(loss_vec[0, 0]))
    tot_small = _allreduce_small(g_small)
    w_small = _pack_small(norm_mix_g, norm_ffn_g, kv_norm_g, final_norm_g, mla_kv_a_norm_g,
                          mla_q_a_norm_g, fox_b_f, zrow)
    m_small = _pack_small(m_norm_mix_g, m_norm_ffn_g, m_kv_norm_g, m_final_norm_g, m_mla_kv_a_norm_g,
                          m_mla_q_a_norm_g, m_fox_b_f, zrow)
    v_small = _pack_small(v_norm_mix_g, v_norm_ffn_g, v_kv_norm_g, v_final_norm_g, v_mla_kv_a_norm_g,
                          v_mla_q_a_norm_g, v_fox_b_f, zrow)
    small = _adamw("adamw_small", tot_small[None], w_small, m_small, v_small)
    loss = tot_small[7, 0]
    small = [_unpack_small(s) for s in small]

    def ordered(i):
        mix, ffn, bf, kv, kva, qa, fin = small[i]
        return [mix, ffn, big["fox_w_in"][i], bf, big["fox_w_out"][i], kv, big["mla_w_kv_a"][i], kva,
                big["mla_w_kv_b"][i], big["mla_w_q_a"][i], qa, big["mla_w_q_b"][i],
                big["mla_w_out"][i], big["ffn_w_up"][i], big["ffn_w_down"][i], fin]

    return (loss, grad_x[None], *ordered(0), *ordered(1), *ordered(2), *ordered(3))
```

```python
import functools
import math

import jax
import jax.numpy as jnp
from jax import lax
from jax.experimental import pallas as pl
from jax.experimental.pallas import tpu as pltpu

F32 = jnp.float32
BF16 = jnp.bfloat16
MESH = pl.DeviceIdType.MESH

N_DEV = 8
D_MODEL = 1024
FOX_HEADS = 16
FOX_HEAD_DIM = 64
FOX_AUG = 128
MLA_AUG = 256
MLA_HEADS = 8
QK_NOPE = 128
QK_ROPE = 64
V_HEAD = 128
Q_LORA = 384
KV_LORA = 256
KV_A_PAD = 384
D_FF = 4096
ROPE_BASE = 10000.0
EPS = 1e-6
NEG = -1e30

ADAM_LR = 0.001
ADAM_B1 = 0.9
ADAM_B2 = 0.999
ADAM_EPS = 1e-08
ADAM_WD = 0.01
ADAM_STEP = 10

VMEM_LIMIT_BYTES = 56 * 1024 * 1024

NN = (((1,), (0,)), ((), ()))
NT = (((1,), (1,)), ((), ()))
TN = (((0,), (0,)), ((), ()))
_FORMS = {"nn": NN, "nt": NT}


def _cparams(sem=None):
    return pltpu.CompilerParams(dimension_semantics=sem, vmem_limit_bytes=VMEM_LIMIT_BYTES)


def _pick(n, cands):
    for c in cands:
        if c <= n and n % c == 0:
            return c
    return n


def _dot(a, b, dims):
    return lax.dot_general(a, b, dims, preferred_element_type=F32)


def _mm(name, a, b, form, out_dtypes, epi=None, extras=(), tm=1024, tn=512):
    M, K = a.shape
    N = b.shape[1] if form == "nn" else b.shape[0]
    tm = _pick(M, (tm, 512, 256, 128))
    tn = _pick(N, (tn, 384, 256, 128))
    n_ex = len(extras)

    def body(*refs):
        a_ref, b_ref = refs[0], refs[1]
        ex = refs[2:2 + n_ex]
        outs = refs[2 + n_ex:]
        acc = _dot(a_ref[...].astype(BF16), b_ref[...].astype(BF16), _FORMS[form])
        res = epi(acc, *[e[...] for e in ex]) if epi is not None else (acc,)
        for o_ref, r in zip(outs, res):
            o_ref[...] = r.astype(o_ref.dtype)

    if form == "nn":
        b_spec = pl.BlockSpec((K, tn), lambda i, j: (0, j))
    else:
        b_spec = pl.BlockSpec((tn, K), lambda i, j: (j, 0))
    tile = pl.BlockSpec((tm, tn), lambda i, j: (i, j))
    out = pl.pallas_call(
        body, name=name, grid=(M // tm, N // tn),
        in_specs=[pl.BlockSpec((tm, K), lambda i, j: (i, 0)), b_spec] + [tile] * n_ex,
        out_specs=[tile] * len(out_dtypes),
        out_shape=[jax.ShapeDtypeStruct((M, N), dt) for dt in out_dtypes],
        compiler_params=_cparams(("parallel", "arbitrary")),
    )(a, b, *extras)
    return out if len(out_dtypes) > 1 else out[0]


def _mm_tn(name, a, b):
    T, Ka = a.shape
    N = b.shape[1]
    tk = _pick(Ka, (1024, 512, 384, 256, 128))
    tn = _pick(N, (1024, 768, 512, 384, 256, 128))
    tt = _pick(T, (1024, 512, 256, 128))

    def body(a_ref, b_ref, o_ref):
        @pl.when(pl.program_id(2) == 0)
        def _():
            o_ref[...] = jnp.zeros_like(o_ref)

        o_ref[...] += _dot(a_ref[...].astype(BF16), b_ref[...].astype(BF16), TN)

    return pl.pallas_call(
        body, name=name, grid=(Ka // tk, N // tn, T // tt),
        in_specs=[pl.BlockSpec((tt, tk), lambda i, j, t: (t, i)),
                  pl.BlockSpec((tt, tn), lambda i, j, t: (t, j))],
        out_specs=pl.BlockSpec((tk, tn), lambda i, j, t: (i, j)),
        out_shape=jax.ShapeDtypeStruct((Ka, N), F32),
        compiler_params=_cparams(("parallel", "parallel", "arbitrary")),
    )(a, b)


def _mm_heads(name, a, w, out_dtype):
    T, K = a.shape
    H, _, N = w.shape
    tm = _pick(T, (1024, 512, 256, 128))

    def body(a_ref, w_ref, o_ref):
        av = a_ref[...].astype(BF16)
        for s in range(H):
            o_ref[s] = _dot(av, w_ref[s].astype(BF16), NN).astype(o_ref.dtype)

    return pl.pallas_call(
        body, name=name, grid=(T // tm,),
        in_specs=[pl.BlockSpec((tm, K), lambda i: (i, 0)), pl.BlockSpec((H, K, N), lambda i: (0, 0, 0))],
        out_specs=pl.BlockSpec((H, tm, N), lambda i: (0, i, 0)),
        out_shape=jax.ShapeDtypeStruct((H, T, N), out_dtype),
        compiler_params=_cparams(("parallel",)),
    )(a, w)


def _mm_head_slabs(name, a, w, tails, head_dim, out_dtype, heads_per_step=8):
    T, K = a.shape
    S, _, tail = tails.shape
    hb = heads_per_step
    tm = _pick(T, (1024, 512, 256, 128))

    def body(a_ref, w_ref, t_ref, o_ref):
        acc = _dot(a_ref[...].astype(BF16), w_ref[...].astype(BF16), NN)
        for s in range(hb):
            slab = jnp.concatenate([acc[:, s * head_dim:(s + 1) * head_dim],
                                    jnp.broadcast_to(t_ref[s], (tm, tail))], axis=-1)
            o_ref[s] = slab.astype(o_ref.dtype)

    return pl.pallas_call(
        body, name=name, grid=(T // tm, S // hb),
        in_specs=[pl.BlockSpec((tm, K), lambda i, j: (i, 0)),
                  pl.BlockSpec((K, hb * head_dim), lambda i, j: (0, j)),
                  pl.BlockSpec((hb, 1, tail), lambda i, j: (j, 0, 0))],
        out_specs=pl.BlockSpec((hb, tm, head_dim + tail), lambda i, j: (j, i, 0)),
        out_shape=jax.ShapeDtypeStruct((S, T, head_dim + tail), out_dtype),
        compiler_params=_cparams(("parallel", "arbitrary")),
    )(a, w, tails)


def _mm_heads_dw(name, a, g):
    T, K = a.shape
    H, _, N = g.shape
    tt = _pick(T, (1024, 512, 256, 128))

    def body(a_ref, g_ref, o_ref):
        @pl.when(pl.program_id(0) == 0)
        def _():
            o_ref[...] = jnp.zeros_like(o_ref)

        av = a_ref[...].astype(BF16)
        for s in range(H):
            o_ref[s] += _dot(av, g_ref[s].astype(BF16), TN)

    return pl.pallas_call(
        body, name=name, grid=(T // tt,),
        in_specs=[pl.BlockSpec((tt, K), lambda t: (t, 0)), pl.BlockSpec((H, tt, N), lambda t: (0, t, 0))],
        out_specs=pl.BlockSpec((H, K, N), lambda t: (0, 0, 0)),
        out_shape=jax.ShapeDtypeStruct((H, K, N), F32),
        compiler_params=_cparams(("arbitrary",)),
    )(a, g)


def _mm_heads_dx(name, g, w):
    H, T, N = g.shape
    K = w.shape[1]
    tm = _pick(T, (512, 256, 128))

    def body(g_ref, w_ref, o_ref):
        acc = _dot(g_ref[0].astype(BF16), w_ref[0].astype(BF16), NT)
        for s in range(1, H):
            acc = acc + _dot(g_ref[s].astype(BF16), w_ref[s].astype(BF16), NT)
        o_ref[...] = acc

    return pl.pallas_call(
        body, name=name, grid=(T // tm,),
        in_specs=[pl.BlockSpec((H, tm, N), lambda i: (0, i, 0)), pl.BlockSpec((H, K, N), lambda i: (0, 0, 0))],
        out_specs=pl.BlockSpec((tm, K), lambda i: (i, 0)),
        out_shape=jax.ShapeDtypeStruct((T, K), F32),
        compiler_params=_cparams(("parallel",)),
    )(g, w)


def _rms(name, x, g, out_dtype):
    T, D = x.shape
    tm = _pick(T, (1024, 512, 256, 128))

    def body(x_ref, g_ref, o_ref):
        xf = x_ref[...]
        r = lax.rsqrt(jnp.mean(xf * xf, axis=-1, keepdims=True) + EPS)
        o_ref[...] = (xf * r * g_ref[...]).astype(o_ref.dtype)

    return pl.pallas_call(
        body, name=name, grid=(T // tm,),
        in_specs=[pl.BlockSpec((tm, D), lambda i: (i, 0)), pl.BlockSpec((1, D), lambda i: (0, 0))],
        out_specs=pl.BlockSpec((tm, D), lambda i: (i, 0)),
        out_shape=jax.ShapeDtypeStruct((T, D), out_dtype),
        compiler_params=_cparams(("parallel",)),
    )(x, g)


def _rms_bwd(name, x, g, dh, dres=None):
    T, D = x.shape
    tm = _pick(T, (512, 256, 128))
    has_res = dres is not None

    def body(*refs):
        if has_res:
            x_ref, g_ref, dh_ref, dres_ref, dx_ref, dg_ref = refs
        else:
            x_ref, g_ref, dh_ref, dx_ref, dg_ref = refs

        @pl.when(pl.program_id(0) == 0)
        def _():
            dg_ref[...] = jnp.zeros_like(dg_ref)

        xf = x_ref[...]
        r = lax.rsqrt(jnp.mean(xf * xf, axis=-1, keepdims=True) + EPS)
        xhat = xf * r
        dy = dh_ref[...].astype(F32)
        dxh = dy * g_ref[...]
        dx = r * (dxh - xhat * jnp.mean(dxh * xhat, axis=-1, keepdims=True))
        if has_res:
            dx = dx + dres_ref[...]
        dx_ref[...] = dx
        dg_ref[...] += jnp.sum(dy * xhat, axis=0, keepdims=True)

    row = pl.BlockSpec((tm, D), lambda i: (i, 0))
    vec = pl.BlockSpec((1, D), lambda i: (0, 0))
    ins = [x, g, dh] + ([dres] if has_res else [])
    return pl.pallas_call(
        body, name=name, grid=(T // tm,),
        in_specs=[row, vec, row] + ([row] if has_res else []),
        out_specs=[row, vec],
        out_shape=[jax.ShapeDtypeStruct((T, D), F32), jax.ShapeDtypeStruct((1, D), F32)],
        compiler_params=_cparams(("arbitrary",)),
    )(*ins)


def _loss_head(name, x, g, tgt):
    T, D = x.shape
    tm = _pick(T, (512, 256, 128))

    def body(x_ref, g_ref, t_ref, dx_ref, dg_ref, loss_ref):
        @pl.when(pl.program_id(0) == 0)
        def _():
            dg_ref[...] = jnp.zeros_like(dg_ref)
            loss_ref[...] = jnp.zeros_like(loss_ref)

        xf = x_ref[...]
        r = lax.rsqrt(jnp.mean(xf * xf, axis=-1, keepdims=True) + EPS)
        xhat = xf * r
        gv = g_ref[...]
        err = xhat * gv - t_ref[...]
        row_loss = jnp.mean(err * err, axis=-1, keepdims=True)
        loss_ref[...] += 0.5 * jnp.sum(row_loss, axis=0, keepdims=True)
        dy = err * (1.0 / D)
        dxh = dy * gv
        dx_ref[...] = r * (dxh - xhat * jnp.mean(dxh * xhat, axis=-1, keepdims=True))
        dg_ref[...] += jnp.sum(dy * xhat, axis=0, keepdims=True)

    row = pl.BlockSpec((tm, D), lambda i: (i, 0))
    vec = pl.BlockSpec((1, D), lambda i: (0, 0))
    return pl.pallas_call(
        body, name=name, grid=(T // tm,),
        in_specs=[row, vec, row],
        out_specs=[row, vec, pl.BlockSpec((1, 128), lambda i: (0, 0))],
        out_shape=[jax.ShapeDtypeStruct((T, D), F32), jax.ShapeDtypeStruct((1, D), F32),
                   jax.ShapeDtypeStruct((1, 128), F32)],
        compiler_params=_cparams(("arbitrary",)),
    )(x, g, tgt)


def _swap_halves(t):
    half = t.shape[-1] // 2
    return jnp.concatenate([t[:, half:], t[:, :half]], axis=-1)


def _rope(name, t, cos2, sgn_sin, out_dtype):
    H, T, R = t.shape
    tm = _pick(T, (1024, 512, 256, 128))

    def body(t_ref, c_ref, s_ref, o_ref):
        tf = t_ref[...].astype(F32)
        o_ref[...] = (tf * c_ref[...] + _swap_halves(tf) * s_ref[...]).astype(o_ref.dtype)

    slab = pl.BlockSpec((None, tm, R), lambda h, i: (h, i, 0))
    tab = pl.BlockSpec((tm, R), lambda h, i: (i, 0))
    return pl.pallas_call(
        body, name=name, grid=(H, T // tm),
        in_specs=[slab, tab, tab], out_specs=slab,
        out_shape=jax.ShapeDtypeStruct((H, T, R), out_dtype),
        compiler_params=_cparams(("parallel", "parallel")),
    )(t, cos2, sgn_sin)


def _mla_q_prep(name, qf, cos2, sgn_sin, scale):
    H, T, W = qf.shape
    R = cos2.shape[1]
    tm = _pick(T, (1024, 512, 256, 128))

    def body(t_ref, c_ref, s_ref, o_ref):
        tf = t_ref[...]
        r = tf[:, W - R:]
        roped = r * c_ref[...] + _swap_halves(r) * s_ref[...]
        o_ref[...] = (jnp.concatenate([tf[:, :W - R], roped], axis=-1) * scale).astype(o_ref.dtype)

    slab = pl.BlockSpec((None, tm, W), lambda h, i: (h, i, 0))
    tab = pl.BlockSpec((tm, R), lambda h, i: (i, 0))
    return pl.pallas_call(
        body, name=name, grid=(H, T // tm),
        in_specs=[slab, tab, tab], out_specs=slab,
        out_shape=jax.ShapeDtypeStruct((H, T, W), BF16),
        compiler_params=_cparams(("parallel", "parallel")),
    )(qf, cos2, sgn_sin)


def _rope_bwd(name, dy, cos2, sgn_sin, sum_heads):
    H, T, R = dy.shape
    tm = _pick(T, (1024, 512, 256, 128))

    def body(d_ref, c_ref, s_ref, o_ref):
        d = d_ref[...]
        if sum_heads:
            tot = d[0]
            for h in range(1, H):
                tot = tot + d[h]
            d = tot
        o_ref[...] = d * c_ref[...] + _swap_halves(d * s_ref[...])

    if sum_heads:
        grid = (T // tm,)
        in_slab = pl.BlockSpec((H, tm, R), lambda i: (0, i, 0))
        out_slab = pl.BlockSpec((tm, R), lambda i: (i, 0))
        tab = pl.BlockSpec((tm, R), lambda i: (i, 0))
        out_shape = jax.ShapeDtypeStruct((T, R), F32)
        sem = ("parallel",)
    else:
        grid = (H, T // tm)
        in_slab = pl.BlockSpec((None, tm, R), lambda h, i: (h, i, 0))
        out_slab = in_slab
        tab = pl.BlockSpec((tm, R), lambda h, i: (i, 0))
        out_shape = jax.ShapeDtypeStruct((H, T, R), F32)
        sem = ("parallel", "parallel")
    return pl.pallas_call(
        body, name=name, grid=grid, in_specs=[in_slab, tab, tab], out_specs=out_slab,
        out_shape=out_shape, compiler_params=_cparams(sem),
    )(dy, cos2, sgn_sin)


def _log_sigmoid(z):
    return jnp.minimum(z, 0.0) - jnp.log(1.0 + jnp.exp(-jnp.abs(z)))


def _gate_cumsum(name, fl, b, tb):
    H, T = fl.shape

    def body(f_ref, b_ref, c_ref, carry):
        @pl.when(pl.program_id(0) == 0)
        def _():
            carry[...] = jnp.zeros_like(carry)

        ls = _log_sigmoid(f_ref[...] + b_ref[...])
        src = lax.broadcasted_iota(jnp.int32, (tb, tb), 0)
        dst = lax.broadcasted_iota(jnp.int32, (tb, tb), 1)
        tri = (src <= dst).astype(F32)
        c = lax.dot_general(ls, tri, NN, precision=lax.Precision.HIGHEST,
                            preferred_element_type=F32) + carry[...]
        c_ref[...] = c
        carry[...] = carry[...] + jnp.sum(ls, axis=-1, keepdims=True)

    return pl.pallas_call(
        body, name=name, grid=(T // tb,),
        in_specs=[pl.BlockSpec((H, tb), lambda i: (0, i)), pl.BlockSpec((H, 1), lambda i: (0, 0))],
        out_specs=pl.BlockSpec((H, tb), lambda i: (0, i)),
        out_shape=jax.ShapeDtypeStruct((H, T), F32),
        scratch_shapes=[pltpu.VMEM((H, 1), F32)],
        compiler_params=_cparams(("arbitrary",)),
    )(fl, b)


def _gate_cumsum_bwd(name, d_query, d_key, fl, b, tb):
    H, T = fl.shape
    nb = T // tb

    def body(dq_ref, dk_ref, f_ref, b_ref, dfl_ref, db_ref, carry):
        @pl.when(pl.program_id(0) == 0)
        def _():
            carry[...] = jnp.zeros_like(carry)
            db_ref[...] = jnp.zeros_like(db_ref)

        d = dq_ref[...] - dk_ref[...]
        src = lax.broadcasted_iota(jnp.int32, (tb, tb), 0)
        dst = lax.broadcasted_iota(jnp.int32, (tb, tb), 1)
        tri = (src >= dst).astype(F32)
        dls = lax.dot_general(d, tri, NN, precision=lax.Precision.HIGHEST,
                              preferred_element_type=F32) + carry[...]
        z = f_ref[...] + b_ref[...]
        dfl = dls * (1.0 / (1.0 + jnp.exp(z)))
        dfl_ref[...] = dfl
        db_ref[...] += jnp.sum(dfl, axis=-1, keepdims=True)
        carry[...] = carry[...] + jnp.sum(d, axis=-1, keepdims=True)

    blk = pl.BlockSpec((H, tb), lambda i: (0, nb - 1 - i))
    vec = pl.BlockSpec((H, 1), lambda i: (0, 0))
    return pl.pallas_call(
        body, name=name, grid=(nb,),
        in_specs=[blk, blk, blk, vec], out_specs=[blk, vec],
        out_shape=[jax.ShapeDtypeStruct((H, T), F32), jax.ShapeDtypeStruct((H, 1), F32)],
        scratch_shapes=[pltpu.VMEM((H, 1), F32)],
        compiler_params=_cparams(("arbitrary",)),
    )(d_query, d_key, fl, b)


def _causal_mask(tq, rows_are_queries):
    r = lax.broadcasted_iota(jnp.int32, (tq, tq), 0)
    c = lax.broadcasted_iota(jnp.int32, (tq, tq), 1)
    return (c <= r) if rows_are_queries else (r <= c)


def _chunk_rows(j, tq):
    return pl.ds(pl.multiple_of(j * tq, tq), tq)


def _column_as_row(col):
    return jnp.broadcast_to(col, (col.shape[0], 128)).T[:1, :]


def _flash_fwd(name, q, k, v_aug, dv, tq, exchange=None, q_head0=0, v_head0=0):
    H, T, dqk = k.shape
    dva = v_aug.shape[2]
    nq = T // tq

    def body(q_ref, k_ref, v_ref, o_ref, lse_ref, m_sc, acc_sc):
        qi = pl.program_id(1)
        m_sc[...] = jnp.full_like(m_sc, NEG)
        acc_sc[...] = jnp.zeros_like(acc_sc)

        def chunk(j, masked):
            rows = _chunk_rows(j, tq)
            s = _dot(q_ref[...], k_ref[rows, :], NT)
            if masked:
                s = jnp.where(_causal_mask(tq, True), s, NEG)
            m_prev = m_sc[...]
            m_new = jnp.maximum(m_prev, jnp.max(s, axis=1, keepdims=True))
            p = jnp.exp(s - jnp.tile(m_new, (1, tq // 128)))
            alpha = jnp.tile(jnp.exp(m_prev - m_new), (1, dva // 128))
            acc_sc[...] = alpha * acc_sc[...] + _dot(p.astype(BF16), v_ref[rows, :], NN)
            m_sc[...] = m_new

        def off_diagonal(j, carry):
            chunk(j, False)
            return carry

        lax.fori_loop(0, qi, off_diagonal, 0)
        chunk(qi, True)
        acc = acc_sc[...]
        l = acc[:, dv:dv + 1]
        o_ref[...] = acc[:, :dv] / l
        lse_ref[...] = _column_as_row(m_sc[:, :1] + jnp.log(l))

    (o, lse_rows), exchanged = _call_carrying(
        body, name, (H, nq),
        in_specs=[pl.BlockSpec((None, tq, dqk), lambda h, i: (h + q_head0, i, 0)),
                  pl.BlockSpec((None, T, dqk), lambda h, i: (h, 0, 0)),
                  pl.BlockSpec((None, T, dva), lambda h, i: (h + v_head0, 0, 0))],
        out_specs=[pl.BlockSpec((None, tq, dv), lambda h, i: (h, i, 0)),
                   pl.BlockSpec((None, 1, tq), lambda h, i: (h, 0, i))],
        out_shape=[jax.ShapeDtypeStruct((H, T, dv), F32), jax.ShapeDtypeStruct((H, 1, T), F32)],
        scratch_shapes=[pltpu.VMEM((tq, 128), F32), pltpu.VMEM((tq, dva), F32)],
        operands=(q, k, v_aug), exchange=exchange)
    return (o, lse_rows.reshape(H, T)), exchanged


def _row_dot(name, a, b):
    H, T, d = a.shape
    tm = _pick(T, (1024, 512, 256, 128))

    def body(a_ref, b_ref, o_ref):
        col = jnp.sum(a_ref[...].astype(F32) * b_ref[...].astype(F32), axis=-1, keepdims=True)
        o_ref[...] = _column_as_row(col)

    slab = pl.BlockSpec((None, tm, d), lambda h, i: (h, i, 0))
    return pl.pallas_call(
        body, name=name, grid=(H, T // tm), in_specs=[slab, slab],
        out_specs=pl.BlockSpec((None, 1, tm), lambda h, i: (h, 0, i)),
        out_shape=jax.ShapeDtypeStruct((H, 1, T), F32),
        compiler_params=_cparams(("parallel", "parallel")),
    )(a, b).reshape(H, T)


def _flash_bwd(name, q, k, v, do, scale, tq, exchange=None, v_head0=0):
    H, T, dqk = q.shape
    dva = v.shape[2]
    nq = T // tq

    def body(q_ref, k_ref, v_ref, do_ref, dq_ref, dk_ref, dv_ref, dk_sc, dv_sc):
        ki = pl.program_id(1)
        dk_sc[...] = jnp.zeros_like(dk_sc)
        dv_sc[...] = jnp.zeros_like(dv_sc)

        @pl.when(ki == 0)
        def _():
            dq_ref[...] = jnp.zeros_like(dq_ref)

        def chunk(i, masked):
            rows = _chunk_rows(i, tq)
            qb = q_ref[rows, :]
            dob = do_ref[rows, :]
            kb = k_ref[...]
            st = _dot(kb, qb, NT)
            if masked:
                st = jnp.where(_causal_mask(tq, False), st, NEG)
            pt = jnp.exp(st)
            dv_sc[...] += _dot(pt.astype(BF16), dob, NN)
            dst = (pt * _dot(v_ref[...], dob, NT)).astype(BF16)
            dk_sc[...] += _dot(dst, qb, NN)
            dq_ref[rows, :] += _dot(dst, kb, TN)

        def off_diagonal(i, carry):
            chunk(i, False)
            return carry

        chunk(ki, True)
        lax.fori_loop(ki + 1, nq, off_diagonal, 0)
        dk_ref[...] = dk_sc[...]
        dv_ref[...] = dv_sc[...]

        @pl.when(ki == nq - 1)
        def _():
            dq_ref[...] = dq_ref[...] * scale

    whole_q = pl.BlockSpec((None, T, dqk), lambda h, j: (h, 0, 0))
    k_spec = pl.BlockSpec((None, tq, dqk), lambda h, j: (h, j, 0))
    v_spec = pl.BlockSpec((None, tq, dva), lambda h, j: (h, j, 0))
    v_in_spec = pl.BlockSpec((None, tq, dva), lambda h, j: (h + v_head0, j, 0))
    return _call_carrying(
        body, name, (H, nq),
        in_specs=[whole_q, k_spec, v_in_spec, pl.BlockSpec((None, T, dva), lambda h, j: (h, 0, 0))],
        out_specs=[whole_q, k_spec, v_spec],
        out_shape=[jax.ShapeDtypeStruct((H, T, dqk), F32), jax.ShapeDtypeStruct((H, T, dqk), F32),
                   jax.ShapeDtypeStruct((H, T, dva), F32)],
        scratch_shapes=[pltpu.VMEM((tq, dqk), F32), pltpu.VMEM((tq, dva), F32)],
        operands=(q, k, v, do), exchange=exchange)


def _adamw_math(w, g, m, v):
    m = ADAM_B1 * m + (1.0 - ADAM_B1) * g
    v = ADAM_B2 * v + (1.0 - ADAM_B2) * (g * g)
    m_hat = m / (1.0 - ADAM_B1 ** ADAM_STEP)
    v_hat = v / (1.0 - ADAM_B2 ** ADAM_STEP)
    delta = -ADAM_LR * (m_hat / (jnp.sqrt(v_hat) + ADAM_EPS) + ADAM_WD * w)
    return delta, m, v


def _adamw(name, parts, w, m, v):
    P, R, C = parts.shape
    tr = _pick(R, (256, 128, 64, 32, 16, 8))

    def body(p_ref, w_ref, m_ref, v_ref, g_out, d_out, m_out, v_out):
        g = p_ref[0].astype(F32)
        for i in range(1, P):
            g = g + p_ref[i].astype(F32)
        delta, m_new, v_new = _adamw_math(w_ref[...], g, m_ref[...], v_ref[...])
        g_out[...] = g
        d_out[...] = delta
        m_out[...] = m_new
        v_out[...] = v_new

    blk = pl.BlockSpec((tr, C), lambda i: (i, 0))
    sds = jax.ShapeDtypeStruct((R, C), F32)
    return pl.pallas_call(
        body, name=name, grid=(R // tr,),
        in_specs=[pl.BlockSpec((P, tr, C), lambda i: (0, i, 0)), blk, blk, blk],
        out_specs=[blk] * 4, out_shape=[sds] * 4,
        compiler_params=_cparams(("parallel",)),
    )(parts, w, m, v)


def _my_position():
    return lax.axis_index("x"), lax.axis_index("y"), lax.axis_index("c")


def _slot(p):
    return 4 * p[0] + 2 * p[1] + p[2]


def _flip(p, k):
    return tuple((1 - p[i]) if (k >> (2 - i)) & 1 else p[i] for i in range(3))


def _allgather_weights(shards):
    n = len(shards)

    def body(*refs):
        ins = refs[:n]
        outs = refs[n:2 * n]
        send_sems, recv_sems, local_sems = refs[2 * n:]
        x, y, c = _my_position()
        me, sibling = (x, y, c), (x, y, 1 - c)
        chips = [(1 - x, y), (x, 1 - y), (1 - x, 1 - y)]

        def copy(a, k, block, to, src=None):
            dst = outs[a].at[_slot(block)]
            return pltpu.make_async_remote_copy(
                src_ref=dst if src is None else src, dst_ref=dst,
                send_sem=send_sems.at[7 * a + k], recv_sem=recv_sems.at[7 * a + k],
                device_id=to, device_id_type=MESH)

        started = []
        for a in range(n):
            mine = pltpu.make_async_copy(ins[a], outs[a].at[_slot(me)], local_sems.at[a])
            mine.start()
            started.append(mine)
        first = []
        for a in range(n):
            first.append(copy(a, 0, me, sibling, src=ins[a]))
            first += [copy(a, 1 + j, me, (*chip, c), src=ins[a]) for j, chip in enumerate(chips)]
        for cp in first:
            cp.start()
        passed = []
        for j, chip in enumerate(chips):
            for a in range(n):
                copy(a, 1 + j, (*chip, c), me).wait_recv()
                fwd = copy(a, 4 + j, (*chip, c), sibling)
                fwd.start()
                passed.append(fwd)
        for a in range(n):
            copy(a, 0, sibling, me).wait_recv()
            for j, chip in enumerate(chips):
                copy(a, 4 + j, (*chip, 1 - c), me).wait_recv()
        for cp in first + passed:
            cp.wait_send()
        for mine in started:
            mine.wait()

    hbm = pl.BlockSpec(memory_space=pl.ANY)
    return pl.pallas_call(
        body, name="allgather_weights",
        in_specs=[hbm] * n, out_specs=[hbm] * n,
        out_shape=[jax.ShapeDtypeStruct((N_DEV,) + s.shape, s.dtype) for s in shards],
        scratch_shapes=[pltpu.SemaphoreType.DMA((7 * n,)), pltpu.SemaphoreType.DMA((7 * n,)),
                        pltpu.SemaphoreType.DMA((n,))],
        compiler_params=pltpu.CompilerParams(has_side_effects=True),
    )(*shards)


def _exchange_copies(kind, x_in, x_out, send_sems, recv_sems, local_sems, receives=True):
    me = _my_position()
    mine = _slot(me)
    local, sends, recvs = [], [], []
    for a in range(len(x_in)):
        src = x_in[a] if kind == "gather" else x_in[a].at[mine]
        local.append(pltpu.make_async_copy(src, x_out[a].at[mine], local_sems.at[a]))
    for k in range(1, N_DEV):
        peer = _flip(me, k)
        theirs = _slot(peer)
        for a in range(len(x_in)):
            src = x_in[a] if kind == "gather" else x_in[a].at[theirs]
            ends = [(x_out[a].at[mine], sends)] + ([(x_out[a].at[theirs], recvs)] if receives else [])
            for dst, group in ends:
                group.append(pltpu.make_async_remote_copy(
                    src_ref=src, dst_ref=dst, send_sem=send_sems.at[7 * a + k - 1],
                    recv_sem=recv_sems.at[7 * a + k - 1], device_id=peer, device_id_type=MESH))
    return local, sends, recvs


def _exchange_out_shapes(kind, arrays):
    return [jax.ShapeDtypeStruct(((N_DEV,) + a.shape) if kind == "gather" else a.shape, a.dtype)
            for a in arrays]


def _exchange_sems(n):
    return [pltpu.SemaphoreType.DMA((7 * n,)), pltpu.SemaphoreType.DMA((7 * n,)),
            pltpu.SemaphoreType.DMA((n,))]


def _alltoall_grads(name, grads):
    n = len(grads)

    def body(*refs):
        local, sends, recvs = _exchange_copies("scatter", refs[:n], refs[n:2 * n], *refs[2 * n:])
        for cp in local + sends:
            cp.start()
        for cp in recvs:
            cp.wait_recv()
        for cp in sends:
            cp.wait_send()
        for cp in local:
            cp.wait()

    hbm = pl.BlockSpec(memory_space=pl.ANY)
    return pl.pallas_call(
        body, name=name,
        in_specs=[hbm] * n, out_specs=[hbm] * n,
        out_shape=_exchange_out_shapes("scatter", grads), scratch_shapes=_exchange_sems(n),
        compiler_params=pltpu.CompilerParams(has_side_effects=True),
    )(*grads)


def _call_carrying(body, name, grid, in_specs, out_specs, out_shape, scratch_shapes, operands, exchange):
    if exchange is None:
        out = pl.pallas_call(
            body, name=name, grid=grid, in_specs=in_specs, out_specs=out_specs, out_shape=out_shape,
            scratch_shapes=scratch_shapes, compiler_params=_cparams(("parallel",) + ("arbitrary",) * (len(grid) - 1)),
        )(*operands)
        return out, None
    kind, arrays = exchange
    n, n_in, n_out, n_sc = len(arrays), len(in_specs), len(out_specs), len(scratch_shapes)

    def full_body(*refs):
        ins, refs = refs[:n_in], refs[n_in:]
        x_in, refs = refs[:n], refs[n:]
        outs, refs = refs[:n_out], refs[n_out:]
        x_out, refs = refs[:n], refs[n:]
        scratch, sems = refs[:n_sc], refs[n_sc:]
        first = last = None
        for axis, size in enumerate(grid):
            at_start = pl.program_id(axis) == 0
            at_end = pl.program_id(axis) == size - 1
            first = at_start if first is None else jnp.logical_and(first, at_start)
            last = at_end if last is None else jnp.logical_and(last, at_end)

        @pl.when(first)
        def _():
            local, sends, _ = _exchange_copies(kind, x_in, x_out, *sems, receives=False)
            for cp in local + sends:
                cp.start()

        body(*ins, *outs, *scratch)

        @pl.when(last)
        def _():
            local, sends, recvs = _exchange_copies(kind, x_in, x_out, *sems)
            for cp in recvs:
                cp.wait_recv()
            for cp in sends:
                cp.wait_send()
            for cp in local:
                cp.wait()

    hbm = pl.BlockSpec(memory_space=pl.ANY)
    out = pl.pallas_call(
        full_body, name=name, grid=grid,
        in_specs=list(in_specs) + [hbm] * n, out_specs=list(out_specs) + [hbm] * n,
        out_shape=list(out_shape) + _exchange_out_shapes(kind, arrays),
        scratch_shapes=list(scratch_shapes) + _exchange_sems(n),
        compiler_params=pltpu.CompilerParams(dimension_semantics=("arbitrary",) * len(grid),
                                             vmem_limit_bytes=VMEM_LIMIT_BYTES, has_side_effects=True),
    )(*operands, *arrays)
    return out[:n_out], out[n_out:]


def _allreduce_small(v):
    R, C = v.shape

    def body(v_ref, o_ref, buf, send_sems, recv_sems):
        me = _my_position()
        buf[_slot(me)] = v_ref[...]
        sends = []
        for k in range(1, N_DEV):
            peer = _flip(me, k)
            cp = pltpu.make_async_remote_copy(
                src_ref=v_ref, dst_ref=buf.at[_slot(me)],
                send_sem=send_sems.at[k - 1], recv_sem=recv_sems.at[k - 1],
                device_id=peer, device_id_type=MESH)
            cp.start()
            sends.append(cp)
        for k in range(1, N_DEV):
            peer = _flip(me, k)
            pltpu.make_async_remote_copy(
                src_ref=v_ref, dst_ref=buf.at[_slot(peer)],
                send_sem=send_sems.at[k - 1], recv_sem=recv_sems.at[k - 1],
                device_id=peer, device_id_type=MESH).wait_recv()
        for cp in sends:
            cp.wait_send()
        tot = buf[0]
        for s in range(1, N_DEV):
            tot = tot + buf[s]
        o_ref[...] = tot

    vm = pl.BlockSpec(memory_space=pltpu.VMEM)
    return pl.pallas_call(
        body, name="allreduce_small",
        in_specs=[vm], out_specs=vm, out_shape=jax.ShapeDtypeStruct((R, C), F32),
        scratch_shapes=[pltpu.VMEM((N_DEV, R, C), F32), pltpu.SemaphoreType.DMA((7,)),
                        pltpu.SemaphoreType.DMA((7,))],
        compiler_params=pltpu.CompilerParams(has_side_effects=True),
    )(v)


def _to_heads(t, heads):
    T = t.shape[0]
    return t.reshape(T, heads, t.shape[1] // heads).transpose(1, 0, 2)


def _from_heads(t):
    H, T, d = t.shape
    return t.transpose(1, 0, 2).reshape(T, H * d)


def _widen(t, width, ones_at=None, pieces_at=None, pieces=None):
    out = jnp.pad(t, ((0, 0), (0, 0), (0, width - t.shape[-1])))
    lane = lax.broadcasted_iota(jnp.int32, (1, 1, width), 2)
    if ones_at is not None:
        out = jnp.where((lane >= ones_at) & (lane < ones_at + 3), jnp.ones((), BF16), out)
    if pieces_at is not None:
        for i in range(3):
            out = jnp.where(lane == pieces_at + i, pieces[i][:, :, None], out)
    return out


def _split3(t):
    hi = lax.reduce_precision(t, 8, 7)
    r = t - hi
    mid = lax.reduce_precision(r, 8, 7)
    lo = lax.reduce_precision(r - mid, 8, 7)
    return hi.astype(BF16), mid.astype(BF16), lo.astype(BF16)


def _pad_cols(t, n):
    return jnp.pad(t, ((0, 0), (0, n - t.shape[1])))


def _pack_small(mix, ffn, kv, fin, kva, qa, bf, last):
    row6 = jnp.concatenate([kva.reshape(-1), qa.reshape(-1), bf.reshape(-1),
                            jnp.zeros((D_MODEL - KV_LORA - Q_LORA - FOX_HEADS,), F32)])
    return jnp.stack([mix[0], mix[1], ffn[0], ffn[1], kv.reshape(-1), fin.reshape(-1), row6, last])


def _unpack_small(p):
    mix = p[0:2]
    ffn = p[2:4]
    kv = p[4]
    fin = p[5]
    kva = p[6, :KV_LORA]
    qa = p[6, KV_LORA:KV_LORA + Q_LORA].reshape(1, Q_LORA)
    bf = p[6, KV_LORA + Q_LORA:KV_LORA + Q_LORA + FOX_HEADS].reshape(1, FOX_HEADS)
    return mix, ffn, bf, kv, kva, qa, fin


def _mlp_fwd(tag, xin, g, w_up, w_down):
    h = _rms(f"{tag}_norm", xin, g, BF16)

    def act(acc):
        r = jnp.maximum(acc, 0.0)
        return acc, r * r

    u, a = _mm(f"{tag}_up", h, w_up, "nn", (BF16, BF16), epi=act)
    xout = _mm(f"{tag}_down", a, w_down, "nn", (F32,), epi=lambda acc, r: (acc + r,), extras=(xin,))
    return xout, (h, u, a)


def _mlp_bwd(tag, gout, xin, g, w_up, w_down, saved):
    h, u, a = saved
    dw_down = _mm_tn(f"{tag}_dwdown", a, gout)
    du = _mm(f"{tag}_du", gout, w_down, "nt", (BF16,),
             epi=lambda acc, uu: (acc * (2.0 * jnp.maximum(uu.astype(F32), 0.0)),), extras=(u,))
    dw_up = _mm_tn(f"{tag}_dwup", h, du)
    dh = _mm(f"{tag}_dh", du, w_up, "nt", (F32,))
    gin, dg = _rms_bwd(f"{tag}_norm_bwd", xin, g, dh, dres=gout)
    return gin, dg, dw_up, dw_down


def kernel(x, norm_mix_g, norm_ffn_g, fox_w_in, fox_b_f, fox_w_out, kv_norm_g, mla_w_kv_a, mla_kv_a_norm_g, mla_w_kv_b, mla_w_q_a, mla_q_a_norm_g, mla_w_q_b, mla_w_out, ffn_w_up, ffn_w_down, final_norm_g, loss_target, m_norm_mix_g, m_norm_ffn_g, m_fox_w_in, m_fox_b_f, m_fox_w_out, m_kv_norm_g, m_mla_w_kv_a, m_mla_kv_a_norm_g, m_mla_w_kv_b, m_mla_w_q_a, m_mla_q_a_norm_g, m_mla_w_q_b, m_mla_w_out, m_ffn_w_up, m_ffn_w_down, m_final_norm_g, v_norm_mix_g, v_norm_ffn_g, v_fox_w_in, v_fox_b_f, v_fox_w_out, v_kv_norm_g, v_mla_w_kv_a, v_mla_kv_a_norm_g, v_mla_w_kv_b, v_mla_w_q_a, v_mla_q_a_norm_g, v_mla_w_q_b, v_mla_w_out, v_ffn_w_up, v_ffn_w_down, v_final_norm_g):
    T = x.shape[1]
    D = D_MODEL
    tq = 512 if T >= 2048 else 128
    x0 = x[0]
    tgt = loss_target[0]

    gat_fox = _allgather_weights([fox_w_in[0].astype(BF16), fox_w_out[0].astype(BF16)])
    later_shards = [s.astype(BF16) for s in (mla_w_kv_a, mla_w_kv_b, mla_w_q_a[0], mla_w_q_b[0],
                                             mla_w_out[0], ffn_w_up, ffn_w_down)]
    w_in = gat_fox[0].transpose(1, 0, 2).reshape(D, 3 * D + FOX_HEADS)
    w_qkv = w_in[:, :3 * D]
    w_f = _pad_cols(w_in[:, 3 * D:], 128)
    w_fo = gat_fox[1].reshape(D, D)
    g_mix0, g_mix1 = norm_mix_g[0:1], norm_mix_g[1:2]
    g_ffn0, g_ffn1 = norm_ffn_g[0:1], norm_ffn_g[1:2]
    g_kv = kv_norm_g.reshape(1, D)
    g_kva = mla_kv_a_norm_g.reshape(1, KV_LORA)
    g_qa = mla_q_a_norm_g.reshape(1, Q_LORA)
    g_fin = final_norm_g.reshape(1, D)

    inv = 1.0 / (ROPE_BASE ** (jnp.arange(0, QK_ROPE, 2, dtype=F32) / QK_ROPE))
    ang = jnp.arange(T, dtype=F32)[:, None] * inv[None, :]
    cos, sin = jnp.cos(ang), jnp.sin(ang)
    cos2 = jnp.concatenate([cos, cos], axis=-1)
    sgn_sin = jnp.concatenate([-sin, sin], axis=-1)

    h0 = _rms("l0_mix_norm", x0, g_mix0, BF16)
    fl_pad = _mm("fox_gate_logit", h0, w_f, "nn", (F32,))
    fl = fl_pad[:, :FOX_HEADS].T
    b_f = fox_b_f.reshape(FOX_HEADS, 1)
    cgate = _gate_cumsum("fox_gate_scan", fl, b_f, tq)
    fox_scale = FOX_HEAD_DIM ** -0.5
    col_scale = jnp.where(jnp.arange(3 * D) < D, fox_scale, 1.0).astype(BF16)
    tail = jnp.arange(FOX_AUG - FOX_HEAD_DIM)
    ones_q = (tail < 3).astype(F32)
    consts_k = ((tail >= 4) & (tail < 7)).astype(F32) + (tail == 3).astype(F32) * (1.0 / fox_scale)
    tails = jnp.broadcast_to(jnp.stack([ones_q, consts_k, ones_q])[:, None, None, :],
                             (3, FOX_HEADS, 1, FOX_AUG - FOX_HEAD_DIM)).reshape(3 * FOX_HEADS, 1, -1)
    qkv_h = _mm_head_slabs("fox_qkv", h0, w_qkv * col_scale, tails, FOX_HEAD_DIM, BF16)
    fk_aug = _widen(qkv_h[FOX_HEADS:2 * FOX_HEADS], FOX_AUG, pieces_at=FOX_HEAD_DIM,
                    pieces=_split3(-cgate))
    (fo, flse), gat = _flash_fwd("fox_attn", qkv_h, fk_aug, qkv_h, FOX_HEAD_DIM, tq,
                                 exchange=("gather", later_shards), q_head0=0, v_head0=2 * FOX_HEADS)
    w_kva = _pad_cols(gat[0].reshape(D, KV_LORA + QK_ROPE), KV_A_PAD)
    w_kvb_h = gat[1]
    w_qa = gat[2].reshape(D, Q_LORA)
    w_qb_h = gat[3]
    w_mo = gat[4].reshape(D, D)
    w_up = gat[5].transpose(1, 2, 0, 3).reshape(2, D, D_FF)
    w_down = gat[6].transpose(1, 0, 2, 3).reshape(2, D_FF, D)
    fctx = _from_heads(fo).astype(BF16)
    x1 = _mm("fox_out", fctx, w_fo, "nn", (F32,), epi=lambda acc, r: (acc + r,), extras=(x0,))
    x2, mlp0 = _mlp_fwd("l0_ffn", x1, g_ffn0, w_up[0], w_down[0])

    src = _rms("kv_norm", x2, g_kv, BF16)
    kva = _mm("kv_a", src, w_kva, "nn", (F32,))
    kva_lat = kva[:, :KV_LORA]
    c_kv = _rms("kv_a_norm", kva_lat, g_kva, BF16)
    k_rope = _rope("k_rope", kva[:, KV_LORA:KV_LORA + QK_ROPE][None], cos2, sgn_sin, BF16)
    kvb_h = _mm_heads("kv_b", c_kv, w_kvb_h, BF16)
    mk = jnp.concatenate([kvb_h[:, :, :QK_NOPE],
                          jnp.broadcast_to(k_rope, (MLA_HEADS, T, QK_ROPE))], axis=-1)
    mv = kvb_h[:, :, QK_NOPE:]

    h1 = _rms("l1_mix_norm", x2, g_mix1, BF16)
    qa = _mm("q_a", h1, w_qa, "nn", (F32,))
    c_q = _rms("q_a_norm", qa, g_qa, BF16)
    qf_h = _mm_heads("q_b", c_q, w_qb_h, F32)
    mla_scale = (QK_NOPE + QK_ROPE) ** -0.5
    mq = _mla_q_prep("q_prep", qf_h, cos2, sgn_sin, mla_scale)
    mv_aug = _widen(mv, MLA_AUG, ones_at=V_HEAD)
    (mo, mlse), _ = _flash_fwd("mla_attn", mq, mk, mv_aug, V_HEAD, tq)
    mctx = _from_heads(mo).astype(BF16)
    x3 = _mm("mla_out", mctx, w_mo, "nn", (F32,), epi=lambda acc, r: (acc + r,), extras=(x2,))
    x4, mlp1 = _mlp_fwd("l1_ffn", x3, g_ffn1, w_up[1], w_down[1])

    g4, dg_fin, loss_vec = _loss_head("loss_head", x4, g_fin, tgt)

    g3, dg_ffn1, dw_up1, dw_down1 = _mlp_bwd("l1_ffn", g4, x3, g_ffn1, w_up[1], w_down[1], mlp1)

    dw_mo = _mm_tn("mla_out_dw", mctx, g3)
    dmo = _to_heads(_mm("mla_out_dx", g3, w_mo, "nt", (BF16,)), MLA_HEADS)
    mdelta = _row_dot("mla_delta", mo, dmo)
    dqk = QK_NOPE + QK_ROPE
    mq_bwd = _widen(mq, MLA_AUG, pieces_at=dqk, pieces=_split3(-mlse))
    mk_bwd = _widen(mk, MLA_AUG, ones_at=dqk)
    mdo_aug = _widen(dmo, MLA_AUG, pieces_at=V_HEAD, pieces=_split3(-mdelta))
    (mdq, mdk, mdv), _ = _flash_bwd("mla_attn_bwd", mq_bwd, mk_bwd, mv_aug, mdo_aug, mla_scale, tq)
    mdq = mdq[:, :, :dqk]
    mdk = mdk[:, :, :dqk]
    mdv = mdv[:, :, :V_HEAD]
    dq_rope = _rope_bwd("q_rope_bwd", mdq[:, :, QK_NOPE:], cos2, sgn_sin, False)
    dqf_h = jnp.concatenate([mdq[:, :, :QK_NOPE], dq_rope], axis=-1)
    dw_qb_h = _mm_heads_dw("q_b_dw", c_q, dqf_h)
    dc_q = _mm_heads_dx("q_b_dx", dqf_h, w_qb_h)
    dqa, dg_qa = _rms_bwd("q_a_norm_bwd", qa, g_qa, dc_q)
    dw_qa = _mm_tn("q_a_dw", h1, dqa)
    dh1 = _mm("q_a_dx", dqa, w_qa, "nt", (F32,))
    g2a, dg_mix1 = _rms_bwd("l1_mix_norm_bwd", x2, g_mix1, dh1, dres=g3)

    dk_rope = _rope_bwd("k_rope_bwd", mdk[:, :, QK_NOPE:], cos2, sgn_sin, True)
    dkvb_h = jnp.concatenate([mdk[:, :, :QK_NOPE], mdv], axis=-1)
    dw_kvb_h = _mm_heads_dw("kv_b_dw", c_kv, dkvb_h)
    dc_kv = _mm_heads_dx("kv_b_dx", dkvb_h, w_kvb_h)
    dkva_lat, dg_kva = _rms_bwd("kv_a_norm_bwd", kva_lat, g_kva, dc_kv)
    dkva = _pad_cols(jnp.concatenate([dkva_lat, dk_rope], axis=-1), KV_A_PAD)
    dw_kva = _mm_tn("kv_a_dw", src, dkva)[:, :KV_LORA + QK_ROPE]
    dsrc = _mm("kv_a_dx", dkva, w_kva, "nt", (F32,))
    g2, dg_kv = _rms_bwd("kv_norm_bwd", x2, g_kv, dsrc, dres=g2a)

    g1, dg_ffn0, dw_up0, dw_down0 = _mlp_bwd("l0_ffn", g2, x1, g_ffn0, w_up[0], w_down[0], mlp0)

    dw_fo = _mm_tn("fox_out_dw", fctx, g1)
    dfo = _to_heads(_mm("fox_out_dx", g1, w_fo, "nt", (BF16,)), FOX_HEADS)
    fdelta = _row_dot("fox_delta", fo, dfo)
    fq_bwd = _widen(qkv_h[:FOX_HEADS], FOX_AUG, pieces_at=FOX_HEAD_DIM + 4, pieces=_split3(-flse))
    fdo_aug = _widen(dfo, FOX_AUG, pieces_at=FOX_HEAD_DIM, pieces=_split3(-fdelta))
    dw_up = jnp.stack([dw_up0, dw_up1])
    dw_down = jnp.stack([dw_down0, dw_down1])
    early = [
        dw_fo.reshape(N_DEV, D // N_DEV, D),
        dw_kva.reshape(N_DEV, D // N_DEV, KV_LORA + QK_ROPE),
        dw_kvb_h,
        dw_qa.reshape(N_DEV, D // N_DEV, Q_LORA),
        dw_qb_h,
        dw_mo.reshape(N_DEV, D // N_DEV, D),
        dw_up.reshape(2, D, N_DEV, -1).transpose(2, 0, 1, 3),
        dw_down.reshape(2, N_DEV, D_FF // N_DEV, D).transpose(1, 0, 2, 3),
    ]
    (fdq_aug, fdk_aug, fdv_aug), early_parts = _flash_bwd(
        "fox_attn_bwd", fq_bwd, fk_aug, qkv_h, fdo_aug, fox_scale, tq,
        exchange=("scatter", [g.astype(BF16) for g in early]), v_head0=2 * FOX_HEADS)
    fdq = fdq_aug[:, :, :FOX_HEAD_DIM]
    fdk = fdk_aug[:, :, :FOX_HEAD_DIM]
    fdv = fdv_aug[:, :, :FOX_HEAD_DIM]
    dfl, db_f = _gate_cumsum_bwd("fox_gate_scan_bwd", fdq_aug[:, :, FOX_HEAD_DIM + 3],
                                 fdk_aug[:, :, FOX_HEAD_DIM], fl, b_f, tq)
    dqkv = jnp.concatenate([_from_heads(fdq), _from_heads(fdk), _from_heads(fdv)], axis=-1).astype(BF16)
    dfl_pad = _pad_cols(dfl.T, 128)
    dw_qkv = _mm_tn("fox_qkv_dw", h0, dqkv)
    dw_f = _mm_tn("fox_gate_dw", h0, dfl_pad)[:, :FOX_HEADS]
    dw_in = jnp.concatenate([dw_qkv, dw_f], axis=-1)
    dh0a = _mm("fox_gate_dx", dfl_pad, w_f, "nt", (F32,))
    dh0 = _mm("fox_qkv_dx", dqkv, w_qkv, "nt", (F32,), epi=lambda acc, r: (acc + r,), extras=(dh0a,))
    grad_x, dg_mix0 = _rms_bwd("l0_mix_norm_bwd", x0, g_mix0, dh0, dres=g1)

    late = dw_in.reshape(D, N_DEV, -1).transpose(1, 0, 2).astype(BF16)
    parts = list(_alltoall_grads("alltoall_fox_w_in", [late])) + list(early_parts)

    names = ["fox_w_in", "fox_w_out", "mla_w_kv_a", "mla_w_kv_b", "mla_w_q_a", "mla_w_q_b",
             "mla_w_out", "ffn_w_up", "ffn_w_down"]
    moms = [m_fox_w_in, m_fox_w_out, m_mla_w_kv_a, m_mla_w_kv_b, m_mla_w_q_a, m_mla_w_q_b,
            m_mla_w_out, m_ffn_w_up, m_ffn_w_down]
    vars_ = [v_fox_w_in, v_fox_w_out, v_mla_w_kv_a, v_mla_w_kv_b, v_mla_w_q_a, v_mla_w_q_b,
             v_mla_w_out, v_ffn_w_up, v_ffn_w_down]
    full = [fox_w_in, fox_w_out, mla_w_kv_a, mla_w_kv_b, mla_w_q_a, mla_w_q_b, mla_w_out,
            ffn_w_up, ffn_w_down]
    big = {}
    for nm, p, w, m, v in zip(names, parts, full, moms, vars_):
        C = w.shape[-1]
        res = _adamw(f"adamw_{nm}", p.reshape(N_DEV, -1, C), w.reshape(-1, C), m.reshape(-1, C),
                     v.reshape(-1, C))
        big[nm] = [r.reshape(w.shape) for r in res]

    zrow = jnp.zeros((D,), F32)
    g_small = _pack_small(jnp.concatenate([dg_mix0, dg_mix1]), jnp.concatenate([dg_ffn0, dg_ffn1]),
                          dg_kv, dg_fin, dg_kva, dg_qa, db_f, zrow.at[0].set(loss_vec[0, 0]))
    tot_small = _allreduce_small(g_small)
    w_small = _pack_small(norm_mix_g, norm_ffn_g, kv_norm_g, final_norm_g, mla_kv_a_norm_g,
                          mla_q_a_norm_g, fox_b_f, zrow)
    m_small = _pack_small(m_norm_mix_g, m_norm_ffn_g, m_kv_norm_g, m_final_norm_g, m_mla_kv_a_norm_g,
                          m_mla_q_a_norm_g, m_fox_b_f, zrow)
    v_small = _pack_small(v_norm_mix_g, v_norm_ffn_g, v_kv_norm_g, v_final_norm_g, v_mla_kv_a_norm_g,
                          v_mla_q_a_norm_g, v_fox_b_f, zrow)
    small = _adamw("adamw_small", tot_small[None], w_small, m_small, v_small)
    loss = tot_small[7, 0]
    small = [_unpack_small(s) for s in small]

    def ordered(i):
        mix, ffn, bf, kv, kva, qa, fin = small[i]
        return [mix, ffn, big["fox_w_in"][i], bf, big["fox_w_out"][i], kv, big["mla_w_kv_a"][i], kva,
                big["mla_w_kv_b"][i], big["mla_w_q_a"][i], qa, big["mla_w_q_b"][i],
                big["mla_w_out"][i], big["ffn_w_up"][i], big["ffn_w_down"][i], fin]

    return (loss, grad_x[None], *ordered(0), *ordered(1), *ordered(2), *ordered(3))
```

```python
import functools
import math

import jax
import jax.numpy as jnp
from jax import lax
from jax.experimental import pallas as pl
from jax.experimental.pallas import tpu as pltpu

F32 = jnp.float32
BF16 = jnp.bfloat16
MESH = pl.DeviceIdType.MESH

N_DEV = 8
D_MODEL = 1024
FOX_HEADS = 16
FOX_HEAD_DIM = 64
FOX_AUG = 128
MLA_AUG = 256
MLA_HEADS = 8
QK_NOPE = 128
QK_ROPE = 64
V_HEAD = 128
Q_LORA = 384
KV_LORA = 256
KV_A_PAD = 384
D_FF = 4096
ROPE_BASE = 10000.0
EPS = 1e-6
NEG = -1e30

ADAM_LR = 0.001
ADAM_B1 = 0.9
ADAM_B2 = 0.999
ADAM_EPS = 1e-08
ADAM_WD = 0.01
ADAM_STEP = 10

VMEM_LIMIT_BYTES = 56 * 1024 * 1024

NN = (((1,), (0,)), ((), ()))
NT = (((1,), (1,)), ((), ()))
TN = (((0,), (0,)), ((), ()))
_FORMS = {"nn": NN, "nt": NT}


def _cparams(sem=None):
    return pltpu.CompilerParams(dimension_semantics=sem, vmem_limit_bytes=VMEM_LIMIT_BYTES)


def _pick(n, cands):
    for c in cands:
        if c <= n and n % c == 0:
            return c
    return n


def _dot(a, b, dims):
    return lax.dot_general(a, b, dims, preferred_element_type=F32)


def _mm(name, a, b, form, out_dtypes, epi=None, extras=(), tm=1024, tn=None):
    M, K = a.shape
    N = b.shape[1] if form == "nn" else b.shape[0]
    tm = _pick(M, (tm, 512, 256, 128))
    tn = _pick(N, (tn or (1024 if K <= 1024 else 512), 512, 384, 256, 128))
    n_ex = len(extras)
    n_out = len(out_dtypes)
    cast_once = a.dtype != BF16

    def body(*refs):
        a_ref, b_ref = refs[0], refs[1]
        ex = refs[2:2 + n_ex]
        outs = refs[2 + n_ex:2 + n_ex + n_out]
        if cast_once:
            a_sc = refs[2 + n_ex + n_out]

            @pl.when(pl.program_id(1) == 0)
            def _():
                a_sc[...] = a_ref[...].astype(BF16)

            av = a_sc[...]
        else:
            av = a_ref[...]
        acc = _dot(av, b_ref[...].astype(BF16), _FORMS[form])
        res = epi(acc, *[e[...] for e in ex]) if epi is not None else (acc,)
        for o_ref, r in zip(outs, res):
            o_ref[...] = r.astype(o_ref.dtype)

    if form == "nn":
        b_spec = pl.BlockSpec((K, tn), lambda i, j: (0, j))
    else:
        b_spec = pl.BlockSpec((tn, K), lambda i, j: (j, 0))
    tile = pl.BlockSpec((tm, tn), lambda i, j: (i, j))
    out = pl.pallas_call(
        body, name=name, grid=(M // tm, N // tn),
        in_specs=[pl.BlockSpec((tm, K), lambda i, j: (i, 0)), b_spec] + [tile] * n_ex,
        out_specs=[tile] * n_out,
        out_shape=[jax.ShapeDtypeStruct((M, N), dt) for dt in out_dtypes],
        scratch_shapes=[pltpu.VMEM((tm, K), BF16)] if cast_once else [],
        compiler_params=_cparams(("parallel", "arbitrary")),
    )(a, b, *extras)
    return out if n_out > 1 else out[0]


def _mm_tn(name, a, b):
    T, Ka = a.shape
    N = b.shape[1]
    tk = _pick(Ka, (1024, 512, 384, 256, 128))
    tn = _pick(N, (1024, 768, 512, 384, 256, 128))
    tt = _pick(T, (1024, 512, 256, 128))

    def body(a_ref, b_ref, o_ref):
        @pl.when(pl.program_id(2) == 0)
        def _():
            o_ref[...] = jnp.zeros_like(o_ref)

        o_ref[...] += _dot(a_ref[...].astype(BF16), b_ref[...].astype(BF16), TN)

    return pl.pallas_call(
        body, name=name, grid=(Ka // tk, N // tn, T // tt),
        in_specs=[pl.BlockSpec((tt, tk), lambda i, j, t: (t, i)),
                  pl.BlockSpec((tt, tn), lambda i, j, t: (t, j))],
        out_specs=pl.BlockSpec((tk, tn), lambda i, j, t: (i, j)),
        out_shape=jax.ShapeDtypeStruct((Ka, N), F32),
        compiler_params=_cparams(("parallel", "parallel", "arbitrary")),
    )(a, b)


def _mm_heads(name, a, w, out_dtype):
    T, K = a.shape
    H, _, N = w.shape
    tm = _pick(T, (1024, 512, 256, 128))

    def body(a_ref, w_ref, o_ref):
        av = a_ref[...].astype(BF16)
        for s in range(H):
            o_ref[s] = _dot(av, w_ref[s].astype(BF16), NN).astype(o_ref.dtype)

    return pl.pallas_call(
        body, name=name, grid=(T // tm,),
        in_specs=[pl.BlockSpec((tm, K), lambda i: (i, 0)), pl.BlockSpec((H, K, N), lambda i: (0, 0, 0))],
        out_specs=pl.BlockSpec((H, tm, N), lambda i: (0, i, 0)),
        out_shape=jax.ShapeDtypeStruct((H, T, N), out_dtype),
        compiler_params=_cparams(("parallel",)),
    )(a, w)


def _mm_head_slabs(name, a, w, tails, head_dim, out_dtype, heads_per_step=8):
    T, K = a.shape
    S, _, tail = tails.shape
    hb = heads_per_step
    tm = _pick(T, (1024, 512, 256, 128))

    def body(a_ref, w_ref, t_ref, o_ref):
        acc = _dot(a_ref[...].astype(BF16), w_ref[...].astype(BF16), NN)
        for s in range(hb):
            slab = jnp.concatenate([acc[:, s * head_dim:(s + 1) * head_dim],
                                    jnp.broadcast_to(t_ref[s], (tm, tail))], axis=-1)
            o_ref[s] = slab.astype(o_ref.dtype)

    return pl.pallas_call(
        body, name=name, grid=(T // tm, S // hb),
        in_specs=[pl.BlockSpec((tm, K), lambda i, j: (i, 0)),
                  pl.BlockSpec((K, hb * head_dim), lambda i, j: (0, j)),
                  pl.BlockSpec((hb, 1, tail), lambda i, j: (j, 0, 0))],
        out_specs=pl.BlockSpec((hb, tm, head_dim + tail), lambda i, j: (j, i, 0)),
        out_shape=jax.ShapeDtypeStruct((S, T, head_dim + tail), out_dtype),
        compiler_params=_cparams(("parallel", "arbitrary")),
    )(a, w, tails)


def _mm_heads_dw(name, a, g):
    T, K = a.shape
    H, _, N = g.shape
    tt = _pick(T, (1024, 512, 256, 128))

    def body(a_ref, g_ref, o_ref):
        @pl.when(pl.program_id(0) == 0)
        def _():
            o_ref[...] = jnp.zeros_like(o_ref)

        av = a_ref[...].astype(BF16)
        for s in range(H):
            o_ref[s] += _dot(av, g_ref[s].astype(BF16), TN)

    return pl.pallas_call(
        body, name=name, grid=(T // tt,),
        in_specs=[pl.BlockSpec((tt, K), lambda t: (t, 0)), pl.BlockSpec((H, tt, N), lambda t: (0, t, 0))],
        out_specs=pl.BlockSpec((H, K, N), lambda t: (0, 0, 0)),
        out_shape=jax.ShapeDtypeStruct((H, K, N), F32),
        compiler_params=_cparams(("arbitrary",)),
    )(a, g)


def _mm_heads_dx(name, g, w, add=None):
    H, T, N = g.shape
    K = w.shape[1]
    tm = _pick(T, (512, 256, 128))
    has_add = add is not None

    def body(*refs):
        g_ref, w_ref, o_ref = refs[0], refs[1], refs[-1]
        acc = _dot(g_ref[0].astype(BF16), w_ref[0].astype(BF16), NT)
        for s in range(1, H):
            acc = acc + _dot(g_ref[s].astype(BF16), w_ref[s].astype(BF16), NT)
        if has_add:
            acc = acc + refs[2][...]
        o_ref[...] = acc

    rows = pl.BlockSpec((tm, K), lambda i: (i, 0))
    return pl.pallas_call(
        body, name=name, grid=(T // tm,),
        in_specs=[pl.BlockSpec((H, tm, N), lambda i: (0, i, 0)), pl.BlockSpec((H, K, N), lambda i: (0, 0, 0))]
        + ([rows] if has_add else []),
        out_specs=rows,
        out_shape=jax.ShapeDtypeStruct((T, K), F32),
        compiler_params=_cparams(("parallel",)),
    )(g, w, *([add] if has_add else []))


def _rms(name, x, g, out_dtype):
    T, D = x.shape
    tm = _pick(T, (1024, 512, 256, 128))

    def body(x_ref, g_ref, o_ref):
        xf = x_ref[...]
        r = lax.rsqrt(jnp.mean(xf * xf, axis=-1, keepdims=True) + EPS)
        o_ref[...] = (xf * r * g_ref[...]).astype(o_ref.dtype)

    return pl.pallas_call(
        body, name=name, grid=(T // tm,),
        in_specs=[pl.BlockSpec((tm, D), lambda i: (i, 0)), pl.BlockSpec((1, D), lambda i: (0, 0))],
        out_specs=pl.BlockSpec((tm, D), lambda i: (i, 0)),
        out_shape=jax.ShapeDtypeStruct((T, D), out_dtype),
        compiler_params=_cparams(("parallel",)),
    )(x, g)


def _rms_bwd(name, x, g, dh, dres=None):
    T, D = x.shape
    tm = _pick(T, (512, 256, 128))
    has_res = dres is not None

    def body(*refs):
        if has_res:
            x_ref, g_ref, dh_ref, dres_ref, dx_ref, dg_ref = refs
        else:
            x_ref, g_ref, dh_ref, dx_ref, dg_ref = refs

        @pl.when(pl.program_id(0) == 0)
        def _():
            dg_ref[...] = jnp.zeros_like(dg_ref)

        xf = x_ref[...]
        r = lax.rsqrt(jnp.mean(xf * xf, axis=-1, keepdims=True) + EPS)
        xhat = xf * r
        dy = dh_ref[...].astype(F32)
        dxh = dy * g_ref[...]
        dx = r * (dxh - xhat * jnp.mean(dxh * xhat, axis=-1, keepdims=True))
        if has_res:
            dx = dx + dres_ref[...]
        dx_ref[...] = dx
        dg_ref[...] += jnp.sum(dy * xhat, axis=0, keepdims=True)

    row = pl.BlockSpec((tm, D), lambda i: (i, 0))
    vec = pl.BlockSpec((1, D), lambda i: (0, 0))
    ins = [x, g, dh] + ([dres] if has_res else [])
    return pl.pallas_call(
        body, name=name, grid=(T // tm,),
        in_specs=[row, vec, row] + ([row] if has_res else []),
        out_specs=[row, vec],
        out_shape=[jax.ShapeDtypeStruct((T, D), F32), jax.ShapeDtypeStruct((1, D), F32)],
        compiler_params=_cparams(("arbitrary",)),
    )(*ins)


def _loss_head(name, x, g, tgt):
    T, D = x.shape
    tm = _pick(T, (512, 256, 128))

    def body(x_ref, g_ref, t_ref, dx_ref, dg_ref, loss_ref):
        @pl.when(pl.program_id(0) == 0)
        def _():
            dg_ref[...] = jnp.zeros_like(dg_ref)
            loss_ref[...] = jnp.zeros_like(loss_ref)

        xf = x_ref[...]
        r = lax.rsqrt(jnp.mean(xf * xf, axis=-1, keepdims=True) + EPS)
        xhat = xf * r
        gv = g_ref[...]
        err = xhat * gv - t_ref[...]
        row_loss = jnp.mean(err * err, axis=-1, keepdims=True)
        loss_ref[...] += 0.5 * jnp.sum(row_loss, axis=0, keepdims=True)
        dy = err * (1.0 / D)
        dxh = dy * gv
        dx_ref[...] = r * (dxh - xhat * jnp.mean(dxh * xhat, axis=-1, keepdims=True))
        dg_ref[...] += jnp.sum(dy * xhat, axis=0, keepdims=True)

    row = pl.BlockSpec((tm, D), lambda i: (i, 0))
    vec = pl.BlockSpec((1, D), lambda i: (0, 0))
    return pl.pallas_call(
        body, name=name, grid=(T // tm,),
        in_specs=[row, vec, row],
        out_specs=[row, vec, pl.BlockSpec((1, 128), lambda i: (0, 0))],
        out_shape=[jax.ShapeDtypeStruct((T, D), F32), jax.ShapeDtypeStruct((1, D), F32),
                   jax.ShapeDtypeStruct((1, 128), F32)],
        compiler_params=_cparams(("arbitrary",)),
    )(x, g, tgt)


def _swap_halves(t):
    half = t.shape[-1] // 2
    return jnp.concatenate([t[:, half:], t[:, :half]], axis=-1)


def _rope(name, t, cos2, sgn_sin, out_dtype):
    H, T, R = t.shape
    tm = _pick(T, (1024, 512, 256, 128))

    def body(t_ref, c_ref, s_ref, o_ref):
        tf = t_ref[...].astype(F32)
        o_ref[...] = (tf * c_ref[...] + _swap_halves(tf) * s_ref[...]).astype(o_ref.dtype)

    slab = pl.BlockSpec((None, tm, R), lambda h, i: (h, i, 0))
    tab = pl.BlockSpec((tm, R), lambda h, i: (i, 0))
    return pl.pallas_call(
        body, name=name, grid=(H, T // tm),
        in_specs=[slab, tab, tab], out_specs=slab,
        out_shape=jax.ShapeDtypeStruct((H, T, R), out_dtype),
        compiler_params=_cparams(("parallel", "parallel")),
    )(t, cos2, sgn_sin)


def _mla_q_prep(name, qf, cos2, sgn_sin, scale):
    H, T, W = qf.shape
    R = cos2.shape[1]
    tm = _pick(T, (1024, 512, 256, 128))

    def body(t_ref, c_ref, s_ref, o_ref):
        tf = t_ref[...]
        r = tf[:, W - R:]
        roped = r * c_ref[...] + _swap_halves(r) * s_ref[...]
        o_ref[...] = (jnp.concatenate([tf[:, :W - R], roped], axis=-1) * scale).astype(o_ref.dtype)

    slab = pl.BlockSpec((None, tm, W), lambda h, i: (h, i, 0))
    tab = pl.BlockSpec((tm, R), lambda h, i: (i, 0))
    return pl.pallas_call(
        body, name=name, grid=(H, T // tm),
        in_specs=[slab, tab, tab], out_specs=slab,
        out_shape=jax.ShapeDtypeStruct((H, T, W), BF16),
        compiler_params=_cparams(("parallel", "parallel")),
    )(qf, cos2, sgn_sin)


def _rope_bwd(name, dy, cos2, sgn_sin, sum_heads):
    H, T, R = dy.shape
    tm = _pick(T, (1024, 512, 256, 128))

    def body(d_ref, c_ref, s_ref, o_ref):
        d = d_ref[...]
        if sum_heads:
            tot = d[0]
            for h in range(1, H):
                tot = tot + d[h]
            d = tot
        o_ref[...] = d * c_ref[...] + _swap_halves(d * s_ref[...])

    if sum_heads:
        grid = (T // tm,)
        in_slab = pl.BlockSpec((H, tm, R), lambda i: (0, i, 0))
        out_slab = pl.BlockSpec((tm, R), lambda i: (i, 0))
        tab = pl.BlockSpec((tm, R), lambda i: (i, 0))
        out_shape = jax.ShapeDtypeStruct((T, R), F32)
        sem = ("parallel",)
    else:
        grid = (H, T // tm)
        in_slab = pl.BlockSpec((None, tm, R), lambda h, i: (h, i, 0))
        out_slab = in_slab
        tab = pl.BlockSpec((tm, R), lambda h, i: (i, 0))
        out_shape = jax.ShapeDtypeStruct((H, T, R), F32)
        sem = ("parallel", "parallel")
    return pl.pallas_call(
        body, name=name, grid=grid, in_specs=[in_slab, tab, tab], out_specs=out_slab,
        out_shape=out_shape, compiler_params=_cparams(sem),
    )(dy, cos2, sgn_sin)


def _log_sigmoid(z):
    return jnp.minimum(z, 0.0) - jnp.log(1.0 + jnp.exp(-jnp.abs(z)))


def _gate_cumsum(name, fl, b, tb):
    H, T = fl.shape

    def body(f_ref, b_ref, c_ref, carry):
        @pl.when(pl.program_id(0) == 0)
        def _():
            carry[...] = jnp.zeros_like(carry)

        ls = _log_sigmoid(f_ref[...] + b_ref[...])
        src = lax.broadcasted_iota(jnp.int32, (tb, tb), 0)
        dst = lax.broadcasted_iota(jnp.int32, (tb, tb), 1)
        tri = (src <= dst).astype(F32)
        c = lax.dot_general(ls, tri, NN, precision=lax.Precision.HIGHEST,
                            preferred_element_type=F32) + carry[...]
        c_ref[...] = c
        carry[...] = carry[...] + jnp.sum(ls, axis=-1, keepdims=True)

    return pl.pallas_call(
        body, name=name, grid=(T // tb,),
        in_specs=[pl.BlockSpec((H, tb), lambda i: (0, i)), pl.BlockSpec((H, 1), lambda i: (0, 0))],
        out_specs=pl.BlockSpec((H, tb), lambda i: (0, i)),
        out_shape=jax.ShapeDtypeStruct((H, T), F32),
        scratch_shapes=[pltpu.VMEM((H, 1), F32)],
        compiler_params=_cparams(("arbitrary",)),
    )(fl, b)


def _gate_cumsum_bwd(name, d_query, d_key, fl, b, tb):
    H, T = fl.shape
    nb = T // tb

    def body(dq_ref, dk_ref, f_ref, b_ref, dfl_ref, db_ref, carry):
        @pl.when(pl.program_id(0) == 0)
        def _():
            carry[...] = jnp.zeros_like(carry)
            db_ref[...] = jnp.zeros_like(db_ref)

        d = dq_ref[...] - dk_ref[...]
        src = lax.broadcasted_iota(jnp.int32, (tb, tb), 0)
        dst = lax.broadcasted_iota(jnp.int32, (tb, tb), 1)
        tri = (src >= dst).astype(F32)
        dls = lax.dot_general(d, tri, NN, precision=lax.Precision.HIGHEST,
                              preferred_element_type=F32) + carry[...]
        z = f_ref[...] + b_ref[...]
        dfl = dls * (1.0 / (1.0 + jnp.exp(z)))
        dfl_ref[...] = dfl
        db_ref[...] += jnp.sum(dfl, axis=-1, keepdims=True)
        carry[...] = carry[...] + jnp.sum(d, axis=-1, keepdims=True)

    blk = pl.BlockSpec((H, tb), lambda i: (0, nb - 1 - i))
    vec = pl.BlockSpec((H, 1), lambda i: (0, 0))
    return pl.pallas_call(
        body, name=name, grid=(nb,),
        in_specs=[blk, blk, blk, vec], out_specs=[blk, vec],
        out_shape=[jax.ShapeDtypeStruct((H, T), F32), jax.ShapeDtypeStruct((H, 1), F32)],
        scratch_shapes=[pltpu.VMEM((H, 1), F32)],
        compiler_params=_cparams(("arbitrary",)),
    )(d_query, d_key, fl, b)


def _causal_mask(tq, rows_are_queries):
    r = lax.broadcasted_iota(jnp.int32, (tq, tq), 0)
    c = lax.broadcasted_iota(jnp.int32, (tq, tq), 1)
    return (c <= r) if rows_are_queries else (r <= c)


def _chunk_rows(j, tq):
    return pl.ds(pl.multiple_of(j * tq, tq), tq)


def _column_as_row(col):
    return jnp.broadcast_to(col, (col.shape[0], 128)).T[:1, :]


def _flash_fwd(name, q, k, v_aug, dv, tq, exchange=None, q_head0=0, v_head0=0):
    H, T, dqk = k.shape
    dva = v_aug.shape[2]
    nq = T // tq

    def body(q_ref, k_ref, v_ref, o_ref, lse_ref, m_sc, acc_sc):
        qi = pl.program_id(1)
        m_sc[...] = jnp.full_like(m_sc, NEG)
        acc_sc[...] = jnp.zeros_like(acc_sc)

        def chunk(j, masked):
            rows = _chunk_rows(j, tq)
            s = _dot(q_ref[...], k_ref[rows, :], NT)
            if masked:
                s = jnp.where(_causal_mask(tq, True), s, NEG)
            m_prev = m_sc[...]
            m_new = jnp.maximum(m_prev, jnp.max(s, axis=1, keepdims=True))
            p = jnp.exp(s - jnp.tile(m_new, (1, tq // 128)))
            alpha = jnp.tile(jnp.exp(m_prev - m_new), (1, dva // 128))
            acc_sc[...] = alpha * acc_sc[...] + _dot(p.astype(BF16), v_ref[rows, :], NN)
            m_sc[...] = m_new

        def off_diagonal(j, carry):
            chunk(j, False)
            return carry

        lax.fori_loop(0, qi, off_diagonal, 0)
        chunk(qi, True)
        acc = acc_sc[...]
        l = acc[:, dv:dv + 1]
        o_ref[...] = acc[:, :dv] / l
        lse_ref[...] = _column_as_row(m_sc[:, :1] + jnp.log(l))

    (o, lse_rows), exchanged = _call_carrying(
        body, name, (H, nq),
        in_specs=[pl.BlockSpec((None, tq, dqk), lambda h, i: (h + q_head0, i, 0)),
                  pl.BlockSpec((None, T, dqk), lambda h, i: (h, 0, 0)),
                  pl.BlockSpec((None, T, dva), lambda h, i: (h + v_head0, 0, 0))],
        out_specs=[pl.BlockSpec((None, tq, dv), lambda h, i: (h, i, 0)),
                   pl.BlockSpec((None, 1, tq), lambda h, i: (h, 0, i))],
        out_shape=[jax.ShapeDtypeStruct((H, T, dv), F32), jax.ShapeDtypeStruct((H, 1, T), F32)],
        scratch_shapes=[pltpu.VMEM((tq, 128), F32), pltpu.VMEM((tq, dva), F32)],
        operands=(q, k, v_aug), exchange=exchange)
    return (o, lse_rows.reshape(H, T)), exchanged


def _row_dot(name, a, b):
    H, T, d = a.shape
    tm = _pick(T, (1024, 512, 256, 128))

    def body(a_ref, b_ref, o_ref):
        col = jnp.sum(a_ref[...].astype(F32) * b_ref[...].astype(F32), axis=-1, keepdims=True)
        o_ref[...] = _column_as_row(col)

    slab = pl.BlockSpec((None, tm, d), lambda h, i: (h, i, 0))
    return pl.pallas_call(
        body, name=name, grid=(H, T // tm), in_specs=[slab, slab],
        out_specs=pl.BlockSpec((None, 1, tm), lambda h, i: (h, 0, i)),
        out_shape=jax.ShapeDtypeStruct((H, 1, T), F32),
        compiler_params=_cparams(("parallel", "parallel")),
    )(a, b).reshape(H, T)


def _flash_bwd(name, q, k, v, do, scale, tq, exchange=None, v_head0=0):
    H, T, dqk = q.shape
    dva = v.shape[2]
    nq = T // tq

    def body(q_ref, k_ref, v_ref, do_ref, dq_ref, dk_ref, dv_ref, dk_sc, dv_sc):
        ki = pl.program_id(1)
        dk_sc[...] = jnp.zeros_like(dk_sc)
        dv_sc[...] = jnp.zeros_like(dv_sc)

        @pl.when(ki == 0)
        def _():
            dq_ref[...] = jnp.zeros_like(dq_ref)

        def chunk(i, masked):
            rows = _chunk_rows(i, tq)
            qb = q_ref[rows, :]
            dob = do_ref[rows, :]
            kb = k_ref[...]
            st = _dot(kb, qb, NT)
            if masked:
                st = jnp.where(_causal_mask(tq, False), st, NEG)
            pt = jnp.exp(st)
            dv_sc[...] += _dot(pt.astype(BF16), dob, NN)
            dst = (pt * _dot(v_ref[...], dob, NT)).astype(BF16)
            dk_sc[...] += _dot(dst, qb, NN)
            dq_ref[rows, :] += _dot(dst, kb, TN)

        def off_diagonal(i, carry):
            chunk(i, False)
            return carry

        chunk(ki, True)
        lax.fori_loop(ki + 1, nq, off_diagonal, 0)
        dk_ref[...] = dk_sc[...]
        dv_ref[...] = dv_sc[...]

        @pl.when(ki == nq - 1)
        def _():
            dq_ref[...] = dq_ref[...] * scale

    whole_q = pl.BlockSpec((None, T, dqk), lambda h, j: (h, 0, 0))
    k_spec = pl.BlockSpec((None, tq, dqk), lambda h, j: (h, j, 0))
    v_spec = pl.BlockSpec((None, tq, dva), lambda h, j: (h, j, 0))
    v_in_spec = pl.BlockSpec((None, tq, dva), lambda h, j: (h + v_head0, j, 0))
    return _call_carrying(
        body, name, (H, nq),
        in_specs=[whole_q, k_spec, v_in_spec, pl.BlockSpec((None, T, dva), lambda h, j: (h, 0, 0))],
        out_specs=[whole_q, k_spec, v_spec],
        out_shape=[jax.ShapeDtypeStruct((H, T, dqk), F32), jax.ShapeDtypeStruct((H, T, dqk), F32),
                   jax.ShapeDtypeStruct((H, T, dva), F32)],
        scratch_shapes=[pltpu.VMEM((tq, dqk), F32), pltpu.VMEM((tq, dva), F32)],
        operands=(q, k, v, do), exchange=exchange)


def _adamw_math(w, g, m, v):
    m = ADAM_B1 * m + (1.0 - ADAM_B1) * g
    v = ADAM_B2 * v + (1.0 - ADAM_B2) * (g * g)
    m_hat = m / (1.0 - ADAM_B1 ** ADAM_STEP)
    v_hat = v / (1.0 - ADAM_B2 ** ADAM_STEP)
    delta = -ADAM_LR * (m_hat / (jnp.sqrt(v_hat) + ADAM_EPS) + ADAM_WD * w)
    return delta, m, v


def _adamw(name, parts, w, m, v):
    P, R, C = parts.shape
    tr = _pick(R, (256, 128, 64, 32, 16, 8))

    def body(p_ref, w_ref, m_ref, v_ref, g_out, d_out, m_out, v_out):
        g = p_ref[0].astype(F32)
        for i in range(1, P):
            g = g + p_ref[i].astype(F32)
        delta, m_new, v_new = _adamw_math(w_ref[...], g, m_ref[...], v_ref[...])
        g_out[...] = g
        d_out[...] = delta
        m_out[...] = m_new
        v_out[...] = v_new

    blk = pl.BlockSpec((tr, C), lambda i: (i, 0))
    sds = jax.ShapeDtypeStruct((R, C), F32)
    return pl.pallas_call(
        body, name=name, grid=(R // tr,),
        in_specs=[pl.BlockSpec((P, tr, C), lambda i: (0, i, 0)), blk, blk, blk],
        out_specs=[blk] * 4, out_shape=[sds] * 4,
        compiler_params=_cparams(("parallel",)),
    )(parts, w, m, v)


def _my_position():
    return lax.axis_index("x"), lax.axis_index("y"), lax.axis_index("c")


def _slot(p):
    return 4 * p[0] + 2 * p[1] + p[2]


def _flip(p, k):
    return tuple((1 - p[i]) if (k >> (2 - i)) & 1 else p[i] for i in range(3))


def _allgather_weights(shards):
    n = len(shards)

    def body(*refs):
        ins = refs[:n]
        outs = refs[n:2 * n]
        send_sems, recv_sems, local_sems = refs[2 * n:]
        x, y, c = _my_position()
        me, sibling = (x, y, c), (x, y, 1 - c)
        chips = [(1 - x, y), (x, 1 - y), (1 - x, 1 - y)]

        def copy(a, k, block, to, src=None):
            dst = outs[a].at[_slot(block)]
            return pltpu.make_async_remote_copy(
                src_ref=dst if src is None else src, dst_ref=dst,
                send_sem=send_sems.at[7 * a + k], recv_sem=recv_sems.at[7 * a + k],
                device_id=to, device_id_type=MESH)

        started = []
        for a in range(n):
            mine = pltpu.make_async_copy(ins[a], outs[a].at[_slot(me)], local_sems.at[a])
            mine.start()
            started.append(mine)
        first = []
        for a in range(n):
            first.append(copy(a, 0, me, sibling, src=ins[a]))
            first += [copy(a, 1 + j, me, (*chip, c), src=ins[a]) for j, chip in enumerate(chips)]
        for cp in first:
            cp.start()
        passed = []
        for j, chip in enumerate(chips):
            for a in range(n):
                copy(a, 1 + j, (*chip, c), me).wait_recv()
                fwd = copy(a, 4 + j, (*chip, c), sibling)
                fwd.start()
                passed.append(fwd)
        for a in range(n):
            copy(a, 0, sibling, me).wait_recv()
            for j, chip in enumerate(chips):
                copy(a, 4 + j, (*chip, 1 - c), me).wait_recv()
        for cp in first + passed:
            cp.wait_send()
        for mine in started:
            mine.wait()

    hbm = pl.BlockSpec(memory_space=pl.ANY)
    return pl.pallas_call(
        body, name="allgather_weights",
        in_specs=[hbm] * n, out_specs=[hbm] * n,
        out_shape=[jax.ShapeDtypeStruct((N_DEV,) + s.shape, s.dtype) for s in shards],
        scratch_shapes=[pltpu.SemaphoreType.DMA((7 * n,)), pltpu.SemaphoreType.DMA((7 * n,)),
                        pltpu.SemaphoreType.DMA((n,))],
        compiler_params=pltpu.CompilerParams(has_side_effects=True),
    )(*shards)


def _exchange_copies(kind, x_in, x_out, send_sems, recv_sems, local_sems, receives=True):
    me = _my_position()
    mine = _slot(me)
    local, sends, recvs = [], [], []
    for a in range(len(x_in)):
        src = x_in[a] if kind == "gather" else x_in[a].at[mine]
        local.append(pltpu.make_async_copy(src, x_out[a].at[mine], local_sems.at[a]))
    for k in range(1, N_DEV):
        peer = _flip(me, k)
        theirs = _slot(peer)
        for a in range(len(x_in)):
            src = x_in[a] if kind == "gather" else x_in[a].at[theirs]
            ends = [(x_out[a].at[mine], sends)] + ([(x_out[a].at[theirs], recvs)] if receives else [])
            for dst, group in ends:
                group.append(pltpu.make_async_remote_copy(
                    src_ref=src, dst_ref=dst, send_sem=send_sems.at[7 * a + k - 1],
                    recv_sem=recv_sems.at[7 * a + k - 1], device_id=peer, device_id_type=MESH))
    return local, sends, recvs


def _exchange_out_shapes(kind, arrays):
    return [jax.ShapeDtypeStruct(((N_DEV,) + a.shape) if kind == "gather" else a.shape, a.dtype)
            for a in arrays]


def _exchange_sems(n):
    return [pltpu.SemaphoreType.DMA((7 * n,)), pltpu.SemaphoreType.DMA((7 * n,)),
            pltpu.SemaphoreType.DMA((n,))]


def _alltoall_grads(name, grads):
    n = len(grads)

    def body(*refs):
        local, sends, recvs = _exchange_copies("scatter", refs[:n], refs[n:2 * n], *refs[2 * n:])
        for cp in local + sends:
            cp.start()
        for cp in recvs:
            cp.wait_recv()
        for cp in sends:
            cp.wait_send()
        for cp in local:
            cp.wait()

    hbm = pl.BlockSpec(memory_space=pl.ANY)
    return pl.pallas_call(
        body, name=name,
        in_specs=[hbm] * n, out_specs=[hbm] * n,
        out_shape=_exchange_out_shapes("scatter", grads), scratch_shapes=_exchange_sems(n),
        compiler_params=pltpu.CompilerParams(has_side_effects=True),
    )(*grads)


def _call_carrying(body, name, grid, in_specs, out_specs, out_shape, scratch_shapes, operands, exchange):
    if exchange is None:
        out = pl.pallas_call(
            body, name=name, grid=grid, in_specs=in_specs, out_specs=out_specs, out_shape=out_shape,
            scratch_shapes=scratch_shapes, compiler_params=_cparams(("parallel",) + ("arbitrary",) * (len(grid) - 1)),
        )(*operands)
        return out, None
    kind, arrays = exchange
    n, n_in, n_out, n_sc = len(arrays), len(in_specs), len(out_specs), len(scratch_shapes)

    def full_body(*refs):
        ins, refs = refs[:n_in], refs[n_in:]
        x_in, refs = refs[:n], refs[n:]
        outs, refs = refs[:n_out], refs[n_out:]
        x_out, refs = refs[:n], refs[n:]
        scratch, sems = refs[:n_sc], refs[n_sc:]
        first = last = None
        for axis, size in enumerate(grid):
            at_start = pl.program_id(axis) == 0
            at_end = pl.program_id(axis) == size - 1
            first = at_start if first is None else jnp.logical_and(first, at_start)
            last = at_end if last is None else jnp.logical_and(last, at_end)

        @pl.when(first)
        def _():
            local, sends, _ = _exchange_copies(kind, x_in, x_out, *sems, receives=False)
            for cp in local + sends:
                cp.start()

        body(*ins, *outs, *scratch)

        @pl.when(last)
        def _():
            local, sends, recvs = _exchange_copies(kind, x_in, x_out, *sems)
            for cp in recvs:
                cp.wait_recv()
            for cp in sends:
                cp.wait_send()
            for cp in local:
                cp.wait()

    hbm = pl.BlockSpec(memory_space=pl.ANY)
    out = pl.pallas_call(
        full_body, name=name, grid=grid,
        in_specs=list(in_specs) + [hbm] * n, out_specs=list(out_specs) + [hbm] * n,
        out_shape=list(out_shape) + _exchange_out_shapes(kind, arrays),
        scratch_shapes=list(scratch_shapes) + _exchange_sems(n),
        compiler_params=pltpu.CompilerParams(dimension_semantics=("arbitrary",) * len(grid),
                                             vmem_limit_bytes=VMEM_LIMIT_BYTES, has_side_effects=True),
    )(*operands, *arrays)
    return out[:n_out], out[n_out:]


def _allreduce_small(v):
    R, C = v.shape

    def body(v_ref, o_ref, buf, send_sems, recv_sems):
        me = _my_position()
        buf[_slot(me)] = v_ref[...]
        sends = []
        for k in range(1, N_DEV):
            peer = _flip(me, k)
            cp = pltpu.make_async_remote_copy(
                src_ref=v_ref, dst_ref=buf.at[_slot(me)],
                send_sem=send_sems.at[k - 1], recv_sem=recv_sems.at[k - 1],
                device_id=peer, device_id_type=MESH)
            cp.start()
            sends.append(cp)
        for k in range(1, N_DEV):
            peer = _flip(me, k)
            pltpu.make_async_remote_copy(
                src_ref=v_ref, dst_ref=buf.at[_slot(peer)],
                send_sem=send_sems.at[k - 1], recv_sem=recv_sems.at[k - 1],
                device_id=peer, device_id_type=MESH).wait_recv()
        for cp in sends:
            cp.wait_send()
        tot = buf[0]
        for s in range(1, N_DEV):
            tot = tot + buf[s]
        o_ref[...] = tot

    vm = pl.BlockSpec(memory_space=pltpu.VMEM)
    return pl.pallas_call(
        body, name="allreduce_small",
        in_specs=[vm], out_specs=vm, out_shape=jax.ShapeDtypeStruct((R, C), F32),
        scratch_shapes=[pltpu.VMEM((N_DEV, R, C), F32), pltpu.SemaphoreType.DMA((7,)),
                        pltpu.SemaphoreType.DMA((7,))],
        compiler_params=pltpu.CompilerParams(has_side_effects=True),
    )(v)


def _to_heads(t, heads):
    T = t.shape[0]
    return t.reshape(T, heads, t.shape[1] // heads).transpose(1, 0, 2)


def _from_heads(t):
    H, T, d = t.shape
    return t.transpose(1, 0, 2).reshape(T, H * d)


def _widen(t, width, ones_at=None, pieces_at=None, pieces=None):
    out = jnp.pad(t, ((0, 0), (0, 0), (0, width - t.shape[-1])))
    lane = lax.broadcasted_iota(jnp.int32, (1, 1, width), 2)
    if ones_at is not None:
        out = jnp.where((lane >= ones_at) & (lane < ones_at + 3), jnp.ones((), BF16), out)
    if pieces_at is not None:
        for i in range(3):
            out = jnp.where(lane == pieces_at + i, pieces[i][:, :, None], out)
    return out


def _split3(t):
    hi = lax.reduce_precision(t, 8, 7)
    r = t - hi
    mid = lax.reduce_precision(r, 8, 7)
    lo = lax.reduce_precision(r - mid, 8, 7)
    return hi.astype(BF16), mid.astype(BF16), lo.astype(BF16)


def _pad_cols(t, n):
    return jnp.pad(t, ((0, 0), (0, n - t.shape[1])))


def _pack_small(mix, ffn, kv, fin, kva, qa, bf, last):
    row6 = jnp.concatenate([kva.reshape(-1), qa.reshape(-1), bf.reshape(-1),
                            jnp.zeros((D_MODEL - KV_LORA - Q_LORA - FOX_HEADS,), F32)])
    return jnp.stack([mix[0], mix[1], ffn[0], ffn[1], kv.reshape(-1), fin.reshape(-1), row6, last])


def _unpack_small(p):
    mix = p[0:2]
    ffn = p[2:4]
    kv = p[4]
    fin = p[5]
    kva = p[6, :KV_LORA]
    qa = p[6, KV_LORA:KV_LORA + Q_LORA].reshape(1, Q_LORA)
    bf = p[6, KV_LORA + Q_LORA:KV_LORA + Q_LORA + FOX_HEADS].reshape(1, FOX_HEADS)
    return mix, ffn, bf, kv, kva, qa, fin


def _mlp_fwd(tag, xin, g, w_up, w_down):
    h = _rms(f"{tag}_norm", xin, g, BF16)

    def act(acc):
        r = jnp.maximum(acc, 0.0)
        return acc, r * r

    u, a = _mm(f"{tag}_up", h, w_up, "nn", (BF16, BF16), epi=act)
    xout = _mm(f"{tag}_down", a, w_down, "nn", (F32,), epi=lambda acc, r: (acc + r,), extras=(xin,))
    return xout, (h, u, a)


def _mlp_bwd(tag, gout, xin, g, w_up, w_down, saved):
    h, u, a = saved
    dw_down = _mm_tn(f"{tag}_dwdown", a, gout)
    du = _mm(f"{tag}_du", gout, w_down, "nt", (BF16,),
             epi=lambda acc, uu: (acc * (2.0 * jnp.maximum(uu.astype(F32), 0.0)),), extras=(u,))
    dw_up = _mm_tn(f"{tag}_dwup", h, du)
    dh = _mm(f"{tag}_dh", du, w_up, "nt", (F32,))
    gin, dg = _rms_bwd(f"{tag}_norm_bwd", xin, g, dh, dres=gout)
    return gin, dg, dw_up, dw_down


def kernel(x, norm_mix_g, norm_ffn_g, fox_w_in, fox_b_f, fox_w_out, kv_norm_g, mla_w_kv_a, mla_kv_a_norm_g, mla_w_kv_b, mla_w_q_a, mla_q_a_norm_g, mla_w_q_b, mla_w_out, ffn_w_up, ffn_w_down, final_norm_g, loss_target, m_norm_mix_g, m_norm_ffn_g, m_fox_w_in, m_fox_b_f, m_fox_w_out, m_kv_norm_g, m_mla_w_kv_a, m_mla_kv_a_norm_g, m_mla_w_kv_b, m_mla_w_q_a, m_mla_q_a_norm_g, m_mla_w_q_b, m_mla_w_out, m_ffn_w_up, m_ffn_w_down, m_final_norm_g, v_norm_mix_g, v_norm_ffn_g, v_fox_w_in, v_fox_b_f, v_fox_w_out, v_kv_norm_g, v_mla_w_kv_a, v_mla_kv_a_norm_g, v_mla_w_kv_b, v_mla_w_q_a, v_mla_q_a_norm_g, v_mla_w_q_b, v_mla_w_out, v_ffn_w_up, v_ffn_w_down, v_final_norm_g):
    T = x.shape[1]
    D = D_MODEL
    tq = 512 if T >= 2048 else 128
    x0 = x[0]
    tgt = loss_target[0]

    gat_fox = _allgather_weights([fox_w_in[0].astype(BF16), fox_w_out[0].astype(BF16)])
    later_shards = [s.astype(BF16) for s in (mla_w_kv_a, mla_w_kv_b, mla_w_q_a[0], mla_w_q_b[0],
                                             mla_w_out[0], ffn_w_up, ffn_w_down)]
    w_in = gat_fox[0].transpose(1, 0, 2).reshape(D, 3 * D + FOX_HEADS)
    w_qkv = w_in[:, :3 * D]
    w_f = _pad_cols(w_in[:, 3 * D:], 128)
    w_fo = gat_fox[1].reshape(D, D)
    g_mix0, g_mix1 = norm_mix_g[0:1], norm_mix_g[1:2]
    g_ffn0, g_ffn1 = norm_ffn_g[0:1], norm_ffn_g[1:2]
    g_kv = kv_norm_g.reshape(1, D)
    g_kva = mla_kv_a_norm_g.reshape(1, KV_LORA)
    g_qa = mla_q_a_norm_g.reshape(1, Q_LORA)
    g_fin = final_norm_g.reshape(1, D)

    inv = 1.0 / (ROPE_BASE ** (jnp.arange(0, QK_ROPE, 2, dtype=F32) / QK_ROPE))
    ang = jnp.arange(T, dtype=F32)[:, None] * inv[None, :]
    cos, sin = jnp.cos(ang), jnp.sin(ang)
    cos2 = jnp.concatenate([cos, cos], axis=-1)
    sgn_sin = jnp.concatenate([-sin, sin], axis=-1)

    h0 = _rms("l0_mix_norm", x0, g_mix0, BF16)
    fl_pad = _mm("fox_gate_logit", h0, w_f, "nn", (F32,))
    fl = fl_pad[:, :FOX_HEADS].T
    b_f = fox_b_f.reshape(FOX_HEADS, 1)
    cgate = _gate_cumsum("fox_gate_scan", fl, b_f, tq)
    fox_scale = FOX_HEAD_DIM ** -0.5
    col_scale = jnp.where(jnp.arange(3 * D) < D, fox_scale, 1.0).astype(BF16)
    tail = jnp.arange(FOX_AUG - FOX_HEAD_DIM)
    ones_q = (tail < 3).astype(F32)
    consts_k = ((tail >= 4) & (tail < 7)).astype(F32) + (tail == 3).astype(F32) * (1.0 / fox_scale)
    tails = jnp.broadcast_to(jnp.stack([ones_q, consts_k, ones_q])[:, None, None, :],
                             (3, FOX_HEADS, 1, FOX_AUG - FOX_HEAD_DIM)).reshape(3 * FOX_HEADS, 1, -1)
    qkv_h = _mm_head_slabs("fox_qkv", h0, w_qkv * col_scale, tails, FOX_HEAD_DIM, BF16)
    fk_aug = _widen(qkv_h[FOX_HEADS:2 * FOX_HEADS], FOX_AUG, pieces_at=FOX_HEAD_DIM,
                    pieces=_split3(-cgate))
    (fo, flse), gat = _flash_fwd("fox_attn", qkv_h, fk_aug, qkv_h, FOX_HEAD_DIM, tq,
                                 exchange=("gather", later_shards), q_head0=0, v_head0=2 * FOX_HEADS)
    w_kva = _pad_cols(gat[0].reshape(D, KV_LORA + QK_ROPE), KV_A_PAD)
    w_kvb_h = gat[1]
    w_qa = gat[2].reshape(D, Q_LORA)
    w_qb_h = gat[3]
    w_mo = gat[4].reshape(D, D)
    w_up = gat[5].transpose(1, 2, 0, 3).reshape(2, D, D_FF)
    w_down = gat[6].transpose(1, 0, 2, 3).reshape(2, D_FF, D)
    fctx = _from_heads(fo).astype(BF16)
    x1 = _mm("fox_out", fctx, w_fo, "nn", (F32,), epi=lambda acc, r: (acc + r,), extras=(x0,))
    x2, mlp0 = _mlp_fwd("l0_ffn", x1, g_ffn0, w_up[0], w_down[0])

    src = _rms("kv_norm", x2, g_kv, BF16)
    kva = _mm("kv_a", src, w_kva, "nn", (F32,))
    kva_lat = kva[:, :KV_LORA]
    c_kv = _rms("kv_a_norm", kva_lat, g_kva, BF16)
    k_rope = _rope("k_rope", kva[:, KV_LORA:KV_LORA + QK_ROPE][None], cos2, sgn_sin, BF16)
    kvb_h = _mm_heads("kv_b", c_kv, w_kvb_h, BF16)
    mk = jnp.concatenate([kvb_h[:, :, :QK_NOPE],
                          jnp.broadcast_to(k_rope, (MLA_HEADS, T, QK_ROPE))], axis=-1)
    mv = kvb_h[:, :, QK_NOPE:]

    h1 = _rms("l1_mix_norm", x2, g_mix1, BF16)
    qa = _mm("q_a", h1, w_qa, "nn", (F32,))
    c_q = _rms("q_a_norm", qa, g_qa, BF16)
    qf_h = _mm_heads("q_b", c_q, w_qb_h, F32)
    mla_scale = (QK_NOPE + QK_ROPE) ** -0.5
    mq = _mla_q_prep("q_prep", qf_h, cos2, sgn_sin, mla_scale)
    mv_aug = _widen(mv, MLA_AUG, ones_at=V_HEAD)
    (mo, mlse), _ = _flash_fwd("mla_attn", mq, mk, mv_aug, V_HEAD, tq)
    mctx = _from_heads(mo).astype(BF16)
    x3 = _mm("mla_out", mctx, w_mo, "nn", (F32,), epi=lambda acc, r: (acc + r,), extras=(x2,))
    x4, mlp1 = _mlp_fwd("l1_ffn", x3, g_ffn1, w_up[1], w_down[1])

    g4, dg_fin, loss_vec = _loss_head("loss_head", x4, g_fin, tgt)

    g3, dg_ffn1, dw_up1, dw_down1 = _mlp_bwd("l1_ffn", g4, x3, g_ffn1, w_up[1], w_down[1], mlp1)

    dw_mo = _mm_tn("mla_out_dw", mctx, g3)
    dmo = _to_heads(_mm("mla_out_dx", g3, w_mo, "nt", (BF16,)), MLA_HEADS)
    mdelta = _row_dot("mla_delta", mo, dmo)
    dqk = QK_NOPE + QK_ROPE
    mq_bwd = _widen(mq, MLA_AUG, pieces_at=dqk, pieces=_split3(-mlse))
    mk_bwd = _widen(mk, MLA_AUG, ones_at=dqk)
    mdo_aug = _widen(dmo, MLA_AUG, pieces_at=V_HEAD, pieces=_split3(-mdelta))
    (mdq, mdk, mdv), _ = _flash_bwd("mla_attn_bwd", mq_bwd, mk_bwd, mv_aug, mdo_aug, mla_scale, tq)
    mdq = mdq[:, :, :dqk]
    mdk = mdk[:, :, :dqk]
    mdv = mdv[:, :, :V_HEAD]
    dq_rope = _rope_bwd("q_rope_bwd", mdq[:, :, QK_NOPE:], cos2, sgn_sin, False)
    dqf_h = jnp.concatenate([mdq[:, :, :QK_NOPE], dq_rope], axis=-1)
    dw_qb_h = _mm_heads_dw("q_b_dw", c_q, dqf_h)
    dc_q = _mm_heads_dx("q_b_dx", dqf_h, w_qb_h)
    dqa, dg_qa = _rms_bwd("q_a_norm_bwd", qa, g_qa, dc_q)
    dw_qa = _mm_tn("q_a_dw", h1, dqa)
    dh1 = _mm("q_a_dx", dqa, w_qa, "nt", (F32,))
    g2a, dg_mix1 = _rms_bwd("l1_mix_norm_bwd", x2, g_mix1, dh1, dres=g3)

    dk_rope = _rope_bwd("k_rope_bwd", mdk[:, :, QK_NOPE:], cos2, sgn_sin, True)
    dkvb_h = jnp.concatenate([mdk[:, :, :QK_NOPE], mdv], axis=-1)
    dw_kvb_h = _mm_heads_dw("kv_b_dw", c_kv, dkvb_h)
    dc_kv = _mm_heads_dx("kv_b_dx", dkvb_h, w_kvb_h)
    dkva_lat, dg_kva = _rms_bwd("kv_a_norm_bwd", kva_lat, g_kva, dc_kv)
    dkva = _pad_cols(jnp.concatenate([dkva_lat, dk_rope], axis=-1), KV_A_PAD)
    dw_kva = _mm_tn("kv_a_dw", src, dkva)[:, :KV_LORA + QK_ROPE]
    dsrc = _mm("kv_a_dx", dkva, w_kva, "nt", (F32,))
    g2, dg_kv = _rms_bwd("kv_norm_bwd", x2, g_kv, dsrc, dres=g2a)

    g1, dg_ffn0, dw_up0, dw_down0 = _mlp_bwd("l0_ffn", g2, x1, g_ffn0, w_up[0], w_down[0], mlp0)

    dw_fo = _mm_tn("fox_out_dw", fctx, g1)
    dfo = _to_heads(_mm("fox_out_dx", g1, w_fo, "nt", (BF16,)), FOX_HEADS)
    fdelta = _row_dot("fox_delta", fo, dfo)
    fq_bwd = _widen(qkv_h[:FOX_HEADS], FOX_AUG, pieces_at=FOX_HEAD_DIM + 4, pieces=_split3(-flse))
    fdo_aug = _widen(dfo, FOX_AUG, pieces_at=FOX_HEAD_DIM, pieces=_split3(-fdelta))
    dw_up = jnp.stack([dw_up0, dw_up1])
    dw_down = jnp.stack([dw_down0, dw_down1])
    early = [
        dw_fo.reshape(N_DEV, D // N_DEV, D),
        dw_kva.reshape(N_DEV, D // N_DEV, KV_LORA + QK_ROPE),
        dw_kvb_h,
        dw_qa.reshape(N_DEV, D // N_DEV, Q_LORA),
        dw_qb_h,
        dw_mo.reshape(N_DEV, D // N_DEV, D),
        dw_up.reshape(2, D, N_DEV, -1).transpose(2, 0, 1, 3),
        dw_down.reshape(2, N_DEV, D_FF // N_DEV, D).transpose(1, 0, 2, 3),
    ]
    (fdq_aug, fdk_aug, fdv_aug), early_parts = _flash_bwd(
        "fox_attn_bwd", fq_bwd, fk_aug, qkv_h, fdo_aug, fox_scale, tq,
        exchange=("scatter", [g.astype(BF16) for g in early]), v_head0=2 * FOX_HEADS)
    dfl, db_f = _gate_cumsum_bwd("fox_gate_scan_bwd", fdq_aug[:, :, FOX_HEAD_DIM + 3],
                                 fdk_aug[:, :, FOX_HEAD_DIM], fl, b_f, tq)
    dfl_pad = _pad_cols(dfl.T, 128)
    dw_f = _mm_tn("fox_gate_dw", h0, dfl_pad)[:, :FOX_HEADS]
    dh0 = _mm("fox_gate_dx", dfl_pad, w_f, "nt", (F32,))
    w_slabs = jnp.pad(w_qkv.reshape(D, 3 * FOX_HEADS, FOX_HEAD_DIM).transpose(1, 0, 2),
                      ((0, 0), (0, 0), (0, FOX_AUG - FOX_HEAD_DIM)))
    dw_slabs = []
    for s, (part, g) in enumerate((("q", fdq_aug), ("k", fdk_aug), ("v", fdv_aug))):
        dw_slabs.append(_mm_heads_dw(f"fox_{part}_dw", h0, g)[:, :, :FOX_HEAD_DIM])
        dh0 = _mm_heads_dx(f"fox_{part}_dx", g, w_slabs[s * FOX_HEADS:(s + 1) * FOX_HEADS], add=dh0)
    dw_qkv = jnp.concatenate(dw_slabs, axis=0).transpose(1, 0, 2).reshape(D, 3 * D)
    dw_in = jnp.concatenate([dw_qkv, dw_f], axis=-1)
    grad_x, dg_mix0 = _rms_bwd("l0_mix_norm_bwd", x0, g_mix0, dh0, dres=g1)

    late = dw_in.reshape(D, N_DEV, -1).transpose(1, 0, 2).astype(BF16)
    parts = list(_alltoall_grads("alltoall_fox_w_in", [late])) + list(early_parts)

    names = ["fox_w_in", "fox_w_out", "mla_w_kv_a", "mla_w_kv_b", "mla_w_q_a", "mla_w_q_b",
             "mla_w_out", "ffn_w_up", "ffn_w_down"]
    moms = [m_fox_w_in, m_fox_w_out, m_mla_w_kv_a, m_mla_w_kv_b, m_mla_w_q_a, m_mla_w_q_b,
            m_mla_w_out, m_ffn_w_up, m_ffn_w_down]
    vars_ = [v_fox_w_in, v_fox_w_out, v_mla_w_kv_a, v_mla_w_kv_b, v_mla_w_q_a, v_mla_w_q_b,
             v_mla_w_out, v_ffn_w_up, v_ffn_w_down]
    full = [fox_w_in, fox_w_out, mla_w_kv_a, mla_w_kv_b, mla_w_q_a, mla_w_q_b, mla_w_out,
            ffn_w_up, ffn_w_down]
    big = {}
    for nm, p, w, m, v in zip(names, parts, full, moms, vars_):
        C = w.shape[-1]
        res = _adamw(f"adamw_{nm}", p.reshape(N_DEV, -1, C), w.reshape(-1, C), m.reshape(-1, C),
                     v.reshape(-1, C))
        big[nm] = [r.reshape(w.shape) for r in res]

    zrow = jnp.zeros((D,), F32)
    g_small = _pack_small(jnp.concatenate([dg_mix0, dg_mix1]), jnp.concatenate([dg_ffn0, dg_ffn1]),
                          dg_kv, dg_fin, dg_kva, dg_qa, db_f, zrow.at[0].set(loss_vec[0, 0]))
    tot_small = _allreduce_small(g_small)
    w_small = _pack_small(norm_mix_g, norm_ffn_g, kv_norm_g, final_norm_g, mla_kv_a_norm_g,
                          mla_q_a_norm_g, fox_b_f, zrow)
    m_small = _pack_small(m_norm_mix_g, m_norm_ffn_g, m_kv_norm_g, m_final_norm_g, m_mla_kv_a_norm_g,
                          m_mla_q_a_norm_g, m_fox_b_f, zrow)
    v_small = _pack_small(v_norm_mix_g, v_norm_ffn_g, v_kv_norm_g, v_final_norm_g, v_mla_kv_a_norm_g,
                          v_mla_q_a_norm_g, v_fox_b_f, zrow)
    small = _adamw("adamw_small", tot_small[None], w_small, m_small, v_small)
    loss = tot_small[7, 0]
    small = [_unpack_small(s) for s in small]

    def ordered(i):
        mix, ffn, bf, kv, kva, qa, fin = small[i]
        return [mix, ffn, big["fox_w_in"][i], bf, big["fox_w_out"][i], kv, big["mla_w_kv_a"][i], kva,
                big["mla_w_kv_b"][i], big["mla_w_q_a"][i], qa, big["mla_w_q_b"][i],
                big["mla_w_out"][i], big["ffn_w_up"][i], big["ffn_w_down"][i], fin]

    return (loss, grad_x[None], *ordered(0), *ordered(1), *ordered(2), *ordered(3))
```

```python
import functools
import math

import jax
import jax.numpy as jnp
from jax import lax
from jax.experimental import pallas as pl
from jax.experimental.pallas import tpu as pltpu

F32 = jnp.float32
BF16 = jnp.bfloat16
MESH = pl.DeviceIdType.MESH

N_DEV = 8
D_MODEL = 1024
FOX_HEADS = 16
FOX_HEAD_DIM = 64
FOX_AUG = 128
MLA_AUG = 256
MLA_HEADS = 8
QK_NOPE = 128
QK_ROPE = 64
V_HEAD = 128
Q_LORA = 384
KV_LORA = 256
KV_A_PAD = 384
D_FF = 4096
ROPE_BASE = 10000.0
EPS = 1e-6
NEG = -1e30

ADAM_LR = 0.001
ADAM_B1 = 0.9
ADAM_B2 = 0.999
ADAM_EPS = 1e-08
ADAM_WD = 0.01
ADAM_STEP = 10

VMEM_LIMIT_BYTES = 56 * 1024 * 1024

NN = (((1,), (0,)), ((), ()))
NT = (((1,), (1,)), ((), ()))
TN = (((0,), (0,)), ((), ()))
_FORMS = {"nn": NN, "nt": NT}


def _cparams(sem=None):
    return pltpu.CompilerParams(dimension_semantics=sem, vmem_limit_bytes=VMEM_LIMIT_BYTES)


def _pick(n, cands):
    for c in cands:
        if c <= n and n % c == 0:
            return c
    return n


def _dot(a, b, dims):
    return lax.dot_general(a, b, dims, preferred_element_type=F32)


def _mm(name, a, b, form, out_dtypes, epi=None, extras=(), tm=1024, tn=None):
    M, K = a.shape
    N = b.shape[1] if form == "nn" else b.shape[0]
    tm = _pick(M, (tm, 512, 256, 128))
    tn = _pick(N, (tn or (1024 if K <= 1024 else 512), 512, 384, 256, 128))
    n_ex = len(extras)
    n_out = len(out_dtypes)
    cast_once = a.dtype != BF16

    def body(*refs):
        a_ref, b_ref = refs[0], refs[1]
        ex = refs[2:2 + n_ex]
        outs = refs[2 + n_ex:2 + n_ex + n_out]
        if cast_once:
            a_sc = refs[2 + n_ex + n_out]

            @pl.when(pl.program_id(1) == 0)
            def _():
                a_sc[...] = a_ref[...].astype(BF16)

            av = a_sc[...]
        else:
            av = a_ref[...]
        acc = _dot(av, b_ref[...].astype(BF16), _FORMS[form])
        res = epi(acc, *[e[...] for e in ex]) if epi is not None else (acc,)
        for o_ref, r in zip(outs, res):
            o_ref[...] = r.astype(o_ref.dtype)

    if form == "nn":
        b_spec = pl.BlockSpec((K, tn), lambda i, j: (0, j))
    else:
        b_spec = pl.BlockSpec((tn, K), lambda i, j: (j, 0))
    tile = pl.BlockSpec((tm, tn), lambda i, j: (i, j))
    out = pl.pallas_call(
        body, name=name, grid=(M // tm, N // tn),
        in_specs=[pl.BlockSpec((tm, K), lambda i, j: (i, 0)), b_spec] + [tile] * n_ex,
        out_specs=[tile] * n_out,
        out_shape=[jax.ShapeDtypeStruct((M, N), dt) for dt in out_dtypes],
        scratch_shapes=[pltpu.VMEM((tm, K), BF16)] if cast_once else [],
        compiler_params=_cparams(("parallel", "arbitrary")),
    )(a, b, *extras)
    return out if n_out > 1 else out[0]


def _mm_tn(name, a, b):
    T, Ka = a.shape
    N = b.shape[1]
    tk = _pick(Ka, (1024, 512, 384, 256, 128))
    tn = _pick(N, (1024, 768, 512, 384, 256, 128))
    tt = _pick(T, (1024, 512, 256, 128))

    def body(a_ref, b_ref, o_ref):
        @pl.when(pl.program_id(2) == 0)
        def _():
            o_ref[...] = jnp.zeros_like(o_ref)

        o_ref[...] += _dot(a_ref[...].astype(BF16), b_ref[...].astype(BF16), TN)

    return pl.pallas_call(
        body, name=name, grid=(Ka // tk, N // tn, T // tt),
        in_specs=[pl.BlockSpec((tt, tk), lambda i, j, t: (t, i)),
                  pl.BlockSpec((tt, tn), lambda i, j, t: (t, j))],
        out_specs=pl.BlockSpec((tk, tn), lambda i, j, t: (i, j)),
        out_shape=jax.ShapeDtypeStruct((Ka, N), F32),
        compiler_params=_cparams(("parallel", "parallel", "arbitrary")),
    )(a, b)


def _mm_heads(name, a, w, out_dtype):
    T, K = a.shape
    H, _, N = w.shape
    tm = _pick(T, (1024, 512, 256, 128))

    def body(a_ref, w_ref, o_ref):
        av = a_ref[...].astype(BF16)
        for s in range(H):
            o_ref[s] = _dot(av, w_ref[s].astype(BF16), NN).astype(o_ref.dtype)

    return pl.pallas_call(
        body, name=name, grid=(T // tm,),
        in_specs=[pl.BlockSpec((tm, K), lambda i: (i, 0)), pl.BlockSpec((H, K, N), lambda i: (0, 0, 0))],
        out_specs=pl.BlockSpec((H, tm, N), lambda i: (0, i, 0)),
        out_shape=jax.ShapeDtypeStruct((H, T, N), out_dtype),
        compiler_params=_cparams(("parallel",)),
    )(a, w)


def _mm_head_slabs(name, a, w, tails, head_dim, out_dtype, heads_per_step=8):
    T, K = a.shape
    S, _, tail = tails.shape
    hb = heads_per_step
    tm = _pick(T, (1024, 512, 256, 128))

    def body(a_ref, w_ref, t_ref, o_ref):
        acc = _dot(a_ref[...].astype(BF16), w_ref[...].astype(BF16), NN)
        for s in range(hb):
            slab = jnp.concatenate([acc[:, s * head_dim:(s + 1) * head_dim],
                                    jnp.broadcast_to(t_ref[s], (tm, tail))], axis=-1)
            o_ref[s] = slab.astype(o_ref.dtype)

    return pl.pallas_call(
        body, name=name, grid=(T // tm, S // hb),
        in_specs=[pl.BlockSpec((tm, K), lambda i, j: (i, 0)),
                  pl.BlockSpec((K, hb * head_dim), lambda i, j: (0, j)),
                  pl.BlockSpec((hb, 1, tail), lambda i, j: (j, 0, 0))],
        out_specs=pl.BlockSpec((hb, tm, head_dim + tail), lambda i, j: (j, i, 0)),
        out_shape=jax.ShapeDtypeStruct((S, T, head_dim + tail), out_dtype),
        compiler_params=_cparams(("parallel", "arbitrary")),
    )(a, w, tails)


def _mm_heads_dw(name, a, g):
    T, K = a.shape
    H, _, N = g.shape
    tt = _pick(T, (1024, 512, 256, 128))

    def body(a_ref, g_ref, o_ref):
        @pl.when(pl.program_id(0) == 0)
        def _():
            o_ref[...] = jnp.zeros_like(o_ref)

        av = a_ref[...].astype(BF16)
        for s in range(H):
            o_ref[s] += _dot(av, g_ref[s].astype(BF16), TN)

    return pl.pallas_call(
        body, name=name, grid=(T // tt,),
        in_specs=[pl.BlockSpec((tt, K), lambda t: (t, 0)), pl.BlockSpec((H, tt, N), lambda t: (0, t, 0))],
        out_specs=pl.BlockSpec((H, K, N), lambda t: (0, 0, 0)),
        out_shape=jax.ShapeDtypeStruct((H, K, N), F32),
        compiler_params=_cparams(("arbitrary",)),
    )(a, g)


def _mm_heads_dx(name, g, w):
    H, T, N = g.shape
    K = w.shape[1]
    tm = _pick(T, (512, 256, 128))

    def body(g_ref, w_ref, o_ref):
        acc = _dot(g_ref[0].astype(BF16), w_ref[0].astype(BF16), NT)
        for s in range(1, H):
            acc = acc + _dot(g_ref[s].astype(BF16), w_ref[s].astype(BF16), NT)
        o_ref[...] = acc

    return pl.pallas_call(
        body, name=name, grid=(T // tm,),
        in_specs=[pl.BlockSpec((H, tm, N), lambda i: (0, i, 0)), pl.BlockSpec((H, K, N), lambda i: (0, 0, 0))],
        out_specs=pl.BlockSpec((tm, K), lambda i: (i, 0)),
        out_shape=jax.ShapeDtypeStruct((T, K), F32),
        compiler_params=_cparams(("parallel",)),
    )(g, w)


def _rms(name, x, g, out_dtype):
    T, D = x.shape
    tm = _pick(T, (1024, 512, 256, 128))

    def body(x_ref, g_ref, o_ref):
        xf = x_ref[...]
        r = lax.rsqrt(jnp.mean(xf * xf, axis=-1, keepdims=True) + EPS)
        o_ref[...] = (xf * r * g_ref[...]).astype(o_ref.dtype)

    return pl.pallas_call(
        body, name=name, grid=(T // tm,),
        in_specs=[pl.BlockSpec((tm, D), lambda i: (i, 0)), pl.BlockSpec((1, D), lambda i: (0, 0))],
        out_specs=pl.BlockSpec((tm, D), lambda i: (i, 0)),
        out_shape=jax.ShapeDtypeStruct((T, D), out_dtype),
        compiler_params=_cparams(("parallel",)),
    )(x, g)


def _rms_bwd(name, x, g, dh, dres=None):
    T, D = x.shape
    tm = _pick(T, (512, 256, 128))
    has_res = dres is not None

    def body(*refs):
        if has_res:
            x_ref, g_ref, dh_ref, dres_ref, dx_ref, dg_ref = refs
        else:
            x_ref, g_ref, dh_ref, dx_ref, dg_ref = refs

        @pl.when(pl.program_id(0) == 0)
        def _():
            dg_ref[...] = jnp.zeros_like(dg_ref)

        xf = x_ref[...]
        r = lax.rsqrt(jnp.mean(xf * xf, axis=-1, keepdims=True) + EPS)
        xhat = xf * r
        dy = dh_ref[...].astype(F32)
        dxh = dy * g_ref[...]
        dx = r * (dxh - xhat * jnp.mean(dxh * xhat, axis=-1, keepdims=True))
        if has_res:
            dx = dx + dres_ref[...]
        dx_ref[...] = dx
        dg_ref[...] += jnp.sum(dy * xhat, axis=0, keepdims=True)

    row = pl.BlockSpec((tm, D), lambda i: (i, 0))
    vec = pl.BlockSpec((1, D), lambda i: (0, 0))
    ins = [x, g, dh] + ([dres] if has_res else [])
    return pl.pallas_call(
        body, name=name, grid=(T // tm,),
        in_specs=[row, vec, row] + ([row] if has_res else []),
        out_specs=[row, vec],
        out_shape=[jax.ShapeDtypeStruct((T, D), F32), jax.ShapeDtypeStruct((1, D), F32)],
        compiler_params=_cparams(("arbitrary",)),
    )(*ins)


def _mm_rms_bwd(name, a, b, x, g, dres, add=None):
    T, K = a.shape
    D = b.shape[0]
    tm = _pick(T, (512, 256, 128))
    has_add = add is not None

    def body(*refs):
        a_ref, b_ref, x_ref, g_ref, dres_ref = refs[:5]
        dx_ref, dg_ref = refs[-2:]

        @pl.when(pl.program_id(0) == 0)
        def _():
            dg_ref[...] = jnp.zeros_like(dg_ref)

        dy = _dot(a_ref[...].astype(BF16), b_ref[...].astype(BF16), NT)
        if has_add:
            dy = dy + refs[5][...]
        xf = x_ref[...]
        r = lax.rsqrt(jnp.mean(xf * xf, axis=-1, keepdims=True) + EPS)
        xhat = xf * r
        dxh = dy * g_ref[...]
        dx_ref[...] = r * (dxh - xhat * jnp.mean(dxh * xhat, axis=-1, keepdims=True)) + dres_ref[...]
        dg_ref[...] += jnp.sum(dy * xhat, axis=0, keepdims=True)

    row = pl.BlockSpec((tm, D), lambda i: (i, 0))
    vec = pl.BlockSpec((1, D), lambda i: (0, 0))
    return pl.pallas_call(
        body, name=name, grid=(T // tm,),
        in_specs=[pl.BlockSpec((tm, K), lambda i: (i, 0)), pl.BlockSpec((D, K), lambda i: (0, 0)), row, vec, row]
        + ([row] if has_add else []),
        out_specs=[row, vec],
        out_shape=[jax.ShapeDtypeStruct((T, D), F32), jax.ShapeDtypeStruct((1, D), F32)],
        compiler_params=_cparams(("arbitrary",)),
    )(a, b, x, g, dres, *([add] if has_add else []))


def _loss_head(name, x, g, tgt):
    T, D = x.shape
    tm = _pick(T, (512, 256, 128))

    def body(x_ref, g_ref, t_ref, dx_ref, dg_ref, loss_ref):
        @pl.when(pl.program_id(0) == 0)
        def _():
            dg_ref[...] = jnp.zeros_like(dg_ref)
            loss_ref[...] = jnp.zeros_like(loss_ref)

        xf = x_ref[...]
        r = lax.rsqrt(jnp.mean(xf * xf, axis=-1, keepdims=True) + EPS)
        xhat = xf * r
        gv = g_ref[...]
        err = xhat * gv - t_ref[...]
        row_loss = jnp.mean(err * err, axis=-1, keepdims=True)
        loss_ref[...] += 0.5 * jnp.sum(row_loss, axis=0, keepdims=True)
        dy = err * (1.0 / D)
        dxh = dy * gv
        dx_ref[...] = r * (dxh - xhat * jnp.mean(dxh * xhat, axis=-1, keepdims=True))
        dg_ref[...] += jnp.sum(dy * xhat, axis=0, keepdims=True)

    row = pl.BlockSpec((tm, D), lambda i: (i, 0))
    vec = pl.BlockSpec((1, D), lambda i: (0, 0))
    return pl.pallas_call(
        body, name=name, grid=(T // tm,),
        in_specs=[row, vec, row],
        out_specs=[row, vec, pl.BlockSpec((1, 128), lambda i: (0, 0))],
        out_shape=[jax.ShapeDtypeStruct((T, D), F32), jax.ShapeDtypeStruct((1, D), F32),
                   jax.ShapeDtypeStruct((1, 128), F32)],
        compiler_params=_cparams(("arbitrary",)),
    )(x, g, tgt)


def _swap_halves(t):
    half = t.shape[-1] // 2
    return jnp.concatenate([t[:, half:], t[:, :half]], axis=-1)


def _rope(name, t, cos2, sgn_sin, out_dtype):
    H, T, R = t.shape
    tm = _pick(T, (1024, 512, 256, 128))

    def body(t_ref, c_ref, s_ref, o_ref):
        tf = t_ref[...].astype(F32)
        o_ref[...] = (tf * c_ref[...] + _swap_halves(tf) * s_ref[...]).astype(o_ref.dtype)

    slab = pl.BlockSpec((None, tm, R), lambda h, i: (h, i, 0))
    tab = pl.BlockSpec((tm, R), lambda h, i: (i, 0))
    return pl.pallas_call(
        body, name=name, grid=(H, T // tm),
        in_specs=[slab, tab, tab], out_specs=slab,
        out_shape=jax.ShapeDtypeStruct((H, T, R), out_dtype),
        compiler_params=_cparams(("parallel", "parallel")),
    )(t, cos2, sgn_sin)


def _mla_q_proj(name, a, w, cos2, sgn_sin, scale):
    T, K = a.shape
    H, _, W = w.shape
    R = cos2.shape[1]
    tm = _pick(T, (1024, 512, 256, 128))

    def body(a_ref, w_ref, c_ref, s_ref, o_ref):
        av = a_ref[...].astype(BF16)
        for h in range(H):
            qf = _dot(av, w_ref[h].astype(BF16), NN)
            r = qf[:, W - R:]
            roped = r * c_ref[...] + _swap_halves(r) * s_ref[...]
            o_ref[h] = (jnp.concatenate([qf[:, :W - R], roped], axis=-1) * scale).astype(o_ref.dtype)

    tab = pl.BlockSpec((tm, R), lambda i: (i, 0))
    return pl.pallas_call(
        body, name=name, grid=(T // tm,),
        in_specs=[pl.BlockSpec((tm, K), lambda i: (i, 0)), pl.BlockSpec((H, K, W), lambda i: (0, 0, 0)), tab, tab],
        out_specs=pl.BlockSpec((H, tm, W), lambda i: (0, i, 0)),
        out_shape=jax.ShapeDtypeStruct((H, T, W), BF16),
        compiler_params=_cparams(("parallel",)),
    )(a, w, cos2, sgn_sin)


def _rope_bwd(name, dy, cos2, sgn_sin, sum_heads):
    H, T, R = dy.shape
    tm = _pick(T, (1024, 512, 256, 128))

    def body(d_ref, c_ref, s_ref, o_ref):
        d = d_ref[...]
        if sum_heads:
            tot = d[0]
            for h in range(1, H):
                tot = tot + d[h]
            d = tot
        o_ref[...] = d * c_ref[...] + _swap_halves(d * s_ref[...])

    if sum_heads:
        grid = (T // tm,)
        in_slab = pl.BlockSpec((H, tm, R), lambda i: (0, i, 0))
        out_slab = pl.BlockSpec((tm, R), lambda i: (i, 0))
        tab = pl.BlockSpec((tm, R), lambda i: (i, 0))
        out_shape = jax.ShapeDtypeStruct((T, R), F32)
        sem = ("parallel",)
    else:
        grid = (H, T // tm)
        in_slab = pl.BlockSpec((None, tm, R), lambda h, i: (h, i, 0))
        out_slab = in_slab
        tab = pl.BlockSpec((tm, R), lambda h, i: (i, 0))
        out_shape = jax.ShapeDtypeStruct((H, T, R), F32)
        sem = ("parallel", "parallel")
    return pl.pallas_call(
        body, name=name, grid=grid, in_specs=[in_slab, tab, tab], out_specs=out_slab,
        out_shape=out_shape, compiler_params=_cparams(sem),
    )(dy, cos2, sgn_sin)


def _log_sigmoid(z):
    return jnp.minimum(z, 0.0) - jnp.log(1.0 + jnp.exp(-jnp.abs(z)))


def _gate_cumsum(name, fl, b, tb):
    H, T = fl.shape

    def body(f_ref, b_ref, c_ref, carry):
        @pl.when(pl.program_id(0) == 0)
        def _():
            carry[...] = jnp.zeros_like(carry)

        ls = _log_sigmoid(f_ref[...] + b_ref[...])
        src = lax.broadcasted_iota(jnp.int32, (tb, tb), 0)
        dst = lax.broadcasted_iota(jnp.int32, (tb, tb), 1)
        tri = (src <= dst).astype(F32)
        c = lax.dot_general(ls, tri, NN, precision=lax.Precision.HIGHEST,
                            preferred_element_type=F32) + carry[...]
        c_ref[...] = c
        carry[...] = carry[...] + jnp.sum(ls, axis=-1, keepdims=True)

    return pl.pallas_call(
        body, name=name, grid=(T // tb,),
        in_specs=[pl.BlockSpec((H, tb), lambda i: (0, i)), pl.BlockSpec((H, 1), lambda i: (0, 0))],
        out_specs=pl.BlockSpec((H, tb), lambda i: (0, i)),
        out_shape=jax.ShapeDtypeStruct((H, T), F32),
        scratch_shapes=[pltpu.VMEM((H, 1), F32)],
        compiler_params=_cparams(("arbitrary",)),
    )(fl, b)


def _gate_cumsum_bwd(name, d_query, d_key, fl, b, tb):
    H, T = fl.shape
    nb = T // tb

    def body(dq_ref, dk_ref, f_ref, b_ref, dfl_ref, db_ref, carry):
        @pl.when(pl.program_id(0) == 0)
        def _():
            carry[...] = jnp.zeros_like(carry)
            db_ref[...] = jnp.zeros_like(db_ref)

        d = dq_ref[...] - dk_ref[...]
        src = lax.broadcasted_iota(jnp.int32, (tb, tb), 0)
        dst = lax.broadcasted_iota(jnp.int32, (tb, tb), 1)
        tri = (src >= dst).astype(F32)
        dls = lax.dot_general(d, tri, NN, precision=lax.Precision.HIGHEST,
                              preferred_element_type=F32) + carry[...]
        z = f_ref[...] + b_ref[...]
        dfl = dls * (1.0 / (1.0 + jnp.exp(z)))
        dfl_ref[...] = dfl
        db_ref[...] += jnp.sum(dfl, axis=-1, keepdims=True)
        carry[...] = carry[...] + jnp.sum(d, axis=-1, keepdims=True)

    blk = pl.BlockSpec((H, tb), lambda i: (0, nb - 1 - i))
    vec = pl.BlockSpec((H, 1), lambda i: (0, 0))
    return pl.pallas_call(
        body, name=name, grid=(nb,),
        in_specs=[blk, blk, blk, vec], out_specs=[blk, vec],
        out_shape=[jax.ShapeDtypeStruct((H, T), F32), jax.ShapeDtypeStruct((H, 1), F32)],
        scratch_shapes=[pltpu.VMEM((H, 1), F32)],
        compiler_params=_cparams(("arbitrary",)),
    )(d_query, d_key, fl, b)


def _causal_mask(tq, rows_are_queries):
    r = lax.broadcasted_iota(jnp.int32, (tq, tq), 0)
    c = lax.broadcasted_iota(jnp.int32, (tq, tq), 1)
    return (c <= r) if rows_are_queries else (r <= c)


def _chunk_rows(j, tq):
    return pl.ds(pl.multiple_of(j * tq, tq), tq)


def _column_as_row(col):
    return jnp.broadcast_to(col, (col.shape[0], 128)).T[:1, :]


def _flash_fwd(name, q, k, v_aug, dv, tq, exchange=None, q_head0=0, v_head0=0):
    H, T, dqk = k.shape
    dva = v_aug.shape[2]
    nq = T // tq

    def body(q_ref, k_ref, v_ref, o_ref, lse_ref, m_sc, acc_sc):
        qi = pl.program_id(1)
        m_sc[...] = jnp.full_like(m_sc, NEG)
        acc_sc[...] = jnp.zeros_like(acc_sc)

        def chunk(j, masked):
            rows = _chunk_rows(j, tq)
            s = _dot(q_ref[...], k_ref[rows, :], NT)
            if masked:
                s = jnp.where(_causal_mask(tq, True), s, NEG)
            m_prev = m_sc[...]
            m_new = jnp.maximum(m_prev, jnp.max(s, axis=1, keepdims=True))
            p = jnp.exp(s - jnp.tile(m_new, (1, tq // 128)))
            alpha = jnp.tile(jnp.exp(m_prev - m_new), (1, dva // 128))
            acc_sc[...] = alpha * acc_sc[...] + _dot(p.astype(BF16), v_ref[rows, :], NN)
            m_sc[...] = m_new

        def off_diagonal(j, carry):
            chunk(j, False)
            return carry

        lax.fori_loop(0, qi, off_diagonal, 0)
        chunk(qi, True)
        acc = acc_sc[...]
        l = acc[:, dv:dv + 1]
        o_ref[...] = acc[:, :dv] / l
        lse_ref[...] = _column_as_row(m_sc[:, :1] + jnp.log(l))

    (o, lse_rows), exchanged = _call_carrying(
        body, name, (H, nq),
        in_specs=[pl.BlockSpec((None, tq, dqk), lambda h, i: (h + q_head0, i, 0)),
                  pl.BlockSpec((None, T, dqk), lambda h, i: (h, 0, 0)),
                  pl.BlockSpec((None, T, dva), lambda h, i: (h + v_head0, 0, 0))],
        out_specs=[pl.BlockSpec((None, tq, dv), lambda h, i: (h, i, 0)),
                   pl.BlockSpec((None, 1, tq), lambda h, i: (h, 0, i))],
        out_shape=[jax.ShapeDtypeStruct((H, T, dv), F32), jax.ShapeDtypeStruct((H, 1, T), F32)],
        scratch_shapes=[pltpu.VMEM((tq, 128), F32), pltpu.VMEM((tq, dva), F32)],
        operands=(q, k, v_aug), exchange=exchange)
    return (o, lse_rows.reshape(H, T)), exchanged


def _row_dot(name, a, b):
    H, T, d = a.shape
    tm = _pick(T, (1024, 512, 256, 128))

    def body(a_ref, b_ref, o_ref):
        col = jnp.sum(a_ref[...].astype(F32) * b_ref[...].astype(F32), axis=-1, keepdims=True)
        o_ref[...] = _column_as_row(col)

    slab = pl.BlockSpec((None, tm, d), lambda h, i: (h, i, 0))
    return pl.pallas_call(
        body, name=name, grid=(H, T // tm), in_specs=[slab, slab],
        out_specs=pl.BlockSpec((None, 1, tm), lambda h, i: (h, 0, i)),
        out_shape=jax.ShapeDtypeStruct((H, 1, T), F32),
        compiler_params=_cparams(("parallel", "parallel")),
    )(a, b).reshape(H, T)


def _flash_bwd(name, q, k, v, do, scale, tq, exchange=None, v_head0=0):
    H, T, dqk = q.shape
    dva = v.shape[2]
    nq = T // tq

    def body(q_ref, k_ref, v_ref, do_ref, dq_ref, dk_ref, dv_ref, dk_sc, dv_sc):
        ki = pl.program_id(1)
        dk_sc[...] = jnp.zeros_like(dk_sc)
        dv_sc[...] = jnp.zeros_like(dv_sc)

        @pl.when(ki == 0)
        def _():
            dq_ref[...] = jnp.zeros_like(dq_ref)

        def chunk(i, masked):
            rows = _chunk_rows(i, tq)
            qb = q_ref[rows, :]
            dob = do_ref[rows, :]
            kb = k_ref[...]
            st = _dot(kb, qb, NT)
            if masked:
                st = jnp.where(_causal_mask(tq, False), st, NEG)
            pt = jnp.exp(st)
            dv_sc[...] += _dot(pt.astype(BF16), dob, NN)
            dst = (pt * _dot(v_ref[...], dob, NT)).astype(BF16)
            dk_sc[...] += _dot(dst, qb, NN)
            dq_ref[rows, :] += _dot(dst, kb, TN)

        def off_diagonal(i, carry):
            chunk(i, False)
            return carry

        chunk(ki, True)
        lax.fori_loop(ki + 1, nq, off_diagonal, 0)
        dk_ref[...] = dk_sc[...]
        dv_ref[...] = dv_sc[...]

        @pl.when(ki == nq - 1)
        def _():
            dq_ref[...] = dq_ref[...] * scale

    whole_q = pl.BlockSpec((None, T, dqk), lambda h, j: (h, 0, 0))
    k_spec = pl.BlockSpec((None, tq, dqk), lambda h, j: (h, j, 0))
    v_spec = pl.BlockSpec((None, tq, dva), lambda h, j: (h, j, 0))
    v_in_spec = pl.BlockSpec((None, tq, dva), lambda h, j: (h + v_head0, j, 0))
    return _call_carrying(
        body, name, (H, nq),
        in_specs=[whole_q, k_spec, v_in_spec, pl.BlockSpec((None, T, dva), lambda h, j: (h, 0, 0))],
        out_specs=[whole_q, k_spec, v_spec],
        out_shape=[jax.ShapeDtypeStruct((H, T, dqk), F32), jax.ShapeDtypeStruct((H, T, dqk), F32),
                   jax.ShapeDtypeStruct((H, T, dva), F32)],
        scratch_shapes=[pltpu.VMEM((tq, dqk), F32), pltpu.VMEM((tq, dva), F32)],
        operands=(q, k, v, do), exchange=exchange)


def _adamw_math(w, g, m, v):
    m = ADAM_B1 * m + (1.0 - ADAM_B1) * g
    v = ADAM_B2 * v + (1.0 - ADAM_B2) * (g * g)
    m_hat = m / (1.0 - ADAM_B1 ** ADAM_STEP)
    v_hat = v / (1.0 - ADAM_B2 ** ADAM_STEP)
    delta = -ADAM_LR * (m_hat / (jnp.sqrt(v_hat) + ADAM_EPS) + ADAM_WD * w)
    return delta, m, v


def _adamw(name, parts, w, m, v):
    P, R, C = parts.shape
    tr = _pick(R, (256, 128, 64, 32, 16, 8))

    def body(p_ref, w_ref, m_ref, v_ref, g_out, d_out, m_out, v_out):
        g = p_ref[0].astype(F32)
        for i in range(1, P):
            g = g + p_ref[i].astype(F32)
        delta, m_new, v_new = _adamw_math(w_ref[...], g, m_ref[...], v_ref[...])
        g_out[...] = g
        d_out[...] = delta
        m_out[...] = m_new
        v_out[...] = v_new

    blk = pl.BlockSpec((tr, C), lambda i: (i, 0))
    sds = jax.ShapeDtypeStruct((R, C), F32)
    return pl.pallas_call(
        body, name=name, grid=(R // tr,),
        in_specs=[pl.BlockSpec((P, tr, C), lambda i: (0, i, 0)), blk, blk, blk],
        out_specs=[blk] * 4, out_shape=[sds] * 4,
        compiler_params=_cparams(("parallel",)),
    )(parts, w, m, v)


def _my_position():
    return lax.axis_index("x"), lax.axis_index("y"), lax.axis_index("c")


def _slot(p):
    return 4 * p[0] + 2 * p[1] + p[2]


def _flip(p, k):
    return tuple((1 - p[i]) if (k >> (2 - i)) & 1 else p[i] for i in range(3))


def _allgather_weights(shards):
    n = len(shards)

    def body(*refs):
        ins = refs[:n]
        outs = refs[n:2 * n]
        send_sems, recv_sems, local_sems = refs[2 * n:]
        x, y, c = _my_position()
        me, sibling = (x, y, c), (x, y, 1 - c)
        chips = [(1 - x, y), (x, 1 - y), (1 - x, 1 - y)]

        def copy(a, k, block, to, src=None):
            dst = outs[a].at[_slot(block)]
            return pltpu.make_async_remote_copy(
                src_ref=dst if src is None else src, dst_ref=dst,
                send_sem=send_sems.at[7 * a + k], recv_sem=recv_sems.at[7 * a + k],
                device_id=to, device_id_type=MESH)

        started = []
        for a in range(n):
            mine = pltpu.make_async_copy(ins[a], outs[a].at[_slot(me)], local_sems.at[a])
            mine.start()
            started.append(mine)
        first = []
        for a in range(n):
            first.append(copy(a, 0, me, sibling, src=ins[a]))
            first += [copy(a, 1 + j, me, (*chip, c), src=ins[a]) for j, chip in enumerate(chips)]
        for cp in first:
            cp.start()
        passed = []
        for j, chip in enumerate(chips):
            for a in range(n):
                copy(a, 1 + j, (*chip, c), me).wait_recv()
                fwd = copy(a, 4 + j, (*chip, c), sibling)
                fwd.start()
                passed.append(fwd)
        for a in range(n):
            copy(a, 0, sibling, me).wait_recv()
            for j, chip in enumerate(chips):
                copy(a, 4 + j, (*chip, 1 - c), me).wait_recv()
        for cp in first + passed:
            cp.wait_send()
        for mine in started:
            mine.wait()

    hbm = pl.BlockSpec(memory_space=pl.ANY)
    return pl.pallas_call(
        body, name="allgather_weights",
        in_specs=[hbm] * n, out_specs=[hbm] * n,
        out_shape=[jax.ShapeDtypeStruct((N_DEV,) + s.shape, s.dtype) for s in shards],
        scratch_shapes=[pltpu.SemaphoreType.DMA((7 * n,)), pltpu.SemaphoreType.DMA((7 * n,)),
                        pltpu.SemaphoreType.DMA((n,))],
        compiler_params=pltpu.CompilerParams(has_side_effects=True),
    )(*shards)


def _exchange_copies(kind, x_in, x_out, send_sems, recv_sems, local_sems, receives=True):
    me = _my_position()
    mine = _slot(me)
    local, sends, recvs = [], [], []
    for a in range(len(x_in)):
        src = x_in[a] if kind == "gather" else x_in[a].at[mine]
        local.append(pltpu.make_async_copy(src, x_out[a].at[mine], local_sems.at[a]))
    for k in range(1, N_DEV):
        peer = _flip(me, k)
        theirs = _slot(peer)
        for a in range(len(x_in)):
            src = x_in[a] if kind == "gather" else x_in[a].at[theirs]
            ends = [(x_out[a].at[mine], sends)] + ([(x_out[a].at[theirs], recvs)] if receives else [])
            for dst, group in ends:
                group.append(pltpu.make_async_remote_copy(
                    src_ref=src, dst_ref=dst, send_sem=send_sems.at[7 * a + k - 1],
                    recv_sem=recv_sems.at[7 * a + k - 1], device_id=peer, device_id_type=MESH))
    return local, sends, recvs


def _exchange_out_shapes(kind, arrays):
    return [jax.ShapeDtypeStruct(((N_DEV,) + a.shape) if kind == "gather" else a.shape, a.dtype)
            for a in arrays]


def _exchange_sems(n):
    return [pltpu.SemaphoreType.DMA((7 * n,)), pltpu.SemaphoreType.DMA((7 * n,)),
            pltpu.SemaphoreType.DMA((n,))]


def _alltoall_grads(name, grads):
    n = len(grads)

    def body(*refs):
        local, sends, recvs = _exchange_copies("scatter", refs[:n], refs[n:2 * n], *refs[2 * n:])
        for cp in local + sends:
            cp.start()
        for cp in recvs:
            cp.wait_recv()
        for cp in sends:
            cp.wait_send()
        for cp in local:
            cp.wait()

    hbm = pl.BlockSpec(memory_space=pl.ANY)
    return pl.pallas_call(
        body, name=name,
        in_specs=[hbm] * n, out_specs=[hbm] * n,
        out_shape=_exchange_out_shapes("scatter", grads), scratch_shapes=_exchange_sems(n),
        compiler_params=pltpu.CompilerParams(has_side_effects=True),
    )(*grads)


def _call_carrying(body, name, grid, in_specs, out_specs, out_shape, scratch_shapes, operands, exchange):
    if exchange is None:
        out = pl.pallas_call(
            body, name=name, grid=grid, in_specs=in_specs, out_specs=out_specs, out_shape=out_shape,
            scratch_shapes=scratch_shapes, compiler_params=_cparams(("parallel",) + ("arbitrary",) * (len(grid) - 1)),
        )(*operands)
        return out, None
    kind, arrays = exchange
    n, n_in, n_out, n_sc = len(arrays), len(in_specs), len(out_specs), len(scratch_shapes)

    def full_body(*refs):
        ins, refs = refs[:n_in], refs[n_in:]
        x_in, refs = refs[:n], refs[n:]
        outs, refs = refs[:n_out], refs[n_out:]
        x_out, refs = refs[:n], refs[n:]
        scratch, sems = refs[:n_sc], refs[n_sc:]
        first = last = None
        for axis, size in enumerate(grid):
            at_start = pl.program_id(axis) == 0
            at_end = pl.program_id(axis) == size - 1
            first = at_start if first is None else jnp.logical_and(first, at_start)
            last = at_end if last is None else jnp.logical_and(last, at_end)

        @pl.when(first)
        def _():
            local, sends, _ = _exchange_copies(kind, x_in, x_out, *sems, receives=False)
            for cp in local + sends:
                cp.start()

        body(*ins, *outs, *scratch)

        @pl.when(last)
        def _():
            local, sends, recvs = _exchange_copies(kind, x_in, x_out, *sems)
            for cp in recvs:
                cp.wait_recv()
            for cp in sends:
                cp.wait_send()
            for cp in local:
                cp.wait()

    hbm = pl.BlockSpec(memory_space=pl.ANY)
    out = pl.pallas_call(
        full_body, name=name, grid=grid,
        in_specs=list(in_specs) + [hbm] * n, out_specs=list(out_specs) + [hbm] * n,
        out_shape=list(out_shape) + _exchange_out_shapes(kind, arrays),
        scratch_shapes=list(scratch_shapes) + _exchange_sems(n),
        compiler_params=pltpu.CompilerParams(dimension_semantics=("arbitrary",) * len(grid),
                                             vmem_limit_bytes=VMEM_LIMIT_BYTES, has_side_effects=True),
    )(*operands, *arrays)
    return out[:n_out], out[n_out:]


def _allreduce_small(v):
    R, C = v.shape

    def body(v_ref, o_ref, buf, send_sems, recv_sems):
        me = _my_position()
        buf[_slot(me)] = v_ref[...]
        sends = []
        for k in range(1, N_DEV):
            peer = _flip(me, k)
            cp = pltpu.make_async_remote_copy(
                src_ref=v_ref, dst_ref=buf.at[_slot(me)],
                send_sem=send_sems.at[k - 1], recv_sem=recv_sems.at[k - 1],
                device_id=peer, device_id_type=MESH)
            cp.start()
            sends.append(cp)
        for k in range(1, N_DEV):
            peer = _flip(me, k)
            pltpu.make_async_remote_copy(
                src_ref=v_ref, dst_ref=buf.at[_slot(peer)],
                send_sem=send_sems.at[k - 1], recv_sem=recv_sems.at[k - 1],
                device_id=peer, device_id_type=MESH).wait_recv()
        for cp in sends:
            cp.wait_send()
        tot = buf[0]
        for s in range(1, N_DEV):
            tot = tot + buf[s]
        o_ref[...] = tot

    vm = pl.BlockSpec(memory_space=pltpu.VMEM)
    return pl.pallas_call(
        body, name="allreduce_small",
        in_specs=[vm], out_specs=vm, out_shape=jax.ShapeDtypeStruct((R, C), F32),
        scratch_shapes=[pltpu.VMEM((N_DEV, R, C), F32), pltpu.SemaphoreType.DMA((7,)),
                        pltpu.SemaphoreType.DMA((7,))],
        compiler_params=pltpu.CompilerParams(has_side_effects=True),
    )(v)


def _to_heads(t, heads):
    T = t.shape[0]
    return t.reshape(T, heads, t.shape[1] // heads).transpose(1, 0, 2)


def _from_heads(t):
    H, T, d = t.shape
    return t.transpose(1, 0, 2).reshape(T, H * d)


def _widen(t, width, ones_at=None, pieces_at=None, pieces=None):
    out = jnp.pad(t, ((0, 0), (0, 0), (0, width - t.shape[-1])))
    lane = lax.broadcasted_iota(jnp.int32, (1, 1, width), 2)
    if ones_at is not None:
        out = jnp.where((lane >= ones_at) & (lane < ones_at + 3), jnp.ones((), BF16), out)
    if pieces_at is not None:
        for i in range(3):
            out = jnp.where(lane == pieces_at + i, pieces[i][:, :, None], out)
    return out


def _split3(t):
    hi = lax.reduce_precision(t, 8, 7)
    r = t - hi
    mid = lax.reduce_precision(r, 8, 7)
    lo = lax.reduce_precision(r - mid, 8, 7)
    return hi.astype(BF16), mid.astype(BF16), lo.astype(BF16)


def _pad_cols(t, n):
    return jnp.pad(t, ((0, 0), (0, n - t.shape[1])))


def _pack_small(mix, ffn, kv, fin, kva, qa, bf, last):
    row6 = jnp.concatenate([kva.reshape(-1), qa.reshape(-1), bf.reshape(-1),
                            jnp.zeros((D_MODEL - KV_LORA - Q_LORA - FOX_HEADS,), F32)])
    return jnp.stack([mix[0], mix[1], ffn[0], ffn[1], kv.reshape(-1), fin.reshape(-1), row6, last])


def _unpack_small(p):
    mix = p[0:2]
    ffn = p[2:4]
    kv = p[4]
    fin = p[5]
    kva = p[6, :KV_LORA]
    qa = p[6, KV_LORA:KV_LORA + Q_LORA].reshape(1, Q_LORA)
    bf = p[6, KV_LORA + Q_LORA:KV_LORA + Q_LORA + FOX_HEADS].reshape(1, FOX_HEADS)
    return mix, ffn, bf, kv, kva, qa, fin


def _mlp_fwd(tag, xin, g, w_up, w_down):
    h = _rms(f"{tag}_norm", xin, g, BF16)

    def act(acc):
        r = jnp.maximum(acc, 0.0)
        return acc, r * r

    u, a = _mm(f"{tag}_up", h, w_up, "nn", (BF16, BF16), epi=act)
    xout = _mm(f"{tag}_down", a, w_down, "nn", (F32,), epi=lambda acc, r: (acc + r,), extras=(xin,))
    return xout, (h, u, a)


def _mlp_bwd(tag, gout, xin, g, w_up, w_down, saved):
    h, u, a = saved
    dw_down = _mm_tn(f"{tag}_dwdown", a, gout)
    du = _mm(f"{tag}_du", gout, w_down, "nt", (BF16,),
             epi=lambda acc, uu: (acc * (2.0 * jnp.maximum(uu.astype(F32), 0.0)),), extras=(u,))
    dw_up = _mm_tn(f"{tag}_dwup", h, du)
    gin, dg = _mm_rms_bwd(f"{tag}_dh_norm_bwd", du, w_up, xin, g, gout)
    return gin, dg, dw_up, dw_down


def kernel(x, norm_mix_g, norm_ffn_g, fox_w_in, fox_b_f, fox_w_out, kv_norm_g, mla_w_kv_a, mla_kv_a_norm_g, mla_w_kv_b, mla_w_q_a, mla_q_a_norm_g, mla_w_q_b, mla_w_out, ffn_w_up, ffn_w_down, final_norm_g, loss_target, m_norm_mix_g, m_norm_ffn_g, m_fox_w_in, m_fox_b_f, m_fox_w_out, m_kv_norm_g, m_mla_w_kv_a, m_mla_kv_a_norm_g, m_mla_w_kv_b, m_mla_w_q_a, m_mla_q_a_norm_g, m_mla_w_q_b, m_mla_w_out, m_ffn_w_up, m_ffn_w_down, m_final_norm_g, v_norm_mix_g, v_norm_ffn_g, v_fox_w_in, v_fox_b_f, v_fox_w_out, v_kv_norm_g, v_mla_w_kv_a, v_mla_kv_a_norm_g, v_mla_w_kv_b, v_mla_w_q_a, v_mla_q_a_norm_g, v_mla_w_q_b, v_mla_w_out, v_ffn_w_up, v_ffn_w_down, v_final_norm_g):
    T = x.shape[1]
    D = D_MODEL
    tq = 512 if T >= 2048 else 128
    x0 = x[0]
    tgt = loss_target[0]

    gat_fox = _allgather_weights([fox_w_in[0].astype(BF16), fox_w_out[0].astype(BF16)])
    later_shards = [s.astype(BF16) for s in (mla_w_kv_a, mla_w_kv_b, mla_w_q_a[0], mla_w_q_b[0],
                                             mla_w_out[0], ffn_w_up, ffn_w_down)]
    w_in = gat_fox[0].transpose(1, 0, 2).reshape(D, 3 * D + FOX_HEADS)
    w_qkv = w_in[:, :3 * D]
    w_f = _pad_cols(w_in[:, 3 * D:], 128)
    w_fo = gat_fox[1].reshape(D, D)
    g_mix0, g_mix1 = norm_mix_g[0:1], norm_mix_g[1:2]
    g_ffn0, g_ffn1 = norm_ffn_g[0:1], norm_ffn_g[1:2]
    g_kv = kv_norm_g.reshape(1, D)
    g_kva = mla_kv_a_norm_g.reshape(1, KV_LORA)
    g_qa = mla_q_a_norm_g.reshape(1, Q_LORA)
    g_fin = final_norm_g.reshape(1, D)

    inv = 1.0 / (ROPE_BASE ** (jnp.arange(0, QK_ROPE, 2, dtype=F32) / QK_ROPE))
    ang = jnp.arange(T, dtype=F32)[:, None] * inv[None, :]
    cos, sin = jnp.cos(ang), jnp.sin(ang)
    cos2 = jnp.concatenate([cos, cos], axis=-1)
    sgn_sin = jnp.concatenate([-sin, sin], axis=-1)

    h0 = _rms("l0_mix_norm", x0, g_mix0, BF16)
    fl_pad = _mm("fox_gate_logit", h0, w_f, "nn", (F32,))
    fl = fl_pad[:, :FOX_HEADS].T
    b_f = fox_b_f.reshape(FOX_HEADS, 1)
    cgate = _gate_cumsum("fox_gate_scan", fl, b_f, tq)
    fox_scale = FOX_HEAD_DIM ** -0.5
    col_scale = jnp.where(jnp.arange(3 * D) < D, fox_scale, 1.0).astype(BF16)
    tail = jnp.arange(FOX_AUG - FOX_HEAD_DIM)
    ones_q = (tail < 3).astype(F32)
    consts_k = ((tail >= 4) & (tail < 7)).astype(F32) + (tail == 3).astype(F32) * (1.0 / fox_scale)
    tails = jnp.broadcast_to(jnp.stack([ones_q, consts_k, ones_q])[:, None, None, :],
                             (3, FOX_HEADS, 1, FOX_AUG - FOX_HEAD_DIM)).reshape(3 * FOX_HEADS, 1, -1)
    qkv_h = _mm_head_slabs("fox_qkv", h0, w_qkv * col_scale, tails, FOX_HEAD_DIM, BF16)
    fk_aug = _widen(qkv_h[FOX_HEADS:2 * FOX_HEADS], FOX_AUG, pieces_at=FOX_HEAD_DIM,
                    pieces=_split3(-cgate))
    (fo, flse), gat = _flash_fwd("fox_attn", qkv_h, fk_aug, qkv_h, FOX_HEAD_DIM, tq,
                                 exchange=("gather", later_shards), q_head0=0, v_head0=2 * FOX_HEADS)
    w_kva = _pad_cols(gat[0].reshape(D, KV_LORA + QK_ROPE), KV_A_PAD)
    w_kvb_h = gat[1]
    w_qa = gat[2].reshape(D, Q_LORA)
    w_qb_h = gat[3]
    w_mo = gat[4].reshape(D, D)
    w_up = gat[5].transpose(1, 2, 0, 3).reshape(2, D, D_FF)
    w_down = gat[6].transpose(1, 0, 2, 3).reshape(2, D_FF, D)
    fctx = _from_heads(fo).astype(BF16)
    x1 = _mm("fox_out", fctx, w_fo, "nn", (F32,), epi=lambda acc, r: (acc + r,), extras=(x0,))
    x2, mlp0 = _mlp_fwd("l0_ffn", x1, g_ffn0, w_up[0], w_down[0])

    src = _rms("kv_norm", x2, g_kv, BF16)
    kva = _mm("kv_a", src, w_kva, "nn", (F32,))
    kva_lat = kva[:, :KV_LORA]
    c_kv = _rms("kv_a_norm", kva_lat, g_kva, BF16)
    k_rope = _rope("k_rope", kva[:, KV_LORA:KV_LORA + QK_ROPE][None], cos2, sgn_sin, BF16)
    kvb_h = _mm_heads("kv_b", c_kv, w_kvb_h, BF16)
    mk = jnp.concatenate([kvb_h[:, :, :QK_NOPE],
                          jnp.broadcast_to(k_rope, (MLA_HEADS, T, QK_ROPE))], axis=-1)
    mv = kvb_h[:, :, QK_NOPE:]

    h1 = _rms("l1_mix_norm", x2, g_mix1, BF16)
    qa = _mm("q_a", h1, w_qa, "nn", (F32,))
    c_q = _rms("q_a_norm", qa, g_qa, BF16)
    mla_scale = (QK_NOPE + QK_ROPE) ** -0.5
    mq = _mla_q_proj("q_b", c_q, w_qb_h, cos2, sgn_sin, mla_scale)
    mv_aug = _widen(mv, MLA_AUG, ones_at=V_HEAD)
    (mo, mlse), _ = _flash_fwd("mla_attn", mq, mk, mv_aug, V_HEAD, tq)
    mctx = _from_heads(mo).astype(BF16)
    x3 = _mm("mla_out", mctx, w_mo, "nn", (F32,), epi=lambda acc, r: (acc + r,), extras=(x2,))
    x4, mlp1 = _mlp_fwd("l1_ffn", x3, g_ffn1, w_up[1], w_down[1])

    g4, dg_fin, loss_vec = _loss_head("loss_head", x4, g_fin, tgt)

    g3, dg_ffn1, dw_up1, dw_down1 = _mlp_bwd("l1_ffn", g4, x3, g_ffn1, w_up[1], w_down[1], mlp1)

    dw_mo = _mm_tn("mla_out_dw", mctx, g3)
    dmo = _to_heads(_mm("mla_out_dx", g3, w_mo, "nt", (BF16,)), MLA_HEADS)
    mdelta = _row_dot("mla_delta", mo, dmo)
    dqk = QK_NOPE + QK_ROPE
    mq_bwd = _widen(mq, MLA_AUG, pieces_at=dqk, pieces=_split3(-mlse))
    mk_bwd = _widen(mk, MLA_AUG, ones_at=dqk)
    mdo_aug = _widen(dmo, MLA_AUG, pieces_at=V_HEAD, pieces=_split3(-mdelta))
    (mdq, mdk, mdv), _ = _flash_bwd("mla_attn_bwd", mq_bwd, mk_bwd, mv_aug, mdo_aug, mla_scale, tq)
    mdq = mdq[:, :, :dqk]
    mdk = mdk[:, :, :dqk]
    mdv = mdv[:, :, :V_HEAD]
    dq_rope = _rope_bwd("q_rope_bwd", mdq[:, :, QK_NOPE:], cos2, sgn_sin, False)
    dqf_h = jnp.concatenate([mdq[:, :, :QK_NOPE], dq_rope], axis=-1)
    dw_qb_h = _mm_heads_dw("q_b_dw", c_q, dqf_h)
    dc_q = _mm_heads_dx("q_b_dx", dqf_h, w_qb_h)
    dqa, dg_qa = _rms_bwd("q_a_norm_bwd", qa, g_qa, dc_q)
    dw_qa = _mm_tn("q_a_dw", h1, dqa)
    g2a, dg_mix1 = _mm_rms_bwd("q_a_dx_norm_bwd", dqa, w_qa, x2, g_mix1, g3)

    dk_rope = _rope_bwd("k_rope_bwd", mdk[:, :, QK_NOPE:], cos2, sgn_sin, True)
    dkvb_h = jnp.concatenate([mdk[:, :, :QK_NOPE], mdv], axis=-1)
    dw_kvb_h = _mm_heads_dw("kv_b_dw", c_kv, dkvb_h)
    dc_kv = _mm_heads_dx("kv_b_dx", dkvb_h, w_kvb_h)
    dkva_lat, dg_kva = _rms_bwd("kv_a_norm_bwd", kva_lat, g_kva, dc_kv)
    dkva = _pad_cols(jnp.concatenate([dkva_lat, dk_rope], axis=-1), KV_A_PAD)
    dw_kva = _mm_tn("kv_a_dw", src, dkva)[:, :KV_LORA + QK_ROPE]
    g2, dg_kv = _mm_rms_bwd("kv_a_dx_norm_bwd", dkva, w_kva, x2, g_kv, g2a)

    g1, dg_ffn0, dw_up0, dw_down0 = _mlp_bwd("l0_ffn", g2, x1, g_ffn0, w_up[0], w_down[0], mlp0)

    dw_fo = _mm_tn("fox_out_dw", fctx, g1)
    dfo = _to_heads(_mm("fox_out_dx", g1, w_fo, "nt", (BF16,)), FOX_HEADS)
    fdelta = _row_dot("fox_delta", fo, dfo)
    fq_bwd = _widen(qkv_h[:FOX_HEADS], FOX_AUG, pieces_at=FOX_HEAD_DIM + 4, pieces=_split3(-flse))
    fdo_aug = _widen(dfo, FOX_AUG, pieces_at=FOX_HEAD_DIM, pieces=_split3(-fdelta))
    dw_up = jnp.stack([dw_up0, dw_up1])
    dw_down = jnp.stack([dw_down0, dw_down1])
    early = [
        dw_fo.reshape(N_DEV, D // N_DEV, D),
        dw_kva.reshape(N_DEV, D // N_DEV, KV_LORA + QK_ROPE),
        dw_kvb_h,
        dw_qa.reshape(N_DEV, D // N_DEV, Q_LORA),
        dw_qb_h,
        dw_mo.reshape(N_DEV, D // N_DEV, D),
        dw_up.reshape(2, D, N_DEV, -1).transpose(2, 0, 1, 3),
        dw_down.reshape(2, N_DEV, D_FF // N_DEV, D).transpose(1, 0, 2, 3),
    ]
    (fdq_aug, fdk_aug, fdv_aug), early_parts = _flash_bwd(
        "fox_attn_bwd", fq_bwd, fk_aug, qkv_h, fdo_aug, fox_scale, tq,
        exchange=("scatter", [g.astype(BF16) for g in early]), v_head0=2 * FOX_HEADS)
    fdq = fdq_aug[:, :, :FOX_HEAD_DIM]
    fdk = fdk_aug[:, :, :FOX_HEAD_DIM]
    fdv = fdv_aug[:, :, :FOX_HEAD_DIM]
    dfl, db_f = _gate_cumsum_bwd("fox_gate_scan_bwd", fdq_aug[:, :, FOX_HEAD_DIM + 3],
                                 fdk_aug[:, :, FOX_HEAD_DIM], fl, b_f, tq)
    dqkv = jnp.concatenate([_from_heads(fdq), _from_heads(fdk), _from_heads(fdv)], axis=-1).astype(BF16)
    dfl_pad = _pad_cols(dfl.T, 128)
    dw_qkv = _mm_tn("fox_qkv_dw", h0, dqkv)
    dw_f = _mm_tn("fox_gate_dw", h0, dfl_pad)[:, :FOX_HEADS]
    dw_in = jnp.concatenate([dw_qkv, dw_f], axis=-1)
    dh0a = _mm("fox_gate_dx", dfl_pad, w_f, "nt", (F32,))
    grad_x, dg_mix0 = _mm_rms_bwd("fox_qkv_dx_norm_bwd", dqkv, w_qkv, x0, g_mix0, g1, add=dh0a)

    late = dw_in.reshape(D, N_DEV, -1).transpose(1, 0, 2).astype(BF16)
    parts = list(_alltoall_grads("alltoall_fox_w_in", [late])) + list(early_parts)

    names = ["fox_w_in", "fox_w_out", "mla_w_kv_a", "mla_w_kv_b", "mla_w_q_a", "mla_w_q_b",
             "mla_w_out", "ffn_w_up", "ffn_w_down"]
    moms = [m_fox_w_in, m_fox_w_out, m_mla_w_kv_a, m_mla_w_kv_b, m_mla_w_q_a, m_mla_w_q_b,
            m_mla_w_out, m_ffn_w_up, m_ffn_w_down]
    vars_ = [v_fox_w_in, v_fox_w_out, v_mla_w_kv_a, v_mla_w_kv_b, v_mla_w_q_a, v_mla_w_q_b,
             v_mla_w_out, v_ffn_w_up, v_ffn_w_down]
    full = [fox_w_in, fox_w_out, mla_w_kv_a, mla_w_kv_b, mla_w_q_a, mla_w_q_b, mla_w_out,
            ffn_w_up, ffn_w_down]
    big = {}
    for nm, p, w, m, v in zip(names, parts, full, moms, vars_):
        C = w.shape[-1]
        res = _adamw(f"adamw_{nm}", p.reshape(N_DEV, -1, C), w.reshape(-1, C), m.reshape(-1, C),
                     v.reshape(-1, C))
        big[nm] = [r.reshape(w.shape) for r in res]

    zrow = jnp.zeros((D,), F32)
    g_small = _pack_small(jnp.concatenate([dg_mix0, dg_mix1]), jnp.concatenate([dg_ffn0, dg_ffn1]),
                          dg_kv, dg_fin, dg_kva, dg_qa, db_f, zrow.at[0].set(loss_vec[0, 0]))
    tot_small = _allreduce_small(g_small)
    w_small = _pack_small(norm_mix_g, norm_ffn_g, kv_norm_g, final_norm_g, mla_kv_a_norm_g,
                          mla_q_a_norm_g, fox_b_f, zrow)
    m_small = _pack_small(m_norm_mix_g, m_norm_ffn_g, m_kv_norm_g, m_final_norm_g, m_mla_kv_a_norm_g,
                          m_mla_q_a_norm_g, m_fox_b_f, zrow)
    v_small = _pack_small(v_norm_mix_g, v_norm_ffn_g, v_kv_norm_g, v_final_norm_g, v_mla_kv_a_norm_g,
                          v_mla_q_a_norm_g, v_fox_b_f, zrow)
    small = _adamw("adamw_small", tot_small[None], w_small, m_small, v_small)
    loss = tot_small[7, 0]
    small = [_unpack_small(s) for s in small]

    def ordered(i):
        mix, ffn, bf, kv, kva, qa, fin = small[i]
        return [mix, ffn, big["fox_w_in"][i], bf, big["fox_w_out"][i], kv, big["mla_w_kv_a"][i], kva,
                big["mla_w_kv_b"][i], big["mla_w_q_a"][i], qa, big["mla_w_q_b"][i],
                big["mla_w_out"][i], big["ffn_w_up"][i], big["ffn_w_down"][i], fin]

    return (loss, grad_x[None], *ordered(0), *ordered(1), *ordered(2), *ordered(3))
```

```python
import functools
import math

import jax
import jax.numpy as jnp
from jax import lax
from jax.experimental import pallas as pl
from jax.experimental.pallas import tpu as pltpu

F32 = jnp.float32
BF16 = jnp.bfloat16
MESH = pl.DeviceIdType.MESH

N_DEV = 8
D_MODEL = 1024
FOX_HEADS = 16
FOX_HEAD_DIM = 64
FOX_AUG = 128
MLA_AUG = 256
MLA_HEADS = 8
QK_NOPE = 128
QK_ROPE = 64
V_HEAD = 128
Q_LORA = 384
KV_LORA = 256
KV_A_PAD = 384
D_FF = 4096
ROPE_BASE = 10000.0
EPS = 1e-6
NEG = -1e30

ADAM_LR = 0.001
ADAM_B1 = 0.9
ADAM_B2 = 0.999
ADAM_EPS = 1e-08
ADAM_WD = 0.01
ADAM_STEP = 10

VMEM_LIMIT_BYTES = 56 * 1024 * 1024

NN = (((1,), (0,)), ((), ()))
NT = (((1,), (1,)), ((), ()))
TN = (((0,), (0,)), ((), ()))
_FORMS = {"nn": NN, "nt": NT}


def _cparams(sem=None):
    return pltpu.CompilerParams(dimension_semantics=sem, vmem_limit_bytes=VMEM_LIMIT_BYTES)


def _pick(n, cands):
    for c in cands:
        if c <= n and n % c == 0:
            return c
    return n


def _dot(a, b, dims):
    return lax.dot_general(a, b, dims, preferred_element_type=F32)


def _mm(name, a, b, form, out_dtypes, epi=None, extras=(), tm=1024, tn=None):
    M, K = a.shape
    N = b.shape[1] if form == "nn" else b.shape[0]
    tm = _pick(M, (tm, 512, 256, 128))
    tn = _pick(N, (tn or (1024 if K <= 1024 else 512), 512, 384, 256, 128))
    n_ex = len(extras)
    n_out = len(out_dtypes)
    cast_once = a.dtype != BF16

    def body(*refs):
        a_ref, b_ref = refs[0], refs[1]
        ex = refs[2:2 + n_ex]
        outs = refs[2 + n_ex:2 + n_ex + n_out]
        if cast_once:
            a_sc = refs[2 + n_ex + n_out]

            @pl.when(pl.program_id(1) == 0)
            def _():
                a_sc[...] = a_ref[...].astype(BF16)

            av = a_sc[...]
        else:
            av = a_ref[...]
        acc = _dot(av, b_ref[...].astype(BF16), _FORMS[form])
        res = epi(acc, *[e[...] for e in ex]) if epi is not None else (acc,)
        for o_ref, r in zip(outs, res):
            o_ref[...] = r.astype(o_ref.dtype)

    if form == "nn":
        b_spec = pl.BlockSpec((K, tn), lambda i, j: (0, j))
    else:
        b_spec = pl.BlockSpec((tn, K), lambda i, j: (j, 0))
    tile = pl.BlockSpec((tm, tn), lambda i, j: (i, j))
    out = pl.pallas_call(
        body, name=name, grid=(M // tm, N // tn),
        in_specs=[pl.BlockSpec((tm, K), lambda i, j: (i, 0)), b_spec] + [tile] * n_ex,
        out_specs=[tile] * n_out,
        out_shape=[jax.ShapeDtypeStruct((M, N), dt) for dt in out_dtypes],
        scratch_shapes=[pltpu.VMEM((tm, K), BF16)] if cast_once else [],
        compiler_params=_cparams(("parallel", "arbitrary")),
    )(a, b, *extras)
    return out if n_out > 1 else out[0]


def _mm_tn(name, a, b):
    T, Ka = a.shape
    N = b.shape[1]
    tk = _pick(Ka, (1024, 512, 384, 256, 128))
    tn = _pick(N, (1024, 768, 512, 384, 256, 128))
    tt = _pick(T, (1024, 512, 256, 128))

    def body(a_ref, b_ref, o_ref):
        @pl.when(pl.program_id(2) == 0)
        def _():
            o_ref[...] = jnp.zeros_like(o_ref)

        o_ref[...] += _dot(a_ref[...].astype(BF16), b_ref[...].astype(BF16), TN)

    return pl.pallas_call(
        body, name=name, grid=(Ka // tk, N // tn, T // tt),
        in_specs=[pl.BlockSpec((tt, tk), lambda i, j, t: (t, i)),
                  pl.BlockSpec((tt, tn), lambda i, j, t: (t, j))],
        out_specs=pl.BlockSpec((tk, tn), lambda i, j, t: (i, j)),
        out_shape=jax.ShapeDtypeStruct((Ka, N), F32),
        compiler_params=_cparams(("parallel", "parallel", "arbitrary")),
    )(a, b)


def _mm_heads(name, a, w, out_dtype):
    T, K = a.shape
    H, _, N = w.shape
    tm = _pick(T, (1024, 512, 256, 128))

    def body(a_ref, w_ref, o_ref):
        av = a_ref[...].astype(BF16)
        for s in range(H):
            o_ref[s] = _dot(av, w_ref[s].astype(BF16), NN).astype(o_ref.dtype)

    return pl.pallas_call(
        body, name=name, grid=(T // tm,),
        in_specs=[pl.BlockSpec((tm, K), lambda i: (i, 0)), pl.BlockSpec((H, K, N), lambda i: (0, 0, 0))],
        out_specs=pl.BlockSpec((H, tm, N), lambda i: (0, i, 0)),
        out_shape=jax.ShapeDtypeStruct((H, T, N), out_dtype),
        compiler_params=_cparams(("parallel",)),
    )(a, w)


def _mm_head_slabs(name, a, w, tails, head_dim, out_dtype, heads_per_step=8):
    T, K = a.shape
    S, _, tail = tails.shape
    hb = heads_per_step
    tm = _pick(T, (1024, 512, 256, 128))

    def body(a_ref, w_ref, t_ref, o_ref):
        acc = _dot(a_ref[...].astype(BF16), w_ref[...].astype(BF16), NN)
        for s in range(hb):
            slab = jnp.concatenate([acc[:, s * head_dim:(s + 1) * head_dim],
                                    jnp.broadcast_to(t_ref[s], (tm, tail))], axis=-1)
            o_ref[s] = slab.astype(o_ref.dtype)

    return pl.pallas_call(
        body, name=name, grid=(T // tm, S // hb),
        in_specs=[pl.BlockSpec((tm, K), lambda i, j: (i, 0)),
                  pl.BlockSpec((K, hb * head_dim), lambda i, j: (0, j)),
                  pl.BlockSpec((hb, 1, tail), lambda i, j: (j, 0, 0))],
        out_specs=pl.BlockSpec((hb, tm, head_dim + tail), lambda i, j: (j, i, 0)),
        out_shape=jax.ShapeDtypeStruct((S, T, head_dim + tail), out_dtype),
        compiler_params=_cparams(("parallel", "arbitrary")),
    )(a, w, tails)


def _mm_heads_dw(name, a, g):
    T, K = a.shape
    H, _, N = g.shape
    tt = _pick(T, (1024, 512, 256, 128))

    def body(a_ref, g_ref, o_ref):
        @pl.when(pl.program_id(0) == 0)
        def _():
            o_ref[...] = jnp.zeros_like(o_ref)

        av = a_ref[...].astype(BF16)
        for s in range(H):
            o_ref[s] += _dot(av, g_ref[s].astype(BF16), TN)

    return pl.pallas_call(
        body, name=name, grid=(T // tt,),
        in_specs=[pl.BlockSpec((tt, K), lambda t: (t, 0)), pl.BlockSpec((H, tt, N), lambda t: (0, t, 0))],
        out_specs=pl.BlockSpec((H, K, N), lambda t: (0, 0, 0)),
        out_shape=jax.ShapeDtypeStruct((H, K, N), F32),
        compiler_params=_cparams(("arbitrary",)),
    )(a, g)


def _mm_heads_dx(name, g, w):
    H, T, N = g.shape
    K = w.shape[1]
    tm = _pick(T, (512, 256, 128))

    def body(g_ref, w_ref, o_ref):
        acc = _dot(g_ref[0].astype(BF16), w_ref[0].astype(BF16), NT)
        for s in range(1, H):
            acc = acc + _dot(g_ref[s].astype(BF16), w_ref[s].astype(BF16), NT)
        o_ref[...] = acc

    return pl.pallas_call(
        body, name=name, grid=(T // tm,),
        in_specs=[pl.BlockSpec((H, tm, N), lambda i: (0, i, 0)), pl.BlockSpec((H, K, N), lambda i: (0, 0, 0))],
        out_specs=pl.BlockSpec((tm, K), lambda i: (i, 0)),
        out_shape=jax.ShapeDtypeStruct((T, K), F32),
        compiler_params=_cparams(("parallel",)),
    )(g, w)


def _rms(name, x, g, out_dtype):
    T, D = x.shape
    tm = _pick(T, (1024, 512, 256, 128))

    def body(x_ref, g_ref, o_ref):
        xf = x_ref[...]
        r = lax.rsqrt(jnp.mean(xf * xf, axis=-1, keepdims=True) + EPS)
        o_ref[...] = (xf * r * g_ref[...]).astype(o_ref.dtype)

    return pl.pallas_call(
        body, name=name, grid=(T // tm,),
        in_specs=[pl.BlockSpec((tm, D), lambda i: (i, 0)), pl.BlockSpec((1, D), lambda i: (0, 0))],
        out_specs=pl.BlockSpec((tm, D), lambda i: (i, 0)),
        out_shape=jax.ShapeDtypeStruct((T, D), out_dtype),
        compiler_params=_cparams(("parallel",)),
    )(x, g)


def _rms_bwd(name, x, g, dh, dres=None):
    T, D = x.shape
    tm = _pick(T, (512, 256, 128))
    has_res = dres is not None

    def body(*refs):
        if has_res:
            x_ref, g_ref, dh_ref, dres_ref, dx_ref, dg_ref = refs
        else:
            x_ref, g_ref, dh_ref, dx_ref, dg_ref = refs

        @pl.when(pl.program_id(0) == 0)
        def _():
            dg_ref[...] = jnp.zeros_like(dg_ref)

        xf = x_ref[...]
        r = lax.rsqrt(jnp.mean(xf * xf, axis=-1, keepdims=True) + EPS)
        xhat = xf * r
        dy = dh_ref[...].astype(F32)
        dxh = dy * g_ref[...]
        dx = r * (dxh - xhat * jnp.mean(dxh * xhat, axis=-1, keepdims=True))
        if has_res:
            dx = dx + dres_ref[...]
        dx_ref[...] = dx
        dg_ref[...] += jnp.sum(dy * xhat, axis=0, keepdims=True)

    row = pl.BlockSpec((tm, D), lambda i: (i, 0))
    vec = pl.BlockSpec((1, D), lambda i: (0, 0))
    ins = [x, g, dh] + ([dres] if has_res else [])
    return pl.pallas_call(
        body, name=name, grid=(T // tm,),
        in_specs=[row, vec, row] + ([row] if has_res else []),
        out_specs=[row, vec],
        out_shape=[jax.ShapeDtypeStruct((T, D), F32), jax.ShapeDtypeStruct((1, D), F32)],
        compiler_params=_cparams(("arbitrary",)),
    )(*ins)


def _mm_rms_bwd(name, a, b, x, g, dres, add=None, exchange=None):
    T, K = a.shape
    D = b.shape[0]
    tm = _pick(T, (512, 256, 128))
    has_add = add is not None

    def body(*refs):
        a_ref, b_ref, x_ref, g_ref, dres_ref = refs[:5]
        dx_ref, dg_ref = refs[-2:]

        @pl.when(pl.program_id(0) == 0)
        def _():
            dg_ref[...] = jnp.zeros_like(dg_ref)

        dy = _dot(a_ref[...].astype(BF16), b_ref[...].astype(BF16), NT)
        if has_add:
            dy = dy + refs[5][...]
        xf = x_ref[...]
        r = lax.rsqrt(jnp.mean(xf * xf, axis=-1, keepdims=True) + EPS)
        xhat = xf * r
        dxh = dy * g_ref[...]
        dx_ref[...] = r * (dxh - xhat * jnp.mean(dxh * xhat, axis=-1, keepdims=True)) + dres_ref[...]
        dg_ref[...] += jnp.sum(dy * xhat, axis=0, keepdims=True)

    row = pl.BlockSpec((tm, D), lambda i: (i, 0))
    vec = pl.BlockSpec((1, D), lambda i: (0, 0))
    (dx, dg), exchanged = _call_carrying(
        body, name, (T // tm,),
        in_specs=[pl.BlockSpec((tm, K), lambda i: (i, 0)), pl.BlockSpec((D, K), lambda i: (0, 0)), row, vec, row]
        + ([row] if has_add else []),
        out_specs=[row, vec],
        out_shape=[jax.ShapeDtypeStruct((T, D), F32), jax.ShapeDtypeStruct((1, D), F32)],
        scratch_shapes=[], operands=(a, b, x, g, dres) + ((add,) if has_add else ()), exchange=exchange,
        sequential=True)
    return (dx, dg) if exchange is None else (dx, dg, exchanged)


def _loss_head(name, x, g, tgt):
    T, D = x.shape
    tm = _pick(T, (512, 256, 128))

    def body(x_ref, g_ref, t_ref, dx_ref, dg_ref, loss_ref):
        @pl.when(pl.program_id(0) == 0)
        def _():
            dg_ref[...] = jnp.zeros_like(dg_ref)
            loss_ref[...] = jnp.zeros_like(loss_ref)

        xf = x_ref[...]
        r = lax.rsqrt(jnp.mean(xf * xf, axis=-1, keepdims=True) + EPS)
        xhat = xf * r
        gv = g_ref[...]
        err = xhat * gv - t_ref[...]
        row_loss = jnp.mean(err * err, axis=-1, keepdims=True)
        loss_ref[...] += 0.5 * jnp.sum(row_loss, axis=0, keepdims=True)
        dy = err * (1.0 / D)
        dxh = dy * gv
        dx_ref[...] = r * (dxh - xhat * jnp.mean(dxh * xhat, axis=-1, keepdims=True))
        dg_ref[...] += jnp.sum(dy * xhat, axis=0, keepdims=True)

    row = pl.BlockSpec((tm, D), lambda i: (i, 0))
    vec = pl.BlockSpec((1, D), lambda i: (0, 0))
    return pl.pallas_call(
        body, name=name, grid=(T // tm,),
        in_specs=[row, vec, row],
        out_specs=[row, vec, pl.BlockSpec((1, 128), lambda i: (0, 0))],
        out_shape=[jax.ShapeDtypeStruct((T, D), F32), jax.ShapeDtypeStruct((1, D), F32),
                   jax.ShapeDtypeStruct((1, 128), F32)],
        compiler_params=_cparams(("arbitrary",)),
    )(x, g, tgt)


def _swap_halves(t):
    half = t.shape[-1] // 2
    return jnp.concatenate([t[:, half:], t[:, :half]], axis=-1)


def _rope(name, t, cos2, sgn_sin, out_dtype):
    H, T, R = t.shape
    tm = _pick(T, (1024, 512, 256, 128))

    def body(t_ref, c_ref, s_ref, o_ref):
        tf = t_ref[...].astype(F32)
        o_ref[...] = (tf * c_ref[...] + _swap_halves(tf) * s_ref[...]).astype(o_ref.dtype)

    slab = pl.BlockSpec((None, tm, R), lambda h, i: (h, i, 0))
    tab = pl.BlockSpec((tm, R), lambda h, i: (i, 0))
    return pl.pallas_call(
        body, name=name, grid=(H, T // tm),
        in_specs=[slab, tab, tab], out_specs=slab,
        out_shape=jax.ShapeDtypeStruct((H, T, R), out_dtype),
        compiler_params=_cparams(("parallel", "parallel")),
    )(t, cos2, sgn_sin)


def _mla_q_proj(name, a, w, cos2, sgn_sin, scale):
    T, K = a.shape
    H, _, W = w.shape
    R = cos2.shape[1]
    tm = _pick(T, (1024, 512, 256, 128))

    def body(a_ref, w_ref, c_ref, s_ref, o_ref):
        av = a_ref[...].astype(BF16)
        for h in range(H):
            qf = _dot(av, w_ref[h].astype(BF16), NN)
            r = qf[:, W - R:]
            roped = r * c_ref[...] + _swap_halves(r) * s_ref[...]
            o_ref[h] = (jnp.concatenate([qf[:, :W - R], roped], axis=-1) * scale).astype(o_ref.dtype)

    tab = pl.BlockSpec((tm, R), lambda i: (i, 0))
    return pl.pallas_call(
        body, name=name, grid=(T // tm,),
        in_specs=[pl.BlockSpec((tm, K), lambda i: (i, 0)), pl.BlockSpec((H, K, W), lambda i: (0, 0, 0)), tab, tab],
        out_specs=pl.BlockSpec((H, tm, W), lambda i: (0, i, 0)),
        out_shape=jax.ShapeDtypeStruct((H, T, W), BF16),
        compiler_params=_cparams(("parallel",)),
    )(a, w, cos2, sgn_sin)


def _rope_bwd(name, dy, cos2, sgn_sin, sum_heads):
    H, T, R = dy.shape
    tm = _pick(T, (1024, 512, 256, 128))

    def body(d_ref, c_ref, s_ref, o_ref):
        d = d_ref[...]
        if sum_heads:
            tot = d[0]
            for h in range(1, H):
                tot = tot + d[h]
            d = tot
        o_ref[...] = d * c_ref[...] + _swap_halves(d * s_ref[...])

    if sum_heads:
        grid = (T // tm,)
        in_slab = pl.BlockSpec((H, tm, R), lambda i: (0, i, 0))
        out_slab = pl.BlockSpec((tm, R), lambda i: (i, 0))
        tab = pl.BlockSpec((tm, R), lambda i: (i, 0))
        out_shape = jax.ShapeDtypeStruct((T, R), F32)
        sem = ("parallel",)
    else:
        grid = (H, T // tm)
        in_slab = pl.BlockSpec((None, tm, R), lambda h, i: (h, i, 0))
        out_slab = in_slab
        tab = pl.BlockSpec((tm, R), lambda h, i: (i, 0))
        out_shape = jax.ShapeDtypeStruct((H, T, R), F32)
        sem = ("parallel", "parallel")
    return pl.pallas_call(
        body, name=name, grid=grid, in_specs=[in_slab, tab, tab], out_specs=out_slab,
        out_shape=out_shape, compiler_params=_cparams(sem),
    )(dy, cos2, sgn_sin)


def _log_sigmoid(z):
    return jnp.minimum(z, 0.0) - jnp.log(1.0 + jnp.exp(-jnp.abs(z)))


def _gate_cumsum(name, fl, b, tb):
    H, T = fl.shape

    def body(f_ref, b_ref, c_ref, carry):
        @pl.when(pl.program_id(0) == 0)
        def _():
            carry[...] = jnp.zeros_like(carry)

        ls = _log_sigmoid(f_ref[...] + b_ref[...])
        src = lax.broadcasted_iota(jnp.int32, (tb, tb), 0)
        dst = lax.broadcasted_iota(jnp.int32, (tb, tb), 1)
        tri = (src <= dst).astype(F32)
        c = lax.dot_general(ls, tri, NN, precision=lax.Precision.HIGHEST,
                            preferred_element_type=F32) + carry[...]
        c_ref[...] = c
        carry[...] = carry[...] + jnp.sum(ls, axis=-1, keepdims=True)

    return pl.pallas_call(
        body, name=name, grid=(T // tb,),
        in_specs=[pl.BlockSpec((H, tb), lambda i: (0, i)), pl.BlockSpec((H, 1), lambda i: (0, 0))],
        out_specs=pl.BlockSpec((H, tb), lambda i: (0, i)),
        out_shape=jax.ShapeDtypeStruct((H, T), F32),
        scratch_shapes=[pltpu.VMEM((H, 1), F32)],
        compiler_params=_cparams(("arbitrary",)),
    )(fl, b)


def _gate_cumsum_bwd(name, d_query, d_key, fl, b, tb):
    H, T = fl.shape
    nb = T // tb

    def body(dq_ref, dk_ref, f_ref, b_ref, dfl_ref, db_ref, carry):
        @pl.when(pl.program_id(0) == 0)
        def _():
            carry[...] = jnp.zeros_like(carry)
            db_ref[...] = jnp.zeros_like(db_ref)

        d = dq_ref[...] - dk_ref[...]
        src = lax.broadcasted_iota(jnp.int32, (tb, tb), 0)
        dst = lax.broadcasted_iota(jnp.int32, (tb, tb), 1)
        tri = (src >= dst).astype(F32)
        dls = lax.dot_general(d, tri, NN, precision=lax.Precision.HIGHEST,
                              preferred_element_type=F32) + carry[...]
        z = f_ref[...] + b_ref[...]
        dfl = dls * (1.0 / (1.0 + jnp.exp(z)))
        dfl_ref[...] = dfl
        db_ref[...] += jnp.sum(dfl, axis=-1, keepdims=True)
        carry[...] = carry[...] + jnp.sum(d, axis=-1, keepdims=True)

    blk = pl.BlockSpec((H, tb), lambda i: (0, nb - 1 - i))
    vec = pl.BlockSpec((H, 1), lambda i: (0, 0))
    return pl.pallas_call(
        body, name=name, grid=(nb,),
        in_specs=[blk, blk, blk, vec], out_specs=[blk, vec],
        out_shape=[jax.ShapeDtypeStruct((H, T), F32), jax.ShapeDtypeStruct((H, 1), F32)],
        scratch_shapes=[pltpu.VMEM((H, 1), F32)],
        compiler_params=_cparams(("arbitrary",)),
    )(d_query, d_key, fl, b)


def _causal_mask(tq, rows_are_queries):
    r = lax.broadcasted_iota(jnp.int32, (tq, tq), 0)
    c = lax.broadcasted_iota(jnp.int32, (tq, tq), 1)
    return (c <= r) if rows_are_queries else (r <= c)


def _chunk_rows(j, tq):
    return pl.ds(pl.multiple_of(j * tq, tq), tq)


def _column_as_row(col):
    return jnp.broadcast_to(col, (col.shape[0], 128)).T[:1, :]


def _flash_fwd(name, q, k, v_aug, dv, tq, exchange=None, q_head0=0, v_head0=0):
    H, T, dqk = k.shape
    dva = v_aug.shape[2]
    nq = T // tq

    def body(q_ref, k_ref, v_ref, o_ref, lse_ref, m_sc, acc_sc):
        qi = pl.program_id(1)
        m_sc[...] = jnp.full_like(m_sc, NEG)
        acc_sc[...] = jnp.zeros_like(acc_sc)

        def chunk(j, masked):
            rows = _chunk_rows(j, tq)
            s = _dot(q_ref[...], k_ref[rows, :], NT)
            if masked:
                s = jnp.where(_causal_mask(tq, True), s, NEG)
            m_prev = m_sc[...]
            m_new = jnp.maximum(m_prev, jnp.max(s, axis=1, keepdims=True))
            p = jnp.exp(s - jnp.tile(m_new, (1, tq // 128)))
            alpha = jnp.tile(jnp.exp(m_prev - m_new), (1, dva // 128))
            acc_sc[...] = alpha * acc_sc[...] + _dot(p.astype(BF16), v_ref[rows, :], NN)
            m_sc[...] = m_new

        def off_diagonal(j, carry):
            chunk(j, False)
            return carry

        lax.fori_loop(0, qi, off_diagonal, 0)
        chunk(qi, True)
        acc = acc_sc[...]
        l = acc[:, dv:dv + 1]
        o_ref[...] = acc[:, :dv] / l
        lse_ref[...] = _column_as_row(m_sc[:, :1] + jnp.log(l))

    (o, lse_rows), exchanged = _call_carrying(
        body, name, (H, nq),
        in_specs=[pl.BlockSpec((None, tq, dqk), lambda h, i: (h + q_head0, i, 0)),
                  pl.BlockSpec((None, T, dqk), lambda h, i: (h, 0, 0)),
                  pl.BlockSpec((None, T, dva), lambda h, i: (h + v_head0, 0, 0))],
        out_specs=[pl.BlockSpec((None, tq, dv), lambda h, i: (h, i, 0)),
                   pl.BlockSpec((None, 1, tq), lambda h, i: (h, 0, i))],
        out_shape=[jax.ShapeDtypeStruct((H, T, dv), F32), jax.ShapeDtypeStruct((H, 1, T), F32)],
        scratch_shapes=[pltpu.VMEM((tq, 128), F32), pltpu.VMEM((tq, dva), F32)],
        operands=(q, k, v_aug), exchange=exchange)
    return (o, lse_rows.reshape(H, T)), exchanged


def _row_dot(name, a, b):
    H, T, d = a.shape
    tm = _pick(T, (1024, 512, 256, 128))

    def body(a_ref, b_ref, o_ref):
        col = jnp.sum(a_ref[...].astype(F32) * b_ref[...].astype(F32), axis=-1, keepdims=True)
        o_ref[...] = _column_as_row(col)

    slab = pl.BlockSpec((None, tm, d), lambda h, i: (h, i, 0))
    return pl.pallas_call(
        body, name=name, grid=(H, T // tm), in_specs=[slab, slab],
        out_specs=pl.BlockSpec((None, 1, tm), lambda h, i: (h, 0, i)),
        out_shape=jax.ShapeDtypeStruct((H, 1, T), F32),
        compiler_params=_cparams(("parallel", "parallel")),
    )(a, b).reshape(H, T)


def _flash_bwd(name, q, k, v, do, scale, tq, exchange=None, v_head0=0, token_major_out=False):
    H, T, dqk = q.shape
    dva = v.shape[2]
    nq = T // tq

    def body(q_ref, k_ref, v_ref, do_ref, dq_ref, dk_ref, dv_ref, dk_sc, dv_sc):
        ki = pl.program_id(1)
        dk_sc[...] = jnp.zeros_like(dk_sc)
        dv_sc[...] = jnp.zeros_like(dv_sc)

        @pl.when(ki == 0)
        def _():
            dq_ref[...] = jnp.zeros_like(dq_ref)

        def chunk(i, masked):
            rows = _chunk_rows(i, tq)
            qb = q_ref[rows, :]
            dob = do_ref[rows, :]
            kb = k_ref[...]
            st = _dot(kb, qb, NT)
            if masked:
                st = jnp.where(_causal_mask(tq, False), st, NEG)
            pt = jnp.exp(st)
            dv_sc[...] += _dot(pt.astype(BF16), dob, NN)
            dst = (pt * _dot(v_ref[...], dob, NT)).astype(BF16)
            dk_sc[...] += _dot(dst, qb, NN)
            dq_ref[rows, :] += _dot(dst, kb, TN)

        def off_diagonal(i, carry):
            chunk(i, False)
            return carry

        chunk(ki, True)
        lax.fori_loop(ki + 1, nq, off_diagonal, 0)
        dk_ref[...] = dk_sc[...]
        dv_ref[...] = dv_sc[...]

        @pl.when(ki == nq - 1)
        def _():
            dq_ref[...] = dq_ref[...] * scale

    whole_q = pl.BlockSpec((None, T, dqk), lambda h, j: (h, 0, 0))
    k_spec = pl.BlockSpec((None, tq, dqk), lambda h, j: (h, j, 0))
    v_spec = pl.BlockSpec((None, tq, dva), lambda h, j: (h, j, 0))
    v_in_spec = pl.BlockSpec((None, tq, dva), lambda h, j: (h + v_head0, j, 0))
    if token_major_out:
        out_specs = [pl.BlockSpec((T, dqk), lambda h, j: (0, h)), pl.BlockSpec((tq, dqk), lambda h, j: (j, h)),
                     pl.BlockSpec((tq, dva), lambda h, j: (j, h))]
        out_shape = [jax.ShapeDtypeStruct((T, H * dqk), F32), jax.ShapeDtypeStruct((T, H * dqk), F32),
                     jax.ShapeDtypeStruct((T, H * dva), F32)]
    else:
        out_specs = [whole_q, k_spec, v_spec]
        out_shape = [jax.ShapeDtypeStruct((H, T, dqk), F32), jax.ShapeDtypeStruct((H, T, dqk), F32),
                     jax.ShapeDtypeStruct((H, T, dva), F32)]
    return _call_carrying(
        body, name, (H, nq),
        in_specs=[whole_q, k_spec, v_in_spec, pl.BlockSpec((None, T, dva), lambda h, j: (h, 0, 0))],
        out_specs=out_specs, out_shape=out_shape,
        scratch_shapes=[pltpu.VMEM((tq, dqk), F32), pltpu.VMEM((tq, dva), F32)],
        operands=(q, k, v, do), exchange=exchange)


def _adamw_math(w, g, m, v):
    m = ADAM_B1 * m + (1.0 - ADAM_B1) * g
    v = ADAM_B2 * v + (1.0 - ADAM_B2) * (g * g)
    m_hat = m / (1.0 - ADAM_B1 ** ADAM_STEP)
    v_hat = v / (1.0 - ADAM_B2 ** ADAM_STEP)
    delta = -ADAM_LR * (m_hat / (jnp.sqrt(v_hat) + ADAM_EPS) + ADAM_WD * w)
    return delta, m, v


def _adamw(name, parts, w, m, v):
    P, R, C = parts.shape
    tr = _pick(R, (256, 128, 64, 32, 16, 8))

    def body(p_ref, w_ref, m_ref, v_ref, g_out, d_out, m_out, v_out):
        g = p_ref[0].astype(F32)
        for i in range(1, P):
            g = g + p_ref[i].astype(F32)
        delta, m_new, v_new = _adamw_math(w_ref[...], g, m_ref[...], v_ref[...])
        g_out[...] = g
        d_out[...] = delta
        m_out[...] = m_new
        v_out[...] = v_new

    blk = pl.BlockSpec((tr, C), lambda i: (i, 0))
    sds = jax.ShapeDtypeStruct((R, C), F32)
    return pl.pallas_call(
        body, name=name, grid=(R // tr,),
        in_specs=[pl.BlockSpec((P, tr, C), lambda i: (0, i, 0)), blk, blk, blk],
        out_specs=[blk] * 4, out_shape=[sds] * 4,
        compiler_params=_cparams(("parallel",)),
    )(parts, w, m, v)


def _my_position():
    return lax.axis_index("x"), lax.axis_index("y"), lax.axis_index("c")


def _slot(p):
    return 4 * p[0] + 2 * p[1] + p[2]


def _flip(p, k):
    return tuple((1 - p[i]) if (k >> (2 - i)) & 1 else p[i] for i in range(3))


def _allgather_weights(shards):
    n = len(shards)

    def body(*refs):
        ins = refs[:n]
        outs = refs[n:2 * n]
        send_sems, recv_sems, local_sems = refs[2 * n:]
        x, y, c = _my_position()
        me, sibling = (x, y, c), (x, y, 1 - c)
        chips = [(1 - x, y), (x, 1 - y), (1 - x, 1 - y)]

        def copy(a, k, block, to, src=None):
            dst = outs[a].at[_slot(block)]
            return pltpu.make_async_remote_copy(
                src_ref=dst if src is None else src, dst_ref=dst,
                send_sem=send_sems.at[7 * a + k], recv_sem=recv_sems.at[7 * a + k],
                device_id=to, device_id_type=MESH)

        started = []
        for a in range(n):
            mine = pltpu.make_async_copy(ins[a], outs[a].at[_slot(me)], local_sems.at[a])
            mine.start()
            started.append(mine)
        first = []
        for a in range(n):
            first.append(copy(a, 0, me, sibling, src=ins[a]))
            first += [copy(a, 1 + j, me, (*chip, c), src=ins[a]) for j, chip in enumerate(chips)]
        for cp in first:
            cp.start()
        passed = []
        for j, chip in enumerate(chips):
            for a in range(n):
                copy(a, 1 + j, (*chip, c), me).wait_recv()
                fwd = copy(a, 4 + j, (*chip, c), sibling)
                fwd.start()
                passed.append(fwd)
        for a in range(n):
            copy(a, 0, sibling, me).wait_recv()
            for j, chip in enumerate(chips):
                copy(a, 4 + j, (*chip, 1 - c), me).wait_recv()
        for cp in first + passed:
            cp.wait_send()
        for mine in started:
            mine.wait()

    hbm = pl.BlockSpec(memory_space=pl.ANY)
    return pl.pallas_call(
        body, name="allgather_weights",
        in_specs=[hbm] * n, out_specs=[hbm] * n,
        out_shape=[jax.ShapeDtypeStruct((N_DEV,) + s.shape, s.dtype) for s in shards],
        scratch_shapes=[pltpu.SemaphoreType.DMA((7 * n,)), pltpu.SemaphoreType.DMA((7 * n,)),
                        pltpu.SemaphoreType.DMA((n,))],
        compiler_params=pltpu.CompilerParams(has_side_effects=True),
    )(*shards)


def _exchange_copies(kind, x_in, x_out, send_sems, recv_sems, local_sems, receives=True):
    me = _my_position()
    mine = _slot(me)
    local, sends, recvs = [], [], []
    for a in range(len(x_in)):
        src = x_in[a] if kind == "gather" else x_in[a].at[mine]
        local.append(pltpu.make_async_copy(src, x_out[a].at[mine], local_sems.at[a]))
    for k in range(1, N_DEV):
        peer = _flip(me, k)
        theirs = _slot(peer)
        for a in range(len(x_in)):
            src = x_in[a] if kind == "gather" else x_in[a].at[theirs]
            ends = [(x_out[a].at[mine], sends)] + ([(x_out[a].at[theirs], recvs)] if receives else [])
            for dst, group in ends:
                group.append(pltpu.make_async_remote_copy(
                    src_ref=src, dst_ref=dst, send_sem=send_sems.at[7 * a + k - 1],
                    recv_sem=recv_sems.at[7 * a + k - 1], device_id=peer, device_id_type=MESH))
    return local, sends, recvs


def _exchange_out_shapes(kind, arrays):
    return [jax.ShapeDtypeStruct(((N_DEV,) + a.shape) if kind == "gather" else a.shape, a.dtype)
            for a in arrays]


def _exchange_sems(n):
    return [pltpu.SemaphoreType.DMA((7 * n,)), pltpu.SemaphoreType.DMA((7 * n,)),
            pltpu.SemaphoreType.DMA((n,))]


def _call_carrying(body, name, grid, in_specs, out_specs, out_shape, scratch_shapes, operands, exchange,
                   sequential=False):
    if exchange is None:
        out = pl.pallas_call(
            body, name=name, grid=grid, in_specs=in_specs, out_specs=out_specs, out_shape=out_shape,
            scratch_shapes=scratch_shapes,
            compiler_params=_cparams((("arbitrary",) if sequential else ("parallel",))
                                     + ("arbitrary",) * (len(grid) - 1)),
        )(*operands)
        return out, None
    kind, arrays = exchange
    n, n_in, n_out, n_sc = len(arrays), len(in_specs), len(out_specs), len(scratch_shapes)

    def full_body(*refs):
        ins, refs = refs[:n_in], refs[n_in:]
        x_in, refs = refs[:n], refs[n:]
        outs, refs = refs[:n_out], refs[n_out:]
        x_out, refs = refs[:n], refs[n:]
        scratch, sems = refs[:n_sc], refs[n_sc:]
        first = last = None
        for axis, size in enumerate(grid):
            at_start = pl.program_id(axis) == 0
            at_end = pl.program_id(axis) == size - 1
            first = at_start if first is None else jnp.logical_and(first, at_start)
            last = at_end if last is None else jnp.logical_and(last, at_end)

        @pl.when(first)
        def _():
            local, sends, _ = _exchange_copies(kind, x_in, x_out, *sems, receives=False)
            for cp in local + sends:
                cp.start()

        body(*ins, *outs, *scratch)

        @pl.when(last)
        def _():
            local, sends, recvs = _exchange_copies(kind, x_in, x_out, *sems)
            for cp in recvs:
                cp.wait_recv()
            for cp in sends:
                cp.wait_send()
            for cp in local:
                cp.wait()

    hbm = pl.BlockSpec(memory_space=pl.ANY)
    out = pl.pallas_call(
        full_body, name=name, grid=grid,
        in_specs=list(in_specs) + [hbm] * n, out_specs=list(out_specs) + [hbm] * n,
        out_shape=list(out_shape) + _exchange_out_shapes(kind, arrays),
        scratch_shapes=list(scratch_shapes) + _exchange_sems(n),
        compiler_params=pltpu.CompilerParams(dimension_semantics=("arbitrary",) * len(grid),
                                             vmem_limit_bytes=VMEM_LIMIT_BYTES, has_side_effects=True),
    )(*operands, *arrays)
    return out[:n_out], out[n_out:]


def _allreduce_small(v):
    R, C = v.shape

    def body(v_ref, o_ref, buf, send_sems, recv_sems):
        me = _my_position()
        buf[_slot(me)] = v_ref[...]
        sends = []
        for k in range(1, N_DEV):
            peer = _flip(me, k)
            cp = pltpu.make_async_remote_copy(
                src_ref=v_ref, dst_ref=buf.at[_slot(me)],
                send_sem=send_sems.at[k - 1], recv_sem=recv_sems.at[k - 1],
                device_id=peer, device_id_type=MESH)
            cp.start()
            sends.append(cp)
        for k in range(1, N_DEV):
            peer = _flip(me, k)
            pltpu.make_async_remote_copy(
                src_ref=v_ref, dst_ref=buf.at[_slot(peer)],
                send_sem=send_sems.at[k - 1], recv_sem=recv_sems.at[k - 1],
                device_id=peer, device_id_type=MESH).wait_recv()
        for cp in sends:
            cp.wait_send()
        tot = buf[0]
        for s in range(1, N_DEV):
            tot = tot + buf[s]
        o_ref[...] = tot

    vm = pl.BlockSpec(memory_space=pltpu.VMEM)
    return pl.pallas_call(
        body, name="allreduce_small",
        in_specs=[vm], out_specs=vm, out_shape=jax.ShapeDtypeStruct((R, C), F32),
        scratch_shapes=[pltpu.VMEM((N_DEV, R, C), F32), pltpu.SemaphoreType.DMA((7,)),
                        pltpu.SemaphoreType.DMA((7,))],
        compiler_params=pltpu.CompilerParams(has_side_effects=True),
    )(v)


def _to_heads(t, heads):
    T = t.shape[0]
    return t.reshape(T, heads, t.shape[1] // heads).transpose(1, 0, 2)


def _from_heads(t):
    H, T, d = t.shape
    return t.transpose(1, 0, 2).reshape(T, H * d)


def _widen(t, width, ones_at=None, pieces_at=None, pieces=None):
    out = jnp.pad(t, ((0, 0), (0, 0), (0, width - t.shape[-1])))
    lane = lax.broadcasted_iota(jnp.int32, (1, 1, width), 2)
    if ones_at is not None:
        out = jnp.where((lane >= ones_at) & (lane < ones_at + 3), jnp.ones((), BF16), out)
    if pieces_at is not None:
        for i in range(3):
            out = jnp.where(lane == pieces_at + i, pieces[i][:, :, None], out)
    return out


def _split3(t):
    hi = lax.reduce_precision(t, 8, 7)
    r = t - hi
    mid = lax.reduce_precision(r, 8, 7)
    lo = lax.reduce_precision(r - mid, 8, 7)
    return hi.astype(BF16), mid.astype(BF16), lo.astype(BF16)


def _pad_cols(t, n):
    return jnp.pad(t, ((0, 0), (0, n - t.shape[1])))


def _pack_small(mix, ffn, kv, fin, kva, qa, bf, last):
    row6 = jnp.concatenate([kva.reshape(-1), qa.reshape(-1), bf.reshape(-1),
                            jnp.zeros((D_MODEL - KV_LORA - Q_LORA - FOX_HEADS,), F32)])
    return jnp.stack([mix[0], mix[1], ffn[0], ffn[1], kv.reshape(-1), fin.reshape(-1), row6, last])


def _unpack_small(p):
    mix = p[0:2]
    ffn = p[2:4]
    kv = p[4]
    fin = p[5]
    kva = p[6, :KV_LORA]
    qa = p[6, KV_LORA:KV_LORA + Q_LORA].reshape(1, Q_LORA)
    bf = p[6, KV_LORA + Q_LORA:KV_LORA + Q_LORA + FOX_HEADS].reshape(1, FOX_HEADS)
    return mix, ffn, bf, kv, kva, qa, fin


def _mlp_fwd(tag, xin, g, w_up, w_down):
    h = _rms(f"{tag}_norm", xin, g, BF16)

    def act(acc):
        r = jnp.maximum(acc, 0.0)
        return acc, r * r

    u, a = _mm(f"{tag}_up", h, w_up, "nn", (BF16, BF16), epi=act)
    xout = _mm(f"{tag}_down", a, w_down, "nn", (F32,), epi=lambda acc, r: (acc + r,), extras=(xin,))
    return xout, (h, u, a)


def _mlp_bwd(tag, gout, xin, g, w_up, w_down, saved):
    h, u, a = saved
    dw_down = _mm_tn(f"{tag}_dwdown", a, gout)
    du = _mm(f"{tag}_du", gout, w_down, "nt", (BF16,),
             epi=lambda acc, uu: (acc * (2.0 * jnp.maximum(uu.astype(F32), 0.0)),), extras=(u,))
    dw_up = _mm_tn(f"{tag}_dwup", h, du)
    gin, dg = _mm_rms_bwd(f"{tag}_dh_norm_bwd", du, w_up, xin, g, gout)
    return gin, dg, dw_up, dw_down


def kernel(x, norm_mix_g, norm_ffn_g, fox_w_in, fox_b_f, fox_w_out, kv_norm_g, mla_w_kv_a, mla_kv_a_norm_g, mla_w_kv_b, mla_w_q_a, mla_q_a_norm_g, mla_w_q_b, mla_w_out, ffn_w_up, ffn_w_down, final_norm_g, loss_target, m_norm_mix_g, m_norm_ffn_g, m_fox_w_in, m_fox_b_f, m_fox_w_out, m_kv_norm_g, m_mla_w_kv_a, m_mla_kv_a_norm_g, m_mla_w_kv_b, m_mla_w_q_a, m_mla_q_a_norm_g, m_mla_w_q_b, m_mla_w_out, m_ffn_w_up, m_ffn_w_down, m_final_norm_g, v_norm_mix_g, v_norm_ffn_g, v_fox_w_in, v_fox_b_f, v_fox_w_out, v_kv_norm_g, v_mla_w_kv_a, v_mla_kv_a_norm_g, v_mla_w_kv_b, v_mla_w_q_a, v_mla_q_a_norm_g, v_mla_w_q_b, v_mla_w_out, v_ffn_w_up, v_ffn_w_down, v_final_norm_g):
    T = x.shape[1]
    D = D_MODEL
    tq = 512 if T >= 2048 else 128
    x0 = x[0]
    tgt = loss_target[0]

    gat_fox = _allgather_weights([fox_w_in[0].astype(BF16), fox_w_out[0].astype(BF16)])
    later_shards = [s.astype(BF16) for s in (mla_w_kv_a, mla_w_kv_b, mla_w_q_a[0], mla_w_q_b[0],
                                             mla_w_out[0], ffn_w_up, ffn_w_down)]
    w_in = gat_fox[0].transpose(1, 0, 2).reshape(D, 3 * D + FOX_HEADS)
    w_qkv = w_in[:, :3 * D]
    w_f = _pad_cols(w_in[:, 3 * D:], 128)
    w_fo = gat_fox[1].reshape(D, D)
    g_mix0, g_mix1 = norm_mix_g[0:1], norm_mix_g[1:2]
    g_ffn0, g_ffn1 = norm_ffn_g[0:1], norm_ffn_g[1:2]
    g_kv = kv_norm_g.reshape(1, D)
    g_kva = mla_kv_a_norm_g.reshape(1, KV_LORA)
    g_qa = mla_q_a_norm_g.reshape(1, Q_LORA)
    g_fin = final_norm_g.reshape(1, D)

    inv = 1.0 / (ROPE_BASE ** (jnp.arange(0, QK_ROPE, 2, dtype=F32) / QK_ROPE))
    ang = jnp.arange(T, dtype=F32)[:, None] * inv[None, :]
    cos, sin = jnp.cos(ang), jnp.sin(ang)
    cos2 = jnp.concatenate([cos, cos], axis=-1)
    sgn_sin = jnp.concatenate([-sin, sin], axis=-1)

    h0 = _rms("l0_mix_norm", x0, g_mix0, BF16)
    fl_pad = _mm("fox_gate_logit", h0, w_f, "nn", (F32,))
    fl = fl_pad[:, :FOX_HEADS].T
    b_f = fox_b_f.reshape(FOX_HEADS, 1)
    cgate = _gate_cumsum("fox_gate_scan", fl, b_f, tq)
    fox_scale = FOX_HEAD_DIM ** -0.5
    col_scale = jnp.where(jnp.arange(3 * D) < D, fox_scale, 1.0).astype(BF16)
    tail = jnp.arange(FOX_AUG - FOX_HEAD_DIM)
    ones_q = (tail < 3).astype(F32)
    consts_k = ((tail >= 4) & (tail < 7)).astype(F32) + (tail == 3).astype(F32) * (1.0 / fox_scale)
    tails = jnp.broadcast_to(jnp.stack([ones_q, consts_k, ones_q])[:, None, None, :],
                             (3, FOX_HEADS, 1, FOX_AUG - FOX_HEAD_DIM)).reshape(3 * FOX_HEADS, 1, -1)
    qkv_h = _mm_head_slabs("fox_qkv", h0, w_qkv * col_scale, tails, FOX_HEAD_DIM, BF16)
    fk_aug = _widen(qkv_h[FOX_HEADS:2 * FOX_HEADS], FOX_AUG, pieces_at=FOX_HEAD_DIM,
                    pieces=_split3(-cgate))
    (fo, flse), gat = _flash_fwd("fox_attn", qkv_h, fk_aug, qkv_h, FOX_HEAD_DIM, tq,
                                 exchange=("gather", later_shards), q_head0=0, v_head0=2 * FOX_HEADS)
    w_kva = _pad_cols(gat[0].reshape(D, KV_LORA + QK_ROPE), KV_A_PAD)
    w_kvb_h = gat[1]
    w_qa = gat[2].reshape(D, Q_LORA)
    w_qb_h = gat[3]
    w_mo = gat[4].reshape(D, D)
    w_up = gat[5].transpose(1, 2, 0, 3).reshape(2, D, D_FF)
    w_down = gat[6].transpose(1, 0, 2, 3).reshape(2, D_FF, D)
    fctx = _from_heads(fo).astype(BF16)
    x1 = _mm("fox_out", fctx, w_fo, "nn", (F32,), epi=lambda acc, r: (acc + r,), extras=(x0,))
    x2, mlp0 = _mlp_fwd("l0_ffn", x1, g_ffn0, w_up[0], w_down[0])

    src = _rms("kv_norm", x2, g_kv, BF16)
    kva = _mm("kv_a", src, w_kva, "nn", (F32,))
    kva_lat = kva[:, :KV_LORA]
    c_kv = _rms("kv_a_norm", kva_lat, g_kva, BF16)
    k_rope = _rope("k_rope", kva[:, KV_LORA:KV_LORA + QK_ROPE][None], cos2, sgn_sin, BF16)
    kvb_h = _mm_heads("kv_b", c_kv, w_kvb_h, BF16)
    mk = jnp.concatenate([kvb_h[:, :, :QK_NOPE],
                          jnp.broadcast_to(k_rope, (MLA_HEADS, T, QK_ROPE))], axis=-1)
    mv = kvb_h[:, :, QK_NOPE:]

    h1 = _rms("l1_mix_norm", x2, g_mix1, BF16)
    qa = _mm("q_a", h1, w_qa, "nn", (F32,))
    c_q = _rms("q_a_norm", qa, g_qa, BF16)
    mla_scale = (QK_NOPE + QK_ROPE) ** -0.5
    mq = _mla_q_proj("q_b", c_q, w_qb_h, cos2, sgn_sin, mla_scale)
    mv_aug = _widen(mv, MLA_AUG, ones_at=V_HEAD)
    (mo, mlse), _ = _flash_fwd("mla_attn", mq, mk, mv_aug, V_HEAD, tq)
    mctx = _from_heads(mo).astype(BF16)
    x3 = _mm("mla_out", mctx, w_mo, "nn", (F32,), epi=lambda acc, r: (acc + r,), extras=(x2,))
    x4, mlp1 = _mlp_fwd("l1_ffn", x3, g_ffn1, w_up[1], w_down[1])

    g4, dg_fin, loss_vec = _loss_head("loss_head", x4, g_fin, tgt)

    g3, dg_ffn1, dw_up1, dw_down1 = _mlp_bwd("l1_ffn", g4, x3, g_ffn1, w_up[1], w_down[1], mlp1)

    dw_mo = _mm_tn("mla_out_dw", mctx, g3)
    dmo = _to_heads(_mm("mla_out_dx", g3, w_mo, "nt", (BF16,)), MLA_HEADS)
    mdelta = _row_dot("mla_delta", mo, dmo)
    dqk = QK_NOPE + QK_ROPE
    mq_bwd = _widen(mq, MLA_AUG, pieces_at=dqk, pieces=_split3(-mlse))
    mk_bwd = _widen(mk, MLA_AUG, ones_at=dqk)
    mdo_aug = _widen(dmo, MLA_AUG, pieces_at=V_HEAD, pieces=_split3(-mdelta))
    (mdq, mdk, mdv), _ = _flash_bwd("mla_attn_bwd", mq_bwd, mk_bwd, mv_aug, mdo_aug, mla_scale, tq)
    mdq = mdq[:, :, :dqk]
    mdk = mdk[:, :, :dqk]
    mdv = mdv[:, :, :V_HEAD]
    dq_rope = _rope_bwd("q_rope_bwd", mdq[:, :, QK_NOPE:], cos2, sgn_sin, False)
    dqf_h = jnp.concatenate([mdq[:, :, :QK_NOPE], dq_rope], axis=-1)
    dw_qb_h = _mm_heads_dw("q_b_dw", c_q, dqf_h)
    dc_q = _mm_heads_dx("q_b_dx", dqf_h, w_qb_h)
    dqa, dg_qa = _rms_bwd("q_a_norm_bwd", qa, g_qa, dc_q)
    dw_qa = _mm_tn("q_a_dw", h1, dqa)
    g2a, dg_mix1 = _mm_rms_bwd("q_a_dx_norm_bwd", dqa, w_qa, x2, g_mix1, g3)

    dk_rope = _rope_bwd("k_rope_bwd", mdk[:, :, QK_NOPE:], cos2, sgn_sin, True)
    dkvb_h = jnp.concatenate([mdk[:, :, :QK_NOPE], mdv], axis=-1)
    dw_kvb_h = _mm_heads_dw("kv_b_dw", c_kv, dkvb_h)
    dc_kv = _mm_heads_dx("kv_b_dx", dkvb_h, w_kvb_h)
    dkva_lat, dg_kva = _rms_bwd("kv_a_norm_bwd", kva_lat, g_kva, dc_kv)
    dkva = _pad_cols(jnp.concatenate([dkva_lat, dk_rope], axis=-1), KV_A_PAD)
    dw_kva = _mm_tn("kv_a_dw", src, dkva)[:, :KV_LORA + QK_ROPE]
    g2, dg_kv = _mm_rms_bwd("kv_a_dx_norm_bwd", dkva, w_kva, x2, g_kv, g2a)

    g1, dg_ffn0, dw_up0, dw_down0 = _mlp_bwd("l0_ffn", g2, x1, g_ffn0, w_up[0], w_down[0], mlp0)

    dw_fo = _mm_tn("fox_out_dw", fctx, g1)
    dfo = _to_heads(_mm("fox_out_dx", g1, w_fo, "nt", (BF16,)), FOX_HEADS)
    fdelta = _row_dot("fox_delta", fo, dfo)
    fq_bwd = _widen(qkv_h[:FOX_HEADS], FOX_AUG, pieces_at=FOX_HEAD_DIM + 4, pieces=_split3(-flse))
    fdo_aug = _widen(dfo, FOX_AUG, pieces_at=FOX_HEAD_DIM, pieces=_split3(-fdelta))
    dw_up = jnp.stack([dw_up0, dw_up1])
    dw_down = jnp.stack([dw_down0, dw_down1])
    early = [
        dw_fo.reshape(N_DEV, D // N_DEV, D),
        dw_kva.reshape(N_DEV, D // N_DEV, KV_LORA + QK_ROPE),
        dw_kvb_h,
        dw_qa.reshape(N_DEV, D // N_DEV, Q_LORA),
        dw_qb_h,
        dw_mo.reshape(N_DEV, D // N_DEV, D),
        dw_up.reshape(2, D, N_DEV, -1).transpose(2, 0, 1, 3),
        dw_down.reshape(2, N_DEV, D_FF // N_DEV, D).transpose(1, 0, 2, 3),
    ]
    fd_aug, early_parts = _flash_bwd(
        "fox_attn_bwd", fq_bwd, fk_aug, qkv_h, fdo_aug, fox_scale, tq,
        exchange=("scatter", [g.astype(BF16) for g in early]), v_head0=2 * FOX_HEADS, token_major_out=True)
    fd_aug = [t.reshape(T, FOX_HEADS, FOX_AUG) for t in fd_aug]
    dfl, db_f = _gate_cumsum_bwd("fox_gate_scan_bwd", fd_aug[0][:, :, FOX_HEAD_DIM + 3].T,
                                 fd_aug[1][:, :, FOX_HEAD_DIM].T, fl, b_f, tq)
    dqkv = jnp.concatenate([t[:, :, :FOX_HEAD_DIM].reshape(T, D) for t in fd_aug], axis=-1).astype(BF16)
    dfl_pad = _pad_cols(dfl.T, 128)
    dw_qkv = _mm_tn("fox_qkv_dw", h0, dqkv)
    dw_f = _mm_tn("fox_gate_dw", h0, dfl_pad)[:, :FOX_HEADS]
    dw_in = jnp.concatenate([dw_qkv, dw_f], axis=-1)
    dh0a = _mm("fox_gate_dx", dfl_pad, w_f, "nt", (F32,))
    late = dw_in.reshape(D, N_DEV, -1).transpose(1, 0, 2).astype(BF16)
    grad_x, dg_mix0, late_parts = _mm_rms_bwd("fox_qkv_dx_norm_bwd", dqkv, w_qkv, x0, g_mix0, g1, add=dh0a,
                                              exchange=("scatter", [late]))

    parts = list(late_parts) + list(early_parts)

    names = ["fox_w_in", "fox_w_out", "mla_w_kv_a", "mla_w_kv_b", "mla_w_q_a", "mla_w_q_b",
             "mla_w_out", "ffn_w_up", "ffn_w_down"]
    moms = [m_fox_w_in, m_fox_w_out, m_mla_w_kv_a, m_mla_w_kv_b, m_mla_w_q_a, m_mla_w_q_b,
            m_mla_w_out, m_ffn_w_up, m_ffn_w_down]
    vars_ = [v_fox_w_in, v_fox_w_out, v_mla_w_kv_a, v_mla_w_kv_b, v_mla_w_q_a, v_mla_w_q_b,
             v_mla_w_out, v_ffn_w_up, v_ffn_w_down]
    full = [fox_w_in, fox_w_out, mla_w_kv_a, mla_w_kv_b, mla_w_q_a, mla_w_q_b, mla_w_out,
            ffn_w_up, ffn_w_down]
    big = {}
    for nm, p, w, m, v in zip(names, parts, full, moms, vars_):
        C = w.shape[-1]
        res = _adamw(f"adamw_{nm}", p.reshape(N_DEV, -1, C), w.reshape(-1, C), m.reshape(-1, C),
                     v.reshape(-1, C))
        big[nm] = [r.reshape(w.shape) for r in res]

    zrow = jnp.zeros((D,), F32)
    g_small = _pack_small(jnp.concatenate([dg_mix0, dg_mix1]), jnp.concatenate([dg_ffn0, dg_ffn1]),
                          dg_kv, dg_fin, dg_kva, dg_qa, db_f, zrow.at[0].set(loss_vec[0, 0]))
    tot_small = _allreduce_small(g_small)
    w_small = _pack_small(norm_mix_g, norm_ffn_g, kv_norm_g, final_norm_g, mla_kv_a_norm_g,
                          mla_q_a_norm_g, fox_b_f, zrow)
    m_small = _pack_small(m_norm_mix_g, m_norm_ffn_g, m_kv_norm_g, m_final_norm_g, m_mla_kv_a_norm_g,
                          m_mla_q_a_norm_g, m_fox_b_f, zrow)
    v_small = _pack_small(v_norm_mix_g, v_norm_ffn_g, v_kv_norm_g, v_final_norm_g, v_mla_kv_a_norm_g,
                          v_mla_q_a_norm_g, v_fox_b_f, zrow)
    small = _adamw("adamw_small", tot_small[None], w_small, m_small, v_small)
    loss = tot_small[7, 0]
    small = [_unpack_small(s) for s in small]

    def ordered(i):
        mix, ffn, bf, kv, kva, qa, fin = small[i]
        return [mix, ffn, big["fox_w_in"][i], bf, big["fox_w_out"][i], kv, big["mla_w_kv_a"][i], kva,
                big["mla_w_kv_b"][i], big["mla_w_q_a"][i], qa, big["mla_w_q_b"][i],
                big["mla_w_out"][i], big["ffn_w_up"][i], big["ffn_w_down"][i], fin]

    return (loss, grad_x[None], *ordered(0), *ordered(1), *ordered(2), *ordered(3))
```

```python
import functools
import math

import jax
import jax.numpy as jnp
from jax import lax
from jax.experimental import pallas as pl
from jax.experimental.pallas import tpu as pltpu

F32 = jnp.float32
BF16 = jnp.bfloat16
MESH = pl.DeviceIdType.MESH

N_DEV = 8
D_MODEL = 1024
FOX_HEADS = 16
FOX_HEAD_DIM = 64
FOX_AUG = 128
MLA_AUG = 256
MLA_HEADS = 8
QK_NOPE = 128
QK_ROPE = 64
V_HEAD = 128
Q_LORA = 384
KV_LORA = 256
KV_A_PAD = 384
D_FF = 4096
ROPE_BASE = 10000.0
EPS = 1e-6
NEG = -1e30

ADAM_LR = 0.001
ADAM_B1 = 0.9
ADAM_B2 = 0.999
ADAM_EPS = 1e-08
ADAM_WD = 0.01
ADAM_STEP = 10

VMEM_LIMIT_BYTES = 56 * 1024 * 1024

NN = (((1,), (0,)), ((), ()))
NT = (((1,), (1,)), ((), ()))
TN = (((0,), (0,)), ((), ()))
_FORMS = {"nn": NN, "nt": NT}


def _cparams(sem=None):
    return pltpu.CompilerParams(dimension_semantics=sem, vmem_limit_bytes=VMEM_LIMIT_BYTES)


def _pick(n, cands):
    for c in cands:
        if c <= n and n % c == 0:
            return c
    return n


def _dot(a, b, dims):
    return lax.dot_general(a, b, dims, preferred_element_type=F32)


def _mm(name, a, b, form, out_dtypes, epi=None, extras=(), tm=1024, tn=None):
    M, K = a.shape
    N = b.shape[1] if form == "nn" else b.shape[0]
    tm = _pick(M, (tm, 512, 256, 128))
    tn = _pick(N, (tn or (1024 if K <= 1024 else 512), 512, 384, 256, 128))
    n_ex = len(extras)
    n_out = len(out_dtypes)
    cast_once = a.dtype != BF16

    def body(*refs):
        a_ref, b_ref = refs[0], refs[1]
        ex = refs[2:2 + n_ex]
        outs = refs[2 + n_ex:2 + n_ex + n_out]
        if cast_once:
            a_sc = refs[2 + n_ex + n_out]

            @pl.when(pl.program_id(1) == 0)
            def _():
                a_sc[...] = a_ref[...].astype(BF16)

            av = a_sc[...]
        else:
            av = a_ref[...]
        acc = _dot(av, b_ref[...].astype(BF16), _FORMS[form])
        res = epi(acc, *[e[...] for e in ex]) if epi is not None else (acc,)
        for o_ref, r in zip(outs, res):
            o_ref[...] = r.astype(o_ref.dtype)

    if form == "nn":
        b_spec = pl.BlockSpec((K, tn), lambda i, j: (0, j))
    else:
        b_spec = pl.BlockSpec((tn, K), lambda i, j: (j, 0))
    tile = pl.BlockSpec((tm, tn), lambda i, j: (i, j))
    out = pl.pallas_call(
        body, name=name, grid=(M // tm, N // tn),
        in_specs=[pl.BlockSpec((tm, K), lambda i, j: (i, 0)), b_spec] + [tile] * n_ex,
        out_specs=[tile] * n_out,
        out_shape=[jax.ShapeDtypeStruct((M, N), dt) for dt in out_dtypes],
        scratch_shapes=[pltpu.VMEM((tm, K), BF16)] if cast_once else [],
        compiler_params=_cparams(("parallel", "arbitrary")),
    )(a, b, *extras)
    return out if n_out > 1 else out[0]


def _mm_tn(name, a, b):
    T, Ka = a.shape
    N = b.shape[1]
    tk = _pick(Ka, (1024, 512, 384, 256, 128))
    tn = _pick(N, (1024, 768, 512, 384, 256, 128))
    tt = _pick(T, (1024, 512, 256, 128))

    def body(a_ref, b_ref, o_ref):
        @pl.when(pl.program_id(2) == 0)
        def _():
            o_ref[...] = jnp.zeros_like(o_ref)

        o_ref[...] += _dot(a_ref[...].astype(BF16), b_ref[...].astype(BF16), TN)

    return pl.pallas_call(
        body, name=name, grid=(Ka // tk, N // tn, T // tt),
        in_specs=[pl.BlockSpec((tt, tk), lambda i, j, t: (t, i)),
                  pl.BlockSpec((tt, tn), lambda i, j, t: (t, j))],
        out_specs=pl.BlockSpec((tk, tn), lambda i, j, t: (i, j)),
        out_shape=jax.ShapeDtypeStruct((Ka, N), F32),
        compiler_params=_cparams(("parallel", "parallel", "arbitrary")),
    )(a, b)


def _mm_heads(name, a, w, out_dtype):
    T, K = a.shape
    H, _, N = w.shape
    tm = _pick(T, (1024, 512, 256, 128))

    def body(a_ref, w_ref, o_ref):
        av = a_ref[...].astype(BF16)
        for s in range(H):
            o_ref[s] = _dot(av, w_ref[s].astype(BF16), NN).astype(o_ref.dtype)

    return pl.pallas_call(
        body, name=name, grid=(T // tm,),
        in_specs=[pl.BlockSpec((tm, K), lambda i: (i, 0)), pl.BlockSpec((H, K, N), lambda i: (0, 0, 0))],
        out_specs=pl.BlockSpec((H, tm, N), lambda i: (0, i, 0)),
        out_shape=jax.ShapeDtypeStruct((H, T, N), out_dtype),
        compiler_params=_cparams(("parallel",)),
    )(a, w)


def _mm_head_slabs(name, a, w, tails, head_dim, out_dtype, heads_per_step=8):
    T, K = a.shape
    S, _, tail = tails.shape
    hb = heads_per_step
    tm = _pick(T, (1024, 512, 256, 128))

    def body(a_ref, w_ref, t_ref, o_ref):
        acc = _dot(a_ref[...].astype(BF16), w_ref[...].astype(BF16), NN)
        for s in range(hb):
            slab = jnp.concatenate([acc[:, s * head_dim:(s + 1) * head_dim],
                                    jnp.broadcast_to(t_ref[s], (tm, tail))], axis=-1)
            o_ref[s] = slab.astype(o_ref.dtype)

    return pl.pallas_call(
        body, name=name, grid=(T // tm, S // hb),
        in_specs=[pl.BlockSpec((tm, K), lambda i, j: (i, 0)),
                  pl.BlockSpec((K, hb * head_dim), lambda i, j: (0, j)),
                  pl.BlockSpec((hb, 1, tail), lambda i, j: (j, 0, 0))],
        out_specs=pl.BlockSpec((hb, tm, head_dim + tail), lambda i, j: (j, i, 0)),
        out_shape=jax.ShapeDtypeStruct((S, T, head_dim + tail), out_dtype),
        compiler_params=_cparams(("parallel", "arbitrary")),
    )(a, w, tails)


def _mm_heads_dw(name, a, g):
    T, K = a.shape
    H, _, N = g.shape
    tt = _pick(T, (1024, 512, 256, 128))

    def body(a_ref, g_ref, o_ref):
        @pl.when(pl.program_id(0) == 0)
        def _():
            o_ref[...] = jnp.zeros_like(o_ref)

        av = a_ref[...].astype(BF16)
        for s in range(H):
            o_ref[s] += _dot(av, g_ref[s].astype(BF16), TN)

    return pl.pallas_call(
        body, name=name, grid=(T // tt,),
        in_specs=[pl.BlockSpec((tt, K), lambda t: (t, 0)), pl.BlockSpec((H, tt, N), lambda t: (0, t, 0))],
        out_specs=pl.BlockSpec((H, K, N), lambda t: (0, 0, 0)),
        out_shape=jax.ShapeDtypeStruct((H, K, N), F32),
        compiler_params=_cparams(("arbitrary",)),
    )(a, g)


def _mm_heads_dx(name, g, w):
    H, T, N = g.shape
    K = w.shape[1]
    tm = _pick(T, (512, 256, 128))

    def body(g_ref, w_ref, o_ref):
        acc = _dot(g_ref[0].astype(BF16), w_ref[0].astype(BF16), NT)
        for s in range(1, H):
            acc = acc + _dot(g_ref[s].astype(BF16), w_ref[s].astype(BF16), NT)
        o_ref[...] = acc

    return pl.pallas_call(
        body, name=name, grid=(T // tm,),
        in_specs=[pl.BlockSpec((H, tm, N), lambda i: (0, i, 0)), pl.BlockSpec((H, K, N), lambda i: (0, 0, 0))],
        out_specs=pl.BlockSpec((tm, K), lambda i: (i, 0)),
        out_shape=jax.ShapeDtypeStruct((T, K), F32),
        compiler_params=_cparams(("parallel",)),
    )(g, w)


def _rms(name, x, g, out_dtype):
    T, D = x.shape
    tm = _pick(T, (1024, 512, 256, 128))

    def body(x_ref, g_ref, o_ref):
        xf = x_ref[...]
        r = lax.rsqrt(jnp.mean(xf * xf, axis=-1, keepdims=True) + EPS)
        o_ref[...] = (xf * r * g_ref[...]).astype(o_ref.dtype)

    return pl.pallas_call(
        body, name=name, grid=(T // tm,),
        in_specs=[pl.BlockSpec((tm, D), lambda i: (i, 0)), pl.BlockSpec((1, D), lambda i: (0, 0))],
        out_specs=pl.BlockSpec((tm, D), lambda i: (i, 0)),
        out_shape=jax.ShapeDtypeStruct((T, D), out_dtype),
        compiler_params=_cparams(("parallel",)),
    )(x, g)


def _rms_bwd(name, x, g, dh, dres=None):
    T, D = x.shape
    tm = _pick(T, (512, 256, 128))
    has_res = dres is not None

    def body(*refs):
        if has_res:
            x_ref, g_ref, dh_ref, dres_ref, dx_ref, dg_ref = refs
        else:
            x_ref, g_ref, dh_ref, dx_ref, dg_ref = refs

        @pl.when(pl.program_id(0) == 0)
        def _():
            dg_ref[...] = jnp.zeros_like(dg_ref)

        xf = x_ref[...]
        r = lax.rsqrt(jnp.mean(xf * xf, axis=-1, keepdims=True) + EPS)
        xhat = xf * r
        dy = dh_ref[...].astype(F32)
        dxh = dy * g_ref[...]
        dx = r * (dxh - xhat * jnp.mean(dxh * xhat, axis=-1, keepdims=True))
        if has_res:
            dx = dx + dres_ref[...]
        dx_ref[...] = dx
        dg_ref[...] += jnp.sum(dy * xhat, axis=0, keepdims=True)

    row = pl.BlockSpec((tm, D), lambda i: (i, 0))
    vec = pl.BlockSpec((1, D), lambda i: (0, 0))
    ins = [x, g, dh] + ([dres] if has_res else [])
    return pl.pallas_call(
        body, name=name, grid=(T // tm,),
        in_specs=[row, vec, row] + ([row] if has_res else []),
        out_specs=[row, vec],
        out_shape=[jax.ShapeDtypeStruct((T, D), F32), jax.ShapeDtypeStruct((1, D), F32)],
        compiler_params=_cparams(("arbitrary",)),
    )(*ins)


def _mm_rms_bwd(name, a, b, x, g, dres, add=None, exchange=None):
    T, K = a.shape
    D = b.shape[0]
    tm = _pick(T, (512, 256, 128))
    has_add = add is not None

    def body(*refs):
        a_ref, b_ref, x_ref, g_ref, dres_ref = refs[:5]
        dx_ref, dg_ref = refs[-2:]

        @pl.when(pl.program_id(0) == 0)
        def _():
            dg_ref[...] = jnp.zeros_like(dg_ref)

        dy = _dot(a_ref[...].astype(BF16), b_ref[...].astype(BF16), NT)
        if has_add:
            dy = dy + refs[5][...]
        xf = x_ref[...]
        r = lax.rsqrt(jnp.mean(xf * xf, axis=-1, keepdims=True) + EPS)
        xhat = xf * r
        dxh = dy * g_ref[...]
        dx_ref[...] = r * (dxh - xhat * jnp.mean(dxh * xhat, axis=-1, keepdims=True)) + dres_ref[...]
        dg_ref[...] += jnp.sum(dy * xhat, axis=0, keepdims=True)

    row = pl.BlockSpec((tm, D), lambda i: (i, 0))
    vec = pl.BlockSpec((1, D), lambda i: (0, 0))
    (dx, dg), exchanged = _call_carrying(
        body, name, (T // tm,),
        in_specs=[pl.BlockSpec((tm, K), lambda i: (i, 0)), pl.BlockSpec((D, K), lambda i: (0, 0)), row, vec, row]
        + ([row] if has_add else []),
        out_specs=[row, vec],
        out_shape=[jax.ShapeDtypeStruct((T, D), F32), jax.ShapeDtypeStruct((1, D), F32)],
        scratch_shapes=[], operands=(a, b, x, g, dres) + ((add,) if has_add else ()), exchange=exchange,
        sequential=True)
    return (dx, dg) if exchange is None else (dx, dg, exchanged)


def _loss_head(name, x, g, tgt):
    T, D = x.shape
    tm = _pick(T, (512, 256, 128))

    def body(x_ref, g_ref, t_ref, dx_ref, dg_ref, loss_ref):
        @pl.when(pl.program_id(0) == 0)
        def _():
            dg_ref[...] = jnp.zeros_like(dg_ref)
            loss_ref[...] = jnp.zeros_like(loss_ref)

        xf = x_ref[...]
        r = lax.rsqrt(jnp.mean(xf * xf, axis=-1, keepdims=True) + EPS)
        xhat = xf * r
        gv = g_ref[...]
        err = xhat * gv - t_ref[...]
        row_loss = jnp.mean(err * err, axis=-1, keepdims=True)
        loss_ref[...] += 0.5 * jnp.sum(row_loss, axis=0, keepdims=True)
        dy = err * (1.0 / D)
        dxh = dy * gv
        dx_ref[...] = r * (dxh - xhat * jnp.mean(dxh * xhat, axis=-1, keepdims=True))
        dg_ref[...] += jnp.sum(dy * xhat, axis=0, keepdims=True)

    row = pl.BlockSpec((tm, D), lambda i: (i, 0))
    vec = pl.BlockSpec((1, D), lambda i: (0, 0))
    return pl.pallas_call(
        body, name=name, grid=(T // tm,),
        in_specs=[row, vec, row],
        out_specs=[row, vec, pl.BlockSpec((1, 128), lambda i: (0, 0))],
        out_shape=[jax.ShapeDtypeStruct((T, D), F32), jax.ShapeDtypeStruct((1, D), F32),
                   jax.ShapeDtypeStruct((1, 128), F32)],
        compiler_params=_cparams(("arbitrary",)),
    )(x, g, tgt)


def _swap_halves(t):
    half = t.shape[-1] // 2
    return jnp.concatenate([t[:, half:], t[:, :half]], axis=-1)


def _rope(name, t, cos2, sgn_sin, out_dtype):
    H, T, R = t.shape
    tm = _pick(T, (1024, 512, 256, 128))

    def body(t_ref, c_ref, s_ref, o_ref):
        tf = t_ref[...].astype(F32)
        o_ref[...] = (tf * c_ref[...] + _swap_halves(tf) * s_ref[...]).astype(o_ref.dtype)

    slab = pl.BlockSpec((None, tm, R), lambda h, i: (h, i, 0))
    tab = pl.BlockSpec((tm, R), lambda h, i: (i, 0))
    return pl.pallas_call(
        body, name=name, grid=(H, T // tm),
        in_specs=[slab, tab, tab], out_specs=slab,
        out_shape=jax.ShapeDtypeStruct((H, T, R), out_dtype),
        compiler_params=_cparams(("parallel", "parallel")),
    )(t, cos2, sgn_sin)


def _mla_q_proj(name, a, w, cos2, sgn_sin, scale):
    T, K = a.shape
    H, _, W = w.shape
    R = cos2.shape[1]
    tm = _pick(T, (1024, 512, 256, 128))

    def body(a_ref, w_ref, c_ref, s_ref, o_ref):
        av = a_ref[...].astype(BF16)
        for h in range(H):
            qf = _dot(av, w_ref[h].astype(BF16), NN)
            r = qf[:, W - R:]
            roped = r * c_ref[...] + _swap_halves(r) * s_ref[...]
            o_ref[h] = (jnp.concatenate([qf[:, :W - R], roped], axis=-1) * scale).astype(o_ref.dtype)

    tab = pl.BlockSpec((tm, R), lambda i: (i, 0))
    return pl.pallas_call(
        body, name=name, grid=(T // tm,),
        in_specs=[pl.BlockSpec((tm, K), lambda i: (i, 0)), pl.BlockSpec((H, K, W), lambda i: (0, 0, 0)), tab, tab],
        out_specs=pl.BlockSpec((H, tm, W), lambda i: (0, i, 0)),
        out_shape=jax.ShapeDtypeStruct((H, T, W), BF16),
        compiler_params=_cparams(("parallel",)),
    )(a, w, cos2, sgn_sin)


def _rope_bwd(name, dy, cos2, sgn_sin, sum_heads):
    H, T, R = dy.shape
    tm = _pick(T, (1024, 512, 256, 128))

    def body(d_ref, c_ref, s_ref, o_ref):
        d = d_ref[...]
        if sum_heads:
            tot = d[0]
            for h in range(1, H):
                tot = tot + d[h]
            d = tot
        o_ref[...] = d * c_ref[...] + _swap_halves(d * s_ref[...])

    if sum_heads:
        grid = (T // tm,)
        in_slab = pl.BlockSpec((H, tm, R), lambda i: (0, i, 0))
        out_slab = pl.BlockSpec((tm, R), lambda i: (i, 0))
        tab = pl.BlockSpec((tm, R), lambda i: (i, 0))
        out_shape = jax.ShapeDtypeStruct((T, R), F32)
        sem = ("parallel",)
    else:
        grid = (H, T // tm)
        in_slab = pl.BlockSpec((None, tm, R), lambda h, i: (h, i, 0))
        out_slab = in_slab
        tab = pl.BlockSpec((tm, R), lambda h, i: (i, 0))
        out_shape = jax.ShapeDtypeStruct((H, T, R), F32)
        sem = ("parallel", "parallel")
    return pl.pallas_call(
        body, name=name, grid=grid, in_specs=[in_slab, tab, tab], out_specs=out_slab,
        out_shape=out_shape, compiler_params=_cparams(sem),
    )(dy, cos2, sgn_sin)


def _log_sigmoid(z):
    return jnp.minimum(z, 0.0) - jnp.log(1.0 + jnp.exp(-jnp.abs(z)))


def _gate_cumsum(name, fl, b, tb):
    H, T = fl.shape

    def body(f_ref, b_ref, c_ref, carry):
        @pl.when(pl.program_id(0) == 0)
        def _():
            carry[...] = jnp.zeros_like(carry)

        ls = _log_sigmoid(f_ref[...] + b_ref[...])
        src = lax.broadcasted_iota(jnp.int32, (tb, tb), 0)
        dst = lax.broadcasted_iota(jnp.int32, (tb, tb), 1)
        tri = (src <= dst).astype(F32)
        c = lax.dot_general(ls, tri, NN, precision=lax.Precision.HIGHEST,
                            preferred_element_type=F32) + carry[...]
        c_ref[...] = c
        carry[...] = carry[...] + jnp.sum(ls, axis=-1, keepdims=True)

    return pl.pallas_call(
        body, name=name, grid=(T // tb,),
        in_specs=[pl.BlockSpec((H, tb), lambda i: (0, i)), pl.BlockSpec((H, 1), lambda i: (0, 0))],
        out_specs=pl.BlockSpec((H, tb), lambda i: (0, i)),
        out_shape=jax.ShapeDtypeStruct((H, T), F32),
        scratch_shapes=[pltpu.VMEM((H, 1), F32)],
        compiler_params=_cparams(("arbitrary",)),
    )(fl, b)


def _gate_cumsum_bwd(name, d_query, d_key, fl, b, tb):
    H, T = fl.shape
    nb = T // tb

    def body(dq_ref, dk_ref, f_ref, b_ref, dfl_ref, db_ref, carry):
        @pl.when(pl.program_id(0) == 0)
        def _():
            carry[...] = jnp.zeros_like(carry)
            db_ref[...] = jnp.zeros_like(db_ref)

        d = dq_ref[...] - dk_ref[...]
        src = lax.broadcasted_iota(jnp.int32, (tb, tb), 0)
        dst = lax.broadcasted_iota(jnp.int32, (tb, tb), 1)
        tri = (src >= dst).astype(F32)
        dls = lax.dot_general(d, tri, NN, precision=lax.Precision.HIGHEST,
                              preferred_element_type=F32) + carry[...]
        z = f_ref[...] + b_ref[...]
        dfl = dls * (1.0 / (1.0 + jnp.exp(z)))
        dfl_ref[...] = dfl
        db_ref[...] += jnp.sum(dfl, axis=-1, keepdims=True)
        carry[...] = carry[...] + jnp.sum(d, axis=-1, keepdims=True)

    blk = pl.BlockSpec((H, tb), lambda i: (0, nb - 1 - i))
    vec = pl.BlockSpec((H, 1), lambda i: (0, 0))
    return pl.pallas_call(
        body, name=name, grid=(nb,),
        in_specs=[blk, blk, blk, vec], out_specs=[blk, vec],
        out_shape=[jax.ShapeDtypeStruct((H, T), F32), jax.ShapeDtypeStruct((H, 1), F32)],
        scratch_shapes=[pltpu.VMEM((H, 1), F32)],
        compiler_params=_cparams(("arbitrary",)),
    )(d_query, d_key, fl, b)


def _causal_mask(tq, rows_are_queries):
    r = lax.broadcasted_iota(jnp.int32, (tq, tq), 0)
    c = lax.broadcasted_iota(jnp.int32, (tq, tq), 1)
    return (c <= r) if rows_are_queries else (r <= c)


def _chunk_rows(j, tq):
    return pl.ds(pl.multiple_of(j * tq, tq), tq)


def _column_as_row(col):
    return jnp.broadcast_to(col, (col.shape[0], 128)).T[:1, :]


def _flash_fwd(name, q, k, v_aug, dv, tq, exchange=None, q_head0=0, v_head0=0):
    H, T, dqk = k.shape
    dva = v_aug.shape[2]
    nq = T // tq

    def body(q_ref, k_ref, v_ref, o_ref, lse_ref, m_sc, acc_sc):
        qi = pl.program_id(1)
        m_sc[...] = jnp.full_like(m_sc, NEG)
        acc_sc[...] = jnp.zeros_like(acc_sc)

        def chunk(j, masked):
            rows = _chunk_rows(j, tq)
            s = _dot(q_ref[...], k_ref[rows, :], NT)
            if masked:
                s = jnp.where(_causal_mask(tq, True), s, NEG)
            m_prev = m_sc[...]
            m_new = jnp.maximum(m_prev, jnp.max(s, axis=1, keepdims=True))
            p = jnp.exp(s - jnp.tile(m_new, (1, tq // 128)))
            alpha = jnp.tile(jnp.exp(m_prev - m_new), (1, dva // 128))
            acc_sc[...] = alpha * acc_sc[...] + _dot(p.astype(BF16), v_ref[rows, :], NN)
            m_sc[...] = m_new

        def off_diagonal(j, carry):
            chunk(j, False)
            return carry

        lax.fori_loop(0, qi, off_diagonal, 0)
        chunk(qi, True)
        acc = acc_sc[...]
        l = acc[:, dv:dv + 1]
        o_ref[...] = acc[:, :dv] / l
        lse_ref[...] = _column_as_row(m_sc[:, :1] + jnp.log(l))

    (o, lse_rows), exchanged = _call_carrying(
        body, name, (H, nq),
        in_specs=[pl.BlockSpec((None, tq, dqk), lambda h, i: (h + q_head0, i, 0)),
                  pl.BlockSpec((None, T, dqk), lambda h, i: (h, 0, 0)),
                  pl.BlockSpec((None, T, dva), lambda h, i: (h + v_head0, 0, 0))],
        out_specs=[pl.BlockSpec((None, tq, dv), lambda h, i: (h, i, 0)),
                   pl.BlockSpec((None, 1, tq), lambda h, i: (h, 0, i))],
        out_shape=[jax.ShapeDtypeStruct((H, T, dv), F32), jax.ShapeDtypeStruct((H, 1, T), F32)],
        scratch_shapes=[pltpu.VMEM((tq, 128), F32), pltpu.VMEM((tq, dva), F32)],
        operands=(q, k, v_aug), exchange=exchange)
    return (o, lse_rows.reshape(H, T)), exchanged


def _row_dot(name, a, b):
    H, T, d = a.shape
    tm = _pick(T, (1024, 512, 256, 128))

    def body(a_ref, b_ref, o_ref):
        col = jnp.sum(a_ref[...].astype(F32) * b_ref[...].astype(F32), axis=-1, keepdims=True)
        o_ref[...] = _column_as_row(col)

    slab = pl.BlockSpec((None, tm, d), lambda h, i: (h, i, 0))
    return pl.pallas_call(
        body, name=name, grid=(H, T // tm), in_specs=[slab, slab],
        out_specs=pl.BlockSpec((None, 1, tm), lambda h, i: (h, 0, i)),
        out_shape=jax.ShapeDtypeStruct((H, 1, T), F32),
        compiler_params=_cparams(("parallel", "parallel")),
    )(a, b).reshape(H, T)


def _flash_bwd(name, q, k, v, do, scale, tq, exchange=None, v_head0=0, token_major_out=False):
    H, T, dqk = q.shape
    dva = v.shape[2]
    nq = T // tq

    def body(q_ref, k_ref, v_ref, do_ref, dq_ref, dk_ref, dv_ref, dk_sc, dv_sc):
        ki = pl.program_id(1)
        dk_sc[...] = jnp.zeros_like(dk_sc)
        dv_sc[...] = jnp.zeros_like(dv_sc)

        @pl.when(ki == 0)
        def _():
            dq_ref[...] = jnp.zeros_like(dq_ref)

        def chunk(i, masked):
            rows = _chunk_rows(i, tq)
            qb = q_ref[rows, :]
            dob = do_ref[rows, :]
            kb = k_ref[...]
            st = _dot(kb, qb, NT)
            if masked:
                st = jnp.where(_causal_mask(tq, False), st, NEG)
            pt = jnp.exp(st)
            dv_sc[...] += _dot(pt.astype(BF16), dob, NN)
            dst = (pt * _dot(v_ref[...], dob, NT)).astype(BF16)
            dk_sc[...] += _dot(dst, qb, NN)
            dq_ref[rows, :] += _dot(dst, kb, TN)

        def off_diagonal(i, carry):
            chunk(i, False)
            return carry

        chunk(ki, True)
        lax.fori_loop(ki + 1, nq, off_diagonal, 0)
        dk_ref[...] = dk_sc[...]
        dv_ref[...] = dv_sc[...]

        @pl.when(ki == nq - 1)
        def _():
            dq_ref[...] = dq_ref[...] * scale

    whole_q = pl.BlockSpec((None, T, dqk), lambda h, j: (h, 0, 0))
    k_spec = pl.BlockSpec((None, tq, dqk), lambda h, j: (h, j, 0))
    v_spec = pl.BlockSpec((None, tq, dva), lambda h, j: (h, j, 0))
    v_in_spec = pl.BlockSpec((None, tq, dva), lambda h, j: (h + v_head0, j, 0))
    if token_major_out:
        out_specs = [pl.BlockSpec((T, dqk), lambda h, j: (0, h)), pl.BlockSpec((tq, dqk), lambda h, j: (j, h)),
                     pl.BlockSpec((tq, dva), lambda h, j: (j, h))]
        out_shape = [jax.ShapeDtypeStruct((T, H * dqk), F32), jax.ShapeDtypeStruct((T, H * dqk), F32),
                     jax.ShapeDtypeStruct((T, H * dva), F32)]
    else:
        out_specs = [whole_q, k_spec, v_spec]
        out_shape = [jax.ShapeDtypeStruct((H, T, dqk), F32), jax.ShapeDtypeStruct((H, T, dqk), F32),
                     jax.ShapeDtypeStruct((H, T, dva), F32)]
    return _call_carrying(
        body, name, (H, nq),
        in_specs=[whole_q, k_spec, v_in_spec, pl.BlockSpec((None, T, dva), lambda h, j: (h, 0, 0))],
        out_specs=out_specs, out_shape=out_shape,
        scratch_shapes=[pltpu.VMEM((tq, dqk), F32), pltpu.VMEM((tq, dva), F32)],
        operands=(q, k, v, do), exchange=exchange)


def _pack_head_grads(name, parts, W, head_dim, picks):
    T, HW = parts[0].shape
    H = HW // W
    n = len(parts)
    tm = _pick(T, (512, 256, 128))
    sels = [(jnp.arange(HW)[:, None] == (jnp.arange(128)[None, :] * W + col)).astype(F32) for _, col in picks]

    def body(*refs):
        xs = refs[:n]
        sel_refs = refs[n:n + len(picks)]
        packed_ref = refs[n + len(picks)]
        row_refs = refs[n + len(picks) + 1:]
        for a in range(n):
            x = xs[a][...]
            heads = [x[:, h * W:h * W + head_dim] for h in range(H)]
            packed_ref[:, a * H * head_dim:(a + 1) * H * head_dim] = (
                jnp.concatenate(heads, axis=-1).astype(packed_ref.dtype))
        for (a, _), s_ref, r_ref in zip(picks, sel_refs, row_refs):
            cols = lax.dot_general(xs[a][...], s_ref[...], NN, precision=lax.Precision.HIGHEST,
                                   preferred_element_type=F32)
            r_ref[...] = cols.T[:H, :]

    row = pl.BlockSpec((tm, HW), lambda i: (i, 0))
    out = pl.pallas_call(
        body, name=name, grid=(T // tm,),
        in_specs=[row] * n + [pl.BlockSpec((HW, 128), lambda i: (0, 0))] * len(picks),
        out_specs=[pl.BlockSpec((tm, n * H * head_dim), lambda i: (i, 0))]
        + [pl.BlockSpec((H, tm), lambda i: (0, i))] * len(picks),
        out_shape=[jax.ShapeDtypeStruct((T, n * H * head_dim), BF16)]
        + [jax.ShapeDtypeStruct((H, T), F32)] * len(picks),
        compiler_params=_cparams(("parallel",)),
    )(*parts, *sels)
    return out[0], out[1:]


def _adamw_math(w, g, m, v):
    m = ADAM_B1 * m + (1.0 - ADAM_B1) * g
    v = ADAM_B2 * v + (1.0 - ADAM_B2) * (g * g)
    m_hat = m / (1.0 - ADAM_B1 ** ADAM_STEP)
    v_hat = v / (1.0 - ADAM_B2 ** ADAM_STEP)
    delta = -ADAM_LR * (m_hat / (jnp.sqrt(v_hat) + ADAM_EPS) + ADAM_WD * w)
    return delta, m, v


def _adamw(name, parts, w, m, v):
    P, R, C = parts.shape
    tr = _pick(R, (256, 128, 64, 32, 16, 8))

    def body(p_ref, w_ref, m_ref, v_ref, g_out, d_out, m_out, v_out):
        g = p_ref[0].astype(F32)
        for i in range(1, P):
            g = g + p_ref[i].astype(F32)
        delta, m_new, v_new = _adamw_math(w_ref[...], g, m_ref[...], v_ref[...])
        g_out[...] = g
        d_out[...] = delta
        m_out[...] = m_new
        v_out[...] = v_new

    blk = pl.BlockSpec((tr, C), lambda i: (i, 0))
    sds = jax.ShapeDtypeStruct((R, C), F32)
    return pl.pallas_call(
        body, name=name, grid=(R // tr,),
        in_specs=[pl.BlockSpec((P, tr, C), lambda i: (0, i, 0)), blk, blk, blk],
        out_specs=[blk] * 4, out_shape=[sds] * 4,
        compiler_params=_cparams(("parallel",)),
    )(parts, w, m, v)


def _my_position():
    return lax.axis_index("x"), lax.axis_index("y"), lax.axis_index("c")


def _slot(p):
    return 4 * p[0] + 2 * p[1] + p[2]


def _flip(p, k):
    return tuple((1 - p[i]) if (k >> (2 - i)) & 1 else p[i] for i in range(3))


def _allgather_weights(shards):
    n = len(shards)

    def body(*refs):
        ins = refs[:n]
        outs = refs[n:2 * n]
        send_sems, recv_sems, local_sems = refs[2 * n:]
        x, y, c = _my_position()
        me, sibling = (x, y, c), (x, y, 1 - c)
        chips = [(1 - x, y), (x, 1 - y), (1 - x, 1 - y)]

        def copy(a, k, block, to, src=None):
            dst = outs[a].at[_slot(block)]
            return pltpu.make_async_remote_copy(
                src_ref=dst if src is None else src, dst_ref=dst,
                send_sem=send_sems.at[7 * a + k], recv_sem=recv_sems.at[7 * a + k],
                device_id=to, device_id_type=MESH)

        started = []
        for a in range(n):
            mine = pltpu.make_async_copy(ins[a], outs[a].at[_slot(me)], local_sems.at[a])
            mine.start()
            started.append(mine)
        first = []
        for a in range(n):
            first.append(copy(a, 0, me, sibling, src=ins[a]))
            first += [copy(a, 1 + j, me, (*chip, c), src=ins[a]) for j, chip in enumerate(chips)]
        for cp in first:
            cp.start()
        passed = []
        for j, chip in enumerate(chips):
            for a in range(n):
                copy(a, 1 + j, (*chip, c), me).wait_recv()
                fwd = copy(a, 4 + j, (*chip, c), sibling)
                fwd.start()
                passed.append(fwd)
        for a in range(n):
            copy(a, 0, sibling, me).wait_recv()
            for j, chip in enumerate(chips):
                copy(a, 4 + j, (*chip, 1 - c), me).wait_recv()
        for cp in first + passed:
            cp.wait_send()
        for mine in started:
            mine.wait()

    hbm = pl.BlockSpec(memory_space=pl.ANY)
    return pl.pallas_call(
        body, name="allgather_weights",
        in_specs=[hbm] * n, out_specs=[hbm] * n,
        out_shape=[jax.ShapeDtypeStruct((N_DEV,) + s.shape, s.dtype) for s in shards],
        scratch_shapes=[pltpu.SemaphoreType.DMA((7 * n,)), pltpu.SemaphoreType.DMA((7 * n,)),
                        pltpu.SemaphoreType.DMA((n,))],
        compiler_params=pltpu.CompilerParams(has_side_effects=True),
    )(*shards)


def _exchange_copies(kind, x_in, x_out, send_sems, recv_sems, local_sems, receives=True):
    me = _my_position()
    mine = _slot(me)
    local, sends, recvs = [], [], []
    for a in range(len(x_in)):
        src = x_in[a] if kind == "gather" else x_in[a].at[mine]
        local.append(pltpu.make_async_copy(src, x_out[a].at[mine], local_sems.at[a]))
    for k in range(1, N_DEV):
        peer = _flip(me, k)
        theirs = _slot(peer)
        for a in range(len(x_in)):
            src = x_in[a] if kind == "gather" else x_in[a].at[theirs]
            ends = [(x_out[a].at[mine], sends)] + ([(x_out[a].at[theirs], recvs)] if receives else [])
            for dst, group in ends:
                group.append(pltpu.make_async_remote_copy(
                    src_ref=src, dst_ref=dst, send_sem=send_sems.at[7 * a + k - 1],
                    recv_sem=recv_sems.at[7 * a + k - 1], device_id=peer, device_id_type=MESH))
    return local, sends, recvs


def _exchange_out_shapes(kind, arrays):
    return [jax.ShapeDtypeStruct(((N_DEV,) + a.shape) if kind == "gather" else a.shape, a.dtype)
            for a in arrays]


def _exchange_sems(n):
    return [pltpu.SemaphoreType.DMA((7 * n,)), pltpu.SemaphoreType.DMA((7 * n,)),
            pltpu.SemaphoreType.DMA((n,))]


def _call_carrying(body, name, grid, in_specs, out_specs, out_shape, scratch_shapes, operands, exchange,
                   sequential=False):
    if exchange is None:
        out = pl.pallas_call(
            body, name=name, grid=grid, in_specs=in_specs, out_specs=out_specs, out_shape=out_shape,
            scratch_shapes=scratch_shapes,
            compiler_params=_cparams((("arbitrary",) if sequential else ("parallel",))
                                     + ("arbitrary",) * (len(grid) - 1)),
        )(*operands)
        return out, None
    kind, arrays = exchange
    n, n_in, n_out, n_sc = len(arrays), len(in_specs), len(out_specs), len(scratch_shapes)

    def full_body(*refs):
        ins, refs = refs[:n_in], refs[n_in:]
        x_in, refs = refs[:n], refs[n:]
        outs, refs = refs[:n_out], refs[n_out:]
        x_out, refs = refs[:n], refs[n:]
        scratch, sems = refs[:n_sc], refs[n_sc:]
        first = last = None
        for axis, size in enumerate(grid):
            at_start = pl.program_id(axis) == 0
            at_end = pl.program_id(axis) == size - 1
            first = at_start if first is None else jnp.logical_and(first, at_start)
            last = at_end if last is None else jnp.logical_and(last, at_end)

        @pl.when(first)
        def _():
            local, sends, _ = _exchange_copies(kind, x_in, x_out, *sems, receives=False)
            for cp in local + sends:
                cp.start()

        body(*ins, *outs, *scratch)

        @pl.when(last)
        def _():
            local, sends, recvs = _exchange_copies(kind, x_in, x_out, *sems)
            for cp in recvs:
                cp.wait_recv()
            for cp in sends:
                cp.wait_send()
            for cp in local:
                cp.wait()

    hbm = pl.BlockSpec(memory_space=pl.ANY)
    out = pl.pallas_call(
        full_body, name=name, grid=grid,
        in_specs=list(in_specs) + [hbm] * n, out_specs=list(out_specs) + [hbm] * n,
        out_shape=list(out_shape) + _exchange_out_shapes(kind, arrays),
        scratch_shapes=list(scratch_shapes) + _exchange_sems(n),
        compiler_params=pltpu.CompilerParams(dimension_semantics=("arbitrary",) * len(grid),
                                             vmem_limit_bytes=VMEM_LIMIT_BYTES, has_side_effects=True),
    )(*operands, *arrays)
    return out[:n_out], out[n_out:]


def _allreduce_small(v):
    R, C = v.shape

    def body(v_ref, o_ref, buf, send_sems, recv_sems):
        me = _my_position()
        buf[_slot(me)] = v_ref[...]
        sends = []
        for k in range(1, N_DEV):
            peer = _flip(me, k)
            cp = pltpu.make_async_remote_copy(
                src_ref=v_ref, dst_ref=buf.at[_slot(me)],
                send_sem=send_sems.at[k - 1], recv_sem=recv_sems.at[k - 1],
                device_id=peer, device_id_type=MESH)
            cp.start()
            sends.append(cp)
        for k in range(1, N_DEV):
            peer = _flip(me, k)
            pltpu.make_async_remote_copy(
                src_ref=v_ref, dst_ref=buf.at[_slot(peer)],
                send_sem=send_sems.at[k - 1], recv_sem=recv_sems.at[k - 1],
                device_id=peer, device_id_type=MESH).wait_recv()
        for cp in sends:
            cp.wait_send()
        tot = buf[0]
        for s in range(1, N_DEV):
            tot = tot + buf[s]
        o_ref[...] = tot

    vm = pl.BlockSpec(memory_space=pltpu.VMEM)
    return pl.pallas_call(
        body, name="allreduce_small",
        in_specs=[vm], out_specs=vm, out_shape=jax.ShapeDtypeStruct((R, C), F32),
        scratch_shapes=[pltpu.VMEM((N_DEV, R, C), F32), pltpu.SemaphoreType.DMA((7,)),
                        pltpu.SemaphoreType.DMA((7,))],
        compiler_params=pltpu.CompilerParams(has_side_effects=True),
    )(v)


def _to_heads(t, heads):
    T = t.shape[0]
    return t.reshape(T, heads, t.shape[1] // heads).transpose(1, 0, 2)


def _from_heads(t):
    H, T, d = t.shape
    return t.transpose(1, 0, 2).reshape(T, H * d)


def _widen(t, width, ones_at=None, pieces_at=None, pieces=None):
    out = jnp.pad(t, ((0, 0), (0, 0), (0, width - t.shape[-1])))
    lane = lax.broadcasted_iota(jnp.int32, (1, 1, width), 2)
    if ones_at is not None:
        out = jnp.where((lane >= ones_at) & (lane < ones_at + 3), jnp.ones((), BF16), out)
    if pieces_at is not None:
        for i in range(3):
            out = jnp.where(lane == pieces_at + i, pieces[i][:, :, None], out)
    return out


def _split3(t):
    hi = lax.reduce_precision(t, 8, 7)
    r = t - hi
    mid = lax.reduce_precision(r, 8, 7)
    lo = lax.reduce_precision(r - mid, 8, 7)
    return hi.astype(BF16), mid.astype(BF16), lo.astype(BF16)


def _pad_cols(t, n):
    return jnp.pad(t, ((0, 0), (0, n - t.shape[1])))


def _pack_small(mix, ffn, kv, fin, kva, qa, bf, last):
    row6 = jnp.concatenate([kva.reshape(-1), qa.reshape(-1), bf.reshape(-1),
                            jnp.zeros((D_MODEL - KV_LORA - Q_LORA - FOX_HEADS,), F32)])
    return jnp.stack([mix[0], mix[1], ffn[0], ffn[1], kv.reshape(-1), fin.reshape(-1), row6, last])


def _unpack_small(p):
    mix = p[0:2]
    ffn = p[2:4]
    kv = p[4]
    fin = p[5]
    kva = p[6, :KV_LORA]
    qa = p[6, KV_LORA:KV_LORA + Q_LORA].reshape(1, Q_LORA)
    bf = p[6, KV_LORA + Q_LORA:KV_LORA + Q_LORA + FOX_HEADS].reshape(1, FOX_HEADS)
    return mix, ffn, bf, kv, kva, qa, fin


def _mlp_fwd(tag, xin, g, w_up, w_down):
    h = _rms(f"{tag}_norm", xin, g, BF16)

    def act(acc):
        r = jnp.maximum(acc, 0.0)
        return acc, r * r

    u, a = _mm(f"{tag}_up", h, w_up, "nn", (BF16, BF16), epi=act)
    xout = _mm(f"{tag}_down", a, w_down, "nn", (F32,), epi=lambda acc, r: (acc + r,), extras=(xin,))
    return xout, (h, u, a)


def _mlp_bwd(tag, gout, xin, g, w_up, w_down, saved):
    h, u, a = saved
    dw_down = _mm_tn(f"{tag}_dwdown", a, gout)
    du = _mm(f"{tag}_du", gout, w_down, "nt", (BF16,),
             epi=lambda acc, uu: (acc * (2.0 * jnp.maximum(uu.astype(F32), 0.0)),), extras=(u,))
    dw_up = _mm_tn(f"{tag}_dwup", h, du)
    gin, dg = _mm_rms_bwd(f"{tag}_dh_norm_bwd", du, w_up, xin, g, gout)
    return gin, dg, dw_up, dw_down


def kernel(x, norm_mix_g, norm_ffn_g, fox_w_in, fox_b_f, fox_w_out, kv_norm_g, mla_w_kv_a, mla_kv_a_norm_g, mla_w_kv_b, mla_w_q_a, mla_q_a_norm_g, mla_w_q_b, mla_w_out, ffn_w_up, ffn_w_down, final_norm_g, loss_target, m_norm_mix_g, m_norm_ffn_g, m_fox_w_in, m_fox_b_f, m_fox_w_out, m_kv_norm_g, m_mla_w_kv_a, m_mla_kv_a_norm_g, m_mla_w_kv_b, m_mla_w_q_a, m_mla_q_a_norm_g, m_mla_w_q_b, m_mla_w_out, m_ffn_w_up, m_ffn_w_down, m_final_norm_g, v_norm_mix_g, v_norm_ffn_g, v_fox_w_in, v_fox_b_f, v_fox_w_out, v_kv_norm_g, v_mla_w_kv_a, v_mla_kv_a_norm_g, v_mla_w_kv_b, v_mla_w_q_a, v_mla_q_a_norm_g, v_mla_w_q_b, v_mla_w_out, v_ffn_w_up, v_ffn_w_down, v_final_norm_g):
    T = x.shape[1]
    D = D_MODEL
    tq = 512 if T >= 2048 else 128
    x0 = x[0]
    tgt = loss_target[0]

    gat_fox = _allgather_weights([fox_w_in[0].astype(BF16), fox_w_out[0].astype(BF16)])
    later_shards = [s.astype(BF16) for s in (mla_w_kv_a, mla_w_kv_b, mla_w_q_a[0], mla_w_q_b[0],
                                             mla_w_out[0], ffn_w_up, ffn_w_down)]
    w_in = gat_fox[0].transpose(1, 0, 2).reshape(D, 3 * D + FOX_HEADS)
    w_qkv = w_in[:, :3 * D]
    w_f = _pad_cols(w_in[:, 3 * D:], 128)
    w_fo = gat_fox[1].reshape(D, D)
    g_mix0, g_mix1 = norm_mix_g[0:1], norm_mix_g[1:2]
    g_ffn0, g_ffn1 = norm_ffn_g[0:1], norm_ffn_g[1:2]
    g_kv = kv_norm_g.reshape(1, D)
    g_kva = mla_kv_a_norm_g.reshape(1, KV_LORA)
    g_qa = mla_q_a_norm_g.reshape(1, Q_LORA)
    g_fin = final_norm_g.reshape(1, D)

    inv = 1.0 / (ROPE_BASE ** (jnp.arange(0, QK_ROPE, 2, dtype=F32) / QK_ROPE))
    ang = jnp.arange(T, dtype=F32)[:, None] * inv[None, :]
    cos, sin = jnp.cos(ang), jnp.sin(ang)
    cos2 = jnp.concatenate([cos, cos], axis=-1)
    sgn_sin = jnp.concatenate([-sin, sin], axis=-1)

    h0 = _rms("l0_mix_norm", x0, g_mix0, BF16)
    fl_pad = _mm("fox_gate_logit", h0, w_f, "nn", (F32,))
    fl = fl_pad[:, :FOX_HEADS].T
    b_f = fox_b_f.reshape(FOX_HEADS, 1)
    cgate = _gate_cumsum("fox_gate_scan", fl, b_f, tq)
    fox_scale = FOX_HEAD_DIM ** -0.5
    col_scale = jnp.where(jnp.arange(3 * D) < D, fox_scale, 1.0).astype(BF16)
    tail = jnp.arange(FOX_AUG - FOX_HEAD_DIM)
    ones_q = (tail < 3).astype(F32)
    consts_k = ((tail >= 4) & (tail < 7)).astype(F32) + (tail == 3).astype(F32) * (1.0 / fox_scale)
    tails = jnp.broadcast_to(jnp.stack([ones_q, consts_k, ones_q])[:, None, None, :],
                             (3, FOX_HEADS, 1, FOX_AUG - FOX_HEAD_DIM)).reshape(3 * FOX_HEADS, 1, -1)
    qkv_h = _mm_head_slabs("fox_qkv", h0, w_qkv * col_scale, tails, FOX_HEAD_DIM, BF16)
    fk_aug = _widen(qkv_h[FOX_HEADS:2 * FOX_HEADS], FOX_AUG, pieces_at=FOX_HEAD_DIM,
                    pieces=_split3(-cgate))
    (fo, flse), gat = _flash_fwd("fox_attn", qkv_h, fk_aug, qkv_h, FOX_HEAD_DIM, tq,
                                 exchange=("gather", later_shards), q_head0=0, v_head0=2 * FOX_HEADS)
    w_kva = _pad_cols(gat[0].reshape(D, KV_LORA + QK_ROPE), KV_A_PAD)
    w_kvb_h = gat[1]
    w_qa = gat[2].reshape(D, Q_LORA)
    w_qb_h = gat[3]
    w_mo = gat[4].reshape(D, D)
    w_up = gat[5].transpose(1, 2, 0, 3).reshape(2, D, D_FF)
    w_down = gat[6].transpose(1, 0, 2, 3).reshape(2, D_FF, D)
    fctx = _from_heads(fo).astype(BF16)
    x1 = _mm("fox_out", fctx, w_fo, "nn", (F32,), epi=lambda acc, r: (acc + r,), extras=(x0,))
    x2, mlp0 = _mlp_fwd("l0_ffn", x1, g_ffn0, w_up[0], w_down[0])

    src = _rms("kv_norm", x2, g_kv, BF16)
    kva = _mm("kv_a", src, w_kva, "nn", (F32,))
    kva_lat = kva[:, :KV_LORA]
    c_kv = _rms("kv_a_norm", kva_lat, g_kva, BF16)
    k_rope = _rope("k_rope", kva[:, KV_LORA:KV_LORA + QK_ROPE][None], cos2, sgn_sin, BF16)
    kvb_h = _mm_heads("kv_b", c_kv, w_kvb_h, BF16)
    mk = jnp.concatenate([kvb_h[:, :, :QK_NOPE],
                          jnp.broadcast_to(k_rope, (MLA_HEADS, T, QK_ROPE))], axis=-1)
    mv = kvb_h[:, :, QK_NOPE:]

    h1 = _rms("l1_mix_norm", x2, g_mix1, BF16)
    qa = _mm("q_a", h1, w_qa, "nn", (F32,))
    c_q = _rms("q_a_norm", qa, g_qa, BF16)
    mla_scale = (QK_NOPE + QK_ROPE) ** -0.5
    mq = _mla_q_proj("q_b", c_q, w_qb_h, cos2, sgn_sin, mla_scale)
    mv_aug = _widen(mv, MLA_AUG, ones_at=V_HEAD)
    (mo, mlse), _ = _flash_fwd("mla_attn", mq, mk, mv_aug, V_HEAD, tq)
    mctx = _from_heads(mo).astype(BF16)
    x3 = _mm("mla_out", mctx, w_mo, "nn", (F32,), epi=lambda acc, r: (acc + r,), extras=(x2,))
    x4, mlp1 = _mlp_fwd("l1_ffn", x3, g_ffn1, w_up[1], w_down[1])

    g4, dg_fin, loss_vec = _loss_head("loss_head", x4, g_fin, tgt)

    g3, dg_ffn1, dw_up1, dw_down1 = _mlp_bwd("l1_ffn", g4, x3, g_ffn1, w_up[1], w_down[1], mlp1)

    dw_mo = _mm_tn("mla_out_dw", mctx, g3)
    dmo = _to_heads(_mm("mla_out_dx", g3, w_mo, "nt", (BF16,)), MLA_HEADS)
    mdelta = _row_dot("mla_delta", mo, dmo)
    dqk = QK_NOPE + QK_ROPE
    mq_bwd = _widen(mq, MLA_AUG, pieces_at=dqk, pieces=_split3(-mlse))
    mk_bwd = _widen(mk, MLA_AUG, ones_at=dqk)
    mdo_aug = _widen(dmo, MLA_AUG, pieces_at=V_HEAD, pieces=_split3(-mdelta))
    (mdq, mdk, mdv), _ = _flash_bwd("mla_attn_bwd", mq_bwd, mk_bwd, mv_aug, mdo_aug, mla_scale, tq)
    mdq = mdq[:, :, :dqk]
    mdk = mdk[:, :, :dqk]
    mdv = mdv[:, :, :V_HEAD]
    dq_rope = _rope_bwd("q_rope_bwd", mdq[:, :, QK_NOPE:], cos2, sgn_sin, False)
    dqf_h = jnp.concatenate([mdq[:, :, :QK_NOPE], dq_rope], axis=-1)
    dw_qb_h = _mm_heads_dw("q_b_dw", c_q, dqf_h)
    dc_q = _mm_heads_dx("q_b_dx", dqf_h, w_qb_h)
    dqa, dg_qa = _rms_bwd("q_a_norm_bwd", qa, g_qa, dc_q)
    dw_qa = _mm_tn("q_a_dw", h1, dqa)
    g2a, dg_mix1 = _mm_rms_bwd("q_a_dx_norm_bwd", dqa, w_qa, x2, g_mix1, g3)

    dk_rope = _rope_bwd("k_rope_bwd", mdk[:, :, QK_NOPE:], cos2, sgn_sin, True)
    dkvb_h = jnp.concatenate([mdk[:, :, :QK_NOPE], mdv], axis=-1)
    dw_kvb_h = _mm_heads_dw("kv_b_dw", c_kv, dkvb_h)
    dc_kv = _mm_heads_dx("kv_b_dx", dkvb_h, w_kvb_h)
    dkva_lat, dg_kva = _rms_bwd("kv_a_norm_bwd", kva_lat, g_kva, dc_kv)
    dkva = _pad_cols(jnp.concatenate([dkva_lat, dk_rope], axis=-1), KV_A_PAD)
    dw_kva = _mm_tn("kv_a_dw", src, dkva)[:, :KV_LORA + QK_ROPE]
    g2, dg_kv = _mm_rms_bwd("kv_a_dx_norm_bwd", dkva, w_kva, x2, g_kv, g2a)

    g1, dg_ffn0, dw_up0, dw_down0 = _mlp_bwd("l0_ffn", g2, x1, g_ffn0, w_up[0], w_down[0], mlp0)

    dw_fo = _mm_tn("fox_out_dw", fctx, g1)
    dfo = _to_heads(_mm("fox_out_dx", g1, w_fo, "nt", (BF16,)), FOX_HEADS)
    fdelta = _row_dot("fox_delta", fo, dfo)
    fq_bwd = _widen(qkv_h[:FOX_HEADS], FOX_AUG, pieces_at=FOX_HEAD_DIM + 4, pieces=_split3(-flse))
    fdo_aug = _widen(dfo, FOX_AUG, pieces_at=FOX_HEAD_DIM, pieces=_split3(-fdelta))
    dw_up = jnp.stack([dw_up0, dw_up1])
    dw_down = jnp.stack([dw_down0, dw_down1])
    early = [
        dw_fo.reshape(N_DEV, D // N_DEV, D),
        dw_kva.reshape(N_DEV, D // N_DEV, KV_LORA + QK_ROPE),
        dw_kvb_h,
        dw_qa.reshape(N_DEV, D // N_DEV, Q_LORA),
        dw_qb_h,
        dw_mo.reshape(N_DEV, D // N_DEV, D),
        dw_up.reshape(2, D, N_DEV, -1).transpose(2, 0, 1, 3),
        dw_down.reshape(2, N_DEV, D_FF // N_DEV, D).transpose(1, 0, 2, 3),
    ]
    fd_aug, early_parts = _flash_bwd(
        "fox_attn_bwd", fq_bwd, fk_aug, qkv_h, fdo_aug, fox_scale, tq,
        exchange=("scatter", [g.astype(BF16) for g in early]), v_head0=2 * FOX_HEADS, token_major_out=True)
    dqkv, (ds_rows, ds_cols) = _pack_head_grads("fox_grad_pack", list(fd_aug), FOX_AUG, FOX_HEAD_DIM,
                                                picks=((0, FOX_HEAD_DIM + 3), (1, FOX_HEAD_DIM)))
    dfl, db_f = _gate_cumsum_bwd("fox_gate_scan_bwd", ds_rows, ds_cols, fl, b_f, tq)
    dfl_pad = _pad_cols(dfl.T, 128)
    dw_qkv = _mm_tn("fox_qkv_dw", h0, dqkv)
    dw_f = _mm_tn("fox_gate_dw", h0, dfl_pad)[:, :FOX_HEADS]
    dw_in = jnp.concatenate([dw_qkv, dw_f], axis=-1)
    dh0a = _mm("fox_gate_dx", dfl_pad, w_f, "nt", (F32,))
    late = dw_in.reshape(D, N_DEV, -1).transpose(1, 0, 2).astype(BF16)
    grad_x, dg_mix0, late_parts = _mm_rms_bwd("fox_qkv_dx_norm_bwd", dqkv, w_qkv, x0, g_mix0, g1, add=dh0a,
                                              exchange=("scatter", [late]))

    parts = list(late_parts) + list(early_parts)

    names = ["fox_w_in", "fox_w_out", "mla_w_kv_a", "mla_w_kv_b", "mla_w_q_a", "mla_w_q_b",
             "mla_w_out", "ffn_w_up", "ffn_w_down"]
    moms = [m_fox_w_in, m_fox_w_out, m_mla_w_kv_a, m_mla_w_kv_b, m_mla_w_q_a, m_mla_w_q_b,
            m_mla_w_out, m_ffn_w_up, m_ffn_w_down]
    vars_ = [v_fox_w_in, v_fox_w_out, v_mla_w_kv_a, v_mla_w_kv_b, v_mla_w_q_a, v_mla_w_q_b,
             v_mla_w_out, v_ffn_w_up, v_ffn_w_down]
    full = [fox_w_in, fox_w_out, mla_w_kv_a, mla_w_kv_b, mla_w_q_a, mla_w_q_b, mla_w_out,
            ffn_w_up, ffn_w_down]
    big = {}
    for nm, p, w, m, v in zip(names, parts, full, moms, vars_):
        C = w.shape[-1]
        res = _adamw(f"adamw_{nm}", p.reshape(N_DEV, -1, C), w.reshape(-1, C), m.reshape(-1, C),
                     v.reshape(-1, C))
        big[nm] = [r.reshape(w.shape) for r in res]

    zrow = jnp.zeros((D,), F32)
    g_small = _pack_small(jnp.concatenate([dg_mix0, dg_mix1]), jnp.concatenate([dg_ffn0, dg_ffn1]),
                          dg_kv, dg_fin, dg_kva, dg_qa, db_f, zrow.at[0].set(loss_vec[0, 0]))
    tot_small = _allreduce_small(g_small)
    w_small = _pack_small(norm_mix_g, norm_ffn_g, kv_norm_g, final_norm_g, mla_kv_a_norm_g,
                          mla_q_a_norm_g, fox_b_f, zrow)
    m_small = _pack_small(m_norm_mix_g, m_norm_ffn_g, m_kv_norm_g, m_final_norm_g, m_mla_kv_a_norm_g,
                          m_mla_q_a_norm_g, m_fox_b_f, zrow)
    v_small = _pack_small(v_norm_mix_g, v_norm_ffn_g, v_kv_norm_g, v_final_norm_g, v_mla_kv_a_norm_g,
                          v_mla_q_a_norm_g, v_fox_b_f, zrow)
    small = _adamw("adamw_small", tot_small[None], w_small, m_small, v_small)
    loss = tot_small[7, 0]
    small = [_unpack_small(s) for s in small]

    def ordered(i):
        mix, ffn, bf, kv, kva, qa, fin = small[i]
        return [mix, ffn, big["fox_w_in"][i], bf, big["fox_w_out"][i], kv, big["mla_w_kv_a"][i], kva,
                big["mla_w_kv_b"][i], big["mla_w_q_a"][i], qa, big["mla_w_q_b"][i],
                big["mla_w_out"][i], big["ffn_w_up"][i], big["ffn_w_down"][i], fin]

    return (loss, grad_x[None], *ordered(0), *ordered(1), *ordered(2), *ordered(3))
```

```python
import functools
import math

import jax
import jax.numpy as jnp
from jax import lax
from jax.experimental import pallas as pl
from jax.experimental.pallas import tpu as pltpu

F32 = jnp.float32
BF16 = jnp.bfloat16
MESH = pl.DeviceIdType.MESH

N_DEV = 8
D_MODEL = 1024
FOX_HEADS = 16
FOX_HEAD_DIM = 64
FOX_AUG = 128
MLA_AUG = 256
MLA_HEADS = 8
QK_NOPE = 128
QK_ROPE = 64
V_HEAD = 128
Q_LORA = 384
KV_LORA = 256
KV_A_PAD = 384
D_FF = 4096
ROPE_BASE = 10000.0
EPS = 1e-6
NEG = -1e30

ADAM_LR = 0.001
ADAM_B1 = 0.9
ADAM_B2 = 0.999
ADAM_EPS = 1e-08
ADAM_WD = 0.01
ADAM_STEP = 10

VMEM_LIMIT_BYTES = 56 * 1024 * 1024

NN = (((1,), (0,)), ((), ()))
NT = (((1,), (1,)), ((), ()))
TN = (((0,), (0,)), ((), ()))
_FORMS = {"nn": NN, "nt": NT}


def _cparams(sem=None):
    return pltpu.CompilerParams(dimension_semantics=sem, vmem_limit_bytes=VMEM_LIMIT_BYTES)


def _pick(n, cands):
    for c in cands:
        if c <= n and n % c == 0:
            return c
    return n


def _dot(a, b, dims):
    return lax.dot_general(a, b, dims, preferred_element_type=F32)


def _mm(name, a, b, form, out_dtypes, epi=None, extras=(), tm=1024, tn=None):
    M, K = a.shape
    N = b.shape[1] if form == "nn" else b.shape[0]
    tm = _pick(M, (tm, 512, 256, 128))
    tn = _pick(N, (tn or (1024 if K <= 1024 else 512), 512, 384, 256, 128))
    n_ex = len(extras)
    n_out = len(out_dtypes)
    cast_once = a.dtype != BF16

    def body(*refs):
        a_ref, b_ref = refs[0], refs[1]
        ex = refs[2:2 + n_ex]
        outs = refs[2 + n_ex:2 + n_ex + n_out]
        if cast_once:
            a_sc = refs[2 + n_ex + n_out]

            @pl.when(pl.program_id(1) == 0)
            def _():
                a_sc[...] = a_ref[...].astype(BF16)

            av = a_sc[...]
        else:
            av = a_ref[...]
        acc = _dot(av, b_ref[...].astype(BF16), _FORMS[form])
        res = epi(acc, *[e[...] for e in ex]) if epi is not None else (acc,)
        for o_ref, r in zip(outs, res):
            o_ref[...] = r.astype(o_ref.dtype)

    if form == "nn":
        b_spec = pl.BlockSpec((K, tn), lambda i, j: (0, j))
    else:
        b_spec = pl.BlockSpec((tn, K), lambda i, j: (j, 0))
    tile = pl.BlockSpec((tm, tn), lambda i, j: (i, j))
    out = pl.pallas_call(
        body, name=name, grid=(M // tm, N // tn),
        in_specs=[pl.BlockSpec((tm, K), lambda i, j: (i, 0)), b_spec] + [tile] * n_ex,
        out_specs=[tile] * n_out,
        out_shape=[jax.ShapeDtypeStruct((M, N), dt) for dt in out_dtypes],
        scratch_shapes=[pltpu.VMEM((tm, K), BF16)] if cast_once else [],
        compiler_params=_cparams(("parallel", "arbitrary")),
    )(a, b, *extras)
    return out if n_out > 1 else out[0]


def _mm_tn(name, a, b):
    T, Ka = a.shape
    N = b.shape[1]
    tk = _pick(Ka, (1024, 512, 384, 256, 128))
    tn = _pick(N, (1024, 768, 512, 384, 256, 128))
    tt = _pick(T, (1024, 512, 256, 128))

    def body(a_ref, b_ref, o_ref):
        @pl.when(pl.program_id(2) == 0)
        def _():
            o_ref[...] = jnp.zeros_like(o_ref)

        o_ref[...] += _dot(a_ref[...].astype(BF16), b_ref[...].astype(BF16), TN)

    return pl.pallas_call(
        body, name=name, grid=(Ka // tk, N // tn, T // tt),
        in_specs=[pl.BlockSpec((tt, tk), lambda i, j, t: (t, i)),
                  pl.BlockSpec((tt, tn), lambda i, j, t: (t, j))],
        out_specs=pl.BlockSpec((tk, tn), lambda i, j, t: (i, j)),
        out_shape=jax.ShapeDtypeStruct((Ka, N), F32),
        compiler_params=_cparams(("parallel", "parallel", "arbitrary")),
    )(a, b)


def _mm_heads(name, a, w, out_dtype):
    T, K = a.shape
    H, _, N = w.shape
    tm = _pick(T, (1024, 512, 256, 128))

    def body(a_ref, w_ref, o_ref):
        av = a_ref[...].astype(BF16)
        for s in range(H):
            o_ref[s] = _dot(av, w_ref[s].astype(BF16), NN).astype(o_ref.dtype)

    return pl.pallas_call(
        body, name=name, grid=(T // tm,),
        in_specs=[pl.BlockSpec((tm, K), lambda i: (i, 0)), pl.BlockSpec((H, K, N), lambda i: (0, 0, 0))],
        out_specs=pl.BlockSpec((H, tm, N), lambda i: (0, i, 0)),
        out_shape=jax.ShapeDtypeStruct((H, T, N), out_dtype),
        compiler_params=_cparams(("parallel",)),
    )(a, w)


def _mm_head_slabs(name, a, w, tails, head_dim, out_dtype, heads_per_step=8):
    T, K = a.shape
    S, _, tail = tails.shape
    hb = heads_per_step
    tm = _pick(T, (1024, 512, 256, 128))

    def body(a_ref, w_ref, t_ref, o_ref):
        acc = _dot(a_ref[...].astype(BF16), w_ref[...].astype(BF16), NN)
        for s in range(hb):
            slab = jnp.concatenate([acc[:, s * head_dim:(s + 1) * head_dim],
                                    jnp.broadcast_to(t_ref[s], (tm, tail))], axis=-1)
            o_ref[s] = slab.astype(o_ref.dtype)

    return pl.pallas_call(
        body, name=name, grid=(T // tm, S // hb),
        in_specs=[pl.BlockSpec((tm, K), lambda i, j: (i, 0)),
                  pl.BlockSpec((K, hb * head_dim), lambda i, j: (0, j)),
                  pl.BlockSpec((hb, 1, tail), lambda i, j: (j, 0, 0))],
        out_specs=pl.BlockSpec((hb, tm, head_dim + tail), lambda i, j: (j, i, 0)),
        out_shape=jax.ShapeDtypeStruct((S, T, head_dim + tail), out_dtype),
        compiler_params=_cparams(("parallel", "arbitrary")),
    )(a, w, tails)


def _mm_heads_dw(name, a, g):
    T, K = a.shape
    H, _, N = g.shape
    tt = _pick(T, (1024, 512, 256, 128))

    def body(a_ref, g_ref, o_ref):
        @pl.when(pl.program_id(0) == 0)
        def _():
            o_ref[...] = jnp.zeros_like(o_ref)

        av = a_ref[...].astype(BF16)
        for s in range(H):
            o_ref[s] += _dot(av, g_ref[s].astype(BF16), TN)

    return pl.pallas_call(
        body, name=name, grid=(T // tt,),
        in_specs=[pl.BlockSpec((tt, K), lambda t: (t, 0)), pl.BlockSpec((H, tt, N), lambda t: (0, t, 0))],
        out_specs=pl.BlockSpec((H, K, N), lambda t: (0, 0, 0)),
        out_shape=jax.ShapeDtypeStruct((H, K, N), F32),
        compiler_params=_cparams(("arbitrary",)),
    )(a, g)


def _mm_heads_dx(name, g, w):
    H, T, N = g.shape
    K = w.shape[1]
    tm = _pick(T, (512, 256, 128))

    def body(g_ref, w_ref, o_ref):
        acc = _dot(g_ref[0].astype(BF16), w_ref[0].astype(BF16), NT)
        for s in range(1, H):
            acc = acc + _dot(g_ref[s].astype(BF16), w_ref[s].astype(BF16), NT)
        o_ref[...] = acc

    return pl.pallas_call(
        body, name=name, grid=(T // tm,),
        in_specs=[pl.BlockSpec((H, tm, N), lambda i: (0, i, 0)), pl.BlockSpec((H, K, N), lambda i: (0, 0, 0))],
        out_specs=pl.BlockSpec((tm, K), lambda i: (i, 0)),
        out_shape=jax.ShapeDtypeStruct((T, K), F32),
        compiler_params=_cparams(("parallel",)),
    )(g, w)


def _rms(name, x, g, out_dtype):
    T, D = x.shape
    tm = _pick(T, (1024, 512, 256, 128))

    def body(x_ref, g_ref, o_ref):
        xf = x_ref[...]
        r = lax.rsqrt(jnp.mean(xf * xf, axis=-1, keepdims=True) + EPS)
        o_ref[...] = (xf * r * g_ref[...]).astype(o_ref.dtype)

    return pl.pallas_call(
        body, name=name, grid=(T // tm,),
        in_specs=[pl.BlockSpec((tm, D), lambda i: (i, 0)), pl.BlockSpec((1, D), lambda i: (0, 0))],
        out_specs=pl.BlockSpec((tm, D), lambda i: (i, 0)),
        out_shape=jax.ShapeDtypeStruct((T, D), out_dtype),
        compiler_params=_cparams(("parallel",)),
    )(x, g)


def _rms_bwd(name, x, g, dh, dres=None):
    T, D = x.shape
    tm = _pick(T, (512, 256, 128))
    has_res = dres is not None

    def body(*refs):
        if has_res:
            x_ref, g_ref, dh_ref, dres_ref, dx_ref, dg_ref = refs
        else:
            x_ref, g_ref, dh_ref, dx_ref, dg_ref = refs

        @pl.when(pl.program_id(0) == 0)
        def _():
            dg_ref[...] = jnp.zeros_like(dg_ref)

        xf = x_ref[...]
        r = lax.rsqrt(jnp.mean(xf * xf, axis=-1, keepdims=True) + EPS)
        xhat = xf * r
        dy = dh_ref[...].astype(F32)
        dxh = dy * g_ref[...]
        dx = r * (dxh - xhat * jnp.mean(dxh * xhat, axis=-1, keepdims=True))
        if has_res:
            dx = dx + dres_ref[...]
        dx_ref[...] = dx
        dg_ref[...] += jnp.sum(dy * xhat, axis=0, keepdims=True)

    row = pl.BlockSpec((tm, D), lambda i: (i, 0))
    vec = pl.BlockSpec((1, D), lambda i: (0, 0))
    ins = [x, g, dh] + ([dres] if has_res else [])
    return pl.pallas_call(
        body, name=name, grid=(T // tm,),
        in_specs=[row, vec, row] + ([row] if has_res else []),
        out_specs=[row, vec],
        out_shape=[jax.ShapeDtypeStruct((T, D), F32), jax.ShapeDtypeStruct((1, D), F32)],
        compiler_params=_cparams(("arbitrary",)),
    )(*ins)


def _mm_rms_bwd(name, a, b, x, g, dres, add=None, exchange=None):
    T, K = a.shape
    D = b.shape[0]
    tm = _pick(T, (512, 256, 128))
    has_add = add is not None

    def body(*refs):
        a_ref, b_ref, x_ref, g_ref, dres_ref = refs[:5]
        dx_ref, dg_ref = refs[-2:]

        @pl.when(pl.program_id(0) == 0)
        def _():
            dg_ref[...] = jnp.zeros_like(dg_ref)

        dy = _dot(a_ref[...].astype(BF16), b_ref[...].astype(BF16), NT)
        if has_add:
            dy = dy + refs[5][...]
        xf = x_ref[...]
        r = lax.rsqrt(jnp.mean(xf * xf, axis=-1, keepdims=True) + EPS)
        xhat = xf * r
        dxh = dy * g_ref[...]
        dx_ref[...] = r * (dxh - xhat * jnp.mean(dxh * xhat, axis=-1, keepdims=True)) + dres_ref[...]
        dg_ref[...] += jnp.sum(dy * xhat, axis=0, keepdims=True)

    row = pl.BlockSpec((tm, D), lambda i: (i, 0))
    vec = pl.BlockSpec((1, D), lambda i: (0, 0))
    (dx, dg), exchanged = _call_carrying(
        body, name, (T // tm,),
        in_specs=[pl.BlockSpec((tm, K), lambda i: (i, 0)), pl.BlockSpec((D, K), lambda i: (0, 0)), row, vec, row]
        + ([row] if has_add else []),
        out_specs=[row, vec],
        out_shape=[jax.ShapeDtypeStruct((T, D), F32), jax.ShapeDtypeStruct((1, D), F32)],
        scratch_shapes=[], operands=(a, b, x, g, dres) + ((add,) if has_add else ()), exchange=exchange,
        sequential=True)
    return (dx, dg) if exchange is None else (dx, dg, exchanged)


def _loss_head(name, x, g, tgt):
    T, D = x.shape
    tm = _pick(T, (512, 256, 128))

    def body(x_ref, g_ref, t_ref, dx_ref, dg_ref, loss_ref):
        @pl.when(pl.program_id(0) == 0)
        def _():
            dg_ref[...] = jnp.zeros_like(dg_ref)
            loss_ref[...] = jnp.zeros_like(loss_ref)

        xf = x_ref[...]
        r = lax.rsqrt(jnp.mean(xf * xf, axis=-1, keepdims=True) + EPS)
        xhat = xf * r
        gv = g_ref[...]
        err = xhat * gv - t_ref[...]
        row_loss = jnp.mean(err * err, axis=-1, keepdims=True)
        loss_ref[...] += 0.5 * jnp.sum(row_loss, axis=0, keepdims=True)
        dy = err * (1.0 / D)
        dxh = dy * gv
        dx_ref[...] = r * (dxh - xhat * jnp.mean(dxh * xhat, axis=-1, keepdims=True))
        dg_ref[...] += jnp.sum(dy * xhat, axis=0, keepdims=True)

    row = pl.BlockSpec((tm, D), lambda i: (i, 0))
    vec = pl.BlockSpec((1, D), lambda i: (0, 0))
    return pl.pallas_call(
        body, name=name, grid=(T // tm,),
        in_specs=[row, vec, row],
        out_specs=[row, vec, pl.BlockSpec((1, 128), lambda i: (0, 0))],
        out_shape=[jax.ShapeDtypeStruct((T, D), F32), jax.ShapeDtypeStruct((1, D), F32),
                   jax.ShapeDtypeStruct((1, 128), F32)],
        compiler_params=_cparams(("arbitrary",)),
    )(x, g, tgt)


def _swap_halves(t):
    half = t.shape[-1] // 2
    return jnp.concatenate([t[:, half:], t[:, :half]], axis=-1)


def _rope(name, t, cos2, sgn_sin, out_dtype):
    H, T, R = t.shape
    tm = _pick(T, (1024, 512, 256, 128))

    def body(t_ref, c_ref, s_ref, o_ref):
        tf = t_ref[...].astype(F32)
        o_ref[...] = (tf * c_ref[...] + _swap_halves(tf) * s_ref[...]).astype(o_ref.dtype)

    slab = pl.BlockSpec((None, tm, R), lambda h, i: (h, i, 0))
    tab = pl.BlockSpec((tm, R), lambda h, i: (i, 0))
    return pl.pallas_call(
        body, name=name, grid=(H, T // tm),
        in_specs=[slab, tab, tab], out_specs=slab,
        out_shape=jax.ShapeDtypeStruct((H, T, R), out_dtype),
        compiler_params=_cparams(("parallel", "parallel")),
    )(t, cos2, sgn_sin)


def _mla_q_proj(name, a, w, cos2, sgn_sin, scale):
    T, K = a.shape
    H, _, W = w.shape
    R = cos2.shape[1]
    tm = _pick(T, (1024, 512, 256, 128))

    def body(a_ref, w_ref, c_ref, s_ref, o_ref):
        av = a_ref[...].astype(BF16)
        for h in range(H):
            qf = _dot(av, w_ref[h].astype(BF16), NN)
            r = qf[:, W - R:]
            roped = r * c_ref[...] + _swap_halves(r) * s_ref[...]
            o_ref[h] = (jnp.concatenate([qf[:, :W - R], roped], axis=-1) * scale).astype(o_ref.dtype)

    tab = pl.BlockSpec((tm, R), lambda i: (i, 0))
    return pl.pallas_call(
        body, name=name, grid=(T // tm,),
        in_specs=[pl.BlockSpec((tm, K), lambda i: (i, 0)), pl.BlockSpec((H, K, W), lambda i: (0, 0, 0)), tab, tab],
        out_specs=pl.BlockSpec((H, tm, W), lambda i: (0, i, 0)),
        out_shape=jax.ShapeDtypeStruct((H, T, W), BF16),
        compiler_params=_cparams(("parallel",)),
    )(a, w, cos2, sgn_sin)


def _rope_bwd(name, dy, cos2, sgn_sin, sum_heads):
    H, T, R = dy.shape
    tm = _pick(T, (1024, 512, 256, 128))

    def body(d_ref, c_ref, s_ref, o_ref):
        d = d_ref[...]
        if sum_heads:
            tot = d[0]
            for h in range(1, H):
                tot = tot + d[h]
            d = tot
        o_ref[...] = d * c_ref[...] + _swap_halves(d * s_ref[...])

    if sum_heads:
        grid = (T // tm,)
        in_slab = pl.BlockSpec((H, tm, R), lambda i: (0, i, 0))
        out_slab = pl.BlockSpec((tm, R), lambda i: (i, 0))
        tab = pl.BlockSpec((tm, R), lambda i: (i, 0))
        out_shape = jax.ShapeDtypeStruct((T, R), F32)
        sem = ("parallel",)
    else:
        grid = (H, T // tm)
        in_slab = pl.BlockSpec((None, tm, R), lambda h, i: (h, i, 0))
        out_slab = in_slab
        tab = pl.BlockSpec((tm, R), lambda h, i: (i, 0))
        out_shape = jax.ShapeDtypeStruct((H, T, R), F32)
        sem = ("parallel", "parallel")
    return pl.pallas_call(
        body, name=name, grid=grid, in_specs=[in_slab, tab, tab], out_specs=out_slab,
        out_shape=out_shape, compiler_params=_cparams(sem),
    )(dy, cos2, sgn_sin)


def _log_sigmoid(z):
    return jnp.minimum(z, 0.0) - jnp.log(1.0 + jnp.exp(-jnp.abs(z)))


def _gate_cumsum(name, fl, b, tb):
    H, T = fl.shape

    def body(f_ref, b_ref, c_ref, carry):
        @pl.when(pl.program_id(0) == 0)
        def _():
            carry[...] = jnp.zeros_like(carry)

        ls = _log_sigmoid(f_ref[...] + b_ref[...])
        src = lax.broadcasted_iota(jnp.int32, (tb, tb), 0)
        dst = lax.broadcasted_iota(jnp.int32, (tb, tb), 1)
        tri = (src <= dst).astype(F32)
        c = lax.dot_general(ls, tri, NN, precision=lax.Precision.HIGHEST,
                            preferred_element_type=F32) + carry[...]
        c_ref[...] = c
        carry[...] = carry[...] + jnp.sum(ls, axis=-1, keepdims=True)

    return pl.pallas_call(
        body, name=name, grid=(T // tb,),
        in_specs=[pl.BlockSpec((H, tb), lambda i: (0, i)), pl.BlockSpec((H, 1), lambda i: (0, 0))],
        out_specs=pl.BlockSpec((H, tb), lambda i: (0, i)),
        out_shape=jax.ShapeDtypeStruct((H, T), F32),
        scratch_shapes=[pltpu.VMEM((H, 1), F32)],
        compiler_params=_cparams(("arbitrary",)),
    )(fl, b)


def _gate_cumsum_bwd(name, d_query, d_key, fl, b, tb):
    H, T = fl.shape
    nb = T // tb

    def body(dq_ref, dk_ref, f_ref, b_ref, dfl_ref, db_ref, carry):
        @pl.when(pl.program_id(0) == 0)
        def _():
            carry[...] = jnp.zeros_like(carry)
            db_ref[...] = jnp.zeros_like(db_ref)

        d = dq_ref[...] - dk_ref[...]
        src = lax.broadcasted_iota(jnp.int32, (tb, tb), 0)
        dst = lax.broadcasted_iota(jnp.int32, (tb, tb), 1)
        tri = (src >= dst).astype(F32)
        dls = lax.dot_general(d, tri, NN, precision=lax.Precision.HIGHEST,
                              preferred_element_type=F32) + carry[...]
        z = f_ref[...] + b_ref[...]
        dfl = dls * (1.0 / (1.0 + jnp.exp(z)))
        dfl_ref[...] = dfl
        db_ref[...] += jnp.sum(dfl, axis=-1, keepdims=True)
        carry[...] = carry[...] + jnp.sum(d, axis=-1, keepdims=True)

    blk = pl.BlockSpec((H, tb), lambda i: (0, nb - 1 - i))
    vec = pl.BlockSpec((H, 1), lambda i: (0, 0))
    return pl.pallas_call(
        body, name=name, grid=(nb,),
        in_specs=[blk, blk, blk, vec], out_specs=[blk, vec],
        out_shape=[jax.ShapeDtypeStruct((H, T), F32), jax.ShapeDtypeStruct((H, 1), F32)],
        scratch_shapes=[pltpu.VMEM((H, 1), F32)],
        compiler_params=_cparams(("arbitrary",)),
    )(d_query, d_key, fl, b)


def _causal_mask(tq, rows_are_queries):
    r = lax.broadcasted_iota(jnp.int32, (tq, tq), 0)
    c = lax.broadcasted_iota(jnp.int32, (tq, tq), 1)
    return (c <= r) if rows_are_queries else (r <= c)


def _chunk_rows(j, tq):
    return pl.ds(pl.multiple_of(j * tq, tq), tq)


def _column_as_row(col):
    return jnp.broadcast_to(col, (col.shape[0], 128)).T[:1, :]


def _flash_fwd(name, q, k, v_aug, dv, tq, exchange=None, q_head0=0, v_head0=0):
    H, T, dqk = k.shape
    dva = v_aug.shape[2]
    tk = tq
    tq = 2 * tk if T % (2 * tk) == 0 else tk
    r = tq // tk
    nq = T // tq

    def body(q_ref, k_ref, v_ref, o_ref, lse_ref, m_sc, acc_sc):
        qi = pl.program_id(1)
        m_sc[...] = jnp.full_like(m_sc, NEG)
        acc_sc[...] = jnp.zeros_like(acc_sc)

        def chunk(j, diag):
            rows = _chunk_rows(j, tk)
            live = slice(0 if diag is None else diag * tk, tq)
            n_live = tq - live.start
            s = _dot(q_ref[live, :], k_ref[rows, :], NT)
            if diag is not None:
                row = lax.broadcasted_iota(jnp.int32, (n_live, tk), 0)
                col = lax.broadcasted_iota(jnp.int32, (n_live, tk), 1)
                s = jnp.where(col <= row, s, NEG)
            m_prev = m_sc[live, :]
            m_new = jnp.maximum(m_prev, jnp.max(s, axis=1, keepdims=True))
            p = jnp.exp(s - jnp.tile(m_new, (1, tk // 128)))
            alpha = jnp.tile(jnp.exp(m_prev - m_new), (1, dva // 128))
            acc_sc[live, :] = alpha * acc_sc[live, :] + _dot(p.astype(BF16), v_ref[rows, :], NN)
            m_sc[live, :] = m_new

        def off_diagonal(j, carry):
            chunk(j, None)
            return carry

        lax.fori_loop(0, qi * r, off_diagonal, 0)
        for d in range(r):
            chunk(qi * r + d, d)
        acc = acc_sc[...]
        l = acc[:, dv:dv + 1]
        o_ref[...] = acc[:, :dv] / l
        lse_ref[...] = _column_as_row(m_sc[:, :1] + jnp.log(l))

    (o, lse_rows), exchanged = _call_carrying(
        body, name, (H, nq),
        in_specs=[pl.BlockSpec((None, tq, dqk), lambda h, i: (h + q_head0, i, 0)),
                  pl.BlockSpec((None, T, dqk), lambda h, i: (h, 0, 0)),
                  pl.BlockSpec((None, T, dva), lambda h, i: (h + v_head0, 0, 0))],
        out_specs=[pl.BlockSpec((None, tq, dv), lambda h, i: (h, i, 0)),
                   pl.BlockSpec((None, 1, tq), lambda h, i: (h, 0, i))],
        out_shape=[jax.ShapeDtypeStruct((H, T, dv), F32), jax.ShapeDtypeStruct((H, 1, T), F32)],
        scratch_shapes=[pltpu.VMEM((tq, 128), F32), pltpu.VMEM((tq, dva), F32)],
        operands=(q, k, v_aug), exchange=exchange)
    return (o, lse_rows.reshape(H, T)), exchanged


def _row_dot(name, a, b):
    H, T, d = a.shape
    tm = _pick(T, (1024, 512, 256, 128))

    def body(a_ref, b_ref, o_ref):
        col = jnp.sum(a_ref[...].astype(F32) * b_ref[...].astype(F32), axis=-1, keepdims=True)
        o_ref[...] = _column_as_row(col)

    slab = pl.BlockSpec((None, tm, d), lambda h, i: (h, i, 0))
    return pl.pallas_call(
        body, name=name, grid=(H, T // tm), in_specs=[slab, slab],
        out_specs=pl.BlockSpec((None, 1, tm), lambda h, i: (h, 0, i)),
        out_shape=jax.ShapeDtypeStruct((H, 1, T), F32),
        compiler_params=_cparams(("parallel", "parallel")),
    )(a, b).reshape(H, T)


def _flash_bwd(name, q, k, v, do, scale, tq, exchange=None, v_head0=0, token_major_out=False):
    H, T, dqk = q.shape
    dva = v.shape[2]
    nq = T // tq

    def body(q_ref, k_ref, v_ref, do_ref, dq_ref, dk_ref, dv_ref, dk_sc, dv_sc):
        ki = pl.program_id(1)
        dk_sc[...] = jnp.zeros_like(dk_sc)
        dv_sc[...] = jnp.zeros_like(dv_sc)

        @pl.when(ki == 0)
        def _():
            dq_ref[...] = jnp.zeros_like(dq_ref)

        def chunk(i, masked):
            rows = _chunk_rows(i, tq)
            qb = q_ref[rows, :]
            dob = do_ref[rows, :]
            kb = k_ref[...]
            st = _dot(kb, qb, NT)
            if masked:
                st = jnp.where(_causal_mask(tq, False), st, NEG)
            pt = jnp.exp(st)
            dv_sc[...] += _dot(pt.astype(BF16), dob, NN)
            dst = (pt * _dot(v_ref[...], dob, NT)).astype(BF16)
            dk_sc[...] += _dot(dst, qb, NN)
            dq_ref[rows, :] += _dot(dst, kb, TN)

        def off_diagonal(i, carry):
            chunk(i, False)
            return carry

        chunk(ki, True)
        lax.fori_loop(ki + 1, nq, off_diagonal, 0)
        dk_ref[...] = dk_sc[...]
        dv_ref[...] = dv_sc[...]

        @pl.when(ki == nq - 1)
        def _():
            dq_ref[...] = dq_ref[...] * scale

    whole_q = pl.BlockSpec((None, T, dqk), lambda h, j: (h, 0, 0))
    k_spec = pl.BlockSpec((None, tq, dqk), lambda h, j: (h, j, 0))
    v_spec = pl.BlockSpec((None, tq, dva), lambda h, j: (h, j, 0))
    v_in_spec = pl.BlockSpec((None, tq, dva), lambda h, j: (h + v_head0, j, 0))
    if token_major_out:
        out_specs = [pl.BlockSpec((T, dqk), lambda h, j: (0, h)), pl.BlockSpec((tq, dqk), lambda h, j: (j, h)),
                     pl.BlockSpec((tq, dva), lambda h, j: (j, h))]
        out_shape = [jax.ShapeDtypeStruct((T, H * dqk), F32), jax.ShapeDtypeStruct((T, H * dqk), F32),
                     jax.ShapeDtypeStruct((T, H * dva), F32)]
    else:
        out_specs = [whole_q, k_spec, v_spec]
        out_shape = [jax.ShapeDtypeStruct((H, T, dqk), F32), jax.ShapeDtypeStruct((H, T, dqk), F32),
                     jax.ShapeDtypeStruct((H, T, dva), F32)]
    return _call_carrying(
        body, name, (H, nq),
        in_specs=[whole_q, k_spec, v_in_spec, pl.BlockSpec((None, T, dva), lambda h, j: (h, 0, 0))],
        out_specs=out_specs, out_shape=out_shape,
        scratch_shapes=[pltpu.VMEM((tq, dqk), F32), pltpu.VMEM((tq, dva), F32)],
        operands=(q, k, v, do), exchange=exchange)


def _pack_head_grads(name, parts, W, head_dim, picks):
    T, HW = parts[0].shape
    H = HW // W
    n = len(parts)
    tm = _pick(T, (512, 256, 128))
    sels = [(jnp.arange(HW)[:, None] == (jnp.arange(128)[None, :] * W + col)).astype(F32) for _, col in picks]

    def body(*refs):
        xs = refs[:n]
        sel_refs = refs[n:n + len(picks)]
        packed_ref = refs[n + len(picks)]
        row_refs = refs[n + len(picks) + 1:]
        for a in range(n):
            x = xs[a][...]
            heads = [x[:, h * W:h * W + head_dim] for h in range(H)]
            packed_ref[:, a * H * head_dim:(a + 1) * H * head_dim] = (
                jnp.concatenate(heads, axis=-1).astype(packed_ref.dtype))
        for (a, _), s_ref, r_ref in zip(picks, sel_refs, row_refs):
            cols = lax.dot_general(xs[a][...], s_ref[...], NN, precision=lax.Precision.HIGHEST,
                                   preferred_element_type=F32)
            r_ref[...] = cols.T[:H, :]

    row = pl.BlockSpec((tm, HW), lambda i: (i, 0))
    out = pl.pallas_call(
        body, name=name, grid=(T // tm,),
        in_specs=[row] * n + [pl.BlockSpec((HW, 128), lambda i: (0, 0))] * len(picks),
        out_specs=[pl.BlockSpec((tm, n * H * head_dim), lambda i: (i, 0))]
        + [pl.BlockSpec((H, tm), lambda i: (0, i))] * len(picks),
        out_shape=[jax.ShapeDtypeStruct((T, n * H * head_dim), BF16)]
        + [jax.ShapeDtypeStruct((H, T), F32)] * len(picks),
        compiler_params=_cparams(("parallel",)),
    )(*parts, *sels)
    return out[0], out[1:]


def _adamw_math(w, g, m, v):
    m = ADAM_B1 * m + (1.0 - ADAM_B1) * g
    v = ADAM_B2 * v + (1.0 - ADAM_B2) * (g * g)
    m_hat = m / (1.0 - ADAM_B1 ** ADAM_STEP)
    v_hat = v / (1.0 - ADAM_B2 ** ADAM_STEP)
    delta = -ADAM_LR * (m_hat / (jnp.sqrt(v_hat) + ADAM_EPS) + ADAM_WD * w)
    return delta, m, v


def _adamw(name, parts, w, m, v):
    P, R, C = parts.shape
    tr = _pick(R, (256, 128, 64, 32, 16, 8))

    def body(p_ref, w_ref, m_ref, v_ref, g_out, d_out, m_out, v_out):
        g = p_ref[0].astype(F32)
        for i in range(1, P):
            g = g + p_ref[i].astype(F32)
        delta, m_new, v_new = _adamw_math(w_ref[...], g, m_ref[...], v_ref[...])
        g_out[...] = g
        d_out[...] = delta
        m_out[...] = m_new
        v_out[...] = v_new

    blk = pl.BlockSpec((tr, C), lambda i: (i, 0))
    sds = jax.ShapeDtypeStruct((R, C), F32)
    return pl.pallas_call(
        body, name=name, grid=(R // tr,),
        in_specs=[pl.BlockSpec((P, tr, C), lambda i: (0, i, 0)), blk, blk, blk],
        out_specs=[blk] * 4, out_shape=[sds] * 4,
        compiler_params=_cparams(("parallel",)),
    )(parts, w, m, v)


def _my_position():
    return lax.axis_index("x"), lax.axis_index("y"), lax.axis_index("c")


def _slot(p):
    return 4 * p[0] + 2 * p[1] + p[2]


def _flip(p, k):
    return tuple((1 - p[i]) if (k >> (2 - i)) & 1 else p[i] for i in range(3))


def _allgather_weights(shards):
    n = len(shards)

    def body(*refs):
        ins = refs[:n]
        outs = refs[n:2 * n]
        send_sems, recv_sems, local_sems = refs[2 * n:]
        x, y, c = _my_position()
        me, sibling = (x, y, c), (x, y, 1 - c)
        chips = [(1 - x, y), (x, 1 - y), (1 - x, 1 - y)]

        def copy(a, k, block, to, src=None):
            dst = outs[a].at[_slot(block)]
            return pltpu.make_async_remote_copy(
                src_ref=dst if src is None else src, dst_ref=dst,
                send_sem=send_sems.at[7 * a + k], recv_sem=recv_sems.at[7 * a + k],
                device_id=to, device_id_type=MESH)

        started = []
        for a in range(n):
            mine = pltpu.make_async_copy(ins[a], outs[a].at[_slot(me)], local_sems.at[a])
            mine.start()
            started.append(mine)
        first = []
        for a in range(n):
            first.append(copy(a, 0, me, sibling, src=ins[a]))
            first += [copy(a, 1 + j, me, (*chip, c), src=ins[a]) for j, chip in enumerate(chips)]
        for cp in first:
            cp.start()
        passed = []
        for j, chip in enumerate(chips):
            for a in range(n):
                copy(a, 1 + j, (*chip, c), me).wait_recv()
                fwd = copy(a, 4 + j, (*chip, c), sibling)
                fwd.start()
                passed.append(fwd)
        for a in range(n):
            copy(a, 0, sibling, me).wait_recv()
            for j, chip in enumerate(chips):
                copy(a, 4 + j, (*chip, 1 - c), me).wait_recv()
        for cp in first + passed:
            cp.wait_send()
        for mine in started:
            mine.wait()

    hbm = pl.BlockSpec(memory_space=pl.ANY)
    return pl.pallas_call(
        body, name="allgather_weights",
        in_specs=[hbm] * n, out_specs=[hbm] * n,
        out_shape=[jax.ShapeDtypeStruct((N_DEV,) + s.shape, s.dtype) for s in shards],
        scratch_shapes=[pltpu.SemaphoreType.DMA((7 * n,)), pltpu.SemaphoreType.DMA((7 * n,)),
                        pltpu.SemaphoreType.DMA((n,))],
        compiler_params=pltpu.CompilerParams(has_side_effects=True),
    )(*shards)


def _exchange_copies(kind, x_in, x_out, send_sems, recv_sems, local_sems, receives=True):
    me = _my_position()
    mine = _slot(me)
    local, sends, recvs = [], [], []
    for a in range(len(x_in)):
        src = x_in[a] if kind == "gather" else x_in[a].at[mine]
        local.append(pltpu.make_async_copy(src, x_out[a].at[mine], local_sems.at[a]))
    for k in range(1, N_DEV):
        peer = _flip(me, k)
        theirs = _slot(peer)
        for a in range(len(x_in)):
            src = x_in[a] if kind == "gather" else x_in[a].at[theirs]
            ends = [(x_out[a].at[mine], sends)] + ([(x_out[a].at[theirs], recvs)] if receives else [])
            for dst, group in ends:
                group.append(pltpu.make_async_remote_copy(
                    src_ref=src, dst_ref=dst, send_sem=send_sems.at[7 * a + k - 1],
                    recv_sem=recv_sems.at[7 * a + k - 1], device_id=peer, device_id_type=MESH))
    return local, sends, recvs


def _exchange_out_shapes(kind, arrays):
    return [jax.ShapeDtypeStruct(((N_DEV,) + a.shape) if kind == "gather" else a.shape, a.dtype)
            for a in arrays]


def _exchange_sems(n):
    return [pltpu.SemaphoreType.DMA((7 * n,)), pltpu.SemaphoreType.DMA((7 * n,)),
            pltpu.SemaphoreType.DMA((n,))]


def _call_carrying(body, name, grid, in_specs, out_specs, out_shape, scratch_shapes, operands, exchange,
                   sequential=False):
    if exchange is None:
        out = pl.pallas_call(
            body, name=name, grid=grid, in_specs=in_specs, out_specs=out_specs, out_shape=out_shape,
            scratch_shapes=scratch_shapes,
            compiler_params=_cparams((("arbitrary",) if sequential else ("parallel",))
                                     + ("arbitrary",) * (len(grid) - 1)),
        )(*operands)
        return out, None
    kind, arrays = exchange
    n, n_in, n_out, n_sc = len(arrays), len(in_specs), len(out_specs), len(scratch_shapes)

    def full_body(*refs):
        ins, refs = refs[:n_in], refs[n_in:]
        x_in, refs = refs[:n], refs[n:]
        outs, refs = refs[:n_out], refs[n_out:]
        x_out, refs = refs[:n], refs[n:]
        scratch, sems = refs[:n_sc], refs[n_sc:]
        first = last = None
        for axis, size in enumerate(grid):
            at_start = pl.program_id(axis) == 0
            at_end = pl.program_id(axis) == size - 1
            first = at_start if first is None else jnp.logical_and(first, at_start)
            last = at_end if last is None else jnp.logical_and(last, at_end)

        @pl.when(first)
        def _():
            local, sends, _ = _exchange_copies(kind, x_in, x_out, *sems, receives=False)
            for cp in local + sends:
                cp.start()

        body(*ins, *outs, *scratch)

        @pl.when(last)
        def _():
            local, sends, recvs = _exchange_copies(kind, x_in, x_out, *sems)
            for cp in recvs:
                cp.wait_recv()
            for cp in sends:
                cp.wait_send()
            for cp in local:
                cp.wait()

    hbm = pl.BlockSpec(memory_space=pl.ANY)
    out = pl.pallas_call(
        full_body, name=name, grid=grid,
        in_specs=list(in_specs) + [hbm] * n, out_specs=list(out_specs) + [hbm] * n,
        out_shape=list(out_shape) + _exchange_out_shapes(kind, arrays),
        scratch_shapes=list(scratch_shapes) + _exchange_sems(n),
        compiler_params=pltpu.CompilerParams(dimension_semantics=("arbitrary",) * len(grid),
                                             vmem_limit_bytes=VMEM_LIMIT_BYTES, has_side_effects=True),
    )(*operands, *arrays)
    return out[:n_out], out[n_out:]


def _allreduce_small(v):
    R, C = v.shape

    def body(v_ref, o_ref, buf, send_sems, recv_sems):
        me = _my_position()
        buf[_slot(me)] = v_ref[...]
        sends = []
        for k in range(1, N_DEV):
            peer = _flip(me, k)
            cp = pltpu.make_async_remote_copy(
                src_ref=v_ref, dst_ref=buf.at[_slot(me)],
                send_sem=send_sems.at[k - 1], recv_sem=recv_sems.at[k - 1],
                device_id=peer, device_id_type=MESH)
            cp.start()
            sends.append(cp)
        for k in range(1, N_DEV):
            peer = _flip(me, k)
            pltpu.make_async_remote_copy(
                src_ref=v_ref, dst_ref=buf.at[_slot(peer)],
                send_sem=send_sems.at[k - 1], recv_sem=recv_sems.at[k - 1],
                device_id=peer, device_id_type=MESH).wait_recv()
        for cp in sends:
            cp.wait_send()
        tot = buf[0]
        for s in range(1, N_DEV):
            tot = tot + buf[s]
        o_ref[...] = tot

    vm = pl.BlockSpec(memory_space=pltpu.VMEM)
    return pl.pallas_call(
        body, name="allreduce_small",
        in_specs=[vm], out_specs=vm, out_shape=jax.ShapeDtypeStruct((R, C), F32),
        scratch_shapes=[pltpu.VMEM((N_DEV, R, C), F32), pltpu.SemaphoreType.DMA((7,)),
                        pltpu.SemaphoreType.DMA((7,))],
        compiler_params=pltpu.CompilerParams(has_side_effects=True),
    )(v)


def _to_heads(t, heads):
    T = t.shape[0]
    return t.reshape(T, heads, t.shape[1] // heads).transpose(1, 0, 2)


def _from_heads(t):
    H, T, d = t.shape
    return t.transpose(1, 0, 2).reshape(T, H * d)


def _widen(t, width, ones_at=None, pieces_at=None, pieces=None):
    out = jnp.pad(t, ((0, 0), (0, 0), (0, width - t.shape[-1])))
    lane = lax.broadcasted_iota(jnp.int32, (1, 1, width), 2)
    if ones_at is not None:
        out = jnp.where((lane >= ones_at) & (lane < ones_at + 3), jnp.ones((), BF16), out)
    if pieces_at is not None:
        for i in range(3):
            out = jnp.where(lane == pieces_at + i, pieces[i][:, :, None], out)
    return out


def _split3(t):
    hi = lax.reduce_precision(t, 8, 7)
    r = t - hi
    mid = lax.reduce_precision(r, 8, 7)
    lo = lax.reduce_precision(r - mid, 8, 7)
    return hi.astype(BF16), mid.astype(BF16), lo.astype(BF16)


def _pad_cols(t, n):
    return jnp.pad(t, ((0, 0), (0, n - t.shape[1])))


def _pack_small(mix, ffn, kv, fin, kva, qa, bf, last):
    row6 = jnp.concatenate([kva.reshape(-1), qa.reshape(-1), bf.reshape(-1),
                            jnp.zeros((D_MODEL - KV_LORA - Q_LORA - FOX_HEADS,), F32)])
    return jnp.stack([mix[0], mix[1], ffn[0], ffn[1], kv.reshape(-1), fin.reshape(-1), row6, last])


def _unpack_small(p):
    mix = p[0:2]
    ffn = p[2:4]
    kv = p[4]
    fin = p[5]
    kva = p[6, :KV_LORA]
    qa = p[6, KV_LORA:KV_LORA + Q_LORA].reshape(1, Q_LORA)
    bf = p[6, KV_LORA + Q_LORA:KV_LORA + Q_LORA + FOX_HEADS].reshape(1, FOX_HEADS)
    return mix, ffn, bf, kv, kva, qa, fin


def _mlp_fwd(tag, xin, g, w_up, w_down):
    h = _rms(f"{tag}_norm", xin, g, BF16)

    def act(acc):
        r = jnp.maximum(acc, 0.0)
        return acc, r * r

    u, a = _mm(f"{tag}_up", h, w_up, "nn", (BF16, BF16), epi=act)
    xout = _mm(f"{tag}_down", a, w_down, "nn", (F32,), epi=lambda acc, r: (acc + r,), extras=(xin,))
    return xout, (h, u, a)


def _mlp_bwd(tag, gout, xin, g, w_up, w_down, saved):
    h, u, a = saved
    dw_down = _mm_tn(f"{tag}_dwdown", a, gout)
    du = _mm(f"{tag}_du", gout, w_down, "nt", (BF16,),
             epi=lambda acc, uu: (acc * (2.0 * jnp.maximum(uu.astype(F32), 0.0)),), extras=(u,))
    dw_up = _mm_tn(f"{tag}_dwup", h, du)
    gin, dg = _mm_rms_bwd(f"{tag}_dh_norm_bwd", du, w_up, xin, g, gout)
    return gin, dg, dw_up, dw_down


def kernel(x, norm_mix_g, norm_ffn_g, fox_w_in, fox_b_f, fox_w_out, kv_norm_g, mla_w_kv_a, mla_kv_a_norm_g, mla_w_kv_b, mla_w_q_a, mla_q_a_norm_g, mla_w_q_b, mla_w_out, ffn_w_up, ffn_w_down, final_norm_g, loss_target, m_norm_mix_g, m_norm_ffn_g, m_fox_w_in, m_fox_b_f, m_fox_w_out, m_kv_norm_g, m_mla_w_kv_a, m_mla_kv_a_norm_g, m_mla_w_kv_b, m_mla_w_q_a, m_mla_q_a_norm_g, m_mla_w_q_b, m_mla_w_out, m_ffn_w_up, m_ffn_w_down, m_final_norm_g, v_norm_mix_g, v_norm_ffn_g, v_fox_w_in, v_fox_b_f, v_fox_w_out, v_kv_norm_g, v_mla_w_kv_a, v_mla_kv_a_norm_g, v_mla_w_kv_b, v_mla_w_q_a, v_mla_q_a_norm_g, v_mla_w_q_b, v_mla_w_out, v_ffn_w_up, v_ffn_w_down, v_final_norm_g):
    T = x.shape[1]
    D = D_MODEL
    tq = 512 if T >= 2048 else 128
    x0 = x[0]
    tgt = loss_target[0]

    gat_fox = _allgather_weights([fox_w_in[0].astype(BF16), fox_w_out[0].astype(BF16)])
    later_shards = [s.astype(BF16) for s in (mla_w_kv_a, mla_w_kv_b, mla_w_q_a[0], mla_w_q_b[0],
                                             mla_w_out[0], ffn_w_up, ffn_w_down)]
    w_in = gat_fox[0].transpose(1, 0, 2).reshape(D, 3 * D + FOX_HEADS)
    w_qkv = w_in[:, :3 * D]
    w_f = _pad_cols(w_in[:, 3 * D:], 128)
    w_fo = gat_fox[1].reshape(D, D)
    g_mix0, g_mix1 = norm_mix_g[0:1], norm_mix_g[1:2]
    g_ffn0, g_ffn1 = norm_ffn_g[0:1], norm_ffn_g[1:2]
    g_kv = kv_norm_g.reshape(1, D)
    g_kva = mla_kv_a_norm_g.reshape(1, KV_LORA)
    g_qa = mla_q_a_norm_g.reshape(1, Q_LORA)
    g_fin = final_norm_g.reshape(1, D)

    inv = 1.0 / (ROPE_BASE ** (jnp.arange(0, QK_ROPE, 2, dtype=F32) / QK_ROPE))
    ang = jnp.arange(T, dtype=F32)[:, None] * inv[None, :]
    cos, sin = jnp.cos(ang), jnp.sin(ang)
    cos2 = jnp.concatenate([cos, cos], axis=-1)
    sgn_sin = jnp.concatenate([-sin, sin], axis=-1)

    h0 = _rms("l0_mix_norm", x0, g_mix0, BF16)
    fl_pad = _mm("fox_gate_logit", h0, w_f, "nn", (F32,))
    fl = fl_pad[:, :FOX_HEADS].T
    b_f = fox_b_f.reshape(FOX_HEADS, 1)
    cgate = _gate_cumsum("fox_gate_scan", fl, b_f, tq)
    fox_scale = FOX_HEAD_DIM ** -0.5
    col_scale = jnp.where(jnp.arange(3 * D) < D, fox_scale, 1.0).astype(BF16)
    tail = jnp.arange(FOX_AUG - FOX_HEAD_DIM)
    ones_q = (tail < 3).astype(F32)
    consts_k = ((tail >= 4) & (tail < 7)).astype(F32) + (tail == 3).astype(F32) * (1.0 / fox_scale)
    tails = jnp.broadcast_to(jnp.stack([ones_q, consts_k, ones_q])[:, None, None, :],
                             (3, FOX_HEADS, 1, FOX_AUG - FOX_HEAD_DIM)).reshape(3 * FOX_HEADS, 1, -1)
    qkv_h = _mm_head_slabs("fox_qkv", h0, w_qkv * col_scale, tails, FOX_HEAD_DIM, BF16)
    fk_aug = _widen(qkv_h[FOX_HEADS:2 * FOX_HEADS], FOX_AUG, pieces_at=FOX_HEAD_DIM,
                    pieces=_split3(-cgate))
    (fo, flse), gat = _flash_fwd("fox_attn", qkv_h, fk_aug, qkv_h, FOX_HEAD_DIM, tq,
                                 exchange=("gather", later_shards), q_head0=0, v_head0=2 * FOX_HEADS)
    w_kva = _pad_cols(gat[0].reshape(D, KV_LORA + QK_ROPE), KV_A_PAD)
    w_kvb_h = gat[1]
    w_qa = gat[2].reshape(D, Q_LORA)
    w_qb_h = gat[3]
    w_mo = gat[4].reshape(D, D)
    w_up = gat[5].transpose(1, 2, 0, 3).reshape(2, D, D_FF)
    w_down = gat[6].transpose(1, 0, 2, 3).reshape(2, D_FF, D)
    fctx = _from_heads(fo).astype(BF16)
    x1 = _mm("fox_out", fctx, w_fo, "nn", (F32,), epi=lambda acc, r: (acc + r,), extras=(x0,))
    x2, mlp0 = _mlp_fwd("l0_ffn", x1, g_ffn0, w_up[0], w_down[0])

    src = _rms("kv_norm", x2, g_kv, BF16)
    kva = _mm("kv_a", src, w_kva, "nn", (F32,))
    kva_lat = kva[:, :KV_LORA]
    c_kv = _rms("kv_a_norm", kva_lat, g_kva, BF16)
    k_rope = _rope("k_rope", kva[:, KV_LORA:KV_LORA + QK_ROPE][None], cos2, sgn_sin, BF16)
    kvb_h = _mm_heads("kv_b", c_kv, w_kvb_h, BF16)
    mk = jnp.concatenate([kvb_h[:, :, :QK_NOPE],
                          jnp.broadcast_to(k_rope, (MLA_HEADS, T, QK_ROPE))], axis=-1)
    mv = kvb_h[:, :, QK_NOPE:]

    h1 = _rms("l1_mix_norm", x2, g_mix1, BF16)
    qa = _mm("q_a", h1, w_qa, "nn", (F32,))
    c_q = _rms("q_a_norm", qa, g_qa, BF16)
    mla_scale = (QK_NOPE + QK_ROPE) ** -0.5
    mq = _mla_q_proj("q_b", c_q, w_qb_h, cos2, sgn_sin, mla_scale)
    mv_aug = _widen(mv, MLA_AUG, ones_at=V_HEAD)
    (mo, mlse), _ = _flash_fwd("mla_attn", mq, mk, mv_aug, V_HEAD, tq)
    mctx = _from_heads(mo).astype(BF16)
    x3 = _mm("mla_out", mctx, w_mo, "nn", (F32,), epi=lambda acc, r: (acc + r,), extras=(x2,))
    x4, mlp1 = _mlp_fwd("l1_ffn", x3, g_ffn1, w_up[1], w_down[1])

    g4, dg_fin, loss_vec = _loss_head("loss_head", x4, g_fin, tgt)

    g3, dg_ffn1, dw_up1, dw_down1 = _mlp_bwd("l1_ffn", g4, x3, g_ffn1, w_up[1], w_down[1], mlp1)

    dw_mo = _mm_tn("mla_out_dw", mctx, g3)
    dmo = _to_heads(_mm("mla_out_dx", g3, w_mo, "nt", (BF16,)), MLA_HEADS)
    mdelta = _row_dot("mla_delta", mo, dmo)
    dqk = QK_NOPE + QK_ROPE
    mq_bwd = _widen(mq, MLA_AUG, pieces_at=dqk, pieces=_split3(-mlse))
    mk_bwd = _widen(mk, MLA_AUG, ones_at=dqk)
    mdo_aug = _widen(dmo, MLA_AUG, pieces_at=V_HEAD, pieces=_split3(-mdelta))
    (mdq, mdk, mdv), _ = _flash_bwd("mla_attn_bwd", mq_bwd, mk_bwd, mv_aug, mdo_aug, mla_scale, tq)
    mdq = mdq[:, :, :dqk]
    mdk = mdk[:, :, :dqk]
    mdv = mdv[:, :, :V_HEAD]
    dq_rope = _rope_bwd("q_rope_bwd", mdq[:, :, QK_NOPE:], cos2, sgn_sin, False)
    dqf_h = jnp.concatenate([mdq[:, :, :QK_NOPE], dq_rope], axis=-1)
    dw_qb_h = _mm_heads_dw("q_b_dw", c_q, dqf_h)
    dc_q = _mm_heads_dx("q_b_dx", dqf_h, w_qb_h)
    dqa, dg_qa = _rms_bwd("q_a_norm_bwd", qa, g_qa, dc_q)
    dw_qa = _mm_tn("q_a_dw", h1, dqa)
    g2a, dg_mix1 = _mm_rms_bwd("q_a_dx_norm_bwd", dqa, w_qa, x2, g_mix1, g3)

    dk_rope = _rope_bwd("k_rope_bwd", mdk[:, :, QK_NOPE:], cos2, sgn_sin, True)
    dkvb_h = jnp.concatenate([mdk[:, :, :QK_NOPE], mdv], axis=-1)
    dw_kvb_h = _mm_heads_dw("kv_b_dw", c_kv, dkvb_h)
    dc_kv = _mm_heads_dx("kv_b_dx", dkvb_h, w_kvb_h)
    dkva_lat, dg_kva = _rms_bwd("kv_a_norm_bwd", kva_lat, g_kva, dc_kv)
    dkva = _pad_cols(jnp.concatenate([dkva_lat, dk_rope], axis=-1), KV_A_PAD)
    dw_kva = _mm_tn("kv_a_dw", src, dkva)[:, :KV_LORA + QK_ROPE]
    g2, dg_kv = _mm_rms_bwd("kv_a_dx_norm_bwd", dkva, w_kva, x2, g_kv, g2a)

    g1, dg_ffn0, dw_up0, dw_down0 = _mlp_bwd("l0_ffn", g2, x1, g_ffn0, w_up[0], w_down[0], mlp0)

    dw_fo = _mm_tn("fox_out_dw", fctx, g1)
    dfo = _to_heads(_mm("fox_out_dx", g1, w_fo, "nt", (BF16,)), FOX_HEADS)
    fdelta = _row_dot("fox_delta", fo, dfo)
    fq_bwd = _widen(qkv_h[:FOX_HEADS], FOX_AUG, pieces_at=FOX_HEAD_DIM + 4, pieces=_split3(-flse))
    fdo_aug = _widen(dfo, FOX_AUG, pieces_at=FOX_HEAD_DIM, pieces=_split3(-fdelta))
    dw_up = jnp.stack([dw_up0, dw_up1])
    dw_down = jnp.stack([dw_down0, dw_down1])
    early = [
        dw_fo.reshape(N_DEV, D // N_DEV, D),
        dw_kva.reshape(N_DEV, D // N_DEV, KV_LORA + QK_ROPE),
        dw_kvb_h,
        dw_qa.reshape(N_DEV, D // N_DEV, Q_LORA),
        dw_qb_h,
        dw_mo.reshape(N_DEV, D // N_DEV, D),
        dw_up.reshape(2, D, N_DEV, -1).transpose(2, 0, 1, 3),
        dw_down.reshape(2, N_DEV, D_FF // N_DEV, D).transpose(1, 0, 2, 3),
    ]
    fd_aug, early_parts = _flash_bwd(
        "fox_attn_bwd", fq_bwd, fk_aug, qkv_h, fdo_aug, fox_scale, tq,
        exchange=("scatter", [g.astype(BF16) for g in early]), v_head0=2 * FOX_HEADS, token_major_out=True)
    dqkv, (ds_rows, ds_cols) = _pack_head_grads("fox_grad_pack", list(fd_aug), FOX_AUG, FOX_HEAD_DIM,
                                                picks=((0, FOX_HEAD_DIM + 3), (1, FOX_HEAD_DIM)))
    dfl, db_f = _gate_cumsum_bwd("fox_gate_scan_bwd", ds_rows, ds_cols, fl, b_f, tq)
    dfl_pad = _pad_cols(dfl.T, 128)
    dw_qkv = _mm_tn("fox_qkv_dw", h0, dqkv)
    dw_f = _mm_tn("fox_gate_dw", h0, dfl_pad)[:, :FOX_HEADS]
    dw_in = jnp.concatenate([dw_qkv, dw_f], axis=-1)
    dh0a = _mm("fox_gate_dx", dfl_pad, w_f, "nt", (F32,))
    late = dw_in.reshape(D, N_DEV, -1).transpose(1, 0, 2).astype(BF16)
    grad_x, dg_mix0, late_parts = _mm_rms_bwd("fox_qkv_dx_norm_bwd", dqkv, w_qkv, x0, g_mix0, g1, add=dh0a,
                                              exchange=("scatter", [late]))

    parts = list(late_parts) + list(early_parts)

    names = ["fox_w_in", "fox_w_out", "mla_w_kv_a", "mla_w_kv_b", "mla_w_q_a", "mla_w_q_b",
             "mla_w_out", "ffn_w_up", "ffn_w_down"]
    moms = [m_fox_w_in, m_fox_w_out, m_mla_w_kv_a, m_mla_w_kv_b, m_mla_w_q_a, m_mla_w_q_b,
            m_mla_w_out, m_ffn_w_up, m_ffn_w_down]
    vars_ = [v_fox_w_in, v_fox_w_out, v_mla_w_kv_a, v_mla_w_kv_b, v_mla_w_q_a, v_mla_w_q_b,
             v_mla_w_out, v_ffn_w_up, v_ffn_w_down]
    full = [fox_w_in, fox_w_out, mla_w_kv_a, mla_w_kv_b, mla_w_q_a, mla_w_q_b, mla_w_out,
            ffn_w_up, ffn_w_down]
    big = {}
    for nm, p, w, m, v in zip(names, parts, full, moms, vars_):
        C = w.shape[-1]
        res = _adamw(f"adamw_{nm}", p.reshape(N_DEV, -1, C), w.reshape(-1, C), m.reshape(-1, C),
                     v.reshape(-1, C))
        big[nm] = [r.reshape(w.shape) for r in res]

    zrow = jnp.zeros((D,), F32)
    g_small = _pack_small(jnp.concatenate([dg_mix0, dg_mix1]), jnp.concatenate([dg_ffn0, dg_ffn1]),
                          dg_kv, dg_fin, dg_kva, dg_qa, db_f, zrow.at[0].set(loss_vec[0, 0]))
    tot_small = _allreduce_small(g_small)
    w_small = _pack_small(norm_mix_g, norm_ffn_g, kv_norm_g, final_norm_g, mla_kv_a_norm_g,
                          mla_q_a_norm_g, fox_b_f, zrow)
    m_small = _pack_small(m_norm_mix_g, m_norm_ffn_g, m_kv_norm_g, m_final_norm_g, m_mla_kv_a_norm_g,
                          m_mla_q_a_norm_g, m_fox_b_f, zrow)
    v_small = _pack_small(v_norm_mix_g, v_norm_ffn_g, v_kv_norm_g, v_final_norm_g, v_mla_kv_a_norm_g,
                          v_mla_q_a_norm_g, v_fox_b_f, zrow)
    small = _adamw("adamw_small", tot_small[None], w_small, m_small, v_small)
    loss = tot_small[7, 0]
    small = [_unpack_small(s) for s in small]

    def ordered(i):
        mix, ffn, bf, kv, kva, qa, fin = small[i]
        return [mix, ffn, big["fox_w_in"][i], bf, big["fox_w_out"][i], kv, big["mla_w_kv_a"][i], kva,
                big["mla_w_kv_b"][i], big["mla_w_q_a"][i], qa, big["mla_w_q_b"][i],
                big["mla_w_out"][i], big["ffn_w_up"][i], big["ffn_w_down"][i], fin]

    return (loss, grad_x[None], *ordered(0), *ordered(1), *ordered(2), *ordered(3))
```

```python
import functools
import math

import jax
import jax.numpy as jnp
from jax import lax
from jax.experimental import pallas as pl
from jax.experimental.pallas import tpu as pltpu

F32 = jnp.float32
BF16 = jnp.bfloat16
MESH = pl.DeviceIdType.MESH

N_DEV = 8
D_MODEL = 1024
FOX_HEADS = 16
FOX_HEAD_DIM = 64
FOX_AUG = 128
MLA_AUG = 256
MLA_HEADS = 8
QK_NOPE = 128
QK_ROPE = 64
V_HEAD = 128
Q_LORA = 384
KV_LORA = 256
KV_A_PAD = 384
D_FF = 4096
ROPE_BASE = 10000.0
EPS = 1e-6
NEG = -1e30

ADAM_LR = 0.001
ADAM_B1 = 0.9
ADAM_B2 = 0.999
ADAM_EPS = 1e-08
ADAM_WD = 0.01
ADAM_STEP = 10

VMEM_LIMIT_BYTES = 56 * 1024 * 1024

NN = (((1,), (0,)), ((), ()))
NT = (((1,), (1,)), ((), ()))
TN = (((0,), (0,)), ((), ()))
_FORMS = {"nn": NN, "nt": NT}


def _cparams(sem=None):
    return pltpu.CompilerParams(dimension_semantics=sem, vmem_limit_bytes=VMEM_LIMIT_BYTES)


def _pick(n, cands):
    for c in cands:
        if c <= n and n % c == 0:
            return c
    return n


def _dot(a, b, dims):
    return lax.dot_general(a, b, dims, preferred_element_type=F32)


def _mm(name, a, b, form, out_dtypes, epi=None, extras=(), tm=1024, tn=None):
    M, K = a.shape
    N = b.shape[1] if form == "nn" else b.shape[0]
    tm = _pick(M, (tm, 512, 256, 128))
    tn = _pick(N, (tn or (1024 if K <= 1024 else 512), 512, 384, 256, 128))
    n_ex = len(extras)
    n_out = len(out_dtypes)
    cast_once = a.dtype != BF16

    def body(*refs):
        a_ref, b_ref = refs[0], refs[1]
        ex = refs[2:2 + n_ex]
        outs = refs[2 + n_ex:2 + n_ex + n_out]
        if cast_once:
            a_sc = refs[2 + n_ex + n_out]

            @pl.when(pl.program_id(1) == 0)
            def _():
                a_sc[...] = a_ref[...].astype(BF16)

            av = a_sc[...]
        else:
            av = a_ref[...]
        acc = _dot(av, b_ref[...].astype(BF16), _FORMS[form])
        res = epi(acc, *[e[...] for e in ex]) if epi is not None else (acc,)
        for o_ref, r in zip(outs, res):
            o_ref[...] = r.astype(o_ref.dtype)

    if form == "nn":
        b_spec = pl.BlockSpec((K, tn), lambda i, j: (0, j))
    else:
        b_spec = pl.BlockSpec((tn, K), lambda i, j: (j, 0))
    tile = pl.BlockSpec((tm, tn), lambda i, j: (i, j))
    out = pl.pallas_call(
        body, name=name, grid=(M // tm, N // tn),
        in_specs=[pl.BlockSpec((tm, K), lambda i, j: (i, 0)), b_spec] + [tile] * n_ex,
        out_specs=[tile] * n_out,
        out_shape=[jax.ShapeDtypeStruct((M, N), dt) for dt in out_dtypes],
        scratch_shapes=[pltpu.VMEM((tm, K), BF16)] if cast_once else [],
        compiler_params=_cparams(("parallel", "arbitrary")),
    )(a, b, *extras)
    return out if n_out > 1 else out[0]


def _mm_tn(name, a, b):
    T, Ka = a.shape
    N = b.shape[1]
    tk = _pick(Ka, (1024, 512, 384, 256, 128))
    tn = _pick(N, (1024, 768, 512, 384, 256, 128))
    tt = _pick(T, (1024, 512, 256, 128))

    def body(a_ref, b_ref, o_ref):
        @pl.when(pl.program_id(2) == 0)
        def _():
            o_ref[...] = jnp.zeros_like(o_ref)

        o_ref[...] += _dot(a_ref[...].astype(BF16), b_ref[...].astype(BF16), TN)

    return pl.pallas_call(
        body, name=name, grid=(Ka // tk, N // tn, T // tt),
        in_specs=[pl.BlockSpec((tt, tk), lambda i, j, t: (t, i)),
                  pl.BlockSpec((tt, tn), lambda i, j, t: (t, j))],
        out_specs=pl.BlockSpec((tk, tn), lambda i, j, t: (i, j)),
        out_shape=jax.ShapeDtypeStruct((Ka, N), F32),
        compiler_params=_cparams(("parallel", "parallel", "arbitrary")),
    )(a, b)


def _mm_heads(name, a, w, out_dtype):
    T, K = a.shape
    H, _, N = w.shape
    tm = _pick(T, (1024, 512, 256, 128))

    def body(a_ref, w_ref, o_ref):
        av = a_ref[...].astype(BF16)
        for s in range(H):
            o_ref[s] = _dot(av, w_ref[s].astype(BF16), NN).astype(o_ref.dtype)

    return pl.pallas_call(
        body, name=name, grid=(T // tm,),
        in_specs=[pl.BlockSpec((tm, K), lambda i: (i, 0)), pl.BlockSpec((H, K, N), lambda i: (0, 0, 0))],
        out_specs=pl.BlockSpec((H, tm, N), lambda i: (0, i, 0)),
        out_shape=jax.ShapeDtypeStruct((H, T, N), out_dtype),
        compiler_params=_cparams(("parallel",)),
    )(a, w)


def _mm_head_slabs(name, a, w, tails, head_dim, out_dtype, heads_per_step=8):
    T, K = a.shape
    S, _, tail = tails.shape
    hb = heads_per_step
    tm = _pick(T, (1024, 512, 256, 128))

    def body(a_ref, w_ref, t_ref, o_ref):
        acc = _dot(a_ref[...].astype(BF16), w_ref[...].astype(BF16), NN)
        for s in range(hb):
            slab = jnp.concatenate([acc[:, s * head_dim:(s + 1) * head_dim],
                                    jnp.broadcast_to(t_ref[s], (tm, tail))], axis=-1)
            o_ref[s] = slab.astype(o_ref.dtype)

    return pl.pallas_call(
        body, name=name, grid=(T // tm, S // hb),
        in_specs=[pl.BlockSpec((tm, K), lambda i, j: (i, 0)),
                  pl.BlockSpec((K, hb * head_dim), lambda i, j: (0, j)),
                  pl.BlockSpec((hb, 1, tail), lambda i, j: (j, 0, 0))],
        out_specs=pl.BlockSpec((hb, tm, head_dim + tail), lambda i, j: (j, i, 0)),
        out_shape=jax.ShapeDtypeStruct((S, T, head_dim + tail), out_dtype),
        compiler_params=_cparams(("parallel", "arbitrary")),
    )(a, w, tails)


def _mm_heads_dw(name, a, g):
    T, K = a.shape
    H, _, N = g.shape
    tt = _pick(T, (1024, 512, 256, 128))

    def body(a_ref, g_ref, o_ref):
        @pl.when(pl.program_id(0) == 0)
        def _():
            o_ref[...] = jnp.zeros_like(o_ref)

        av = a_ref[...].astype(BF16)
        for s in range(H):
            o_ref[s] += _dot(av, g_ref[s].astype(BF16), TN)

    return pl.pallas_call(
        body, name=name, grid=(T // tt,),
        in_specs=[pl.BlockSpec((tt, K), lambda t: (t, 0)), pl.BlockSpec((H, tt, N), lambda t: (0, t, 0))],
        out_specs=pl.BlockSpec((H, K, N), lambda t: (0, 0, 0)),
        out_shape=jax.ShapeDtypeStruct((H, K, N), F32),
        compiler_params=_cparams(("arbitrary",)),
    )(a, g)


def _mm_heads_dx(name, g, w):
    H, T, N = g.shape
    K = w.shape[1]
    tm = _pick(T, (512, 256, 128))

    def body(g_ref, w_ref, o_ref):
        acc = _dot(g_ref[0].astype(BF16), w_ref[0].astype(BF16), NT)
        for s in range(1, H):
            acc = acc + _dot(g_ref[s].astype(BF16), w_ref[s].astype(BF16), NT)
        o_ref[...] = acc

    return pl.pallas_call(
        body, name=name, grid=(T // tm,),
        in_specs=[pl.BlockSpec((H, tm, N), lambda i: (0, i, 0)), pl.BlockSpec((H, K, N), lambda i: (0, 0, 0))],
        out_specs=pl.BlockSpec((tm, K), lambda i: (i, 0)),
        out_shape=jax.ShapeDtypeStruct((T, K), F32),
        compiler_params=_cparams(("parallel",)),
    )(g, w)


def _rms(name, x, g, out_dtype):
    T, D = x.shape
    tm = _pick(T, (1024, 512, 256, 128))

    def body(x_ref, g_ref, o_ref):
        xf = x_ref[...]
        r = lax.rsqrt(jnp.mean(xf * xf, axis=-1, keepdims=True) + EPS)
        o_ref[...] = (xf * r * g_ref[...]).astype(o_ref.dtype)

    return pl.pallas_call(
        body, name=name, grid=(T // tm,),
        in_specs=[pl.BlockSpec((tm, D), lambda i: (i, 0)), pl.BlockSpec((1, D), lambda i: (0, 0))],
        out_specs=pl.BlockSpec((tm, D), lambda i: (i, 0)),
        out_shape=jax.ShapeDtypeStruct((T, D), out_dtype),
        compiler_params=_cparams(("parallel",)),
    )(x, g)


def _rms_bwd(name, x, g, dh, dres=None):
    T, D = x.shape
    tm = _pick(T, (512, 256, 128))
    has_res = dres is not None

    def body(*refs):
        if has_res:
            x_ref, g_ref, dh_ref, dres_ref, dx_ref, dg_ref = refs
        else:
            x_ref, g_ref, dh_ref, dx_ref, dg_ref = refs

        @pl.when(pl.program_id(0) == 0)
        def _():
            dg_ref[...] = jnp.zeros_like(dg_ref)

        xf = x_ref[...]
        r = lax.rsqrt(jnp.mean(xf * xf, axis=-1, keepdims=True) + EPS)
        xhat = xf * r
        dy = dh_ref[...].astype(F32)
        dxh = dy * g_ref[...]
        dx = r * (dxh - xhat * jnp.mean(dxh * xhat, axis=-1, keepdims=True))
        if has_res:
            dx = dx + dres_ref[...]
        dx_ref[...] = dx
        dg_ref[...] += jnp.sum(dy * xhat, axis=0, keepdims=True)

    row = pl.BlockSpec((tm, D), lambda i: (i, 0))
    vec = pl.BlockSpec((1, D), lambda i: (0, 0))
    ins = [x, g, dh] + ([dres] if has_res else [])
    return pl.pallas_call(
        body, name=name, grid=(T // tm,),
        in_specs=[row, vec, row] + ([row] if has_res else []),
        out_specs=[row, vec],
        out_shape=[jax.ShapeDtypeStruct((T, D), F32), jax.ShapeDtypeStruct((1, D), F32)],
        compiler_params=_cparams(("arbitrary",)),
    )(*ins)


def _mm_rms_bwd(name, a, b, x, g, dres, add=None, exchange=None):
    T, K = a.shape
    D = b.shape[0]
    tm = _pick(T, (512, 256, 128))
    has_add = add is not None

    def body(*refs):
        a_ref, b_ref, x_ref, g_ref, dres_ref = refs[:5]
        dx_ref, dg_ref = refs[-2:]

        @pl.when(pl.program_id(0) == 0)
        def _():
            dg_ref[...] = jnp.zeros_like(dg_ref)

        dy = _dot(a_ref[...].astype(BF16), b_ref[...].astype(BF16), NT)
        if has_add:
            dy = dy + refs[5][...]
        xf = x_ref[...]
        r = lax.rsqrt(jnp.mean(xf * xf, axis=-1, keepdims=True) + EPS)
        xhat = xf * r
        dxh = dy * g_ref[...]
        dx_ref[...] = r * (dxh - xhat * jnp.mean(dxh * xhat, axis=-1, keepdims=True)) + dres_ref[...]
        dg_ref[...] += jnp.sum(dy * xhat, axis=0, keepdims=True)

    row = pl.BlockSpec((tm, D), lambda i: (i, 0))
    vec = pl.BlockSpec((1, D), lambda i: (0, 0))
    (dx, dg), exchanged = _call_carrying(
        body, name, (T // tm,),
        in_specs=[pl.BlockSpec((tm, K), lambda i: (i, 0)), pl.BlockSpec((D, K), lambda i: (0, 0)), row, vec, row]
        + ([row] if has_add else []),
        out_specs=[row, vec],
        out_shape=[jax.ShapeDtypeStruct((T, D), F32), jax.ShapeDtypeStruct((1, D), F32)],
        scratch_shapes=[], operands=(a, b, x, g, dres) + ((add,) if has_add else ()), exchange=exchange,
        sequential=True)
    return (dx, dg) if exchange is None else (dx, dg, exchanged)


def _loss_head(name, x, g, tgt):
    T, D = x.shape
    tm = _pick(T, (512, 256, 128))

    def body(x_ref, g_ref, t_ref, dx_ref, dg_ref, loss_ref):
        @pl.when(pl.program_id(0) == 0)
        def _():
            dg_ref[...] = jnp.zeros_like(dg_ref)
            loss_ref[...] = jnp.zeros_like(loss_ref)

        xf = x_ref[...]
        r = lax.rsqrt(jnp.mean(xf * xf, axis=-1, keepdims=True) + EPS)
        xhat = xf * r
        gv = g_ref[...]
        err = xhat * gv - t_ref[...]
        row_loss = jnp.mean(err * err, axis=-1, keepdims=True)
        loss_ref[...] += 0.5 * jnp.sum(row_loss, axis=0, keepdims=True)
        dy = err * (1.0 / D)
        dxh = dy * gv
        dx_ref[...] = r * (dxh - xhat * jnp.mean(dxh * xhat, axis=-1, keepdims=True))
        dg_ref[...] += jnp.sum(dy * xhat, axis=0, keepdims=True)

    row = pl.BlockSpec((tm, D), lambda i: (i, 0))
    vec = pl.BlockSpec((1, D), lambda i: (0, 0))
    return pl.pallas_call(
        body, name=name, grid=(T // tm,),
        in_specs=[row, vec, row],
        out_specs=[row, vec, pl.BlockSpec((1, 128), lambda i: (0, 0))],
        out_shape=[jax.ShapeDtypeStruct((T, D), F32), jax.ShapeDtypeStruct((1, D), F32),
                   jax.ShapeDtypeStruct((1, 128), F32)],
        compiler_params=_cparams(("arbitrary",)),
    )(x, g, tgt)


def _swap_halves(t):
    half = t.shape[-1] // 2
    return jnp.concatenate([t[:, half:], t[:, :half]], axis=-1)


def _rope(name, t, cos2, sgn_sin, out_dtype):
    H, T, R = t.shape
    tm = _pick(T, (1024, 512, 256, 128))

    def body(t_ref, c_ref, s_ref, o_ref):
        tf = t_ref[...].astype(F32)
        o_ref[...] = (tf * c_ref[...] + _swap_halves(tf) * s_ref[...]).astype(o_ref.dtype)

    slab = pl.BlockSpec((None, tm, R), lambda h, i: (h, i, 0))
    tab = pl.BlockSpec((tm, R), lambda h, i: (i, 0))
    return pl.pallas_call(
        body, name=name, grid=(H, T // tm),
        in_specs=[slab, tab, tab], out_specs=slab,
        out_shape=jax.ShapeDtypeStruct((H, T, R), out_dtype),
        compiler_params=_cparams(("parallel", "parallel")),
    )(t, cos2, sgn_sin)


def _mla_q_proj(name, a, w, cos2, sgn_sin, scale):
    T, K = a.shape
    H, _, W = w.shape
    R = cos2.shape[1]
    tm = _pick(T, (1024, 512, 256, 128))

    def body(a_ref, w_ref, c_ref, s_ref, o_ref):
        av = a_ref[...].astype(BF16)
        for h in range(H):
            qf = _dot(av, w_ref[h].astype(BF16), NN)
            r = qf[:, W - R:]
            roped = r * c_ref[...] + _swap_halves(r) * s_ref[...]
            o_ref[h] = (jnp.concatenate([qf[:, :W - R], roped], axis=-1) * scale).astype(o_ref.dtype)

    tab = pl.BlockSpec((tm, R), lambda i: (i, 0))
    return pl.pallas_call(
        body, name=name, grid=(T // tm,),
        in_specs=[pl.BlockSpec((tm, K), lambda i: (i, 0)), pl.BlockSpec((H, K, W), lambda i: (0, 0, 0)), tab, tab],
        out_specs=pl.BlockSpec((H, tm, W), lambda i: (0, i, 0)),
        out_shape=jax.ShapeDtypeStruct((H, T, W), BF16),
        compiler_params=_cparams(("parallel",)),
    )(a, w, cos2, sgn_sin)


def _rope_bwd(name, dy, cos2, sgn_sin, sum_heads):
    H, T, R = dy.shape
    tm = _pick(T, (1024, 512, 256, 128))

    def body(d_ref, c_ref, s_ref, o_ref):
        d = d_ref[...]
        if sum_heads:
            tot = d[0]
            for h in range(1, H):
                tot = tot + d[h]
            d = tot
        o_ref[...] = d * c_ref[...] + _swap_halves(d * s_ref[...])

    if sum_heads:
        grid = (T // tm,)
        in_slab = pl.BlockSpec((H, tm, R), lambda i: (0, i, 0))
        out_slab = pl.BlockSpec((tm, R), lambda i: (i, 0))
        tab = pl.BlockSpec((tm, R), lambda i: (i, 0))
        out_shape = jax.ShapeDtypeStruct((T, R), F32)
        sem = ("parallel",)
    else:
        grid = (H, T // tm)
        in_slab = pl.BlockSpec((None, tm, R), lambda h, i: (h, i, 0))
        out_slab = in_slab
        tab = pl.BlockSpec((tm, R), lambda h, i: (i, 0))
        out_shape = jax.ShapeDtypeStruct((H, T, R), F32)
        sem = ("parallel", "parallel")
    return pl.pallas_call(
        body, name=name, grid=grid, in_specs=[in_slab, tab, tab], out_specs=out_slab,
        out_shape=out_shape, compiler_params=_cparams(sem),
    )(dy, cos2, sgn_sin)


def _log_sigmoid(z):
    return jnp.minimum(z, 0.0) - jnp.log(1.0 + jnp.exp(-jnp.abs(z)))


def _gate_cumsum(name, fl, b, tb):
    H, T = fl.shape

    def body(f_ref, b_ref, c_ref, carry):
        @pl.when(pl.program_id(0) == 0)
        def _():
            carry[...] = jnp.zeros_like(carry)

        ls = _log_sigmoid(f_ref[...] + b_ref[...])
        src = lax.broadcasted_iota(jnp.int32, (tb, tb), 0)
        dst = lax.broadcasted_iota(jnp.int32, (tb, tb), 1)
        tri = (src <= dst).astype(F32)
        c = lax.dot_general(ls, tri, NN, precision=lax.Precision.HIGHEST,
                            preferred_element_type=F32) + carry[...]
        c_ref[...] = c
        carry[...] = carry[...] + jnp.sum(ls, axis=-1, keepdims=True)

    return pl.pallas_call(
        body, name=name, grid=(T // tb,),
        in_specs=[pl.BlockSpec((H, tb), lambda i: (0, i)), pl.BlockSpec((H, 1), lambda i: (0, 0))],
        out_specs=pl.BlockSpec((H, tb), lambda i: (0, i)),
        out_shape=jax.ShapeDtypeStruct((H, T), F32),
        scratch_shapes=[pltpu.VMEM((H, 1), F32)],
        compiler_params=_cparams(("arbitrary",)),
    )(fl, b)


def _gate_cumsum_bwd(name, d_query, d_key, fl, b, tb):
    H, T = fl.shape
    nb = T // tb

    def body(dq_ref, dk_ref, f_ref, b_ref, dfl_ref, db_ref, carry):
        @pl.when(pl.program_id(0) == 0)
        def _():
            carry[...] = jnp.zeros_like(carry)
            db_ref[...] = jnp.zeros_like(db_ref)

        d = dq_ref[...] - dk_ref[...]
        src = lax.broadcasted_iota(jnp.int32, (tb, tb), 0)
        dst = lax.broadcasted_iota(jnp.int32, (tb, tb), 1)
        tri = (src >= dst).astype(F32)
        dls = lax.dot_general(d, tri, NN, precision=lax.Precision.HIGHEST,
                              preferred_element_type=F32) + carry[...]
        z = f_ref[...] + b_ref[...]
        dfl = dls * (1.0 / (1.0 + jnp.exp(z)))
        dfl_ref[...] = dfl
        db_ref[...] += jnp.sum(dfl, axis=-1, keepdims=True)
        carry[...] = carry[...] + jnp.sum(d, axis=-1, keepdims=True)

    blk = pl.BlockSpec((H, tb), lambda i: (0, nb - 1 - i))
    vec = pl.BlockSpec((H, 1), lambda i: (0, 0))
    return pl.pallas_call(
        body, name=name, grid=(nb,),
        in_specs=[blk, blk, blk, vec], out_specs=[blk, vec],
        out_shape=[jax.ShapeDtypeStruct((H, T), F32), jax.ShapeDtypeStruct((H, 1), F32)],
        scratch_shapes=[pltpu.VMEM((H, 1), F32)],
        compiler_params=_cparams(("arbitrary",)),
    )(d_query, d_key, fl, b)


def _chunk_rows(j, tq):
    return pl.ds(pl.multiple_of(j * tq, tq), tq)


def _column_as_row(col):
    return jnp.broadcast_to(col, (col.shape[0], 128)).T[:1, :]


def _flash_fwd(name, q, k, v_aug, dv, tq, exchange=None, q_head0=0, v_head0=0):
    H, T, dqk = k.shape
    dva = v_aug.shape[2]
    tk = tq
    tq = 2 * tk if T % (2 * tk) == 0 else tk
    r = tq // tk
    nq = T // tq

    def body(q_ref, k_ref, v_ref, o_ref, lse_ref, m_sc, acc_sc):
        qi = pl.program_id(1)
        m_sc[...] = jnp.full_like(m_sc, NEG)
        acc_sc[...] = jnp.zeros_like(acc_sc)

        def chunk(j, diag):
            rows = _chunk_rows(j, tk)
            live = slice(0 if diag is None else diag * tk, tq)
            n_live = tq - live.start
            s = _dot(q_ref[live, :], k_ref[rows, :], NT)
            if diag is not None:
                row = lax.broadcasted_iota(jnp.int32, (n_live, tk), 0)
                col = lax.broadcasted_iota(jnp.int32, (n_live, tk), 1)
                s = jnp.where(col <= row, s, NEG)
            m_prev = m_sc[live, :]
            m_new = jnp.maximum(m_prev, jnp.max(s, axis=1, keepdims=True))
            p = jnp.exp(s - jnp.tile(m_new, (1, tk // 128)))
            alpha = jnp.tile(jnp.exp(m_prev - m_new), (1, dva // 128))
            acc_sc[live, :] = alpha * acc_sc[live, :] + _dot(p.astype(BF16), v_ref[rows, :], NN)
            m_sc[live, :] = m_new

        def off_diagonal(j, carry):
            chunk(j, None)
            return carry

        lax.fori_loop(0, qi * r, off_diagonal, 0)
        for d in range(r):
            chunk(qi * r + d, d)
        acc = acc_sc[...]
        l = acc[:, dv:dv + 1]
        o_ref[...] = acc[:, :dv] / l
        lse_ref[...] = _column_as_row(m_sc[:, :1] + jnp.log(l))

    (o, lse_rows), exchanged = _call_carrying(
        body, name, (H, nq),
        in_specs=[pl.BlockSpec((None, tq, dqk), lambda h, i: (h + q_head0, i, 0)),
                  pl.BlockSpec((None, T, dqk), lambda h, i: (h, 0, 0)),
                  pl.BlockSpec((None, T, dva), lambda h, i: (h + v_head0, 0, 0))],
        out_specs=[pl.BlockSpec((None, tq, dv), lambda h, i: (h, i, 0)),
                   pl.BlockSpec((None, 1, tq), lambda h, i: (h, 0, i))],
        out_shape=[jax.ShapeDtypeStruct((H, T, dv), F32), jax.ShapeDtypeStruct((H, 1, T), F32)],
        scratch_shapes=[pltpu.VMEM((tq, 128), F32), pltpu.VMEM((tq, dva), F32)],
        operands=(q, k, v_aug), exchange=exchange)
    return (o, lse_rows.reshape(H, T)), exchanged


def _row_dot(name, a, b):
    H, T, d = a.shape
    tm = _pick(T, (1024, 512, 256, 128))

    def body(a_ref, b_ref, o_ref):
        col = jnp.sum(a_ref[...].astype(F32) * b_ref[...].astype(F32), axis=-1, keepdims=True)
        o_ref[...] = _column_as_row(col)

    slab = pl.BlockSpec((None, tm, d), lambda h, i: (h, i, 0))
    return pl.pallas_call(
        body, name=name, grid=(H, T // tm), in_specs=[slab, slab],
        out_specs=pl.BlockSpec((None, 1, tm), lambda h, i: (h, 0, i)),
        out_shape=jax.ShapeDtypeStruct((H, 1, T), F32),
        compiler_params=_cparams(("parallel", "parallel")),
    )(a, b).reshape(H, T)


def _flash_bwd(name, q, k, v, do, scale, tq, exchange=None, v_head0=0, token_major_out=False):
    H, T, dqk = q.shape
    dva = v.shape[2]
    tkb = 2 * tq if T % (2 * tq) == 0 else tq
    r = tkb // tq
    nk = T // tkb

    def body(q_ref, k_ref, v_ref, do_ref, dq_ref, dk_ref, dv_ref, dk_sc, dv_sc):
        kj = pl.program_id(1)
        dk_sc[...] = jnp.zeros_like(dk_sc)
        dv_sc[...] = jnp.zeros_like(dv_sc)

        @pl.when(kj == 0)
        def _():
            dq_ref[...] = jnp.zeros_like(dq_ref)

        def chunk(i, diag):
            rows = _chunk_rows(i, tq)
            qb = q_ref[rows, :]
            dob = do_ref[rows, :]
            live = slice(0, tkb if diag is None else (diag + 1) * tq)
            kb = k_ref[live, :]
            st = _dot(kb, qb, NT)
            if diag is not None:
                key = lax.broadcasted_iota(jnp.int32, st.shape, 0)
                qry = lax.broadcasted_iota(jnp.int32, st.shape, 1) + diag * tq
                st = jnp.where(key <= qry, st, NEG)
            pt = jnp.exp(st)
            dv_sc[live, :] += _dot(pt.astype(BF16), dob, NN)
            dst = (pt * _dot(v_ref[live, :], dob, NT)).astype(BF16)
            dk_sc[live, :] += _dot(dst, qb, NN)
            dq_ref[rows, :] += _dot(dst, kb, TN)

        def off_diagonal(i, carry):
            chunk(i, None)
            return carry

        for d in range(r):
            chunk(kj * r + d, d)
        lax.fori_loop(kj * r + r, T // tq, off_diagonal, 0)
        dk_ref[...] = dk_sc[...]
        dv_ref[...] = dv_sc[...]

        @pl.when(kj == nk - 1)
        def _():
            dq_ref[...] = dq_ref[...] * scale

    whole_q = pl.BlockSpec((None, T, dqk), lambda h, j: (h, 0, 0))
    k_spec = pl.BlockSpec((None, tkb, dqk), lambda h, j: (h, j, 0))
    v_spec = pl.BlockSpec((None, tkb, dva), lambda h, j: (h, j, 0))
    v_in_spec = pl.BlockSpec((None, tkb, dva), lambda h, j: (h + v_head0, j, 0))
    if token_major_out:
        out_specs = [pl.BlockSpec((T, dqk), lambda h, j: (0, h)), pl.BlockSpec((tkb, dqk), lambda h, j: (j, h)),
                     pl.BlockSpec((tkb, dva), lambda h, j: (j, h))]
        out_shape = [jax.ShapeDtypeStruct((T, H * dqk), F32), jax.ShapeDtypeStruct((T, H * dqk), F32),
                     jax.ShapeDtypeStruct((T, H * dva), F32)]
    else:
        out_specs = [whole_q, k_spec, v_spec]
        out_shape = [jax.ShapeDtypeStruct((H, T, dqk), F32), jax.ShapeDtypeStruct((H, T, dqk), F32),
                     jax.ShapeDtypeStruct((H, T, dva), F32)]
    return _call_carrying(
        body, name, (H, nk),
        in_specs=[whole_q, k_spec, v_in_spec, pl.BlockSpec((None, T, dva), lambda h, j: (h, 0, 0))],
        out_specs=out_specs, out_shape=out_shape,
        scratch_shapes=[pltpu.VMEM((tkb, dqk), F32), pltpu.VMEM((tkb, dva), F32)],
        operands=(q, k, v, do), exchange=exchange)


def _pack_head_grads(name, parts, W, head_dim, picks):
    T, HW = parts[0].shape
    H = HW // W
    n = len(parts)
    tm = _pick(T, (512, 256, 128))
    sels = [(jnp.arange(HW)[:, None] == (jnp.arange(128)[None, :] * W + col)).astype(F32) for _, col in picks]

    def body(*refs):
        xs = refs[:n]
        sel_refs = refs[n:n + len(picks)]
        packed_ref = refs[n + len(picks)]
        row_refs = refs[n + len(picks) + 1:]
        for a in range(n):
            x = xs[a][...]
            heads = [x[:, h * W:h * W + head_dim] for h in range(H)]
            packed_ref[:, a * H * head_dim:(a + 1) * H * head_dim] = (
                jnp.concatenate(heads, axis=-1).astype(packed_ref.dtype))
        for (a, _), s_ref, r_ref in zip(picks, sel_refs, row_refs):
            cols = lax.dot_general(xs[a][...], s_ref[...], NN, precision=lax.Precision.HIGHEST,
                                   preferred_element_type=F32)
            r_ref[...] = cols.T[:H, :]

    row = pl.BlockSpec((tm, HW), lambda i: (i, 0))
    out = pl.pallas_call(
        body, name=name, grid=(T // tm,),
        in_specs=[row] * n + [pl.BlockSpec((HW, 128), lambda i: (0, 0))] * len(picks),
        out_specs=[pl.BlockSpec((tm, n * H * head_dim), lambda i: (i, 0))]
        + [pl.BlockSpec((H, tm), lambda i: (0, i))] * len(picks),
        out_shape=[jax.ShapeDtypeStruct((T, n * H * head_dim), BF16)]
        + [jax.ShapeDtypeStruct((H, T), F32)] * len(picks),
        compiler_params=_cparams(("parallel",)),
    )(*parts, *sels)
    return out[0], out[1:]


def _adamw_math(w, g, m, v):
    m = ADAM_B1 * m + (1.0 - ADAM_B1) * g
    v = ADAM_B2 * v + (1.0 - ADAM_B2) * (g * g)
    m_hat = m / (1.0 - ADAM_B1 ** ADAM_STEP)
    v_hat = v / (1.0 - ADAM_B2 ** ADAM_STEP)
    delta = -ADAM_LR * (m_hat / (jnp.sqrt(v_hat) + ADAM_EPS) + ADAM_WD * w)
    return delta, m, v


def _adamw(name, parts, w, m, v):
    P, R, C = parts.shape
    tr = _pick(R, (256, 128, 64, 32, 16, 8))

    def body(p_ref, w_ref, m_ref, v_ref, g_out, d_out, m_out, v_out):
        g = p_ref[0].astype(F32)
        for i in range(1, P):
            g = g + p_ref[i].astype(F32)
        delta, m_new, v_new = _adamw_math(w_ref[...], g, m_ref[...], v_ref[...])
        g_out[...] = g
        d_out[...] = delta
        m_out[...] = m_new
        v_out[...] = v_new

    blk = pl.BlockSpec((tr, C), lambda i: (i, 0))
    sds = jax.ShapeDtypeStruct((R, C), F32)
    return pl.pallas_call(
        body, name=name, grid=(R // tr,),
        in_specs=[pl.BlockSpec((P, tr, C), lambda i: (0, i, 0)), blk, blk, blk],
        out_specs=[blk] * 4, out_shape=[sds] * 4,
        compiler_params=_cparams(("parallel",)),
    )(parts, w, m, v)


def _my_position():
    return lax.axis_index("x"), lax.axis_index("y"), lax.axis_index("c")


def _slot(p):
    return 4 * p[0] + 2 * p[1] + p[2]


def _flip(p, k):
    return tuple((1 - p[i]) if (k >> (2 - i)) & 1 else p[i] for i in range(3))


def _allgather_weights(shards):
    n = len(shards)

    def body(*refs):
        ins = refs[:n]
        outs = refs[n:2 * n]
        send_sems, recv_sems, local_sems = refs[2 * n:]
        x, y, c = _my_position()
        me, sibling = (x, y, c), (x, y, 1 - c)
        chips = [(1 - x, y), (x, 1 - y), (1 - x, 1 - y)]

        def copy(a, k, block, to, src=None):
            dst = outs[a].at[_slot(block)]
            return pltpu.make_async_remote_copy(
                src_ref=dst if src is None else src, dst_ref=dst,
                send_sem=send_sems.at[7 * a + k], recv_sem=recv_sems.at[7 * a + k],
                device_id=to, device_id_type=MESH)

        started = []
        for a in range(n):
            mine = pltpu.make_async_copy(ins[a], outs[a].at[_slot(me)], local_sems.at[a])
            mine.start()
            started.append(mine)
        first = []
        for a in range(n):
            first.append(copy(a, 0, me, sibling, src=ins[a]))
            first += [copy(a, 1 + j, me, (*chip, c), src=ins[a]) for j, chip in enumerate(chips)]
        for cp in first:
            cp.start()
        passed = []
        for j, chip in enumerate(chips):
            for a in range(n):
                copy(a, 1 + j, (*chip, c), me).wait_recv()
                fwd = copy(a, 4 + j, (*chip, c), sibling)
                fwd.start()
                passed.append(fwd)
        for a in range(n):
            copy(a, 0, sibling, me).wait_recv()
            for j, chip in enumerate(chips):
                copy(a, 4 + j, (*chip, 1 - c), me).wait_recv()
        for cp in first + passed:
            cp.wait_send()
        for mine in started:
            mine.wait()

    hbm = pl.BlockSpec(memory_space=pl.ANY)
    return pl.pallas_call(
        body, name="allgather_weights",
        in_specs=[hbm] * n, out_specs=[hbm] * n,
        out_shape=[jax.ShapeDtypeStruct((N_DEV,) + s.shape, s.dtype) for s in shards],
        scratch_shapes=[pltpu.SemaphoreType.DMA((7 * n,)), pltpu.SemaphoreType.DMA((7 * n,)),
                        pltpu.SemaphoreType.DMA((n,))],
        compiler_params=pltpu.CompilerParams(has_side_effects=True),
    )(*shards)


def _exchange_copies(kind, x_in, x_out, send_sems, recv_sems, local_sems, receives=True):
    me = _my_position()
    mine = _slot(me)
    local, sends, recvs = [], [], []
    for a in range(len(x_in)):
        src = x_in[a] if kind == "gather" else x_in[a].at[mine]
        local.append(pltpu.make_async_copy(src, x_out[a].at[mine], local_sems.at[a]))
    for k in range(1, N_DEV):
        peer = _flip(me, k)
        theirs = _slot(peer)
        for a in range(len(x_in)):
            src = x_in[a] if kind == "gather" else x_in[a].at[theirs]
            ends = [(x_out[a].at[mine], sends)] + ([(x_out[a].at[theirs], recvs)] if receives else [])
            for dst, group in ends:
                group.append(pltpu.make_async_remote_copy(
                    src_ref=src, dst_ref=dst, send_sem=send_sems.at[7 * a + k - 1],
                    recv_sem=recv_sems.at[7 * a + k - 1], device_id=peer, device_id_type=MESH))
    return local, sends, recvs


def _exchange_out_shapes(kind, arrays):
    return [jax.ShapeDtypeStruct(((N_DEV,) + a.shape) if kind == "gather" else a.shape, a.dtype)
            for a in arrays]


def _exchange_sems(n):
    return [pltpu.SemaphoreType.DMA((7 * n,)), pltpu.SemaphoreType.DMA((7 * n,)),
            pltpu.SemaphoreType.DMA((n,))]


def _call_carrying(body, name, grid, in_specs, out_specs, out_shape, scratch_shapes, operands, exchange,
                   sequential=False):
    if exchange is None:
        out = pl.pallas_call(
            body, name=name, grid=grid, in_specs=in_specs, out_specs=out_specs, out_shape=out_shape,
            scratch_shapes=scratch_shapes,
            compiler_params=_cparams((("arbitrary",) if sequential else ("parallel",))
                                     + ("arbitrary",) * (len(grid) - 1)),
        )(*operands)
        return out, None
    kind, arrays = exchange
    n, n_in, n_out, n_sc = len(arrays), len(in_specs), len(out_specs), len(scratch_shapes)

    def full_body(*refs):
        ins, refs = refs[:n_in], refs[n_in:]
        x_in, refs = refs[:n], refs[n:]
        outs, refs = refs[:n_out], refs[n_out:]
        x_out, refs = refs[:n], refs[n:]
        scratch, sems = refs[:n_sc], refs[n_sc:]
        first = last = None
        for axis, size in enumerate(grid):
            at_start = pl.program_id(axis) == 0
            at_end = pl.program_id(axis) == size - 1
            first = at_start if first is None else jnp.logical_and(first, at_start)
            last = at_end if last is None else jnp.logical_and(last, at_end)

        @pl.when(first)
        def _():
            local, sends, _ = _exchange_copies(kind, x_in, x_out, *sems, receives=False)
            for cp in local + sends:
                cp.start()

        body(*ins, *outs, *scratch)

        @pl.when(last)
        def _():
            local, sends, recvs = _exchange_copies(kind, x_in, x_out, *sems)
            for cp in recvs:
                cp.wait_recv()
            for cp in sends:
                cp.wait_send()
            for cp in local:
                cp.wait()

    hbm = pl.BlockSpec(memory_space=pl.ANY)
    out = pl.pallas_call(
        full_body, name=name, grid=grid,
        in_specs=list(in_specs) + [hbm] * n, out_specs=list(out_specs) + [hbm] * n,
        out_shape=list(out_shape) + _exchange_out_shapes(kind, arrays),
        scratch_shapes=list(scratch_shapes) + _exchange_sems(n),
        compiler_params=pltpu.CompilerParams(dimension_semantics=("arbitrary",) * len(grid),
                                             vmem_limit_bytes=VMEM_LIMIT_BYTES, has_side_effects=True),
    )(*operands, *arrays)
    return out[:n_out], out[n_out:]


def _allreduce_small(v):
    R, C = v.shape

    def body(v_ref, o_ref, buf, send_sems, recv_sems):
        me = _my_position()
        buf[_slot(me)] = v_ref[...]
        sends = []
        for k in range(1, N_DEV):
            peer = _flip(me, k)
            cp = pltpu.make_async_remote_copy(
                src_ref=v_ref, dst_ref=buf.at[_slot(me)],
                send_sem=send_sems.at[k - 1], recv_sem=recv_sems.at[k - 1],
                device_id=peer, device_id_type=MESH)
            cp.start()
            sends.append(cp)
        for k in range(1, N_DEV):
            peer = _flip(me, k)
            pltpu.make_async_remote_copy(
                src_ref=v_ref, dst_ref=buf.at[_slot(peer)],
                send_sem=send_sems.at[k - 1], recv_sem=recv_sems.at[k - 1],
                device_id=peer, device_id_type=MESH).wait_recv()
        for cp in sends:
            cp.wait_send()
        tot = buf[0]
        for s in range(1, N_DEV):
            tot = tot + buf[s]
        o_ref[...] = tot

    vm = pl.BlockSpec(memory_space=pltpu.VMEM)
    return pl.pallas_call(
        body, name="allreduce_small",
        in_specs=[vm], out_specs=vm, out_shape=jax.ShapeDtypeStruct((R, C), F32),
        scratch_shapes=[pltpu.VMEM((N_DEV, R, C), F32), pltpu.SemaphoreType.DMA((7,)),
                        pltpu.SemaphoreType.DMA((7,))],
        compiler_params=pltpu.CompilerParams(has_side_effects=True),
    )(v)


def _to_heads(t, heads):
    T = t.shape[0]
    return t.reshape(T, heads, t.shape[1] // heads).transpose(1, 0, 2)


def _from_heads(t):
    H, T, d = t.shape
    return t.transpose(1, 0, 2).reshape(T, H * d)


def _widen(t, width, ones_at=None, pieces_at=None, pieces=None):
    out = jnp.pad(t, ((0, 0), (0, 0), (0, width - t.shape[-1])))
    lane = lax.broadcasted_iota(jnp.int32, (1, 1, width), 2)
    if ones_at is not None:
        out = jnp.where((lane >= ones_at) & (lane < ones_at + 3), jnp.ones((), BF16), out)
    if pieces_at is not None:
        for i in range(3):
            out = jnp.where(lane == pieces_at + i, pieces[i][:, :, None], out)
    return out


def _split3(t):
    hi = lax.reduce_precision(t, 8, 7)
    r = t - hi
    mid = lax.reduce_precision(r, 8, 7)
    lo = lax.reduce_precision(r - mid, 8, 7)
    return hi.astype(BF16), mid.astype(BF16), lo.astype(BF16)


def _pad_cols(t, n):
    return jnp.pad(t, ((0, 0), (0, n - t.shape[1])))


def _pack_small(mix, ffn, kv, fin, kva, qa, bf, last):
    row6 = jnp.concatenate([kva.reshape(-1), qa.reshape(-1), bf.reshape(-1),
                            jnp.zeros((D_MODEL - KV_LORA - Q_LORA - FOX_HEADS,), F32)])
    return jnp.stack([mix[0], mix[1], ffn[0], ffn[1], kv.reshape(-1), fin.reshape(-1), row6, last])


def _unpack_small(p):
    mix = p[0:2]
    ffn = p[2:4]
    kv = p[4]
    fin = p[5]
    kva = p[6, :KV_LORA]
    qa = p[6, KV_LORA:KV_LORA + Q_LORA].reshape(1, Q_LORA)
    bf = p[6, KV_LORA + Q_LORA:KV_LORA + Q_LORA + FOX_HEADS].reshape(1, FOX_HEADS)
    return mix, ffn, bf, kv, kva, qa, fin


def _mlp_fwd(tag, xin, g, w_up, w_down):
    h = _rms(f"{tag}_norm", xin, g, BF16)

    def act(acc):
        r = jnp.maximum(acc, 0.0)
        return acc, r * r

    u, a = _mm(f"{tag}_up", h, w_up, "nn", (BF16, BF16), epi=act)
    xout = _mm(f"{tag}_down", a, w_down, "nn", (F32,), epi=lambda acc, r: (acc + r,), extras=(xin,))
    return xout, (h, u, a)


def _mlp_bwd(tag, gout, xin, g, w_up, w_down, saved):
    h, u, a = saved
    dw_down = _mm_tn(f"{tag}_dwdown", a, gout)
    du = _mm(f"{tag}_du", gout, w_down, "nt", (BF16,),
             epi=lambda acc, uu: (acc * (2.0 * jnp.maximum(uu.astype(F32), 0.0)),), extras=(u,))
    dw_up = _mm_tn(f"{tag}_dwup", h, du)
    gin, dg = _mm_rms_bwd(f"{tag}_dh_norm_bwd", du, w_up, xin, g, gout)
    return gin, dg, dw_up, dw_down


def kernel(x, norm_mix_g, norm_ffn_g, fox_w_in, fox_b_f, fox_w_out, kv_norm_g, mla_w_kv_a, mla_kv_a_norm_g, mla_w_kv_b, mla_w_q_a, mla_q_a_norm_g, mla_w_q_b, mla_w_out, ffn_w_up, ffn_w_down, final_norm_g, loss_target, m_norm_mix_g, m_norm_ffn_g, m_fox_w_in, m_fox_b_f, m_fox_w_out, m_kv_norm_g, m_mla_w_kv_a, m_mla_kv_a_norm_g, m_mla_w_kv_b, m_mla_w_q_a, m_mla_q_a_norm_g, m_mla_w_q_b, m_mla_w_out, m_ffn_w_up, m_ffn_w_down, m_final_norm_g, v_norm_mix_g, v_norm_ffn_g, v_fox_w_in, v_fox_b_f, v_fox_w_out, v_kv_norm_g, v_mla_w_kv_a, v_mla_kv_a_norm_g, v_mla_w_kv_b, v_mla_w_q_a, v_mla_q_a_norm_g, v_mla_w_q_b, v_mla_w_out, v_ffn_w_up, v_ffn_w_down, v_final_norm_g):
    T = x.shape[1]
    D = D_MODEL
    tq = 512 if T >= 2048 else 128
    x0 = x[0]
    tgt = loss_target[0]

    gat_fox = _allgather_weights([fox_w_in[0].astype(BF16), fox_w_out[0].astype(BF16)])
    later_shards = [s.astype(BF16) for s in (mla_w_kv_a, mla_w_kv_b, mla_w_q_a[0], mla_w_q_b[0],
                                             mla_w_out[0], ffn_w_up, ffn_w_down)]
    w_in = gat_fox[0].transpose(1, 0, 2).reshape(D, 3 * D + FOX_HEADS)
    w_qkv = w_in[:, :3 * D]
    w_f = _pad_cols(w_in[:, 3 * D:], 128)
    w_fo = gat_fox[1].reshape(D, D)
    g_mix0, g_mix1 = norm_mix_g[0:1], norm_mix_g[1:2]
    g_ffn0, g_ffn1 = norm_ffn_g[0:1], norm_ffn_g[1:2]
    g_kv = kv_norm_g.reshape(1, D)
    g_kva = mla_kv_a_norm_g.reshape(1, KV_LORA)
    g_qa = mla_q_a_norm_g.reshape(1, Q_LORA)
    g_fin = final_norm_g.reshape(1, D)

    inv = 1.0 / (ROPE_BASE ** (jnp.arange(0, QK_ROPE, 2, dtype=F32) / QK_ROPE))
    ang = jnp.arange(T, dtype=F32)[:, None] * inv[None, :]
    cos, sin = jnp.cos(ang), jnp.sin(ang)
    cos2 = jnp.concatenate([cos, cos], axis=-1)
    sgn_sin = jnp.concatenate([-sin, sin], axis=-1)

    h0 = _rms("l0_mix_norm", x0, g_mix0, BF16)
    fl_pad = _mm("fox_gate_logit", h0, w_f, "nn", (F32,))
    fl = fl_pad[:, :FOX_HEADS].T
    b_f = fox_b_f.reshape(FOX_HEADS, 1)
    cgate = _gate_cumsum("fox_gate_scan", fl, b_f, tq)
    fox_scale = FOX_HEAD_DIM ** -0.5
    col_scale = jnp.where(jnp.arange(3 * D) < D, fox_scale, 1.0).astype(BF16)
    tail = jnp.arange(FOX_AUG - FOX_HEAD_DIM)
    ones_q = (tail < 3).astype(F32)
    consts_k = ((tail >= 4) & (tail < 7)).astype(F32) + (tail == 3).astype(F32) * (1.0 / fox_scale)
    tails = jnp.broadcast_to(jnp.stack([ones_q, consts_k, ones_q])[:, None, None, :],
                             (3, FOX_HEADS, 1, FOX_AUG - FOX_HEAD_DIM)).reshape(3 * FOX_HEADS, 1, -1)
    qkv_h = _mm_head_slabs("fox_qkv", h0, w_qkv * col_scale, tails, FOX_HEAD_DIM, BF16)
    fk_aug = _widen(qkv_h[FOX_HEADS:2 * FOX_HEADS], FOX_AUG, pieces_at=FOX_HEAD_DIM,
                    pieces=_split3(-cgate))
    (fo, flse), gat = _flash_fwd("fox_attn", qkv_h, fk_aug, qkv_h, FOX_HEAD_DIM, tq,
                                 exchange=("gather", later_shards), q_head0=0, v_head0=2 * FOX_HEADS)
    w_kva = _pad_cols(gat[0].reshape(D, KV_LORA + QK_ROPE), KV_A_PAD)
    w_kvb_h = gat[1]
    w_qa = gat[2].reshape(D, Q_LORA)
    w_qb_h = gat[3]
    w_mo = gat[4].reshape(D, D)
    w_up = gat[5].transpose(1, 2, 0, 3).reshape(2, D, D_FF)
    w_down = gat[6].transpose(1, 0, 2, 3).reshape(2, D_FF, D)
    fctx = _from_heads(fo).astype(BF16)
    x1 = _mm("fox_out", fctx, w_fo, "nn", (F32,), epi=lambda acc, r: (acc + r,), extras=(x0,))
    x2, mlp0 = _mlp_fwd("l0_ffn", x1, g_ffn0, w_up[0], w_down[0])

    src = _rms("kv_norm", x2, g_kv, BF16)
    kva = _mm("kv_a", src, w_kva, "nn", (F32,))
    kva_lat = kva[:, :KV_LORA]
    c_kv = _rms("kv_a_norm", kva_lat, g_kva, BF16)
    k_rope = _rope("k_rope", kva[:, KV_LORA:KV_LORA + QK_ROPE][None], cos2, sgn_sin, BF16)
    kvb_h = _mm_heads("kv_b", c_kv, w_kvb_h, BF16)
    mk = jnp.concatenate([kvb_h[:, :, :QK_NOPE],
                          jnp.broadcast_to(k_rope, (MLA_HEADS, T, QK_ROPE))], axis=-1)
    mv = kvb_h[:, :, QK_NOPE:]

    h1 = _rms("l1_mix_norm", x2, g_mix1, BF16)
    qa = _mm("q_a", h1, w_qa, "nn", (F32,))
    c_q = _rms("q_a_norm", qa, g_qa, BF16)
    mla_scale = (QK_NOPE + QK_ROPE) ** -0.5
    mq = _mla_q_proj("q_b", c_q, w_qb_h, cos2, sgn_sin, mla_scale)
    mv_aug = _widen(mv, MLA_AUG, ones_at=V_HEAD)
    (mo, mlse), _ = _flash_fwd("mla_attn", mq, mk, mv_aug, V_HEAD, tq)
    mctx = _from_heads(mo).astype(BF16)
    x3 = _mm("mla_out", mctx, w_mo, "nn", (F32,), epi=lambda acc, r: (acc + r,), extras=(x2,))
    x4, mlp1 = _mlp_fwd("l1_ffn", x3, g_ffn1, w_up[1], w_down[1])

    g4, dg_fin, loss_vec = _loss_head("loss_head", x4, g_fin, tgt)

    g3, dg_ffn1, dw_up1, dw_down1 = _mlp_bwd("l1_ffn", g4, x3, g_ffn1, w_up[1], w_down[1], mlp1)

    dw_mo = _mm_tn("mla_out_dw", mctx, g3)
    dmo = _to_heads(_mm("mla_out_dx", g3, w_mo, "nt", (BF16,)), MLA_HEADS)
    mdelta = _row_dot("mla_delta", mo, dmo)
    dqk = QK_NOPE + QK_ROPE
    mq_bwd = _widen(mq, MLA_AUG, pieces_at=dqk, pieces=_split3(-mlse))
    mk_bwd = _widen(mk, MLA_AUG, ones_at=dqk)
    mdo_aug = _widen(dmo, MLA_AUG, pieces_at=V_HEAD, pieces=_split3(-mdelta))
    (mdq, mdk, mdv), _ = _flash_bwd("mla_attn_bwd", mq_bwd, mk_bwd, mv_aug, mdo_aug, mla_scale, tq)
    mdq = mdq[:, :, :dqk]
    mdk = mdk[:, :, :dqk]
    mdv = mdv[:, :, :V_HEAD]
    dq_rope = _rope_bwd("q_rope_bwd", mdq[:, :, QK_NOPE:], cos2, sgn_sin, False)
    dqf_h = jnp.concatenate([mdq[:, :, :QK_NOPE], dq_rope], axis=-1)
    dw_qb_h = _mm_heads_dw("q_b_dw", c_q, dqf_h)
    dc_q = _mm_heads_dx("q_b_dx", dqf_h, w_qb_h)
    dqa, dg_qa = _rms_bwd("q_a_norm_bwd", qa, g_qa, dc_q)
    dw_qa = _mm_tn("q_a_dw", h1, dqa)
    g2a, dg_mix1 = _mm_rms_bwd("q_a_dx_norm_bwd", dqa, w_qa, x2, g_mix1, g3)

    dk_rope = _rope_bwd("k_rope_bwd", mdk[:, :, QK_NOPE:], cos2, sgn_sin, True)
    dkvb_h = jnp.concatenate([mdk[:, :, :QK_NOPE], mdv], axis=-1)
    dw_kvb_h = _mm_heads_dw("kv_b_dw", c_kv, dkvb_h)
    dc_kv = _mm_heads_dx("kv_b_dx", dkvb_h, w_kvb_h)
    dkva_lat, dg_kva = _rms_bwd("kv_a_norm_bwd", kva_lat, g_kva, dc_kv)
    dkva = _pad_cols(jnp.concatenate([dkva_lat, dk_rope], axis=-1), KV_A_PAD)
    dw_kva = _mm_tn("kv_a_dw", src, dkva)[:, :KV_LORA + QK_ROPE]
    g2, dg_kv = _mm_rms_bwd("kv_a_dx_norm_bwd", dkva, w_kva, x2, g_kv, g2a)

    g1, dg_ffn0, dw_up0, dw_down0 = _mlp_bwd("l0_ffn", g2, x1, g_ffn0, w_up[0], w_down[0], mlp0)

    dw_fo = _mm_tn("fox_out_dw", fctx, g1)
    dfo = _to_heads(_mm("fox_out_dx", g1, w_fo, "nt", (BF16,)), FOX_HEADS)
    fdelta = _row_dot("fox_delta", fo, dfo)
    fq_bwd = _widen(qkv_h[:FOX_HEADS], FOX_AUG, pieces_at=FOX_HEAD_DIM + 4, pieces=_split3(-flse))
    fdo_aug = _widen(dfo, FOX_AUG, pieces_at=FOX_HEAD_DIM, pieces=_split3(-fdelta))
    dw_up = jnp.stack([dw_up0, dw_up1])
    dw_down = jnp.stack([dw_down0, dw_down1])
    early = [
        dw_fo.reshape(N_DEV, D // N_DEV, D),
        dw_kva.reshape(N_DEV, D // N_DEV, KV_LORA + QK_ROPE),
        dw_kvb_h,
        dw_qa.reshape(N_DEV, D // N_DEV, Q_LORA),
        dw_qb_h,
        dw_mo.reshape(N_DEV, D // N_DEV, D),
        dw_up.reshape(2, D, N_DEV, -1).transpose(2, 0, 1, 3),
        dw_down.reshape(2, N_DEV, D_FF // N_DEV, D).transpose(1, 0, 2, 3),
    ]
    fd_aug, early_parts = _flash_bwd(
        "fox_attn_bwd", fq_bwd, fk_aug, qkv_h, fdo_aug, fox_scale, tq,
        exchange=("scatter", [g.astype(BF16) for g in early]), v_head0=2 * FOX_HEADS, token_major_out=True)
    dqkv, (ds_rows, ds_cols) = _pack_head_grads("fox_grad_pack", list(fd_aug), FOX_AUG, FOX_HEAD_DIM,
                                                picks=((0, FOX_HEAD_DIM + 3), (1, FOX_HEAD_DIM)))
    dfl, db_f = _gate_cumsum_bwd("fox_gate_scan_bwd", ds_rows, ds_cols, fl, b_f, tq)
    dfl_pad = _pad_cols(dfl.T, 128)
    dw_qkv = _mm_tn("fox_qkv_dw", h0, dqkv)
    dw_f = _mm_tn("fox_gate_dw", h0, dfl_pad)[:, :FOX_HEADS]
    dw_in = jnp.concatenate([dw_qkv, dw_f], axis=-1)
    dh0a = _mm("fox_gate_dx", dfl_pad, w_f, "nt", (F32,))
    late = dw_in.reshape(D, N_DEV, -1).transpose(1, 0, 2).astype(BF16)
    grad_x, dg_mix0, late_parts = _mm_rms_bwd("fox_qkv_dx_norm_bwd", dqkv, w_qkv, x0, g_mix0, g1, add=dh0a,
                                              exchange=("scatter", [late]))

    parts = list(late_parts) + list(early_parts)

    names = ["fox_w_in", "fox_w_out", "mla_w_kv_a", "mla_w_kv_b", "mla_w_q_a", "mla_w_q_b",
             "mla_w_out", "ffn_w_up", "ffn_w_down"]
    moms = [m_fox_w_in, m_fox_w_out, m_mla_w_kv_a, m_mla_w_kv_b, m_mla_w_q_a, m_mla_w_q_b,
            m_mla_w_out, m_ffn_w_up, m_ffn_w_down]
    vars_ = [v_fox_w_in, v_fox_w_out, v_mla_w_kv_a, v_mla_w_kv_b, v_mla_w_q_a, v_mla_w_q_b,
             v_mla_w_out, v_ffn_w_up, v_ffn_w_down]
    full = [fox_w_in, fox_w_out, mla_w_kv_a, mla_w_kv_b, mla_w_q_a, mla_w_q_b, mla_w_out,
            ffn_w_up, ffn_w_down]
    big = {}
    for nm, p, w, m, v in zip(names, parts, full, moms, vars_):
        C = w.shape[-1]
        res = _adamw(f"adamw_{nm}", p.reshape(N_DEV, -1, C), w.reshape(-1, C), m.reshape(-1, C),
                     v.reshape(-1, C))
        big[nm] = [r.reshape(w.shape) for r in res]

    zrow = jnp.zeros((D,), F32)
    g_small = _pack_small(jnp.concatenate([dg_mix0, dg_mix1]), jnp.concatenate([dg_ffn0, dg_ffn1]),
                          dg_kv, dg_fin, dg_kva, dg_qa, db_f, zrow.at[0].set(loss_vec[0, 0]))
    tot_small = _allreduce_small(g_small)
    w_small = _pack_small(norm_mix_g, norm_ffn_g, kv_norm_g, final_norm_g, mla_kv_a_norm_g,
                          mla_q_a_norm_g, fox_b_f, zrow)
    m_small = _pack_small(m_norm_mix_g, m_norm_ffn_g, m_kv_norm_g, m_final_norm_g, m_mla_kv_a_norm_g,
                          m_mla_q_a_norm_g, m_fox_b_f, zrow)
    v_small = _pack_small(v_norm_mix_g, v_norm_ffn_g, v_kv_norm_g, v_final_norm_g, v_mla_kv_a_norm_g,
                          v_mla_q_a_norm_g, v_fox_b_f, zrow)
    small = _adamw("adamw_small", tot_small[None], w_small, m_small, v_small)
    loss = tot_small[7, 0]
    small = [_unpack_small(s) for s in small]

    def ordered(i):
        mix, ffn, bf, kv, kva, qa, fin = small[i]
        return [mix, ffn, big["fox_w_in"][i], bf, big["fox_w_out"][i], kv, big["mla_w_kv_a"][i], kva,
                big["mla_w_kv_b"][i], big["mla_w_q_a"][i], qa, big["mla_w_q_b"][i],
                big["mla_w_out"][i], big["ffn_w_up"][i], big["ffn_w_down"][i], fin]

    return (loss, grad_x[None], *ordered(0), *ordered(1), *ordered(2), *ordered(3))
```

```python
import functools
import math

import jax
import jax.numpy as jnp
from jax import lax
from jax.experimental import pallas as pl
from jax.experimental.pallas import tpu as pltpu

F32 = jnp.float32
BF16 = jnp.bfloat16
MESH = pl.DeviceIdType.MESH

N_DEV = 8
D_MODEL = 1024
FOX_HEADS = 16
FOX_HEAD_DIM = 64
FOX_AUG = 128
MLA_AUG = 256
MLA_HEADS = 8
QK_NOPE = 128
QK_ROPE = 64
V_HEAD = 128
Q_LORA = 384
KV_LORA = 256
KV_A_PAD = 384
D_FF = 4096
ROPE_BASE = 10000.0
EPS = 1e-6
NEG = -1e30

ADAM_LR = 0.001
ADAM_B1 = 0.9
ADAM_B2 = 0.999
ADAM_EPS = 1e-08
ADAM_WD = 0.01
ADAM_STEP = 10

VMEM_LIMIT_BYTES = 56 * 1024 * 1024

NN = (((1,), (0,)), ((), ()))
NT = (((1,), (1,)), ((), ()))
TN = (((0,), (0,)), ((), ()))
_FORMS = {"nn": NN, "nt": NT}


def _cparams(sem=None):
    return pltpu.CompilerParams(dimension_semantics=sem, vmem_limit_bytes=VMEM_LIMIT_BYTES)


def _pick(n, cands):
    for c in cands:
        if c <= n and n % c == 0:
            return c
    return n


def _dot(a, b, dims):
    return lax.dot_general(a, b, dims, preferred_element_type=F32)


def _mm(name, a, b, form, out_dtypes, epi=None, extras=(), tm=1024, tn=None):
    M, K = a.shape
    N = b.shape[1] if form == "nn" else b.shape[0]
    tm = _pick(M, (tm, 512, 256, 128))
    tn = _pick(N, (tn or (1024 if K <= 1024 else 512), 512, 384, 256, 128))
    n_ex = len(extras)
    n_out = len(out_dtypes)
    cast_once = a.dtype != BF16

    def body(*refs):
        a_ref, b_ref = refs[0], refs[1]
        ex = refs[2:2 + n_ex]
        outs = refs[2 + n_ex:2 + n_ex + n_out]
        if cast_once:
            a_sc = refs[2 + n_ex + n_out]

            @pl.when(pl.program_id(1) == 0)
            def _():
                a_sc[...] = a_ref[...].astype(BF16)

            av = a_sc[...]
        else:
            av = a_ref[...]
        acc = _dot(av, b_ref[...].astype(BF16), _FORMS[form])
        res = epi(acc, *[e[...] for e in ex]) if epi is not None else (acc,)
        for o_ref, r in zip(outs, res):
            o_ref[...] = r.astype(o_ref.dtype)

    if form == "nn":
        b_spec = pl.BlockSpec((K, tn), lambda i, j: (0, j))
    else:
        b_spec = pl.BlockSpec((tn, K), lambda i, j: (j, 0))
    tile = pl.BlockSpec((tm, tn), lambda i, j: (i, j))
    out = pl.pallas_call(
        body, name=name, grid=(M // tm, N // tn),
        in_specs=[pl.BlockSpec((tm, K), lambda i, j: (i, 0)), b_spec] + [tile] * n_ex,
        out_specs=[tile] * n_out,
        out_shape=[jax.ShapeDtypeStruct((M, N), dt) for dt in out_dtypes],
        scratch_shapes=[pltpu.VMEM((tm, K), BF16)] if cast_once else [],
        compiler_params=_cparams(("parallel", "arbitrary")),
    )(a, b, *extras)
    return out if n_out > 1 else out[0]


def _mm_tn(name, a, b):
    T, Ka = a.shape
    N = b.shape[1]
    tk = _pick(Ka, (1024, 512, 384, 256, 128))
    tn = _pick(N, (1024, 768, 512, 384, 256, 128))
    tt = _pick(T, (1024, 512, 256, 128))

    def body(a_ref, b_ref, o_ref):
        @pl.when(pl.program_id(2) == 0)
        def _():
            o_ref[...] = jnp.zeros_like(o_ref)

        o_ref[...] += _dot(a_ref[...].astype(BF16), b_ref[...].astype(BF16), TN)

    return pl.pallas_call(
        body, name=name, grid=(Ka // tk, N // tn, T // tt),
        in_specs=[pl.BlockSpec((tt, tk), lambda i, j, t: (t, i)),
                  pl.BlockSpec((tt, tn), lambda i, j, t: (t, j))],
        out_specs=pl.BlockSpec((tk, tn), lambda i, j, t: (i, j)),
        out_shape=jax.ShapeDtypeStruct((Ka, N), F32),
        compiler_params=_cparams(("parallel", "parallel", "arbitrary")),
    )(a, b)


def _mm_heads(name, a, w, out_dtype):
    T, K = a.shape
    H, _, N = w.shape
    tm = _pick(T, (1024, 512, 256, 128))

    def body(a_ref, w_ref, o_ref):
        av = a_ref[...].astype(BF16)
        for s in range(H):
            o_ref[s] = _dot(av, w_ref[s].astype(BF16), NN).astype(o_ref.dtype)

    return pl.pallas_call(
        body, name=name, grid=(T // tm,),
        in_specs=[pl.BlockSpec((tm, K), lambda i: (i, 0)), pl.BlockSpec((H, K, N), lambda i: (0, 0, 0))],
        out_specs=pl.BlockSpec((H, tm, N), lambda i: (0, i, 0)),
        out_shape=jax.ShapeDtypeStruct((H, T, N), out_dtype),
        compiler_params=_cparams(("parallel",)),
    )(a, w)


def _mm_head_slabs(name, a, w, tails, head_dim, out_dtype, heads_per_step=8):
    T, K = a.shape
    S, _, tail = tails.shape
    hb = heads_per_step
    tm = _pick(T, (1024, 512, 256, 128))

    def body(a_ref, w_ref, t_ref, o_ref):
        acc = _dot(a_ref[...].astype(BF16), w_ref[...].astype(BF16), NN)
        for s in range(hb):
            slab = jnp.concatenate([acc[:, s * head_dim:(s + 1) * head_dim],
                                    jnp.broadcast_to(t_ref[s], (tm, tail))], axis=-1)
            o_ref[s] = slab.astype(o_ref.dtype)

    return pl.pallas_call(
        body, name=name, grid=(T // tm, S // hb),
        in_specs=[pl.BlockSpec((tm, K), lambda i, j: (i, 0)),
                  pl.BlockSpec((K, hb * head_dim), lambda i, j: (0, j)),
                  pl.BlockSpec((hb, 1, tail), lambda i, j: (j, 0, 0))],
        out_specs=pl.BlockSpec((hb, tm, head_dim + tail), lambda i, j: (j, i, 0)),
        out_shape=jax.ShapeDtypeStruct((S, T, head_dim + tail), out_dtype),
        compiler_params=_cparams(("parallel", "arbitrary")),
    )(a, w, tails)


def _mm_heads_dw(name, a, g):
    T, K = a.shape
    H, _, N = g.shape
    tt = _pick(T, (1024, 512, 256, 128))

    def body(a_ref, g_ref, o_ref):
        @pl.when(pl.program_id(0) == 0)
        def _():
            o_ref[...] = jnp.zeros_like(o_ref)

        av = a_ref[...].astype(BF16)
        for s in range(H):
            o_ref[s] += _dot(av, g_ref[s].astype(BF16), TN)

    return pl.pallas_call(
        body, name=name, grid=(T // tt,),
        in_specs=[pl.BlockSpec((tt, K), lambda t: (t, 0)), pl.BlockSpec((H, tt, N), lambda t: (0, t, 0))],
        out_specs=pl.BlockSpec((H, K, N), lambda t: (0, 0, 0)),
        out_shape=jax.ShapeDtypeStruct((H, K, N), F32),
        compiler_params=_cparams(("arbitrary",)),
    )(a, g)


def _mm_heads_dx(name, g, w):
    H, T, N = g.shape
    K = w.shape[1]
    tm = _pick(T, (512, 256, 128))

    def body(g_ref, w_ref, o_ref):
        acc = _dot(g_ref[0].astype(BF16), w_ref[0].astype(BF16), NT)
        for s in range(1, H):
            acc = acc + _dot(g_ref[s].astype(BF16), w_ref[s].astype(BF16), NT)
        o_ref[...] = acc

    return pl.pallas_call(
        body, name=name, grid=(T // tm,),
        in_specs=[pl.BlockSpec((H, tm, N), lambda i: (0, i, 0)), pl.BlockSpec((H, K, N), lambda i: (0, 0, 0))],
        out_specs=pl.BlockSpec((tm, K), lambda i: (i, 0)),
        out_shape=jax.ShapeDtypeStruct((T, K), F32),
        compiler_params=_cparams(("parallel",)),
    )(g, w)


def _rms(name, x, g, out_dtype):
    T, D = x.shape
    tm = _pick(T, (1024, 512, 256, 128))

    def body(x_ref, g_ref, o_ref):
        xf = x_ref[...]
        r = lax.rsqrt(jnp.mean(xf * xf, axis=-1, keepdims=True) + EPS)
        o_ref[...] = (xf * r * g_ref[...]).astype(o_ref.dtype)

    return pl.pallas_call(
        body, name=name, grid=(T // tm,),
        in_specs=[pl.BlockSpec((tm, D), lambda i: (i, 0)), pl.BlockSpec((1, D), lambda i: (0, 0))],
        out_specs=pl.BlockSpec((tm, D), lambda i: (i, 0)),
        out_shape=jax.ShapeDtypeStruct((T, D), out_dtype),
        compiler_params=_cparams(("parallel",)),
    )(x, g)


def _rms_bwd(name, x, g, dh, dres=None):
    T, D = x.shape
    tm = _pick(T, (512, 256, 128))
    has_res = dres is not None

    def body(*refs):
        if has_res:
            x_ref, g_ref, dh_ref, dres_ref, dx_ref, dg_ref = refs
        else:
            x_ref, g_ref, dh_ref, dx_ref, dg_ref = refs

        @pl.when(pl.program_id(0) == 0)
        def _():
            dg_ref[...] = jnp.zeros_like(dg_ref)

        xf = x_ref[...]
        r = lax.rsqrt(jnp.mean(xf * xf, axis=-1, keepdims=True) + EPS)
        xhat = xf * r
        dy = dh_ref[...].astype(F32)
        dxh = dy * g_ref[...]
        dx = r * (dxh - xhat * jnp.mean(dxh * xhat, axis=-1, keepdims=True))
        if has_res:
            dx = dx + dres_ref[...]
        dx_ref[...] = dx
        dg_ref[...] += jnp.sum(dy * xhat, axis=0, keepdims=True)

    row = pl.BlockSpec((tm, D), lambda i: (i, 0))
    vec = pl.BlockSpec((1, D), lambda i: (0, 0))
    ins = [x, g, dh] + ([dres] if has_res else [])
    return pl.pallas_call(
        body, name=name, grid=(T // tm,),
        in_specs=[row, vec, row] + ([row] if has_res else []),
        out_specs=[row, vec],
        out_shape=[jax.ShapeDtypeStruct((T, D), F32), jax.ShapeDtypeStruct((1, D), F32)],
        compiler_params=_cparams(("arbitrary",)),
    )(*ins)


def _mm_rms_bwd(name, a, b, x, g, dres, add=None, exchange=None):
    T, K = a.shape
    D = b.shape[0]
    tm = _pick(T, (512, 256, 128))
    has_add = add is not None

    def body(*refs):
        a_ref, b_ref, x_ref, g_ref, dres_ref = refs[:5]
        dx_ref, dg_ref = refs[-2:]

        @pl.when(pl.program_id(0) == 0)
        def _():
            dg_ref[...] = jnp.zeros_like(dg_ref)

        dy = _dot(a_ref[...].astype(BF16), b_ref[...].astype(BF16), NT)
        if has_add:
            dy = dy + refs[5][...]
        xf = x_ref[...]
        r = lax.rsqrt(jnp.mean(xf * xf, axis=-1, keepdims=True) + EPS)
        xhat = xf * r
        dxh = dy * g_ref[...]
        dx_ref[...] = r * (dxh - xhat * jnp.mean(dxh * xhat, axis=-1, keepdims=True)) + dres_ref[...]
        dg_ref[...] += jnp.sum(dy * xhat, axis=0, keepdims=True)

    row = pl.BlockSpec((tm, D), lambda i: (i, 0))
    vec = pl.BlockSpec((1, D), lambda i: (0, 0))
    (dx, dg), exchanged = _call_carrying(
        body, name, (T // tm,),
        in_specs=[pl.BlockSpec((tm, K), lambda i: (i, 0)), pl.BlockSpec((D, K), lambda i: (0, 0)), row, vec, row]
        + ([row] if has_add else []),
        out_specs=[row, vec],
        out_shape=[jax.ShapeDtypeStruct((T, D), F32), jax.ShapeDtypeStruct((1, D), F32)],
        scratch_shapes=[], operands=(a, b, x, g, dres) + ((add,) if has_add else ()), exchange=exchange,
        sequential=True)
    return (dx, dg) if exchange is None else (dx, dg, exchanged)


def _loss_head(name, x, g, tgt):
    T, D = x.shape
    tm = _pick(T, (512, 256, 128))

    def body(x_ref, g_ref, t_ref, dx_ref, dg_ref, loss_ref):
        @pl.when(pl.program_id(0) == 0)
        def _():
            dg_ref[...] = jnp.zeros_like(dg_ref)
            loss_ref[...] = jnp.zeros_like(loss_ref)

        xf = x_ref[...]
        r = lax.rsqrt(jnp.mean(xf * xf, axis=-1, keepdims=True) + EPS)
        xhat = xf * r
        gv = g_ref[...]
        err = xhat * gv - t_ref[...]
        row_loss = jnp.mean(err * err, axis=-1, keepdims=True)
        loss_ref[...] += 0.5 * jnp.sum(row_loss, axis=0, keepdims=True)
        dy = err * (1.0 / D)
        dxh = dy * gv
        dx_ref[...] = r * (dxh - xhat * jnp.mean(dxh * xhat, axis=-1, keepdims=True))
        dg_ref[...] += jnp.sum(dy * xhat, axis=0, keepdims=True)

    row = pl.BlockSpec((tm, D), lambda i: (i, 0))
    vec = pl.BlockSpec((1, D), lambda i: (0, 0))
    return pl.pallas_call(
        body, name=name, grid=(T // tm,),
        in_specs=[row, vec, row],
        out_specs=[row, vec, pl.BlockSpec((1, 128), lambda i: (0, 0))],
        out_shape=[jax.ShapeDtypeStruct((T, D), F32), jax.ShapeDtypeStruct((1, D), F32),
                   jax.ShapeDtypeStruct((1, 128), F32)],
        compiler_params=_cparams(("arbitrary",)),
    )(x, g, tgt)


def _swap_halves(t):
    half = t.shape[-1] // 2
    return jnp.concatenate([t[:, half:], t[:, :half]], axis=-1)


def _rope(name, t, cos2, sgn_sin, out_dtype):
    H, T, R = t.shape
    tm = _pick(T, (1024, 512, 256, 128))

    def body(t_ref, c_ref, s_ref, o_ref):
        tf = t_ref[...].astype(F32)
        o_ref[...] = (tf * c_ref[...] + _swap_halves(tf) * s_ref[...]).astype(o_ref.dtype)

    slab = pl.BlockSpec((None, tm, R), lambda h, i: (h, i, 0))
    tab = pl.BlockSpec((tm, R), lambda h, i: (i, 0))
    return pl.pallas_call(
        body, name=name, grid=(H, T // tm),
        in_specs=[slab, tab, tab], out_specs=slab,
        out_shape=jax.ShapeDtypeStruct((H, T, R), out_dtype),
        compiler_params=_cparams(("parallel", "parallel")),
    )(t, cos2, sgn_sin)


def _mla_q_proj(name, a, w, cos2, sgn_sin, scale):
    T, K = a.shape
    H, _, W = w.shape
    R = cos2.shape[1]
    tm = _pick(T, (1024, 512, 256, 128))

    def body(a_ref, w_ref, c_ref, s_ref, o_ref):
        av = a_ref[...].astype(BF16)
        for h in range(H):
            qf = _dot(av, w_ref[h].astype(BF16), NN)
            r = qf[:, W - R:]
            roped = r * c_ref[...] + _swap_halves(r) * s_ref[...]
            o_ref[h] = (jnp.concatenate([qf[:, :W - R], roped], axis=-1) * scale).astype(o_ref.dtype)

    tab = pl.BlockSpec((tm, R), lambda i: (i, 0))
    return pl.pallas_call(
        body, name=name, grid=(T // tm,),
        in_specs=[pl.BlockSpec((tm, K), lambda i: (i, 0)), pl.BlockSpec((H, K, W), lambda i: (0, 0, 0)), tab, tab],
        out_specs=pl.BlockSpec((H, tm, W), lambda i: (0, i, 0)),
        out_shape=jax.ShapeDtypeStruct((H, T, W), BF16),
        compiler_params=_cparams(("parallel",)),
    )(a, w, cos2, sgn_sin)


def _rope_bwd(name, dy, cos2, sgn_sin, sum_heads):
    H, T, R = dy.shape
    tm = _pick(T, (1024, 512, 256, 128))

    def body(d_ref, c_ref, s_ref, o_ref):
        d = d_ref[...]
        if sum_heads:
            tot = d[0]
            for h in range(1, H):
                tot = tot + d[h]
            d = tot
        o_ref[...] = d * c_ref[...] + _swap_halves(d * s_ref[...])

    if sum_heads:
        grid = (T // tm,)
        in_slab = pl.BlockSpec((H, tm, R), lambda i: (0, i, 0))
        out_slab = pl.BlockSpec((tm, R), lambda i: (i, 0))
        tab = pl.BlockSpec((tm, R), lambda i: (i, 0))
        out_shape = jax.ShapeDtypeStruct((T, R), F32)
        sem = ("parallel",)
    else:
        grid = (H, T // tm)
        in_slab = pl.BlockSpec((None, tm, R), lambda h, i: (h, i, 0))
        out_slab = in_slab
        tab = pl.BlockSpec((tm, R), lambda h, i: (i, 0))
        out_shape = jax.ShapeDtypeStruct((H, T, R), F32)
        sem = ("parallel", "parallel")
    return pl.pallas_call(
        body, name=name, grid=grid, in_specs=[in_slab, tab, tab], out_specs=out_slab,
        out_shape=out_shape, compiler_params=_cparams(sem),
    )(dy, cos2, sgn_sin)


def _log_sigmoid(z):
    return jnp.minimum(z, 0.0) - jnp.log(1.0 + jnp.exp(-jnp.abs(z)))


def _gate_cumsum(name, fl, b, tb):
    H, T = fl.shape

    def body(f_ref, b_ref, c_ref, carry):
        @pl.when(pl.program_id(0) == 0)
        def _():
            carry[...] = jnp.zeros_like(carry)

        ls = _log_sigmoid(f_ref[...] + b_ref[...])
        src = lax.broadcasted_iota(jnp.int32, (tb, tb), 0)
        dst = lax.broadcasted_iota(jnp.int32, (tb, tb), 1)
        tri = (src <= dst).astype(F32)
        c = lax.dot_general(ls, tri, NN, precision=lax.Precision.HIGHEST,
                            preferred_element_type=F32) + carry[...]
        c_ref[...] = c
        carry[...] = carry[...] + jnp.sum(ls, axis=-1, keepdims=True)

    return pl.pallas_call(
        body, name=name, grid=(T // tb,),
        in_specs=[pl.BlockSpec((H, tb), lambda i: (0, i)), pl.BlockSpec((H, 1), lambda i: (0, 0))],
        out_specs=pl.BlockSpec((H, tb), lambda i: (0, i)),
        out_shape=jax.ShapeDtypeStruct((H, T), F32),
        scratch_shapes=[pltpu.VMEM((H, 1), F32)],
        compiler_params=_cparams(("arbitrary",)),
    )(fl, b)


def _gate_cumsum_bwd(name, d_query, d_key, fl, b, tb):
    H, T = fl.shape
    nb = T // tb

    def body(dq_ref, dk_ref, f_ref, b_ref, dfl_ref, db_ref, carry):
        @pl.when(pl.program_id(0) == 0)
        def _():
            carry[...] = jnp.zeros_like(carry)
            db_ref[...] = jnp.zeros_like(db_ref)

        d = dq_ref[...] - dk_ref[...]
        src = lax.broadcasted_iota(jnp.int32, (tb, tb), 0)
        dst = lax.broadcasted_iota(jnp.int32, (tb, tb), 1)
        tri = (src >= dst).astype(F32)
        dls = lax.dot_general(d, tri, NN, precision=lax.Precision.HIGHEST,
                              preferred_element_type=F32) + carry[...]
        z = f_ref[...] + b_ref[...]
        dfl = dls * (1.0 / (1.0 + jnp.exp(z)))
        dfl_ref[...] = dfl
        db_ref[...] += jnp.sum(dfl, axis=-1, keepdims=True)
        carry[...] = carry[...] + jnp.sum(d, axis=-1, keepdims=True)

    blk = pl.BlockSpec((H, tb), lambda i: (0, nb - 1 - i))
    vec = pl.BlockSpec((H, 1), lambda i: (0, 0))
    return pl.pallas_call(
        body, name=name, grid=(nb,),
        in_specs=[blk, blk, blk, vec], out_specs=[blk, vec],
        out_shape=[jax.ShapeDtypeStruct((H, T), F32), jax.ShapeDtypeStruct((H, 1), F32)],
        scratch_shapes=[pltpu.VMEM((H, 1), F32)],
        compiler_params=_cparams(("arbitrary",)),
    )(d_query, d_key, fl, b)


def _chunk_rows(j, tq):
    return pl.ds(pl.multiple_of(j * tq, tq), tq)


def _column_as_row(col):
    return jnp.broadcast_to(col, (col.shape[0], 128)).T[:1, :]


def _flash_fwd(name, q, k, v_aug, dv, tq, exchange=None, q_head0=0, v_head0=0):
    H, T, dqk = k.shape
    dva = v_aug.shape[2]
    tk = tq
    tq = next(m * tk for m in (4, 2, 1) if T % (m * tk) == 0)
    r = tq // tk
    nq = T // tq

    def body(q_ref, k_ref, v_ref, o_ref, lse_ref, m_sc, acc_sc):
        qi = pl.program_id(1)
        m_sc[...] = jnp.full_like(m_sc, NEG)
        acc_sc[...] = jnp.zeros_like(acc_sc)

        def chunk(j, diag):
            rows = _chunk_rows(j, tk)
            live = slice(0 if diag is None else diag * tk, tq)
            n_live = tq - live.start
            s = _dot(q_ref[live, :], k_ref[rows, :], NT)
            if diag is not None:
                row = lax.broadcasted_iota(jnp.int32, (n_live, tk), 0)
                col = lax.broadcasted_iota(jnp.int32, (n_live, tk), 1)
                s = jnp.where(col <= row, s, NEG)
            m_prev = m_sc[live, :]
            m_new = jnp.maximum(m_prev, jnp.max(s, axis=1, keepdims=True))
            p = jnp.exp(s - jnp.tile(m_new, (1, tk // 128)))
            alpha = jnp.tile(jnp.exp(m_prev - m_new), (1, dva // 128))
            acc_sc[live, :] = alpha * acc_sc[live, :] + _dot(p.astype(BF16), v_ref[rows, :], NN)
            m_sc[live, :] = m_new

        def off_diagonal(j, carry):
            chunk(j, None)
            return carry

        lax.fori_loop(0, qi * r, off_diagonal, 0)
        for d in range(r):
            chunk(qi * r + d, d)
        acc = acc_sc[...]
        l = acc[:, dv:dv + 1]
        o_ref[...] = acc[:, :dv] / l
        lse_ref[...] = _column_as_row(m_sc[:, :1] + jnp.log(l))

    (o, lse_rows), exchanged = _call_carrying(
        body, name, (H, nq),
        in_specs=[pl.BlockSpec((None, tq, dqk), lambda h, i: (h + q_head0, i, 0)),
                  pl.BlockSpec((None, T, dqk), lambda h, i: (h, 0, 0)),
                  pl.BlockSpec((None, T, dva), lambda h, i: (h + v_head0, 0, 0))],
        out_specs=[pl.BlockSpec((None, tq, dv), lambda h, i: (h, i, 0)),
                   pl.BlockSpec((None, 1, tq), lambda h, i: (h, 0, i))],
        out_shape=[jax.ShapeDtypeStruct((H, T, dv), F32), jax.ShapeDtypeStruct((H, 1, T), F32)],
        scratch_shapes=[pltpu.VMEM((tq, 128), F32), pltpu.VMEM((tq, dva), F32)],
        operands=(q, k, v_aug), exchange=exchange)
    return (o, lse_rows.reshape(H, T)), exchanged


def _row_dot(name, a, b):
    H, T, d = a.shape
    tm = _pick(T, (1024, 512, 256, 128))

    def body(a_ref, b_ref, o_ref):
        col = jnp.sum(a_ref[...].astype(F32) * b_ref[...].astype(F32), axis=-1, keepdims=True)
        o_ref[...] = _column_as_row(col)

    slab = pl.BlockSpec((None, tm, d), lambda h, i: (h, i, 0))
    return pl.pallas_call(
        body, name=name, grid=(H, T // tm), in_specs=[slab, slab],
        out_specs=pl.BlockSpec((None, 1, tm), lambda h, i: (h, 0, i)),
        out_shape=jax.ShapeDtypeStruct((H, 1, T), F32),
        compiler_params=_cparams(("parallel", "parallel")),
    )(a, b).reshape(H, T)


def _flash_bwd(name, q, k, v, do, scale, tq, exchange=None, v_head0=0, token_major_out=False):
    H, T, dqk = q.shape
    dva = v.shape[2]
    tkb = 2 * tq if T % (2 * tq) == 0 else tq
    r = tkb // tq
    nk = T // tkb

    def body(q_ref, k_ref, v_ref, do_ref, dq_ref, dk_ref, dv_ref, dk_sc, dv_sc):
        kj = pl.program_id(1)
        dk_sc[...] = jnp.zeros_like(dk_sc)
        dv_sc[...] = jnp.zeros_like(dv_sc)

        @pl.when(kj == 0)
        def _():
            dq_ref[...] = jnp.zeros_like(dq_ref)

        def chunk(i, diag):
            rows = _chunk_rows(i, tq)
            qb = q_ref[rows, :]
            dob = do_ref[rows, :]
            live = slice(0, tkb if diag is None else (diag + 1) * tq)
            kb = k_ref[live, :]
            st = _dot(kb, qb, NT)
            if diag is not None:
                key = lax.broadcasted_iota(jnp.int32, st.shape, 0)
                qry = lax.broadcasted_iota(jnp.int32, st.shape, 1) + diag * tq
                st = jnp.where(key <= qry, st, NEG)
            pt = jnp.exp(st)
            dv_sc[live, :] += _dot(pt.astype(BF16), dob, NN)
            dst = (pt * _dot(v_ref[live, :], dob, NT)).astype(BF16)
            dk_sc[live, :] += _dot(dst, qb, NN)
            dq_ref[rows, :] += _dot(dst, kb, TN)

        def off_diagonal(i, carry):
            chunk(i, None)
            return carry

        for d in range(r):
            chunk(kj * r + d, d)
        lax.fori_loop(kj * r + r, T // tq, off_diagonal, 0)
        dk_ref[...] = dk_sc[...]
        dv_ref[...] = dv_sc[...]

        @pl.when(kj == nk - 1)
        def _():
            dq_ref[...] = dq_ref[...] * scale

    whole_q = pl.BlockSpec((None, T, dqk), lambda h, j: (h, 0, 0))
    k_spec = pl.BlockSpec((None, tkb, dqk), lambda h, j: (h, j, 0))
    v_spec = pl.BlockSpec((None, tkb, dva), lambda h, j: (h, j, 0))
    v_in_spec = pl.BlockSpec((None, tkb, dva), lambda h, j: (h + v_head0, j, 0))
    if token_major_out:
        out_specs = [pl.BlockSpec((T, dqk), lambda h, j: (0, h)), pl.BlockSpec((tkb, dqk), lambda h, j: (j, h)),
                     pl.BlockSpec((tkb, dva), lambda h, j: (j, h))]
        out_shape = [jax.ShapeDtypeStruct((T, H * dqk), F32), jax.ShapeDtypeStruct((T, H * dqk), F32),
                     jax.ShapeDtypeStruct((T, H * dva), F32)]
    else:
        out_specs = [whole_q, k_spec, v_spec]
        out_shape = [jax.ShapeDtypeStruct((H, T, dqk), F32), jax.ShapeDtypeStruct((H, T, dqk), F32),
                     jax.ShapeDtypeStruct((H, T, dva), F32)]
    return _call_carrying(
        body, name, (H, nk),
        in_specs=[whole_q, k_spec, v_in_spec, pl.BlockSpec((None, T, dva), lambda h, j: (h, 0, 0))],
        out_specs=out_specs, out_shape=out_shape,
        scratch_shapes=[pltpu.VMEM((tkb, dqk), F32), pltpu.VMEM((tkb, dva), F32)],
        operands=(q, k, v, do), exchange=exchange)


def _pack_head_grads(name, parts, W, head_dim, picks):
    T, HW = parts[0].shape
    H = HW // W
    n = len(parts)
    tm = _pick(T, (512, 256, 128))
    sels = [(jnp.arange(HW)[:, None] == (jnp.arange(128)[None, :] * W + col)).astype(F32) for _, col in picks]

    def body(*refs):
        xs = refs[:n]
        sel_refs = refs[n:n + len(picks)]
        packed_ref = refs[n + len(picks)]
        row_refs = refs[n + len(picks) + 1:]
        for a in range(n):
            x = xs[a][...]
            heads = [x[:, h * W:h * W + head_dim] for h in range(H)]
            packed_ref[:, a * H * head_dim:(a + 1) * H * head_dim] = (
                jnp.concatenate(heads, axis=-1).astype(packed_ref.dtype))
        for (a, _), s_ref, r_ref in zip(picks, sel_refs, row_refs):
            cols = lax.dot_general(xs[a][...], s_ref[...], NN, precision=lax.Precision.HIGHEST,
                                   preferred_element_type=F32)
            r_ref[...] = cols.T[:H, :]

    row = pl.BlockSpec((tm, HW), lambda i: (i, 0))
    out = pl.pallas_call(
        body, name=name, grid=(T // tm,),
        in_specs=[row] * n + [pl.BlockSpec((HW, 128), lambda i: (0, 0))] * len(picks),
        out_specs=[pl.BlockSpec((tm, n * H * head_dim), lambda i: (i, 0))]
        + [pl.BlockSpec((H, tm), lambda i: (0, i))] * len(picks),
        out_shape=[jax.ShapeDtypeStruct((T, n * H * head_dim), BF16)]
        + [jax.ShapeDtypeStruct((H, T), F32)] * len(picks),
        compiler_params=_cparams(("parallel",)),
    )(*parts, *sels)
    return out[0], out[1:]


def _adamw_math(w, g, m, v):
    m = ADAM_B1 * m + (1.0 - ADAM_B1) * g
    v = ADAM_B2 * v + (1.0 - ADAM_B2) * (g * g)
    m_hat = m / (1.0 - ADAM_B1 ** ADAM_STEP)
    v_hat = v / (1.0 - ADAM_B2 ** ADAM_STEP)
    delta = -ADAM_LR * (m_hat / (jnp.sqrt(v_hat) + ADAM_EPS) + ADAM_WD * w)
    return delta, m, v


def _adamw(name, parts, w, m, v):
    P, R, C = parts.shape
    tr = _pick(R, (256, 128, 64, 32, 16, 8))

    def body(p_ref, w_ref, m_ref, v_ref, g_out, d_out, m_out, v_out):
        g = p_ref[0].astype(F32)
        for i in range(1, P):
            g = g + p_ref[i].astype(F32)
        delta, m_new, v_new = _adamw_math(w_ref[...], g, m_ref[...], v_ref[...])
        g_out[...] = g
        d_out[...] = delta
        m_out[...] = m_new
        v_out[...] = v_new

    blk = pl.BlockSpec((tr, C), lambda i: (i, 0))
    sds = jax.ShapeDtypeStruct((R, C), F32)
    return pl.pallas_call(
        body, name=name, grid=(R // tr,),
        in_specs=[pl.BlockSpec((P, tr, C), lambda i: (0, i, 0)), blk, blk, blk],
        out_specs=[blk] * 4, out_shape=[sds] * 4,
        compiler_params=_cparams(("parallel",)),
    )(parts, w, m, v)


def _my_position():
    return lax.axis_index("x"), lax.axis_index("y"), lax.axis_index("c")


def _slot(p):
    return 4 * p[0] + 2 * p[1] + p[2]


def _flip(p, k):
    return tuple((1 - p[i]) if (k >> (2 - i)) & 1 else p[i] for i in range(3))


def _allgather_weights(shards):
    n = len(shards)

    def body(*refs):
        ins = refs[:n]
        outs = refs[n:2 * n]
        send_sems, recv_sems, local_sems = refs[2 * n:]
        x, y, c = _my_position()
        me, sibling = (x, y, c), (x, y, 1 - c)
        chips = [(1 - x, y), (x, 1 - y), (1 - x, 1 - y)]

        def copy(a, k, block, to, src=None):
            dst = outs[a].at[_slot(block)]
            return pltpu.make_async_remote_copy(
                src_ref=dst if src is None else src, dst_ref=dst,
                send_sem=send_sems.at[7 * a + k], recv_sem=recv_sems.at[7 * a + k],
                device_id=to, device_id_type=MESH)

        started = []
        for a in range(n):
            mine = pltpu.make_async_copy(ins[a], outs[a].at[_slot(me)], local_sems.at[a])
            mine.start()
            started.append(mine)
        first = []
        for a in range(n):
            first.append(copy(a, 0, me, sibling, src=ins[a]))
            first += [copy(a, 1 + j, me, (*chip, c), src=ins[a]) for j, chip in enumerate(chips)]
        for cp in first:
            cp.start()
        passed = []
        for j, chip in enumerate(chips):
            for a in range(n):
                copy(a, 1 + j, (*chip, c), me).wait_recv()
                fwd = copy(a, 4 + j, (*chip, c), sibling)
                fwd.start()
                passed.append(fwd)
        for a in range(n):
            copy(a, 0, sibling, me).wait_recv()
            for j, chip in enumerate(chips):
                copy(a, 4 + j, (*chip, 1 - c), me).wait_recv()
        for cp in first + passed:
            cp.wait_send()
        for mine in started:
            mine.wait()

    hbm = pl.BlockSpec(memory_space=pl.ANY)
    return pl.pallas_call(
        body, name="allgather_weights",
        in_specs=[hbm] * n, out_specs=[hbm] * n,
        out_shape=[jax.ShapeDtypeStruct((N_DEV,) + s.shape, s.dtype) for s in shards],
        scratch_shapes=[pltpu.SemaphoreType.DMA((7 * n,)), pltpu.SemaphoreType.DMA((7 * n,)),
                        pltpu.SemaphoreType.DMA((n,))],
        compiler_params=pltpu.CompilerParams(has_side_effects=True),
    )(*shards)


def _exchange_copies(kind, x_in, x_out, send_sems, recv_sems, local_sems, receives=True):
    me = _my_position()
    mine = _slot(me)
    local, sends, recvs = [], [], []
    for a in range(len(x_in)):
        src = x_in[a] if kind == "gather" else x_in[a].at[mine]
        local.append(pltpu.make_async_copy(src, x_out[a].at[mine], local_sems.at[a]))
    for k in range(1, N_DEV):
        peer = _flip(me, k)
        theirs = _slot(peer)
        for a in range(len(x_in)):
            src = x_in[a] if kind == "gather" else x_in[a].at[theirs]
            ends = [(x_out[a].at[mine], sends)] + ([(x_out[a].at[theirs], recvs)] if receives else [])
            for dst, group in ends:
                group.append(pltpu.make_async_remote_copy(
                    src_ref=src, dst_ref=dst, send_sem=send_sems.at[7 * a + k - 1],
                    recv_sem=recv_sems.at[7 * a + k - 1], device_id=peer, device_id_type=MESH))
    return local, sends, recvs


def _exchange_out_shapes(kind, arrays):
    return [jax.ShapeDtypeStruct(((N_DEV,) + a.shape) if kind == "gather" else a.shape, a.dtype)
            for a in arrays]


def _exchange_sems(n):
    return [pltpu.SemaphoreType.DMA((7 * n,)), pltpu.SemaphoreType.DMA((7 * n,)),
            pltpu.SemaphoreType.DMA((n,))]


def _call_carrying(body, name, grid, in_specs, out_specs, out_shape, scratch_shapes, operands, exchange,
                   sequential=False):
    if exchange is None:
        out = pl.pallas_call(
            body, name=name, grid=grid, in_specs=in_specs, out_specs=out_specs, out_shape=out_shape,
            scratch_shapes=scratch_shapes,
            compiler_params=_cparams((("arbitrary",) if sequential else ("parallel",))
                                     + ("arbitrary",) * (len(grid) - 1)),
        )(*operands)
        return out, None
    kind, arrays = exchange
    n, n_in, n_out, n_sc = len(arrays), len(in_specs), len(out_specs), len(scratch_shapes)

    def full_body(*refs):
        ins, refs = refs[:n_in], refs[n_in:]
        x_in, refs = refs[:n], refs[n:]
        outs, refs = refs[:n_out], refs[n_out:]
        x_out, refs = refs[:n], refs[n:]
        scratch, sems = refs[:n_sc], refs[n_sc:]
        first = last = None
        for axis, size in enumerate(grid):
            at_start = pl.program_id(axis) == 0
            at_end = pl.program_id(axis) == size - 1
            first = at_start if first is None else jnp.logical_and(first, at_start)
            last = at_end if last is None else jnp.logical_and(last, at_end)

        @pl.when(first)
        def _():
            local, sends, _ = _exchange_copies(kind, x_in, x_out, *sems, receives=False)
            for cp in local + sends:
                cp.start()

        body(*ins, *outs, *scratch)

        @pl.when(last)
        def _():
            local, sends, recvs = _exchange_copies(kind, x_in, x_out, *sems)
            for cp in recvs:
                cp.wait_recv()
            for cp in sends:
                cp.wait_send()
            for cp in local:
                cp.wait()

    hbm = pl.BlockSpec(memory_space=pl.ANY)
    out = pl.pallas_call(
        full_body, name=name, grid=grid,
        in_specs=list(in_specs) + [hbm] * n, out_specs=list(out_specs) + [hbm] * n,
        out_shape=list(out_shape) + _exchange_out_shapes(kind, arrays),
        scratch_shapes=list(scratch_shapes) + _exchange_sems(n),
        compiler_params=pltpu.CompilerParams(dimension_semantics=("arbitrary",) * len(grid),
                                             vmem_limit_bytes=VMEM_LIMIT_BYTES, has_side_effects=True),
    )(*operands, *arrays)
    return out[:n_out], out[n_out:]


def _allreduce_small(v):
    R, C = v.shape

    def body(v_ref, o_ref, buf, send_sems, recv_sems):
        me = _my_position()
        buf[_slot(me)] = v_ref[...]
        sends = []
        for k in range(1, N_DEV):
            peer = _flip(me, k)
            cp = pltpu.make_async_remote_copy(
                src_ref=v_ref, dst_ref=buf.at[_slot(me)],
                send_sem=send_sems.at[k - 1], recv_sem=recv_sems.at[k - 1],
                device_id=peer, device_id_type=MESH)
            cp.start()
            sends.append(cp)
        for k in range(1, N_DEV):
            peer = _flip(me, k)
            pltpu.make_async_remote_copy(
                src_ref=v_ref, dst_ref=buf.at[_slot(peer)],
                send_sem=send_sems.at[k - 1], recv_sem=recv_sems.at[k - 1],
                device_id=peer, device_id_type=MESH).wait_recv()
        for cp in sends:
            cp.wait_send()
        tot = buf[0]
        for s in range(1, N_DEV):
            tot = tot + buf[s]
        o_ref[...] = tot

    vm = pl.BlockSpec(memory_space=pltpu.VMEM)
    return pl.pallas_call(
        body, name="allreduce_small",
        in_specs=[vm], out_specs=vm, out_shape=jax.ShapeDtypeStruct((R, C), F32),
        scratch_shapes=[pltpu.VMEM((N_DEV, R, C), F32), pltpu.SemaphoreType.DMA((7,)),
                        pltpu.SemaphoreType.DMA((7,))],
        compiler_params=pltpu.CompilerParams(has_side_effects=True),
    )(v)


def _to_heads(t, heads):
    T = t.shape[0]
    return t.reshape(T, heads, t.shape[1] // heads).transpose(1, 0, 2)


def _from_heads(t):
    H, T, d = t.shape
    return t.transpose(1, 0, 2).reshape(T, H * d)


def _widen(t, width, ones_at=None, pieces_at=None, pieces=None):
    out = jnp.pad(t, ((0, 0), (0, 0), (0, width - t.shape[-1])))
    lane = lax.broadcasted_iota(jnp.int32, (1, 1, width), 2)
    if ones_at is not None:
        out = jnp.where((lane >= ones_at) & (lane < ones_at + 3), jnp.ones((), BF16), out)
    if pieces_at is not None:
        for i in range(3):
            out = jnp.where(lane == pieces_at + i, pieces[i][:, :, None], out)
    return out


def _split3(t):
    hi = lax.reduce_precision(t, 8, 7)
    r = t - hi
    mid = lax.reduce_precision(r, 8, 7)
    lo = lax.reduce_precision(r - mid, 8, 7)
    return hi.astype(BF16), mid.astype(BF16), lo.astype(BF16)


def _pad_cols(t, n):
    return jnp.pad(t, ((0, 0), (0, n - t.shape[1])))


def _pack_small(mix, ffn, kv, fin, kva, qa, bf, last):
    row6 = jnp.concatenate([kva.reshape(-1), qa.reshape(-1), bf.reshape(-1),
                            jnp.zeros((D_MODEL - KV_LORA - Q_LORA - FOX_HEADS,), F32)])
    return jnp.stack([mix[0], mix[1], ffn[0], ffn[1], kv.reshape(-1), fin.reshape(-1), row6, last])


def _unpack_small(p):
    mix = p[0:2]
    ffn = p[2:4]
    kv = p[4]
    fin = p[5]
    kva = p[6, :KV_LORA]
    qa = p[6, KV_LORA:KV_LORA + Q_LORA].reshape(1, Q_LORA)
    bf = p[6, KV_LORA + Q_LORA:KV_LORA + Q_LORA + FOX_HEADS].reshape(1, FOX_HEADS)
    return mix, ffn, bf, kv, kva, qa, fin


def _mlp_fwd(tag, xin, g, w_up, w_down):
    h = _rms(f"{tag}_norm", xin, g, BF16)

    def act(acc):
        r = jnp.maximum(acc, 0.0)
        return acc, r * r

    u, a = _mm(f"{tag}_up", h, w_up, "nn", (BF16, BF16), epi=act)
    xout = _mm(f"{tag}_down", a, w_down, "nn", (F32,), epi=lambda acc, r: (acc + r,), extras=(xin,))
    return xout, (h, u, a)


def _mlp_bwd(tag, gout, xin, g, w_up, w_down, saved):
    h, u, a = saved
    dw_down = _mm_tn(f"{tag}_dwdown", a, gout)
    du = _mm(f"{tag}_du", gout, w_down, "nt", (BF16,),
             epi=lambda acc, uu: (acc * (2.0 * jnp.maximum(uu.astype(F32), 0.0)),), extras=(u,))
    dw_up = _mm_tn(f"{tag}_dwup", h, du)
    gin, dg = _mm_rms_bwd(f"{tag}_dh_norm_bwd", du, w_up, xin, g, gout)
    return gin, dg, dw_up, dw_down


def kernel(x, norm_mix_g, norm_ffn_g, fox_w_in, fox_b_f, fox_w_out, kv_norm_g, mla_w_kv_a, mla_kv_a_norm_g, mla_w_kv_b, mla_w_q_a, mla_q_a_norm_g, mla_w_q_b, mla_w_out, ffn_w_up, ffn_w_down, final_norm_g, loss_target, m_norm_mix_g, m_norm_ffn_g, m_fox_w_in, m_fox_b_f, m_fox_w_out, m_kv_norm_g, m_mla_w_kv_a, m_mla_kv_a_norm_g, m_mla_w_kv_b, m_mla_w_q_a, m_mla_q_a_norm_g, m_mla_w_q_b, m_mla_w_out, m_ffn_w_up, m_ffn_w_down, m_final_norm_g, v_norm_mix_g, v_norm_ffn_g, v_fox_w_in, v_fox_b_f, v_fox_w_out, v_kv_norm_g, v_mla_w_kv_a, v_mla_kv_a_norm_g, v_mla_w_kv_b, v_mla_w_q_a, v_mla_q_a_norm_g, v_mla_w_q_b, v_mla_w_out, v_ffn_w_up, v_ffn_w_down, v_final_norm_g):
    T = x.shape[1]
    D = D_MODEL
    tq = 512 if T >= 2048 else 128
    x0 = x[0]
    tgt = loss_target[0]

    gat_fox = _allgather_weights([fox_w_in[0].astype(BF16), fox_w_out[0].astype(BF16)])
    later_shards = [s.astype(BF16) for s in (mla_w_kv_a, mla_w_kv_b, mla_w_q_a[0], mla_w_q_b[0],
                                             mla_w_out[0], ffn_w_up, ffn_w_down)]
    w_in = gat_fox[0].transpose(1, 0, 2).reshape(D, 3 * D + FOX_HEADS)
    w_qkv = w_in[:, :3 * D]
    w_f = _pad_cols(w_in[:, 3 * D:], 128)
    w_fo = gat_fox[1].reshape(D, D)
    g_mix0, g_mix1 = norm_mix_g[0:1], norm_mix_g[1:2]
    g_ffn0, g_ffn1 = norm_ffn_g[0:1], norm_ffn_g[1:2]
    g_kv = kv_norm_g.reshape(1, D)
    g_kva = mla_kv_a_norm_g.reshape(1, KV_LORA)
    g_qa = mla_q_a_norm_g.reshape(1, Q_LORA)
    g_fin = final_norm_g.reshape(1, D)

    inv = 1.0 / (ROPE_BASE ** (jnp.arange(0, QK_ROPE, 2, dtype=F32) / QK_ROPE))
    ang = jnp.arange(T, dtype=F32)[:, None] * inv[None, :]
    cos, sin = jnp.cos(ang), jnp.sin(ang)
    cos2 = jnp.concatenate([cos, cos], axis=-1)
    sgn_sin = jnp.concatenate([-sin, sin], axis=-1)

    h0 = _rms("l0_mix_norm", x0, g_mix0, BF16)
    fl_pad = _mm("fox_gate_logit", h0, w_f, "nn", (F32,))
    fl = fl_pad[:, :FOX_HEADS].T
    b_f = fox_b_f.reshape(FOX_HEADS, 1)
    cgate = _gate_cumsum("fox_gate_scan", fl, b_f, tq)
    fox_scale = FOX_HEAD_DIM ** -0.5
    col_scale = jnp.where(jnp.arange(3 * D) < D, fox_scale, 1.0).astype(BF16)
    tail = jnp.arange(FOX_AUG - FOX_HEAD_DIM)
    ones_q = (tail < 3).astype(F32)
    consts_k = ((tail >= 4) & (tail < 7)).astype(F32) + (tail == 3).astype(F32) * (1.0 / fox_scale)
    tails = jnp.broadcast_to(jnp.stack([ones_q, consts_k, ones_q])[:, None, None, :],
                             (3, FOX_HEADS, 1, FOX_AUG - FOX_HEAD_DIM)).reshape(3 * FOX_HEADS, 1, -1)
    qkv_h = _mm_head_slabs("fox_qkv", h0, w_qkv * col_scale, tails, FOX_HEAD_DIM, BF16)
    fk_aug = _widen(qkv_h[FOX_HEADS:2 * FOX_HEADS], FOX_AUG, pieces_at=FOX_HEAD_DIM,
                    pieces=_split3(-cgate))
    (fo, flse), gat = _flash_fwd("fox_attn", qkv_h, fk_aug, qkv_h, FOX_HEAD_DIM, tq,
                                 exchange=("gather", later_shards), q_head0=0, v_head0=2 * FOX_HEADS)
    w_kva = _pad_cols(gat[0].reshape(D, KV_LORA + QK_ROPE), KV_A_PAD)
    w_kvb_h = gat[1]
    w_qa = gat[2].reshape(D, Q_LORA)
    w_qb_h = gat[3]
    w_mo = gat[4].reshape(D, D)
    w_up = gat[5].transpose(1, 2, 0, 3).reshape(2, D, D_FF)
    w_down = gat[6].transpose(1, 0, 2, 3).reshape(2, D_FF, D)
    fctx = _from_heads(fo).astype(BF16)
    x1 = _mm("fox_out", fctx, w_fo, "nn", (F32,), epi=lambda acc, r: (acc + r,), extras=(x0,))
    x2, mlp0 = _mlp_fwd("l0_ffn", x1, g_ffn0, w_up[0], w_down[0])

    src = _rms("kv_norm", x2, g_kv, BF16)
    kva = _mm("kv_a", src, w_kva, "nn", (F32,))
    kva_lat = kva[:, :KV_LORA]
    c_kv = _rms("kv_a_norm", kva_lat, g_kva, BF16)
    k_rope = _rope("k_rope", kva[:, KV_LORA:KV_LORA + QK_ROPE][None], cos2, sgn_sin, BF16)
    kvb_h = _mm_heads("kv_b", c_kv, w_kvb_h, BF16)
    mk = jnp.concatenate([kvb_h[:, :, :QK_NOPE],
                          jnp.broadcast_to(k_rope, (MLA_HEADS, T, QK_ROPE))], axis=-1)
    mv = kvb_h[:, :, QK_NOPE:]

    h1 = _rms("l1_mix_norm", x2, g_mix1, BF16)
    qa = _mm("q_a", h1, w_qa, "nn", (F32,))
    c_q = _rms("q_a_norm", qa, g_qa, BF16)
    mla_scale = (QK_NOPE + QK_ROPE) ** -0.5
    mq = _mla_q_proj("q_b", c_q, w_qb_h, cos2, sgn_sin, mla_scale)
    mv_aug = _widen(mv, MLA_AUG, ones_at=V_HEAD)
    (mo, mlse), _ = _flash_fwd("mla_attn", mq, mk, mv_aug, V_HEAD, tq)
    mctx = _from_heads(mo).astype(BF16)
    x3 = _mm("mla_out", mctx, w_mo, "nn", (F32,), epi=lambda acc, r: (acc + r,), extras=(x2,))
    x4, mlp1 = _mlp_fwd("l1_ffn", x3, g_ffn1, w_up[1], w_down[1])

    g4, dg_fin, loss_vec = _loss_head("loss_head", x4, g_fin, tgt)

    g3, dg_ffn1, dw_up1, dw_down1 = _mlp_bwd("l1_ffn", g4, x3, g_ffn1, w_up[1], w_down[1], mlp1)

    dw_mo = _mm_tn("mla_out_dw", mctx, g3)
    dmo = _to_heads(_mm("mla_out_dx", g3, w_mo, "nt", (BF16,)), MLA_HEADS)
    mdelta = _row_dot("mla_delta", mo, dmo)
    dqk = QK_NOPE + QK_ROPE
    mq_bwd = _widen(mq, MLA_AUG, pieces_at=dqk, pieces=_split3(-mlse))
    mk_bwd = _widen(mk, MLA_AUG, ones_at=dqk)
    mdo_aug = _widen(dmo, MLA_AUG, pieces_at=V_HEAD, pieces=_split3(-mdelta))
    (mdq, mdk, mdv), _ = _flash_bwd("mla_attn_bwd", mq_bwd, mk_bwd, mv_aug, mdo_aug, mla_scale, tq)
    mdq = mdq[:, :, :dqk]
    mdk = mdk[:, :, :dqk]
    mdv = mdv[:, :, :V_HEAD]
    dq_rope = _rope_bwd("q_rope_bwd", mdq[:, :, QK_NOPE:], cos2, sgn_sin, False)
    dqf_h = jnp.concatenate([mdq[:, :, :QK_NOPE], dq_rope], axis=-1)
    dw_qb_h = _mm_heads_dw("q_b_dw", c_q, dqf_h)
    dc_q = _mm_heads_dx("q_b_dx", dqf_h, w_qb_h)
    dqa, dg_qa = _rms_bwd("q_a_norm_bwd", qa, g_qa, dc_q)
    dw_qa = _mm_tn("q_a_dw", h1, dqa)
    g2a, dg_mix1 = _mm_rms_bwd("q_a_dx_norm_bwd", dqa, w_qa, x2, g_mix1, g3)

    dk_rope = _rope_bwd("k_rope_bwd", mdk[:, :, QK_NOPE:], cos2, sgn_sin, True)
    dkvb_h = jnp.concatenate([mdk[:, :, :QK_NOPE], mdv], axis=-1)
    dw_kvb_h = _mm_heads_dw("kv_b_dw", c_kv, dkvb_h)
    dc_kv = _mm_heads_dx("kv_b_dx", dkvb_h, w_kvb_h)
    dkva_lat, dg_kva = _rms_bwd("kv_a_norm_bwd", kva_lat, g_kva, dc_kv)
    dkva = _pad_cols(jnp.concatenate([dkva_lat, dk_rope], axis=-1), KV_A_PAD)
    dw_kva = _mm_tn("kv_a_dw", src, dkva)[:, :KV_LORA + QK_ROPE]
    g2, dg_kv = _mm_rms_bwd("kv_a_dx_norm_bwd", dkva, w_kva, x2, g_kv, g2a)

    g1, dg_ffn0, dw_up0, dw_down0 = _mlp_bwd("l0_ffn", g2, x1, g_ffn0, w_up[0], w_down[0], mlp0)

    dw_fo = _mm_tn("fox_out_dw", fctx, g1)
    dfo = _to_heads(_mm("fox_out_dx", g1, w_fo, "nt", (BF16,)), FOX_HEADS)
    fdelta = _row_dot("fox_delta", fo, dfo)
    fq_bwd = _widen(qkv_h[:FOX_HEADS], FOX_AUG, pieces_at=FOX_HEAD_DIM + 4, pieces=_split3(-flse))
    fdo_aug = _widen(dfo, FOX_AUG, pieces_at=FOX_HEAD_DIM, pieces=_split3(-fdelta))
    dw_up = jnp.stack([dw_up0, dw_up1])
    dw_down = jnp.stack([dw_down0, dw_down1])
    early = [
        dw_fo.reshape(N_DEV, D // N_DEV, D),
        dw_kva.reshape(N_DEV, D // N_DEV, KV_LORA + QK_ROPE),
        dw_kvb_h,
        dw_qa.reshape(N_DEV, D // N_DEV, Q_LORA),
        dw_qb_h,
        dw_mo.reshape(N_DEV, D // N_DEV, D),
        dw_up.reshape(2, D, N_DEV, -1).transpose(2, 0, 1, 3),
        dw_down.reshape(2, N_DEV, D_FF // N_DEV, D).transpose(1, 0, 2, 3),
    ]
    fd_aug, early_parts = _flash_bwd(
        "fox_attn_bwd", fq_bwd, fk_aug, qkv_h, fdo_aug, fox_scale, tq,
        exchange=("scatter", [g.astype(BF16) for g in early]), v_head0=2 * FOX_HEADS, token_major_out=True)
    dqkv, (ds_rows, ds_cols) = _pack_head_grads("fox_grad_pack", list(fd_aug), FOX_AUG, FOX_HEAD_DIM,
                                                picks=((0, FOX_HEAD_DIM + 3), (1, FOX_HEAD_DIM)))
    dfl, db_f = _gate_cumsum_bwd("fox_gate_scan_bwd", ds_rows, ds_cols, fl, b_f, tq)
    dfl_pad = _pad_cols(dfl.T, 128)
    dw_qkv = _mm_tn("fox_qkv_dw", h0, dqkv)
    dw_f = _mm_tn("fox_gate_dw", h0, dfl_pad)[:, :FOX_HEADS]
    dw_in = jnp.concatenate([dw_qkv, dw_f], axis=-1)
    dh0a = _mm("fox_gate_dx", dfl_pad, w_f, "nt", (F32,))
    late = dw_in.reshape(D, N_DEV, -1).transpose(1, 0, 2).astype(BF16)
    grad_x, dg_mix0, late_parts = _mm_rms_bwd("fox_qkv_dx_norm_bwd", dqkv, w_qkv, x0, g_mix0, g1, add=dh0a,
                                              exchange=("scatter", [late]))

    parts = list(late_parts) + list(early_parts)

    names = ["fox_w_in", "fox_w_out", "mla_w_kv_a", "mla_w_kv_b", "mla_w_q_a", "mla_w_q_b",
             "mla_w_out", "ffn_w_up", "ffn_w_down"]
    moms = [m_fox_w_in, m_fox_w_out, m_mla_w_kv_a, m_mla_w_kv_b, m_mla_w_q_a, m_mla_w_q_b,
            m_mla_w_out, m_ffn_w_up, m_ffn_w_down]
    vars_ = [v_fox_w_in, v_fox_w_out, v_mla_w_kv_a, v_mla_w_kv_b, v_mla_w_q_a, v_mla_w_q_b,
             v_mla_w_out, v_ffn_w_up, v_ffn_w_down]
    full = [fox_w_in, fox_w_out, mla_w_kv_a, mla_w_kv_b, mla_w_q_a, mla_w_q_b, mla_w_out,
            ffn_w_up, ffn_w_down]
    big = {}
    for nm, p, w, m, v in zip(names, parts, full, moms, vars_):
        C = w.shape[-1]
        res = _adamw(f"adamw_{nm}", p.reshape(N_DEV, -1, C), w.reshape(-1, C), m.reshape(-1, C),
                     v.reshape(-1, C))
        big[nm] = [r.reshape(w.shape) for r in res]

    zrow = jnp.zeros((D,), F32)
    g_small = _pack_small(jnp.concatenate([dg_mix0, dg_mix1]), jnp.concatenate([dg_ffn0, dg_ffn1]),
                          dg_kv, dg_fin, dg_kva, dg_qa, db_f, zrow.at[0].set(loss_vec[0, 0]))
    tot_small = _allreduce_small(g_small)
    w_small = _pack_small(norm_mix_g, norm_ffn_g, kv_norm_g, final_norm_g, mla_kv_a_norm_g,
                          mla_q_a_norm_g, fox_b_f, zrow)
    m_small = _pack_small(m_norm_mix_g, m_norm_ffn_g, m_kv_norm_g, m_final_norm_g, m_mla_kv_a_norm_g,
                          m_mla_q_a_norm_g, m_fox_b_f, zrow)
    v_small = _pack_small(v_norm_mix_g, v_norm_ffn_g, v_kv_norm_g, v_final_norm_g, v_mla_kv_a_norm_g,
                          v_mla_q_a_norm_g, v_fox_b_f, zrow)
    small = _adamw("adamw_small", tot_small[None], w_small, m_small, v_small)
    loss = tot_small[7, 0]
    small = [_unpack_small(s) for s in small]

    def ordered(i):
        mix, ffn, bf, kv, kva, qa, fin = small[i]
        return [mix, ffn, big["fox_w_in"][i], bf, big["fox_w_out"][i], kv, big["mla_w_kv_a"][i], kva,
                big["mla_w_kv_b"][i], big["mla_w_q_a"][i], qa, big["mla_w_q_b"][i],
                big["mla_w_out"][i], big["ffn_w_up"][i], big["ffn_w_down"][i], fin]

    return (loss, grad_x[None], *ordered(0), *ordered(1), *ordered(2), *ordered(3))
```

```python
import functools
import math

import jax
import jax.numpy as jnp
from jax import lax
from jax.experimental import pallas as pl
from jax.experimental.pallas import tpu as pltpu

F32 = jnp.float32
BF16 = jnp.bfloat16
MESH = pl.DeviceIdType.MESH

N_DEV = 8
D_MODEL = 1024
FOX_HEADS = 16
FOX_HEAD_DIM = 64
FOX_AUG = 128
MLA_AUG = 256
MLA_HEADS = 8
QK_NOPE = 128
QK_ROPE = 64
V_HEAD = 128
Q_LORA = 384
KV_LORA = 256
KV_A_PAD = 384
D_FF = 4096
ROPE_BASE = 10000.0
EPS = 1e-6
NEG = -1e30

ADAM_LR = 0.001
ADAM_B1 = 0.9
ADAM_B2 = 0.999
ADAM_EPS = 1e-08
ADAM_WD = 0.01
ADAM_STEP = 10

VMEM_LIMIT_BYTES = 56 * 1024 * 1024
MAX_KEY_BLOCK_ELEMS = 2048 * 128

NN = (((1,), (0,)), ((), ()))
NT = (((1,), (1,)), ((), ()))
TN = (((0,), (0,)), ((), ()))
_FORMS = {"nn": NN, "nt": NT}


def _cparams(sem=None):
    return pltpu.CompilerParams(dimension_semantics=sem, vmem_limit_bytes=VMEM_LIMIT_BYTES)


def _pick(n, cands):
    for c in cands:
        if c <= n and n % c == 0:
            return c
    return n


def _dot(a, b, dims):
    return lax.dot_general(a, b, dims, preferred_element_type=F32)


def _mm(name, a, b, form, out_dtypes, epi=None, extras=(), tm=1024, tn=None):
    M, K = a.shape
    N = b.shape[1] if form == "nn" else b.shape[0]
    tm = _pick(M, (tm, 512, 256, 128))
    tn = _pick(N, (tn or (1024 if K <= 1024 else 512), 512, 384, 256, 128))
    n_ex = len(extras)
    n_out = len(out_dtypes)
    cast_once = a.dtype != BF16

    def body(*refs):
        a_ref, b_ref = refs[0], refs[1]
        ex = refs[2:2 + n_ex]
        outs = refs[2 + n_ex:2 + n_ex + n_out]
        if cast_once:
            a_sc = refs[2 + n_ex + n_out]

            @pl.when(pl.program_id(1) == 0)
            def _():
                a_sc[...] = a_ref[...].astype(BF16)

            av = a_sc[...]
        else:
            av = a_ref[...]
        acc = _dot(av, b_ref[...].astype(BF16), _FORMS[form])
        res = epi(acc, *[e[...] for e in ex]) if epi is not None else (acc,)
        for o_ref, r in zip(outs, res):
            o_ref[...] = r.astype(o_ref.dtype)

    if form == "nn":
        b_spec = pl.BlockSpec((K, tn), lambda i, j: (0, j))
    else:
        b_spec = pl.BlockSpec((tn, K), lambda i, j: (j, 0))
    tile = pl.BlockSpec((tm, tn), lambda i, j: (i, j))
    out = pl.pallas_call(
        body, name=name, grid=(M // tm, N // tn),
        in_specs=[pl.BlockSpec((tm, K), lambda i, j: (i, 0)), b_spec] + [tile] * n_ex,
        out_specs=[tile] * n_out,
        out_shape=[jax.ShapeDtypeStruct((M, N), dt) for dt in out_dtypes],
        scratch_shapes=[pltpu.VMEM((tm, K), BF16)] if cast_once else [],
        compiler_params=_cparams(("parallel", "arbitrary")),
    )(a, b, *extras)
    return out if n_out > 1 else out[0]


def _mm_tn(name, a, b):
    T, Ka = a.shape
    N = b.shape[1]
    tk = _pick(Ka, (1024, 512, 384, 256, 128))
    tn = _pick(N, (1024, 768, 512, 384, 256, 128))
    tt = _pick(T, (1024, 512, 256, 128))

    def body(a_ref, b_ref, o_ref):
        @pl.when(pl.program_id(2) == 0)
        def _():
            o_ref[...] = jnp.zeros_like(o_ref)

        o_ref[...] += _dot(a_ref[...].astype(BF16), b_ref[...].astype(BF16), TN)

    return pl.pallas_call(
        body, name=name, grid=(Ka // tk, N // tn, T // tt),
        in_specs=[pl.BlockSpec((tt, tk), lambda i, j, t: (t, i)),
                  pl.BlockSpec((tt, tn), lambda i, j, t: (t, j))],
        out_specs=pl.BlockSpec((tk, tn), lambda i, j, t: (i, j)),
        out_shape=jax.ShapeDtypeStruct((Ka, N), F32),
        compiler_params=_cparams(("parallel", "parallel", "arbitrary")),
    )(a, b)


def _mm_heads(name, a, w, out_dtype):
    T, K = a.shape
    H, _, N = w.shape
    tm = _pick(T, (1024, 512, 256, 128))

    def body(a_ref, w_ref, o_ref):
        av = a_ref[...].astype(BF16)
        for s in range(H):
            o_ref[s] = _dot(av, w_ref[s].astype(BF16), NN).astype(o_ref.dtype)

    return pl.pallas_call(
        body, name=name, grid=(T // tm,),
        in_specs=[pl.BlockSpec((tm, K), lambda i: (i, 0)), pl.BlockSpec((H, K, N), lambda i: (0, 0, 0))],
        out_specs=pl.BlockSpec((H, tm, N), lambda i: (0, i, 0)),
        out_shape=jax.ShapeDtypeStruct((H, T, N), out_dtype),
        compiler_params=_cparams(("parallel",)),
    )(a, w)


def _mm_head_slabs(name, a, w, tails, head_dim, out_dtype, heads_per_step=8):
    T, K = a.shape
    S, _, tail = tails.shape
    hb = heads_per_step
    tm = _pick(T, (1024, 512, 256, 128))

    def body(a_ref, w_ref, t_ref, o_ref):
        acc = _dot(a_ref[...].astype(BF16), w_ref[...].astype(BF16), NN)
        for s in range(hb):
            slab = jnp.concatenate([acc[:, s * head_dim:(s + 1) * head_dim],
                                    jnp.broadcast_to(t_ref[s], (tm, tail))], axis=-1)
            o_ref[s] = slab.astype(o_ref.dtype)

    return pl.pallas_call(
        body, name=name, grid=(T // tm, S // hb),
        in_specs=[pl.BlockSpec((tm, K), lambda i, j: (i, 0)),
                  pl.BlockSpec((K, hb * head_dim), lambda i, j: (0, j)),
                  pl.BlockSpec((hb, 1, tail), lambda i, j: (j, 0, 0))],
        out_specs=pl.BlockSpec((hb, tm, head_dim + tail), lambda i, j: (j, i, 0)),
        out_shape=jax.ShapeDtypeStruct((S, T, head_dim + tail), out_dtype),
        compiler_params=_cparams(("parallel", "arbitrary")),
    )(a, w, tails)


def _mm_heads_dw(name, a, g):
    T, K = a.shape
    H, _, N = g.shape
    tt = _pick(T, (1024, 512, 256, 128))

    def body(a_ref, g_ref, o_ref):
        @pl.when(pl.program_id(0) == 0)
        def _():
            o_ref[...] = jnp.zeros_like(o_ref)

        av = a_ref[...].astype(BF16)
        for s in range(H):
            o_ref[s] += _dot(av, g_ref[s].astype(BF16), TN)

    return pl.pallas_call(
        body, name=name, grid=(T // tt,),
        in_specs=[pl.BlockSpec((tt, K), lambda t: (t, 0)), pl.BlockSpec((H, tt, N), lambda t: (0, t, 0))],
        out_specs=pl.BlockSpec((H, K, N), lambda t: (0, 0, 0)),
        out_shape=jax.ShapeDtypeStruct((H, K, N), F32),
        compiler_params=_cparams(("arbitrary",)),
    )(a, g)


def _mm_heads_dx(name, g, w):
    H, T, N = g.shape
    K = w.shape[1]
    tm = _pick(T, (512, 256, 128))

    def body(g_ref, w_ref, o_ref):
        acc = _dot(g_ref[0].astype(BF16), w_ref[0].astype(BF16), NT)
        for s in range(1, H):
            acc = acc + _dot(g_ref[s].astype(BF16), w_ref[s].astype(BF16), NT)
        o_ref[...] = acc

    return pl.pallas_call(
        body, name=name, grid=(T // tm,),
        in_specs=[pl.BlockSpec((H, tm, N), lambda i: (0, i, 0)), pl.BlockSpec((H, K, N), lambda i: (0, 0, 0))],
        out_specs=pl.BlockSpec((tm, K), lambda i: (i, 0)),
        out_shape=jax.ShapeDtypeStruct((T, K), F32),
        compiler_params=_cparams(("parallel",)),
    )(g, w)


def _rms(name, x, g, out_dtype):
    T, D = x.shape
    tm = _pick(T, (1024, 512, 256, 128))

    def body(x_ref, g_ref, o_ref):
        xf = x_ref[...]
        r = lax.rsqrt(jnp.mean(xf * xf, axis=-1, keepdims=True) + EPS)
        o_ref[...] = (xf * r * g_ref[...]).astype(o_ref.dtype)

    return pl.pallas_call(
        body, name=name, grid=(T // tm,),
        in_specs=[pl.BlockSpec((tm, D), lambda i: (i, 0)), pl.BlockSpec((1, D), lambda i: (0, 0))],
        out_specs=pl.BlockSpec((tm, D), lambda i: (i, 0)),
        out_shape=jax.ShapeDtypeStruct((T, D), out_dtype),
        compiler_params=_cparams(("parallel",)),
    )(x, g)


def _rms_bwd(name, x, g, dh, dres=None):
    T, D = x.shape
    tm = _pick(T, (512, 256, 128))
    has_res = dres is not None

    def body(*refs):
        if has_res:
            x_ref, g_ref, dh_ref, dres_ref, dx_ref, dg_ref = refs
        else:
            x_ref, g_ref, dh_ref, dx_ref, dg_ref = refs

        @pl.when(pl.program_id(0) == 0)
        def _():
            dg_ref[...] = jnp.zeros_like(dg_ref)

        xf = x_ref[...]
        r = lax.rsqrt(jnp.mean(xf * xf, axis=-1, keepdims=True) + EPS)
        xhat = xf * r
        dy = dh_ref[...].astype(F32)
        dxh = dy * g_ref[...]
        dx = r * (dxh - xhat * jnp.mean(dxh * xhat, axis=-1, keepdims=True))
        if has_res:
            dx = dx + dres_ref[...]
        dx_ref[...] = dx
        dg_ref[...] += jnp.sum(dy * xhat, axis=0, keepdims=True)

    row = pl.BlockSpec((tm, D), lambda i: (i, 0))
    vec = pl.BlockSpec((1, D), lambda i: (0, 0))
    ins = [x, g, dh] + ([dres] if has_res else [])
    return pl.pallas_call(
        body, name=name, grid=(T // tm,),
        in_specs=[row, vec, row] + ([row] if has_res else []),
        out_specs=[row, vec],
        out_shape=[jax.ShapeDtypeStruct((T, D), F32), jax.ShapeDtypeStruct((1, D), F32)],
        compiler_params=_cparams(("arbitrary",)),
    )(*ins)


def _mm_rms_bwd(name, a, b, x, g, dres, add=None, exchange=None):
    T, K = a.shape
    D = b.shape[0]
    tm = _pick(T, (512, 256, 128))
    has_add = add is not None

    def body(*refs):
        a_ref, b_ref, x_ref, g_ref, dres_ref = refs[:5]
        dx_ref, dg_ref = refs[-2:]

        @pl.when(pl.program_id(0) == 0)
        def _():
            dg_ref[...] = jnp.zeros_like(dg_ref)

        dy = _dot(a_ref[...].astype(BF16), b_ref[...].astype(BF16), NT)
        if has_add:
            dy = dy + refs[5][...]
        xf = x_ref[...]
        r = lax.rsqrt(jnp.mean(xf * xf, axis=-1, keepdims=True) + EPS)
        xhat = xf * r
        dxh = dy * g_ref[...]
        dx_ref[...] = r * (dxh - xhat * jnp.mean(dxh * xhat, axis=-1, keepdims=True)) + dres_ref[...]
        dg_ref[...] += jnp.sum(dy * xhat, axis=0, keepdims=True)

    row = pl.BlockSpec((tm, D), lambda i: (i, 0))
    vec = pl.BlockSpec((1, D), lambda i: (0, 0))
    (dx, dg), exchanged = _call_carrying(
        body, name, (T // tm,),
        in_specs=[pl.BlockSpec((tm, K), lambda i: (i, 0)), pl.BlockSpec((D, K), lambda i: (0, 0)), row, vec, row]
        + ([row] if has_add else []),
        out_specs=[row, vec],
        out_shape=[jax.ShapeDtypeStruct((T, D), F32), jax.ShapeDtypeStruct((1, D), F32)],
        scratch_shapes=[], operands=(a, b, x, g, dres) + ((add,) if has_add else ()), exchange=exchange,
        sequential=True)
    return (dx, dg) if exchange is None else (dx, dg, exchanged)


def _loss_head(name, x, g, tgt):
    T, D = x.shape
    tm = _pick(T, (512, 256, 128))

    def body(x_ref, g_ref, t_ref, dx_ref, dg_ref, loss_ref):
        @pl.when(pl.program_id(0) == 0)
        def _():
            dg_ref[...] = jnp.zeros_like(dg_ref)
            loss_ref[...] = jnp.zeros_like(loss_ref)

        xf = x_ref[...]
        r = lax.rsqrt(jnp.mean(xf * xf, axis=-1, keepdims=True) + EPS)
        xhat = xf * r
        gv = g_ref[...]
        err = xhat * gv - t_ref[...]
        row_loss = jnp.mean(err * err, axis=-1, keepdims=True)
        loss_ref[...] += 0.5 * jnp.sum(row_loss, axis=0, keepdims=True)
        dy = err * (1.0 / D)
        dxh = dy * gv
        dx_ref[...] = r * (dxh - xhat * jnp.mean(dxh * xhat, axis=-1, keepdims=True))
        dg_ref[...] += jnp.sum(dy * xhat, axis=0, keepdims=True)

    row = pl.BlockSpec((tm, D), lambda i: (i, 0))
    vec = pl.BlockSpec((1, D), lambda i: (0, 0))
    return pl.pallas_call(
        body, name=name, grid=(T // tm,),
        in_specs=[row, vec, row],
        out_specs=[row, vec, pl.BlockSpec((1, 128), lambda i: (0, 0))],
        out_shape=[jax.ShapeDtypeStruct((T, D), F32), jax.ShapeDtypeStruct((1, D), F32),
                   jax.ShapeDtypeStruct((1, 128), F32)],
        compiler_params=_cparams(("arbitrary",)),
    )(x, g, tgt)


def _swap_halves(t):
    half = t.shape[-1] // 2
    return jnp.concatenate([t[:, half:], t[:, :half]], axis=-1)


def _rope(name, t, cos2, sgn_sin, out_dtype):
    H, T, R = t.shape
    tm = _pick(T, (1024, 512, 256, 128))

    def body(t_ref, c_ref, s_ref, o_ref):
        tf = t_ref[...].astype(F32)
        o_ref[...] = (tf * c_ref[...] + _swap_halves(tf) * s_ref[...]).astype(o_ref.dtype)

    slab = pl.BlockSpec((None, tm, R), lambda h, i: (h, i, 0))
    tab = pl.BlockSpec((tm, R), lambda h, i: (i, 0))
    return pl.pallas_call(
        body, name=name, grid=(H, T // tm),
        in_specs=[slab, tab, tab], out_specs=slab,
        out_shape=jax.ShapeDtypeStruct((H, T, R), out_dtype),
        compiler_params=_cparams(("parallel", "parallel")),
    )(t, cos2, sgn_sin)


def _mla_q_proj(name, a, w, cos2, sgn_sin, scale):
    T, K = a.shape
    H, _, W = w.shape
    R = cos2.shape[1]
    tm = _pick(T, (1024, 512, 256, 128))

    def body(a_ref, w_ref, c_ref, s_ref, o_ref):
        av = a_ref[...].astype(BF16)
        for h in range(H):
            qf = _dot(av, w_ref[h].astype(BF16), NN)
            r = qf[:, W - R:]
            roped = r * c_ref[...] + _swap_halves(r) * s_ref[...]
            o_ref[h] = (jnp.concatenate([qf[:, :W - R], roped], axis=-1) * scale).astype(o_ref.dtype)

    tab = pl.BlockSpec((tm, R), lambda i: (i, 0))
    return pl.pallas_call(
        body, name=name, grid=(T // tm,),
        in_specs=[pl.BlockSpec((tm, K), lambda i: (i, 0)), pl.BlockSpec((H, K, W), lambda i: (0, 0, 0)), tab, tab],
        out_specs=pl.BlockSpec((H, tm, W), lambda i: (0, i, 0)),
        out_shape=jax.ShapeDtypeStruct((H, T, W), BF16),
        compiler_params=_cparams(("parallel",)),
    )(a, w, cos2, sgn_sin)


def _rope_bwd(name, dy, cos2, sgn_sin, sum_heads):
    H, T, R = dy.shape
    tm = _pick(T, (1024, 512, 256, 128))

    def body(d_ref, c_ref, s_ref, o_ref):
        d = d_ref[...]
        if sum_heads:
            tot = d[0]
            for h in range(1, H):
                tot = tot + d[h]
            d = tot
        o_ref[...] = d * c_ref[...] + _swap_halves(d * s_ref[...])

    if sum_heads:
        grid = (T // tm,)
        in_slab = pl.BlockSpec((H, tm, R), lambda i: (0, i, 0))
        out_slab = pl.BlockSpec((tm, R), lambda i: (i, 0))
        tab = pl.BlockSpec((tm, R), lambda i: (i, 0))
        out_shape = jax.ShapeDtypeStruct((T, R), F32)
        sem = ("parallel",)
    else:
        grid = (H, T // tm)
        in_slab = pl.BlockSpec((None, tm, R), lambda h, i: (h, i, 0))
        out_slab = in_slab
        tab = pl.BlockSpec((tm, R), lambda h, i: (i, 0))
        out_shape = jax.ShapeDtypeStruct((H, T, R), F32)
        sem = ("parallel", "parallel")
    return pl.pallas_call(
        body, name=name, grid=grid, in_specs=[in_slab, tab, tab], out_specs=out_slab,
        out_shape=out_shape, compiler_params=_cparams(sem),
    )(dy, cos2, sgn_sin)


def _log_sigmoid(z):
    return jnp.minimum(z, 0.0) - jnp.log(1.0 + jnp.exp(-jnp.abs(z)))


def _gate_cumsum(name, fl, b, tb):
    H, T = fl.shape

    def body(f_ref, b_ref, c_ref, carry):
        @pl.when(pl.program_id(0) == 0)
        def _():
            carry[...] = jnp.zeros_like(carry)

        ls = _log_sigmoid(f_ref[...] + b_ref[...])
        src = lax.broadcasted_iota(jnp.int32, (tb, tb), 0)
        dst = lax.broadcasted_iota(jnp.int32, (tb, tb), 1)
        tri = (src <= dst).astype(F32)
        c = lax.dot_general(ls, tri, NN, precision=lax.Precision.HIGHEST,
                            preferred_element_type=F32) + carry[...]
        c_ref[...] = c
        carry[...] = carry[...] + jnp.sum(ls, axis=-1, keepdims=True)

    return pl.pallas_call(
        body, name=name, grid=(T // tb,),
        in_specs=[pl.BlockSpec((H, tb), lambda i: (0, i)), pl.BlockSpec((H, 1), lambda i: (0, 0))],
        out_specs=pl.BlockSpec((H, tb), lambda i: (0, i)),
        out_shape=jax.ShapeDtypeStruct((H, T), F32),
        scratch_shapes=[pltpu.VMEM((H, 1), F32)],
        compiler_params=_cparams(("arbitrary",)),
    )(fl, b)


def _gate_cumsum_bwd(name, d_query, d_key, fl, b, tb):
    H, T = fl.shape
    nb = T // tb

    def body(dq_ref, dk_ref, f_ref, b_ref, dfl_ref, db_ref, carry):
        @pl.when(pl.program_id(0) == 0)
        def _():
            carry[...] = jnp.zeros_like(carry)
            db_ref[...] = jnp.zeros_like(db_ref)

        d = dq_ref[...] - dk_ref[...]
        src = lax.broadcasted_iota(jnp.int32, (tb, tb), 0)
        dst = lax.broadcasted_iota(jnp.int32, (tb, tb), 1)
        tri = (src >= dst).astype(F32)
        dls = lax.dot_general(d, tri, NN, precision=lax.Precision.HIGHEST,
                              preferred_element_type=F32) + carry[...]
        z = f_ref[...] + b_ref[...]
        dfl = dls * (1.0 / (1.0 + jnp.exp(z)))
        dfl_ref[...] = dfl
        db_ref[...] += jnp.sum(dfl, axis=-1, keepdims=True)
        carry[...] = carry[...] + jnp.sum(d, axis=-1, keepdims=True)

    blk = pl.BlockSpec((H, tb), lambda i: (0, nb - 1 - i))
    vec = pl.BlockSpec((H, 1), lambda i: (0, 0))
    return pl.pallas_call(
        body, name=name, grid=(nb,),
        in_specs=[blk, blk, blk, vec], out_specs=[blk, vec],
        out_shape=[jax.ShapeDtypeStruct((H, T), F32), jax.ShapeDtypeStruct((H, 1), F32)],
        scratch_shapes=[pltpu.VMEM((H, 1), F32)],
        compiler_params=_cparams(("arbitrary",)),
    )(d_query, d_key, fl, b)


def _chunk_rows(j, tq):
    return pl.ds(pl.multiple_of(j * tq, tq), tq)


def _column_as_row(col):
    return jnp.broadcast_to(col, (col.shape[0], 128)).T[:1, :]


def _flash_fwd(name, q, k, v_aug, dv, tq, exchange=None, q_head0=0, v_head0=0):
    H, T, dqk = k.shape
    dva = v_aug.shape[2]
    tk = tq
    tq = next(m * tk for m in (4, 2, 1) if T % (m * tk) == 0)
    r = tq // tk
    nq = T // tq

    def body(q_ref, k_ref, v_ref, o_ref, lse_ref, m_sc, acc_sc):
        qi = pl.program_id(1)
        m_sc[...] = jnp.full_like(m_sc, NEG)
        acc_sc[...] = jnp.zeros_like(acc_sc)

        def chunk(j, diag):
            rows = _chunk_rows(j, tk)
            live = slice(0 if diag is None else diag * tk, tq)
            n_live = tq - live.start
            s = _dot(q_ref[live, :], k_ref[rows, :], NT)
            if diag is not None:
                row = lax.broadcasted_iota(jnp.int32, (n_live, tk), 0)
                col = lax.broadcasted_iota(jnp.int32, (n_live, tk), 1)
                s = jnp.where(col <= row, s, NEG)
            m_prev = m_sc[live, :]
            m_new = jnp.maximum(m_prev, jnp.max(s, axis=1, keepdims=True))
            p = jnp.exp(s - jnp.tile(m_new, (1, tk // 128)))
            alpha = jnp.tile(jnp.exp(m_prev - m_new), (1, dva // 128))
            acc_sc[live, :] = alpha * acc_sc[live, :] + _dot(p.astype(BF16), v_ref[rows, :], NN)
            m_sc[live, :] = m_new

        def off_diagonal(j, carry):
            chunk(j, None)
            return carry

        lax.fori_loop(0, qi * r, off_diagonal, 0)
        for d in range(r):
            chunk(qi * r + d, d)
        acc = acc_sc[...]
        l = acc[:, dv:dv + 1]
        o_ref[...] = acc[:, :dv] / l
        lse_ref[...] = _column_as_row(m_sc[:, :1] + jnp.log(l))

    (o, lse_rows), exchanged = _call_carrying(
        body, name, (H, nq),
        in_specs=[pl.BlockSpec((None, tq, dqk), lambda h, i: (h + q_head0, i, 0)),
                  pl.BlockSpec((None, T, dqk), lambda h, i: (h, 0, 0)),
                  pl.BlockSpec((None, T, dva), lambda h, i: (h + v_head0, 0, 0))],
        out_specs=[pl.BlockSpec((None, tq, dv), lambda h, i: (h, i, 0)),
                   pl.BlockSpec((None, 1, tq), lambda h, i: (h, 0, i))],
        out_shape=[jax.ShapeDtypeStruct((H, T, dv), F32), jax.ShapeDtypeStruct((H, 1, T), F32)],
        scratch_shapes=[pltpu.VMEM((tq, 128), F32), pltpu.VMEM((tq, dva), F32)],
        operands=(q, k, v_aug), exchange=exchange)
    return (o, lse_rows.reshape(H, T)), exchanged


def _row_dot(name, a, b):
    H, T, d = a.shape
    tm = _pick(T, (1024, 512, 256, 128))

    def body(a_ref, b_ref, o_ref):
        col = jnp.sum(a_ref[...].astype(F32) * b_ref[...].astype(F32), axis=-1, keepdims=True)
        o_ref[...] = _column_as_row(col)

    slab = pl.BlockSpec((None, tm, d), lambda h, i: (h, i, 0))
    return pl.pallas_call(
        body, name=name, grid=(H, T // tm), in_specs=[slab, slab],
        out_specs=pl.BlockSpec((None, 1, tm), lambda h, i: (h, 0, i)),
        out_shape=jax.ShapeDtypeStruct((H, 1, T), F32),
        compiler_params=_cparams(("parallel", "parallel")),
    )(a, b).reshape(H, T)


def _flash_bwd(name, q, k, v, do, scale, tq, exchange=None, v_head0=0, token_major_out=False):
    H, T, dqk = q.shape
    dva = v.shape[2]
    tkb = next(m * tq for m in (4, 2, 1)
               if T % (m * tq) == 0 and (m == 1 or m * tq * max(dqk, dva) <= MAX_KEY_BLOCK_ELEMS))
    r = tkb // tq
    nk = T // tkb

    def body(q_ref, k_ref, v_ref, do_ref, dq_ref, dk_ref, dv_ref, dk_sc, dv_sc):
        kj = pl.program_id(1)
        dk_sc[...] = jnp.zeros_like(dk_sc)
        dv_sc[...] = jnp.zeros_like(dv_sc)

        @pl.when(kj == 0)
        def _():
            dq_ref[...] = jnp.zeros_like(dq_ref)

        def chunk(i, diag):
            rows = _chunk_rows(i, tq)
            qb = q_ref[rows, :]
            dob = do_ref[rows, :]
            live = slice(0, tkb if diag is None else (diag + 1) * tq)
            kb = k_ref[live, :]
            st = _dot(kb, qb, NT)
            if diag is not None:
                key = lax.broadcasted_iota(jnp.int32, st.shape, 0)
                qry = lax.broadcasted_iota(jnp.int32, st.shape, 1) + diag * tq
                st = jnp.where(key <= qry, st, NEG)
            pt = jnp.exp(st)
            dv_sc[live, :] += _dot(pt.astype(BF16), dob, NN)
            dst = (pt * _dot(v_ref[live, :], dob, NT)).astype(BF16)
            dk_sc[live, :] += _dot(dst, qb, NN)
            dq_ref[rows, :] += _dot(dst, kb, TN)

        def off_diagonal(i, carry):
            chunk(i, None)
            return carry

        for d in range(r):
            chunk(kj * r + d, d)
        lax.fori_loop(kj * r + r, T // tq, off_diagonal, 0)
        dk_ref[...] = dk_sc[...]
        dv_ref[...] = dv_sc[...]

        @pl.when(kj == nk - 1)
        def _():
            dq_ref[...] = dq_ref[...] * scale

    whole_q = pl.BlockSpec((None, T, dqk), lambda h, j: (h, 0, 0))
    k_spec = pl.BlockSpec((None, tkb, dqk), lambda h, j: (h, j, 0))
    v_spec = pl.BlockSpec((None, tkb, dva), lambda h, j: (h, j, 0))
    v_in_spec = pl.BlockSpec((None, tkb, dva), lambda h, j: (h + v_head0, j, 0))
    if token_major_out:
        out_specs = [pl.BlockSpec((T, dqk), lambda h, j: (0, h)), pl.BlockSpec((tkb, dqk), lambda h, j: (j, h)),
                     pl.BlockSpec((tkb, dva), lambda h, j: (j, h))]
        out_shape = [jax.ShapeDtypeStruct((T, H * dqk), F32), jax.ShapeDtypeStruct((T, H * dqk), F32),
                     jax.ShapeDtypeStruct((T, H * dva), F32)]
    else:
        out_specs = [whole_q, k_spec, v_spec]
        out_shape = [jax.ShapeDtypeStruct((H, T, dqk), F32), jax.ShapeDtypeStruct((H, T, dqk), F32),
                     jax.ShapeDtypeStruct((H, T, dva), F32)]
    return _call_carrying(
        body, name, (H, nk),
        in_specs=[whole_q, k_spec, v_in_spec, pl.BlockSpec((None, T, dva), lambda h, j: (h, 0, 0))],
        out_specs=out_specs, out_shape=out_shape,
        scratch_shapes=[pltpu.VMEM((tkb, dqk), F32), pltpu.VMEM((tkb, dva), F32)],
        operands=(q, k, v, do), exchange=exchange)


def _pack_head_grads(name, parts, W, head_dim, picks):
    T, HW = parts[0].shape
    H = HW // W
    n = len(parts)
    tm = _pick(T, (512, 256, 128))
    sels = [(jnp.arange(HW)[:, None] == (jnp.arange(128)[None, :] * W + col)).astype(F32) for _, col in picks]

    def body(*refs):
        xs = refs[:n]
        sel_refs = refs[n:n + len(picks)]
        packed_ref = refs[n + len(picks)]
        row_refs = refs[n + len(picks) + 1:]
        for a in range(n):
            x = xs[a][...]
            heads = [x[:, h * W:h * W + head_dim] for h in range(H)]
            packed_ref[:, a * H * head_dim:(a + 1) * H * head_dim] = (
                jnp.concatenate(heads, axis=-1).astype(packed_ref.dtype))
        for (a, _), s_ref, r_ref in zip(picks, sel_refs, row_refs):
            cols = lax.dot_general(xs[a][...], s_ref[...], NN, precision=lax.Precision.HIGHEST,
                                   preferred_element_type=F32)
            r_ref[...] = cols.T[:H, :]

    row = pl.BlockSpec((tm, HW), lambda i: (i, 0))
    out = pl.pallas_call(
        body, name=name, grid=(T // tm,),
        in_specs=[row] * n + [pl.BlockSpec((HW, 128), lambda i: (0, 0))] * len(picks),
        out_specs=[pl.BlockSpec((tm, n * H * head_dim), lambda i: (i, 0))]
        + [pl.BlockSpec((H, tm), lambda i: (0, i))] * len(picks),
        out_shape=[jax.ShapeDtypeStruct((T, n * H * head_dim), BF16)]
        + [jax.ShapeDtypeStruct((H, T), F32)] * len(picks),
        compiler_params=_cparams(("parallel",)),
    )(*parts, *sels)
    return out[0], out[1:]


def _adamw_math(w, g, m, v):
    m = ADAM_B1 * m + (1.0 - ADAM_B1) * g
    v = ADAM_B2 * v + (1.0 - ADAM_B2) * (g * g)
    m_hat = m / (1.0 - ADAM_B1 ** ADAM_STEP)
    v_hat = v / (1.0 - ADAM_B2 ** ADAM_STEP)
    delta = -ADAM_LR * (m_hat / (jnp.sqrt(v_hat) + ADAM_EPS) + ADAM_WD * w)
    return delta, m, v


def _adamw(name, parts, w, m, v):
    P, R, C = parts.shape
    tr = _pick(R, (256, 128, 64, 32, 16, 8))

    def body(p_ref, w_ref, m_ref, v_ref, g_out, d_out, m_out, v_out):
        g = p_ref[0].astype(F32)
        for i in range(1, P):
            g = g + p_ref[i].astype(F32)
        delta, m_new, v_new = _adamw_math(w_ref[...], g, m_ref[...], v_ref[...])
        g_out[...] = g
        d_out[...] = delta
        m_out[...] = m_new
        v_out[...] = v_new

    blk = pl.BlockSpec((tr, C), lambda i: (i, 0))
    sds = jax.ShapeDtypeStruct((R, C), F32)
    return pl.pallas_call(
        body, name=name, grid=(R // tr,),
        in_specs=[pl.BlockSpec((P, tr, C), lambda i: (0, i, 0)), blk, blk, blk],
        out_specs=[blk] * 4, out_shape=[sds] * 4,
        compiler_params=_cparams(("parallel",)),
    )(parts, w, m, v)


def _my_position():
    return lax.axis_index("x"), lax.axis_index("y"), lax.axis_index("c")


def _slot(p):
    return 4 * p[0] + 2 * p[1] + p[2]


def _flip(p, k):
    return tuple((1 - p[i]) if (k >> (2 - i)) & 1 else p[i] for i in range(3))


def _allgather_weights(shards):
    n = len(shards)

    def body(*refs):
        ins = refs[:n]
        outs = refs[n:2 * n]
        send_sems, recv_sems, local_sems = refs[2 * n:]
        x, y, c = _my_position()
        me, sibling = (x, y, c), (x, y, 1 - c)
        chips = [(1 - x, y), (x, 1 - y), (1 - x, 1 - y)]

        def copy(a, k, block, to, src=None):
            dst = outs[a].at[_slot(block)]
            return pltpu.make_async_remote_copy(
                src_ref=dst if src is None else src, dst_ref=dst,
                send_sem=send_sems.at[7 * a + k], recv_sem=recv_sems.at[7 * a + k],
                device_id=to, device_id_type=MESH)

        started = []
        for a in range(n):
            mine = pltpu.make_async_copy(ins[a], outs[a].at[_slot(me)], local_sems.at[a])
            mine.start()
            started.append(mine)
        first = []
        for a in range(n):
            first.append(copy(a, 0, me, sibling, src=ins[a]))
            first += [copy(a, 1 + j, me, (*chip, c), src=ins[a]) for j, chip in enumerate(chips)]
        for cp in first:
            cp.start()
        passed = []
        for j, chip in enumerate(chips):
            for a in range(n):
                copy(a, 1 + j, (*chip, c), me).wait_recv()
                fwd = copy(a, 4 + j, (*chip, c), sibling)
                fwd.start()
                passed.append(fwd)
        for a in range(n):
            copy(a, 0, sibling, me).wait_recv()
            for j, chip in enumerate(chips):
                copy(a, 4 + j, (*chip, 1 - c), me).wait_recv()
        for cp in first + passed:
            cp.wait_send()
        for mine in started:
            mine.wait()

    hbm = pl.BlockSpec(memory_space=pl.ANY)
    return pl.pallas_call(
        body, name="allgather_weights",
        in_specs=[hbm] * n, out_specs=[hbm] * n,
        out_shape=[jax.ShapeDtypeStruct((N_DEV,) + s.shape, s.dtype) for s in shards],
        scratch_shapes=[pltpu.SemaphoreType.DMA((7 * n,)), pltpu.SemaphoreType.DMA((7 * n,)),
                        pltpu.SemaphoreType.DMA((n,))],
        compiler_params=pltpu.CompilerParams(has_side_effects=True),
    )(*shards)


def _exchange_copies(kind, x_in, x_out, send_sems, recv_sems, local_sems, receives=True):
    me = _my_position()
    mine = _slot(me)
    local, sends, recvs = [], [], []
    for a in range(len(x_in)):
        src = x_in[a] if kind == "gather" else x_in[a].at[mine]
        local.append(pltpu.make_async_copy(src, x_out[a].at[mine], local_sems.at[a]))
    for k in range(1, N_DEV):
        peer = _flip(me, k)
        theirs = _slot(peer)
        for a in range(len(x_in)):
            src = x_in[a] if kind == "gather" else x_in[a].at[theirs]
            ends = [(x_out[a].at[mine], sends)] + ([(x_out[a].at[theirs], recvs)] if receives else [])
            for dst, group in ends:
                group.append(pltpu.make_async_remote_copy(
                    src_ref=src, dst_ref=dst, send_sem=send_sems.at[7 * a + k - 1],
                    recv_sem=recv_sems.at[7 * a + k - 1], device_id=peer, device_id_type=MESH))
    return local, sends, recvs


def _exchange_out_shapes(kind, arrays):
    return [jax.ShapeDtypeStruct(((N_DEV,) + a.shape) if kind == "gather" else a.shape, a.dtype)
            for a in arrays]


def _exchange_sems(n):
    return [pltpu.SemaphoreType.DMA((7 * n,)), pltpu.SemaphoreType.DMA((7 * n,)),
            pltpu.SemaphoreType.DMA((n,))]


def _call_carrying(body, name, grid, in_specs, out_specs, out_shape, scratch_shapes, operands, exchange,
                   sequential=False):
    if exchange is None:
        out = pl.pallas_call(
            body, name=name, grid=grid, in_specs=in_specs, out_specs=out_specs, out_shape=out_shape,
            scratch_shapes=scratch_shapes,
            compiler_params=_cparams((("arbitrary",) if sequential else ("parallel",))
                                     + ("arbitrary",) * (len(grid) - 1)),
        )(*operands)
        return out, None
    kind, arrays = exchange
    n, n_in, n_out, n_sc = len(arrays), len(in_specs), len(out_specs), len(scratch_shapes)

    def full_body(*refs):
        ins, refs = refs[:n_in], refs[n_in:]
        x_in, refs = refs[:n], refs[n:]
        outs, refs = refs[:n_out], refs[n_out:]
        x_out, refs = refs[:n], refs[n:]
        scratch, sems = refs[:n_sc], refs[n_sc:]
        first = last = None
        for axis, size in enumerate(grid):
            at_start = pl.program_id(axis) == 0
            at_end = pl.program_id(axis) == size - 1
            first = at_start if first is None else jnp.logical_and(first, at_start)
            last = at_end if last is None else jnp.logical_and(last, at_end)

        @pl.when(first)
        def _():
            local, sends, _ = _exchange_copies(kind, x_in, x_out, *sems, receives=False)
            for cp in local + sends:
                cp.start()

        body(*ins, *outs, *scratch)

        @pl.when(last)
        def _():
            local, sends, recvs = _exchange_copies(kind, x_in, x_out, *sems)
            for cp in recvs:
                cp.wait_recv()
            for cp in sends:
                cp.wait_send()
            for cp in local:
                cp.wait()

    hbm = pl.BlockSpec(memory_space=pl.ANY)
    out = pl.pallas_call(
        full_body, name=name, grid=grid,
        in_specs=list(in_specs) + [hbm] * n, out_specs=list(out_specs) + [hbm] * n,
        out_shape=list(out_shape) + _exchange_out_shapes(kind, arrays),
        scratch_shapes=list(scratch_shapes) + _exchange_sems(n),
        compiler_params=pltpu.CompilerParams(dimension_semantics=("arbitrary",) * len(grid),
                                             vmem_limit_bytes=VMEM_LIMIT_BYTES, has_side_effects=True),
    )(*operands, *arrays)
    return out[:n_out], out[n_out:]


def _allreduce_small(v):
    R, C = v.shape

    def body(v_ref, o_ref, buf, send_sems, recv_sems):
        me = _my_position()
        buf[_slot(me)] = v_ref[...]
        sends = []
        for k in range(1, N_DEV):
            peer = _flip(me, k)
            cp = pltpu.make_async_remote_copy(
                src_ref=v_ref, dst_ref=buf.at[_slot(me)],
                send_sem=send_sems.at[k - 1], recv_sem=recv_sems.at[k - 1],
                device_id=peer, device_id_type=MESH)
            cp.start()
            sends.append(cp)
        for k in range(1, N_DEV):
            peer = _flip(me, k)
            pltpu.make_async_remote_copy(
                src_ref=v_ref, dst_ref=buf.at[_slot(peer)],
                send_sem=send_sems.at[k - 1], recv_sem=recv_sems.at[k - 1],
                device_id=peer, device_id_type=MESH).wait_recv()
        for cp in sends:
            cp.wait_send()
        tot = buf[0]
        for s in range(1, N_DEV):
            tot = tot + buf[s]
        o_ref[...] = tot

    vm = pl.BlockSpec(memory_space=pltpu.VMEM)
    return pl.pallas_call(
        body, name="allreduce_small",
        in_specs=[vm], out_specs=vm, out_shape=jax.ShapeDtypeStruct((R, C), F32),
        scratch_shapes=[pltpu.VMEM((N_DEV, R, C), F32), pltpu.SemaphoreType.DMA((7,)),
                        pltpu.SemaphoreType.DMA((7,))],
        compiler_params=pltpu.CompilerParams(has_side_effects=True),
    )(v)


def _to_heads(t, heads):
    T = t.shape[0]
    return t.reshape(T, heads, t.shape[1] // heads).transpose(1, 0, 2)


def _from_heads(t):
    H, T, d = t.shape
    return t.transpose(1, 0, 2).reshape(T, H * d)


def _widen(t, width, ones_at=None, pieces_at=None, pieces=None):
    out = jnp.pad(t, ((0, 0), (0, 0), (0, width - t.shape[-1])))
    lane = lax.broadcasted_iota(jnp.int32, (1, 1, width), 2)
    if ones_at is not None:
        out = jnp.where((lane >= ones_at) & (lane < ones_at + 3), jnp.ones((), BF16), out)
    if pieces_at is not None:
        for i in range(3):
            out = jnp.where(lane == pieces_at + i, pieces[i][:, :, None], out)
    return out


def _split3(t):
    hi = lax.reduce_precision(t, 8, 7)
    r = t - hi
    mid = lax.reduce_precision(r, 8, 7)
    lo = lax.reduce_precision(r - mid, 8, 7)
    return hi.astype(BF16), mid.astype(BF16), lo.astype(BF16)


def _pad_cols(t, n):
    return jnp.pad(t, ((0, 0), (0, n - t.shape[1])))


def _pack_small(mix, ffn, kv, fin, kva, qa, bf, last):
    row6 = jnp.concatenate([kva.reshape(-1), qa.reshape(-1), bf.reshape(-1),
                            jnp.zeros((D_MODEL - KV_LORA - Q_LORA - FOX_HEADS,), F32)])
    return jnp.stack([mix[0], mix[1], ffn[0], ffn[1], kv.reshape(-1), fin.reshape(-1), row6, last])


def _unpack_small(p):
    mix = p[0:2]
    ffn = p[2:4]
    kv = p[4]
    fin = p[5]
    kva = p[6, :KV_LORA]
    qa = p[6, KV_LORA:KV_LORA + Q_LORA].reshape(1, Q_LORA)
    bf = p[6, KV_LORA + Q_LORA:KV_LORA + Q_LORA + FOX_HEADS].reshape(1, FOX_HEADS)
    return mix, ffn, bf, kv, kva, qa, fin


def _mlp_fwd(tag, xin, g, w_up, w_down):
    h = _rms(f"{tag}_norm", xin, g, BF16)

    def act(acc):
        r = jnp.maximum(acc, 0.0)
        return acc, r * r

    u, a = _mm(f"{tag}_up", h, w_up, "nn", (BF16, BF16), epi=act)
    xout = _mm(f"{tag}_down", a, w_down, "nn", (F32,), epi=lambda acc, r: (acc + r,), extras=(xin,))
    return xout, (h, u, a)


def _mlp_bwd(tag, gout, xin, g, w_up, w_down, saved):
    h, u, a = saved
    dw_down = _mm_tn(f"{tag}_dwdown", a, gout)
    du = _mm(f"{tag}_du", gout, w_down, "nt", (BF16,),
             epi=lambda acc, uu: (acc * (2.0 * jnp.maximum(uu.astype(F32), 0.0)),), extras=(u,))
    dw_up = _mm_tn(f"{tag}_dwup", h, du)
    gin, dg = _mm_rms_bwd(f"{tag}_dh_norm_bwd", du, w_up, xin, g, gout)
    return gin, dg, dw_up, dw_down


def kernel(x, norm_mix_g, norm_ffn_g, fox_w_in, fox_b_f, fox_w_out, kv_norm_g, mla_w_kv_a, mla_kv_a_norm_g, mla_w_kv_b, mla_w_q_a, mla_q_a_norm_g, mla_w_q_b, mla_w_out, ffn_w_up, ffn_w_down, final_norm_g, loss_target, m_norm_mix_g, m_norm_ffn_g, m_fox_w_in, m_fox_b_f, m_fox_w_out, m_kv_norm_g, m_mla_w_kv_a, m_mla_kv_a_norm_g, m_mla_w_kv_b, m_mla_w_q_a, m_mla_q_a_norm_g, m_mla_w_q_b, m_mla_w_out, m_ffn_w_up, m_ffn_w_down, m_final_norm_g, v_norm_mix_g, v_norm_ffn_g, v_fox_w_in, v_fox_b_f, v_fox_w_out, v_kv_norm_g, v_mla_w_kv_a, v_mla_kv_a_norm_g, v_mla_w_kv_b, v_mla_w_q_a, v_mla_q_a_norm_g, v_mla_w_q_b, v_mla_w_out, v_ffn_w_up, v_ffn_w_down, v_final_norm_g):
    T = x.shape[1]
    D = D_MODEL
    tq = 512 if T >= 2048 else 128
    x0 = x[0]
    tgt = loss_target[0]

    gat_fox = _allgather_weights([fox_w_in[0].astype(BF16), fox_w_out[0].astype(BF16)])
    later_shards = [s.astype(BF16) for s in (mla_w_kv_a, mla_w_kv_b, mla_w_q_a[0], mla_w_q_b[0],
                                             mla_w_out[0], ffn_w_up, ffn_w_down)]
    w_in = gat_fox[0].transpose(1, 0, 2).reshape(D, 3 * D + FOX_HEADS)
    w_qkv = w_in[:, :3 * D]
    w_f = _pad_cols(w_in[:, 3 * D:], 128)
    w_fo = gat_fox[1].reshape(D, D)
    g_mix0, g_mix1 = norm_mix_g[0:1], norm_mix_g[1:2]
    g_ffn0, g_ffn1 = norm_ffn_g[0:1], norm_ffn_g[1:2]
    g_kv = kv_norm_g.reshape(1, D)
    g_kva = mla_kv_a_norm_g.reshape(1, KV_LORA)
    g_qa = mla_q_a_norm_g.reshape(1, Q_LORA)
    g_fin = final_norm_g.reshape(1, D)

    inv = 1.0 / (ROPE_BASE ** (jnp.arange(0, QK_ROPE, 2, dtype=F32) / QK_ROPE))
    ang = jnp.arange(T, dtype=F32)[:, None] * inv[None, :]
    cos, sin = jnp.cos(ang), jnp.sin(ang)
    cos2 = jnp.concatenate([cos, cos], axis=-1)
    sgn_sin = jnp.concatenate([-sin, sin], axis=-1)

    h0 = _rms("l0_mix_norm", x0, g_mix0, BF16)
    fl_pad = _mm("fox_gate_logit", h0, w_f, "nn", (F32,))
    fl = fl_pad[:, :FOX_HEADS].T
    b_f = fox_b_f.reshape(FOX_HEADS, 1)
    cgate = _gate_cumsum("fox_gate_scan", fl, b_f, tq)
    fox_scale = FOX_HEAD_DIM ** -0.5
    col_scale = jnp.where(jnp.arange(3 * D) < D, fox_scale, 1.0).astype(BF16)
    tail = jnp.arange(FOX_AUG - FOX_HEAD_DIM)
    ones_q = (tail < 3).astype(F32)
    consts_k = ((tail >= 4) & (tail < 7)).astype(F32) + (tail == 3).astype(F32) * (1.0 / fox_scale)
    tails = jnp.broadcast_to(jnp.stack([ones_q, consts_k, ones_q])[:, None, None, :],
                             (3, FOX_HEADS, 1, FOX_AUG - FOX_HEAD_DIM)).reshape(3 * FOX_HEADS, 1, -1)
    qkv_h = _mm_head_slabs("fox_qkv", h0, w_qkv * col_scale, tails, FOX_HEAD_DIM, BF16)
    fk_aug = _widen(qkv_h[FOX_HEADS:2 * FOX_HEADS], FOX_AUG, pieces_at=FOX_HEAD_DIM,
                    pieces=_split3(-cgate))
    (fo, flse), gat = _flash_fwd("fox_attn", qkv_h, fk_aug, qkv_h, FOX_HEAD_DIM, tq,
                                 exchange=("gather", later_shards), q_head0=0, v_head0=2 * FOX_HEADS)
    w_kva = _pad_cols(gat[0].reshape(D, KV_LORA + QK_ROPE), KV_A_PAD)
    w_kvb_h = gat[1]
    w_qa = gat[2].reshape(D, Q_LORA)
    w_qb_h = gat[3]
    w_mo = gat[4].reshape(D, D)
    w_up = gat[5].transpose(1, 2, 0, 3).reshape(2, D, D_FF)
    w_down = gat[6].transpose(1, 0, 2, 3).reshape(2, D_FF, D)
    fctx = _from_heads(fo).astype(BF16)
    x1 = _mm("fox_out", fctx, w_fo, "nn", (F32,), epi=lambda acc, r: (acc + r,), extras=(x0,))
    x2, mlp0 = _mlp_fwd("l0_ffn", x1, g_ffn0, w_up[0], w_down[0])

    src = _rms("kv_norm", x2, g_kv, BF16)
    kva = _mm("kv_a", src, w_kva, "nn", (F32,))
    kva_lat = kva[:, :KV_LORA]
    c_kv = _rms("kv_a_norm", kva_lat, g_kva, BF16)
    k_rope = _rope("k_rope", kva[:, KV_LORA:KV_LORA + QK_ROPE][None], cos2, sgn_sin, BF16)
    kvb_h = _mm_heads("kv_b", c_kv, w_kvb_h, BF16)
    mk = jnp.concatenate([kvb_h[:, :, :QK_NOPE],
                          jnp.broadcast_to(k_rope, (MLA_HEADS, T, QK_ROPE))], axis=-1)
    mv = kvb_h[:, :, QK_NOPE:]

    h1 = _rms("l1_mix_norm", x2, g_mix1, BF16)
    qa = _mm("q_a", h1, w_qa, "nn", (F32,))
    c_q = _rms("q_a_norm", qa, g_qa, BF16)
    mla_scale = (QK_NOPE + QK_ROPE) ** -0.5
    mq = _mla_q_proj("q_b", c_q, w_qb_h, cos2, sgn_sin, mla_scale)
    mv_aug = _widen(mv, MLA_AUG, ones_at=V_HEAD)
    (mo, mlse), _ = _flash_fwd("mla_attn", mq, mk, mv_aug, V_HEAD, tq)
    mctx = _from_heads(mo).astype(BF16)
    x3 = _mm("mla_out", mctx, w_mo, "nn", (F32,), epi=lambda acc, r: (acc + r,), extras=(x2,))
    x4, mlp1 = _mlp_fwd("l1_ffn", x3, g_ffn1, w_up[1], w_down[1])

    g4, dg_fin, loss_vec = _loss_head("loss_head", x4, g_fin, tgt)

    g3, dg_ffn1, dw_up1, dw_down1 = _mlp_bwd("l1_ffn", g4, x3, g_ffn1, w_up[1], w_down[1], mlp1)

    dw_mo = _mm_tn("mla_out_dw", mctx, g3)
    dmo = _to_heads(_mm("mla_out_dx", g3, w_mo, "nt", (BF16,)), MLA_HEADS)
    mdelta = _row_dot("mla_delta", mo, dmo)
    dqk = QK_NOPE + QK_ROPE
    mq_bwd = _widen(mq, MLA_AUG, pieces_at=dqk, pieces=_split3(-mlse))
    mk_bwd = _widen(mk, MLA_AUG, ones_at=dqk)
    mdo_aug = _widen(dmo, MLA_AUG, pieces_at=V_HEAD, pieces=_split3(-mdelta))
    (mdq, mdk, mdv), _ = _flash_bwd("mla_attn_bwd", mq_bwd, mk_bwd, mv_aug, mdo_aug, mla_scale, tq)
    mdq = mdq[:, :, :dqk]
    mdk = mdk[:, :, :dqk]
    mdv = mdv[:, :, :V_HEAD]
    dq_rope = _rope_bwd("q_rope_bwd", mdq[:, :, QK_NOPE:], cos2, sgn_sin, False)
    dqf_h = jnp.concatenate([mdq[:, :, :QK_NOPE], dq_rope], axis=-1)
    dw_qb_h = _mm_heads_dw("q_b_dw", c_q, dqf_h)
    dc_q = _mm_heads_dx("q_b_dx", dqf_h, w_qb_h)
    dqa, dg_qa = _rms_bwd("q_a_norm_bwd", qa, g_qa, dc_q)
    dw_qa = _mm_tn("q_a_dw", h1, dqa)
    g2a, dg_mix1 = _mm_rms_bwd("q_a_dx_norm_bwd", dqa, w_qa, x2, g_mix1, g3)

    dk_rope = _rope_bwd("k_rope_bwd", mdk[:, :, QK_NOPE:], cos2, sgn_sin, True)
    dkvb_h = jnp.concatenate([mdk[:, :, :QK_NOPE], mdv], axis=-1)
    dw_kvb_h = _mm_heads_dw("kv_b_dw", c_kv, dkvb_h)
    dc_kv = _mm_heads_dx("kv_b_dx", dkvb_h, w_kvb_h)
    dkva_lat, dg_kva = _rms_bwd("kv_a_norm_bwd", kva_lat, g_kva, dc_kv)
    dkva = _pad_cols(jnp.concatenate([dkva_lat, dk_rope], axis=-1), KV_A_PAD)
    dw_kva = _mm_tn("kv_a_dw", src, dkva)[:, :KV_LORA + QK_ROPE]
    g2, dg_kv = _mm_rms_bwd("kv_a_dx_norm_bwd", dkva, w_kva, x2, g_kv, g2a)

    g1, dg_ffn0, dw_up0, dw_down0 = _mlp_bwd("l0_ffn", g2, x1, g_ffn0, w_up[0], w_down[0], mlp0)

    dw_fo = _mm_tn("fox_out_dw", fctx, g1)
    dfo = _to_heads(_mm("fox_out_dx", g1, w_fo, "nt", (BF16,)), FOX_HEADS)
    fdelta = _row_dot("fox_delta", fo, dfo)
    fq_bwd = _widen(qkv_h[:FOX_HEADS], FOX_AUG, pieces_at=FOX_HEAD_DIM + 4, pieces=_split3(-flse))
    fdo_aug = _widen(dfo, FOX_AUG, pieces_at=FOX_HEAD_DIM, pieces=_split3(-fdelta))
    dw_up = jnp.stack([dw_up0, dw_up1])
    dw_down = jnp.stack([dw_down0, dw_down1])
    early = [
        dw_fo.reshape(N_DEV, D // N_DEV, D),
        dw_kva.reshape(N_DEV, D // N_DEV, KV_LORA + QK_ROPE),
        dw_kvb_h,
        dw_qa.reshape(N_DEV, D // N_DEV, Q_LORA),
        dw_qb_h,
        dw_mo.reshape(N_DEV, D // N_DEV, D),
        dw_up.reshape(2, D, N_DEV, -1).transpose(2, 0, 1, 3),
        dw_down.reshape(2, N_DEV, D_FF // N_DEV, D).transpose(1, 0, 2, 3),
    ]
    fd_aug, early_parts = _flash_bwd(
        "fox_attn_bwd", fq_bwd, fk_aug, qkv_h, fdo_aug, fox_scale, tq,
        exchange=("scatter", [g.astype(BF16) for g in early]), v_head0=2 * FOX_HEADS, token_major_out=True)
    dqkv, (ds_rows, ds_cols) = _pack_head_grads("fox_grad_pack", list(fd_aug), FOX_AUG, FOX_HEAD_DIM,
                                                picks=((0, FOX_HEAD_DIM + 3), (1, FOX_HEAD_DIM)))
    dfl, db_f = _gate_cumsum_bwd("fox_gate_scan_bwd", ds_rows, ds_cols, fl, b_f, tq)
    dfl_pad = _pad_cols(dfl.T, 128)
    dw_qkv = _mm_tn("fox_qkv_dw", h0, dqkv)
    dw_f = _mm_tn("fox_gate_dw", h0, dfl_pad)[:, :FOX_HEADS]
    dw_in = jnp.concatenate([dw_qkv, dw_f], axis=-1)
    dh0a = _mm("fox_gate_dx", dfl_pad, w_f, "nt", (F32,))
    late = dw_in.reshape(D, N_DEV, -1).transpose(1, 0, 2).astype(BF16)
    grad_x, dg_mix0, late_parts = _mm_rms_bwd("fox_qkv_dx_norm_bwd", dqkv, w_qkv, x0, g_mix0, g1, add=dh0a,
                                              exchange=("scatter", [late]))

    parts = list(late_parts) + list(early_parts)

    names = ["fox_w_in", "fox_w_out", "mla_w_kv_a", "mla_w_kv_b", "mla_w_q_a", "mla_w_q_b",
             "mla_w_out", "ffn_w_up", "ffn_w_down"]
    moms = [m_fox_w_in, m_fox_w_out, m_mla_w_kv_a, m_mla_w_kv_b, m_mla_w_q_a, m_mla_w_q_b,
            m_mla_w_out, m_ffn_w_up, m_ffn_w_down]
    vars_ = [v_fox_w_in, v_fox_w_out, v_mla_w_kv_a, v_mla_w_kv_b, v_mla_w_q_a, v_mla_w_q_b,
             v_mla_w_out, v_ffn_w_up, v_ffn_w_down]
    full = [fox_w_in, fox_w_out, mla_w_kv_a, mla_w_kv_b, mla_w_q_a, mla_w_q_b, mla_w_out,
            ffn_w_up, ffn_w_down]
    big = {}
    for nm, p, w, m, v in zip(names, parts, full, moms, vars_):
        C = w.shape[-1]
        res = _adamw(f"adamw_{nm}", p.reshape(N_DEV, -1, C), w.reshape(-1, C), m.reshape(-1, C),
                     v.reshape(-1, C))
        big[nm] = [r.reshape(w.shape) for r in res]

    zrow = jnp.zeros((D,), F32)
    g_small = _pack_small(jnp.concatenate([dg_mix0, dg_mix1]), jnp.concatenate([dg_ffn0, dg_ffn1]),
                          dg_kv, dg_fin, dg_kva, dg_qa, db_f, zrow.at[0].set(loss_vec[0, 0]))
    tot_small = _allreduce_small(g_small)
    w_small = _pack_small(norm_mix_g, norm_ffn_g, kv_norm_g, final_norm_g, mla_kv_a_norm_g,
                          mla_q_a_norm_g, fox_b_f, zrow)
    m_small = _pack_small(m_norm_mix_g, m_norm_ffn_g, m_kv_norm_g, m_final_norm_g, m_mla_kv_a_norm_g,
                          m_mla_q_a_norm_g, m_fox_b_f, zrow)
    v_small = _pack_small(v_norm_mix_g, v_norm_ffn_g, v_kv_norm_g, v_final_norm_g, v_mla_kv_a_norm_g,
                          v_mla_q_a_norm_g, v_fox_b_f, zrow)
    small = _adamw("adamw_small", tot_small[None], w_small, m_small, v_small)
    loss = tot_small[7, 0]
    small = [_unpack_small(s) for s in small]

    def ordered(i):
        mix, ffn, bf, kv, kva, qa, fin = small[i]
        return [mix, ffn, big["fox_w_in"][i], bf, big["fox_w_out"][i], kv, big["mla_w_kv_a"][i], kva,
                big["mla_w_kv_b"][i], big["mla_w_q_a"][i], qa, big["mla_w_q_b"][i],
                big["mla_w_out"][i], big["ffn_w_up"][i], big["ffn_w_down"][i], fin]

    return (loss, grad_x[None], *ordered(0), *ordered(1), *ordered(2), *ordered(3))
```

```python
import functools
import math

import jax
import jax.numpy as jnp
from jax import lax
from jax.experimental import pallas as pl
from jax.experimental.pallas import tpu as pltpu

F32 = jnp.float32
BF16 = jnp.bfloat16
MESH = pl.DeviceIdType.MESH

N_DEV = 8
D_MODEL = 1024
FOX_HEADS = 16
FOX_HEAD_DIM = 64
FOX_AUG = 128
MLA_AUG = 256
MLA_HEADS = 8
QK_NOPE = 128
QK_ROPE = 64
V_HEAD = 128
Q_LORA = 384
KV_LORA = 256
KV_A_PAD = 384
D_FF = 4096
ROPE_BASE = 10000.0
EPS = 1e-6
NEG = -1e30

ADAM_LR = 0.001
ADAM_B1 = 0.9
ADAM_B2 = 0.999
ADAM_EPS = 1e-08
ADAM_WD = 0.01
ADAM_STEP = 10

VMEM_LIMIT_BYTES = 56 * 1024 * 1024
MAX_KEY_BLOCK_ELEMS = 2048 * 128

NN = (((1,), (0,)), ((), ()))
NT = (((1,), (1,)), ((), ()))
TN = (((0,), (0,)), ((), ()))
_FORMS = {"nn": NN, "nt": NT}


def _cparams(sem=None):
    return pltpu.CompilerParams(dimension_semantics=sem, vmem_limit_bytes=VMEM_LIMIT_BYTES)


def _pick(n, cands):
    for c in cands:
        if c <= n and n % c == 0:
            return c
    return n


def _dot(a, b, dims):
    return lax.dot_general(a, b, dims, preferred_element_type=F32)


def _mm(name, a, b, form, out_dtypes, epi=None, extras=(), tm=1024, tn=None):
    M, K = a.shape
    N = b.shape[1] if form == "nn" else b.shape[0]
    tm = _pick(M, (tm, 512, 256, 128))
    tn = _pick(N, (tn or (1024 if K <= 1024 else 512), 512, 384, 256, 128))
    n_ex = len(extras)
    n_out = len(out_dtypes)
    cast_once = a.dtype != BF16

    def body(*refs):
        a_ref, b_ref = refs[0], refs[1]
        ex = refs[2:2 + n_ex]
        outs = refs[2 + n_ex:2 + n_ex + n_out]
        if cast_once:
            a_sc = refs[2 + n_ex + n_out]

            @pl.when(pl.program_id(1) == 0)
            def _():
                a_sc[...] = a_ref[...].astype(BF16)

            av = a_sc[...]
        else:
            av = a_ref[...]
        acc = _dot(av, b_ref[...].astype(BF16), _FORMS[form])
        res = epi(acc, *[e[...] for e in ex]) if epi is not None else (acc,)
        for o_ref, r in zip(outs, res):
            o_ref[...] = r.astype(o_ref.dtype)

    if form == "nn":
        b_spec = pl.BlockSpec((K, tn), lambda i, j: (0, j))
    else:
        b_spec = pl.BlockSpec((tn, K), lambda i, j: (j, 0))
    tile = pl.BlockSpec((tm, tn), lambda i, j: (i, j))
    out = pl.pallas_call(
        body, name=name, grid=(M // tm, N // tn),
        in_specs=[pl.BlockSpec((tm, K), lambda i, j: (i, 0)), b_spec] + [tile] * n_ex,
        out_specs=[tile] * n_out,
        out_shape=[jax.ShapeDtypeStruct((M, N), dt) for dt in out_dtypes],
        scratch_shapes=[pltpu.VMEM((tm, K), BF16)] if cast_once else [],
        compiler_params=_cparams(("parallel", "arbitrary")),
    )(a, b, *extras)
    return out if n_out > 1 else out[0]


def _mm_tn(name, a, b):
    T, Ka = a.shape
    N = b.shape[1]
    tk = _pick(Ka, (1024, 512, 384, 256, 128))
    tn = _pick(N, (1024, 768, 512, 384, 256, 128))
    tt = _pick(T, (1024, 512, 256, 128))

    def body(a_ref, b_ref, o_ref):
        @pl.when(pl.program_id(2) == 0)
        def _():
            o_ref[...] = jnp.zeros_like(o_ref)

        o_ref[...] += _dot(a_ref[...].astype(BF16), b_ref[...].astype(BF16), TN)

    return pl.pallas_call(
        body, name=name, grid=(Ka // tk, N // tn, T // tt),
        in_specs=[pl.BlockSpec((tt, tk), lambda i, j, t: (t, i)),
                  pl.BlockSpec((tt, tn), lambda i, j, t: (t, j))],
        out_specs=pl.BlockSpec((tk, tn), lambda i, j, t: (i, j)),
        out_shape=jax.ShapeDtypeStruct((Ka, N), F32),
        compiler_params=_cparams(("parallel", "parallel", "arbitrary")),
    )(a, b)


def _mm_heads(name, a, w, out_dtype):
    T, K = a.shape
    H, _, N = w.shape
    tm = _pick(T, (1024, 512, 256, 128))

    def body(a_ref, w_ref, o_ref):
        av = a_ref[...].astype(BF16)
        for s in range(H):
            o_ref[s] = _dot(av, w_ref[s].astype(BF16), NN).astype(o_ref.dtype)

    return pl.pallas_call(
        body, name=name, grid=(T // tm,),
        in_specs=[pl.BlockSpec((tm, K), lambda i: (i, 0)), pl.BlockSpec((H, K, N), lambda i: (0, 0, 0))],
        out_specs=pl.BlockSpec((H, tm, N), lambda i: (0, i, 0)),
        out_shape=jax.ShapeDtypeStruct((H, T, N), out_dtype),
        compiler_params=_cparams(("parallel",)),
    )(a, w)


def _mm_head_slabs(name, a, w, tails, head_dim, out_dtype, heads_per_step=8):
    T, K = a.shape
    S, _, tail = tails.shape
    hb = heads_per_step
    tm = _pick(T, (1024, 512, 256, 128))

    def body(a_ref, w_ref, t_ref, o_ref):
        acc = _dot(a_ref[...].astype(BF16), w_ref[...].astype(BF16), NN)
        for s in range(hb):
            slab = jnp.concatenate([acc[:, s * head_dim:(s + 1) * head_dim],
                                    jnp.broadcast_to(t_ref[s], (tm, tail))], axis=-1)
            o_ref[s] = slab.astype(o_ref.dtype)

    return pl.pallas_call(
        body, name=name, grid=(T // tm, S // hb),
        in_specs=[pl.BlockSpec((tm, K), lambda i, j: (i, 0)),
                  pl.BlockSpec((K, hb * head_dim), lambda i, j: (0, j)),
                  pl.BlockSpec((hb, 1, tail), lambda i, j: (j, 0, 0))],
        out_specs=pl.BlockSpec((hb, tm, head_dim + tail), lambda i, j: (j, i, 0)),
        out_shape=jax.ShapeDtypeStruct((S, T, head_dim + tail), out_dtype),
        compiler_params=_cparams(("parallel", "arbitrary")),
    )(a, w, tails)


def _mm_heads_dw(name, a, g):
    T, K = a.shape
    H, _, N = g.shape
    tt = _pick(T, (1024, 512, 256, 128))

    def body(a_ref, g_ref, o_ref):
        @pl.when(pl.program_id(0) == 0)
        def _():
            o_ref[...] = jnp.zeros_like(o_ref)

        av = a_ref[...].astype(BF16)
        for s in range(H):
            o_ref[s] += _dot(av, g_ref[s].astype(BF16), TN)

    return pl.pallas_call(
        body, name=name, grid=(T // tt,),
        in_specs=[pl.BlockSpec((tt, K), lambda t: (t, 0)), pl.BlockSpec((H, tt, N), lambda t: (0, t, 0))],
        out_specs=pl.BlockSpec((H, K, N), lambda t: (0, 0, 0)),
        out_shape=jax.ShapeDtypeStruct((H, K, N), F32),
        compiler_params=_cparams(("arbitrary",)),
    )(a, g)


def _mm_heads_dx(name, g, w):
    H, T, N = g.shape
    K = w.shape[1]
    tm = _pick(T, (512, 256, 128))

    def body(g_ref, w_ref, o_ref):
        acc = _dot(g_ref[0].astype(BF16), w_ref[0].astype(BF16), NT)
        for s in range(1, H):
            acc = acc + _dot(g_ref[s].astype(BF16), w_ref[s].astype(BF16), NT)
        o_ref[...] = acc

    return pl.pallas_call(
        body, name=name, grid=(T // tm,),
        in_specs=[pl.BlockSpec((H, tm, N), lambda i: (0, i, 0)), pl.BlockSpec((H, K, N), lambda i: (0, 0, 0))],
        out_specs=pl.BlockSpec((tm, K), lambda i: (i, 0)),
        out_shape=jax.ShapeDtypeStruct((T, K), F32),
        compiler_params=_cparams(("parallel",)),
    )(g, w)


def _rms(name, x, g, out_dtype):
    T, D = x.shape
    tm = _pick(T, (1024, 512, 256, 128))

    def body(x_ref, g_ref, o_ref):
        xf = x_ref[...]
        r = lax.rsqrt(jnp.mean(xf * xf, axis=-1, keepdims=True) + EPS)
        o_ref[...] = (xf * r * g_ref[...]).astype(o_ref.dtype)

    return pl.pallas_call(
        body, name=name, grid=(T // tm,),
        in_specs=[pl.BlockSpec((tm, D), lambda i: (i, 0)), pl.BlockSpec((1, D), lambda i: (0, 0))],
        out_specs=pl.BlockSpec((tm, D), lambda i: (i, 0)),
        out_shape=jax.ShapeDtypeStruct((T, D), out_dtype),
        compiler_params=_cparams(("parallel",)),
    )(x, g)


def _rms_bwd(name, x, g, dh, dres=None):
    T, D = x.shape
    tm = _pick(T, (512, 256, 128))
    has_res = dres is not None

    def body(*refs):
        if has_res:
            x_ref, g_ref, dh_ref, dres_ref, dx_ref, dg_ref = refs
        else:
            x_ref, g_ref, dh_ref, dx_ref, dg_ref = refs

        @pl.when(pl.program_id(0) == 0)
        def _():
            dg_ref[...] = jnp.zeros_like(dg_ref)

        xf = x_ref[...]
        r = lax.rsqrt(jnp.mean(xf * xf, axis=-1, keepdims=True) + EPS)
        xhat = xf * r
        dy = dh_ref[...].astype(F32)
        dxh = dy * g_ref[...]
        dx = r * (dxh - xhat * jnp.mean(dxh * xhat, axis=-1, keepdims=True))
        if has_res:
            dx = dx + dres_ref[...]
        dx_ref[...] = dx
        dg_ref[...] += jnp.sum(dy * xhat, axis=0, keepdims=True)

    row = pl.BlockSpec((tm, D), lambda i: (i, 0))
    vec = pl.BlockSpec((1, D), lambda i: (0, 0))
    ins = [x, g, dh] + ([dres] if has_res else [])
    return pl.pallas_call(
        body, name=name, grid=(T // tm,),
        in_specs=[row, vec, row] + ([row] if has_res else []),
        out_specs=[row, vec],
        out_shape=[jax.ShapeDtypeStruct((T, D), F32), jax.ShapeDtypeStruct((1, D), F32)],
        compiler_params=_cparams(("arbitrary",)),
    )(*ins)


def _mm_rms_bwd(name, a, b, x, g, dres, add=None, exchange=None, emit_bf16=False):
    T, K = a.shape
    D = b.shape[0]
    tm = _pick(T, (512, 256, 128))
    has_add = add is not None

    def body(*refs):
        a_ref, b_ref, x_ref, g_ref, dres_ref = refs[:5]
        n_in = 6 if has_add else 5
        dx_ref, dg_ref = refs[n_in], refs[n_in + 1]

        @pl.when(pl.program_id(0) == 0)
        def _():
            dg_ref[...] = jnp.zeros_like(dg_ref)

        dy = _dot(a_ref[...].astype(BF16), b_ref[...].astype(BF16), NT)
        if has_add:
            dy = dy + refs[5][...]
        xf = x_ref[...]
        r = lax.rsqrt(jnp.mean(xf * xf, axis=-1, keepdims=True) + EPS)
        xhat = xf * r
        dxh = dy * g_ref[...]
        dx = r * (dxh - xhat * jnp.mean(dxh * xhat, axis=-1, keepdims=True)) + dres_ref[...]
        dx_ref[...] = dx
        if emit_bf16:
            refs[n_in + 2][...] = dx.astype(BF16)
        dg_ref[...] += jnp.sum(dy * xhat, axis=0, keepdims=True)

    row = pl.BlockSpec((tm, D), lambda i: (i, 0))
    vec = pl.BlockSpec((1, D), lambda i: (0, 0))
    outs, exchanged = _call_carrying(
        body, name, (T // tm,),
        in_specs=[pl.BlockSpec((tm, K), lambda i: (i, 0)), pl.BlockSpec((D, K), lambda i: (0, 0)), row, vec, row]
        + ([row] if has_add else []),
        out_specs=[row, vec] + ([row] if emit_bf16 else []),
        out_shape=[jax.ShapeDtypeStruct((T, D), F32), jax.ShapeDtypeStruct((1, D), F32)]
        + ([jax.ShapeDtypeStruct((T, D), BF16)] if emit_bf16 else []),
        scratch_shapes=[], operands=(a, b, x, g, dres) + ((add,) if has_add else ()), exchange=exchange,
        sequential=True)
    return tuple(outs) if exchange is None else tuple(outs) + (exchanged,)


def _loss_head(name, x, g, tgt):
    T, D = x.shape
    tm = _pick(T, (512, 256, 128))

    def body(x_ref, g_ref, t_ref, dx_ref, dg_ref, loss_ref, dx16_ref):
        @pl.when(pl.program_id(0) == 0)
        def _():
            dg_ref[...] = jnp.zeros_like(dg_ref)
            loss_ref[...] = jnp.zeros_like(loss_ref)

        xf = x_ref[...]
        r = lax.rsqrt(jnp.mean(xf * xf, axis=-1, keepdims=True) + EPS)
        xhat = xf * r
        gv = g_ref[...]
        err = xhat * gv - t_ref[...]
        row_loss = jnp.mean(err * err, axis=-1, keepdims=True)
        loss_ref[...] += 0.5 * jnp.sum(row_loss, axis=0, keepdims=True)
        dy = err * (1.0 / D)
        dxh = dy * gv
        dx = r * (dxh - xhat * jnp.mean(dxh * xhat, axis=-1, keepdims=True))
        dx_ref[...] = dx
        dx16_ref[...] = dx.astype(BF16)
        dg_ref[...] += jnp.sum(dy * xhat, axis=0, keepdims=True)

    row = pl.BlockSpec((tm, D), lambda i: (i, 0))
    vec = pl.BlockSpec((1, D), lambda i: (0, 0))
    return pl.pallas_call(
        body, name=name, grid=(T // tm,),
        in_specs=[row, vec, row],
        out_specs=[row, vec, pl.BlockSpec((1, 128), lambda i: (0, 0)), row],
        out_shape=[jax.ShapeDtypeStruct((T, D), F32), jax.ShapeDtypeStruct((1, D), F32),
                   jax.ShapeDtypeStruct((1, 128), F32), jax.ShapeDtypeStruct((T, D), BF16)],
        compiler_params=_cparams(("arbitrary",)),
    )(x, g, tgt)


def _swap_halves(t):
    half = t.shape[-1] // 2
    return jnp.concatenate([t[:, half:], t[:, :half]], axis=-1)


def _rope(name, t, cos2, sgn_sin, out_dtype):
    H, T, R = t.shape
    tm = _pick(T, (1024, 512, 256, 128))

    def body(t_ref, c_ref, s_ref, o_ref):
        tf = t_ref[...].astype(F32)
        o_ref[...] = (tf * c_ref[...] + _swap_halves(tf) * s_ref[...]).astype(o_ref.dtype)

    slab = pl.BlockSpec((None, tm, R), lambda h, i: (h, i, 0))
    tab = pl.BlockSpec((tm, R), lambda h, i: (i, 0))
    return pl.pallas_call(
        body, name=name, grid=(H, T // tm),
        in_specs=[slab, tab, tab], out_specs=slab,
        out_shape=jax.ShapeDtypeStruct((H, T, R), out_dtype),
        compiler_params=_cparams(("parallel", "parallel")),
    )(t, cos2, sgn_sin)


def _mla_q_proj(name, a, w, cos2, sgn_sin, scale):
    T, K = a.shape
    H, _, W = w.shape
    R = cos2.shape[1]
    tm = _pick(T, (1024, 512, 256, 128))

    def body(a_ref, w_ref, c_ref, s_ref, o_ref):
        av = a_ref[...].astype(BF16)
        for h in range(H):
            qf = _dot(av, w_ref[h].astype(BF16), NN)
            r = qf[:, W - R:]
            roped = r * c_ref[...] + _swap_halves(r) * s_ref[...]
            o_ref[h] = (jnp.concatenate([qf[:, :W - R], roped], axis=-1) * scale).astype(o_ref.dtype)

    tab = pl.BlockSpec((tm, R), lambda i: (i, 0))
    return pl.pallas_call(
        body, name=name, grid=(T // tm,),
        in_specs=[pl.BlockSpec((tm, K), lambda i: (i, 0)), pl.BlockSpec((H, K, W), lambda i: (0, 0, 0)), tab, tab],
        out_specs=pl.BlockSpec((H, tm, W), lambda i: (0, i, 0)),
        out_shape=jax.ShapeDtypeStruct((H, T, W), BF16),
        compiler_params=_cparams(("parallel",)),
    )(a, w, cos2, sgn_sin)


def _rope_bwd(name, dy, cos2, sgn_sin, sum_heads):
    H, T, R = dy.shape
    tm = _pick(T, (1024, 512, 256, 128))

    def body(d_ref, c_ref, s_ref, o_ref):
        d = d_ref[...]
        if sum_heads:
            tot = d[0]
            for h in range(1, H):
                tot = tot + d[h]
            d = tot
        o_ref[...] = d * c_ref[...] + _swap_halves(d * s_ref[...])

    if sum_heads:
        grid = (T // tm,)
        in_slab = pl.BlockSpec((H, tm, R), lambda i: (0, i, 0))
        out_slab = pl.BlockSpec((tm, R), lambda i: (i, 0))
        tab = pl.BlockSpec((tm, R), lambda i: (i, 0))
        out_shape = jax.ShapeDtypeStruct((T, R), F32)
        sem = ("parallel",)
    else:
        grid = (H, T // tm)
        in_slab = pl.BlockSpec((None, tm, R), lambda h, i: (h, i, 0))
        out_slab = in_slab
        tab = pl.BlockSpec((tm, R), lambda h, i: (i, 0))
        out_shape = jax.ShapeDtypeStruct((H, T, R), F32)
        sem = ("parallel", "parallel")
    return pl.pallas_call(
        body, name=name, grid=grid, in_specs=[in_slab, tab, tab], out_specs=out_slab,
        out_shape=out_shape, compiler_params=_cparams(sem),
    )(dy, cos2, sgn_sin)


def _log_sigmoid(z):
    return jnp.minimum(z, 0.0) - jnp.log(1.0 + jnp.exp(-jnp.abs(z)))


def _gate_cumsum(name, fl, b, tb):
    H, T = fl.shape

    def body(f_ref, b_ref, c_ref, carry):
        @pl.when(pl.program_id(0) == 0)
        def _():
            carry[...] = jnp.zeros_like(carry)

        ls = _log_sigmoid(f_ref[...] + b_ref[...])
        src = lax.broadcasted_iota(jnp.int32, (tb, tb), 0)
        dst = lax.broadcasted_iota(jnp.int32, (tb, tb), 1)
        tri = (src <= dst).astype(F32)
        c = lax.dot_general(ls, tri, NN, precision=lax.Precision.HIGHEST,
                            preferred_element_type=F32) + carry[...]
        c_ref[...] = c
        carry[...] = carry[...] + jnp.sum(ls, axis=-1, keepdims=True)

    return pl.pallas_call(
        body, name=name, grid=(T // tb,),
        in_specs=[pl.BlockSpec((H, tb), lambda i: (0, i)), pl.BlockSpec((H, 1), lambda i: (0, 0))],
        out_specs=pl.BlockSpec((H, tb), lambda i: (0, i)),
        out_shape=jax.ShapeDtypeStruct((H, T), F32),
        scratch_shapes=[pltpu.VMEM((H, 1), F32)],
        compiler_params=_cparams(("arbitrary",)),
    )(fl, b)


def _gate_cumsum_bwd(name, d_query, d_key, fl, b, tb):
    H, T = fl.shape
    nb = T // tb

    def body(dq_ref, dk_ref, f_ref, b_ref, dfl_ref, db_ref, carry):
        @pl.when(pl.program_id(0) == 0)
        def _():
            carry[...] = jnp.zeros_like(carry)
            db_ref[...] = jnp.zeros_like(db_ref)

        d = dq_ref[...] - dk_ref[...]
        src = lax.broadcasted_iota(jnp.int32, (tb, tb), 0)
        dst = lax.broadcasted_iota(jnp.int32, (tb, tb), 1)
        tri = (src >= dst).astype(F32)
        dls = lax.dot_general(d, tri, NN, precision=lax.Precision.HIGHEST,
                              preferred_element_type=F32) + carry[...]
        z = f_ref[...] + b_ref[...]
        dfl = dls * (1.0 / (1.0 + jnp.exp(z)))
        dfl_ref[...] = dfl
        db_ref[...] += jnp.sum(dfl, axis=-1, keepdims=True)
        carry[...] = carry[...] + jnp.sum(d, axis=-1, keepdims=True)

    blk = pl.BlockSpec((H, tb), lambda i: (0, nb - 1 - i))
    vec = pl.BlockSpec((H, 1), lambda i: (0, 0))
    return pl.pallas_call(
        body, name=name, grid=(nb,),
        in_specs=[blk, blk, blk, vec], out_specs=[blk, vec],
        out_shape=[jax.ShapeDtypeStruct((H, T), F32), jax.ShapeDtypeStruct((H, 1), F32)],
        scratch_shapes=[pltpu.VMEM((H, 1), F32)],
        compiler_params=_cparams(("arbitrary",)),
    )(d_query, d_key, fl, b)


def _chunk_rows(j, tq):
    return pl.ds(pl.multiple_of(j * tq, tq), tq)


def _column_as_row(col):
    return jnp.broadcast_to(col, (col.shape[0], 128)).T[:1, :]


def _flash_fwd(name, q, k, v_aug, dv, tq, exchange=None, q_head0=0, v_head0=0):
    H, T, dqk = k.shape
    dva = v_aug.shape[2]
    tk = tq
    tq = next(m * tk for m in (4, 2, 1) if T % (m * tk) == 0)
    r = tq // tk
    nq = T // tq

    def body(q_ref, k_ref, v_ref, o_ref, lse_ref, m_sc, acc_sc):
        qi = pl.program_id(1)
        m_sc[...] = jnp.full_like(m_sc, NEG)
        acc_sc[...] = jnp.zeros_like(acc_sc)

        def chunk(j, diag):
            rows = _chunk_rows(j, tk)
            live = slice(0 if diag is None else diag * tk, tq)
            n_live = tq - live.start
            s = _dot(q_ref[live, :], k_ref[rows, :], NT)
            if diag is not None:
                row = lax.broadcasted_iota(jnp.int32, (n_live, tk), 0)
                col = lax.broadcasted_iota(jnp.int32, (n_live, tk), 1)
                s = jnp.where(col <= row, s, NEG)
            m_prev = m_sc[live, :]
            m_new = jnp.maximum(m_prev, jnp.max(s, axis=1, keepdims=True))
            p = jnp.exp(s - jnp.tile(m_new, (1, tk // 128)))
            alpha = jnp.tile(jnp.exp(m_prev - m_new), (1, dva // 128))
            acc_sc[live, :] = alpha * acc_sc[live, :] + _dot(p.astype(BF16), v_ref[rows, :], NN)
            m_sc[live, :] = m_new

        def off_diagonal(j, carry):
            chunk(j, None)
            return carry

        lax.fori_loop(0, qi * r, off_diagonal, 0)
        for d in range(r):
            chunk(qi * r + d, d)
        acc = acc_sc[...]
        l = acc[:, dv:dv + 1]
        o_ref[...] = acc[:, :dv] / l
        lse_ref[...] = _column_as_row(m_sc[:, :1] + jnp.log(l))

    (o, lse_rows), exchanged = _call_carrying(
        body, name, (H, nq),
        in_specs=[pl.BlockSpec((None, tq, dqk), lambda h, i: (h + q_head0, i, 0)),
                  pl.BlockSpec((None, T, dqk), lambda h, i: (h, 0, 0)),
                  pl.BlockSpec((None, T, dva), lambda h, i: (h + v_head0, 0, 0))],
        out_specs=[pl.BlockSpec((None, tq, dv), lambda h, i: (h, i, 0)),
                   pl.BlockSpec((None, 1, tq), lambda h, i: (h, 0, i))],
        out_shape=[jax.ShapeDtypeStruct((H, T, dv), F32), jax.ShapeDtypeStruct((H, 1, T), F32)],
        scratch_shapes=[pltpu.VMEM((tq, 128), F32), pltpu.VMEM((tq, dva), F32)],
        operands=(q, k, v_aug), exchange=exchange)
    return (o, lse_rows.reshape(H, T)), exchanged


def _row_dot(name, a, b):
    H, T, d = a.shape
    tm = _pick(T, (1024, 512, 256, 128))

    def body(a_ref, b_ref, o_ref):
        col = jnp.sum(a_ref[...].astype(F32) * b_ref[...].astype(F32), axis=-1, keepdims=True)
        o_ref[...] = _column_as_row(col)

    slab = pl.BlockSpec((None, tm, d), lambda h, i: (h, i, 0))
    return pl.pallas_call(
        body, name=name, grid=(H, T // tm), in_specs=[slab, slab],
        out_specs=pl.BlockSpec((None, 1, tm), lambda h, i: (h, 0, i)),
        out_shape=jax.ShapeDtypeStruct((H, 1, T), F32),
        compiler_params=_cparams(("parallel", "parallel")),
    )(a, b).reshape(H, T)


def _flash_bwd(name, q, k, v, do, scale, tq, exchange=None, v_head0=0, token_major_out=False):
    H, T, dqk = q.shape
    dva = v.shape[2]
    tkb = next(m * tq for m in (4, 2, 1)
               if T % (m * tq) == 0 and (m == 1 or m * tq * max(dqk, dva) <= MAX_KEY_BLOCK_ELEMS))
    r = tkb // tq
    nk = T // tkb

    def body(q_ref, k_ref, v_ref, do_ref, dq_ref, dk_ref, dv_ref, dk_sc, dv_sc):
        kj = pl.program_id(1)
        dk_sc[...] = jnp.zeros_like(dk_sc)
        dv_sc[...] = jnp.zeros_like(dv_sc)

        @pl.when(kj == 0)
        def _():
            dq_ref[...] = jnp.zeros_like(dq_ref)

        def chunk(i, diag):
            rows = _chunk_rows(i, tq)
            qb = q_ref[rows, :]
            dob = do_ref[rows, :]
            live = slice(0, tkb if diag is None else (diag + 1) * tq)
            kb = k_ref[live, :]
            st = _dot(kb, qb, NT)
            if diag is not None:
                key = lax.broadcasted_iota(jnp.int32, st.shape, 0)
                qry = lax.broadcasted_iota(jnp.int32, st.shape, 1) + diag * tq
                st = jnp.where(key <= qry, st, NEG)
            pt = jnp.exp(st)
            dv_sc[live, :] += _dot(pt.astype(BF16), dob, NN)
            dst = (pt * _dot(v_ref[live, :], dob, NT)).astype(BF16)
            dk_sc[live, :] += _dot(dst, qb, NN)
            dq_ref[rows, :] += _dot(dst, kb, TN)

        def off_diagonal(i, carry):
            chunk(i, None)
            return carry

        for d in range(r):
            chunk(kj * r + d, d)
        lax.fori_loop(kj * r + r, T // tq, off_diagonal, 0)
        dk_ref[...] = dk_sc[...]
        dv_ref[...] = dv_sc[...]

        @pl.when(kj == nk - 1)
        def _():
            dq_ref[...] = dq_ref[...] * scale

    whole_q = pl.BlockSpec((None, T, dqk), lambda h, j: (h, 0, 0))
    k_spec = pl.BlockSpec((None, tkb, dqk), lambda h, j: (h, j, 0))
    v_spec = pl.BlockSpec((None, tkb, dva), lambda h, j: (h, j, 0))
    v_in_spec = pl.BlockSpec((None, tkb, dva), lambda h, j: (h + v_head0, j, 0))
    if token_major_out:
        out_specs = [pl.BlockSpec((T, dqk), lambda h, j: (0, h)), pl.BlockSpec((tkb, dqk), lambda h, j: (j, h)),
                     pl.BlockSpec((tkb, dva), lambda h, j: (j, h))]
        out_shape = [jax.ShapeDtypeStruct((T, H * dqk), F32), jax.ShapeDtypeStruct((T, H * dqk), F32),
                     jax.ShapeDtypeStruct((T, H * dva), F32)]
    else:
        out_specs = [whole_q, k_spec, v_spec]
        out_shape = [jax.ShapeDtypeStruct((H, T, dqk), F32), jax.ShapeDtypeStruct((H, T, dqk), F32),
                     jax.ShapeDtypeStruct((H, T, dva), F32)]
    return _call_carrying(
        body, name, (H, nk),
        in_specs=[whole_q, k_spec, v_in_spec, pl.BlockSpec((None, T, dva), lambda h, j: (h, 0, 0))],
        out_specs=out_specs, out_shape=out_shape,
        scratch_shapes=[pltpu.VMEM((tkb, dqk), F32), pltpu.VMEM((tkb, dva), F32)],
        operands=(q, k, v, do), exchange=exchange)


def _pack_head_grads(name, parts, W, head_dim, picks):
    T, HW = parts[0].shape
    H = HW // W
    n = len(parts)
    tm = _pick(T, (512, 256, 128))
    sels = [(jnp.arange(HW)[:, None] == (jnp.arange(128)[None, :] * W + col)).astype(F32) for _, col in picks]

    def body(*refs):
        xs = refs[:n]
        sel_refs = refs[n:n + len(picks)]
        packed_ref = refs[n + len(picks)]
        row_refs = refs[n + len(picks) + 1:]
        for a in range(n):
            x = xs[a][...]
            heads = [x[:, h * W:h * W + head_dim] for h in range(H)]
            packed_ref[:, a * H * head_dim:(a + 1) * H * head_dim] = (
                jnp.concatenate(heads, axis=-1).astype(packed_ref.dtype))
        for (a, _), s_ref, r_ref in zip(picks, sel_refs, row_refs):
            cols = lax.dot_general(xs[a][...], s_ref[...], NN, precision=lax.Precision.HIGHEST,
                                   preferred_element_type=F32)
            r_ref[...] = cols.T[:H, :]

    row = pl.BlockSpec((tm, HW), lambda i: (i, 0))
    out = pl.pallas_call(
        body, name=name, grid=(T // tm,),
        in_specs=[row] * n + [pl.BlockSpec((HW, 128), lambda i: (0, 0))] * len(picks),
        out_specs=[pl.BlockSpec((tm, n * H * head_dim), lambda i: (i, 0))]
        + [pl.BlockSpec((H, tm), lambda i: (0, i))] * len(picks),
        out_shape=[jax.ShapeDtypeStruct((T, n * H * head_dim), BF16)]
        + [jax.ShapeDtypeStruct((H, T), F32)] * len(picks),
        compiler_params=_cparams(("parallel",)),
    )(*parts, *sels)
    return out[0], out[1:]


def _adamw_math(w, g, m, v):
    m = ADAM_B1 * m + (1.0 - ADAM_B1) * g
    v = ADAM_B2 * v + (1.0 - ADAM_B2) * (g * g)
    m_hat = m / (1.0 - ADAM_B1 ** ADAM_STEP)
    v_hat = v / (1.0 - ADAM_B2 ** ADAM_STEP)
    delta = -ADAM_LR * (m_hat / (jnp.sqrt(v_hat) + ADAM_EPS) + ADAM_WD * w)
    return delta, m, v


def _adamw(name, parts, w, m, v):
    P, R, C = parts.shape
    tr = _pick(R, (256, 128, 64, 32, 16, 8))

    def body(p_ref, w_ref, m_ref, v_ref, g_out, d_out, m_out, v_out):
        g = p_ref[0].astype(F32)
        for i in range(1, P):
            g = g + p_ref[i].astype(F32)
        delta, m_new, v_new = _adamw_math(w_ref[...], g, m_ref[...], v_ref[...])
        g_out[...] = g
        d_out[...] = delta
        m_out[...] = m_new
        v_out[...] = v_new

    blk = pl.BlockSpec((tr, C), lambda i: (i, 0))
    sds = jax.ShapeDtypeStruct((R, C), F32)
    return pl.pallas_call(
        body, name=name, grid=(R // tr,),
        in_specs=[pl.BlockSpec((P, tr, C), lambda i: (0, i, 0)), blk, blk, blk],
        out_specs=[blk] * 4, out_shape=[sds] * 4,
        compiler_params=_cparams(("parallel",)),
    )(parts, w, m, v)


def _my_position():
    return lax.axis_index("x"), lax.axis_index("y"), lax.axis_index("c")


def _slot(p):
    return 4 * p[0] + 2 * p[1] + p[2]


def _flip(p, k):
    return tuple((1 - p[i]) if (k >> (2 - i)) & 1 else p[i] for i in range(3))


def _allgather_weights(shards):
    n = len(shards)

    def body(*refs):
        ins = refs[:n]
        outs = refs[n:2 * n]
        send_sems, recv_sems, local_sems = refs[2 * n:]
        x, y, c = _my_position()
        me, sibling = (x, y, c), (x, y, 1 - c)
        chips = [(1 - x, y), (x, 1 - y), (1 - x, 1 - y)]

        def copy(a, k, block, to, src=None):
            dst = outs[a].at[_slot(block)]
            return pltpu.make_async_remote_copy(
                src_ref=dst if src is None else src, dst_ref=dst,
                send_sem=send_sems.at[7 * a + k], recv_sem=recv_sems.at[7 * a + k],
                device_id=to, device_id_type=MESH)

        started = []
        for a in range(n):
            mine = pltpu.make_async_copy(ins[a], outs[a].at[_slot(me)], local_sems.at[a])
            mine.start()
            started.append(mine)
        first = []
        for a in range(n):
            first.append(copy(a, 0, me, sibling, src=ins[a]))
            first += [copy(a, 1 + j, me, (*chip, c), src=ins[a]) for j, chip in enumerate(chips)]
        for cp in first:
            cp.start()
        passed = []
        for j, chip in enumerate(chips):
            for a in range(n):
                copy(a, 1 + j, (*chip, c), me).wait_recv()
                fwd = copy(a, 4 + j, (*chip, c), sibling)
                fwd.start()
                passed.append(fwd)
        for a in range(n):
            copy(a, 0, sibling, me).wait_recv()
            for j, chip in enumerate(chips):
                copy(a, 4 + j, (*chip, 1 - c), me).wait_recv()
        for cp in first + passed:
            cp.wait_send()
        for mine in started:
            mine.wait()

    hbm = pl.BlockSpec(memory_space=pl.ANY)
    return pl.pallas_call(
        body, name="allgather_weights",
        in_specs=[hbm] * n, out_specs=[hbm] * n,
        out_shape=[jax.ShapeDtypeStruct((N_DEV,) + s.shape, s.dtype) for s in shards],
        scratch_shapes=[pltpu.SemaphoreType.DMA((7 * n,)), pltpu.SemaphoreType.DMA((7 * n,)),
                        pltpu.SemaphoreType.DMA((n,))],
        compiler_params=pltpu.CompilerParams(has_side_effects=True),
    )(*shards)


def _exchange_copies(kind, x_in, x_out, send_sems, recv_sems, local_sems, receives=True):
    me = _my_position()
    mine = _slot(me)
    local, sends, recvs = [], [], []
    for a in range(len(x_in)):
        src = x_in[a] if kind == "gather" else x_in[a].at[mine]
        local.append(pltpu.make_async_copy(src, x_out[a].at[mine], local_sems.at[a]))
    for k in range(1, N_DEV):
        peer = _flip(me, k)
        theirs = _slot(peer)
        for a in range(len(x_in)):
            src = x_in[a] if kind == "gather" else x_in[a].at[theirs]
            ends = [(x_out[a].at[mine], sends)] + ([(x_out[a].at[theirs], recvs)] if receives else [])
            for dst, group in ends:
                group.append(pltpu.make_async_remote_copy(
                    src_ref=src, dst_ref=dst, send_sem=send_sems.at[7 * a + k - 1],
                    recv_sem=recv_sems.at[7 * a + k - 1], device_id=peer, device_id_type=MESH))
    return local, sends, recvs


def _exchange_out_shapes(kind, arrays):
    return [jax.ShapeDtypeStruct(((N_DEV,) + a.shape) if kind == "gather" else a.shape, a.dtype)
            for a in arrays]


def _exchange_sems(n):
    return [pltpu.SemaphoreType.DMA((7 * n,)), pltpu.SemaphoreType.DMA((7 * n,)),
            pltpu.SemaphoreType.DMA((n,))]


def _call_carrying(body, name, grid, in_specs, out_specs, out_shape, scratch_shapes, operands, exchange,
                   sequential=False):
    if exchange is None:
        out = pl.pallas_call(
            body, name=name, grid=grid, in_specs=in_specs, out_specs=out_specs, out_shape=out_shape,
            scratch_shapes=scratch_shapes,
            compiler_params=_cparams((("arbitrary",) if sequential else ("parallel",))
                                     + ("arbitrary",) * (len(grid) - 1)),
        )(*operands)
        return out, None
    kind, arrays = exchange
    n, n_in, n_out, n_sc = len(arrays), len(in_specs), len(out_specs), len(scratch_shapes)

    def full_body(*refs):
        ins, refs = refs[:n_in], refs[n_in:]
        x_in, refs = refs[:n], refs[n:]
        outs, refs = refs[:n_out], refs[n_out:]
        x_out, refs = refs[:n], refs[n:]
        scratch, sems = refs[:n_sc], refs[n_sc:]
        first = last = None
        for axis, size in enumerate(grid):
            at_start = pl.program_id(axis) == 0
            at_end = pl.program_id(axis) == size - 1
            first = at_start if first is None else jnp.logical_and(first, at_start)
            last = at_end if last is None else jnp.logical_and(last, at_end)

        @pl.when(first)
        def _():
            local, sends, _ = _exchange_copies(kind, x_in, x_out, *sems, receives=False)
            for cp in local + sends:
                cp.start()

        body(*ins, *outs, *scratch)

        @pl.when(last)
        def _():
            local, sends, recvs = _exchange_copies(kind, x_in, x_out, *sems)
            for cp in recvs:
                cp.wait_recv()
            for cp in sends:
                cp.wait_send()
            for cp in local:
                cp.wait()

    hbm = pl.BlockSpec(memory_space=pl.ANY)
    out = pl.pallas_call(
        full_body, name=name, grid=grid,
        in_specs=list(in_specs) + [hbm] * n, out_specs=list(out_specs) + [hbm] * n,
        out_shape=list(out_shape) + _exchange_out_shapes(kind, arrays),
        scratch_shapes=list(scratch_shapes) + _exchange_sems(n),
        compiler_params=pltpu.CompilerParams(dimension_semantics=("arbitrary",) * len(grid),
                                             vmem_limit_bytes=VMEM_LIMIT_BYTES, has_side_effects=True),
    )(*operands, *arrays)
    return out[:n_out], out[n_out:]


def _allreduce_small(v):
    R, C = v.shape

    def body(v_ref, o_ref, buf, send_sems, recv_sems):
        me = _my_position()
        buf[_slot(me)] = v_ref[...]
        sends = []
        for k in range(1, N_DEV):
            peer = _flip(me, k)
            cp = pltpu.make_async_remote_copy(
                src_ref=v_ref, dst_ref=buf.at[_slot(me)],
                send_sem=send_sems.at[k - 1], recv_sem=recv_sems.at[k - 1],
                device_id=peer, device_id_type=MESH)
            cp.start()
            sends.append(cp)
        for k in range(1, N_DEV):
            peer = _flip(me, k)
            pltpu.make_async_remote_copy(
                src_ref=v_ref, dst_ref=buf.at[_slot(peer)],
                send_sem=send_sems.at[k - 1], recv_sem=recv_sems.at[k - 1],
                device_id=peer, device_id_type=MESH).wait_recv()
        for cp in sends:
            cp.wait_send()
        tot = buf[0]
        for s in range(1, N_DEV):
            tot = tot + buf[s]
        o_ref[...] = tot

    vm = pl.BlockSpec(memory_space=pltpu.VMEM)
    return pl.pallas_call(
        body, name="allreduce_small",
        in_specs=[vm], out_specs=vm, out_shape=jax.ShapeDtypeStruct((R, C), F32),
        scratch_shapes=[pltpu.VMEM((N_DEV, R, C), F32), pltpu.SemaphoreType.DMA((7,)),
                        pltpu.SemaphoreType.DMA((7,))],
        compiler_params=pltpu.CompilerParams(has_side_effects=True),
    )(v)


def _to_heads(t, heads):
    T = t.shape[0]
    return t.reshape(T, heads, t.shape[1] // heads).transpose(1, 0, 2)


def _from_heads(t):
    H, T, d = t.shape
    return t.transpose(1, 0, 2).reshape(T, H * d)


def _widen(t, width, ones_at=None, pieces_at=None, pieces=None):
    out = jnp.pad(t, ((0, 0), (0, 0), (0, width - t.shape[-1])))
    lane = lax.broadcasted_iota(jnp.int32, (1, 1, width), 2)
    if ones_at is not None:
        out = jnp.where((lane >= ones_at) & (lane < ones_at + 3), jnp.ones((), BF16), out)
    if pieces_at is not None:
        for i in range(3):
            out = jnp.where(lane == pieces_at + i, pieces[i][:, :, None], out)
    return out


def _split3(t):
    hi = lax.reduce_precision(t, 8, 7)
    r = t - hi
    mid = lax.reduce_precision(r, 8, 7)
    lo = lax.reduce_precision(r - mid, 8, 7)
    return hi.astype(BF16), mid.astype(BF16), lo.astype(BF16)


def _pad_cols(t, n):
    return jnp.pad(t, ((0, 0), (0, n - t.shape[1])))


def _pack_small(mix, ffn, kv, fin, kva, qa, bf, last):
    row6 = jnp.concatenate([kva.reshape(-1), qa.reshape(-1), bf.reshape(-1),
                            jnp.zeros((D_MODEL - KV_LORA - Q_LORA - FOX_HEADS,), F32)])
    return jnp.stack([mix[0], mix[1], ffn[0], ffn[1], kv.reshape(-1), fin.reshape(-1), row6, last])


def _unpack_small(p):
    mix = p[0:2]
    ffn = p[2:4]
    kv = p[4]
    fin = p[5]
    kva = p[6, :KV_LORA]
    qa = p[6, KV_LORA:KV_LORA + Q_LORA].reshape(1, Q_LORA)
    bf = p[6, KV_LORA + Q_LORA:KV_LORA + Q_LORA + FOX_HEADS].reshape(1, FOX_HEADS)
    return mix, ffn, bf, kv, kva, qa, fin


def _mlp_fwd(tag, xin, g, w_up, w_down):
    h = _rms(f"{tag}_norm", xin, g, BF16)

    def act(acc):
        r = jnp.maximum(acc, 0.0)
        return acc, r * r

    u, a = _mm(f"{tag}_up", h, w_up, "nn", (BF16, BF16), epi=act)
    xout = _mm(f"{tag}_down", a, w_down, "nn", (F32,), epi=lambda acc, r: (acc + r,), extras=(xin,))
    return xout, (h, u, a)


def _mlp_bwd(tag, gout, gout16, xin, g, w_up, w_down, saved):
    h, u, a = saved
    dw_down = _mm_tn(f"{tag}_dwdown", a, gout16)
    du = _mm(f"{tag}_du", gout16, w_down, "nt", (BF16,),
             epi=lambda acc, uu: (acc * (2.0 * jnp.maximum(uu.astype(F32), 0.0)),), extras=(u,))
    dw_up = _mm_tn(f"{tag}_dwup", h, du)
    gin, dg, gin16 = _mm_rms_bwd(f"{tag}_dh_norm_bwd", du, w_up, xin, g, gout, emit_bf16=True)
    return gin, gin16, dg, dw_up, dw_down


def kernel(x, norm_mix_g, norm_ffn_g, fox_w_in, fox_b_f, fox_w_out, kv_norm_g, mla_w_kv_a, mla_kv_a_norm_g, mla_w_kv_b, mla_w_q_a, mla_q_a_norm_g, mla_w_q_b, mla_w_out, ffn_w_up, ffn_w_down, final_norm_g, loss_target, m_norm_mix_g, m_norm_ffn_g, m_fox_w_in, m_fox_b_f, m_fox_w_out, m_kv_norm_g, m_mla_w_kv_a, m_mla_kv_a_norm_g, m_mla_w_kv_b, m_mla_w_q_a, m_mla_q_a_norm_g, m_mla_w_q_b, m_mla_w_out, m_ffn_w_up, m_ffn_w_down, m_final_norm_g, v_norm_mix_g, v_norm_ffn_g, v_fox_w_in, v_fox_b_f, v_fox_w_out, v_kv_norm_g, v_mla_w_kv_a, v_mla_kv_a_norm_g, v_mla_w_kv_b, v_mla_w_q_a, v_mla_q_a_norm_g, v_mla_w_q_b, v_mla_w_out, v_ffn_w_up, v_ffn_w_down, v_final_norm_g):
    T = x.shape[1]
    D = D_MODEL
    tq = 512 if T >= 2048 else 128
    x0 = x[0]
    tgt = loss_target[0]

    gat_fox = _allgather_weights([fox_w_in[0].astype(BF16), fox_w_out[0].astype(BF16)])
    later_shards = [s.astype(BF16) for s in (mla_w_kv_a, mla_w_kv_b, mla_w_q_a[0], mla_w_q_b[0],
                                             mla_w_out[0], ffn_w_up, ffn_w_down)]
    w_in = gat_fox[0].transpose(1, 0, 2).reshape(D, 3 * D + FOX_HEADS)
    w_qkv = w_in[:, :3 * D]
    w_f = _pad_cols(w_in[:, 3 * D:], 128)
    w_fo = gat_fox[1].reshape(D, D)
    g_mix0, g_mix1 = norm_mix_g[0:1], norm_mix_g[1:2]
    g_ffn0, g_ffn1 = norm_ffn_g[0:1], norm_ffn_g[1:2]
    g_kv = kv_norm_g.reshape(1, D)
    g_kva = mla_kv_a_norm_g.reshape(1, KV_LORA)
    g_qa = mla_q_a_norm_g.reshape(1, Q_LORA)
    g_fin = final_norm_g.reshape(1, D)

    inv = 1.0 / (ROPE_BASE ** (jnp.arange(0, QK_ROPE, 2, dtype=F32) / QK_ROPE))
    ang = jnp.arange(T, dtype=F32)[:, None] * inv[None, :]
    cos, sin = jnp.cos(ang), jnp.sin(ang)
    cos2 = jnp.concatenate([cos, cos], axis=-1)
    sgn_sin = jnp.concatenate([-sin, sin], axis=-1)

    h0 = _rms("l0_mix_norm", x0, g_mix0, BF16)
    fl_pad = _mm("fox_gate_logit", h0, w_f, "nn", (F32,))
    fl = fl_pad[:, :FOX_HEADS].T
    b_f = fox_b_f.reshape(FOX_HEADS, 1)
    cgate = _gate_cumsum("fox_gate_scan", fl, b_f, tq)
    fox_scale = FOX_HEAD_DIM ** -0.5
    col_scale = jnp.where(jnp.arange(3 * D) < D, fox_scale, 1.0).astype(BF16)
    tail = jnp.arange(FOX_AUG - FOX_HEAD_DIM)
    ones_q = (tail < 3).astype(F32)
    consts_k = ((tail >= 4) & (tail < 7)).astype(F32) + (tail == 3).astype(F32) * (1.0 / fox_scale)
    tails = jnp.broadcast_to(jnp.stack([ones_q, consts_k, ones_q])[:, None, None, :],
                             (3, FOX_HEADS, 1, FOX_AUG - FOX_HEAD_DIM)).reshape(3 * FOX_HEADS, 1, -1)
    qkv_h = _mm_head_slabs("fox_qkv", h0, w_qkv * col_scale, tails, FOX_HEAD_DIM, BF16)
    fk_aug = _widen(qkv_h[FOX_HEADS:2 * FOX_HEADS], FOX_AUG, pieces_at=FOX_HEAD_DIM,
                    pieces=_split3(-cgate))
    (fo, flse), gat = _flash_fwd("fox_attn", qkv_h, fk_aug, qkv_h, FOX_HEAD_DIM, tq,
                                 exchange=("gather", later_shards), q_head0=0, v_head0=2 * FOX_HEADS)
    w_kva = _pad_cols(gat[0].reshape(D, KV_LORA + QK_ROPE), KV_A_PAD)
    w_kvb_h = gat[1]
    w_qa = gat[2].reshape(D, Q_LORA)
    w_qb_h = gat[3]
    w_mo = gat[4].reshape(D, D)
    w_up = gat[5].transpose(1, 2, 0, 3).reshape(2, D, D_FF)
    w_down = gat[6].transpose(1, 0, 2, 3).reshape(2, D_FF, D)
    fctx = _from_heads(fo).astype(BF16)
    x1 = _mm("fox_out", fctx, w_fo, "nn", (F32,), epi=lambda acc, r: (acc + r,), extras=(x0,))
    x2, mlp0 = _mlp_fwd("l0_ffn", x1, g_ffn0, w_up[0], w_down[0])

    src = _rms("kv_norm", x2, g_kv, BF16)
    kva = _mm("kv_a", src, w_kva, "nn", (F32,))
    kva_lat = kva[:, :KV_LORA]
    c_kv = _rms("kv_a_norm", kva_lat, g_kva, BF16)
    k_rope = _rope("k_rope", kva[:, KV_LORA:KV_LORA + QK_ROPE][None], cos2, sgn_sin, BF16)
    kvb_h = _mm_heads("kv_b", c_kv, w_kvb_h, BF16)
    mk = jnp.concatenate([kvb_h[:, :, :QK_NOPE],
                          jnp.broadcast_to(k_rope, (MLA_HEADS, T, QK_ROPE))], axis=-1)
    mv = kvb_h[:, :, QK_NOPE:]

    h1 = _rms("l1_mix_norm", x2, g_mix1, BF16)
    qa = _mm("q_a", h1, w_qa, "nn", (F32,))
    c_q = _rms("q_a_norm", qa, g_qa, BF16)
    mla_scale = (QK_NOPE + QK_ROPE) ** -0.5
    mq = _mla_q_proj("q_b", c_q, w_qb_h, cos2, sgn_sin, mla_scale)
    mv_aug = _widen(mv, MLA_AUG, ones_at=V_HEAD)
    (mo, mlse), _ = _flash_fwd("mla_attn", mq, mk, mv_aug, V_HEAD, tq)
    mctx = _from_heads(mo).astype(BF16)
    x3 = _mm("mla_out", mctx, w_mo, "nn", (F32,), epi=lambda acc, r: (acc + r,), extras=(x2,))
    x4, mlp1 = _mlp_fwd("l1_ffn", x3, g_ffn1, w_up[1], w_down[1])

    g4, dg_fin, loss_vec, g4h = _loss_head("loss_head", x4, g_fin, tgt)

    g3, g3h, dg_ffn1, dw_up1, dw_down1 = _mlp_bwd("l1_ffn", g4, g4h, x3, g_ffn1, w_up[1], w_down[1], mlp1)

    dw_mo = _mm_tn("mla_out_dw", mctx, g3h)
    dmo = _to_heads(_mm("mla_out_dx", g3h, w_mo, "nt", (BF16,)), MLA_HEADS)
    mdelta = _row_dot("mla_delta", mo, dmo)
    dqk = QK_NOPE + QK_ROPE
    mq_bwd = _widen(mq, MLA_AUG, pieces_at=dqk, pieces=_split3(-mlse))
    mk_bwd = _widen(mk, MLA_AUG, ones_at=dqk)
    mdo_aug = _widen(dmo, MLA_AUG, pieces_at=V_HEAD, pieces=_split3(-mdelta))
    (mdq, mdk, mdv), _ = _flash_bwd("mla_attn_bwd", mq_bwd, mk_bwd, mv_aug, mdo_aug, mla_scale, tq)
    mdq = mdq[:, :, :dqk]
    mdk = mdk[:, :, :dqk]
    mdv = mdv[:, :, :V_HEAD]
    dq_rope = _rope_bwd("q_rope_bwd", mdq[:, :, QK_NOPE:], cos2, sgn_sin, False)
    dqf_h = jnp.concatenate([mdq[:, :, :QK_NOPE], dq_rope], axis=-1)
    dw_qb_h = _mm_heads_dw("q_b_dw", c_q, dqf_h)
    dc_q = _mm_heads_dx("q_b_dx", dqf_h, w_qb_h)
    dqa, dg_qa = _rms_bwd("q_a_norm_bwd", qa, g_qa, dc_q)
    dw_qa = _mm_tn("q_a_dw", h1, dqa)
    g2a, dg_mix1 = _mm_rms_bwd("q_a_dx_norm_bwd", dqa, w_qa, x2, g_mix1, g3)

    dk_rope = _rope_bwd("k_rope_bwd", mdk[:, :, QK_NOPE:], cos2, sgn_sin, True)
    dkvb_h = jnp.concatenate([mdk[:, :, :QK_NOPE], mdv], axis=-1)
    dw_kvb_h = _mm_heads_dw("kv_b_dw", c_kv, dkvb_h)
    dc_kv = _mm_heads_dx("kv_b_dx", dkvb_h, w_kvb_h)
    dkva_lat, dg_kva = _rms_bwd("kv_a_norm_bwd", kva_lat, g_kva, dc_kv)
    dkva = _pad_cols(jnp.concatenate([dkva_lat, dk_rope], axis=-1), KV_A_PAD)
    dw_kva = _mm_tn("kv_a_dw", src, dkva)[:, :KV_LORA + QK_ROPE]
    g2, dg_kv, g2h = _mm_rms_bwd("kv_a_dx_norm_bwd", dkva, w_kva, x2, g_kv, g2a, emit_bf16=True)

    g1, g1h, dg_ffn0, dw_up0, dw_down0 = _mlp_bwd("l0_ffn", g2, g2h, x1, g_ffn0, w_up[0], w_down[0], mlp0)

    dw_fo = _mm_tn("fox_out_dw", fctx, g1h)
    dfo = _to_heads(_mm("fox_out_dx", g1h, w_fo, "nt", (BF16,)), FOX_HEADS)
    fdelta = _row_dot("fox_delta", fo, dfo)
    fq_bwd = _widen(qkv_h[:FOX_HEADS], FOX_AUG, pieces_at=FOX_HEAD_DIM + 4, pieces=_split3(-flse))
    fdo_aug = _widen(dfo, FOX_AUG, pieces_at=FOX_HEAD_DIM, pieces=_split3(-fdelta))
    dw_up = jnp.stack([dw_up0, dw_up1])
    dw_down = jnp.stack([dw_down0, dw_down1])
    early = [
        dw_fo.reshape(N_DEV, D // N_DEV, D),
        dw_kva.reshape(N_DEV, D // N_DEV, KV_LORA + QK_ROPE),
        dw_kvb_h,
        dw_qa.reshape(N_DEV, D // N_DEV, Q_LORA),
        dw_qb_h,
        dw_mo.reshape(N_DEV, D // N_DEV, D),
        dw_up.reshape(2, D, N_DEV, -1).transpose(2, 0, 1, 3),
        dw_down.reshape(2, N_DEV, D_FF // N_DEV, D).transpose(1, 0, 2, 3),
    ]
    fd_aug, early_parts = _flash_bwd(
        "fox_attn_bwd", fq_bwd, fk_aug, qkv_h, fdo_aug, fox_scale, tq,
        exchange=("scatter", [g.astype(BF16) for g in early]), v_head0=2 * FOX_HEADS, token_major_out=True)
    dqkv, (ds_rows, ds_cols) = _pack_head_grads("fox_grad_pack", list(fd_aug), FOX_AUG, FOX_HEAD_DIM,
                                                picks=((0, FOX_HEAD_DIM + 3), (1, FOX_HEAD_DIM)))
    dfl, db_f = _gate_cumsum_bwd("fox_gate_scan_bwd", ds_rows, ds_cols, fl, b_f, tq)
    dfl_pad = _pad_cols(dfl.T, 128)
    dw_qkv = _mm_tn("fox_qkv_dw", h0, dqkv)
    dw_f = _mm_tn("fox_gate_dw", h0, dfl_pad)[:, :FOX_HEADS]
    dw_in = jnp.concatenate([dw_qkv, dw_f], axis=-1)
    dh0a = _mm("fox_gate_dx", dfl_pad, w_f, "nt", (F32,))
    late = dw_in.reshape(D, N_DEV, -1).transpose(1, 0, 2).astype(BF16)
    grad_x, dg_mix0, late_parts = _mm_rms_bwd("fox_qkv_dx_norm_bwd", dqkv, w_qkv, x0, g_mix0, g1, add=dh0a,
                                              exchange=("scatter", [late]))

    parts = list(late_parts) + list(early_parts)

    names = ["fox_w_in", "fox_w_out", "mla_w_kv_a", "mla_w_kv_b", "mla_w_q_a", "mla_w_q_b",
             "mla_w_out", "ffn_w_up", "ffn_w_down"]
    moms = [m_fox_w_in, m_fox_w_out, m_mla_w_kv_a, m_mla_w_kv_b, m_mla_w_q_a, m_mla_w_q_b,
            m_mla_w_out, m_ffn_w_up, m_ffn_w_down]
    vars_ = [v_fox_w_in, v_fox_w_out, v_mla_w_kv_a, v_mla_w_kv_b, v_mla_w_q_a, v_mla_w_q_b,
             v_mla_w_out, v_ffn_w_up, v_ffn_w_down]
    full = [fox_w_in, fox_w_out, mla_w_kv_a, mla_w_kv_b, mla_w_q_a, mla_w_q_b, mla_w_out,
            ffn_w_up, ffn_w_down]
    big = {}
    for nm, p, w, m, v in zip(names, parts, full, moms, vars_):
        C = w.shape[-1]
        res = _adamw(f"adamw_{nm}", p.reshape(N_DEV, -1, C), w.reshape(-1, C), m.reshape(-1, C),
                     v.reshape(-1, C))
        big[nm] = [r.reshape(w.shape) for r in res]

    zrow = jnp.zeros((D,), F32)
    g_small = _pack_small(jnp.concatenate([dg_mix0, dg_mix1]), jnp.concatenate([dg_ffn0, dg_ffn1]),
                          dg_kv, dg_fin, dg_kva, dg_qa, db_f, zrow.at[0].set(loss_vec[0, 0]))
    tot_small = _allreduce_small(g_small)
    w_small = _pack_small(norm_mix_g, norm_ffn_g, kv_norm_g, final_norm_g, mla_kv_a_norm_g,
                          mla_q_a_norm_g, fox_b_f, zrow)
    m_small = _pack_small(m_norm_mix_g, m_norm_ffn_g, m_kv_norm_g, m_final_norm_g, m_mla_kv_a_norm_g,
                          m_mla_q_a_norm_g, m_fox_b_f, zrow)
    v_small = _pack_small(v_norm_mix_g, v_norm_ffn_g, v_kv_norm_g, v_final_norm_g, v_mla_kv_a_norm_g,
                          v_mla_q_a_norm_g, v_fox_b_f, zrow)
    small = _adamw("adamw_small", tot_small[None], w_small, m_small, v_small)
    loss = tot_small[7, 0]
    small = [_unpack_small(s) for s in small]

    def ordered(i):
        mix, ffn, bf, kv, kva, qa, fin = small[i]
        return [mix, ffn, big["fox_w_in"][i], bf, big["fox_w_out"][i], kv, big["mla_w_kv_a"][i], kva,
                big["mla_w_kv_b"][i], big["mla_w_q_a"][i], qa, big["mla_w_q_b"][i],
                big["mla_w_out"][i], big["ffn_w_up"][i], big["ffn_w_down"][i], fin]

    return (loss, grad_x[None], *ordered(0), *ordered(1), *ordered(2), *ordered(3))
```
